```python
import jax, jax.numpy as jnp
from jax import lax
import numpy as np

D_MODEL = 1024
BATCH = 16
SEQ = 4096
DEPTH = 4

N_A_LAYERS = DEPTH // 2
CONV_WIDTH = 3
PATTERNS = ((128, 1), (512, 4), (2048, 16))
N_GROUPS = len(PATTERNS)
H_G = 8
HEAD_DIM = 64
D_FF = 4 * D_MODEL
EPS = 1e-5
ALIBI_MAX_BIAS = 8.0
NEG_INF = -1e30

kernel_name = 'yoco_shortconv_dilated_attn_trunk'


def rmsnorm(x, g):
    xf = x.astype(jnp.float32)
    y = xf * lax.rsqrt(jnp.mean(xf * xf, axis=-1, keepdims=True) + EPS)
    return (y * g.astype(jnp.float32)).astype(x.dtype)


def short_conv_mixer(h, w_in, conv_w, w_out):
    S = h.shape[1]
    b, c, u = jnp.split(h @ w_in, 3, axis=-1)
    up = jnp.pad(c * u, ((0, 0), (CONV_WIDTH - 1, 0), (0, 0)))
    conv = sum(conv_w[k] * up[:, CONV_WIDTH - 1 - k: CONV_WIDTH - 1 - k + S] for k in range(CONV_WIDTH))
    return (b * conv) @ w_out


def padded_len(S, window):
    return -(-S // window) * window


def to_blocks(t, dilation, blk, seq_pad):
    B, S = t.shape[:2]
    t = jnp.pad(t, [(0, 0), (0, seq_pad - S)] + [(0, 0)] * (t.ndim - 2))
    return t.reshape(B, seq_pad // (dilation * blk), blk, dilation, *t.shape[2:])


def kv_context(tb):
    prev = jnp.pad(tb, [(0, 0), (1, 0)] + [(0, 0)] * 4)[:, :-1]
    return jnp.concatenate([prev, tb], axis=2)


def dilated_branch(q, k_ctx, v_ctx, window, dilation, slopes):
    B, S, H, dh = q.shape
    blk = window // dilation
    seq_pad = padded_len(S, window)
    qb = to_blocks(q, dilation, blk, seq_pad)
    nb = qb.shape[1]
    s = jnp.einsum('bnqrhd,bnkrhd->bnrhqk', qb, k_ctx).astype(jnp.float32) * (dh ** -0.5)
    a = jnp.arange(blk)[:, None]
    c = jnp.arange(2 * blk)[None, :]
    j = blk + a - c
    n = jnp.arange(nb)[:, None, None]
    valid = (j >= 0) & (j <= blk) & ((n > 0) | (c >= blk))
    bias = -slopes[:, None, None] * (dilation * j).astype(jnp.float32)
    s = jnp.where(valid[None, :, None, None], s + bias, NEG_INF)
    lse = jax.nn.logsumexp(s, axis=-1)
    p = jnp.exp(s - lse[..., None]).astype(v_ctx.dtype)
    o = jnp.einsum('bnrhqk,bnkrhd->bnqrhd', p, v_ctx).reshape(B, seq_pad, H, dh)[:, :S]
    lse = jnp.transpose(lse, (0, 1, 4, 2, 3)).reshape(B, seq_pad, H)[:, :S]
    return o, lse


def _fwd_setup_inputs(seed: int = 0) -> dict:
    key = jax.random.key(seed)
    ks = jax.random.split(key, 14)
    n_a = N_A_LAYERS
    n_b = DEPTH - N_A_LAYERS
    qw = N_GROUPS * H_G * HEAD_DIM
    f32 = jnp.float32
    nrm = lambda k, shape: jax.random.normal(k, shape, f32)
    return {
        'x': nrm(ks[0], (BATCH, SEQ, D_MODEL)),
        'norm_mix': 1.0 + 0.05 * nrm(ks[1], (DEPTH, D_MODEL)),
        'norm_mlp': 1.0 + 0.05 * nrm(ks[2], (DEPTH, D_MODEL)),
        'w_a_in': nrm(ks[3], (n_a, D_MODEL, 3 * D_MODEL)) * D_MODEL ** -0.5,
        'conv_w': nrm(ks[4], (n_a, CONV_WIDTH, D_MODEL)) * CONV_WIDTH ** -0.5,
        'w_a_out': nrm(ks[5], (n_a, D_MODEL, D_MODEL)) * D_MODEL ** -0.5,
        'norm_kv': 1.0 + 0.05 * nrm(ks[6], (D_MODEL,)),
        'w_kv': nrm(ks[7], (D_MODEL, 2 * qw)) * D_MODEL ** -0.5,
        'w_q': nrm(ks[8], (n_b, D_MODEL, qw)) * D_MODEL ** -0.5,
        'w_o': nrm(ks[9], (n_b, H_G * HEAD_DIM, D_MODEL)) * (H_G * HEAD_DIM) ** -0.5,
        'w_up': nrm(ks[10], (DEPTH, D_MODEL, D_FF)) * D_MODEL ** -0.5,
        'w_down': nrm(ks[11], (DEPTH, D_FF, D_MODEL)) * (0.5 * D_FF ** -0.5),
        'norm_final': 1.0 + 0.05 * nrm(ks[12], (D_MODEL,)),
    }


def _fwd_reference(x, norm_mix, norm_mlp, w_a_in, conv_w, w_a_out, norm_kv, w_kv, w_q, w_o, w_up, w_down, norm_final):
    B, S, _ = x.shape
    slopes = 2.0 ** (-ALIBI_MAX_BIAS * jnp.arange(1, H_G + 1, dtype=jnp.float32) / H_G)
    h = x
    shared = []
    for l in range(DEPTH):
        if l < N_A_LAYERS:
            h = h + short_conv_mixer(rmsnorm(h, norm_mix[l]), w_a_in[l], conv_w[l], w_a_out[l])
        else:
            if l == N_A_LAYERS:
                kv = (rmsnorm(h, norm_kv) @ w_kv).reshape(B, S, N_GROUPS, 2, H_G, HEAD_DIM)
                for g, (window, dil) in enumerate(PATTERNS):
                    blk = window // dil
                    sp = padded_len(S, window)
                    shared.append((kv_context(to_blocks(kv[:, :, g, 0], dil, blk, sp)),
                                   kv_context(to_blocks(kv[:, :, g, 1], dil, blk, sp))))
            i = l - N_A_LAYERS
            q = (rmsnorm(h, norm_mix[l]) @ w_q[i]).reshape(B, S, N_GROUPS, H_G, HEAD_DIM)
            outs, lses = [], []
            for g, (window, dil) in enumerate(PATTERNS):
                o_g, lse_g = dilated_branch(q[:, :, g], shared[g][0], shared[g][1], window, dil, slopes)
                outs.append(o_g.astype(jnp.float32))
                lses.append(lse_g)
            wts = jax.nn.softmax(jnp.stack(lses), axis=0)
            o = jnp.sum(wts[..., None] * jnp.stack(outs), axis=0).astype(h.dtype)
            h = h + o.reshape(B, S, H_G * HEAD_DIM) @ w_o[i]
        hn = rmsnorm(h, norm_mlp[l])
        h = h + jnp.square(jax.nn.relu(hn @ w_up[l])) @ w_down[l]
    return rmsnorm(h, norm_final)


import jax as _jax
import jax.numpy as _jnp

TWIN_FORMAT = 'train_step'
FWD_PARAMS = ['x', 'norm_mix', 'norm_mlp', 'w_a_in', 'conv_w', 'w_a_out', 'norm_kv', 'w_kv', 'w_q', 'w_o', 'w_up', 'w_down', 'norm_final']
TWIN_WEIGHTS = ['norm_mix', 'norm_mlp', 'w_a_in', 'conv_w', 'w_a_out', 'norm_kv', 'w_kv', 'w_q', 'w_o', 'w_up', 'w_down', 'norm_final']
TWIN_DIFF_INPUT = 'x'
TWIN_INPUTS = ['x', 'norm_mix', 'norm_mlp', 'w_a_in', 'conv_w', 'w_a_out', 'norm_kv', 'w_kv', 'w_q', 'w_o', 'w_up', 'w_down', 'norm_final', 'loss_target', 'm_norm_mix', 'm_norm_mlp', 'm_w_a_in', 'm_conv_w', 'm_w_a_out', 'm_norm_kv', 'm_w_kv', 'm_w_q', 'm_w_o', 'm_w_up', 'm_w_down', 'm_norm_final', 'v_norm_mix', 'v_norm_mlp', 'v_w_a_in', 'v_conv_w', 'v_w_a_out', 'v_norm_kv', 'v_w_kv', 'v_w_q', 'v_w_o', 'v_w_up', 'v_w_down', 'v_norm_final']
TWIN_OUTPUTS = ['loss', 'grad_x', 'grad_norm_mix', 'grad_norm_mlp', 'grad_w_a_in', 'grad_conv_w', 'grad_w_a_out', 'grad_norm_kv', 'grad_w_kv', 'grad_w_q', 'grad_w_o', 'grad_w_up', 'grad_w_down', 'grad_norm_final', 'delta_norm_mix', 'delta_norm_mlp', 'delta_w_a_in', 'delta_conv_w', 'delta_w_a_out', 'delta_norm_kv', 'delta_w_kv', 'delta_w_q', 'delta_w_o', 'delta_w_up', 'delta_w_down', 'delta_norm_final', 'new_m_norm_mix', 'new_m_norm_mlp', 'new_m_w_a_in', 'new_m_conv_w', 'new_m_w_a_out', 'new_m_norm_kv', 'new_m_w_kv', 'new_m_w_q', 'new_m_w_o', 'new_m_w_up', 'new_m_w_down', 'new_m_norm_final', 'new_v_norm_mix', 'new_v_norm_mlp', 'new_v_w_a_in', 'new_v_conv_w', 'new_v_w_a_out', 'new_v_norm_kv', 'new_v_w_kv', 'new_v_w_q', 'new_v_w_o', 'new_v_w_up', 'new_v_w_down', 'new_v_norm_final']
TWIN_LEAF_KINDS = {'loss': 'loss', 'grad_x': 'grad_x', 'grad_norm_mix': 'grad_w', 'grad_norm_mlp': 'grad_w', 'grad_w_a_in': 'grad_w', 'grad_conv_w': 'grad_w', 'grad_w_a_out': 'grad_w', 'grad_norm_kv': 'grad_w', 'grad_w_kv': 'grad_w', 'grad_w_q': 'grad_w', 'grad_w_o': 'grad_w', 'grad_w_up': 'grad_w', 'grad_w_down': 'grad_w', 'grad_norm_final': 'grad_w', 'delta_norm_mix': 'delta_w', 'delta_norm_mlp': 'delta_w', 'delta_w_a_in': 'delta_w', 'delta_conv_w': 'delta_w', 'delta_w_a_out': 'delta_w', 'delta_norm_kv': 'delta_w', 'delta_w_kv': 'delta_w', 'delta_w_q': 'delta_w', 'delta_w_o': 'delta_w', 'delta_w_up': 'delta_w', 'delta_w_down': 'delta_w', 'delta_norm_final': 'delta_w', 'new_m_norm_mix': 'new_m', 'new_m_norm_mlp': 'new_m', 'new_m_w_a_in': 'new_m', 'new_m_conv_w': 'new_m', 'new_m_w_a_out': 'new_m', 'new_m_norm_kv': 'new_m', 'new_m_w_kv': 'new_m', 'new_m_w_q': 'new_m', 'new_m_w_o': 'new_m', 'new_m_w_up': 'new_m', 'new_m_w_down': 'new_m', 'new_m_norm_final': 'new_m', 'new_v_norm_mix': 'new_v', 'new_v_norm_mlp': 'new_v', 'new_v_w_a_in': 'new_v', 'new_v_conv_w': 'new_v', 'new_v_w_a_out': 'new_v', 'new_v_norm_kv': 'new_v', 'new_v_w_kv': 'new_v', 'new_v_w_q': 'new_v', 'new_v_w_o': 'new_v', 'new_v_w_up': 'new_v', 'new_v_w_down': 'new_v', 'new_v_norm_final': 'new_v'}


def _forward(args):
    return _fwd_reference(*[args[k] for k in FWD_PARAMS])


def _output_shape():
    out = _jax.eval_shape(lambda: _forward(_fwd_setup_inputs(0)))
    return out.shape, out.dtype

N_MICROBATCH = 1
ADAM_LR = 0.001
ADAM_B1 = 0.9
ADAM_B2 = 0.999
ADAM_EPS = 1e-08
ADAM_WD = 0.01
ADAM_STEP = 10
PER_EXAMPLE_BATCH_AXIS = {'x': 0, 'loss_target': 0}
SHARED_INPUTS = []
_WEIGHT_DTYPES = {'norm_mix': _jnp.float32, 'norm_mlp': _jnp.float32, 'w_a_in': _jnp.float32, 'conv_w': _jnp.float32, 'w_a_out': _jnp.float32, 'norm_kv': _jnp.float32, 'w_kv': _jnp.float32, 'w_q': _jnp.float32, 'w_o': _jnp.float32, 'w_up': _jnp.float32, 'w_down': _jnp.float32, 'norm_final': _jnp.float32}
MOMENT_SCALE = {'norm_mix': 2.542327e-01, 'norm_mlp': 1.118036e-01, 'w_a_in': 2.054206e-01, 'conv_w': 2.041331e-01, 'w_a_out': 2.049969e-01, 'norm_kv': 8.781998e-02, 'w_kv': 4.789919e-02, 'w_q': 2.527574e-02, 'w_o': 5.295020e-02, 'w_up': 5.611158e-02, 'w_down': 2.532401e-01, 'norm_final': 6.449810e+01}


def _to_microbatches(a, axis):
    t = _jnp.moveaxis(a, axis, 0)
    t = t.reshape((N_MICROBATCH, t.shape[0] // N_MICROBATCH) + t.shape[1:])
    return _jnp.moveaxis(t, 1, axis + 1)


def setup_inputs(seed: int = 0) -> dict:
    inp = _fwd_setup_inputs(seed)
    key = _jax.random.fold_in(_jax.random.key(seed), 7919)
    shape, _ = _output_shape()
    out = dict(inp)
    out["loss_target"] = _jax.random.normal(_jax.random.fold_in(key, 0), shape, _jnp.float32)
    for i, name in enumerate(TWIN_WEIGHTS):
        w = inp[name].astype(_jnp.float32)
        if MOMENT_SCALE is None:
            s = _jnp.sqrt(_jnp.mean(_jnp.square(w)) + 1e-30)
        else:
            s = MOMENT_SCALE[name]
        km, kv = _jax.random.split(_jax.random.fold_in(key, i + 1))
        out[name] = w
        out["m_" + name] = s * _jax.random.normal(km, w.shape, _jnp.float32)
        out["v_" + name] = (s * s) * _jax.random.uniform(kv, w.shape, _jnp.float32, 0.5, 1.5)
    if N_MICROBATCH > 1:
        for name, axis in PER_EXAMPLE_BATCH_AXIS.items():
            out[name] = _to_microbatches(out[name], axis)
    return {'x': out['x'], 'norm_mix': out['norm_mix'], 'norm_mlp': out['norm_mlp'], 'w_a_in': out['w_a_in'], 'conv_w': out['conv_w'], 'w_a_out': out['w_a_out'], 'norm_kv': out['norm_kv'], 'w_kv': out['w_kv'], 'w_q': out['w_q'], 'w_o': out['w_o'], 'w_up': out['w_up'], 'w_down': out['w_down'], 'norm_final': out['norm_final'], 'loss_target': out['loss_target'], 'm_norm_mix': out['m_norm_mix'], 'm_norm_mlp': out['m_norm_mlp'], 'm_w_a_in': out['m_w_a_in'], 'm_conv_w': out['m_conv_w'], 'm_w_a_out': out['m_w_a_out'], 'm_norm_kv': out['m_norm_kv'], 'm_w_kv': out['m_w_kv'], 'm_w_q': out['m_w_q'], 'm_w_o': out['m_w_o'], 'm_w_up': out['m_w_up'], 'm_w_down': out['m_w_down'], 'm_norm_final': out['m_norm_final'], 'v_norm_mix': out['v_norm_mix'], 'v_norm_mlp': out['v_norm_mlp'], 'v_w_a_in': out['v_w_a_in'], 'v_conv_w': out['v_conv_w'], 'v_w_a_out': out['v_w_a_out'], 'v_norm_kv': out['v_norm_kv'], 'v_w_kv': out['v_w_kv'], 'v_w_q': out['v_w_q'], 'v_w_o': out['v_w_o'], 'v_w_up': out['v_w_up'], 'v_w_down': out['v_w_down'], 'v_norm_final': out['v_norm_final']}


def _loss(weights, diff, rest, loss_target):
    with _jax.named_scope("forward"):
        args = {**rest, TWIN_DIFF_INPUT: diff, **{k: w.astype(_WEIGHT_DTYPES[k]) for k, w in weights.items()}}
        y = _forward(args)
    with _jax.named_scope("loss_head"):
        err = _jnp.square(y.astype(_jnp.float32) - loss_target)
        return 0.5 * _jnp.sum(_jnp.mean(err, axis=-1)) if err.ndim else 0.5 * err


def _adamw(w, g, m, v):
    m = ADAM_B1 * m + (1.0 - ADAM_B1) * g
    v = ADAM_B2 * v + (1.0 - ADAM_B2) * _jnp.square(g)
    m_hat = m / (1.0 - ADAM_B1 ** ADAM_STEP)
    v_hat = v / (1.0 - ADAM_B2 ** ADAM_STEP)
    delta = -ADAM_LR * (m_hat / (_jnp.sqrt(v_hat) + ADAM_EPS) + ADAM_WD * w)
    return delta, m, v


def reference(x, norm_mix, norm_mlp, w_a_in, conv_w, w_a_out, norm_kv, w_kv, w_q, w_o, w_up, w_down, norm_final, loss_target, m_norm_mix, m_norm_mlp, m_w_a_in, m_conv_w, m_w_a_out, m_norm_kv, m_w_kv, m_w_q, m_w_o, m_w_up, m_w_down, m_norm_final, v_norm_mix, v_norm_mlp, v_w_a_in, v_conv_w, v_w_a_out, v_norm_kv, v_w_kv, v_w_q, v_w_o, v_w_up, v_w_down, v_norm_final):
    given = dict(x=x, norm_mix=norm_mix, norm_mlp=norm_mlp, w_a_in=w_a_in, conv_w=conv_w, w_a_out=w_a_out, norm_kv=norm_kv, w_kv=w_kv, w_q=w_q, w_o=w_o, w_up=w_up, w_down=w_down, norm_final=norm_final, loss_target=loss_target, m_norm_mix=m_norm_mix, m_norm_mlp=m_norm_mlp, m_w_a_in=m_w_a_in, m_conv_w=m_conv_w, m_w_a_out=m_w_a_out, m_norm_kv=m_norm_kv, m_w_kv=m_w_kv, m_w_q=m_w_q, m_w_o=m_w_o, m_w_up=m_w_up, m_w_down=m_w_down, m_norm_final=m_norm_final, v_norm_mix=v_norm_mix, v_norm_mlp=v_norm_mlp, v_w_a_in=v_w_a_in, v_conv_w=v_conv_w, v_w_a_out=v_w_a_out, v_norm_kv=v_norm_kv, v_w_kv=v_w_kv, v_w_q=v_w_q, v_w_o=v_w_o, v_w_up=v_w_up, v_w_down=v_w_down, v_norm_final=v_norm_final)
    weights = {n: given[n] for n in TWIN_WEIGHTS}
    shared = {n: given[n] for n in SHARED_INPUTS}
    per_example = {n: given[n] for n in ['x']}
    grad_fn = _jax.value_and_grad(_loss, argnums=(0, 1))

    def one_microbatch(ex, loss_target):
        ex = dict(ex)
        diff = ex.pop(TWIN_DIFF_INPUT)
        return grad_fn(weights, diff, {**shared, **ex}, loss_target)

    if N_MICROBATCH == 1:
        loss, (grad_w, grad_x) = one_microbatch(per_example, given["loss_target"])
    else:
        def body(carry, xs):
            loss_sum, grad_sum = carry
            l_k, (gw_k, gx_k) = one_microbatch(xs[0], xs[1])
            with _jax.named_scope("update"):
                return (loss_sum + l_k, _jax.tree.map(_jnp.add, grad_sum, gw_k)), gx_k

        init = (_jnp.zeros((), _jnp.float32), _jax.tree.map(_jnp.zeros_like, weights))
        (loss, grad_w), grad_x = _jax.lax.scan(body, init, (per_example, given["loss_target"]))
    with _jax.named_scope("update"):
        delta_w, new_m, new_v = {}, {}, {}
        for n in TWIN_WEIGHTS:
            delta_w[n], new_m[n], new_v[n] = _adamw(weights[n], grad_w[n], given["m_" + n], given["v_" + n])
    return (loss, grad_x, *[grad_w[n] for n in TWIN_WEIGHTS], *[delta_w[n] for n in TWIN_WEIGHTS],
            *[new_m[n] for n in TWIN_WEIGHTS], *[new_v[n] for n in TWIN_WEIGHTS])
```

```python
import functools

import jax
import jax.numpy as jnp
from jax import lax
from jax.experimental import pallas as pl
from jax.experimental.pallas import tpu as pltpu

F32 = jnp.float32
BF16 = jnp.bfloat16
SDS = jax.ShapeDtypeStruct

EPS = 1e-5
N_A_LAYERS = 2
DEPTH = 4
PATTERNS = ((128, 1), (512, 4), (2048, 16))
N_GROUPS = 3
H_G = 8
HEAD_DIM = 64
QW = H_G * HEAD_DIM
ATT_BLK = 128
ALIBI_MAX_BIAS = 8.0
NEG_INF = -1e30

ADAM_LR = 0.001
ADAM_B1 = 0.9
ADAM_B2 = 0.999
ADAM_EPS = 1e-08
ADAM_WD = 0.01
ADAM_STEP = 10

N_DEV = 8
PACK_COLS = 1024
SUBLANES = 8
V7X_VMEM_LIMIT = 48 * 1024 * 1024
MM_CHUNK = 512

BIG_WEIGHTS = (("w_a_in", 2), ("w_a_out", 1), ("w_kv", 1), ("w_q", 2), ("w_o", 2), ("w_up", 2), ("w_down", 1))


def _call(name, body, grid, ins, outs, scratch=()):
    res = pl.pallas_call(
        body,
        name=name,
        grid=grid,
        in_specs=[s for _, s in ins],
        out_specs=[s for _, s in outs],
        out_shape=[o for o, _ in outs],
        scratch_shapes=list(scratch),
        compiler_params=pltpu.CompilerParams(
            dimension_semantics=("arbitrary",) * len(grid), vmem_limit_bytes=V7X_VMEM_LIMIT),
    )(*[a for a, _ in ins])
    return res


def _rows(a, tm, cb=None, col=0):
    cb = cb or a.shape[1]
    return (a, pl.BlockSpec((tm, cb), lambda i: (i, col)))


def _full(a):
    nd = a.ndim
    return (a, pl.BlockSpec(a.shape, lambda i: (0,) * nd))


def _prev8(a, tm, cb, col):
    return (a, pl.BlockSpec((SUBLANES, cb), lambda i: (jnp.maximum(i * (tm // SUBLANES) - 1, 0), col)))


def _next8(a, tm, cb, col):
    last = a.shape[0] // SUBLANES - 1
    return (a, pl.BlockSpec((SUBLANES, cb), lambda i: (jnp.minimum((i + 1) * (tm // SUBLANES), last), col)))


def _rows2(a, tt, cb=None, colfn=None):
    cb = cb or a.shape[1]
    colfn = colfn or (lambda s: 0)
    return (a, pl.BlockSpec((tt, cb), lambda s, t: (t, colfn(s))))


def _full2(a):
    nd = a.ndim
    return (a, pl.BlockSpec(a.shape, lambda s, t: (0,) * nd))


def _prev8_2(a, tt, cb, col):
    return (a, pl.BlockSpec((SUBLANES, cb), lambda s, t: (jnp.maximum(t * (tt // SUBLANES) - 1, 0), col)))


def _out_rows(T, n, dtype, tm):
    return (SDS((T, n), dtype), pl.BlockSpec((tm, n), lambda i: (i, 0)))


def _out_acc8(d):
    return (SDS((SUBLANES, d), F32), pl.BlockSpec((SUBLANES, d), lambda i: (0, 0)))


def _rstd(x):
    return lax.rsqrt(jnp.mean(x * x, axis=-1, keepdims=True) + EPS)


def _normed(h_ref, g_ref):
    x = h_ref[...]
    return x * _rstd(x) * g_ref[...]


def _acc8(ref, val, i, n):
    part = val.reshape(-1, SUBLANES, val.shape[-1]).sum(axis=0)

    @pl.when(i == 0)
    def _():
        ref[...] = part

    @pl.when(i > 0)
    def _():
        ref[...] += part

    @pl.when(i == n - 1)
    def _():
        ref[...] = jnp.broadcast_to(jnp.sum(ref[...], axis=0, keepdims=True), ref.shape)


def _gate(b, c, u, c_halo, u_halo, cw_ref, first):
    cu = c * u
    halo = jnp.where(first, 0.0, c_halo * u_halo)
    rows = lax.broadcasted_iota(jnp.int32, cu.shape, 0)
    h1 = halo[SUBLANES - 1:SUBLANES, :]
    h2 = halo[SUBLANES - 2:SUBLANES - 1, :]
    cu1 = jnp.where(rows == 0, h1, pltpu.roll(cu, 1, 0))
    cu2 = jnp.where(rows == 0, h2, jnp.where(rows == 1, h1, pltpu.roll(cu, 2, 0)))
    conv = cw_ref[0:1, :] * cu + cw_ref[1:2, :] * cu1 + cw_ref[2:3, :] * cu2
    return b * conv, conv, cu, cu1, cu2


def _relu2(a):
    r = jnp.maximum(a, 0.0)
    return r * r


def _dot(a, b):
    return jnp.dot(a, b, preferred_element_type=F32)


def _dot_nt(a, b):
    return lax.dot_general(a, b, (((1,), (1,)), ((), ())), preferred_element_type=F32)


def _dot_tn(a, b):
    return lax.dot_general(a, b, (((0,), (0,)), ((), ())), preferred_element_type=F32)


def _chunks(n):
    c = min(MM_CHUNK, n)
    assert n % c == 0, n
    return [(k * c, (k + 1) * c) for k in range(n // c)]


def _norm_mm(name, h, g, w, tm=256):
    T, _ = h.shape
    N = w.shape[1]

    def body(h_ref, g_ref, w_ref, o_ref):
        a = _normed(h_ref, g_ref).astype(BF16)
        for lo, hi in _chunks(N):
            o_ref[:, lo:hi] = _dot(a, w_ref[:, lo:hi])

    return _call(name, body, (T // tm,), [_rows(h, tm), _full(g), _full(w)], [_out_rows(T, N, F32, tm)])[0]


def _gate_mm_res(name, bcu, cw, w, h, seq, tm=256):
    T, D = h.shape

    def body(b_ref, c_ref, u_ref, ch_ref, uh_ref, cw_ref, w_ref, h_ref, o_ref):
        first = (pl.program_id(0) * tm) % seq == 0
        gated = _gate(b_ref[...], c_ref[...], u_ref[...], ch_ref[...], uh_ref[...], cw_ref, first)[0].astype(BF16)
        for lo, hi in _chunks(D):
            o_ref[:, lo:hi] = h_ref[:, lo:hi] + _dot(gated, w_ref[:, lo:hi])

    ins = [_rows(bcu, tm, D, 0), _rows(bcu, tm, D, 1), _rows(bcu, tm, D, 2), _prev8(bcu, tm, D, 1),
           _prev8(bcu, tm, D, 2), _full(cw), _full(w), _rows(h, tm)]
    return _call(name, body, (T // tm,), ins, [_out_rows(T, D, F32, tm)])[0]


def _relu2_mm_res(name, a, w, h, tm=256):
    T, D = h.shape
    K = a.shape[1]

    def body(a_ref, w_ref, h_ref, o_ref, acc_ref):
        for n, (lo, hi) in enumerate(_chunks(K)):
            d = _dot(_relu2(a_ref[:, lo:hi]).astype(BF16), w_ref[lo:hi, :])
            if n == 0:
                acc_ref[...] = d
            else:
                acc_ref[...] += d
        o_ref[...] = h_ref[...] + acc_ref[...]

    return _call(name, body, (T // tm,), [_rows(a, tm), _full(w), _rows(h, tm)], [_out_rows(T, D, F32, tm)],
                 scratch=[pltpu.VMEM((tm, D), F32)])[0]


def _mm_res(name, a, w, h, tm=256):
    T, D = h.shape

    def body(a_ref, w_ref, h_ref, o_ref):
        av = a_ref[...].astype(BF16)
        for lo, hi in _chunks(D):
            o_ref[:, lo:hi] = h_ref[:, lo:hi] + _dot(av, w_ref[:, lo:hi])

    return _call(name, body, (T // tm,), [_rows(a, tm), _full(w), _rows(h, tm)], [_out_rows(T, D, F32, tm)])[0]


def _nt_relu2_bwd(name, dh, w, a, tm=256):
    T, _ = dh.shape
    K = w.shape[0]

    def body(dh_ref, w_ref, a_ref, o_ref):
        d = dh_ref[...].astype(BF16)
        for lo, hi in _chunks(K):
            dr = _dot_nt(d, w_ref[lo:hi, :])
            o_ref[:, lo:hi] = (dr * (2.0 * jnp.maximum(a_ref[:, lo:hi], 0.0))).astype(BF16)

    return _call(name, body, (T // tm,), [_rows(dh, tm), _full(w), _rows(a, tm)], [_out_rows(T, K, BF16, tm)])[0]


def _concat_bf16(*refs):
    vals = [r[...].astype(BF16) for r in refs]
    return vals[0] if len(vals) == 1 else jnp.concatenate(vals, axis=1)


def _nt_plain(name, dy, w, tm=256):
    T, N = dy.shape
    K = w.shape[0]

    def body(dy_ref, w_ref, o_ref, acc_ref):
        for n, (lo, hi) in enumerate(_chunks(N)):
            d = _dot_nt(dy_ref[:, lo:hi].astype(BF16), w_ref[:, lo:hi])
            if n == 0:
                acc_ref[...] = d
            else:
                acc_ref[...] += d
        o_ref[...] = acc_ref[...]

    return _call(name, body, (T // tm,), [_rows(dy, tm), _full(w)], [_out_rows(T, K, F32, tm)],
                 scratch=[pltpu.VMEM((tm, K), F32)])[0]


def _nt_norm_bwd(name, dys, w, h, g, dh_in, tm=256):
    T, D = h.shape
    N = w.shape[1]
    n_steps = T // tm
    n_dy = len(dys)

    def body(*refs):
        dy_refs = refs[:n_dy]
        w_ref, h_ref, g_ref, dhin_ref, o_ref, dg_ref, acc_ref = refs[n_dy:]
        i = pl.program_id(0)
        dy = _concat_bf16(*dy_refs)
        for n, (lo, hi) in enumerate(_chunks(N)):
            d = _dot_nt(dy[:, lo:hi], w_ref[:, lo:hi])
            if n == 0:
                acc_ref[...] = d
            else:
                acc_ref[...] += d
        dn = acc_ref[...]
        x = h_ref[...]
        rstd = _rstd(x)
        xhat = x * rstd
        dxhat = dn * g_ref[...]
        dx = rstd * (dxhat - xhat * jnp.mean(dxhat * xhat, axis=-1, keepdims=True))
        o_ref[...] = dhin_ref[...] + dx
        _acc8(dg_ref, dn * xhat, i, n_steps)

    ins = [_rows(d, tm) for d in dys] + [_full(w), _rows(h, tm), _full(g), _rows(dh_in, tm)]
    outs = [_out_rows(T, D, F32, tm), _out_acc8(D)]
    dh, dg = _call(name, body, (n_steps,), ins, outs, scratch=[pltpu.VMEM((tm, D), F32)])
    return dh, dg[0:1]


def _tn(name, a_ins, a_fn, y_ins, y_fn, K, N, T, tt, split=None):
    kind, parts = split or ("n", 1)
    kb, nb = (K // parts, N) if kind == "k" else (K, N // parts)
    n_steps = T // tt
    n_a = len(a_ins)
    n_y = len(y_ins)

    def body(*refs):
        a_refs = refs[:n_a]
        y_refs = refs[n_a:n_a + n_y]
        o_ref, acc_ref = refs[n_a + n_y:]
        t = pl.program_id(1)
        a = a_fn(*a_refs).astype(BF16)
        y = y_fn(*y_refs).astype(BF16)
        for lo, hi in _chunks(nb):
            d = _dot_tn(a, y[:, lo:hi])

            @pl.when(t == 0)
            def _():
                acc_ref[:, lo:hi] = d

            @pl.when(t > 0)
            def _():
                acc_ref[:, lo:hi] += d

        @pl.when(t == n_steps - 1)
        def _():
            o_ref[...] = acc_ref[...].astype(BF16)

    out_spec = pl.BlockSpec((kb, nb), (lambda s, t: (s, 0)) if kind == "k" else (lambda s, t: (0, s)))
    return _call(name, body, (parts, n_steps), list(a_ins) + list(y_ins), [(SDS((K, N), BF16), out_spec)],
                 scratch=[pltpu.VMEM((kb, nb), F32)])[0]


def _val(ref):
    return ref[...]


def _concat_f32(*refs):
    vals = [r[...] for r in refs]
    return vals[0] if len(vals) == 1 else jnp.concatenate(vals, axis=1)


def _att_masks(n, dil):
    a = lax.broadcasted_iota(jnp.int32, (ATT_BLK, ATT_BLK), 0)
    c = lax.broadcasted_iota(jnp.int32, (ATT_BLK, ATT_BLK), 1)
    dist_prev = ((ATT_BLK + a - c) * dil).astype(F32)
    dist_cur = ((a - c) * dil).astype(F32)
    return (c >= a) & (n > 0), c <= a, dist_prev, dist_cur


def _slope(h):
    return 2.0 ** (-ALIBI_MAX_BIAS * (h + 1) / H_G)


def _att_views(t, bl, dil, width):
    return t.reshape(bl, t.shape[0] // (bl * dil), dil * width)


def _attn_fwd(name, q, kv, g, bl):
    T = q.shape[0]
    dil = PATTERNS[g][1]
    m_len = T // (bl * dil)
    nb = m_len // ATT_BLK
    scale = HEAD_DIM ** -0.5
    qv = _att_views(q, bl, dil, N_GROUPS * QW)
    kvv = _att_views(kv, bl, dil, 2 * N_GROUPS * QW)

    def body(q_ref, kp_ref, kc_ref, vp_ref, vc_ref, o_ref, lse_ref):
        n = pl.program_id(2)
        valid_p, valid_c, dist_p, dist_c = _att_masks(n, dil)
        for h in range(H_G):
            hs = slice(h * HEAD_DIM, (h + 1) * HEAD_DIM)
            qh = q_ref[:, hs].astype(BF16)
            sp = _dot_nt(qh, kp_ref[:, hs].astype(BF16)) * scale
            sc = _dot_nt(qh, kc_ref[:, hs].astype(BF16)) * scale
            sp = jnp.where(valid_p, sp - _slope(h) * dist_p, NEG_INF)
            sc = jnp.where(valid_c, sc - _slope(h) * dist_c, NEG_INF)
            mx = jnp.maximum(jnp.max(sp, axis=-1, keepdims=True), jnp.max(sc, axis=-1, keepdims=True))
            ep = jnp.exp(sp - mx)
            ec = jnp.exp(sc - mx)
            den = jnp.sum(ep, axis=-1, keepdims=True) + jnp.sum(ec, axis=-1, keepdims=True)
            acc = _dot(ep.astype(BF16), vp_ref[:, hs].astype(BF16)) + _dot(ec.astype(BF16), vc_ref[:, hs].astype(BF16))
            o_ref[:, hs] = acc / den
            lse_ref[:, hs] = jnp.broadcast_to(mx + jnp.log(den), (ATT_BLK, HEAD_DIM))

    blk = (None, ATT_BLK, QW)
    ins = [
        (qv, pl.BlockSpec(blk, lambda b, r, n: (b, n, r * 3 + g))),
        (kvv, pl.BlockSpec(blk, lambda b, r, n: (b, jnp.maximum(n - 1, 0), r * 6 + 2 * g))),
        (kvv, pl.BlockSpec(blk, lambda b, r, n: (b, n, r * 6 + 2 * g))),
        (kvv, pl.BlockSpec(blk, lambda b, r, n: (b, jnp.maximum(n - 1, 0), r * 6 + 2 * g + 1))),
        (kvv, pl.BlockSpec(blk, lambda b, r, n: (b, n, r * 6 + 2 * g + 1))),
    ]
    out = (SDS((bl, m_len, dil * QW), F32), pl.BlockSpec(blk, lambda b, r, n: (b, n, r)))
    o, lse = _call(name, body, (bl, dil, nb), ins, [out, out])
    return o.reshape(T, QW), lse.reshape(T, QW)


def _combine(name, os_, lses, tm=512):
    T = os_[0].shape[0]

    def body(o0, o1, o2, l0, l1, l2, o_ref, lse_ref):
        ls = [l0[...], l1[...], l2[...]]
        mx = jnp.maximum(jnp.maximum(ls[0], ls[1]), ls[2])
        es = [jnp.exp(l - mx) for l in ls]
        den = es[0] + es[1] + es[2]
        o_ref[...] = (es[0] * o0[...] + es[1] * o1[...] + es[2] * o2[...]) / den
        lse_ref[...] = mx + jnp.log(den)

    ins = [_rows(t, tm) for t in list(os_) + list(lses)]
    return _call(name, body, (T // tm,), ins, [_out_rows(T, QW, F32, tm), _out_rows(T, QW, F32, tm)])


def _attn_bwd_dq(name, q, kv, do, o, lse, g, bl):
    T = q.shape[0]
    dil = PATTERNS[g][1]
    m_len = T // (bl * dil)
    nb = m_len // ATT_BLK
    scale = HEAD_DIM ** -0.5
    qv = _att_views(q, bl, dil, N_GROUPS * QW)
    kvv = _att_views(kv, bl, dil, 2 * N_GROUPS * QW)
    dov, ov, lsev = (_att_views(t, bl, dil, QW) for t in (do, o, lse))

    def body(q_ref, kp_ref, kc_ref, vp_ref, vc_ref, do_ref, o_ref, lse_ref, dq_ref):
        n = pl.program_id(2)
        valid_p, valid_c, dist_p, dist_c = _att_masks(n, dil)
        for h in range(H_G):
            hs = slice(h * HEAD_DIM, (h + 1) * HEAD_DIM)
            qh = q_ref[:, hs].astype(BF16)
            kp = kp_ref[:, hs].astype(BF16)
            kc = kc_ref[:, hs].astype(BF16)
            lse_h = lse_ref[:, h * HEAD_DIM:h * HEAD_DIM + 1]
            sp = jnp.where(valid_p, _dot_nt(qh, kp) * scale - _slope(h) * dist_p, NEG_INF)
            sc = jnp.where(valid_c, _dot_nt(qh, kc) * scale - _slope(h) * dist_c, NEG_INF)
            pp = jnp.exp(sp - lse_h)
            pc = jnp.exp(sc - lse_h)
            doh = do_ref[:, hs]
            delta = jnp.sum(doh * o_ref[:, hs], axis=-1, keepdims=True)
            dob = doh.astype(BF16)
            dsp = pp * (_dot_nt(dob, vp_ref[:, hs].astype(BF16)) - delta)
            dsc = pc * (_dot_nt(dob, vc_ref[:, hs].astype(BF16)) - delta)
            dq_ref[:, hs] = (_dot(dsp.astype(BF16), kp) + _dot(dsc.astype(BF16), kc)) * scale

    blk = (None, ATT_BLK, QW)
    own = pl.BlockSpec(blk, lambda b, r, n: (b, n, r))
    ins = [
        (qv, pl.BlockSpec(blk, lambda b, r, n: (b, n, r * 3 + g))),
        (kvv, pl.BlockSpec(blk, lambda b, r, n: (b, jnp.maximum(n - 1, 0), r * 6 + 2 * g))),
        (kvv, pl.BlockSpec(blk, lambda b, r, n: (b, n, r * 6 + 2 * g))),
        (kvv, pl.BlockSpec(blk, lambda b, r, n: (b, jnp.maximum(n - 1, 0), r * 6 + 2 * g + 1))),
        (kvv, pl.BlockSpec(blk, lambda b, r, n: (b, n, r * 6 + 2 * g + 1))),
        (dov, own), (ov, own), (lsev, own),
    ]
    dq = _call(name, body, (bl, dil, nb), ins, [(SDS((bl, m_len, dil * QW), F32), own)])[0]
    return dq.reshape(T, QW)


def _attn_bwd_dkv(name, q, kv, do, o, lse, g, bl, prev=None):
    T = q.shape[0]
    dil = PATTERNS[g][1]
    m_len = T // (bl * dil)
    nb = m_len // ATT_BLK
    scale = HEAD_DIM ** -0.5
    qv = _att_views(q, bl, dil, N_GROUPS * QW)
    kvv = _att_views(kv, bl, dil, 2 * N_GROUPS * QW)
    dov, ov, lsev = (_att_views(t, bl, dil, QW) for t in (do, o, lse))
    has_prev = prev is not None

    def body(*refs):
        k_ref, v_ref, qc_ref, qn_ref, doc_ref, don_ref, oc_ref, on_ref, lc_ref, ln_ref = refs[:10]
        rest = refs[10:]
        if has_prev:
            dkp_ref, dvp_ref, dk_ref, dv_ref = rest
        else:
            dk_ref, dv_ref = rest
        n = pl.program_id(2)
        a = lax.broadcasted_iota(jnp.int32, (ATT_BLK, ATT_BLK), 0)
        c = lax.broadcasted_iota(jnp.int32, (ATT_BLK, ATT_BLK), 1)
        valid_c = c <= a
        valid_n = (c >= a) & (n + 1 < nb)
        dist_c = ((a - c) * dil).astype(F32)
        dist_n = ((ATT_BLK + a - c) * dil).astype(F32)
        for h in range(H_G):
            hs = slice(h * HEAD_DIM, (h + 1) * HEAD_DIM)
            one = slice(h * HEAD_DIM, h * HEAD_DIM + 1)
            kh = k_ref[:, hs].astype(BF16)
            vh = v_ref[:, hs].astype(BF16)
            dk = jnp.zeros((ATT_BLK, HEAD_DIM), F32)
            dv = jnp.zeros((ATT_BLK, HEAD_DIM), F32)
            for q_ref, do_ref, o_ref, l_ref, valid, dist in (
                    (qc_ref, doc_ref, oc_ref, lc_ref, valid_c, dist_c), (qn_ref, don_ref, on_ref, ln_ref, valid_n, dist_n)):
                qh = q_ref[:, hs].astype(BF16)
                s = jnp.where(valid, _dot_nt(qh, kh) * scale - _slope(h) * dist, NEG_INF)
                p = jnp.exp(s - l_ref[:, one])
                doh = do_ref[:, hs]
                delta = jnp.sum(doh * o_ref[:, hs], axis=-1, keepdims=True)
                dob = doh.astype(BF16)
                ds = p * (_dot_nt(dob, vh) - delta)
                dv = dv + _dot_tn(p.astype(BF16), dob)
                dk = dk + _dot_tn(ds.astype(BF16), qh)
            dk = dk * scale
            if has_prev:
                dk = dk + dkp_ref[:, hs]
                dv = dv + dvp_ref[:, hs]
            dk_ref[:, hs] = dk
            dv_ref[:, hs] = dv

    blk = (None, ATT_BLK, QW)
    own = pl.BlockSpec(blk, lambda b, r, n: (b, n, r))
    nxt = pl.BlockSpec(blk, lambda b, r, n: (b, jnp.minimum(n + 1, nb - 1), r))
    ins = [
        (kvv, pl.BlockSpec(blk, lambda b, r, n: (b, n, r * 6 + 2 * g))),
        (kvv, pl.BlockSpec(blk, lambda b, r, n: (b, n, r * 6 + 2 * g + 1))),
        (qv, pl.BlockSpec(blk, lambda b, r, n: (b, n, r * 3 + g))),
        (qv, pl.BlockSpec(blk, lambda b, r, n: (b, jnp.minimum(n + 1, nb - 1), r * 3 + g))),
        (dov, own), (dov, nxt), (ov, own), (ov, nxt), (lsev, own), (lsev, nxt),
    ]
    if has_prev:
        ins += [(_att_views(prev[0], bl, dil, QW), own), (_att_views(prev[1], bl, dil, QW), own)]
    out = (SDS((bl, m_len, dil * QW), F32), own)
    dk, dv = _call(name, body, (bl, dil, nb), ins, [out, out])
    return dk.reshape(T, QW), dv.reshape(T, QW)


def _final_loss(name, h, tgt, g, tm=256):
    T, D = h.shape
    n_steps = T // tm

    def body(h_ref, t_ref, g_ref, dh_ref, loss_ref, dg_ref, sq_ref):
        i = pl.program_id(0)
        x = h_ref[...]
        rstd = _rstd(x)
        xhat = x * rstd
        err = xhat * g_ref[...] - t_ref[...]
        _acc8(sq_ref, err * err, i, n_steps)
        dy = err * (1.0 / D)
        dxhat = dy * g_ref[...]
        dh_ref[...] = rstd * (dxhat - xhat * jnp.mean(dxhat * xhat, axis=-1, keepdims=True))
        _acc8(dg_ref, dy * xhat, i, n_steps)

        @pl.when(i == n_steps - 1)
        def _():
            loss_ref[...] = jnp.full(loss_ref.shape, jnp.sum(sq_ref[0:1, :]), F32)

    outs = [_out_rows(T, D, F32, tm), (SDS((SUBLANES, 128), F32), pl.BlockSpec((SUBLANES, 128), lambda i: (0, 0))),
            _out_acc8(D)]
    dh, loss, dg = _call(name, body, (n_steps,), [_rows(h, tm), _rows(tgt, tm), _full(g)], outs,
                         scratch=[pltpu.VMEM((SUBLANES, D), F32)])
    return dh, loss[0, 0], dg[0:1]


def _conv_bwd(name, bcu, dgated, cw, seq, tm=256):
    T, D = dgated.shape
    n_steps = T // tm

    def body(b_ref, c_ref, u_ref, ch_ref, uh_ref, dg_ref, dgn_ref, bn_ref, cw_ref, o_ref, t0_ref, t1_ref, t2_ref):
        i = pl.program_id(0)
        first = (i * tm) % seq == 0
        last = ((i + 1) * tm) % seq == 0
        b = b_ref[...]
        c = c_ref[...]
        u = u_ref[...]
        _, conv, cu, cu1, cu2 = _gate(b, c, u, ch_ref[...], uh_ref[...], cw_ref, first)
        dgat = dg_ref[...]
        dconv = dgat * b
        nxt = jnp.where(last, 0.0, dgn_ref[...] * bn_ref[...])
        rows = lax.broadcasted_iota(jnp.int32, dconv.shape, 0)
        n1 = nxt[0:1, :]
        n2 = nxt[1:2, :]
        dc1 = jnp.where(rows == tm - 1, n1, pltpu.roll(dconv, tm - 1, 0))
        dc2 = jnp.where(rows == tm - 1, n2, jnp.where(rows == tm - 2, n1, pltpu.roll(dconv, tm - 2, 0)))
        dcu = cw_ref[0:1, :] * dconv + cw_ref[1:2, :] * dc1 + cw_ref[2:3, :] * dc2
        o_ref[:, 0:D] = (dgat * conv).astype(BF16)
        o_ref[:, D:2 * D] = (dcu * u).astype(BF16)
        o_ref[:, 2 * D:3 * D] = (dcu * c).astype(BF16)
        _acc8(t0_ref, dconv * cu, i, n_steps)
        _acc8(t1_ref, dconv * cu1, i, n_steps)
        _acc8(t2_ref, dconv * cu2, i, n_steps)

    ins = [_rows(bcu, tm, D, 0), _rows(bcu, tm, D, 1), _rows(bcu, tm, D, 2), _prev8(bcu, tm, D, 1), _prev8(bcu, tm, D, 2),
           _rows(dgated, tm), _next8(dgated, tm, D, 0), _next8(bcu, tm, D, 0), _full(cw)]
    outs = [_out_rows(T, 3 * D, BF16, tm), _out_acc8(D), _out_acc8(D), _out_acc8(D)]
    dbcu, t0, t1, t2 = _call(name, body, (n_steps,), ins, outs)
    return dbcu, jnp.concatenate([t0[0:1], t1[0:1], t2[0:1]], axis=0)


def _sum8_adamw(name, parts, w, m, v, tr):
    R, C = w.shape
    b1c = 1.0 - ADAM_B1 ** ADAM_STEP
    b2c = 1.0 - ADAM_B2 ** ADAM_STEP

    def body(p_ref, w_ref, m_ref, v_ref, g_ref, d_ref, nm_ref, nv_ref):
        g = p_ref[0].astype(F32)
        for j in range(1, N_DEV):
            g = g + p_ref[j].astype(F32)
        nm = ADAM_B1 * m_ref[...] + (1.0 - ADAM_B1) * g
        nv = ADAM_B2 * v_ref[...] + (1.0 - ADAM_B2) * (g * g)
        m_hat = nm / b1c
        v_hat = nv / b2c
        g_ref[...] = g
        d_ref[...] = -ADAM_LR * (m_hat / (jnp.sqrt(v_hat) + ADAM_EPS) + ADAM_WD * w_ref[...])
        nm_ref[...] = nm
        nv_ref[...] = nv

    ins = [(parts, pl.BlockSpec((N_DEV, tr, C), lambda i: (0, i, 0))), _rows(w, tr), _rows(m, tr), _rows(v, tr)]
    outs = [_out_rows(R, C, F32, tr)] * 4
    return _call(name, body, (R // tr,), ins, outs)


def _mesh_pos():
    return lax.axis_index("x"), lax.axis_index("y"), lax.axis_index("c")


def _flip(v, bit):
    return 1 - v if bit else v


def _all_gather(name, shard):
    R, C = shard.shape

    def body(x_ref, out_ref, send_sems, recv_sems, local_sem):
        x, y, c = _mesh_pos()
        me, sibling = (x, y, c), (x, y, 1 - c)
        chips = [(1 - x, y), (x, 1 - y), (1 - x, 1 - y)]

        def slot(px, py, pc):
            return out_ref.at[4 * px + 2 * py + pc]

        def copy(k, block, to, src=None):
            return pltpu.make_async_remote_copy(
                src_ref=slot(*block) if src is None else src, dst_ref=slot(*block),
                send_sem=send_sems.at[k], recv_sem=recv_sems.at[k], device_id=to, device_id_type=pl.DeviceIdType.MESH)

        mine = pltpu.make_async_copy(x_ref, slot(*me), local_sem)
        mine.start()
        first = [copy(0, me, sibling, src=x_ref)]
        first += [copy(1 + j, me, (*chip, c), src=x_ref) for j, chip in enumerate(chips)]
        for cp in first:
            cp.start()
        passed = [copy(4 + j, (*chip, c), sibling) for j, chip in enumerate(chips)]
        for j, chip in enumerate(chips):
            copy(1 + j, (*chip, c), me).wait_recv()
            passed[j].start()
        copy(0, sibling, me).wait_recv()
        for j, chip in enumerate(chips):
            copy(4 + j, (*chip, 1 - c), me).wait_recv()
        for cp in first + passed:
            cp.wait_send()
        mine.wait()

    return pl.pallas_call(
        body, name=name,
        out_shape=SDS((N_DEV, R, C), shard.dtype),
        in_specs=[pl.BlockSpec(memory_space=pl.ANY)],
        out_specs=pl.BlockSpec(memory_space=pl.ANY),
        scratch_shapes=[pltpu.SemaphoreType.DMA((7,)), pltpu.SemaphoreType.DMA((7,)), pltpu.SemaphoreType.DMA(())],
    )(shard)


def _all_to_all(name, blocks):
    _, R, C = blocks.shape

    def body(g_ref, out_ref, send_sems, recv_sems, local_sem):
        x, y, c = _mesh_pos()
        me_idx = 4 * x + 2 * y + c
        mine = pltpu.make_async_copy(g_ref.at[me_idx], out_ref.at[me_idx], local_sem)
        mine.start()
        sends, recvs = [], []
        for k in range(1, N_DEV):
            px, py, pc = _flip(x, k & 4), _flip(y, k & 2), _flip(c, k & 1)
            peer_idx = 4 * px + 2 * py + pc
            sends.append(pltpu.make_async_remote_copy(
                src_ref=g_ref.at[peer_idx], dst_ref=out_ref.at[me_idx], send_sem=send_sems.at[k - 1],
                recv_sem=recv_sems.at[k - 1], device_id=(px, py, pc), device_id_type=pl.DeviceIdType.MESH))
            recvs.append(pltpu.make_async_remote_copy(
                src_ref=g_ref.at[peer_idx], dst_ref=out_ref.at[peer_idx], send_sem=send_sems.at[k - 1],
                recv_sem=recv_sems.at[k - 1], device_id=(px, py, pc), device_id_type=pl.DeviceIdType.MESH))
        for cp in sends:
            cp.start()
        for cp in recvs:
            cp.wait_recv()
        for cp in sends:
            cp.wait_send()
        mine.wait()

    return pl.pallas_call(
        body, name=name,
        out_shape=SDS((N_DEV, R, C), blocks.dtype),
        in_specs=[pl.BlockSpec(memory_space=pl.ANY)],
        out_specs=pl.BlockSpec(memory_space=pl.ANY),
        scratch_shapes=[pltpu.SemaphoreType.DMA((7,)), pltpu.SemaphoreType.DMA((7,)), pltpu.SemaphoreType.DMA(())],
    )(blocks)


def _pack_rows(shape):
    n = 1
    for s in shape:
        n *= s
    assert n % PACK_COLS == 0, shape
    return n // PACK_COLS


def _pack_shards(tensors, dtype):
    return jnp.concatenate([t.astype(dtype).reshape(-1, PACK_COLS) for t in tensors], axis=0)


def _unpack_gathered(gathered, shard_shapes):
    out, r0 = [], 0
    for (_, ax), shp in zip(BIG_WEIGHTS, shard_shapes):
        r1 = r0 + _pack_rows(shp)
        seg = jnp.moveaxis(gathered[:, r0:r1].reshape((N_DEV,) + tuple(shp)), 0, ax)
        full = list(shp)
        full[ax] *= N_DEV
        out.append(seg.reshape(full))
        r0 = r1
    return out


def _pack_full_grads(grads, shard_shapes):
    segs = []
    for (_, ax), shp, gfull in zip(BIG_WEIGHTS, shard_shapes, grads):
        split = list(shp)
        split.insert(ax, N_DEV)
        seg = jnp.moveaxis(gfull.reshape(split), ax, 0)
        segs.append(seg.reshape(N_DEV, _pack_rows(shp), PACK_COLS))
    return jnp.concatenate(segs, axis=1)


def _unpack_shards(packed, shard_shapes):
    out, r0 = [], 0
    for shp in shard_shapes:
        r1 = r0 + _pack_rows(shp)
        out.append(packed[r0:r1].reshape(shp))
        r0 = r1
    return out


def _pad8(t):
    return jnp.pad(t, ((0, SUBLANES - t.shape[0]), (0, 0)))


def _local_grads(x, tgt, norm_mix, norm_mlp, norm_kv, norm_final, conv_w, w_a_in, w_a_out, w_kv, w_q, w_o, w_up, w_down):
    bl, seq, D = x.shape
    T = bl * seq
    h = x.reshape(T, D)
    tgt = tgt.reshape(T, D)
    row = lambda t, l: t[l:l + 1]
    saved = []
    kv = h_kv = None
    for l in range(DEPTH):
        if l < N_A_LAYERS:
            bcu = _norm_mm(f"l{l}_in", h, row(norm_mix, l), w_a_in[l])
            h2 = _gate_mm_res(f"l{l}_conv_out", bcu, _pad8(conv_w[l]), w_a_out[l], h, seq)
            saved.append((h, bcu, h2))
        else:
            i = l - N_A_LAYERS
            if l == N_A_LAYERS:
                h_kv = h
                kv = _norm_mm("kv", h, norm_kv.reshape(1, D), w_kv)
            q = _norm_mm(f"l{l}_q", h, row(norm_mix, l), w_q[i])
            per_group = [_attn_fwd(f"l{l}_att{g}", q, kv, g, bl) for g in range(N_GROUPS)]
            o, lse = _combine(f"l{l}_combine", [p[0] for p in per_group], [p[1] for p in per_group])
            h2 = _mm_res(f"l{l}_att_out", o, w_o[i], h)
            saved.append((h, q, o, lse, h2))
        a = _norm_mm(f"l{l}_up", h2, row(norm_mlp, l), w_up[l])
        h = _relu2_mm_res(f"l{l}_down", a, w_down[l], h2)
        saved[-1] = saved[-1] + (a,)

    dh, sq_err, d_norm_final = _final_loss("loss", h, tgt, norm_final.reshape(1, D))

    d_norm_mix = [None] * DEPTH
    d_norm_mlp = [None] * DEPTH
    d_in, d_out, d_q, d_o, d_up, d_down = ([None] * 2 for _ in range(6))
    d_up = [None] * DEPTH
    d_down = [None] * DEPTH
    d_conv = [None] * N_A_LAYERS
    d_kv = d_norm_kv = None
    dkv_acc = [None] * N_GROUPS
    FF = w_up.shape[2]
    tt = 512
    for l in reversed(range(DEPTH)):
        a, h2 = saved[l][-1], saved[l][-2]
        h_in = saved[l][0]
        g_mlp = row(norm_mlp, l)
        g_mix = row(norm_mix, l)
        da = _nt_relu2_bwd(f"l{l}_down_bwd", dh, w_down[l], a)
        d_down[l] = _tn(f"l{l}_dw_down", [_rows2(a, tt, FF // 2, lambda s: s)], lambda r: _relu2(r[...]),
                        [_rows2(dh, tt)], _val, FF, D, T, tt, split=("k", 2))
        d_up[l] = _tn(f"l{l}_dw_up", [_rows2(h2, tt), _full2(g_mlp)], _normed,
                      [_rows2(da, tt, FF // 2, lambda s: s)], _val, D, FF, T, tt, split=("n", 2))
        dh2, d_norm_mlp[l] = _nt_norm_bwd(f"l{l}_up_bwd", [da], w_up[l], h2, g_mlp, dh)
        if l >= N_A_LAYERS:
            i = l - N_A_LAYERS
            _, q, o, lse, _, _ = saved[l]
            do = _nt_plain(f"l{l}_att_out_bwd", dh2, w_o[i])
            d_o[i] = _tn(f"l{l}_dw_o", [_rows2(o, tt)], _val, [_rows2(dh2, tt)], _val, QW, D, T, tt)
            dqs = []
            for g in range(N_GROUPS):
                dqs.append(_attn_bwd_dq(f"l{l}_att{g}_dq", q, kv, do, o, lse, g, bl))
                dkv_acc[g] = _attn_bwd_dkv(f"l{l}_att{g}_dkv", q, kv, do, o, lse, g, bl, prev=dkv_acc[g])
            d_q[i] = _tn(f"l{l}_dw_q", [_rows2(h_in, tt), _full2(g_mix)], _normed,
                         [_rows2(t, tt) for t in dqs], _concat_f32, D, N_GROUPS * QW, T, tt)
            dh, d_norm_mix[l] = _nt_norm_bwd(f"l{l}_q_bwd", dqs, w_q[i], h_in, g_mix, dh2)
            if l == N_A_LAYERS:
                dkvs = [t for pair in dkv_acc for t in pair]
                g_kv = norm_kv.reshape(1, D)
                d_kv = _tn("dw_kv", [_rows2(h_kv, 256), _full2(g_kv)], _normed,
                           [_rows2(t, 256) for t in dkvs], _concat_f32, D, 2 * N_GROUPS * QW, T, 256)
                dh, d_norm_kv = _nt_norm_bwd("kv_bwd", dkvs, w_kv, h_kv, g_kv, dh)
        else:
            _, bcu, _, _ = saved[l]
            cw = _pad8(conv_w[l])
            dgated = _nt_plain(f"l{l}_conv_out_bwd", dh2, w_a_out[l])

            def gated_tile(b_ref, c_ref, u_ref, ch_ref, uh_ref, cw_ref):
                first = (pl.program_id(1) * tt) % seq == 0
                return _gate(b_ref[...], c_ref[...], u_ref[...], ch_ref[...], uh_ref[...], cw_ref, first)[0]

            d_out[l] = _tn(f"l{l}_dw_conv_out",
                           [_rows2(bcu, tt, D, lambda s: 0), _rows2(bcu, tt, D, lambda s: 1), _rows2(bcu, tt, D, lambda s: 2),
                            _prev8_2(bcu, tt, D, 1), _prev8_2(bcu, tt, D, 2), _full2(cw)], gated_tile,
                           [_rows2(dh2, tt)], _val, D, D, T, tt)
            dbcu, d_conv[l] = _conv_bwd(f"l{l}_conv_bwd", bcu, dgated, cw, seq)
            d_in[l] = _tn(f"l{l}_dw_in", [_rows2(h_in, tt), _full2(g_mix)], _normed, [_rows2(dbcu, tt)], _val,
                          D, 3 * D, T, tt)
            dh, d_norm_mix[l] = _nt_norm_bwd(f"l{l}_in_bwd", [dbcu], w_a_in[l], h_in, g_mix, dh2)

    big = [jnp.stack(d_in), jnp.stack(d_out), d_kv, jnp.stack(d_q), jnp.stack(d_o), jnp.stack(d_up), jnp.stack(d_down)]
    small = jnp.concatenate(d_norm_mix + d_norm_mlp + [d_norm_kv, d_norm_final] + d_conv, axis=0)
    return sq_err, dh.reshape(bl, seq, D), big, small


def kernel(x, norm_mix, norm_mlp, w_a_in, conv_w, w_a_out, norm_kv, w_kv, w_q, w_o, w_up, w_down, norm_final, loss_target, m_norm_mix, m_norm_mlp, m_w_a_in, m_conv_w, m_w_a_out, m_norm_kv, m_w_kv, m_w_q, m_w_o, m_w_up, m_w_down, m_norm_final, v_norm_mix, v_norm_mlp, v_w_a_in, v_conv_w, v_w_a_out, v_norm_kv, v_w_kv, v_w_q, v_w_o, v_w_up, v_w_down, v_norm_final):
    D = x.shape[-1]
    xi, yi, ci = _mesh_pos()
    me_idx = 4 * xi + 2 * yi + ci
    w_big = dict(w_a_in=w_a_in, w_a_out=w_a_out, w_kv=w_kv, w_q=w_q, w_o=w_o, w_up=w_up, w_down=w_down)
    m_big = dict(w_a_in=m_w_a_in, w_a_out=m_w_a_out, w_kv=m_w_kv, w_q=m_w_q, w_o=m_w_o, w_up=m_w_up, w_down=m_w_down)
    v_big = dict(w_a_in=v_w_a_in, w_a_out=v_w_a_out, w_kv=v_w_kv, w_q=v_w_q, w_o=v_w_o, w_up=v_w_up, w_down=v_w_down)
    names = [n for n, _ in BIG_WEIGHTS]
    shard_shapes = [w_big[n].shape for n in names]

    gathered = _all_gather("gather_weights", _pack_shards([w_big[n] for n in names], BF16))
    full = _unpack_gathered(gathered, shard_shapes)
    dc = conv_w.shape[-1]
    taps = conv_w.shape[0] * conv_w.shape[1]
    conv_all = _all_gather("gather_conv", _pad8(conv_w.reshape(taps, dc)))
    conv_full = jnp.moveaxis(conv_all[:, :taps], 0, 1).reshape(conv_w.shape[0], conv_w.shape[1], N_DEV * dc)

    sq_err, grad_x, big, small = _local_grads(x, loss_target, norm_mix, norm_mlp, norm_kv, norm_final, conv_full, *full)
    loss = lax.psum(sq_err * (0.5 / D), ("x", "y", "c"))

    parts = _all_to_all("exchange_grads", _pack_full_grads(big, shard_shapes))
    packed = [_pack_shards([t[n] for n in names], F32) for t in (w_big, m_big, v_big)]
    g_p, d_p, m_p, v_p = _sum8_adamw("adamw", parts, *packed, tr=64)
    grads, deltas, new_m, new_v = (dict(zip(names, _unpack_shards(t, shard_shapes))) for t in (g_p, d_p, m_p, v_p))

    n_gain = 2 * DEPTH + 2
    rows_small = small.shape[0]
    small_all = _all_gather("gather_small_grads", small)

    def small_pack(nm, nl, nk, nf, cw):
        gains = jnp.concatenate([nm, nl, nk.reshape(1, D), nf.reshape(1, D)], axis=0)
        taps_full = lax.dynamic_update_slice(jnp.zeros((taps, D), F32), cw.reshape(taps, dc), (0, me_idx * dc))
        return jnp.concatenate([gains, taps_full], axis=0)

    sp = [small_pack(*t) for t in ((norm_mix, norm_mlp, norm_kv, norm_final, conv_w),
                                   (m_norm_mix, m_norm_mlp, m_norm_kv, m_norm_final, m_conv_w),
                                   (v_norm_mix, v_norm_mlp, v_norm_kv, v_norm_final, v_conv_w))]
    small_out = _sum8_adamw("adamw_small", small_all, *sp, tr=rows_small)

    def small_unpack(t):
        res = dict(norm_mix=t[0:DEPTH], norm_mlp=t[DEPTH:2 * DEPTH], norm_kv=t[2 * DEPTH], norm_final=t[2 * DEPTH + 1])
        res["conv_w"] = lax.dynamic_slice(t[n_gain:], (0, me_idx * dc), (taps, dc)).reshape(conv_w.shape)
        return res

    for dst, t in zip((grads, deltas, new_m, new_v), small_out):
        dst.update(small_unpack(t))

    order = ["norm_mix", "norm_mlp", "w_a_in", "conv_w", "w_a_out", "norm_kv", "w_kv", "w_q", "w_o", "w_up", "w_down",
             "norm_final"]
    return (loss, grad_x, *[grads[n] for n in order], *[deltas[n] for n in order], *[new_m[n] for n in order],
            *[new_v[n] for n in order])
```

```python
import functools

import jax
import jax.numpy as jnp
from jax import lax
from jax.experimental import pallas as pl
from jax.experimental.pallas import tpu as pltpu

F32 = jnp.float32
BF16 = jnp.bfloat16
SDS = jax.ShapeDtypeStruct

EPS = 1e-5
N_A_LAYERS = 2
DEPTH = 4
PATTERNS = ((128, 1), (512, 4), (2048, 16))
N_GROUPS = 3
H_G = 8
HEAD_DIM = 64
QW = H_G * HEAD_DIM
ATT_BLK = 128
ALIBI_MAX_BIAS = 8.0
NEG_INF = -1e30

ADAM_LR = 0.001
ADAM_B1 = 0.9
ADAM_B2 = 0.999
ADAM_EPS = 1e-08
ADAM_WD = 0.01
ADAM_STEP = 10

N_DEV = 8
PACK_COLS = 1024
SUBLANES = 8
V7X_VMEM_LIMIT = 48 * 1024 * 1024
MM_CHUNK = 512

BIG_WEIGHTS = (("w_a_in", 2), ("w_a_out", 1), ("w_kv", 1), ("w_q", 2), ("w_o", 2), ("w_up", 2), ("w_down", 1))


def _call(name, body, grid, ins, outs, scratch=()):
    res = pl.pallas_call(
        body,
        name=name,
        grid=grid,
        in_specs=[s for _, s in ins],
        out_specs=[s for _, s in outs],
        out_shape=[o for o, _ in outs],
        scratch_shapes=list(scratch),
        compiler_params=pltpu.CompilerParams(
            dimension_semantics=("arbitrary",) * len(grid), vmem_limit_bytes=V7X_VMEM_LIMIT),
    )(*[a for a, _ in ins])
    return res


def _rows(a, tm, cb=None, col=0):
    cb = cb or a.shape[1]
    return (a, pl.BlockSpec((tm, cb), lambda i: (i, col)))


def _full(a):
    nd = a.ndim
    return (a, pl.BlockSpec(a.shape, lambda i: (0,) * nd))


def _prev8(a, tm, cb, col):
    return (a, pl.BlockSpec((SUBLANES, cb), lambda i: (jnp.maximum(i * (tm // SUBLANES) - 1, 0), col)))


def _next8(a, tm, cb, col):
    last = a.shape[0] // SUBLANES - 1
    return (a, pl.BlockSpec((SUBLANES, cb), lambda i: (jnp.minimum((i + 1) * (tm // SUBLANES), last), col)))


def _rows2(a, tt, cb=None, colfn=None):
    cb = cb or a.shape[1]
    colfn = colfn or (lambda s: 0)
    return (a, pl.BlockSpec((tt, cb), lambda s, t: (t, colfn(s))))


def _full2(a):
    nd = a.ndim
    return (a, pl.BlockSpec(a.shape, lambda s, t: (0,) * nd))


def _prev8_2(a, tt, cb, col):
    return (a, pl.BlockSpec((SUBLANES, cb), lambda s, t: (jnp.maximum(t * (tt // SUBLANES) - 1, 0), col)))


def _out_rows(T, n, dtype, tm):
    return (SDS((T, n), dtype), pl.BlockSpec((tm, n), lambda i: (i, 0)))


def _out_acc8(d):
    return (SDS((SUBLANES, d), F32), pl.BlockSpec((SUBLANES, d), lambda i: (0, 0)))


def _rstd(x):
    return lax.rsqrt(jnp.mean(x * x, axis=-1, keepdims=True) + EPS)


def _normed(h_ref, g_ref):
    x = h_ref[...]
    return x * _rstd(x) * g_ref[...]


def _acc8(ref, val, i, n):
    part = val.reshape(-1, SUBLANES, val.shape[-1]).sum(axis=0)

    @pl.when(i == 0)
    def _():
        ref[...] = part

    @pl.when(i > 0)
    def _():
        ref[...] += part

    @pl.when(i == n - 1)
    def _():
        ref[...] = jnp.broadcast_to(jnp.sum(ref[...], axis=0, keepdims=True), ref.shape)


def _gate(b, c, u, c_halo, u_halo, cw_ref, first):
    cu = c * u
    halo = jnp.where(first, 0.0, c_halo * u_halo)
    rows = lax.broadcasted_iota(jnp.int32, cu.shape, 0)
    h1 = halo[SUBLANES - 1:SUBLANES, :]
    h2 = halo[SUBLANES - 2:SUBLANES - 1, :]
    cu1 = jnp.where(rows == 0, h1, pltpu.roll(cu, 1, 0))
    cu2 = jnp.where(rows == 0, h2, jnp.where(rows == 1, h1, pltpu.roll(cu, 2, 0)))
    conv = cw_ref[0:1, :] * cu + cw_ref[1:2, :] * cu1 + cw_ref[2:3, :] * cu2
    return b * conv, conv, cu, cu1, cu2


def _relu2(a):
    r = jnp.maximum(a, 0.0)
    return r * r


def _dot(a, b):
    return jnp.dot(a, b, preferred_element_type=F32)


def _dot_nt(a, b):
    return lax.dot_general(a, b, (((1,), (1,)), ((), ())), preferred_element_type=F32)


def _dot_tn(a, b):
    return lax.dot_general(a, b, (((0,), (0,)), ((), ())), preferred_element_type=F32)


def _chunks(n):
    c = min(MM_CHUNK, n)
    assert n % c == 0, n
    return [(k * c, (k + 1) * c) for k in range(n // c)]


def _norm_mm(name, h, g, w, tm=256):
    T, _ = h.shape
    N = w.shape[1]

    def body(h_ref, g_ref, w_ref, o_ref):
        a = _normed(h_ref, g_ref).astype(BF16)
        for lo, hi in _chunks(N):
            o_ref[:, lo:hi] = _dot(a, w_ref[:, lo:hi])

    return _call(name, body, (T // tm,), [_rows(h, tm), _full(g), _full(w)], [_out_rows(T, N, F32, tm)])[0]


def _gate_mm_res(name, bcu, cw, w, h, seq, tm=256):
    T, D = h.shape

    def body(b_ref, c_ref, u_ref, ch_ref, uh_ref, cw_ref, w_ref, h_ref, o_ref):
        first = (pl.program_id(0) * tm) % seq == 0
        gated = _gate(b_ref[...], c_ref[...], u_ref[...], ch_ref[...], uh_ref[...], cw_ref, first)[0].astype(BF16)
        for lo, hi in _chunks(D):
            o_ref[:, lo:hi] = h_ref[:, lo:hi] + _dot(gated, w_ref[:, lo:hi])

    ins = [_rows(bcu, tm, D, 0), _rows(bcu, tm, D, 1), _rows(bcu, tm, D, 2), _prev8(bcu, tm, D, 1),
           _prev8(bcu, tm, D, 2), _full(cw), _full(w), _rows(h, tm)]
    return _call(name, body, (T // tm,), ins, [_out_rows(T, D, F32, tm)])[0]


def _relu2_mm_res(name, a, w, h, tm=256):
    T, D = h.shape
    K = a.shape[1]

    def body(a_ref, w_ref, h_ref, o_ref, acc_ref):
        for n, (lo, hi) in enumerate(_chunks(K)):
            d = _dot(_relu2(a_ref[:, lo:hi]).astype(BF16), w_ref[lo:hi, :])
            if n == 0:
                acc_ref[...] = d
            else:
                acc_ref[...] += d
        o_ref[...] = h_ref[...] + acc_ref[...]

    return _call(name, body, (T // tm,), [_rows(a, tm), _full(w), _rows(h, tm)], [_out_rows(T, D, F32, tm)],
                 scratch=[pltpu.VMEM((tm, D), F32)])[0]


def _mm_res(name, a, w, h, tm=256):
    T, D = h.shape

    def body(a_ref, w_ref, h_ref, o_ref):
        av = a_ref[...].astype(BF16)
        for lo, hi in _chunks(D):
            o_ref[:, lo:hi] = h_ref[:, lo:hi] + _dot(av, w_ref[:, lo:hi])

    return _call(name, body, (T // tm,), [_rows(a, tm), _full(w), _rows(h, tm)], [_out_rows(T, D, F32, tm)])[0]


def _nt_relu2_bwd(name, dh, w, a, tm=256):
    T, _ = dh.shape
    K = w.shape[0]

    def body(dh_ref, w_ref, a_ref, o_ref):
        d = dh_ref[...].astype(BF16)
        for lo, hi in _chunks(K):
            dr = _dot_nt(d, w_ref[lo:hi, :])
            o_ref[:, lo:hi] = (dr * (2.0 * jnp.maximum(a_ref[:, lo:hi], 0.0))).astype(BF16)

    return _call(name, body, (T // tm,), [_rows(dh, tm), _full(w), _rows(a, tm)], [_out_rows(T, K, BF16, tm)])[0]


def _concat_bf16(*refs):
    vals = [r[...].astype(BF16) for r in refs]
    return vals[0] if len(vals) == 1 else jnp.concatenate(vals, axis=1)


def _nt_plain(name, dy, w, tm=256):
    T, N = dy.shape
    K = w.shape[0]

    def body(dy_ref, w_ref, o_ref, acc_ref):
        for n, (lo, hi) in enumerate(_chunks(N)):
            d = _dot_nt(dy_ref[:, lo:hi].astype(BF16), w_ref[:, lo:hi])
            if n == 0:
                acc_ref[...] = d
            else:
                acc_ref[...] += d
        o_ref[...] = acc_ref[...]

    return _call(name, body, (T // tm,), [_rows(dy, tm), _full(w)], [_out_rows(T, K, F32, tm)],
                 scratch=[pltpu.VMEM((tm, K), F32)])[0]


def _nt_norm_bwd(name, dys, w, h, g, dh_in, tm=256):
    T, D = h.shape
    N = w.shape[1]
    n_steps = T // tm
    n_dy = len(dys)

    def body(*refs):
        dy_refs = refs[:n_dy]
        w_ref, h_ref, g_ref, dhin_ref, o_ref, dg_ref, acc_ref = refs[n_dy:]
        i = pl.program_id(0)
        dy = _concat_bf16(*dy_refs)
        for n, (lo, hi) in enumerate(_chunks(N)):
            d = _dot_nt(dy[:, lo:hi], w_ref[:, lo:hi])
            if n == 0:
                acc_ref[...] = d
            else:
                acc_ref[...] += d
        dn = acc_ref[...]
        x = h_ref[...]
        rstd = _rstd(x)
        xhat = x * rstd
        dxhat = dn * g_ref[...]
        dx = rstd * (dxhat - xhat * jnp.mean(dxhat * xhat, axis=-1, keepdims=True))
        o_ref[...] = dhin_ref[...] + dx
        _acc8(dg_ref, dn * xhat, i, n_steps)

    ins = [_rows(d, tm) for d in dys] + [_full(w), _rows(h, tm), _full(g), _rows(dh_in, tm)]
    outs = [_out_rows(T, D, F32, tm), _out_acc8(D)]
    dh, dg = _call(name, body, (n_steps,), ins, outs, scratch=[pltpu.VMEM((tm, D), F32)])
    return dh, dg[0:1]


def _tn(name, a_ins, a_fn, y_ins, y_fn, K, N, T, tt, split=None):
    kind, parts = split or ("n", 1)
    kb, nb = (K // parts, N) if kind == "k" else (K, N // parts)
    n_steps = T // tt
    n_a = len(a_ins)
    n_y = len(y_ins)

    def body(*refs):
        a_refs = refs[:n_a]
        y_refs = refs[n_a:n_a + n_y]
        o_ref, acc_ref = refs[n_a + n_y:]
        t = pl.program_id(1)
        a = a_fn(*a_refs).astype(BF16)
        y = y_fn(*y_refs).astype(BF16)
        for lo, hi in _chunks(nb):
            d = _dot_tn(a, y[:, lo:hi])

            @pl.when(t == 0)
            def _():
                acc_ref[:, lo:hi] = d

            @pl.when(t > 0)
            def _():
                acc_ref[:, lo:hi] += d

        @pl.when(t == n_steps - 1)
        def _():
            o_ref[...] = acc_ref[...].astype(BF16)

    out_spec = pl.BlockSpec((kb, nb), (lambda s, t: (s, 0)) if kind == "k" else (lambda s, t: (0, s)))
    return _call(name, body, (parts, n_steps), list(a_ins) + list(y_ins), [(SDS((K, N), BF16), out_spec)],
                 scratch=[pltpu.VMEM((kb, nb), F32)])[0]


def _val(ref):
    return ref[...]


def _concat_f32(*refs):
    vals = [r[...] for r in refs]
    return vals[0] if len(vals) == 1 else jnp.concatenate(vals, axis=1)


ATT_TILE_ROWS = 512
HEAD_PAIRS = H_G // 2
ATT_SCALE = HEAD_DIM ** -0.5


def _slope(h):
    return 2.0 ** (-ALIBI_MAX_BIAS * (h + 1) / H_G)


def _att_geom(T, bl, g):
    dil = PATTERNS[g][1]
    sub = ATT_BLK * dil
    nsub = max(1, ATT_TILE_ROWS // sub)
    rows = sub * nsub
    return dil, sub, nsub, rows, T // bl // rows


def _att_specs(T, bl, g):
    _, sub, nsub, rows, nt = _att_geom(T, bl, g)
    last_sub = T // sub - 1
    tile = lambda col: pl.BlockSpec((rows, 128), lambda b, i, hp: (b * nt + i, col(hp)))
    prev = lambda col: pl.BlockSpec((sub, 128), lambda b, i, hp: (jnp.maximum((b * nt + i) * nsub - 1, 0), col(hp)))
    nxt = lambda col: pl.BlockSpec((sub, 128), lambda b, i, hp: (jnp.minimum((b * nt + i + 1) * nsub, last_sub), col(hp)))
    return tile, prev, nxt


def _sub_rows(j, r, dil):
    start = j * ATT_BLK * dil + r
    return pl.ds(start, ATT_BLK, stride=dil) if dil > 1 else pl.ds(start, ATT_BLK)


def _att_consts(hp, dil):
    h0 = lax.broadcasted_iota(jnp.int32, (ATT_BLK, 128), 1) < HEAD_DIM
    a = lax.broadcasted_iota(jnp.int32, (ATT_BLK, ATT_BLK), 0)
    c = lax.broadcasted_iota(jnp.int32, (ATT_BLK, ATT_BLK), 1)
    dist_p = ((ATT_BLK + a - c) * dil).astype(F32)
    dist_c = ((a - c) * dil).astype(F32)
    bias_p, bias_c = [], []
    for h in range(2):
        slope = jnp.float32(_slope(2 * (HEAD_PAIRS - 1) + h))
        for p in range(HEAD_PAIRS - 2, -1, -1):
            slope = jnp.where(hp == p, jnp.float32(_slope(2 * p + h)), slope)
        bias_p.append(jnp.where(c >= a, -slope * dist_p, NEG_INF))
        bias_c.append(jnp.where(c <= a, -slope * dist_c, NEG_INF))
    return h0, bias_p, bias_c


def _split_heads(x, h0):
    return [jnp.where(h0, x, 0.0).astype(BF16), jnp.where(h0, 0.0, x).astype(BF16)]


def _head_cols(x):
    return [x[:, 0:1], x[:, HEAD_DIM:HEAD_DIM + 1]]


def _head_sums(x, h0):
    return [jnp.sum(jnp.where(h0, x, 0.0), axis=-1, keepdims=True), jnp.sum(jnp.where(h0, 0.0, x), axis=-1, keepdims=True)]


def _attn_fwd(name, q, kv, g, bl):
    T = q.shape[0]
    dil, _, nsub, _, _ = _att_geom(T, bl, g)
    tile, prev, _ = _att_specs(T, bl, g)

    def body(q_ref, kp_ref, kc_ref, vp_ref, vc_ref, o_ref, lse_ref):
        first = pl.program_id(1) == 0
        h0, bias_p, bias_c = _att_consts(pl.program_id(2), dil)
        bias_first = [jnp.where(first, NEG_INF, b) for b in bias_p]
        for j in range(nsub):
            for r in range(dil):
                cur = _sub_rows(j, r, dil)
                if j == 0:
                    before = _sub_rows(0, r, dil)
                    kp, vp, bp = kp_ref[before, :], vp_ref[before, :], bias_first
                else:
                    before = _sub_rows(j - 1, r, dil)
                    kp, vp, bp = kc_ref[before, :], vc_ref[before, :], bias_p
                kp, vp = kp.astype(BF16), vp.astype(BF16)
                kc, vc = kc_ref[cur, :].astype(BF16), vc_ref[cur, :].astype(BF16)
                qh = _split_heads(q_ref[cur, :] * ATT_SCALE, h0)
                sp = [_dot_nt(qh[h], kp) + bp[h] for h in range(2)]
                sc = [_dot_nt(qh[h], kc) + bias_c[h] for h in range(2)]
                mx = [jnp.maximum(jnp.max(sp[h], axis=-1, keepdims=True), jnp.max(sc[h], axis=-1, keepdims=True))
                      for h in range(2)]
                ep = [jnp.exp(sp[h] - mx[h]) for h in range(2)]
                ec = [jnp.exp(sc[h] - mx[h]) for h in range(2)]
                den = [jnp.sum(ep[h], axis=-1, keepdims=True) + jnp.sum(ec[h], axis=-1, keepdims=True) for h in range(2)]
                acc = [_dot(ep[h].astype(BF16), vp) + _dot(ec[h].astype(BF16), vc) for h in range(2)]
                o_ref[cur, :] = jnp.where(h0, acc[0] / den[0], acc[1] / den[1])
                lse_ref[cur, :] = jnp.where(h0, mx[0] + jnp.log(den[0]), mx[1] + jnp.log(den[1]))

    ins = [(q, tile(lambda hp: 4 * g + hp)), (kv, prev(lambda hp: 8 * g + hp)), (kv, tile(lambda hp: 8 * g + hp)),
           (kv, prev(lambda hp: 8 * g + 4 + hp)), (kv, tile(lambda hp: 8 * g + 4 + hp))]
    out = (SDS((T, QW), F32), tile(lambda hp: hp))
    _, _, _, _, nt = _att_geom(T, bl, g)
    return _call(name, body, (bl, nt, HEAD_PAIRS), ins, [out, out])


def _combine(name, os_, lses, tm=512):
    T = os_[0].shape[0]

    def body(o0, o1, o2, l0, l1, l2, o_ref, lse_ref):
        ls = [l0[...], l1[...], l2[...]]
        mx = jnp.maximum(jnp.maximum(ls[0], ls[1]), ls[2])
        es = [jnp.exp(l - mx) for l in ls]
        den = es[0] + es[1] + es[2]
        o_ref[...] = (es[0] * o0[...] + es[1] * o1[...] + es[2] * o2[...]) / den
        lse_ref[...] = mx + jnp.log(den)

    ins = [_rows(t, tm) for t in list(os_) + list(lses)]
    return _call(name, body, (T // tm,), ins, [_out_rows(T, QW, F32, tm), _out_rows(T, QW, F32, tm)])


def _attn_bwd_dq(name, q, kv, do, o, lse, g, bl):
    T = q.shape[0]
    dil, _, nsub, _, nt = _att_geom(T, bl, g)
    tile, prev, _ = _att_specs(T, bl, g)

    def body(q_ref, kp_ref, kc_ref, vp_ref, vc_ref, do_ref, o_ref, lse_ref, dq_ref):
        first = pl.program_id(1) == 0
        h0, bias_p, bias_c = _att_consts(pl.program_id(2), dil)
        bias_first = [jnp.where(first, NEG_INF, b) for b in bias_p]
        for j in range(nsub):
            for r in range(dil):
                cur = _sub_rows(j, r, dil)
                if j == 0:
                    before = _sub_rows(0, r, dil)
                    kp, vp, bp = kp_ref[before, :], vp_ref[before, :], bias_first
                else:
                    before = _sub_rows(j - 1, r, dil)
                    kp, vp, bp = kc_ref[before, :], vc_ref[before, :], bias_p
                kp, vp = kp.astype(BF16), vp.astype(BF16)
                kc, vc = kc_ref[cur, :].astype(BF16), vc_ref[cur, :].astype(BF16)
                do2 = do_ref[cur, :]
                qh = _split_heads(q_ref[cur, :] * ATT_SCALE, h0)
                dob = _split_heads(do2, h0)
                lse_h = _head_cols(lse_ref[cur, :])
                delta = _head_sums(do2 * o_ref[cur, :], h0)
                pp = [jnp.exp(_dot_nt(qh[h], kp) + bp[h] - lse_h[h]) for h in range(2)]
                pc = [jnp.exp(_dot_nt(qh[h], kc) + bias_c[h] - lse_h[h]) for h in range(2)]
                dsp = [(pp[h] * (_dot_nt(dob[h], vp) - delta[h])).astype(BF16) for h in range(2)]
                dsc = [(pc[h] * (_dot_nt(dob[h], vc) - delta[h])).astype(BF16) for h in range(2)]
                dqh = [_dot(dsp[h], kp) + _dot(dsc[h], kc) for h in range(2)]
                dq_ref[cur, :] = jnp.where(h0, dqh[0], dqh[1]) * ATT_SCALE

    own = lambda hp: hp
    ins = [(q, tile(lambda hp: 4 * g + hp)), (kv, prev(lambda hp: 8 * g + hp)), (kv, tile(lambda hp: 8 * g + hp)),
           (kv, prev(lambda hp: 8 * g + 4 + hp)), (kv, tile(lambda hp: 8 * g + 4 + hp)),
           (do, tile(own)), (o, tile(own)), (lse, tile(own))]
    return _call(name, body, (bl, nt, HEAD_PAIRS), ins, [(SDS((T, QW), F32), tile(own))])[0]


def _attn_bwd_dkv(name, q, kv, do, o, lse, g, bl, prev=None):
    T = q.shape[0]
    dil, _, nsub, _, nt = _att_geom(T, bl, g)
    tile, _, nxt = _att_specs(T, bl, g)
    has_prev = prev is not None

    def body(*refs):
        k_ref, v_ref, q_ref, qn_ref, do_ref, don_ref, o_ref, on_ref, l_ref, ln_ref = refs[:10]
        rest = refs[10:]
        if has_prev:
            dkp_ref, dvp_ref, dk_ref, dv_ref = rest
        else:
            dk_ref, dv_ref = rest
        last = pl.program_id(1) == nt - 1
        h0, bias_p, bias_c = _att_consts(pl.program_id(2), dil)
        bias_last = [jnp.where(last, NEG_INF, b) for b in bias_p]
        for j in range(nsub):
            for r in range(dil):
                cur = _sub_rows(j, r, dil)
                kb, vb = k_ref[cur, :].astype(BF16), v_ref[cur, :].astype(BF16)
                sets = [(q_ref, do_ref, o_ref, l_ref, cur, bias_c)]
                if j < nsub - 1:
                    sets.append((q_ref, do_ref, o_ref, l_ref, _sub_rows(j + 1, r, dil), bias_p))
                else:
                    sets.append((qn_ref, don_ref, on_ref, ln_ref, _sub_rows(0, r, dil), bias_last))
                dk = [None, None]
                dv = [None, None]
                for qr, dor, orr, lr, rows, bias in sets:
                    qs = qr[rows, :] * ATT_SCALE
                    do2 = dor[rows, :]
                    qh = _split_heads(qs, h0)
                    dob = _split_heads(do2, h0)
                    qsb, do2b = qs.astype(BF16), do2.astype(BF16)
                    lse_h = _head_cols(lr[rows, :])
                    delta = _head_sums(do2 * orr[rows, :], h0)
                    p = [jnp.exp(_dot_nt(qh[h], kb) + bias[h] - lse_h[h]) for h in range(2)]
                    ds = [(p[h] * (_dot_nt(dob[h], vb) - delta[h])).astype(BF16) for h in range(2)]
                    for h in range(2):
                        dvh = _dot_tn(p[h].astype(BF16), do2b)
                        dkh = _dot_tn(ds[h], qsb)
                        dv[h] = dvh if dv[h] is None else dv[h] + dvh
                        dk[h] = dkh if dk[h] is None else dk[h] + dkh
                dk2 = jnp.where(h0, dk[0], dk[1])
                dv2 = jnp.where(h0, dv[0], dv[1])
                if has_prev:
                    dk2 = dk2 + dkp_ref[cur, :]
                    dv2 = dv2 + dvp_ref[cur, :]
                dk_ref[cur, :] = dk2
                dv_ref[cur, :] = dv2

    own = lambda hp: hp
    qcol = lambda hp: 4 * g + hp
    ins = [(kv, tile(lambda hp: 8 * g + hp)), (kv, tile(lambda hp: 8 * g + 4 + hp)), (q, tile(qcol)), (q, nxt(qcol)),
           (do, tile(own)), (do, nxt(own)), (o, tile(own)), (o, nxt(own)), (lse, tile(own)), (lse, nxt(own))]
    if has_prev:
        ins += [(prev[0], tile(own)), (prev[1], tile(own))]
    out = (SDS((T, QW), F32), tile(own))
    return _call(name, body, (bl, nt, HEAD_PAIRS), ins, [out, out])


def _final_loss(name, h, tgt, g, tm=256):
    T, D = h.shape
    n_steps = T // tm

    def body(h_ref, t_ref, g_ref, dh_ref, loss_ref, dg_ref, sq_ref):
        i = pl.program_id(0)
        x = h_ref[...]
        rstd = _rstd(x)
        xhat = x * rstd
        err = xhat * g_ref[...] - t_ref[...]
        _acc8(sq_ref, err * err, i, n_steps)
        dy = err * (1.0 / D)
        dxhat = dy * g_ref[...]
        dh_ref[...] = rstd * (dxhat - xhat * jnp.mean(dxhat * xhat, axis=-1, keepdims=True))
        _acc8(dg_ref, dy * xhat, i, n_steps)

        @pl.when(i == n_steps - 1)
        def _():
            loss_ref[...] = jnp.full(loss_ref.shape, jnp.sum(sq_ref[0:1, :]), F32)

    outs = [_out_rows(T, D, F32, tm), (SDS((SUBLANES, 128), F32), pl.BlockSpec((SUBLANES, 128), lambda i: (0, 0))),
            _out_acc8(D)]
    dh, loss, dg = _call(name, body, (n_steps,), [_rows(h, tm), _rows(tgt, tm), _full(g)], outs,
                         scratch=[pltpu.VMEM((SUBLANES, D), F32)])
    return dh, loss[0, 0], dg[0:1]


def _conv_bwd(name, bcu, dgated, cw, seq, tm=256):
    T, D = dgated.shape
    n_steps = T // tm

    def body(b_ref, c_ref, u_ref, ch_ref, uh_ref, dg_ref, dgn_ref, bn_ref, cw_ref, o_ref, t0_ref, t1_ref, t2_ref):
        i = pl.program_id(0)
        first = (i * tm) % seq == 0
        last = ((i + 1) * tm) % seq == 0
        b = b_ref[...]
        c = c_ref[...]
        u = u_ref[...]
        _, conv, cu, cu1, cu2 = _gate(b, c, u, ch_ref[...], uh_ref[...], cw_ref, first)
        dgat = dg_ref[...]
        dconv = dgat * b
        nxt = jnp.where(last, 0.0, dgn_ref[...] * bn_ref[...])
        rows = lax.broadcasted_iota(jnp.int32, dconv.shape, 0)
        n1 = nxt[0:1, :]
        n2 = nxt[1:2, :]
        dc1 = jnp.where(rows == tm - 1, n1, pltpu.roll(dconv, tm - 1, 0))
        dc2 = jnp.where(rows == tm - 1, n2, jnp.where(rows == tm - 2, n1, pltpu.roll(dconv, tm - 2, 0)))
        dcu = cw_ref[0:1, :] * dconv + cw_ref[1:2, :] * dc1 + cw_ref[2:3, :] * dc2
        o_ref[:, 0:D] = (dgat * conv).astype(BF16)
        o_ref[:, D:2 * D] = (dcu * u).astype(BF16)
        o_ref[:, 2 * D:3 * D] = (dcu * c).astype(BF16)
        _acc8(t0_ref, dconv * cu, i, n_steps)
        _acc8(t1_ref, dconv * cu1, i, n_steps)
        _acc8(t2_ref, dconv * cu2, i, n_steps)

    ins = [_rows(bcu, tm, D, 0), _rows(bcu, tm, D, 1), _rows(bcu, tm, D, 2), _prev8(bcu, tm, D, 1), _prev8(bcu, tm, D, 2),
           _rows(dgated, tm), _next8(dgated, tm, D, 0), _next8(bcu, tm, D, 0), _full(cw)]
    outs = [_out_rows(T, 3 * D, BF16, tm), _out_acc8(D), _out_acc8(D), _out_acc8(D)]
    dbcu, t0, t1, t2 = _call(name, body, (n_steps,), ins, outs)
    return dbcu, jnp.concatenate([t0[0:1], t1[0:1], t2[0:1]], axis=0)


def _sum8_adamw(name, parts, w, m, v, tr):
    R, C = w.shape
    b1c = 1.0 - ADAM_B1 ** ADAM_STEP
    b2c = 1.0 - ADAM_B2 ** ADAM_STEP

    def body(p_ref, w_ref, m_ref, v_ref, g_ref, d_ref, nm_ref, nv_ref):
        g = p_ref[0].astype(F32)
        for j in range(1, N_DEV):
            g = g + p_ref[j].astype(F32)
        nm = ADAM_B1 * m_ref[...] + (1.0 - ADAM_B1) * g
        nv = ADAM_B2 * v_ref[...] + (1.0 - ADAM_B2) * (g * g)
        m_hat = nm / b1c
        v_hat = nv / b2c
        g_ref[...] = g
        d_ref[...] = -ADAM_LR * (m_hat / (jnp.sqrt(v_hat) + ADAM_EPS) + ADAM_WD * w_ref[...])
        nm_ref[...] = nm
        nv_ref[...] = nv

    ins = [(parts, pl.BlockSpec((N_DEV, tr, C), lambda i: (0, i, 0))), _rows(w, tr), _rows(m, tr), _rows(v, tr)]
    outs = [_out_rows(R, C, F32, tr)] * 4
    return _call(name, body, (R // tr,), ins, outs)


def _mesh_pos():
    return lax.axis_index("x"), lax.axis_index("y"), lax.axis_index("c")


def _flip(v, bit):
    return 1 - v if bit else v


def _all_gather(name, shard):
    R, C = shard.shape

    def body(x_ref, out_ref, send_sems, recv_sems, local_sem):
        x, y, c = _mesh_pos()
        me, sibling = (x, y, c), (x, y, 1 - c)
        chips = [(1 - x, y), (x, 1 - y), (1 - x, 1 - y)]

        def slot(px, py, pc):
            return out_ref.at[4 * px + 2 * py + pc]

        def copy(k, block, to, src=None):
            return pltpu.make_async_remote_copy(
                src_ref=slot(*block) if src is None else src, dst_ref=slot(*block),
                send_sem=send_sems.at[k], recv_sem=recv_sems.at[k], device_id=to, device_id_type=pl.DeviceIdType.MESH)

        mine = pltpu.make_async_copy(x_ref, slot(*me), local_sem)
        mine.start()
        first = [copy(0, me, sibling, src=x_ref)]
        first += [copy(1 + j, me, (*chip, c), src=x_ref) for j, chip in enumerate(chips)]
        for cp in first:
            cp.start()
        passed = [copy(4 + j, (*chip, c), sibling) for j, chip in enumerate(chips)]
        for j, chip in enumerate(chips):
            copy(1 + j, (*chip, c), me).wait_recv()
            passed[j].start()
        copy(0, sibling, me).wait_recv()
        for j, chip in enumerate(chips):
            copy(4 + j, (*chip, 1 - c), me).wait_recv()
        for cp in first + passed:
            cp.wait_send()
        mine.wait()

    return pl.pallas_call(
        body, name=name,
        out_shape=SDS((N_DEV, R, C), shard.dtype),
        in_specs=[pl.BlockSpec(memory_space=pl.ANY)],
        out_specs=pl.BlockSpec(memory_space=pl.ANY),
        scratch_shapes=[pltpu.SemaphoreType.DMA((7,)), pltpu.SemaphoreType.DMA((7,)), pltpu.SemaphoreType.DMA(())],
    )(shard)


def _all_to_all(name, blocks):
    _, R, C = blocks.shape

    def body(g_ref, out_ref, send_sems, recv_sems, local_sem):
        x, y, c = _mesh_pos()
        me_idx = 4 * x + 2 * y + c
        mine = pltpu.make_async_copy(g_ref.at[me_idx], out_ref.at[me_idx], local_sem)
        mine.start()
        sends, recvs = [], []
        for k in range(1, N_DEV):
            px, py, pc = _flip(x, k & 4), _flip(y, k & 2), _flip(c, k & 1)
            peer_idx = 4 * px + 2 * py + pc
            sends.append(pltpu.make_async_remote_copy(
                src_ref=g_ref.at[peer_idx], dst_ref=out_ref.at[me_idx], send_sem=send_sems.at[k - 1],
                recv_sem=recv_sems.at[k - 1], device_id=(px, py, pc), device_id_type=pl.DeviceIdType.MESH))
            recvs.append(pltpu.make_async_remote_copy(
                src_ref=g_ref.at[peer_idx], dst_ref=out_ref.at[peer_idx], send_sem=send_sems.at[k - 1],
                recv_sem=recv_sems.at[k - 1], device_id=(px, py, pc), device_id_type=pl.DeviceIdType.MESH))
        for cp in sends:
            cp.start()
        for cp in recvs:
            cp.wait_recv()
        for cp in sends:
            cp.wait_send()
        mine.wait()

    return pl.pallas_call(
        body, name=name,
        out_shape=SDS((N_DEV, R, C), blocks.dtype),
        in_specs=[pl.BlockSpec(memory_space=pl.ANY)],
        out_specs=pl.BlockSpec(memory_space=pl.ANY),
        scratch_shapes=[pltpu.SemaphoreType.DMA((7,)), pltpu.SemaphoreType.DMA((7,)), pltpu.SemaphoreType.DMA(())],
    )(blocks)


def _pack_rows(shape):
    n = 1
    for s in shape:
        n *= s
    assert n % PACK_COLS == 0, shape
    return n // PACK_COLS


def _pack_shards(tensors, dtype):
    return jnp.concatenate([t.astype(dtype).reshape(-1, PACK_COLS) for t in tensors], axis=0)


def _unpack_gathered(gathered, shard_shapes):
    out, r0 = [], 0
    for (_, ax), shp in zip(BIG_WEIGHTS, shard_shapes):
        r1 = r0 + _pack_rows(shp)
        seg = jnp.moveaxis(gathered[:, r0:r1].reshape((N_DEV,) + tuple(shp)), 0, ax)
        full = list(shp)
        full[ax] *= N_DEV
        out.append(seg.reshape(full))
        r0 = r1
    return out


def _pack_full_grads(grads, shard_shapes):
    segs = []
    for (_, ax), shp, gfull in zip(BIG_WEIGHTS, shard_shapes, grads):
        split = list(shp)
        split.insert(ax, N_DEV)
        seg = jnp.moveaxis(gfull.reshape(split), ax, 0)
        segs.append(seg.reshape(N_DEV, _pack_rows(shp), PACK_COLS))
    return jnp.concatenate(segs, axis=1)


def _unpack_shards(packed, shard_shapes):
    out, r0 = [], 0
    for shp in shard_shapes:
        r1 = r0 + _pack_rows(shp)
        out.append(packed[r0:r1].reshape(shp))
        r0 = r1
    return out


def _pad8(t):
    return jnp.pad(t, ((0, SUBLANES - t.shape[0]), (0, 0)))


def _local_grads(x, tgt, norm_mix, norm_mlp, norm_kv, norm_final, conv_w, w_a_in, w_a_out, w_kv, w_q, w_o, w_up, w_down):
    bl, seq, D = x.shape
    T = bl * seq
    h = x.reshape(T, D)
    tgt = tgt.reshape(T, D)
    row = lambda t, l: t[l:l + 1]
    saved = []
    kv = h_kv = None
    for l in range(DEPTH):
        if l < N_A_LAYERS:
            bcu = _norm_mm(f"l{l}_in", h, row(norm_mix, l), w_a_in[l])
            h2 = _gate_mm_res(f"l{l}_conv_out", bcu, _pad8(conv_w[l]), w_a_out[l], h, seq)
            saved.append((h, bcu, h2))
        else:
            i = l - N_A_LAYERS
            if l == N_A_LAYERS:
                h_kv = h
                kv = _norm_mm("kv", h, norm_kv.reshape(1, D), w_kv)
            q = _norm_mm(f"l{l}_q", h, row(norm_mix, l), w_q[i])
            per_group = [_attn_fwd(f"l{l}_att{g}", q, kv, g, bl) for g in range(N_GROUPS)]
            o, lse = _combine(f"l{l}_combine", [p[0] for p in per_group], [p[1] for p in per_group])
            h2 = _mm_res(f"l{l}_att_out", o, w_o[i], h)
            saved.append((h, q, o, lse, h2))
        a = _norm_mm(f"l{l}_up", h2, row(norm_mlp, l), w_up[l])
        h = _relu2_mm_res(f"l{l}_down", a, w_down[l], h2)
        saved[-1] = saved[-1] + (a,)

    dh, sq_err, d_norm_final = _final_loss("loss", h, tgt, norm_final.reshape(1, D))

    d_norm_mix = [None] * DEPTH
    d_norm_mlp = [None] * DEPTH
    d_in, d_out, d_q, d_o, d_up, d_down = ([None] * 2 for _ in range(6))
    d_up = [None] * DEPTH
    d_down = [None] * DEPTH
    d_conv = [None] * N_A_LAYERS
    d_kv = d_norm_kv = None
    dkv_acc = [None] * N_GROUPS
    FF = w_up.shape[2]
    tt = 512
    for l in reversed(range(DEPTH)):
        a, h2 = saved[l][-1], saved[l][-2]
        h_in = saved[l][0]
        g_mlp = row(norm_mlp, l)
        g_mix = row(norm_mix, l)
        da = _nt_relu2_bwd(f"l{l}_down_bwd", dh, w_down[l], a)
        d_down[l] = _tn(f"l{l}_dw_down", [_rows2(a, tt, FF // 2, lambda s: s)], lambda r: _relu2(r[...]),
                        [_rows2(dh, tt)], _val, FF, D, T, tt, split=("k", 2))
        d_up[l] = _tn(f"l{l}_dw_up", [_rows2(h2, tt), _full2(g_mlp)], _normed,
                      [_rows2(da, tt, FF // 2, lambda s: s)], _val, D, FF, T, tt, split=("n", 2))
        dh2, d_norm_mlp[l] = _nt_norm_bwd(f"l{l}_up_bwd", [da], w_up[l], h2, g_mlp, dh)
        if l >= N_A_LAYERS:
            i = l - N_A_LAYERS
            _, q, o, lse, _, _ = saved[l]
            do = _nt_plain(f"l{l}_att_out_bwd", dh2, w_o[i])
            d_o[i] = _tn(f"l{l}_dw_o", [_rows2(o, tt)], _val, [_rows2(dh2, tt)], _val, QW, D, T, tt)
            dqs = []
            for g in range(N_GROUPS):
                dqs.append(_attn_bwd_dq(f"l{l}_att{g}_dq", q, kv, do, o, lse, g, bl))
                dkv_acc[g] = _attn_bwd_dkv(f"l{l}_att{g}_dkv", q, kv, do, o, lse, g, bl, prev=dkv_acc[g])
            d_q[i] = _tn(f"l{l}_dw_q", [_rows2(h_in, tt), _full2(g_mix)], _normed,
                         [_rows2(t, tt) for t in dqs], _concat_f32, D, N_GROUPS * QW, T, tt)
            dh, d_norm_mix[l] = _nt_norm_bwd(f"l{l}_q_bwd", dqs, w_q[i], h_in, g_mix, dh2)
            if l == N_A_LAYERS:
                dkvs = [t for pair in dkv_acc for t in pair]
                g_kv = norm_kv.reshape(1, D)
                d_kv = _tn("dw_kv", [_rows2(h_kv, 256), _full2(g_kv)], _normed,
                           [_rows2(t, 256) for t in dkvs], _concat_f32, D, 2 * N_GROUPS * QW, T, 256)
                dh, d_norm_kv = _nt_norm_bwd("kv_bwd", dkvs, w_kv, h_kv, g_kv, dh)
        else:
            _, bcu, _, _ = saved[l]
            cw = _pad8(conv_w[l])
            dgated = _nt_plain(f"l{l}_conv_out_bwd", dh2, w_a_out[l])

            def gated_tile(b_ref, c_ref, u_ref, ch_ref, uh_ref, cw_ref):
                first = (pl.program_id(1) * tt) % seq == 0
                return _gate(b_ref[...], c_ref[...], u_ref[...], ch_ref[...], uh_ref[...], cw_ref, first)[0]

            d_out[l] = _tn(f"l{l}_dw_conv_out",
                           [_rows2(bcu, tt, D, lambda s: 0), _rows2(bcu, tt, D, lambda s: 1), _rows2(bcu, tt, D, lambda s: 2),
                            _prev8_2(bcu, tt, D, 1), _prev8_2(bcu, tt, D, 2), _full2(cw)], gated_tile,
                           [_rows2(dh2, tt)], _val, D, D, T, tt)
            dbcu, d_conv[l] = _conv_bwd(f"l{l}_conv_bwd", bcu, dgated, cw, seq)
            d_in[l] = _tn(f"l{l}_dw_in", [_rows2(h_in, tt), _full2(g_mix)], _normed, [_rows2(dbcu, tt)], _val,
                          D, 3 * D, T, tt)
            dh, d_norm_mix[l] = _nt_norm_bwd(f"l{l}_in_bwd", [dbcu], w_a_in[l], h_in, g_mix, dh2)

    big = [jnp.stack(d_in), jnp.stack(d_out), d_kv, jnp.stack(d_q), jnp.stack(d_o), jnp.stack(d_up), jnp.stack(d_down)]
    small = jnp.concatenate(d_norm_mix + d_norm_mlp + [d_norm_kv, d_norm_final] + d_conv, axis=0)
    return sq_err, dh.reshape(bl, seq, D), big, small


def kernel(x, norm_mix, norm_mlp, w_a_in, conv_w, w_a_out, norm_kv, w_kv, w_q, w_o, w_up, w_down, norm_final, loss_target, m_norm_mix, m_norm_mlp, m_w_a_in, m_conv_w, m_w_a_out, m_norm_kv, m_w_kv, m_w_q, m_w_o, m_w_up, m_w_down, m_norm_final, v_norm_mix, v_norm_mlp, v_w_a_in, v_conv_w, v_w_a_out, v_norm_kv, v_w_kv, v_w_q, v_w_o, v_w_up, v_w_down, v_norm_final):
    D = x.shape[-1]
    xi, yi, ci = _mesh_pos()
    me_idx = 4 * xi + 2 * yi + ci
    w_big = dict(w_a_in=w_a_in, w_a_out=w_a_out, w_kv=w_kv, w_q=w_q, w_o=w_o, w_up=w_up, w_down=w_down)
    m_big = dict(w_a_in=m_w_a_in, w_a_out=m_w_a_out, w_kv=m_w_kv, w_q=m_w_q, w_o=m_w_o, w_up=m_w_up, w_down=m_w_down)
    v_big = dict(w_a_in=v_w_a_in, w_a_out=v_w_a_out, w_kv=v_w_kv, w_q=v_w_q, w_o=v_w_o, w_up=v_w_up, w_down=v_w_down)
    names = [n for n, _ in BIG_WEIGHTS]
    shard_shapes = [w_big[n].shape for n in names]

    gathered = _all_gather("gather_weights", _pack_shards([w_big[n] for n in names], BF16))
    full = _unpack_gathered(gathered, shard_shapes)
    dc = conv_w.shape[-1]
    taps = conv_w.shape[0] * conv_w.shape[1]
    conv_all = _all_gather("gather_conv", _pad8(conv_w.reshape(taps, dc)))
    conv_full = jnp.moveaxis(conv_all[:, :taps], 0, 1).reshape(conv_w.shape[0], conv_w.shape[1], N_DEV * dc)

    sq_err, grad_x, big, small = _local_grads(x, loss_target, norm_mix, norm_mlp, norm_kv, norm_final, conv_full, *full)
    loss = lax.psum(sq_err * (0.5 / D), ("x", "y", "c"))

    parts = _all_to_all("exchange_grads", _pack_full_grads(big, shard_shapes))
    packed = [_pack_shards([t[n] for n in names], F32) for t in (w_big, m_big, v_big)]
    g_p, d_p, m_p, v_p = _sum8_adamw("adamw", parts, *packed, tr=64)
    grads, deltas, new_m, new_v = (dict(zip(names, _unpack_shards(t, shard_shapes))) for t in (g_p, d_p, m_p, v_p))

    n_gain = 2 * DEPTH + 2
    rows_small = small.shape[0]
    small_all = _all_gather("gather_small_grads", small)

    def small_pack(nm, nl, nk, nf, cw):
        gains = jnp.concatenate([nm, nl, nk.reshape(1, D), nf.reshape(1, D)], axis=0)
        taps_full = lax.dynamic_update_slice(jnp.zeros((taps, D), F32), cw.reshape(taps, dc), (0, me_idx * dc))
        return jnp.concatenate([gains, taps_full], axis=0)

    sp = [small_pack(*t) for t in ((norm_mix, norm_mlp, norm_kv, norm_final, conv_w),
                                   (m_norm_mix, m_norm_mlp, m_norm_kv, m_norm_final, m_conv_w),
                                   (v_norm_mix, v_norm_mlp, v_norm_kv, v_norm_final, v_conv_w))]
    small_out = _sum8_adamw("adamw_small", small_all, *sp, tr=rows_small)

    def small_unpack(t):
        res = dict(norm_mix=t[0:DEPTH], norm_mlp=t[DEPTH:2 * DEPTH], norm_kv=t[2 * DEPTH], norm_final=t[2 * DEPTH + 1])
        res["conv_w"] = lax.dynamic_slice(t[n_gain:], (0, me_idx * dc), (taps, dc)).reshape(conv_w.shape)
        return res

    for dst, t in zip((grads, deltas, new_m, new_v), small_out):
        dst.update(small_unpack(t))

    order = ["norm_mix", "norm_mlp", "w_a_in", "conv_w", "w_a_out", "norm_kv", "w_kv", "w_q", "w_o", "w_up", "w_down",
             "norm_final"]
    return (loss, grad_x, *[grads[n] for n in order], *[deltas[n] for n in order], *[new_m[n] for n in order],
            *[new_v[n] for n in order])
```

```python
import functools

import jax
import jax.numpy as jnp
from jax import lax
from jax.experimental import pallas as pl
from jax.experimental.pallas import tpu as pltpu

F32 = jnp.float32
BF16 = jnp.bfloat16
SDS = jax.ShapeDtypeStruct

EPS = 1e-5
N_A_LAYERS = 2
DEPTH = 4
PATTERNS = ((128, 1), (512, 4), (2048, 16))
N_GROUPS = 3
H_G = 8
HEAD_DIM = 64
QW = H_G * HEAD_DIM
ATT_BLK = 128
ALIBI_MAX_BIAS = 8.0
NEG_INF = -1e30

ADAM_LR = 0.001
ADAM_B1 = 0.9
ADAM_B2 = 0.999
ADAM_EPS = 1e-08
ADAM_WD = 0.01
ADAM_STEP = 10

N_DEV = 8
SUBLANES = 8
V7X_VMEM_LIMIT = 48 * 1024 * 1024
MM_CHUNK = 512

FETCH_DURING = {
    "l0_in": [("w_a_in", 1)], "l0_conv_out": [("w_a_out", 1)], "l0_up": [("w_up", 1)], "l0_down": [("w_down", 1)],
    "l1_in": [("w_kv", None)], "l1_conv_out": [("w_q", 0), ("w_o", 0)], "l1_up": [("w_up", 2)], "l1_down": [("w_down", 2)],
    "kv": [("w_q", 1), ("w_o", 1)], "l2_up": [("w_up", 3)], "l2_down": [("w_down", 3)],
}
PUSH_DURING = {
    "l3_dw_up": [("w_down", 3)], "l3_up_bwd": [("w_up", 3)], "l3_att0_dq": [("w_o", 1)], "l3_q_bwd": [("w_q", 1)],
    "l2_dw_up": [("w_down", 2)], "l2_up_bwd": [("w_up", 2)], "l2_att0_dq": [("w_o", 0)], "l2_q_bwd": [("w_q", 0)],
    "kv_bwd": [("w_kv", None)],
    "l1_dw_up": [("w_down", 1)], "l1_up_bwd": [("w_up", 1)], "l1_conv_bwd": [("w_a_out", 1)], "l1_in_bwd": [("w_a_in", 1)],
    "l0_dw_up": [("w_down", 0)], "l0_up_bwd": [("w_up", 0)], "l0_conv_bwd": [("w_a_out", 0)], "l0_in_bwd": [("w_a_in", 0)],
}


def _mesh_pos():
    return lax.axis_index("x"), lax.axis_index("y"), lax.axis_index("c")


def _flip(v, bit):
    return 1 - v if bit else v


class _Transfer:
    def __init__(self, kind, key, src, src_idx=None, dst=None, dst_idx=None, dst_shape=None):
        self.kind, self.key, self.src, self.src_idx = kind, key, src, src_idx
        self.dst, self.dst_idx, self.dst_shape = dst, dst_idx, dst_shape

    def copies(self, src_ref, dst_ref, send_sems, recv_sems, local_sem):
        x, y, c = _mesh_pos()
        me = 4 * x + 2 * y + c

        def src_slot(j):
            if self.kind == "exchange":
                return src_ref.at[j]
            return src_ref if self.src_idx is None else src_ref.at[self.src_idx]

        def dst_slot(j):
            r = dst_ref.at[j]
            return r if self.dst_idx is None else r.at[self.dst_idx]

        local = pltpu.make_async_copy(src_slot(me), dst_slot(me), local_sem)
        sends, recvs = [], []
        for k in range(1, N_DEV):
            peer = (_flip(x, k & 4), _flip(y, k & 2), _flip(c, k & 1))
            peer_idx = 4 * peer[0] + 2 * peer[1] + peer[2]
            for dst_j, out in ((me, sends), (peer_idx, recvs)):
                out.append(pltpu.make_async_remote_copy(
                    src_ref=src_slot(peer_idx), dst_ref=dst_slot(dst_j), send_sem=send_sems.at[k - 1],
                    recv_sem=recv_sems.at[k - 1], device_id=peer, device_id_type=pl.DeviceIdType.MESH))
        return local, sends, recvs


class _Hub:
    def __init__(self, fetch, push, shards, landing):
        self.fetch, self.push, self.shards, self.landing = fetch, push, shards, landing
        self.weights = {}
        self.grads = {}

    def transfers(self, host):
        out = []
        for name, l in self.fetch.get(host, ()):
            src = self.shards[name]
            shard = src.shape if l is None else src.shape[1:]
            out.append(_Transfer("gather", (name, l), src, src_idx=l, dst_shape=(N_DEV,) + tuple(shard)))
        for name, l in self.push.get(host, ()):
            out.append(_Transfer("exchange", (name, l), self.grads.pop((name, l)), dst=self.landing[name], dst_idx=l))
        return out

    def accept(self, transfers, results):
        for t, r in zip(transfers, results):
            if t.kind == "gather":
                self.weights[t.key] = r
            else:
                self.landing[t.key[0]] = r


def _call(name, body, grid, ins, outs, scratch=(), hub=None):
    transfers = hub.transfers(name) if hub is not None else []
    n_in, n_out, n_scr, n_tr = len(ins), len(outs), len(scratch), len(transfers)
    c_in, c_out, aliases, places = [], [], {}, []
    for t in transfers:
        c_in.append(t.src)
        src_pos = len(c_in) - 1
        if t.dst is not None:
            c_in.append(t.dst)
            aliases[n_in + len(c_in) - 1] = n_out + len(c_out)
            c_out.append(SDS(t.dst.shape, t.dst.dtype))
        else:
            c_out.append(SDS(t.dst_shape, t.src.dtype))
        places.append((src_pos, len(c_out) - 1))
    sems = [pltpu.SemaphoreType.DMA((n_tr, N_DEV - 1)), pltpu.SemaphoreType.DMA((n_tr, N_DEV - 1)),
            pltpu.SemaphoreType.DMA((n_tr,))] if n_tr else []

    def wrapped(*refs):
        in_refs = refs[:n_in]
        cin_refs = refs[n_in:n_in + len(c_in)]
        o0 = n_in + len(c_in)
        out_refs = refs[o0:o0 + n_out]
        cout_refs = refs[o0 + n_out:o0 + n_out + len(c_out)]
        s0 = o0 + n_out + len(c_out)
        scr_refs = refs[s0:s0 + n_scr]
        if n_tr:
            send_sems, recv_sems, local_sems = refs[s0 + n_scr:]
            first = last = None
            for ax, n in enumerate(grid):
                i = pl.program_id(ax)
                first = (i == 0) if first is None else first & (i == 0)
                last = (i == n - 1) if last is None else last & (i == n - 1)

            def all_copies():
                return [t.copies(cin_refs[sp], cout_refs[dp], send_sems.at[n], recv_sems.at[n], local_sems.at[n])
                        for n, (t, (sp, dp)) in enumerate(zip(transfers, places))]

            @pl.when(first)
            def _():
                for local, sends, _ in all_copies():
                    local.start()
                    for cp in sends:
                        cp.start()

        body(*in_refs, *out_refs, *scr_refs)

        if n_tr:
            @pl.when(last)
            def _():
                for local, sends, recvs in all_copies():
                    for cp in recvs:
                        cp.wait_recv()
                    for cp in sends:
                        cp.wait_send()
                    local.wait()

    any_spec = pl.BlockSpec(memory_space=pl.ANY)
    res = pl.pallas_call(
        wrapped,
        name=name,
        grid=grid,
        in_specs=[s for _, s in ins] + [any_spec] * len(c_in),
        out_specs=[s for _, s in outs] + [any_spec] * len(c_out),
        out_shape=[o for o, _ in outs] + c_out,
        scratch_shapes=list(scratch) + sems,
        input_output_aliases=aliases,
        compiler_params=pltpu.CompilerParams(
            dimension_semantics=("arbitrary",) * len(grid), vmem_limit_bytes=V7X_VMEM_LIMIT),
    )(*[a for a, _ in ins], *c_in)
    if n_tr:
        hub.accept(transfers, res[n_out:])
    return res[:n_out]


def _rows(a, tm, cb=None, col=0):
    cb = cb or a.shape[1]
    return (a, pl.BlockSpec((tm, cb), lambda i: (i, col)))


def _full(a):
    nd = a.ndim
    return (a, pl.BlockSpec(a.shape, lambda i: (0,) * nd))


def _prev8(a, tm, cb, col):
    return (a, pl.BlockSpec((SUBLANES, cb), lambda i: (jnp.maximum(i * (tm // SUBLANES) - 1, 0), col)))


def _next8(a, tm, cb, col):
    last = a.shape[0] // SUBLANES - 1
    return (a, pl.BlockSpec((SUBLANES, cb), lambda i: (jnp.minimum((i + 1) * (tm // SUBLANES), last), col)))


def _rows2(a, tt, cb=None, colfn=None):
    cb = cb or a.shape[1]
    colfn = colfn or (lambda s: 0)
    return (a, pl.BlockSpec((tt, cb), lambda s, t: (t, colfn(s))))


def _full2(a):
    nd = a.ndim
    return (a, pl.BlockSpec(a.shape, lambda s, t: (0,) * nd))


def _prev8_2(a, tt, cb, col):
    return (a, pl.BlockSpec((SUBLANES, cb), lambda s, t: (jnp.maximum(t * (tt // SUBLANES) - 1, 0), col)))


def _out_rows(T, n, dtype, tm):
    return (SDS((T, n), dtype), pl.BlockSpec((tm, n), lambda i: (i, 0)))


def _out_acc8(d):
    return (SDS((SUBLANES, d), F32), pl.BlockSpec((SUBLANES, d), lambda i: (0, 0)))


def _rstd(x):
    return lax.rsqrt(jnp.mean(x * x, axis=-1, keepdims=True) + EPS)


def _normed(h_ref, g_ref):
    x = h_ref[...]
    return x * _rstd(x) * g_ref[...]


def _acc8(ref, val, i, n):
    part = val.reshape(-1, SUBLANES, val.shape[-1]).sum(axis=0)

    @pl.when(i == 0)
    def _():
        ref[...] = part

    @pl.when(i > 0)
    def _():
        ref[...] += part

    @pl.when(i == n - 1)
    def _():
        ref[...] = jnp.broadcast_to(jnp.sum(ref[...], axis=0, keepdims=True), ref.shape)


def _gate(b, c, u, c_halo, u_halo, cw_ref, first):
    cu = c * u
    halo = jnp.where(first, 0.0, c_halo * u_halo)
    rows = lax.broadcasted_iota(jnp.int32, cu.shape, 0)
    h1 = halo[SUBLANES - 1:SUBLANES, :]
    h2 = halo[SUBLANES - 2:SUBLANES - 1, :]
    cu1 = jnp.where(rows == 0, h1, pltpu.roll(cu, 1, 0))
    cu2 = jnp.where(rows == 0, h2, jnp.where(rows == 1, h1, pltpu.roll(cu, 2, 0)))
    conv = cw_ref[0:1, :] * cu + cw_ref[1:2, :] * cu1 + cw_ref[2:3, :] * cu2
    return b * conv, conv, cu, cu1, cu2


def _relu2(a):
    r = jnp.maximum(a, 0.0)
    return r * r


def _dot(a, b):
    return jnp.dot(a, b, preferred_element_type=F32)


def _dot_nt(a, b):
    return lax.dot_general(a, b, (((1,), (1,)), ((), ())), preferred_element_type=F32)


def _dot_tn(a, b):
    return lax.dot_general(a, b, (((0,), (0,)), ((), ())), preferred_element_type=F32)


def _chunks(n):
    c = min(MM_CHUNK, n)
    assert n % c == 0, n
    return [(k * c, (k + 1) * c) for k in range(n // c)]


def _col_weight(w):
    _, K, ns = w.shape
    N = N_DEV * ns
    direct = ns % 128 == 0 and ns >= 256
    scratch = [] if direct else [pltpu.VMEM((K, N), BF16)]

    def prepare(w_ref, s_ref, step):
        if direct:
            return

        @pl.when(step == 0)
        def _():
            for j in range(N_DEV):
                s_ref[:, j * ns:(j + 1) * ns] = w_ref[j]

    def chunks(w_ref, s_ref):
        if direct:
            return [(j * ns, (j + 1) * ns, (lambda j=j: w_ref[j])) for j in range(N_DEV)]
        return [(lo, hi, (lambda lo=lo, hi=hi: s_ref[:, lo:hi])) for lo, hi in _chunks(N)]

    return N, scratch, prepare, chunks


def _norm_mm(name, h, g, w, tm=256, hub=None):
    T, _ = h.shape
    N, w_scratch, prepare, chunks = _col_weight(w)

    def body(h_ref, g_ref, w_ref, o_ref, *s):
        s_ref = s[0] if s else None
        prepare(w_ref, s_ref, pl.program_id(0))
        a = _normed(h_ref, g_ref).astype(BF16)
        for lo, hi, load in chunks(w_ref, s_ref):
            o_ref[:, lo:hi] = _dot(a, load())

    return _call(name, body, (T // tm,), [_rows(h, tm), _full(g), _full(w)], [_out_rows(T, N, F32, tm)],
                 scratch=w_scratch, hub=hub)[0]


def _gate_mm_res(name, bcu, cw, w, h, seq, tm=256, hub=None):
    T, D = h.shape

    def body(b_ref, c_ref, u_ref, ch_ref, uh_ref, cw_ref, w_ref, h_ref, o_ref):
        first = (pl.program_id(0) * tm) % seq == 0
        gated = _gate(b_ref[...], c_ref[...], u_ref[...], ch_ref[...], uh_ref[...], cw_ref, first)[0].astype(BF16)
        for lo, hi in _chunks(D):
            o_ref[:, lo:hi] = h_ref[:, lo:hi] + _dot(gated, w_ref[:, lo:hi])

    ins = [_rows(bcu, tm, D, 0), _rows(bcu, tm, D, 1), _rows(bcu, tm, D, 2), _prev8(bcu, tm, D, 1),
           _prev8(bcu, tm, D, 2), _full(cw), _full(w), _rows(h, tm)]
    return _call(name, body, (T // tm,), ins, [_out_rows(T, D, F32, tm)], hub=hub)[0]


def _relu2_mm_res(name, a, w, h, tm=256, hub=None):
    T, D = h.shape
    K = a.shape[1]

    def body(a_ref, w_ref, h_ref, o_ref, acc_ref):
        for n, (lo, hi) in enumerate(_chunks(K)):
            d = _dot(_relu2(a_ref[:, lo:hi]).astype(BF16), w_ref[lo:hi, :])
            if n == 0:
                acc_ref[...] = d
            else:
                acc_ref[...] += d
        o_ref[...] = h_ref[...] + acc_ref[...]

    return _call(name, body, (T // tm,), [_rows(a, tm), _full(w), _rows(h, tm)], [_out_rows(T, D, F32, tm)],
                 scratch=[pltpu.VMEM((tm, D), F32)], hub=hub)[0]


def _mm_res(name, a, w, h, tm=256):
    T, D = h.shape
    _, w_scratch, prepare, chunks = _col_weight(w)

    def body(a_ref, w_ref, h_ref, o_ref, *s):
        s_ref = s[0] if s else None
        prepare(w_ref, s_ref, pl.program_id(0))
        av = a_ref[...].astype(BF16)
        for lo, hi, load in chunks(w_ref, s_ref):
            o_ref[:, lo:hi] = h_ref[:, lo:hi] + _dot(av, load())

    return _call(name, body, (T // tm,), [_rows(a, tm), _full(w), _rows(h, tm)], [_out_rows(T, D, F32, tm)],
                 scratch=w_scratch)[0]


def _nt_relu2_bwd(name, dh, w, a, tm=256):
    T, _ = dh.shape
    K = w.shape[0]

    def body(dh_ref, w_ref, a_ref, o_ref):
        d = dh_ref[...].astype(BF16)
        for lo, hi in _chunks(K):
            dr = _dot_nt(d, w_ref[lo:hi, :])
            o_ref[:, lo:hi] = (dr * (2.0 * jnp.maximum(a_ref[:, lo:hi], 0.0))).astype(BF16)

    return _call(name, body, (T // tm,), [_rows(dh, tm), _full(w), _rows(a, tm)], [_out_rows(T, K, BF16, tm)])[0]


def _concat_bf16(*refs):
    vals = [r[...].astype(BF16) for r in refs]
    return vals[0] if len(vals) == 1 else jnp.concatenate(vals, axis=1)


def _nt_plain(name, dy, w, tm=256):
    T, N = dy.shape
    if w.ndim == 3:
        K = w.shape[1]
        _, w_scratch, prepare, chunks = _col_weight(w)
    else:
        K = w.shape[0]
        w_scratch, prepare = [], (lambda w_ref, s_ref, step: None)
        chunks = lambda w_ref, s_ref: [(lo, hi, (lambda lo=lo, hi=hi: w_ref[:, lo:hi])) for lo, hi in _chunks(N)]

    def body(dy_ref, w_ref, o_ref, acc_ref, *s):
        s_ref = s[0] if s else None
        prepare(w_ref, s_ref, pl.program_id(0))
        for n, (lo, hi, load) in enumerate(chunks(w_ref, s_ref)):
            d = _dot_nt(dy_ref[:, lo:hi].astype(BF16), load())
            if n == 0:
                acc_ref[...] = d
            else:
                acc_ref[...] += d
        o_ref[...] = acc_ref[...]

    return _call(name, body, (T // tm,), [_rows(dy, tm), _full(w)], [_out_rows(T, K, F32, tm)],
                 scratch=[pltpu.VMEM((tm, K), F32)] + w_scratch)[0]


def _nt_norm_bwd(name, dys, w, h, g, dh_in, tm=256, hub=None):
    T, D = h.shape
    _, w_scratch, prepare, chunks = _col_weight(w)
    n_steps = T // tm
    n_dy = len(dys)

    def body(*refs):
        dy_refs = refs[:n_dy]
        w_ref, h_ref, g_ref, dhin_ref, o_ref, dg_ref, acc_ref = refs[n_dy:n_dy + 7]
        s_ref = refs[n_dy + 7] if len(refs) > n_dy + 7 else None
        i = pl.program_id(0)
        prepare(w_ref, s_ref, i)
        dy = _concat_bf16(*dy_refs)
        for n, (lo, hi, load) in enumerate(chunks(w_ref, s_ref)):
            d = _dot_nt(dy[:, lo:hi], load())
            if n == 0:
                acc_ref[...] = d
            else:
                acc_ref[...] += d
        dn = acc_ref[...]
        x = h_ref[...]
        rstd = _rstd(x)
        xhat = x * rstd
        dxhat = dn * g_ref[...]
        dx = rstd * (dxhat - xhat * jnp.mean(dxhat * xhat, axis=-1, keepdims=True))
        o_ref[...] = dhin_ref[...] + dx
        _acc8(dg_ref, dn * xhat, i, n_steps)

    ins = [_rows(d, tm) for d in dys] + [_full(w), _rows(h, tm), _full(g), _rows(dh_in, tm)]
    outs = [_out_rows(T, D, F32, tm), _out_acc8(D)]
    dh, dg = _call(name, body, (n_steps,), ins, outs, scratch=[pltpu.VMEM((tm, D), F32)] + w_scratch, hub=hub)
    return dh, dg[0:1]


def _tn(name, a_ins, a_fn, y_ins, y_fn, K, N, T, tt, split=None, out_cols=None, hub=None):
    kind, parts = split or ("n", 1)
    kb, nb = (K // parts, N) if kind == "k" else (K, N // parts)
    n_steps = T // tt
    n_a = len(a_ins)
    n_y = len(y_ins)
    assert out_cols is None or (kind == "n" and nb % out_cols == 0)

    def body(*refs):
        a_refs = refs[:n_a]
        y_refs = refs[n_a:n_a + n_y]
        o_ref, acc_ref = refs[n_a + n_y:]
        t = pl.program_id(1)
        a = a_fn(*a_refs).astype(BF16)
        y = y_fn(*y_refs).astype(BF16)
        for lo, hi in _chunks(nb):
            d = _dot_tn(a, y[:, lo:hi])

            @pl.when(t == 0)
            def _():
                acc_ref[:, lo:hi] = d

            @pl.when(t > 0)
            def _():
                acc_ref[:, lo:hi] += d

        @pl.when(t == n_steps - 1)
        def _():
            if out_cols is None:
                o_ref[...] = acc_ref[...].astype(BF16)
            else:
                for j in range(nb // out_cols):
                    o_ref[j] = acc_ref[:, j * out_cols:(j + 1) * out_cols].astype(BF16)

    if out_cols is None:
        out = (SDS((K, N), BF16), pl.BlockSpec((kb, nb), (lambda s, t: (s, 0)) if kind == "k" else (lambda s, t: (0, s))))
    else:
        out = (SDS((N // out_cols, K, out_cols), BF16), pl.BlockSpec((nb // out_cols, K, out_cols), lambda s, t: (s, 0, 0)))
    return _call(name, body, (parts, n_steps), list(a_ins) + list(y_ins), [out],
                 scratch=[pltpu.VMEM((kb, nb), F32)], hub=hub)[0]


def _val(ref):
    return ref[...]


def _concat_f32(*refs):
    vals = [r[...] for r in refs]
    return vals[0] if len(vals) == 1 else jnp.concatenate(vals, axis=1)


ATT_TILE_ROWS = 512
HEAD_PAIRS = H_G // 2
ATT_SCALE = HEAD_DIM ** -0.5


def _slope(h):
    return 2.0 ** (-ALIBI_MAX_BIAS * (h + 1) / H_G)


def _att_geom(T, bl, g):
    dil = PATTERNS[g][1]
    sub = ATT_BLK * dil
    nsub = max(1, ATT_TILE_ROWS // sub)
    rows = sub * nsub
    return dil, sub, nsub, rows, T // bl // rows


def _att_specs(T, bl, g):
    _, sub, nsub, rows, nt = _att_geom(T, bl, g)
    last_sub = T // sub - 1
    tile = lambda col: pl.BlockSpec((rows, 128), lambda b, i, hp: (b * nt + i, col(hp)))
    prev = lambda col: pl.BlockSpec((sub, 128), lambda b, i, hp: (jnp.maximum((b * nt + i) * nsub - 1, 0), col(hp)))
    nxt = lambda col: pl.BlockSpec((sub, 128), lambda b, i, hp: (jnp.minimum((b * nt + i + 1) * nsub, last_sub), col(hp)))
    return tile, prev, nxt


def _sub_rows(j, r, dil):
    start = j * ATT_BLK * dil + r
    return pl.ds(start, ATT_BLK, stride=dil) if dil > 1 else pl.ds(start, ATT_BLK)


def _att_consts(hp, dil):
    h0 = lax.broadcasted_iota(jnp.int32, (ATT_BLK, 128), 1) < HEAD_DIM
    a = lax.broadcasted_iota(jnp.int32, (ATT_BLK, ATT_BLK), 0)
    c = lax.broadcasted_iota(jnp.int32, (ATT_BLK, ATT_BLK), 1)
    dist_p = ((ATT_BLK + a - c) * dil).astype(F32)
    dist_c = ((a - c) * dil).astype(F32)
    bias_p, bias_c = [], []
    for h in range(2):
        slope = jnp.float32(_slope(2 * (HEAD_PAIRS - 1) + h))
        for p in range(HEAD_PAIRS - 2, -1, -1):
            slope = jnp.where(hp == p, jnp.float32(_slope(2 * p + h)), slope)
        bias_p.append(jnp.where(c >= a, -slope * dist_p, NEG_INF))
        bias_c.append(jnp.where(c <= a, -slope * dist_c, NEG_INF))
    return h0, bias_p, bias_c


def _split_heads(x, h0):
    return [jnp.where(h0, x, 0.0).astype(BF16), jnp.where(h0, 0.0, x).astype(BF16)]


def _head_cols(x):
    return [x[:, 0:1], x[:, HEAD_DIM:HEAD_DIM + 1]]


def _head_sums(x, h0):
    return [jnp.sum(jnp.where(h0, x, 0.0), axis=-1, keepdims=True), jnp.sum(jnp.where(h0, 0.0, x), axis=-1, keepdims=True)]


def _attn_fwd(name, q, kv, g, bl):
    T = q.shape[0]
    dil, _, nsub, _, _ = _att_geom(T, bl, g)
    tile, prev, _ = _att_specs(T, bl, g)

    def body(q_ref, kp_ref, kc_ref, vp_ref, vc_ref, o_ref, lse_ref):
        first = pl.program_id(1) == 0
        h0, bias_p, bias_c = _att_consts(pl.program_id(2), dil)
        bias_first = [jnp.where(first, NEG_INF, b) for b in bias_p]
        for j in range(nsub):
            for r in range(dil):
                cur = _sub_rows(j, r, dil)
                if j == 0:
                    before = _sub_rows(0, r, dil)
                    kp, vp, bp = kp_ref[before, :], vp_ref[before, :], bias_first
                else:
                    before = _sub_rows(j - 1, r, dil)
                    kp, vp, bp = kc_ref[before, :], vc_ref[before, :], bias_p
                kp, vp = kp.astype(BF16), vp.astype(BF16)
                kc, vc = kc_ref[cur, :].astype(BF16), vc_ref[cur, :].astype(BF16)
                qh = _split_heads(q_ref[cur, :] * ATT_SCALE, h0)
                sp = [_dot_nt(qh[h], kp) + bp[h] for h in range(2)]
                sc = [_dot_nt(qh[h], kc) + bias_c[h] for h in range(2)]
                mx = [jnp.maximum(jnp.max(sp[h], axis=-1, keepdims=True), jnp.max(sc[h], axis=-1, keepdims=True))
                      for h in range(2)]
                ep = [jnp.exp(sp[h] - mx[h]) for h in range(2)]
                ec = [jnp.exp(sc[h] - mx[h]) for h in range(2)]
                den = [jnp.sum(ep[h], axis=-1, keepdims=True) + jnp.sum(ec[h], axis=-1, keepdims=True) for h in range(2)]
                acc = [_dot(ep[h].astype(BF16), vp) + _dot(ec[h].astype(BF16), vc) for h in range(2)]
                o_ref[cur, :] = jnp.where(h0, acc[0] / den[0], acc[1] / den[1])
                lse_ref[cur, :] = jnp.where(h0, mx[0] + jnp.log(den[0]), mx[1] + jnp.log(den[1]))

    ins = [(q, tile(lambda hp: 4 * g + hp)), (kv, prev(lambda hp: 8 * g + hp)), (kv, tile(lambda hp: 8 * g + hp)),
           (kv, prev(lambda hp: 8 * g + 4 + hp)), (kv, tile(lambda hp: 8 * g + 4 + hp))]
    out = (SDS((T, QW), F32), tile(lambda hp: hp))
    _, _, _, _, nt = _att_geom(T, bl, g)
    return _call(name, body, (bl, nt, HEAD_PAIRS), ins, [out, out])


def _combine(name, os_, lses, tm=512):
    T = os_[0].shape[0]

    def body(o0, o1, o2, l0, l1, l2, o_ref, lse_ref):
        ls = [l0[...], l1[...], l2[...]]
        mx = jnp.maximum(jnp.maximum(ls[0], ls[1]), ls[2])
        es = [jnp.exp(l - mx) for l in ls]
        den = es[0] + es[1] + es[2]
        o_ref[...] = (es[0] * o0[...] + es[1] * o1[...] + es[2] * o2[...]) / den
        lse_ref[...] = mx + jnp.log(den)

    ins = [_rows(t, tm) for t in list(os_) + list(lses)]
    return _call(name, body, (T // tm,), ins, [_out_rows(T, QW, F32, tm), _out_rows(T, QW, F32, tm)])


def _attn_bwd_dq(name, q, kv, do, o, lse, g, bl, hub=None):
    T = q.shape[0]
    dil, _, nsub, _, nt = _att_geom(T, bl, g)
    tile, prev, _ = _att_specs(T, bl, g)

    def body(q_ref, kp_ref, kc_ref, vp_ref, vc_ref, do_ref, o_ref, lse_ref, dq_ref):
        first = pl.program_id(1) == 0
        h0, bias_p, bias_c = _att_consts(pl.program_id(2), dil)
        bias_first = [jnp.where(first, NEG_INF, b) for b in bias_p]
        for j in range(nsub):
            for r in range(dil):
                cur = _sub_rows(j, r, dil)
                if j == 0:
                    before = _sub_rows(0, r, dil)
                    kp, vp, bp = kp_ref[before, :], vp_ref[before, :], bias_first
                else:
                    before = _sub_rows(j - 1, r, dil)
                    kp, vp, bp = kc_ref[before, :], vc_ref[before, :], bias_p
                kp, vp = kp.astype(BF16), vp.astype(BF16)
                kc, vc = kc_ref[cur, :].astype(BF16), vc_ref[cur, :].astype(BF16)
                do2 = do_ref[cur, :]
                qh = _split_heads(q_ref[cur, :] * ATT_SCALE, h0)
                dob = _split_heads(do2, h0)
                lse_h = _head_cols(lse_ref[cur, :])
                delta = _head_sums(do2 * o_ref[cur, :], h0)
                pp = [jnp.exp(_dot_nt(qh[h], kp) + bp[h] - lse_h[h]) for h in range(2)]
                pc = [jnp.exp(_dot_nt(qh[h], kc) + bias_c[h] - lse_h[h]) for h in range(2)]
                dsp = [(pp[h] * (_dot_nt(dob[h], vp) - delta[h])).astype(BF16) for h in range(2)]
                dsc = [(pc[h] * (_dot_nt(dob[h], vc) - delta[h])).astype(BF16) for h in range(2)]
                dqh = [_dot(dsp[h], kp) + _dot(dsc[h], kc) for h in range(2)]
                dq_ref[cur, :] = jnp.where(h0, dqh[0], dqh[1]) * ATT_SCALE

    own = lambda hp: hp
    ins = [(q, tile(lambda hp: 4 * g + hp)), (kv, prev(lambda hp: 8 * g + hp)), (kv, tile(lambda hp: 8 * g + hp)),
           (kv, prev(lambda hp: 8 * g + 4 + hp)), (kv, tile(lambda hp: 8 * g + 4 + hp)),
           (do, tile(own)), (o, tile(own)), (lse, tile(own))]
    return _call(name, body, (bl, nt, HEAD_PAIRS), ins, [(SDS((T, QW), F32), tile(own))], hub=hub)[0]


def _attn_bwd_dkv(name, q, kv, do, o, lse, g, bl, prev=None):
    T = q.shape[0]
    dil, _, nsub, _, nt = _att_geom(T, bl, g)
    tile, _, nxt = _att_specs(T, bl, g)
    has_prev = prev is not None

    def body(*refs):
        k_ref, v_ref, q_ref, qn_ref, do_ref, don_ref, o_ref, on_ref, l_ref, ln_ref = refs[:10]
        rest = refs[10:]
        if has_prev:
            dkp_ref, dvp_ref, dk_ref, dv_ref = rest
        else:
            dk_ref, dv_ref = rest
        last = pl.program_id(1) == nt - 1
        h0, bias_p, bias_c = _att_consts(pl.program_id(2), dil)
        bias_last = [jnp.where(last, NEG_INF, b) for b in bias_p]
        for j in range(nsub):
            for r in range(dil):
                cur = _sub_rows(j, r, dil)
                kb, vb = k_ref[cur, :].astype(BF16), v_ref[cur, :].astype(BF16)
                sets = [(q_ref, do_ref, o_ref, l_ref, cur, bias_c)]
                if j < nsub - 1:
                    sets.append((q_ref, do_ref, o_ref, l_ref, _sub_rows(j + 1, r, dil), bias_p))
                else:
                    sets.append((qn_ref, don_ref, on_ref, ln_ref, _sub_rows(0, r, dil), bias_last))
                dk = [None, None]
                dv = [None, None]
                for qr, dor, orr, lr, rows, bias in sets:
                    qs = qr[rows, :] * ATT_SCALE
                    do2 = dor[rows, :]
                    qh = _split_heads(qs, h0)
                    dob = _split_heads(do2, h0)
                    qsb, do2b = qs.astype(BF16), do2.astype(BF16)
                    lse_h = _head_cols(lr[rows, :])
                    delta = _head_sums(do2 * orr[rows, :], h0)
                    p = [jnp.exp(_dot_nt(qh[h], kb) + bias[h] - lse_h[h]) for h in range(2)]
                    ds = [(p[h] * (_dot_nt(dob[h], vb) - delta[h])).astype(BF16) for h in range(2)]
                    for h in range(2):
                        dvh = _dot_tn(p[h].astype(BF16), do2b)
                        dkh = _dot_tn(ds[h], qsb)
                        dv[h] = dvh if dv[h] is None else dv[h] + dvh
                        dk[h] = dkh if dk[h] is None else dk[h] + dkh
                dk2 = jnp.where(h0, dk[0], dk[1])
                dv2 = jnp.where(h0, dv[0], dv[1])
                if has_prev:
                    dk2 = dk2 + dkp_ref[cur, :]
                    dv2 = dv2 + dvp_ref[cur, :]
                dk_ref[cur, :] = dk2
                dv_ref[cur, :] = dv2

    own = lambda hp: hp
    qcol = lambda hp: 4 * g + hp
    ins = [(kv, tile(lambda hp: 8 * g + hp)), (kv, tile(lambda hp: 8 * g + 4 + hp)), (q, tile(qcol)), (q, nxt(qcol)),
           (do, tile(own)), (do, nxt(own)), (o, tile(own)), (o, nxt(own)), (lse, tile(own)), (lse, nxt(own))]
    if has_prev:
        ins += [(prev[0], tile(own)), (prev[1], tile(own))]
    out = (SDS((T, QW), F32), tile(own))
    return _call(name, body, (bl, nt, HEAD_PAIRS), ins, [out, out])


def _final_loss(name, h, tgt, g, tm=256):
    T, D = h.shape
    n_steps = T // tm

    def body(h_ref, t_ref, g_ref, dh_ref, loss_ref, dg_ref, sq_ref):
        i = pl.program_id(0)
        x = h_ref[...]
        rstd = _rstd(x)
        xhat = x * rstd
        err = xhat * g_ref[...] - t_ref[...]
        _acc8(sq_ref, err * err, i, n_steps)
        dy = err * (1.0 / D)
        dxhat = dy * g_ref[...]
        dh_ref[...] = rstd * (dxhat - xhat * jnp.mean(dxhat * xhat, axis=-1, keepdims=True))
        _acc8(dg_ref, dy * xhat, i, n_steps)

        @pl.when(i == n_steps - 1)
        def _():
            loss_ref[...] = jnp.full(loss_ref.shape, jnp.sum(sq_ref[0:1, :]), F32)

    outs = [_out_rows(T, D, F32, tm), (SDS((SUBLANES, 128), F32), pl.BlockSpec((SUBLANES, 128), lambda i: (0, 0))),
            _out_acc8(D)]
    dh, loss, dg = _call(name, body, (n_steps,), [_rows(h, tm), _rows(tgt, tm), _full(g)], outs,
                         scratch=[pltpu.VMEM((SUBLANES, D), F32)])
    return dh, loss[0, 0], dg[0:1]


def _conv_bwd(name, bcu, dgated, cw, seq, tm=256, hub=None):
    T, D = dgated.shape
    n_steps = T // tm

    def body(b_ref, c_ref, u_ref, ch_ref, uh_ref, dg_ref, dgn_ref, bn_ref, cw_ref, o_ref, t0_ref, t1_ref, t2_ref):
        i = pl.program_id(0)
        first = (i * tm) % seq == 0
        last = ((i + 1) * tm) % seq == 0
        b = b_ref[...]
        c = c_ref[...]
        u = u_ref[...]
        _, conv, cu, cu1, cu2 = _gate(b, c, u, ch_ref[...], uh_ref[...], cw_ref, first)
        dgat = dg_ref[...]
        dconv = dgat * b
        nxt = jnp.where(last, 0.0, dgn_ref[...] * bn_ref[...])
        rows = lax.broadcasted_iota(jnp.int32, dconv.shape, 0)
        n1 = nxt[0:1, :]
        n2 = nxt[1:2, :]
        dc1 = jnp.where(rows == tm - 1, n1, pltpu.roll(dconv, tm - 1, 0))
        dc2 = jnp.where(rows == tm - 1, n2, jnp.where(rows == tm - 2, n1, pltpu.roll(dconv, tm - 2, 0)))
        dcu = cw_ref[0:1, :] * dconv + cw_ref[1:2, :] * dc1 + cw_ref[2:3, :] * dc2
        o_ref[:, 0:D] = (dgat * conv).astype(BF16)
        o_ref[:, D:2 * D] = (dcu * u).astype(BF16)
        o_ref[:, 2 * D:3 * D] = (dcu * c).astype(BF16)
        _acc8(t0_ref, dconv * cu, i, n_steps)
        _acc8(t1_ref, dconv * cu1, i, n_steps)
        _acc8(t2_ref, dconv * cu2, i, n_steps)

    ins = [_rows(bcu, tm, D, 0), _rows(bcu, tm, D, 1), _rows(bcu, tm, D, 2), _prev8(bcu, tm, D, 1), _prev8(bcu, tm, D, 2),
           _rows(dgated, tm), _next8(dgated, tm, D, 0), _next8(bcu, tm, D, 0), _full(cw)]
    outs = [_out_rows(T, 3 * D, BF16, tm), _out_acc8(D), _out_acc8(D), _out_acc8(D)]
    dbcu, t0, t1, t2 = _call(name, body, (n_steps,), ins, outs, hub=hub)
    return dbcu, jnp.concatenate([t0[0:1], t1[0:1], t2[0:1]], axis=0)


def _sum8_adamw(name, parts, w, m, v, tr):
    R, C = w.shape
    b1c = 1.0 - ADAM_B1 ** ADAM_STEP
    b2c = 1.0 - ADAM_B2 ** ADAM_STEP

    def body(p_ref, w_ref, m_ref, v_ref, g_ref, d_ref, nm_ref, nv_ref):
        g = p_ref[0].astype(F32)
        for j in range(1, N_DEV):
            g = g + p_ref[j].astype(F32)
        nm = ADAM_B1 * m_ref[...] + (1.0 - ADAM_B1) * g
        nv = ADAM_B2 * v_ref[...] + (1.0 - ADAM_B2) * (g * g)
        m_hat = nm / b1c
        v_hat = nv / b2c
        g_ref[...] = g
        d_ref[...] = -ADAM_LR * (m_hat / (jnp.sqrt(v_hat) + ADAM_EPS) + ADAM_WD * w_ref[...])
        nm_ref[...] = nm
        nv_ref[...] = nv

    ins = [(parts, pl.BlockSpec((N_DEV, tr, C), lambda i: (0, i, 0))), _rows(w, tr), _rows(m, tr), _rows(v, tr)]
    outs = [_out_rows(R, C, F32, tr)] * 4
    return _call(name, body, (R // tr,), ins, outs)


def _all_gather(name, items):
    n = len(items)
    shapes = [tuple(a.shape if idx is None else a.shape[1:]) for a, idx in items]

    def body(*refs):
        x_refs, out_refs = refs[:n], refs[n:2 * n]
        send_sems, recv_sems, local_sems = refs[2 * n:]
        x, y, c = _mesh_pos()
        me, sibling = (x, y, c), (x, y, 1 - c)
        chips = [(1 - x, y), (x, 1 - y), (1 - x, 1 - y)]

        def copy(t, k, block, to, own=False):
            dst = out_refs[t].at[4 * block[0] + 2 * block[1] + block[2]]
            src = dst
            if own:
                src = x_refs[t] if items[t][1] is None else x_refs[t].at[items[t][1]]
            return pltpu.make_async_remote_copy(
                src_ref=src, dst_ref=dst, send_sem=send_sems.at[t, k], recv_sem=recv_sems.at[t, k],
                device_id=to, device_id_type=pl.DeviceIdType.MESH)

        started = []
        for t in range(n):
            src = x_refs[t] if items[t][1] is None else x_refs[t].at[items[t][1]]
            mine = pltpu.make_async_copy(src, out_refs[t].at[4 * x + 2 * y + c], local_sems.at[t])
            mine.start()
            first = [copy(t, 0, me, sibling, own=True)]
            first += [copy(t, 1 + j, me, (*chip, c), own=True) for j, chip in enumerate(chips)]
            for cp in first:
                cp.start()
            started.append((mine, first))
        passed = []
        for t in range(n):
            for j, chip in enumerate(chips):
                copy(t, 1 + j, (*chip, c), me).wait_recv()
                fwd = copy(t, 4 + j, (*chip, c), sibling)
                fwd.start()
                passed.append(fwd)
        for t in range(n):
            copy(t, 0, sibling, me).wait_recv()
            for j, chip in enumerate(chips):
                copy(t, 4 + j, (*chip, 1 - c), me).wait_recv()
        for mine, first in started:
            for cp in first:
                cp.wait_send()
            mine.wait()
        for cp in passed:
            cp.wait_send()

    any_spec = pl.BlockSpec(memory_space=pl.ANY)
    return pl.pallas_call(
        body, name=name,
        out_shape=[SDS((N_DEV,) + s, a.dtype) for s, (a, _) in zip(shapes, items)],
        in_specs=[any_spec] * n,
        out_specs=[any_spec] * n,
        scratch_shapes=[pltpu.SemaphoreType.DMA((n, 7)), pltpu.SemaphoreType.DMA((n, 7)), pltpu.SemaphoreType.DMA((n,))],
    )(*[a for a, _ in items])


def _pad8(t):
    return jnp.pad(t, ((0, SUBLANES - t.shape[0]), (0, 0)))


def _rows_merged(w):
    return w.reshape(w.shape[0] * w.shape[1], w.shape[2])


def _local_grads(x, tgt, norm_mix, norm_mlp, norm_kv, norm_final, conv_w, hub):
    bl, seq, D = x.shape
    T = bl * seq
    h = x.reshape(T, D)
    tgt = tgt.reshape(T, D)
    row = lambda t, l: t[l:l + 1]
    W = hub.weights
    saved = []
    kv = h_kv = None
    for l in range(DEPTH):
        if l < N_A_LAYERS:
            bcu = _norm_mm(f"l{l}_in", h, row(norm_mix, l), W["w_a_in", l], hub=hub)
            h2 = _gate_mm_res(f"l{l}_conv_out", bcu, _pad8(conv_w[l]), _rows_merged(W["w_a_out", l]), h, seq, hub=hub)
            saved.append((h, bcu, h2))
        else:
            i = l - N_A_LAYERS
            if l == N_A_LAYERS:
                h_kv = h
                kv = _norm_mm("kv", h, norm_kv.reshape(1, D), W["w_kv", None], hub=hub)
            q = _norm_mm(f"l{l}_q", h, row(norm_mix, l), W["w_q", i])
            per_group = [_attn_fwd(f"l{l}_att{g}", q, kv, g, bl) for g in range(N_GROUPS)]
            o, lse = _combine(f"l{l}_combine", [p[0] for p in per_group], [p[1] for p in per_group])
            h2 = _mm_res(f"l{l}_att_out", o, W["w_o", i], h)
            saved.append((h, q, o, lse, h2))
        a = _norm_mm(f"l{l}_up", h2, row(norm_mlp, l), W["w_up", l], hub=hub)
        h = _relu2_mm_res(f"l{l}_down", a, _rows_merged(W["w_down", l]), h2, hub=hub)
        saved[-1] = saved[-1] + (a,)

    dh, sq_err, d_norm_final = _final_loss("loss", h, tgt, norm_final.reshape(1, D))

    d_norm_mix = [None] * DEPTH
    d_norm_mlp = [None] * DEPTH
    d_conv = [None] * N_A_LAYERS
    d_norm_kv = None
    dkv_acc = [None] * N_GROUPS
    G = hub.grads
    as_slots = lambda g: g.reshape(N_DEV, g.shape[0] // N_DEV, g.shape[1])
    tt = 512
    for l in reversed(range(DEPTH)):
        a, h2 = saved[l][-1], saved[l][-2]
        h_in = saved[l][0]
        g_mlp = row(norm_mlp, l)
        g_mix = row(norm_mix, l)
        w_up_l = W["w_up", l]
        FF = N_DEV * w_up_l.shape[2]
        da = _nt_relu2_bwd(f"l{l}_down_bwd", dh, _rows_merged(W["w_down", l]), a)
        G["w_down", l] = as_slots(_tn(f"l{l}_dw_down", [_rows2(a, tt, FF // 2, lambda s: s)], lambda r: _relu2(r[...]),
                                      [_rows2(dh, tt)], _val, FF, D, T, tt, split=("k", 2)))
        G["w_up", l] = _tn(f"l{l}_dw_up", [_rows2(h2, tt), _full2(g_mlp)], _normed,
                           [_rows2(da, tt, FF // 2, lambda s: s)], _val, D, FF, T, tt, split=("n", 2),
                           out_cols=w_up_l.shape[2], hub=hub)
        dh2, d_norm_mlp[l] = _nt_norm_bwd(f"l{l}_up_bwd", [da], w_up_l, h2, g_mlp, dh, hub=hub)
        if l >= N_A_LAYERS:
            i = l - N_A_LAYERS
            _, q, o, lse, _, _ = saved[l]
            w_o_i, w_q_i = W["w_o", i], W["w_q", i]
            do = _nt_plain(f"l{l}_att_out_bwd", dh2, w_o_i)
            G["w_o", i] = _tn(f"l{l}_dw_o", [_rows2(o, tt)], _val, [_rows2(dh2, tt)], _val, QW, D, T, tt,
                              out_cols=w_o_i.shape[2])
            dqs = []
            for g in range(N_GROUPS):
                dqs.append(_attn_bwd_dq(f"l{l}_att{g}_dq", q, kv, do, o, lse, g, bl, hub=hub))
                dkv_acc[g] = _attn_bwd_dkv(f"l{l}_att{g}_dkv", q, kv, do, o, lse, g, bl, prev=dkv_acc[g])
            G["w_q", i] = _tn(f"l{l}_dw_q", [_rows2(h_in, tt), _full2(g_mix)], _normed,
                              [_rows2(t, tt) for t in dqs], _concat_f32, D, N_GROUPS * QW, T, tt, out_cols=w_q_i.shape[2])
            dh, d_norm_mix[l] = _nt_norm_bwd(f"l{l}_q_bwd", dqs, w_q_i, h_in, g_mix, dh2, hub=hub)
            if l == N_A_LAYERS:
                dkvs = [t for pair in dkv_acc for t in pair]
                g_kv = norm_kv.reshape(1, D)
                w_kv = W["w_kv", None]
                G["w_kv", None] = _tn("dw_kv", [_rows2(h_kv, 256), _full2(g_kv)], _normed,
                                      [_rows2(t, 256) for t in dkvs], _concat_f32, D, 2 * N_GROUPS * QW, T, 256,
                                      out_cols=w_kv.shape[2])
                dh, d_norm_kv = _nt_norm_bwd("kv_bwd", dkvs, w_kv, h_kv, g_kv, dh, hub=hub)
        else:
            _, bcu, _, _ = saved[l]
            cw = _pad8(conv_w[l])
            w_in_l = W["w_a_in", l]
            dgated = _nt_plain(f"l{l}_conv_out_bwd", dh2, _rows_merged(W["w_a_out", l]))

            def gated_tile(b_ref, c_ref, u_ref, ch_ref, uh_ref, cw_ref):
                first = (pl.program_id(1) * tt) % seq == 0
                return _gate(b_ref[...], c_ref[...], u_ref[...], ch_ref[...], uh_ref[...], cw_ref, first)[0]

            G["w_a_out", l] = as_slots(_tn(
                f"l{l}_dw_conv_out",
                [_rows2(bcu, tt, D, lambda s: 0), _rows2(bcu, tt, D, lambda s: 1), _rows2(bcu, tt, D, lambda s: 2),
                 _prev8_2(bcu, tt, D, 1), _prev8_2(bcu, tt, D, 2), _full2(cw)], gated_tile,
                [_rows2(dh2, tt)], _val, D, D, T, tt))
            dbcu, d_conv[l] = _conv_bwd(f"l{l}_conv_bwd", bcu, dgated, cw, seq, hub=hub)
            G["w_a_in", l] = _tn(f"l{l}_dw_in", [_rows2(h_in, tt), _full2(g_mix)], _normed, [_rows2(dbcu, tt)], _val,
                                 D, 3 * D, T, tt, out_cols=w_in_l.shape[2])
            dh, d_norm_mix[l] = _nt_norm_bwd(f"l{l}_in_bwd", [dbcu], w_in_l, h_in, g_mix, dh2, hub=hub)

    small = jnp.concatenate(d_norm_mix + d_norm_mlp + [d_norm_kv, d_norm_final] + d_conv, axis=0)
    return sq_err, dh.reshape(bl, seq, D), small


def kernel(x, norm_mix, norm_mlp, w_a_in, conv_w, w_a_out, norm_kv, w_kv, w_q, w_o, w_up, w_down, norm_final, loss_target, m_norm_mix, m_norm_mlp, m_w_a_in, m_conv_w, m_w_a_out, m_norm_kv, m_w_kv, m_w_q, m_w_o, m_w_up, m_w_down, m_norm_final, v_norm_mix, v_norm_mlp, v_w_a_in, v_conv_w, v_w_a_out, v_norm_kv, v_w_kv, v_w_q, v_w_o, v_w_up, v_w_down, v_norm_final):
    D = x.shape[-1]
    xi, yi, ci = _mesh_pos()
    me_idx = 4 * xi + 2 * yi + ci
    w_big = dict(w_a_in=w_a_in, w_a_out=w_a_out, w_kv=w_kv, w_q=w_q, w_o=w_o, w_up=w_up, w_down=w_down)
    m_big = dict(w_a_in=m_w_a_in, w_a_out=m_w_a_out, w_kv=m_w_kv, w_q=m_w_q, w_o=m_w_o, w_up=m_w_up, w_down=m_w_down)
    v_big = dict(w_a_in=v_w_a_in, w_a_out=v_w_a_out, w_kv=v_w_kv, w_q=v_w_q, w_o=v_w_o, w_up=v_w_up, w_down=v_w_down)
    names = list(w_big)

    shards = {n: w.astype(BF16) for n, w in w_big.items()}
    landing = {n: lax.empty((N_DEV,) + w.shape, BF16) for n, w in w_big.items()}
    hub = _Hub(FETCH_DURING, PUSH_DURING, shards, landing)
    dc = conv_w.shape[-1]
    taps = conv_w.shape[0] * conv_w.shape[1]
    now = ["w_a_in", "w_a_out", "w_up", "w_down"]
    got = _all_gather("gather_layer0", [(shards[n], 0) for n in now] + [(_pad8(conv_w.reshape(taps, dc)), None)])
    for n, w in zip(now, got):
        hub.weights[n, 0] = w
    conv_full = jnp.moveaxis(got[-1][:, :taps], 0, 1).reshape(conv_w.shape[0], conv_w.shape[1], N_DEV * dc)

    sq_err, grad_x, small = _local_grads(x, loss_target, norm_mix, norm_mlp, norm_kv, norm_final, conv_full, hub)
    loss = lax.psum(sq_err * (0.5 / D), ("x", "y", "c"))

    grads, deltas, new_m, new_v = {}, {}, {}, {}
    for n in names:
        shape = w_big[n].shape
        cols = shape[-1]
        flat = lambda t: t.reshape(-1, cols)
        parts = hub.landing[n].reshape(N_DEV, -1, cols)
        outs = _sum8_adamw(f"adamw_{n}", parts, flat(w_big[n]), flat(m_big[n]), flat(v_big[n]), tr=min(256, parts.shape[1]))
        grads[n], deltas[n], new_m[n], new_v[n] = (t.reshape(shape) for t in outs)

    n_gain = 2 * DEPTH + 2
    rows_small = small.shape[0]
    small_all = _all_gather("gather_small_grads", [(small, None)])[0]

    def small_pack(nm, nl, nk, nf, cw):
        gains = jnp.concatenate([nm, nl, nk.reshape(1, D), nf.reshape(1, D)], axis=0)
        taps_full = lax.dynamic_update_slice(jnp.zeros((taps, D), F32), cw.reshape(taps, dc), (0, me_idx * dc))
        return jnp.concatenate([gains, taps_full], axis=0)

    sp = [small_pack(*t) for t in ((norm_mix, norm_mlp, norm_kv, norm_final, conv_w),
                                   (m_norm_mix, m_norm_mlp, m_norm_kv, m_norm_final, m_conv_w),
                                   (v_norm_mix, v_norm_mlp, v_norm_kv, v_norm_final, v_conv_w))]
    small_out = _sum8_adamw("adamw_small", small_all, *sp, tr=rows_small)

    def small_unpack(t):
        res = dict(norm_mix=t[0:DEPTH], norm_mlp=t[DEPTH:2 * DEPTH], norm_kv=t[2 * DEPTH], norm_final=t[2 * DEPTH + 1])
        res["conv_w"] = lax.dynamic_slice(t[n_gain:], (0, me_idx * dc), (taps, dc)).reshape(conv_w.shape)
        return res

    for dst, t in zip((grads, deltas, new_m, new_v), small_out):
        dst.update(small_unpack(t))

    order = ["norm_mix", "norm_mlp", "w_a_in", "conv_w", "w_a_out", "norm_kv", "w_kv", "w_q", "w_o", "w_up", "w_down",
             "norm_final"]
    return (loss, grad_x, *[grads[n] for n in order], *[deltas[n] for n in order], *[new_m[n] for n in order],
            *[new_v[n] for n in order])
```

```python
import functools

import jax
import jax.numpy as jnp
from jax import lax
from jax.experimental import pallas as pl
from jax.experimental.pallas import tpu as pltpu

F32 = jnp.float32
BF16 = jnp.bfloat16
SDS = jax.ShapeDtypeStruct

EPS = 1e-5
N_A_LAYERS = 2
DEPTH = 4
PATTERNS = ((128, 1), (512, 4), (2048, 16))
N_GROUPS = 3
H_G = 8
HEAD_DIM = 64
QW = H_G * HEAD_DIM
ATT_BLK = 128
ALIBI_MAX_BIAS = 8.0
NEG_INF = -1e30

ADAM_LR = 0.001
ADAM_B1 = 0.9
ADAM_B2 = 0.999
ADAM_EPS = 1e-08
ADAM_WD = 0.01
ADAM_STEP = 10

N_DEV = 8
SUBLANES = 8
HALO = 16
V7X_VMEM_LIMIT = 48 * 1024 * 1024
MM_CHUNK = 512

FETCH_DURING = {
    "l0_in": [("w_a_in", 1)], "l0_conv_out": [("w_a_out", 1)], "l0_up": [("w_up", 1)], "l0_down": [("w_down", 1)],
    "l1_in": [("w_kv", None)], "l1_conv_out": [("w_q", 0), ("w_o", 0)], "l1_up": [("w_up", 2)], "l1_down": [("w_down", 2)],
    "kv": [("w_q", 1), ("w_o", 1)], "l2_up": [("w_up", 3)], "l2_down": [("w_down", 3)],
}
PUSH_DURING = {
    "l3_dw_up": [("w_down", 3)], "l3_up_bwd": [("w_up", 3)], "l3_att0_dq": [("w_o", 1)], "l3_q_bwd": [("w_q", 1)],
    "l2_dw_up": [("w_down", 2)], "l2_up_bwd": [("w_up", 2)], "l2_att0_dq": [("w_o", 0)], "l2_q_bwd": [("w_q", 0)],
    "kv_bwd": [("w_kv", None)],
    "l1_dw_up": [("w_down", 1)], "l1_up_bwd": [("w_up", 1)], "l1_conv_bwd": [("w_a_out", 1)], "l1_in_bwd": [("w_a_in", 1)],
    "l0_dw_up": [("w_down", 0)], "l0_up_bwd": [("w_up", 0)], "l0_conv_bwd": [("w_a_out", 0)], "l0_in_bwd": [("w_a_in", 0)],
}


def _mesh_pos():
    return lax.axis_index("x"), lax.axis_index("y"), lax.axis_index("c")


def _flip(v, bit):
    return 1 - v if bit else v


class _Transfer:
    def __init__(self, kind, key, src, src_idx=None, dst=None, dst_idx=None, dst_shape=None):
        self.kind, self.key, self.src, self.src_idx = kind, key, src, src_idx
        self.dst, self.dst_idx, self.dst_shape = dst, dst_idx, dst_shape

    def copies(self, src_ref, dst_ref, send_sems, recv_sems, local_sem):
        x, y, c = _mesh_pos()
        me = 4 * x + 2 * y + c

        def src_slot(j):
            if self.kind == "exchange":
                return src_ref.at[j]
            return src_ref if self.src_idx is None else src_ref.at[self.src_idx]

        def dst_slot(j):
            r = dst_ref.at[j]
            return r if self.dst_idx is None else r.at[self.dst_idx]

        local = pltpu.make_async_copy(src_slot(me), dst_slot(me), local_sem)
        sends, recvs = [], []
        for k in range(1, N_DEV):
            peer = (_flip(x, k & 4), _flip(y, k & 2), _flip(c, k & 1))
            peer_idx = 4 * peer[0] + 2 * peer[1] + peer[2]
            for dst_j, out in ((me, sends), (peer_idx, recvs)):
                out.append(pltpu.make_async_remote_copy(
                    src_ref=src_slot(peer_idx), dst_ref=dst_slot(dst_j), send_sem=send_sems.at[k - 1],
                    recv_sem=recv_sems.at[k - 1], device_id=peer, device_id_type=pl.DeviceIdType.MESH))
        return local, sends, recvs


class _Hub:
    def __init__(self, fetch, push, shards, landing):
        self.fetch, self.push, self.shards, self.landing = fetch, push, shards, landing
        self.weights = {}
        self.grads = {}

    def transfers(self, host):
        out = []
        for name, l in self.fetch.get(host, ()):
            src = self.shards[name]
            shard = src.shape if l is None else src.shape[1:]
            out.append(_Transfer("gather", (name, l), src, src_idx=l, dst_shape=(N_DEV,) + tuple(shard)))
        for name, l in self.push.get(host, ()):
            out.append(_Transfer("exchange", (name, l), self.grads.pop((name, l)), dst=self.landing[name], dst_idx=l))
        return out

    def accept(self, transfers, results):
        for t, r in zip(transfers, results):
            if t.kind == "gather":
                self.weights[t.key] = r
            else:
                self.landing[t.key[0]] = r


def _call(name, body, grid, ins, outs, scratch=(), hub=None):
    transfers = hub.transfers(name) if hub is not None else []
    n_in, n_out, n_scr, n_tr = len(ins), len(outs), len(scratch), len(transfers)
    c_in, c_out, aliases, places = [], [], {}, []
    for t in transfers:
        c_in.append(t.src)
        src_pos = len(c_in) - 1
        if t.dst is not None:
            c_in.append(t.dst)
            aliases[n_in + len(c_in) - 1] = n_out + len(c_out)
            c_out.append(SDS(t.dst.shape, t.dst.dtype))
        else:
            c_out.append(SDS(t.dst_shape, t.src.dtype))
        places.append((src_pos, len(c_out) - 1))
    sems = [pltpu.SemaphoreType.DMA((n_tr, N_DEV - 1)), pltpu.SemaphoreType.DMA((n_tr, N_DEV - 1)),
            pltpu.SemaphoreType.DMA((n_tr,))] if n_tr else []

    def wrapped(*refs):
        in_refs = refs[:n_in]
        cin_refs = refs[n_in:n_in + len(c_in)]
        o0 = n_in + len(c_in)
        out_refs = refs[o0:o0 + n_out]
        cout_refs = refs[o0 + n_out:o0 + n_out + len(c_out)]
        s0 = o0 + n_out + len(c_out)
        scr_refs = refs[s0:s0 + n_scr]
        if n_tr:
            send_sems, recv_sems, local_sems = refs[s0 + n_scr:]
            first = last = None
            for ax, n in enumerate(grid):
                i = pl.program_id(ax)
                first = (i == 0) if first is None else first & (i == 0)
                last = (i == n - 1) if last is None else last & (i == n - 1)

            def all_copies():
                return [t.copies(cin_refs[sp], cout_refs[dp], send_sems.at[n], recv_sems.at[n], local_sems.at[n])
                        for n, (t, (sp, dp)) in enumerate(zip(transfers, places))]

            @pl.when(first)
            def _():
                for local, sends, _ in all_copies():
                    local.start()
                    for cp in sends:
                        cp.start()

        body(*in_refs, *out_refs, *scr_refs)

        if n_tr:
            @pl.when(last)
            def _():
                for local, sends, recvs in all_copies():
                    for cp in recvs:
                        cp.wait_recv()
                    for cp in sends:
                        cp.wait_send()
                    local.wait()

    any_spec = pl.BlockSpec(memory_space=pl.ANY)
    res = pl.pallas_call(
        wrapped,
        name=name,
        grid=grid,
        in_specs=[s for _, s in ins] + [any_spec] * len(c_in),
        out_specs=[s for _, s in outs] + [any_spec] * len(c_out),
        out_shape=[o for o, _ in outs] + c_out,
        scratch_shapes=list(scratch) + sems,
        input_output_aliases=aliases,
        compiler_params=pltpu.CompilerParams(
            dimension_semantics=("arbitrary",) * len(grid), vmem_limit_bytes=V7X_VMEM_LIMIT),
    )(*[a for a, _ in ins], *c_in)
    if n_tr:
        hub.accept(transfers, res[n_out:])
    return res[:n_out]


def _rows(a, tm, cb=None, col=0):
    cb = cb or a.shape[1]
    return (a, pl.BlockSpec((tm, cb), lambda i: (i, col)))


def _full(a):
    nd = a.ndim
    return (a, pl.BlockSpec(a.shape, lambda i: (0,) * nd))


def _prev8(a, tm, cb, col):
    return (a, pl.BlockSpec((HALO, cb), lambda i: (jnp.maximum(i * (tm // HALO) - 1, 0), col)))


def _next8(a, tm, cb, col):
    last = a.shape[0] // HALO - 1
    return (a, pl.BlockSpec((HALO, cb), lambda i: (jnp.minimum((i + 1) * (tm // HALO), last), col)))


def _rows2(a, tt, cb=None, colfn=None):
    cb = cb or a.shape[1]
    colfn = colfn or (lambda s: 0)
    return (a, pl.BlockSpec((tt, cb), lambda s, t: (t, colfn(s))))


def _full2(a):
    nd = a.ndim
    return (a, pl.BlockSpec(a.shape, lambda s, t: (0,) * nd))


def _prev8_2(a, tt, cb, col):
    return (a, pl.BlockSpec((HALO, cb), lambda s, t: (jnp.maximum(t * (tt // HALO) - 1, 0), col)))


def _out_rows(T, n, dtype, tm):
    return (SDS((T, n), dtype), pl.BlockSpec((tm, n), lambda i: (i, 0)))


def _out_acc8(d):
    return (SDS((SUBLANES, d), F32), pl.BlockSpec((SUBLANES, d), lambda i: (0, 0)))


def _rstd(x):
    return lax.rsqrt(jnp.mean(x * x, axis=-1, keepdims=True) + EPS)


def _normed(h_ref, g_ref):
    x = h_ref[...]
    return x * _rstd(x) * g_ref[...]


def _acc8(ref, val, i, n):
    part = val.reshape(-1, SUBLANES, val.shape[-1]).sum(axis=0)

    @pl.when(i == 0)
    def _():
        ref[...] = part

    @pl.when(i > 0)
    def _():
        ref[...] += part

    @pl.when(i == n - 1)
    def _():
        ref[...] = jnp.broadcast_to(jnp.sum(ref[...], axis=0, keepdims=True), ref.shape)


def _gate(b_ref, c_ref, u_ref, ch_ref, uh_ref, cw_ref, first):
    b, c, u = (r[...].astype(F32) for r in (b_ref, c_ref, u_ref))
    cu = c * u
    halo = jnp.where(first, 0.0, ch_ref[...].astype(F32) * uh_ref[...].astype(F32))
    rows = lax.broadcasted_iota(jnp.int32, cu.shape, 0)
    h1 = halo[HALO - 1:HALO, :]
    h2 = halo[HALO - 2:HALO - 1, :]
    cu1 = jnp.where(rows == 0, h1, pltpu.roll(cu, 1, 0))
    cu2 = jnp.where(rows == 0, h2, jnp.where(rows == 1, h1, pltpu.roll(cu, 2, 0)))
    conv = cw_ref[0:1, :] * cu + cw_ref[1:2, :] * cu1 + cw_ref[2:3, :] * cu2
    return b * conv, (b, c, u), conv, (cu, cu1, cu2)


def _relu2(a_ref):
    r = jnp.maximum(a_ref[...].astype(F32), 0.0)
    return r * r


def _dot(a, b):
    return jnp.dot(a, b, preferred_element_type=F32)


def _dot_nt(a, b):
    return lax.dot_general(a, b, (((1,), (1,)), ((), ())), preferred_element_type=F32)


def _dot_tn(a, b):
    return lax.dot_general(a, b, (((0,), (0,)), ((), ())), preferred_element_type=F32)


def _chunks(n):
    c = min(MM_CHUNK, n)
    assert n % c == 0, n
    return [(k * c, (k + 1) * c) for k in range(n // c)]


def _col_weight(w):
    _, K, ns = w.shape
    N = N_DEV * ns
    direct = ns % 128 == 0 and ns >= 256
    scratch = [] if direct else [pltpu.VMEM((K, N), BF16)]

    def prepare(w_ref, s_ref, step):
        if direct:
            return

        @pl.when(step == 0)
        def _():
            for j in range(N_DEV):
                s_ref[:, j * ns:(j + 1) * ns] = w_ref[j]

    def chunks(w_ref, s_ref):
        if direct:
            return [(j * ns, (j + 1) * ns, (lambda j=j: w_ref[j])) for j in range(N_DEV)]
        return [(lo, hi, (lambda lo=lo, hi=hi: s_ref[:, lo:hi])) for lo, hi in _chunks(N)]

    return N, scratch, prepare, chunks


def _norm_mm(name, h, g, w, tm=256, out_dtype=F32, hub=None):
    T, _ = h.shape
    N, w_scratch, prepare, chunks = _col_weight(w)

    def body(h_ref, g_ref, w_ref, o_ref, *s):
        s_ref = s[0] if s else None
        prepare(w_ref, s_ref, pl.program_id(0))
        a = _normed(h_ref, g_ref).astype(BF16)
        for lo, hi, load in chunks(w_ref, s_ref):
            o_ref[:, lo:hi] = _dot(a, load()).astype(out_dtype)

    return _call(name, body, (T // tm,), [_rows(h, tm), _full(g), _full(w)], [_out_rows(T, N, out_dtype, tm)],
                 scratch=w_scratch, hub=hub)[0]


def _gate_mm_res(name, bcu, cw, w, h, seq, tm=256, hub=None):
    T, D = h.shape

    def body(b_ref, c_ref, u_ref, ch_ref, uh_ref, cw_ref, w_ref, h_ref, o_ref):
        first = (pl.program_id(0) * tm) % seq == 0
        gated = _gate(b_ref, c_ref, u_ref, ch_ref, uh_ref, cw_ref, first)[0].astype(BF16)
        for lo, hi in _chunks(D):
            o_ref[:, lo:hi] = h_ref[:, lo:hi] + _dot(gated, w_ref[:, lo:hi])

    ins = [_rows(bcu, tm, D, 0), _rows(bcu, tm, D, 1), _rows(bcu, tm, D, 2), _prev8(bcu, tm, D, 1),
           _prev8(bcu, tm, D, 2), _full(cw), _full(w), _rows(h, tm)]
    return _call(name, body, (T // tm,), ins, [_out_rows(T, D, F32, tm)], hub=hub)[0]


def _relu2_mm_res(name, a, w, h, tm=256, hub=None):
    T, D = h.shape
    K = a.shape[1]

    def body(a_ref, w_ref, h_ref, o_ref, acc_ref):
        for n, (lo, hi) in enumerate(_chunks(K)):
            d = _dot(_relu2(a_ref.at[:, lo:hi]).astype(BF16), w_ref[lo:hi, :])
            if n == 0:
                acc_ref[...] = d
            else:
                acc_ref[...] += d
        o_ref[...] = h_ref[...] + acc_ref[...]

    return _call(name, body, (T // tm,), [_rows(a, tm), _full(w), _rows(h, tm)], [_out_rows(T, D, F32, tm)],
                 scratch=[pltpu.VMEM((tm, D), F32)], hub=hub)[0]


def _mm_res(name, a, w, h, tm=256):
    T, D = h.shape
    _, w_scratch, prepare, chunks = _col_weight(w)

    def body(a_ref, w_ref, h_ref, o_ref, *s):
        s_ref = s[0] if s else None
        prepare(w_ref, s_ref, pl.program_id(0))
        av = a_ref[...].astype(BF16)
        for lo, hi, load in chunks(w_ref, s_ref):
            o_ref[:, lo:hi] = h_ref[:, lo:hi] + _dot(av, load())

    return _call(name, body, (T // tm,), [_rows(a, tm), _full(w), _rows(h, tm)], [_out_rows(T, D, F32, tm)],
                 scratch=w_scratch)[0]


def _nt_relu2_bwd(name, dh, w, a, tm=256):
    T, _ = dh.shape
    K = w.shape[0]

    def body(dh_ref, w_ref, a_ref, o_ref):
        d = dh_ref[...].astype(BF16)
        for lo, hi in _chunks(K):
            dr = _dot_nt(d, w_ref[lo:hi, :])
            o_ref[:, lo:hi] = (dr * (2.0 * jnp.maximum(a_ref[:, lo:hi].astype(F32), 0.0))).astype(BF16)

    return _call(name, body, (T // tm,), [_rows(dh, tm), _full(w), _rows(a, tm)], [_out_rows(T, K, BF16, tm)])[0]


def _concat_bf16(*refs):
    vals = [r[...].astype(BF16) for r in refs]
    return vals[0] if len(vals) == 1 else jnp.concatenate(vals, axis=1)


def _nt_plain(name, dy, w, tm=256):
    T, N = dy.shape
    if w.ndim == 3:
        K = w.shape[1]
        _, w_scratch, prepare, chunks = _col_weight(w)
    else:
        K = w.shape[0]
        w_scratch, prepare = [], (lambda w_ref, s_ref, step: None)
        chunks = lambda w_ref, s_ref: [(lo, hi, (lambda lo=lo, hi=hi: w_ref[:, lo:hi])) for lo, hi in _chunks(N)]

    def body(dy_ref, w_ref, o_ref, acc_ref, *s):
        s_ref = s[0] if s else None
        prepare(w_ref, s_ref, pl.program_id(0))
        for n, (lo, hi, load) in enumerate(chunks(w_ref, s_ref)):
            d = _dot_nt(dy_ref[:, lo:hi].astype(BF16), load())
            if n == 0:
                acc_ref[...] = d
            else:
                acc_ref[...] += d
        o_ref[...] = acc_ref[...]

    return _call(name, body, (T // tm,), [_rows(dy, tm), _full(w)], [_out_rows(T, K, F32, tm)],
                 scratch=[pltpu.VMEM((tm, K), F32)] + w_scratch)[0]


def _att_out_bwd(name, dy, w, o, tm=256):
    T, _ = dy.shape
    K = w.shape[1]
    _, w_scratch, prepare, chunks = _col_weight(w)

    def body(dy_ref, w_ref, o_ref, do_ref, dl_ref, acc_ref, *s):
        s_ref = s[0] if s else None
        prepare(w_ref, s_ref, pl.program_id(0))
        for n, (lo, hi, load) in enumerate(chunks(w_ref, s_ref)):
            d = _dot_nt(dy_ref[:, lo:hi].astype(BF16), load())
            if n == 0:
                acc_ref[...] = d
            else:
                acc_ref[...] += d
        do = acc_ref[...]
        do_ref[...] = do
        prod = do * o_ref[...]
        high = prod.astype(BF16)
        low = (prod - high.astype(F32)).astype(BF16)
        head_of = lambda axis: jnp.right_shift(lax.broadcasted_iota(jnp.int32, (K, K), axis), HEAD_DIM.bit_length() - 1)
        same_head = jnp.where(head_of(0) == head_of(1), 1.0, 0.0).astype(BF16)
        dl_ref[...] = _dot(high, same_head) + _dot(low, same_head)

    outs = [_out_rows(T, K, F32, tm), _out_rows(T, K, F32, tm)]
    return _call(name, body, (T // tm,), [_rows(dy, tm), _full(w), _rows(o, tm)], outs,
                 scratch=[pltpu.VMEM((tm, K), F32)] + w_scratch)


def _nt_norm_bwd(name, dys, w, h, g, dh_in, tm=256, hub=None):
    T, D = h.shape
    _, w_scratch, prepare, chunks = _col_weight(w)
    n_steps = T // tm
    n_dy = len(dys)

    def body(*refs):
        dy_refs = refs[:n_dy]
        w_ref, h_ref, g_ref, dhin_ref, o_ref, dg_ref, acc_ref = refs[n_dy:n_dy + 7]
        s_ref = refs[n_dy + 7] if len(refs) > n_dy + 7 else None
        i = pl.program_id(0)
        prepare(w_ref, s_ref, i)
        dy = _concat_bf16(*dy_refs)
        for n, (lo, hi, load) in enumerate(chunks(w_ref, s_ref)):
            d = _dot_nt(dy[:, lo:hi], load())
            if n == 0:
                acc_ref[...] = d
            else:
                acc_ref[...] += d
        dn = acc_ref[...]
        x = h_ref[...]
        rstd = _rstd(x)
        xhat = x * rstd
        dxhat = dn * g_ref[...]
        dx = rstd * (dxhat - xhat * jnp.mean(dxhat * xhat, axis=-1, keepdims=True))
        o_ref[...] = dhin_ref[...] + dx
        _acc8(dg_ref, dn * xhat, i, n_steps)

    ins = [_rows(d, tm) for d in dys] + [_full(w), _rows(h, tm), _full(g), _rows(dh_in, tm)]
    outs = [_out_rows(T, D, F32, tm), _out_acc8(D)]
    dh, dg = _call(name, body, (n_steps,), ins, outs, scratch=[pltpu.VMEM((tm, D), F32)] + w_scratch, hub=hub)
    return dh, dg[0:1]


def _tn(name, a_ins, a_fn, y_ins, y_fn, K, N, T, tt, split=None, out_cols=None, hub=None):
    kind, parts = split or ("n", 1)
    kb, nb = (K // parts, N) if kind == "k" else (K, N // parts)
    n_steps = T // tt
    n_a = len(a_ins)
    n_y = len(y_ins)
    assert out_cols is None or (kind == "n" and nb % out_cols == 0)

    def body(*refs):
        a_refs = refs[:n_a]
        y_refs = refs[n_a:n_a + n_y]
        o_ref, acc_ref = refs[n_a + n_y:]
        t = pl.program_id(1)
        a = a_fn(*a_refs).astype(BF16)
        y = y_fn(*y_refs).astype(BF16)
        for lo, hi in _chunks(nb):
            d = _dot_tn(a, y[:, lo:hi])

            @pl.when(t == 0)
            def _():
                acc_ref[:, lo:hi] = d

            @pl.when(t > 0)
            def _():
                acc_ref[:, lo:hi] += d

        @pl.when(t == n_steps - 1)
        def _():
            if out_cols is None:
                o_ref[...] = acc_ref[...].astype(BF16)
            else:
                for j in range(nb // out_cols):
                    o_ref[j] = acc_ref[:, j * out_cols:(j + 1) * out_cols].astype(BF16)

    if out_cols is None:
        out = (SDS((K, N), BF16), pl.BlockSpec((kb, nb), (lambda s, t: (s, 0)) if kind == "k" else (lambda s, t: (0, s))))
    else:
        out = (SDS((N // out_cols, K, out_cols), BF16), pl.BlockSpec((nb // out_cols, K, out_cols), lambda s, t: (s, 0, 0)))
    return _call(name, body, (parts, n_steps), list(a_ins) + list(y_ins), [out],
                 scratch=[pltpu.VMEM((kb, nb), F32)], hub=hub)[0]


def _val(ref):
    return ref[...]


def _concat_f32(*refs):
    vals = [r[...] for r in refs]
    return vals[0] if len(vals) == 1 else jnp.concatenate(vals, axis=1)


ATT_TILE_ROWS = 512
HEAD_PAIRS = H_G // 2
ATT_SCALE = HEAD_DIM ** -0.5
ATT_UNITS_TOGETHER = 4


def _slope(h):
    return 2.0 ** (-ALIBI_MAX_BIAS * (h + 1) / H_G)


def _att_geom(T, bl, g):
    dil = PATTERNS[g][1]
    sub = ATT_BLK * dil
    nsub = max(1, ATT_TILE_ROWS // sub)
    rows = sub * nsub
    return dil, sub, nsub, rows, T // bl // rows


def _att_specs(T, bl, g):
    _, sub, nsub, rows, nt = _att_geom(T, bl, g)
    last_sub = T // sub - 1
    tile = lambda col: pl.BlockSpec((rows, 128), lambda b, i, hp: (b * nt + i, col(hp)))
    prev = lambda col: pl.BlockSpec((sub, 128), lambda b, i, hp: (jnp.maximum((b * nt + i) * nsub - 1, 0), col(hp)))
    nxt = lambda col: pl.BlockSpec((sub, 128), lambda b, i, hp: (jnp.minimum((b * nt + i + 1) * nsub, last_sub), col(hp)))
    return tile, prev, nxt


def _sub_rows(j, r, dil):
    start = j * ATT_BLK * dil + r
    return pl.ds(start, ATT_BLK, stride=dil) if dil > 1 else pl.ds(start, ATT_BLK)


def _att_consts(hp, dil):
    h0 = lax.broadcasted_iota(jnp.int32, (ATT_BLK, 128), 1) < HEAD_DIM
    a = lax.broadcasted_iota(jnp.int32, (ATT_BLK, ATT_BLK), 0)
    c = lax.broadcasted_iota(jnp.int32, (ATT_BLK, ATT_BLK), 1)
    dist_p = ((ATT_BLK + a - c) * dil).astype(F32)
    dist_c = ((a - c) * dil).astype(F32)
    bias_p, bias_c = [], []
    for h in range(2):
        slope = jnp.float32(_slope(2 * (HEAD_PAIRS - 1) + h))
        for p in range(HEAD_PAIRS - 2, -1, -1):
            slope = jnp.where(hp == p, jnp.float32(_slope(2 * p + h)), slope)
        bias_p.append(jnp.where(c >= a, -slope * dist_p, NEG_INF))
        bias_c.append(jnp.where(c <= a, -slope * dist_c, NEG_INF))
    return h0, bias_p, bias_c


def _split_heads(x, h0):
    return [jnp.where(h0, x, 0.0).astype(BF16), jnp.where(h0, 0.0, x).astype(BF16)]


def _head_cols(x):
    return [x[:, 0:1], x[:, HEAD_DIM:HEAD_DIM + 1]]


def _in_groups(units, first_stage, *later_stages):
    for u0 in range(0, len(units), ATT_UNITS_TOGETHER):
        staged = [first_stage(*u) for u in units[u0:u0 + ATT_UNITS_TOGETHER]]
        for stage in later_stages:
            staged = [stage(*s) for s in staged]


def _attn_fwd(name, q, kv, g, bl):
    T = q.shape[0]
    dil, _, nsub, _, _ = _att_geom(T, bl, g)
    tile, prev, _ = _att_specs(T, bl, g)

    def body(q_ref, kp_ref, kc_ref, vp_ref, vc_ref, o_ref, lse_ref):
        first = pl.program_id(1) == 0
        h0, bias_p, bias_c = _att_consts(pl.program_id(2), dil)
        bias_first = [jnp.where(first, NEG_INF, b) for b in bias_p]
        ones = jnp.ones((ATT_BLK, 128), BF16)

        def scores(j, r):
            cur = _sub_rows(j, r, dil)
            if j == 0:
                before = _sub_rows(0, r, dil)
                kp, vp, bp = kp_ref[before, :], vp_ref[before, :], bias_first
            else:
                before = _sub_rows(j - 1, r, dil)
                kp, vp, bp = kc_ref[before, :], vc_ref[before, :], bias_p
            kp, kc = kp.astype(BF16), kc_ref[cur, :].astype(BF16)
            qh = _split_heads(q_ref[cur, :] * ATT_SCALE, h0)
            sp = [_dot_nt(qh[h], kp) + bp[h] for h in range(2)]
            sc = [_dot_nt(qh[h], kc) + bias_c[h] for h in range(2)]
            return cur, sp, sc, vp.astype(BF16), vc_ref[cur, :].astype(BF16)

        def weights(cur, sp, sc, vp, vc):
            mx = [jnp.max(jnp.maximum(sp[h], sc[h]), axis=-1, keepdims=True) for h in range(2)]
            ep = [jnp.exp(sp[h] - mx[h]).astype(BF16) for h in range(2)]
            ec = [jnp.exp(sc[h] - mx[h]).astype(BF16) for h in range(2)]
            return cur, mx, ep, ec, vp, vc

        def outputs(cur, mx, ep, ec, vp, vc):
            den = [_dot(ep[h], ones) + _dot(ec[h], ones) for h in range(2)]
            acc = [_dot(ep[h], vp) + _dot(ec[h], vc) for h in range(2)]
            o_ref[cur, :] = jnp.where(h0, acc[0] / den[0], acc[1] / den[1])
            lse_ref[cur, :] = jnp.where(h0, mx[0] + jnp.log(den[0]), mx[1] + jnp.log(den[1]))
            return ()

        _in_groups([(j, r) for j in range(nsub) for r in range(dil)], scores, weights, outputs)

    ins = [(q, tile(lambda hp: 4 * g + hp)), (kv, prev(lambda hp: 8 * g + hp)), (kv, tile(lambda hp: 8 * g + hp)),
           (kv, prev(lambda hp: 8 * g + 4 + hp)), (kv, tile(lambda hp: 8 * g + 4 + hp))]
    out = (SDS((T, QW), F32), tile(lambda hp: hp))
    _, _, _, _, nt = _att_geom(T, bl, g)
    return _call(name, body, (bl, nt, HEAD_PAIRS), ins, [out, out])


def _combine(name, os_, lses, tm=512):
    T = os_[0].shape[0]

    def body(o0, o1, o2, l0, l1, l2, o_ref, lse_ref):
        ls = [l0[...], l1[...], l2[...]]
        mx = jnp.maximum(jnp.maximum(ls[0], ls[1]), ls[2])
        es = [jnp.exp(l - mx) for l in ls]
        den = es[0] + es[1] + es[2]
        o_ref[...] = (es[0] * o0[...] + es[1] * o1[...] + es[2] * o2[...]) / den
        lse_ref[...] = mx + jnp.log(den)

    ins = [_rows(t, tm) for t in list(os_) + list(lses)]
    return _call(name, body, (T // tm,), ins, [_out_rows(T, QW, F32, tm), _out_rows(T, QW, F32, tm)])


def _attn_bwd_dq(name, q, kv, do, delta, lse, g, bl, hub=None):
    T = q.shape[0]
    dil, _, nsub, _, nt = _att_geom(T, bl, g)
    tile, prev, _ = _att_specs(T, bl, g)

    def body(q_ref, kp_ref, kc_ref, vp_ref, vc_ref, do_ref, dl_ref, lse_ref, dq_ref):
        first = pl.program_id(1) == 0
        h0, bias_p, bias_c = _att_consts(pl.program_id(2), dil)
        bias_first = [jnp.where(first, NEG_INF, b) for b in bias_p]

        def probs(j, r):
            cur = _sub_rows(j, r, dil)
            if j == 0:
                before = _sub_rows(0, r, dil)
                kp, vp, bp = kp_ref[before, :], vp_ref[before, :], bias_first
            else:
                before = _sub_rows(j - 1, r, dil)
                kp, vp, bp = kc_ref[before, :], vc_ref[before, :], bias_p
            kp, vp = kp.astype(BF16), vp.astype(BF16)
            kc, vc = kc_ref[cur, :].astype(BF16), vc_ref[cur, :].astype(BF16)
            qh = _split_heads(q_ref[cur, :] * ATT_SCALE, h0)
            dob = _split_heads(do_ref[cur, :], h0)
            lse_h = _head_cols(lse_ref[cur, :])
            pp = [jnp.exp(_dot_nt(qh[h], kp) + bp[h] - lse_h[h]) for h in range(2)]
            pc = [jnp.exp(_dot_nt(qh[h], kc) + bias_c[h] - lse_h[h]) for h in range(2)]
            dpp = [_dot_nt(dob[h], vp) for h in range(2)]
            dpc = [_dot_nt(dob[h], vc) for h in range(2)]
            return cur, pp, pc, dpp, dpc, kp, kc

        def dscores(cur, pp, pc, dpp, dpc, kp, kc):
            dl = _head_cols(dl_ref[cur, :])
            dsp = [(pp[h] * (dpp[h] - dl[h])).astype(BF16) for h in range(2)]
            dsc = [(pc[h] * (dpc[h] - dl[h])).astype(BF16) for h in range(2)]
            return cur, dsp, dsc, kp, kc

        def outputs(cur, dsp, dsc, kp, kc):
            dqh = [_dot(dsp[h], kp) + _dot(dsc[h], kc) for h in range(2)]
            dq_ref[cur, :] = jnp.where(h0, dqh[0], dqh[1]) * ATT_SCALE
            return ()

        _in_groups([(j, r) for j in range(nsub) for r in range(dil)], probs, dscores, outputs)

    own = lambda hp: hp
    ins = [(q, tile(lambda hp: 4 * g + hp)), (kv, prev(lambda hp: 8 * g + hp)), (kv, tile(lambda hp: 8 * g + hp)),
           (kv, prev(lambda hp: 8 * g + 4 + hp)), (kv, tile(lambda hp: 8 * g + 4 + hp)),
           (do, tile(own)), (delta, tile(own)), (lse, tile(own))]
    return _call(name, body, (bl, nt, HEAD_PAIRS), ins, [(SDS((T, QW), F32), tile(own))], hub=hub)[0]


def _attn_bwd_dkv(name, q, kv, do, delta, lse, g, bl, prev=None):
    T = q.shape[0]
    dil, _, nsub, _, nt = _att_geom(T, bl, g)
    tile, _, nxt = _att_specs(T, bl, g)
    has_prev = prev is not None

    def body(*refs):
        k_ref, v_ref, q_ref, qn_ref, do_ref, don_ref, dl_ref, dln_ref, l_ref, ln_ref = refs[:10]
        rest = refs[10:]
        if has_prev:
            dkp_ref, dvp_ref, dk_ref, dv_ref = rest
        else:
            dk_ref, dv_ref = rest
        last = pl.program_id(1) == nt - 1
        h0, bias_p, bias_c = _att_consts(pl.program_id(2), dil)
        bias_last = [jnp.where(last, NEG_INF, b) for b in bias_p]

        def probs(j, r):
            cur = _sub_rows(j, r, dil)
            kb, vb = k_ref[cur, :].astype(BF16), v_ref[cur, :].astype(BF16)
            sets = [(q_ref, do_ref, dl_ref, l_ref, cur, bias_c)]
            if j < nsub - 1:
                sets.append((q_ref, do_ref, dl_ref, l_ref, _sub_rows(j + 1, r, dil), bias_p))
            else:
                sets.append((qn_ref, don_ref, dln_ref, ln_ref, _sub_rows(0, r, dil), bias_last))
            out = []
            for qr, dor, dlr, lr, rows, bias in sets:
                qs = qr[rows, :] * ATT_SCALE
                do2 = dor[rows, :]
                qh = _split_heads(qs, h0)
                dob = _split_heads(do2, h0)
                lse_h = _head_cols(lr[rows, :])
                p = [jnp.exp(_dot_nt(qh[h], kb) + bias[h] - lse_h[h]) for h in range(2)]
                dp = [_dot_nt(dob[h], vb) for h in range(2)]
                out.append((p, dp, dlr, rows, qs.astype(BF16), do2.astype(BF16)))
            return cur, out

        def dscores(cur, sets):
            out = []
            for p, dp, dlr, rows, qsb, do2b in sets:
                dl = _head_cols(dlr[rows, :])
                ds = [(p[h] * (dp[h] - dl[h])).astype(BF16) for h in range(2)]
                out.append(([p[h].astype(BF16) for h in range(2)], ds, qsb, do2b))
            return cur, out

        def outputs(cur, sets):
            dk = [None, None]
            dv = [None, None]
            for pb, ds, qsb, do2b in sets:
                for h in range(2):
                    dvh = _dot_tn(pb[h], do2b)
                    dkh = _dot_tn(ds[h], qsb)
                    dv[h] = dvh if dv[h] is None else dv[h] + dvh
                    dk[h] = dkh if dk[h] is None else dk[h] + dkh
            dk2 = jnp.where(h0, dk[0], dk[1])
            dv2 = jnp.where(h0, dv[0], dv[1])
            if has_prev:
                dk2 = dk2 + dkp_ref[cur, :]
                dv2 = dv2 + dvp_ref[cur, :]
            dk_ref[cur, :] = dk2
            dv_ref[cur, :] = dv2
            return ()

        _in_groups([(j, r) for j in range(nsub) for r in range(dil)], probs, dscores, outputs)

    own = lambda hp: hp
    qcol = lambda hp: 4 * g + hp
    ins = [(kv, tile(lambda hp: 8 * g + hp)), (kv, tile(lambda hp: 8 * g + 4 + hp)), (q, tile(qcol)), (q, nxt(qcol)),
           (do, tile(own)), (do, nxt(own)), (delta, tile(own)), (delta, nxt(own)), (lse, tile(own)), (lse, nxt(own))]
    if has_prev:
        ins += [(prev[0], tile(own)), (prev[1], tile(own))]
    out = (SDS((T, QW), F32), tile(own))
    return _call(name, body, (bl, nt, HEAD_PAIRS), ins, [out, out])


def _final_loss(name, h, tgt, g, tm=256):
    T, D = h.shape
    n_steps = T // tm

    def body(h_ref, t_ref, g_ref, dh_ref, loss_ref, dg_ref, sq_ref):
        i = pl.program_id(0)
        x = h_ref[...]
        rstd = _rstd(x)
        xhat = x * rstd
        err = xhat * g_ref[...] - t_ref[...]
        _acc8(sq_ref, err * err, i, n_steps)
        dy = err * (1.0 / D)
        dxhat = dy * g_ref[...]
        dh_ref[...] = rstd * (dxhat - xhat * jnp.mean(dxhat * xhat, axis=-1, keepdims=True))
        _acc8(dg_ref, dy * xhat, i, n_steps)

        @pl.when(i == n_steps - 1)
        def _():
            loss_ref[...] = jnp.full(loss_ref.shape, jnp.sum(sq_ref[0:1, :]), F32)

    outs = [_out_rows(T, D, F32, tm), (SDS((SUBLANES, 128), F32), pl.BlockSpec((SUBLANES, 128), lambda i: (0, 0))),
            _out_acc8(D)]
    dh, loss, dg = _call(name, body, (n_steps,), [_rows(h, tm), _rows(tgt, tm), _full(g)], outs,
                         scratch=[pltpu.VMEM((SUBLANES, D), F32)])
    return dh, loss[0, 0], dg[0:1]


def _conv_bwd(name, bcu, dgated, cw, seq, tm=256, hub=None):
    T, D = dgated.shape
    n_steps = T // tm

    def body(b_ref, c_ref, u_ref, ch_ref, uh_ref, dg_ref, dgn_ref, bn_ref, cw_ref, o_ref, t0_ref, t1_ref, t2_ref):
        i = pl.program_id(0)
        first = (i * tm) % seq == 0
        last = ((i + 1) * tm) % seq == 0
        _, (b, c, u), conv, (cu, cu1, cu2) = _gate(b_ref, c_ref, u_ref, ch_ref, uh_ref, cw_ref, first)
        dgat = dg_ref[...]
        dconv = dgat * b
        nxt = jnp.where(last, 0.0, dgn_ref[...] * bn_ref[...].astype(F32))
        rows = lax.broadcasted_iota(jnp.int32, dconv.shape, 0)
        n1 = nxt[0:1, :]
        n2 = nxt[1:2, :]
        dc1 = jnp.where(rows == tm - 1, n1, pltpu.roll(dconv, tm - 1, 0))
        dc2 = jnp.where(rows == tm - 1, n2, jnp.where(rows == tm - 2, n1, pltpu.roll(dconv, tm - 2, 0)))
        dcu = cw_ref[0:1, :] * dconv + cw_ref[1:2, :] * dc1 + cw_ref[2:3, :] * dc2
        o_ref[:, 0:D] = (dgat * conv).astype(BF16)
        o_ref[:, D:2 * D] = (dcu * u).astype(BF16)
        o_ref[:, 2 * D:3 * D] = (dcu * c).astype(BF16)
        _acc8(t0_ref, dconv * cu, i, n_steps)
        _acc8(t1_ref, dconv * cu1, i, n_steps)
        _acc8(t2_ref, dconv * cu2, i, n_steps)

    ins = [_rows(bcu, tm, D, 0), _rows(bcu, tm, D, 1), _rows(bcu, tm, D, 2), _prev8(bcu, tm, D, 1), _prev8(bcu, tm, D, 2),
           _rows(dgated, tm), _next8(dgated, tm, D, 0), _next8(bcu, tm, D, 0), _full(cw)]
    outs = [_out_rows(T, 3 * D, BF16, tm), _out_acc8(D), _out_acc8(D), _out_acc8(D)]
    dbcu, t0, t1, t2 = _call(name, body, (n_steps,), ins, outs, hub=hub)
    return dbcu, jnp.concatenate([t0[0:1], t1[0:1], t2[0:1]], axis=0)


def _sum8_adamw(name, parts, w, m, v, tr):
    R, C = w.shape
    b1c = 1.0 - ADAM_B1 ** ADAM_STEP
    b2c = 1.0 - ADAM_B2 ** ADAM_STEP

    def body(p_ref, w_ref, m_ref, v_ref, g_ref, d_ref, nm_ref, nv_ref):
        g = p_ref[0].astype(F32)
        for j in range(1, N_DEV):
            g = g + p_ref[j].astype(F32)
        nm = ADAM_B1 * m_ref[...] + (1.0 - ADAM_B1) * g
        nv = ADAM_B2 * v_ref[...] + (1.0 - ADAM_B2) * (g * g)
        m_hat = nm / b1c
        v_hat = nv / b2c
        g_ref[...] = g
        d_ref[...] = -ADAM_LR * (m_hat / (jnp.sqrt(v_hat) + ADAM_EPS) + ADAM_WD * w_ref[...])
        nm_ref[...] = nm
        nv_ref[...] = nv

    ins = [(parts, pl.BlockSpec((N_DEV, tr, C), lambda i: (0, i, 0))), _rows(w, tr), _rows(m, tr), _rows(v, tr)]
    outs = [_out_rows(R, C, F32, tr)] * 4
    return _call(name, body, (R // tr,), ins, outs)


def _all_gather(name, items):
    n = len(items)
    shapes = [tuple(a.shape if idx is None else a.shape[1:]) for a, idx in items]

    def body(*refs):
        x_refs, out_refs = refs[:n], refs[n:2 * n]
        send_sems, recv_sems, local_sems = refs[2 * n:]
        x, y, c = _mesh_pos()
        me, sibling = (x, y, c), (x, y, 1 - c)
        chips = [(1 - x, y), (x, 1 - y), (1 - x, 1 - y)]

        def copy(t, k, block, to, own=False):
            dst = out_refs[t].at[4 * block[0] + 2 * block[1] + block[2]]
            src = dst
            if own:
                src = x_refs[t] if items[t][1] is None else x_refs[t].at[items[t][1]]
            return pltpu.make_async_remote_copy(
                src_ref=src, dst_ref=dst, send_sem=send_sems.at[t, k], recv_sem=recv_sems.at[t, k],
                device_id=to, device_id_type=pl.DeviceIdType.MESH)

        started = []
        for t in range(n):
            src = x_refs[t] if items[t][1] is None else x_refs[t].at[items[t][1]]
            mine = pltpu.make_async_copy(src, out_refs[t].at[4 * x + 2 * y + c], local_sems.at[t])
            mine.start()
            first = [copy(t, 0, me, sibling, own=True)]
            first += [copy(t, 1 + j, me, (*chip, c), own=True) for j, chip in enumerate(chips)]
            for cp in first:
                cp.start()
            started.append((mine, first))
        passed = []
        for t in range(n):
            for j, chip in enumerate(chips):
                copy(t, 1 + j, (*chip, c), me).wait_recv()
                fwd = copy(t, 4 + j, (*chip, c), sibling)
                fwd.start()
                passed.append(fwd)
        for t in range(n):
            copy(t, 0, sibling, me).wait_recv()
            for j, chip in enumerate(chips):
                copy(t, 4 + j, (*chip, 1 - c), me).wait_recv()
        for mine, first in started:
            for cp in first:
                cp.wait_send()
            mine.wait()
        for cp in passed:
            cp.wait_send()

    any_spec = pl.BlockSpec(memory_space=pl.ANY)
    return pl.pallas_call(
        body, name=name,
        out_shape=[SDS((N_DEV,) + s, a.dtype) for s, (a, _) in zip(shapes, items)],
        in_specs=[any_spec] * n,
        out_specs=[any_spec] * n,
        scratch_shapes=[pltpu.SemaphoreType.DMA((n, 7)), pltpu.SemaphoreType.DMA((n, 7)), pltpu.SemaphoreType.DMA((n,))],
    )(*[a for a, _ in items])


def _pad8(t):
    return jnp.pad(t, ((0, SUBLANES - t.shape[0]), (0, 0)))


def _rows_merged(w):
    return w.reshape(w.shape[0] * w.shape[1], w.shape[2])


def _local_grads(x, tgt, norm_mix, norm_mlp, norm_kv, norm_final, conv_w, hub):
    bl, seq, D = x.shape
    T = bl * seq
    h = x.reshape(T, D)
    tgt = tgt.reshape(T, D)
    row = lambda t, l: t[l:l + 1]
    W = hub.weights
    saved = []
    kv = h_kv = None
    for l in range(DEPTH):
        if l < N_A_LAYERS:
            bcu = _norm_mm(f"l{l}_in", h, row(norm_mix, l), W["w_a_in", l], out_dtype=BF16, hub=hub)
            h2 = _gate_mm_res(f"l{l}_conv_out", bcu, _pad8(conv_w[l]), _rows_merged(W["w_a_out", l]), h, seq, hub=hub)
            saved.append((h, bcu, h2))
        else:
            i = l - N_A_LAYERS
            if l == N_A_LAYERS:
                h_kv = h
                kv = _norm_mm("kv", h, norm_kv.reshape(1, D), W["w_kv", None], hub=hub)
            q = _norm_mm(f"l{l}_q", h, row(norm_mix, l), W["w_q", i])
            per_group = [_attn_fwd(f"l{l}_att{g}", q, kv, g, bl) for g in range(N_GROUPS)]
            o, lse = _combine(f"l{l}_combine", [p[0] for p in per_group], [p[1] for p in per_group])
            h2 = _mm_res(f"l{l}_att_out", o, W["w_o", i], h)
            saved.append((h, q, o, lse, h2))
        a = _norm_mm(f"l{l}_up", h2, row(norm_mlp, l), W["w_up", l], out_dtype=BF16, hub=hub)
        h = _relu2_mm_res(f"l{l}_down", a, _rows_merged(W["w_down", l]), h2, hub=hub)
        saved[-1] = saved[-1] + (a,)

    dh, sq_err, d_norm_final = _final_loss("loss", h, tgt, norm_final.reshape(1, D))

    d_norm_mix = [None] * DEPTH
    d_norm_mlp = [None] * DEPTH
    d_conv = [None] * N_A_LAYERS
    d_norm_kv = None
    dkv_acc = [None] * N_GROUPS
    G = hub.grads
    as_slots = lambda g: g.reshape(N_DEV, g.shape[0] // N_DEV, g.shape[1])
    tt = 512
    for l in reversed(range(DEPTH)):
        a, h2 = saved[l][-1], saved[l][-2]
        h_in = saved[l][0]
        g_mlp = row(norm_mlp, l)
        g_mix = row(norm_mix, l)
        w_up_l = W["w_up", l]
        FF = N_DEV * w_up_l.shape[2]
        da = _nt_relu2_bwd(f"l{l}_down_bwd", dh, _rows_merged(W["w_down", l]), a)
        G["w_down", l] = as_slots(_tn(f"l{l}_dw_down", [_rows2(a, tt, FF // 2, lambda s: s)], _relu2,
                                      [_rows2(dh, tt)], _val, FF, D, T, tt, split=("k", 2)))
        G["w_up", l] = _tn(f"l{l}_dw_up", [_rows2(h2, tt), _full2(g_mlp)], _normed,
                           [_rows2(da, tt, FF // 2, lambda s: s)], _val, D, FF, T, tt, split=("n", 2),
                           out_cols=w_up_l.shape[2], hub=hub)
        dh2, d_norm_mlp[l] = _nt_norm_bwd(f"l{l}_up_bwd", [da], w_up_l, h2, g_mlp, dh, hub=hub)
        if l >= N_A_LAYERS:
            i = l - N_A_LAYERS
            _, q, o, lse, _, _ = saved[l]
            w_o_i, w_q_i = W["w_o", i], W["w_q", i]
            do, delta = _att_out_bwd(f"l{l}_att_out_bwd", dh2, w_o_i, o)
            G["w_o", i] = _tn(f"l{l}_dw_o", [_rows2(o, tt)], _val, [_rows2(dh2, tt)], _val, QW, D, T, tt,
                              out_cols=w_o_i.shape[2])
            dqs = []
            for g in range(N_GROUPS):
                dqs.append(_attn_bwd_dq(f"l{l}_att{g}_dq", q, kv, do, delta, lse, g, bl, hub=hub))
                dkv_acc[g] = _attn_bwd_dkv(f"l{l}_att{g}_dkv", q, kv, do, delta, lse, g, bl, prev=dkv_acc[g])
            G["w_q", i] = _tn(f"l{l}_dw_q", [_rows2(h_in, tt), _full2(g_mix)], _normed,
                              [_rows2(t, tt) for t in dqs], _concat_f32, D, N_GROUPS * QW, T, tt, out_cols=w_q_i.shape[2])
            dh, d_norm_mix[l] = _nt_norm_bwd(f"l{l}_q_bwd", dqs, w_q_i, h_in, g_mix, dh2, hub=hub)
            if l == N_A_LAYERS:
                dkvs = [t for pair in dkv_acc for t in pair]
                g_kv = norm_kv.reshape(1, D)
                w_kv = W["w_kv", None]
                G["w_kv", None] = _tn("dw_kv", [_rows2(h_kv, 256), _full2(g_kv)], _normed,
                                      [_rows2(t, 256) for t in dkvs], _concat_f32, D, 2 * N_GROUPS * QW, T, 256,
                                      out_cols=w_kv.shape[2])
                dh, d_norm_kv = _nt_norm_bwd("kv_bwd", dkvs, w_kv, h_kv, g_kv, dh, hub=hub)
        else:
            _, bcu, _, _ = saved[l]
            cw = _pad8(conv_w[l])
            w_in_l = W["w_a_in", l]
            dgated = _nt_plain(f"l{l}_conv_out_bwd", dh2, _rows_merged(W["w_a_out", l]))

            def gated_tile(b_ref, c_ref, u_ref, ch_ref, uh_ref, cw_ref):
                first = (pl.program_id(1) * tt) % seq == 0
                return _gate(b_ref, c_ref, u_ref, ch_ref, uh_ref, cw_ref, first)[0]

            G["w_a_out", l] = as_slots(_tn(
                f"l{l}_dw_conv_out",
                [_rows2(bcu, tt, D, lambda s: 0), _rows2(bcu, tt, D, lambda s: 1), _rows2(bcu, tt, D, lambda s: 2),
                 _prev8_2(bcu, tt, D, 1), _prev8_2(bcu, tt, D, 2), _full2(cw)], gated_tile,
                [_rows2(dh2, tt)], _val, D, D, T, tt))
            dbcu, d_conv[l] = _conv_bwd(f"l{l}_conv_bwd", bcu, dgated, cw, seq, hub=hub)
            G["w_a_in", l] = _tn(f"l{l}_dw_in", [_rows2(h_in, tt), _full2(g_mix)], _normed, [_rows2(dbcu, tt)], _val,
                                 D, 3 * D, T, tt, out_cols=w_in_l.shape[2])
            dh, d_norm_mix[l] = _nt_norm_bwd(f"l{l}_in_bwd", [dbcu], w_in_l, h_in, g_mix, dh2, hub=hub)

    small = jnp.concatenate(d_norm_mix + d_norm_mlp + [d_norm_kv, d_norm_final] + d_conv, axis=0)
    return sq_err, dh.reshape(bl, seq, D), small


def kernel(x, norm_mix, norm_mlp, w_a_in, conv_w, w_a_out, norm_kv, w_kv, w_q, w_o, w_up, w_down, norm_final, loss_target, m_norm_mix, m_norm_mlp, m_w_a_in, m_conv_w, m_w_a_out, m_norm_kv, m_w_kv, m_w_q, m_w_o, m_w_up, m_w_down, m_norm_final, v_norm_mix, v_norm_mlp, v_w_a_in, v_conv_w, v_w_a_out, v_norm_kv, v_w_kv, v_w_q, v_w_o, v_w_up, v_w_down, v_norm_final):
    D = x.shape[-1]
    xi, yi, ci = _mesh_pos()
    me_idx = 4 * xi + 2 * yi + ci
    w_big = dict(w_a_in=w_a_in, w_a_out=w_a_out, w_kv=w_kv, w_q=w_q, w_o=w_o, w_up=w_up, w_down=w_down)
    m_big = dict(w_a_in=m_w_a_in, w_a_out=m_w_a_out, w_kv=m_w_kv, w_q=m_w_q, w_o=m_w_o, w_up=m_w_up, w_down=m_w_down)
    v_big = dict(w_a_in=v_w_a_in, w_a_out=v_w_a_out, w_kv=v_w_kv, w_q=v_w_q, w_o=v_w_o, w_up=v_w_up, w_down=v_w_down)
    names = list(w_big)

    shards = {n: w.astype(BF16) for n, w in w_big.items()}
    landing = {n: lax.empty((N_DEV,) + w.shape, BF16) for n, w in w_big.items()}
    hub = _Hub(FETCH_DURING, PUSH_DURING, shards, landing)
    dc = conv_w.shape[-1]
    taps = conv_w.shape[0] * conv_w.shape[1]
    now = ["w_a_in", "w_a_out", "w_up", "w_down"]
    got = _all_gather("gather_layer0", [(shards[n], 0) for n in now] + [(_pad8(conv_w.reshape(taps, dc)), None)])
    for n, w in zip(now, got):
        hub.weights[n, 0] = w
    conv_full = jnp.moveaxis(got[-1][:, :taps], 0, 1).reshape(conv_w.shape[0], conv_w.shape[1], N_DEV * dc)

    sq_err, grad_x, small = _local_grads(x, loss_target, norm_mix, norm_mlp, norm_kv, norm_final, conv_full, hub)
    loss = lax.psum(sq_err * (0.5 / D), ("x", "y", "c"))

    grads, deltas, new_m, new_v = {}, {}, {}, {}
    for n in names:
        shape = w_big[n].shape
        cols = shape[-1]
        flat = lambda t: t.reshape(-1, cols)
        parts = hub.landing[n].reshape(N_DEV, -1, cols)
        outs = _sum8_adamw(f"adamw_{n}", parts, flat(w_big[n]), flat(m_big[n]), flat(v_big[n]), tr=min(256, parts.shape[1]))
        grads[n], deltas[n], new_m[n], new_v[n] = (t.reshape(shape) for t in outs)

    n_gain = 2 * DEPTH + 2
    rows_small = small.shape[0]
    small_all = _all_gather("gather_small_grads", [(small, None)])[0]

    def small_pack(nm, nl, nk, nf, cw):
        gains = jnp.concatenate([nm, nl, nk.reshape(1, D), nf.reshape(1, D)], axis=0)
        taps_full = lax.dynamic_update_slice(jnp.zeros((taps, D), F32), cw.reshape(taps, dc), (0, me_idx * dc))
        return jnp.concatenate([gains, taps_full], axis=0)

    sp = [small_pack(*t) for t in ((norm_mix, norm_mlp, norm_kv, norm_final, conv_w),
                                   (m_norm_mix, m_norm_mlp, m_norm_kv, m_norm_final, m_conv_w),
                                   (v_norm_mix, v_norm_mlp, v_norm_kv, v_norm_final, v_conv_w))]
    small_out = _sum8_adamw("adamw_small", small_all, *sp, tr=rows_small)

    def small_unpack(t):
        res = dict(norm_mix=t[0:DEPTH], norm_mlp=t[DEPTH:2 * DEPTH], norm_kv=t[2 * DEPTH], norm_final=t[2 * DEPTH + 1])
        res["conv_w"] = lax.dynamic_slice(t[n_gain:], (0, me_idx * dc), (taps, dc)).reshape(conv_w.shape)
        return res

    for dst, t in zip((grads, deltas, new_m, new_v), small_out):
        dst.update(small_unpack(t))

    order = ["norm_mix", "norm_mlp", "w_a_in", "conv_w", "w_a_out", "norm_kv", "w_kv", "w_q", "w_o", "w_up", "w_down",
             "norm_final"]
    return (loss, grad_x, *[grads[n] for n in order], *[deltas[n] for n in order], *[new_m[n] for n in order],
            *[new_v[n] for n in order])
```

```python
import functools

import jax
import jax.numpy as jnp
from jax import lax
from jax.experimental import pallas as pl
from jax.experimental.pallas import tpu as pltpu

F32 = jnp.float32
BF16 = jnp.bfloat16
SDS = jax.ShapeDtypeStruct

EPS = 1e-5
N_A_LAYERS = 2
DEPTH = 4
PATTERNS = ((128, 1), (512, 4), (2048, 16))
N_GROUPS = 3
H_G = 8
HEAD_DIM = 64
QW = H_G * HEAD_DIM
ATT_BLK = 128
ALIBI_MAX_BIAS = 8.0
NEG_INF = -1e30

ADAM_LR = 0.001
ADAM_B1 = 0.9
ADAM_B2 = 0.999
ADAM_EPS = 1e-08
ADAM_WD = 0.01
ADAM_STEP = 10

N_DEV = 8
SUBLANES = 8
HALO = 16
V7X_VMEM_LIMIT = 48 * 1024 * 1024
MM_CHUNK = 512
MM_ROWS = 512
DW_TOKENS = 1024

FETCH_DURING = {
    "l0_in": [("w_a_in", 1)], "l0_conv_out": [("w_a_out", 1)], "l0_up": [("w_up", 1)], "l0_down": [("w_down", 1)],
    "l1_in": [("w_kv", None)], "l1_conv_out": [("w_q", 0), ("w_o", 0)], "l1_up": [("w_up", 2)], "l1_down": [("w_down", 2)],
    "kv": [("w_q", 1), ("w_o", 1)], "l2_up": [("w_up", 3)], "l2_down": [("w_down", 3)],
}
PUSH_DURING = {
    "l3_dw_up": [("w_down", 3)], "l3_up_bwd": [("w_up", 3)], "l3_att0_dq": [("w_o", 1)], "l3_q_bwd": [("w_q", 1)],
    "l2_dw_up": [("w_down", 2)], "l2_up_bwd": [("w_up", 2)], "l2_att0_dq": [("w_o", 0)], "l2_q_bwd": [("w_q", 0)],
    "kv_bwd": [("w_kv", None)],
    "l1_dw_up": [("w_down", 1)], "l1_up_bwd": [("w_up", 1)], "l1_conv_bwd": [("w_a_out", 1)], "l1_in_bwd": [("w_a_in", 1)],
    "l0_dw_up": [("w_down", 0)], "l0_up_bwd": [("w_up", 0)], "l0_conv_bwd": [("w_a_out", 0)], "l0_in_bwd": [("w_a_in", 0)],
}


def _mesh_pos():
    return lax.axis_index("x"), lax.axis_index("y"), lax.axis_index("c")


def _flip(v, bit):
    return 1 - v if bit else v


class _Transfer:
    def __init__(self, kind, key, src, src_idx=None, dst=None, dst_idx=None, dst_shape=None):
        self.kind, self.key, self.src, self.src_idx = kind, key, src, src_idx
        self.dst, self.dst_idx, self.dst_shape = dst, dst_idx, dst_shape

    def copies(self, src_ref, dst_ref, send_sems, recv_sems, local_sem):
        x, y, c = _mesh_pos()
        me = 4 * x + 2 * y + c

        def src_slot(j):
            if self.kind == "exchange":
                return src_ref.at[j]
            return src_ref if self.src_idx is None else src_ref.at[self.src_idx]

        def dst_slot(j):
            r = dst_ref.at[j]
            return r if self.dst_idx is None else r.at[self.dst_idx]

        local = pltpu.make_async_copy(src_slot(me), dst_slot(me), local_sem)
        sends, recvs = [], []
        for k in range(1, N_DEV):
            peer = (_flip(x, k & 4), _flip(y, k & 2), _flip(c, k & 1))
            peer_idx = 4 * peer[0] + 2 * peer[1] + peer[2]
            for dst_j, out in ((me, sends), (peer_idx, recvs)):
                out.append(pltpu.make_async_remote_copy(
                    src_ref=src_slot(peer_idx), dst_ref=dst_slot(dst_j), send_sem=send_sems.at[k - 1],
                    recv_sem=recv_sems.at[k - 1], device_id=peer, device_id_type=pl.DeviceIdType.MESH))
        return local, sends, recvs


class _Hub:
    def __init__(self, fetch, push, shards, landing):
        self.fetch, self.push, self.shards, self.landing = fetch, push, shards, landing
        self.weights = {}
        self.grads = {}

    def transfers(self, host):
        out = []
        for name, l in self.fetch.get(host, ()):
            src = self.shards[name]
            shard = src.shape if l is None else src.shape[1:]
            out.append(_Transfer("gather", (name, l), src, src_idx=l, dst_shape=(N_DEV,) + tuple(shard)))
        for name, l in self.push.get(host, ()):
            out.append(_Transfer("exchange", (name, l), self.grads.pop((name, l)), dst=self.landing[name], dst_idx=l))
        return out

    def accept(self, transfers, results):
        for t, r in zip(transfers, results):
            if t.kind == "gather":
                self.weights[t.key] = r
            else:
                self.landing[t.key[0]] = r


def _call(name, body, grid, ins, outs, scratch=(), hub=None):
    transfers = hub.transfers(name) if hub is not None else []
    n_in, n_out, n_scr, n_tr = len(ins), len(outs), len(scratch), len(transfers)
    c_in, c_out, aliases, places = [], [], {}, []
    for t in transfers:
        c_in.append(t.src)
        src_pos = len(c_in) - 1
        if t.dst is not None:
            c_in.append(t.dst)
            aliases[n_in + len(c_in) - 1] = n_out + len(c_out)
            c_out.append(SDS(t.dst.shape, t.dst.dtype))
        else:
            c_out.append(SDS(t.dst_shape, t.src.dtype))
        places.append((src_pos, len(c_out) - 1))
    sems = [pltpu.SemaphoreType.DMA((n_tr, N_DEV - 1)), pltpu.SemaphoreType.DMA((n_tr, N_DEV - 1)),
            pltpu.SemaphoreType.DMA((n_tr,))] if n_tr else []

    def wrapped(*refs):
        in_refs = refs[:n_in]
        cin_refs = refs[n_in:n_in + len(c_in)]
        o0 = n_in + len(c_in)
        out_refs = refs[o0:o0 + n_out]
        cout_refs = refs[o0 + n_out:o0 + n_out + len(c_out)]
        s0 = o0 + n_out + len(c_out)
        scr_refs = refs[s0:s0 + n_scr]
        if n_tr:
            send_sems, recv_sems, local_sems = refs[s0 + n_scr:]
            first = last = None
            for ax, n in enumerate(grid):
                i = pl.program_id(ax)
                first = (i == 0) if first is None else first & (i == 0)
                last = (i == n - 1) if last is None else last & (i == n - 1)

            def all_copies():
                return [t.copies(cin_refs[sp], cout_refs[dp], send_sems.at[n], recv_sems.at[n], local_sems.at[n])
                        for n, (t, (sp, dp)) in enumerate(zip(transfers, places))]

            @pl.when(first)
            def _():
                for local, sends, _ in all_copies():
                    local.start()
                    for cp in sends:
                        cp.start()

        body(*in_refs, *out_refs, *scr_refs)

        if n_tr:
            @pl.when(last)
            def _():
                for local, sends, recvs in all_copies():
                    for cp in recvs:
                        cp.wait_recv()
                    for cp in sends:
                        cp.wait_send()
                    local.wait()

    any_spec = pl.BlockSpec(memory_space=pl.ANY)
    res = pl.pallas_call(
        wrapped,
        name=name,
        grid=grid,
        in_specs=[s for _, s in ins] + [any_spec] * len(c_in),
        out_specs=[s for _, s in outs] + [any_spec] * len(c_out),
        out_shape=[o for o, _ in outs] + c_out,
        scratch_shapes=list(scratch) + sems,
        input_output_aliases=aliases,
        compiler_params=pltpu.CompilerParams(
            dimension_semantics=("arbitrary",) * len(grid), vmem_limit_bytes=V7X_VMEM_LIMIT),
    )(*[a for a, _ in ins], *c_in)
    if n_tr:
        hub.accept(transfers, res[n_out:])
    return res[:n_out]


def _rows(a, tm, cb=None, col=0):
    cb = cb or a.shape[1]
    return (a, pl.BlockSpec((tm, cb), lambda i: (i, col)))


def _full(a):
    nd = a.ndim
    return (a, pl.BlockSpec(a.shape, lambda i: (0,) * nd))


def _prev8(a, tm, cb, col):
    return (a, pl.BlockSpec((HALO, cb), lambda i: (jnp.maximum(i * (tm // HALO) - 1, 0), col)))


def _next8(a, tm, cb, col):
    last = a.shape[0] // HALO - 1
    return (a, pl.BlockSpec((HALO, cb), lambda i: (jnp.minimum((i + 1) * (tm // HALO), last), col)))


def _rows2(a, tt, cb=None, colfn=None):
    cb = cb or a.shape[1]
    colfn = colfn or (lambda s: 0)
    return (a, pl.BlockSpec((tt, cb), lambda s, t: (t, colfn(s))))


def _full2(a):
    nd = a.ndim
    return (a, pl.BlockSpec(a.shape, lambda s, t: (0,) * nd))


def _prev8_2(a, tt, cb, col):
    return (a, pl.BlockSpec((HALO, cb), lambda s, t: (jnp.maximum(t * (tt // HALO) - 1, 0), col)))


def _out_rows(T, n, dtype, tm):
    return (SDS((T, n), dtype), pl.BlockSpec((tm, n), lambda i: (i, 0)))


def _out_acc8(d):
    return (SDS((SUBLANES, d), F32), pl.BlockSpec((SUBLANES, d), lambda i: (0, 0)))


def _rstd(x):
    return lax.rsqrt(jnp.mean(x * x, axis=-1, keepdims=True) + EPS)


def _normed(h_ref, g_ref):
    x = h_ref[...]
    return x * _rstd(x) * g_ref[...]


def _acc8(ref, val, i, n):
    part = val.reshape(-1, SUBLANES, val.shape[-1]).sum(axis=0)

    @pl.when(i == 0)
    def _():
        ref[...] = part

    @pl.when(i > 0)
    def _():
        ref[...] += part

    @pl.when(i == n - 1)
    def _():
        ref[...] = jnp.broadcast_to(jnp.sum(ref[...], axis=0, keepdims=True), ref.shape)


def _gate(b_ref, c_ref, u_ref, ch_ref, uh_ref, cw_ref, first):
    b, c, u = (r[...].astype(F32) for r in (b_ref, c_ref, u_ref))
    cu = c * u
    halo = jnp.where(first, 0.0, ch_ref[...].astype(F32) * uh_ref[...].astype(F32))
    rows = lax.broadcasted_iota(jnp.int32, cu.shape, 0)
    h1 = halo[HALO - 1:HALO, :]
    h2 = halo[HALO - 2:HALO - 1, :]
    cu1 = jnp.where(rows == 0, h1, pltpu.roll(cu, 1, 0))
    cu2 = jnp.where(rows == 0, h2, jnp.where(rows == 1, h1, pltpu.roll(cu, 2, 0)))
    conv = cw_ref[0:1, :] * cu + cw_ref[1:2, :] * cu1 + cw_ref[2:3, :] * cu2
    return b * conv, (b, c, u), conv, (cu, cu1, cu2)


def _relu2(a_ref):
    r = jnp.maximum(a_ref[...].astype(F32), 0.0)
    return r * r


def _dot(a, b):
    return jnp.dot(a, b, preferred_element_type=F32)


def _dot_nt(a, b):
    return lax.dot_general(a, b, (((1,), (1,)), ((), ())), preferred_element_type=F32)


def _dot_tn(a, b):
    return lax.dot_general(a, b, (((0,), (0,)), ((), ())), preferred_element_type=F32)


def _chunks(n):
    c = min(MM_CHUNK, n)
    assert n % c == 0, n
    return [(k * c, (k + 1) * c) for k in range(n // c)]


def _col_weight(w):
    _, K, ns = w.shape
    N = N_DEV * ns
    direct = ns % 128 == 0 and ns >= 256
    scratch = [] if direct else [pltpu.VMEM((K, N), BF16)]

    def prepare(w_ref, s_ref, step):
        if direct:
            return

        @pl.when(step == 0)
        def _():
            for j in range(N_DEV):
                s_ref[:, j * ns:(j + 1) * ns] = w_ref[j]

    def chunks(w_ref, s_ref):
        if direct:
            return [(j * ns, (j + 1) * ns, (lambda j=j: w_ref[j])) for j in range(N_DEV)]
        return [(lo, hi, (lambda lo=lo, hi=hi: s_ref[:, lo:hi])) for lo, hi in _chunks(N)]

    return N, scratch, prepare, chunks


def _norm_mm(name, h, g, w, tm=MM_ROWS, out_dtype=F32, hub=None):
    T, _ = h.shape
    N, w_scratch, prepare, chunks = _col_weight(w)

    def body(h_ref, g_ref, w_ref, o_ref, *s):
        s_ref = s[0] if s else None
        prepare(w_ref, s_ref, pl.program_id(0))
        a = _normed(h_ref, g_ref).astype(BF16)
        for lo, hi, load in chunks(w_ref, s_ref):
            o_ref[:, lo:hi] = _dot(a, load()).astype(out_dtype)

    return _call(name, body, (T // tm,), [_rows(h, tm), _full(g), _full(w)], [_out_rows(T, N, out_dtype, tm)],
                 scratch=w_scratch, hub=hub)[0]


def _gate_mm_res(name, bcu, cw, w, h, seq, tm=MM_ROWS, hub=None):
    T, D = h.shape

    def body(b_ref, c_ref, u_ref, ch_ref, uh_ref, cw_ref, w_ref, h_ref, o_ref):
        first = (pl.program_id(0) * tm) % seq == 0
        gated = _gate(b_ref, c_ref, u_ref, ch_ref, uh_ref, cw_ref, first)[0].astype(BF16)
        for lo, hi in _chunks(D):
            o_ref[:, lo:hi] = h_ref[:, lo:hi] + _dot(gated, w_ref[:, lo:hi])

    ins = [_rows(bcu, tm, D, 0), _rows(bcu, tm, D, 1), _rows(bcu, tm, D, 2), _prev8(bcu, tm, D, 1),
           _prev8(bcu, tm, D, 2), _full(cw), _full(w), _rows(h, tm)]
    return _call(name, body, (T // tm,), ins, [_out_rows(T, D, F32, tm)], hub=hub)[0]


def _relu2_mm_res(name, a, w, h, tm=MM_ROWS, hub=None):
    T, D = h.shape
    K = a.shape[1]

    def body(a_ref, w_ref, h_ref, o_ref, acc_ref):
        for n, (lo, hi) in enumerate(_chunks(K)):
            d = _dot(_relu2(a_ref.at[:, lo:hi]).astype(BF16), w_ref[lo:hi, :])
            if n == 0:
                acc_ref[...] = d
            else:
                acc_ref[...] += d
        o_ref[...] = h_ref[...] + acc_ref[...]

    return _call(name, body, (T // tm,), [_rows(a, tm), _full(w), _rows(h, tm)], [_out_rows(T, D, F32, tm)],
                 scratch=[pltpu.VMEM((tm, D), F32)], hub=hub)[0]


def _mm_res(name, a, w, h, tm=MM_ROWS):
    T, D = h.shape
    _, w_scratch, prepare, chunks = _col_weight(w)

    def body(a_ref, w_ref, h_ref, o_ref, *s):
        s_ref = s[0] if s else None
        prepare(w_ref, s_ref, pl.program_id(0))
        av = a_ref[...].astype(BF16)
        for lo, hi, load in chunks(w_ref, s_ref):
            o_ref[:, lo:hi] = h_ref[:, lo:hi] + _dot(av, load())

    return _call(name, body, (T // tm,), [_rows(a, tm), _full(w), _rows(h, tm)], [_out_rows(T, D, F32, tm)],
                 scratch=w_scratch)[0]


def _nt_relu2_bwd(name, dh, w, a, tm=MM_ROWS):
    T, _ = dh.shape
    K = w.shape[0]

    def body(dh_ref, w_ref, a_ref, o_ref):
        d = dh_ref[...].astype(BF16)
        for lo, hi in _chunks(K):
            dr = _dot_nt(d, w_ref[lo:hi, :])
            o_ref[:, lo:hi] = (dr * (2.0 * jnp.maximum(a_ref[:, lo:hi].astype(F32), 0.0))).astype(BF16)

    return _call(name, body, (T // tm,), [_rows(dh, tm), _full(w), _rows(a, tm)], [_out_rows(T, K, BF16, tm)])[0]


def _concat_bf16(*refs):
    vals = [r[...].astype(BF16) for r in refs]
    return vals[0] if len(vals) == 1 else jnp.concatenate(vals, axis=1)


def _nt_plain(name, dy, w, tm=MM_ROWS):
    T, N = dy.shape
    if w.ndim == 3:
        K = w.shape[1]
        _, w_scratch, prepare, chunks = _col_weight(w)
    else:
        K = w.shape[0]
        w_scratch, prepare = [], (lambda w_ref, s_ref, step: None)
        chunks = lambda w_ref, s_ref: [(lo, hi, (lambda lo=lo, hi=hi: w_ref[:, lo:hi])) for lo, hi in _chunks(N)]

    def body(dy_ref, w_ref, o_ref, acc_ref, *s):
        s_ref = s[0] if s else None
        prepare(w_ref, s_ref, pl.program_id(0))
        for n, (lo, hi, load) in enumerate(chunks(w_ref, s_ref)):
            d = _dot_nt(dy_ref[:, lo:hi].astype(BF16), load())
            if n == 0:
                acc_ref[...] = d
            else:
                acc_ref[...] += d
        o_ref[...] = acc_ref[...]

    return _call(name, body, (T // tm,), [_rows(dy, tm), _full(w)], [_out_rows(T, K, F32, tm)],
                 scratch=[pltpu.VMEM((tm, K), F32)] + w_scratch)[0]


def _att_out_bwd(name, dy, w, o, tm=MM_ROWS):
    T, _ = dy.shape
    K = w.shape[1]
    _, w_scratch, prepare, chunks = _col_weight(w)

    def body(dy_ref, w_ref, o_ref, do_ref, dl_ref, acc_ref, *s):
        s_ref = s[0] if s else None
        prepare(w_ref, s_ref, pl.program_id(0))
        for n, (lo, hi, load) in enumerate(chunks(w_ref, s_ref)):
            d = _dot_nt(dy_ref[:, lo:hi].astype(BF16), load())
            if n == 0:
                acc_ref[...] = d
            else:
                acc_ref[...] += d
        do = acc_ref[...]
        do_ref[...] = do
        prod = do * o_ref[...]
        high = prod.astype(BF16)
        low = (prod - high.astype(F32)).astype(BF16)
        head_of = lambda axis: jnp.right_shift(lax.broadcasted_iota(jnp.int32, (K, K), axis), HEAD_DIM.bit_length() - 1)
        same_head = jnp.where(head_of(0) == head_of(1), 1.0, 0.0).astype(BF16)
        dl_ref[...] = _dot(high, same_head) + _dot(low, same_head)

    outs = [_out_rows(T, K, F32, tm), _out_rows(T, K, F32, tm)]
    return _call(name, body, (T // tm,), [_rows(dy, tm), _full(w), _rows(o, tm)], outs,
                 scratch=[pltpu.VMEM((tm, K), F32)] + w_scratch)


def _nt_norm_bwd(name, dys, w, h, g, dh_in, tm=MM_ROWS, hub=None):
    T, D = h.shape
    _, w_scratch, prepare, chunks = _col_weight(w)
    n_steps = T // tm
    n_dy = len(dys)

    def body(*refs):
        dy_refs = refs[:n_dy]
        w_ref, h_ref, g_ref, dhin_ref, o_ref, dg_ref, acc_ref = refs[n_dy:n_dy + 7]
        s_ref = refs[n_dy + 7] if len(refs) > n_dy + 7 else None
        i = pl.program_id(0)
        prepare(w_ref, s_ref, i)
        dy = _concat_bf16(*dy_refs)
        for n, (lo, hi, load) in enumerate(chunks(w_ref, s_ref)):
            d = _dot_nt(dy[:, lo:hi], load())
            if n == 0:
                acc_ref[...] = d
            else:
                acc_ref[...] += d
        dn = acc_ref[...]
        x = h_ref[...]
        rstd = _rstd(x)
        xhat = x * rstd
        dxhat = dn * g_ref[...]
        dx = rstd * (dxhat - xhat * jnp.mean(dxhat * xhat, axis=-1, keepdims=True))
        o_ref[...] = dhin_ref[...] + dx
        _acc8(dg_ref, dn * xhat, i, n_steps)

    ins = [_rows(d, tm) for d in dys] + [_full(w), _rows(h, tm), _full(g), _rows(dh_in, tm)]
    outs = [_out_rows(T, D, F32, tm), _out_acc8(D)]
    dh, dg = _call(name, body, (n_steps,), ins, outs, scratch=[pltpu.VMEM((tm, D), F32)] + w_scratch, hub=hub)
    return dh, dg[0:1]


def _tn(name, a_ins, a_fn, y_ins, y_fn, K, N, T, tt, split=None, out_cols=None, hub=None):
    kind, parts = split or ("n", 1)
    kb, nb = (K // parts, N) if kind == "k" else (K, N // parts)
    n_steps = T // tt
    n_a = len(a_ins)
    n_y = len(y_ins)
    assert out_cols is None or (kind == "n" and nb % out_cols == 0)

    def body(*refs):
        a_refs = refs[:n_a]
        y_refs = refs[n_a:n_a + n_y]
        o_ref, acc_ref = refs[n_a + n_y:]
        t = pl.program_id(1)
        a_t = a_fn(*a_refs).T.astype(BF16)
        y = y_fn(*y_refs).astype(BF16)
        for lo, hi in _chunks(nb):
            d = _dot(a_t, y[:, lo:hi])

            @pl.when(t == 0)
            def _():
                acc_ref[:, lo:hi] = d

            @pl.when(t > 0)
            def _():
                acc_ref[:, lo:hi] += d

        @pl.when(t == n_steps - 1)
        def _():
            if out_cols is None:
                o_ref[...] = acc_ref[...].astype(BF16)
            else:
                for j in range(nb // out_cols):
                    o_ref[j] = acc_ref[:, j * out_cols:(j + 1) * out_cols].astype(BF16)

    if out_cols is None:
        out = (SDS((K, N), BF16), pl.BlockSpec((kb, nb), (lambda s, t: (s, 0)) if kind == "k" else (lambda s, t: (0, s))))
    else:
        out = (SDS((N // out_cols, K, out_cols), BF16), pl.BlockSpec((nb // out_cols, K, out_cols), lambda s, t: (s, 0, 0)))
    return _call(name, body, (parts, n_steps), list(a_ins) + list(y_ins), [out],
                 scratch=[pltpu.VMEM((kb, nb), F32)], hub=hub)[0]


def _val(ref):
    return ref[...]


def _concat_f32(*refs):
    vals = [r[...] for r in refs]
    return vals[0] if len(vals) == 1 else jnp.concatenate(vals, axis=1)


ATT_TILE_ROWS = 512
HEAD_PAIRS = H_G // 2
ATT_SCALE = HEAD_DIM ** -0.5
ATT_UNITS_TOGETHER = 4


def _slope(h):
    return 2.0 ** (-ALIBI_MAX_BIAS * (h + 1) / H_G)


def _att_geom(T, bl, g):
    dil = PATTERNS[g][1]
    sub = ATT_BLK * dil
    nsub = max(1, ATT_TILE_ROWS // sub)
    rows = sub * nsub
    return dil, sub, nsub, rows, T // bl // rows


def _att_specs(T, bl, g):
    _, sub, nsub, rows, nt = _att_geom(T, bl, g)
    last_sub = T // sub - 1
    tile = lambda col: pl.BlockSpec((rows, 128), lambda b, i, hp: (b * nt + i, col(hp)))
    prev = lambda col: pl.BlockSpec((sub, 128), lambda b, i, hp: (jnp.maximum((b * nt + i) * nsub - 1, 0), col(hp)))
    nxt = lambda col: pl.BlockSpec((sub, 128), lambda b, i, hp: (jnp.minimum((b * nt + i + 1) * nsub, last_sub), col(hp)))
    return tile, prev, nxt


def _sub_rows(j, r, dil):
    start = j * ATT_BLK * dil + r
    return pl.ds(start, ATT_BLK, stride=dil) if dil > 1 else pl.ds(start, ATT_BLK)


def _att_consts(hp, dil):
    h0 = lax.broadcasted_iota(jnp.int32, (ATT_BLK, 128), 1) < HEAD_DIM
    a = lax.broadcasted_iota(jnp.int32, (ATT_BLK, ATT_BLK), 0)
    c = lax.broadcasted_iota(jnp.int32, (ATT_BLK, ATT_BLK), 1)
    dist_p = ((ATT_BLK + a - c) * dil).astype(F32)
    dist_c = ((a - c) * dil).astype(F32)
    bias_p, bias_c = [], []
    for h in range(2):
        slope = jnp.float32(_slope(2 * (HEAD_PAIRS - 1) + h))
        for p in range(HEAD_PAIRS - 2, -1, -1):
            slope = jnp.where(hp == p, jnp.float32(_slope(2 * p + h)), slope)
        bias_p.append(jnp.where(c >= a, -slope * dist_p, NEG_INF))
        bias_c.append(jnp.where(c <= a, -slope * dist_c, NEG_INF))
    return h0, bias_p, bias_c


def _split_heads(x, h0):
    return [jnp.where(h0, x, 0.0).astype(BF16), jnp.where(h0, 0.0, x).astype(BF16)]


def _head_cols(x):
    return [x[:, 0:1], x[:, HEAD_DIM:HEAD_DIM + 1]]


def _in_groups(units, first_stage, *later_stages):
    for u0 in range(0, len(units), ATT_UNITS_TOGETHER):
        staged = [first_stage(*u) for u in units[u0:u0 + ATT_UNITS_TOGETHER]]
        for stage in later_stages:
            staged = [stage(*s) for s in staged]


def _attn_fwd(name, q, kv, g, bl):
    T = q.shape[0]
    dil, _, nsub, _, _ = _att_geom(T, bl, g)
    tile, prev, _ = _att_specs(T, bl, g)

    def body(q_ref, kp_ref, kc_ref, vp_ref, vc_ref, o_ref, lse_ref):
        first = pl.program_id(1) == 0
        h0, bias_p, bias_c = _att_consts(pl.program_id(2), dil)
        bias_first = [jnp.where(first, NEG_INF, b) for b in bias_p]
        ones = jnp.ones((ATT_BLK, 128), BF16)

        def scores(j, r):
            cur = _sub_rows(j, r, dil)
            if j == 0:
                before = _sub_rows(0, r, dil)
                kp, vp, bp = kp_ref[before, :], vp_ref[before, :], bias_first
            else:
                before = _sub_rows(j - 1, r, dil)
                kp, vp, bp = kc_ref[before, :], vc_ref[before, :], bias_p
            kp, kc = kp.astype(BF16), kc_ref[cur, :].astype(BF16)
            qh = _split_heads(q_ref[cur, :] * ATT_SCALE, h0)
            sp = [_dot_nt(qh[h], kp) + bp[h] for h in range(2)]
            sc = [_dot_nt(qh[h], kc) + bias_c[h] for h in range(2)]
            return cur, sp, sc, vp.astype(BF16), vc_ref[cur, :].astype(BF16)

        def weights(cur, sp, sc, vp, vc):
            mx = [jnp.max(jnp.maximum(sp[h], sc[h]), axis=-1, keepdims=True) for h in range(2)]
            ep = [jnp.exp(sp[h] - mx[h]).astype(BF16) for h in range(2)]
            ec = [jnp.exp(sc[h] - mx[h]).astype(BF16) for h in range(2)]
            return cur, mx, ep, ec, vp, vc

        def outputs(cur, mx, ep, ec, vp, vc):
            den = [_dot(ep[h], ones) + _dot(ec[h], ones) for h in range(2)]
            acc = [_dot(ep[h], vp) + _dot(ec[h], vc) for h in range(2)]
            o_ref[cur, :] = jnp.where(h0, acc[0] / den[0], acc[1] / den[1])
            lse_ref[cur, :] = jnp.where(h0, mx[0] + jnp.log(den[0]), mx[1] + jnp.log(den[1]))
            return ()

        _in_groups([(j, r) for j in range(nsub) for r in range(dil)], scores, weights, outputs)

    ins = [(q, tile(lambda hp: 4 * g + hp)), (kv, prev(lambda hp: 8 * g + hp)), (kv, tile(lambda hp: 8 * g + hp)),
           (kv, prev(lambda hp: 8 * g + 4 + hp)), (kv, tile(lambda hp: 8 * g + 4 + hp))]
    out = (SDS((T, QW), F32), tile(lambda hp: hp))
    _, _, _, _, nt = _att_geom(T, bl, g)
    return _call(name, body, (bl, nt, HEAD_PAIRS), ins, [out, out])


def _combine(name, os_, lses, tm=512):
    T = os_[0].shape[0]

    def body(o0, o1, o2, l0, l1, l2, o_ref, lse_ref):
        ls = [l0[...], l1[...], l2[...]]
        mx = jnp.maximum(jnp.maximum(ls[0], ls[1]), ls[2])
        es = [jnp.exp(l - mx) for l in ls]
        den = es[0] + es[1] + es[2]
        o_ref[...] = (es[0] * o0[...] + es[1] * o1[...] + es[2] * o2[...]) / den
        lse_ref[...] = mx + jnp.log(den)

    ins = [_rows(t, tm) for t in list(os_) + list(lses)]
    return _call(name, body, (T // tm,), ins, [_out_rows(T, QW, F32, tm), _out_rows(T, QW, F32, tm)])


def _attn_bwd_dq(name, q, kv, do, delta, lse, g, bl, hub=None):
    T = q.shape[0]
    dil, _, nsub, _, nt = _att_geom(T, bl, g)
    tile, prev, _ = _att_specs(T, bl, g)

    def body(q_ref, kp_ref, kc_ref, vp_ref, vc_ref, do_ref, dl_ref, lse_ref, dq_ref):
        first = pl.program_id(1) == 0
        h0, bias_p, bias_c = _att_consts(pl.program_id(2), dil)
        bias_first = [jnp.where(first, NEG_INF, b) for b in bias_p]

        def probs(j, r):
            cur = _sub_rows(j, r, dil)
            if j == 0:
                before = _sub_rows(0, r, dil)
                kp, vp, bp = kp_ref[before, :], vp_ref[before, :], bias_first
            else:
                before = _sub_rows(j - 1, r, dil)
                kp, vp, bp = kc_ref[before, :], vc_ref[before, :], bias_p
            kp, vp = kp.astype(BF16), vp.astype(BF16)
            kc, vc = kc_ref[cur, :].astype(BF16), vc_ref[cur, :].astype(BF16)
            qh = _split_heads(q_ref[cur, :] * ATT_SCALE, h0)
            dob = _split_heads(do_ref[cur, :], h0)
            lse_h = _head_cols(lse_ref[cur, :])
            pp = [jnp.exp(_dot_nt(qh[h], kp) + bp[h] - lse_h[h]) for h in range(2)]
            pc = [jnp.exp(_dot_nt(qh[h], kc) + bias_c[h] - lse_h[h]) for h in range(2)]
            dpp = [_dot_nt(dob[h], vp) for h in range(2)]
            dpc = [_dot_nt(dob[h], vc) for h in range(2)]
            return cur, pp, pc, dpp, dpc, kp, kc

        def dscores(cur, pp, pc, dpp, dpc, kp, kc):
            dl = _head_cols(dl_ref[cur, :])
            dsp = [(pp[h] * (dpp[h] - dl[h])).astype(BF16) for h in range(2)]
            dsc = [(pc[h] * (dpc[h] - dl[h])).astype(BF16) for h in range(2)]
            return cur, dsp, dsc, kp, kc

        def outputs(cur, dsp, dsc, kp, kc):
            dqh = [_dot(dsp[h], kp) + _dot(dsc[h], kc) for h in range(2)]
            dq_ref[cur, :] = jnp.where(h0, dqh[0], dqh[1]) * ATT_SCALE
            return ()

        _in_groups([(j, r) for j in range(nsub) for r in range(dil)], probs, dscores, outputs)

    own = lambda hp: hp
    ins = [(q, tile(lambda hp: 4 * g + hp)), (kv, prev(lambda hp: 8 * g + hp)), (kv, tile(lambda hp: 8 * g + hp)),
           (kv, prev(lambda hp: 8 * g + 4 + hp)), (kv, tile(lambda hp: 8 * g + 4 + hp)),
           (do, tile(own)), (delta, tile(own)), (lse, tile(own))]
    return _call(name, body, (bl, nt, HEAD_PAIRS), ins, [(SDS((T, QW), F32), tile(own))], hub=hub)[0]


def _attn_bwd_dkv(name, q, kv, do, delta, lse, g, bl, prev=None):
    T = q.shape[0]
    dil, _, nsub, _, nt = _att_geom(T, bl, g)
    tile, _, nxt = _att_specs(T, bl, g)
    has_prev = prev is not None

    def body(*refs):
        k_ref, v_ref, q_ref, qn_ref, do_ref, don_ref, dl_ref, dln_ref, l_ref, ln_ref = refs[:10]
        rest = refs[10:]
        if has_prev:
            dkp_ref, dvp_ref, dk_ref, dv_ref = rest
        else:
            dk_ref, dv_ref = rest
        last = pl.program_id(1) == nt - 1
        h0, bias_p, bias_c = _att_consts(pl.program_id(2), dil)
        bias_last = [jnp.where(last, NEG_INF, b) for b in bias_p]

        def probs(j, r):
            cur = _sub_rows(j, r, dil)
            kb, vb = k_ref[cur, :].astype(BF16), v_ref[cur, :].astype(BF16)
            sets = [(q_ref, do_ref, dl_ref, l_ref, cur, bias_c)]
            if j < nsub - 1:
                sets.append((q_ref, do_ref, dl_ref, l_ref, _sub_rows(j + 1, r, dil), bias_p))
            else:
                sets.append((qn_ref, don_ref, dln_ref, ln_ref, _sub_rows(0, r, dil), bias_last))
            out = []
            for qr, dor, dlr, lr, rows, bias in sets:
                qs = qr[rows, :] * ATT_SCALE
                do2 = dor[rows, :]
                qh = _split_heads(qs, h0)
                dob = _split_heads(do2, h0)
                lse_h = _head_cols(lr[rows, :])
                p = [jnp.exp(_dot_nt(qh[h], kb) + bias[h] - lse_h[h]) for h in range(2)]
                dp = [_dot_nt(dob[h], vb) for h in range(2)]
                out.append((p, dp, dlr, rows, qs.astype(BF16), do2.astype(BF16)))
            return cur, out

        def dscores(cur, sets):
            out = []
            for p, dp, dlr, rows, qsb, do2b in sets:
                dl = _head_cols(dlr[rows, :])
                ds = [(p[h] * (dp[h] - dl[h])).astype(BF16) for h in range(2)]
                out.append(([p[h].astype(BF16) for h in range(2)], ds, qsb, do2b))
            return cur, out

        def outputs(cur, sets):
            dk = [None, None]
            dv = [None, None]
            for pb, ds, qsb, do2b in sets:
                for h in range(2):
                    dvh = _dot_tn(pb[h], do2b)
                    dkh = _dot_tn(ds[h], qsb)
                    dv[h] = dvh if dv[h] is None else dv[h] + dvh
                    dk[h] = dkh if dk[h] is None else dk[h] + dkh
            dk2 = jnp.where(h0, dk[0], dk[1])
            dv2 = jnp.where(h0, dv[0], dv[1])
            if has_prev:
                dk2 = dk2 + dkp_ref[cur, :]
                dv2 = dv2 + dvp_ref[cur, :]
            dk_ref[cur, :] = dk2
            dv_ref[cur, :] = dv2
            return ()

        _in_groups([(j, r) for j in range(nsub) for r in range(dil)], probs, dscores, outputs)

    own = lambda hp: hp
    qcol = lambda hp: 4 * g + hp
    ins = [(kv, tile(lambda hp: 8 * g + hp)), (kv, tile(lambda hp: 8 * g + 4 + hp)), (q, tile(qcol)), (q, nxt(qcol)),
           (do, tile(own)), (do, nxt(own)), (delta, tile(own)), (delta, nxt(own)), (lse, tile(own)), (lse, nxt(own))]
    if has_prev:
        ins += [(prev[0], tile(own)), (prev[1], tile(own))]
    out = (SDS((T, QW), F32), tile(own))
    return _call(name, body, (bl, nt, HEAD_PAIRS), ins, [out, out])


def _final_loss(name, h, tgt, g, tm=256):
    T, D = h.shape
    n_steps = T // tm

    def body(h_ref, t_ref, g_ref, dh_ref, loss_ref, dg_ref, sq_ref):
        i = pl.program_id(0)
        x = h_ref[...]
        rstd = _rstd(x)
        xhat = x * rstd
        err = xhat * g_ref[...] - t_ref[...]
        _acc8(sq_ref, err * err, i, n_steps)
        dy = err * (1.0 / D)
        dxhat = dy * g_ref[...]
        dh_ref[...] = rstd * (dxhat - xhat * jnp.mean(dxhat * xhat, axis=-1, keepdims=True))
        _acc8(dg_ref, dy * xhat, i, n_steps)

        @pl.when(i == n_steps - 1)
        def _():
            loss_ref[...] = jnp.full(loss_ref.shape, jnp.sum(sq_ref[0:1, :]), F32)

    outs = [_out_rows(T, D, F32, tm), (SDS((SUBLANES, 128), F32), pl.BlockSpec((SUBLANES, 128), lambda i: (0, 0))),
            _out_acc8(D)]
    dh, loss, dg = _call(name, body, (n_steps,), [_rows(h, tm), _rows(tgt, tm), _full(g)], outs,
                         scratch=[pltpu.VMEM((SUBLANES, D), F32)])
    return dh, loss[0, 0], dg[0:1]


def _conv_bwd(name, bcu, dgated, cw, seq, tm=256, hub=None):
    T, D = dgated.shape
    n_steps = T // tm

    def body(b_ref, c_ref, u_ref, ch_ref, uh_ref, dg_ref, dgn_ref, bn_ref, cw_ref, o_ref, t0_ref, t1_ref, t2_ref):
        i = pl.program_id(0)
        first = (i * tm) % seq == 0
        last = ((i + 1) * tm) % seq == 0
        _, (b, c, u), conv, (cu, cu1, cu2) = _gate(b_ref, c_ref, u_ref, ch_ref, uh_ref, cw_ref, first)
        dgat = dg_ref[...]
        dconv = dgat * b
        nxt = jnp.where(last, 0.0, dgn_ref[...] * bn_ref[...].astype(F32))
        rows = lax.broadcasted_iota(jnp.int32, dconv.shape, 0)
        n1 = nxt[0:1, :]
        n2 = nxt[1:2, :]
        dc1 = jnp.where(rows == tm - 1, n1, pltpu.roll(dconv, tm - 1, 0))
        dc2 = jnp.where(rows == tm - 1, n2, jnp.where(rows == tm - 2, n1, pltpu.roll(dconv, tm - 2, 0)))
        dcu = cw_ref[0:1, :] * dconv + cw_ref[1:2, :] * dc1 + cw_ref[2:3, :] * dc2
        o_ref[:, 0:D] = (dgat * conv).astype(BF16)
        o_ref[:, D:2 * D] = (dcu * u).astype(BF16)
        o_ref[:, 2 * D:3 * D] = (dcu * c).astype(BF16)
        _acc8(t0_ref, dconv * cu, i, n_steps)
        _acc8(t1_ref, dconv * cu1, i, n_steps)
        _acc8(t2_ref, dconv * cu2, i, n_steps)

    ins = [_rows(bcu, tm, D, 0), _rows(bcu, tm, D, 1), _rows(bcu, tm, D, 2), _prev8(bcu, tm, D, 1), _prev8(bcu, tm, D, 2),
           _rows(dgated, tm), _next8(dgated, tm, D, 0), _next8(bcu, tm, D, 0), _full(cw)]
    outs = [_out_rows(T, 3 * D, BF16, tm), _out_acc8(D), _out_acc8(D), _out_acc8(D)]
    dbcu, t0, t1, t2 = _call(name, body, (n_steps,), ins, outs, hub=hub)
    return dbcu, jnp.concatenate([t0[0:1], t1[0:1], t2[0:1]], axis=0)


def _sum8_adamw(name, parts, w, m, v, tr):
    R, C = w.shape
    b1c = 1.0 - ADAM_B1 ** ADAM_STEP
    b2c = 1.0 - ADAM_B2 ** ADAM_STEP

    def body(p_ref, w_ref, m_ref, v_ref, g_ref, d_ref, nm_ref, nv_ref):
        g = p_ref[0].astype(F32)
        for j in range(1, N_DEV):
            g = g + p_ref[j].astype(F32)
        nm = ADAM_B1 * m_ref[...] + (1.0 - ADAM_B1) * g
        nv = ADAM_B2 * v_ref[...] + (1.0 - ADAM_B2) * (g * g)
        m_hat = nm / b1c
        v_hat = nv / b2c
        g_ref[...] = g
        d_ref[...] = -ADAM_LR * (m_hat / (jnp.sqrt(v_hat) + ADAM_EPS) + ADAM_WD * w_ref[...])
        nm_ref[...] = nm
        nv_ref[...] = nv

    ins = [(parts, pl.BlockSpec((N_DEV, tr, C), lambda i: (0, i, 0))), _rows(w, tr), _rows(m, tr), _rows(v, tr)]
    outs = [_out_rows(R, C, F32, tr)] * 4
    return _call(name, body, (R // tr,), ins, outs)


def _all_gather(name, items):
    n = len(items)
    shapes = [tuple(a.shape if idx is None else a.shape[1:]) for a, idx in items]

    def body(*refs):
        x_refs, out_refs = refs[:n], refs[n:2 * n]
        send_sems, recv_sems, local_sems = refs[2 * n:]
        x, y, c = _mesh_pos()
        me, sibling = (x, y, c), (x, y, 1 - c)
        chips = [(1 - x, y), (x, 1 - y), (1 - x, 1 - y)]

        def copy(t, k, block, to, own=False):
            dst = out_refs[t].at[4 * block[0] + 2 * block[1] + block[2]]
            src = dst
            if own:
                src = x_refs[t] if items[t][1] is None else x_refs[t].at[items[t][1]]
            return pltpu.make_async_remote_copy(
                src_ref=src, dst_ref=dst, send_sem=send_sems.at[t, k], recv_sem=recv_sems.at[t, k],
                device_id=to, device_id_type=pl.DeviceIdType.MESH)

        started = []
        for t in range(n):
            src = x_refs[t] if items[t][1] is None else x_refs[t].at[items[t][1]]
            mine = pltpu.make_async_copy(src, out_refs[t].at[4 * x + 2 * y + c], local_sems.at[t])
            mine.start()
            first = [copy(t, 0, me, sibling, own=True)]
            first += [copy(t, 1 + j, me, (*chip, c), own=True) for j, chip in enumerate(chips)]
            for cp in first:
                cp.start()
            started.append((mine, first))
        passed = []
        for t in range(n):
            for j, chip in enumerate(chips):
                copy(t, 1 + j, (*chip, c), me).wait_recv()
                fwd = copy(t, 4 + j, (*chip, c), sibling)
                fwd.start()
                passed.append(fwd)
        for t in range(n):
            copy(t, 0, sibling, me).wait_recv()
            for j, chip in enumerate(chips):
                copy(t, 4 + j, (*chip, 1 - c), me).wait_recv()
        for mine, first in started:
            for cp in first:
                cp.wait_send()
            mine.wait()
        for cp in passed:
            cp.wait_send()

    any_spec = pl.BlockSpec(memory_space=pl.ANY)
    return pl.pallas_call(
        body, name=name,
        out_shape=[SDS((N_DEV,) + s, a.dtype) for s, (a, _) in zip(shapes, items)],
        in_specs=[any_spec] * n,
        out_specs=[any_spec] * n,
        scratch_shapes=[pltpu.SemaphoreType.DMA((n, 7)), pltpu.SemaphoreType.DMA((n, 7)), pltpu.SemaphoreType.DMA((n,))],
    )(*[a for a, _ in items])


def _pad8(t):
    return jnp.pad(t, ((0, SUBLANES - t.shape[0]), (0, 0)))


def _rows_merged(w):
    return w.reshape(w.shape[0] * w.shape[1], w.shape[2])


def _local_grads(x, tgt, norm_mix, norm_mlp, norm_kv, norm_final, conv_w, hub):
    bl, seq, D = x.shape
    T = bl * seq
    h = x.reshape(T, D)
    tgt = tgt.reshape(T, D)
    row = lambda t, l: t[l:l + 1]
    W = hub.weights
    saved = []
    kv = h_kv = None
    for l in range(DEPTH):
        if l < N_A_LAYERS:
            bcu = _norm_mm(f"l{l}_in", h, row(norm_mix, l), W["w_a_in", l], out_dtype=BF16, hub=hub)
            h2 = _gate_mm_res(f"l{l}_conv_out", bcu, _pad8(conv_w[l]), _rows_merged(W["w_a_out", l]), h, seq, hub=hub)
            saved.append((h, bcu, h2))
        else:
            i = l - N_A_LAYERS
            if l == N_A_LAYERS:
                h_kv = h
                kv = _norm_mm("kv", h, norm_kv.reshape(1, D), W["w_kv", None], hub=hub)
            q = _norm_mm(f"l{l}_q", h, row(norm_mix, l), W["w_q", i])
            per_group = [_attn_fwd(f"l{l}_att{g}", q, kv, g, bl) for g in range(N_GROUPS)]
            o, lse = _combine(f"l{l}_combine", [p[0] for p in per_group], [p[1] for p in per_group])
            h2 = _mm_res(f"l{l}_att_out", o, W["w_o", i], h)
            saved.append((h, q, o, lse, h2))
        a = _norm_mm(f"l{l}_up", h2, row(norm_mlp, l), W["w_up", l], out_dtype=BF16, hub=hub)
        h = _relu2_mm_res(f"l{l}_down", a, _rows_merged(W["w_down", l]), h2, hub=hub)
        saved[-1] = saved[-1] + (a,)

    dh, sq_err, d_norm_final = _final_loss("loss", h, tgt, norm_final.reshape(1, D))

    d_norm_mix = [None] * DEPTH
    d_norm_mlp = [None] * DEPTH
    d_conv = [None] * N_A_LAYERS
    d_norm_kv = None
    dkv_acc = [None] * N_GROUPS
    G = hub.grads
    as_slots = lambda g: g.reshape(N_DEV, g.shape[0] // N_DEV, g.shape[1])
    tt = DW_TOKENS
    tc = DW_TOKENS // 2
    for l in reversed(range(DEPTH)):
        a, h2 = saved[l][-1], saved[l][-2]
        h_in = saved[l][0]
        g_mlp = row(norm_mlp, l)
        g_mix = row(norm_mix, l)
        w_up_l = W["w_up", l]
        FF = N_DEV * w_up_l.shape[2]
        da = _nt_relu2_bwd(f"l{l}_down_bwd", dh, _rows_merged(W["w_down", l]), a)
        G["w_down", l] = as_slots(_tn(f"l{l}_dw_down", [_rows2(a, tt, FF // 2, lambda s: s)], _relu2,
                                      [_rows2(dh, tt)], _val, FF, D, T, tt, split=("k", 2)))
        G["w_up", l] = _tn(f"l{l}_dw_up", [_rows2(h2, 2 * tt), _full2(g_mlp)], _normed,
                           [_rows2(da, 2 * tt, FF // 4, lambda s: s)], _val, D, FF, T, 2 * tt, split=("n", 4),
                           out_cols=w_up_l.shape[2], hub=hub)
        dh2, d_norm_mlp[l] = _nt_norm_bwd(f"l{l}_up_bwd", [da], w_up_l, h2, g_mlp, dh, hub=hub)
        if l >= N_A_LAYERS:
            i = l - N_A_LAYERS
            _, q, o, lse, _, _ = saved[l]
            w_o_i, w_q_i = W["w_o", i], W["w_q", i]
            do, delta = _att_out_bwd(f"l{l}_att_out_bwd", dh2, w_o_i, o)
            G["w_o", i] = _tn(f"l{l}_dw_o", [_rows2(o, tt)], _val, [_rows2(dh2, tt)], _val, QW, D, T, tt,
                              out_cols=w_o_i.shape[2])
            dqs = []
            for g in range(N_GROUPS):
                dqs.append(_attn_bwd_dq(f"l{l}_att{g}_dq", q, kv, do, delta, lse, g, bl, hub=hub))
                dkv_acc[g] = _attn_bwd_dkv(f"l{l}_att{g}_dkv", q, kv, do, delta, lse, g, bl, prev=dkv_acc[g])
            G["w_q", i] = _tn(f"l{l}_dw_q", [_rows2(h_in, tt), _full2(g_mix)], _normed,
                              [_rows2(t, tt) for t in dqs], _concat_f32, D, N_GROUPS * QW, T, tt, out_cols=w_q_i.shape[2])
            dh, d_norm_mix[l] = _nt_norm_bwd(f"l{l}_q_bwd", dqs, w_q_i, h_in, g_mix, dh2, hub=hub)
            if l == N_A_LAYERS:
                dkvs = [t for pair in dkv_acc for t in pair]
                g_kv = norm_kv.reshape(1, D)
                w_kv = W["w_kv", None]
                G["w_kv", None] = _tn("dw_kv", [_rows2(h_kv, 256), _full2(g_kv)], _normed,
                                      [_rows2(t, 256) for t in dkvs], _concat_f32, D, 2 * N_GROUPS * QW, T, 256,
                                      out_cols=w_kv.shape[2])
                dh, d_norm_kv = _nt_norm_bwd("kv_bwd", dkvs, w_kv, h_kv, g_kv, dh, hub=hub)
        else:
            _, bcu, _, _ = saved[l]
            cw = _pad8(conv_w[l])
            w_in_l = W["w_a_in", l]
            dgated = _nt_plain(f"l{l}_conv_out_bwd", dh2, _rows_merged(W["w_a_out", l]))

            def gated_tile(b_ref, c_ref, u_ref, ch_ref, uh_ref, cw_ref):
                first = (pl.program_id(1) * tc) % seq == 0
                return _gate(b_ref, c_ref, u_ref, ch_ref, uh_ref, cw_ref, first)[0]

            G["w_a_out", l] = as_slots(_tn(
                f"l{l}_dw_conv_out",
                [_rows2(bcu, tc, D, lambda s: 0), _rows2(bcu, tc, D, lambda s: 1), _rows2(bcu, tc, D, lambda s: 2),
                 _prev8_2(bcu, tc, D, 1), _prev8_2(bcu, tc, D, 2), _full2(cw)], gated_tile,
                [_rows2(dh2, tc)], _val, D, D, T, tc))
            dbcu, d_conv[l] = _conv_bwd(f"l{l}_conv_bwd", bcu, dgated, cw, seq, hub=hub)
            G["w_a_in", l] = _tn(f"l{l}_dw_in", [_rows2(h_in, tt), _full2(g_mix)], _normed,
                                 [_rows2(dbcu, tt, 3 * D // 2, lambda s: s)], _val, D, 3 * D, T, tt, split=("n", 2),
                                 out_cols=w_in_l.shape[2])
            dh, d_norm_mix[l] = _nt_norm_bwd(f"l{l}_in_bwd", [dbcu], w_in_l, h_in, g_mix, dh2, hub=hub)

    small = jnp.concatenate(d_norm_mix + d_norm_mlp + [d_norm_kv, d_norm_final] + d_conv, axis=0)
    return sq_err, dh.reshape(bl, seq, D), small


def kernel(x, norm_mix, norm_mlp, w_a_in, conv_w, w_a_out, norm_kv, w_kv, w_q, w_o, w_up, w_down, norm_final, loss_target, m_norm_mix, m_norm_mlp, m_w_a_in, m_conv_w, m_w_a_out, m_norm_kv, m_w_kv, m_w_q, m_w_o, m_w_up, m_w_down, m_norm_final, v_norm_mix, v_norm_mlp, v_w_a_in, v_conv_w, v_w_a_out, v_norm_kv, v_w_kv, v_w_q, v_w_o, v_w_up, v_w_down, v_norm_final):
    D = x.shape[-1]
    xi, yi, ci = _mesh_pos()
    me_idx = 4 * xi + 2 * yi + ci
    w_big = dict(w_a_in=w_a_in, w_a_out=w_a_out, w_kv=w_kv, w_q=w_q, w_o=w_o, w_up=w_up, w_down=w_down)
    m_big = dict(w_a_in=m_w_a_in, w_a_out=m_w_a_out, w_kv=m_w_kv, w_q=m_w_q, w_o=m_w_o, w_up=m_w_up, w_down=m_w_down)
    v_big = dict(w_a_in=v_w_a_in, w_a_out=v_w_a_out, w_kv=v_w_kv, w_q=v_w_q, w_o=v_w_o, w_up=v_w_up, w_down=v_w_down)
    names = list(w_big)

    shards = {n: w.astype(BF16) for n, w in w_big.items()}
    landing = {n: lax.empty((N_DEV,) + w.shape, BF16) for n, w in w_big.items()}
    hub = _Hub(FETCH_DURING, PUSH_DURING, shards, landing)
    dc = conv_w.shape[-1]
    taps = conv_w.shape[0] * conv_w.shape[1]
    now = ["w_a_in", "w_a_out", "w_up", "w_down"]
    got = _all_gather("gather_layer0", [(shards[n], 0) for n in now] + [(_pad8(conv_w.reshape(taps, dc)), None)])
    for n, w in zip(now, got):
        hub.weights[n, 0] = w
    conv_full = jnp.moveaxis(got[-1][:, :taps], 0, 1).reshape(conv_w.shape[0], conv_w.shape[1], N_DEV * dc)

    sq_err, grad_x, small = _local_grads(x, loss_target, norm_mix, norm_mlp, norm_kv, norm_final, conv_full, hub)
    loss = lax.psum(sq_err * (0.5 / D), ("x", "y", "c"))

    grads, deltas, new_m, new_v = {}, {}, {}, {}
    for n in names:
        shape = w_big[n].shape
        cols = shape[-1]
        flat = lambda t: t.reshape(-1, cols)
        parts = hub.landing[n].reshape(N_DEV, -1, cols)
        outs = _sum8_adamw(f"adamw_{n}", parts, flat(w_big[n]), flat(m_big[n]), flat(v_big[n]), tr=min(256, parts.shape[1]))
        grads[n], deltas[n], new_m[n], new_v[n] = (t.reshape(shape) for t in outs)

    n_gain = 2 * DEPTH + 2
    rows_small = small.shape[0]
    small_all = _all_gather("gather_small_grads", [(small, None)])[0]

    def small_pack(nm, nl, nk, nf, cw):
        gains = jnp.concatenate([nm, nl, nk.reshape(1, D), nf.reshape(1, D)], axis=0)
        taps_full = lax.dynamic_update_slice(jnp.zeros((taps, D), F32), cw.reshape(taps, dc), (0, me_idx * dc))
        return jnp.concatenate([gains, taps_full], axis=0)

    sp = [small_pack(*t) for t in ((norm_mix, norm_mlp, norm_kv, norm_final, conv_w),
                                   (m_norm_mix, m_norm_mlp, m_norm_kv, m_norm_final, m_conv_w),
                                   (v_norm_mix, v_norm_mlp, v_norm_kv, v_norm_final, v_conv_w))]
    small_out = _sum8_adamw("adamw_small", small_all, *sp, tr=rows_small)

    def small_unpack(t):
        res = dict(norm_mix=t[0:DEPTH], norm_mlp=t[DEPTH:2 * DEPTH], norm_kv=t[2 * DEPTH], norm_final=t[2 * DEPTH + 1])
        res["conv_w"] = lax.dynamic_slice(t[n_gain:], (0, me_idx * dc), (taps, dc)).reshape(conv_w.shape)
        return res

    for dst, t in zip((grads, deltas, new_m, new_v), small_out):
        dst.update(small_unpack(t))

    order = ["norm_mix", "norm_mlp", "w_a_in", "conv_w", "w_a_out", "norm_kv", "w_kv", "w_q", "w_o", "w_up", "w_down",
             "norm_final"]
    return (loss, grad_x, *[grads[n] for n in order], *[deltas[n] for n in order], *[new_m[n] for n in order],
            *[new_v[n] for n in order])
```

```python
import functools

import jax
import jax.numpy as jnp
from jax import lax
from jax.experimental import pallas as pl
from jax.experimental.pallas import tpu as pltpu

F32 = jnp.float32
BF16 = jnp.bfloat16
SDS = jax.ShapeDtypeStruct

EPS = 1e-5
N_A_LAYERS = 2
DEPTH = 4
PATTERNS = ((128, 1), (512, 4), (2048, 16))
N_GROUPS = 3
H_G = 8
HEAD_DIM = 64
QW = H_G * HEAD_DIM
ATT_BLK = 128
ALIBI_MAX_BIAS = 8.0
NEG_INF = -1e30

ADAM_LR = 0.001
ADAM_B1 = 0.9
ADAM_B2 = 0.999
ADAM_EPS = 1e-08
ADAM_WD = 0.01
ADAM_STEP = 10

N_DEV = 8
SUBLANES = 8
HALO = 16
V7X_VMEM_LIMIT = 48 * 1024 * 1024
MM_CHUNK = 512
MM_ROWS = 512
DW_TOKENS = 1024

FETCH_DURING = {
    "l0_in": [("w_a_in", 1, 0, 2)], "l0_conv_out": [("w_a_in", 1, 1, 2)],
    "l0_up": [("w_up", 1, 0, 2)], "l0_down": [("w_up", 1, 1, 2), ("w_a_out", 1)],
    "l1_in": [("w_down", 1, 0, 2)], "l1_conv_out": [("w_down", 1, 1, 2)],
    "l1_up": [("w_kv", None, 0, 2)], "l1_down": [("w_kv", None, 1, 2), ("w_q", 0)],
    "kv": [("w_o", 0), ("w_up", 2, 0, 2)], "l2_att0": [("w_up", 2, 1, 2)],
    "l2_att1": [("w_down", 2, 0, 2)], "l2_att2": [("w_down", 2, 1, 2)],
    "l2_up": [("w_q", 1), ("w_o", 1)], "l2_down": [("w_up", 3, 0, 2)], "l3_att0": [("w_up", 3, 1, 2)],
    "l3_att1": [("w_down", 3, 0, 2)], "l3_att2": [("w_down", 3, 1, 2)],
}
PUSH_DURING = {
    "l3_dw_up": [("w_down", 3, 0, 2)], "l3_up_bwd": [("w_down", 3, 1, 2)],
    "l3_att0_dq": [("w_up", 3, 0, 2)], "l3_att0_dkv": [("w_up", 3, 1, 2)], "l3_att1_dq": [("w_o", 1)], "l3_q_bwd": [("w_q", 1)],
    "l2_dw_up": [("w_down", 2, 0, 2)], "l2_up_bwd": [("w_down", 2, 1, 2)],
    "l2_att0_dq": [("w_up", 2, 0, 2)], "l2_att0_dkv": [("w_up", 2, 1, 2)], "l2_att1_dq": [("w_o", 0)], "l2_q_bwd": [("w_q", 0)],
    "kv_bwd": [("w_kv", None, 0, 2)], "l1_down_bwd": [("w_kv", None, 1, 2)],
    "l1_dw_up": [("w_down", 1, 0, 2)], "l1_up_bwd": [("w_down", 1, 1, 2)], "l1_conv_bwd": [("w_up", 1, 0, 2)],
    "l1_dw_in": [("w_up", 1, 1, 2), ("w_a_out", 1)], "l1_in_bwd": [("w_a_in", 1, 0, 2)], "l0_down_bwd": [("w_a_in", 1, 1, 2)],
    "l0_dw_up": [("w_down", 0, 0, 2)], "l0_up_bwd": [("w_down", 0, 1, 2)], "l0_conv_bwd": [("w_up", 0, 0, 2)],
    "l0_dw_in": [("w_up", 0, 1, 2), ("w_a_out", 0)], "l0_in_bwd": [("w_a_in", 0)],
}


def _mesh_pos():
    return lax.axis_index("x"), lax.axis_index("y"), lax.axis_index("c")


def _flip(v, bit):
    return 1 - v if bit else v


class _Transfer:
    def __init__(self, kind, key, src, src_idx=None, dst=None, dst_idx=None, dst_shape=None, rows=None):
        self.kind, self.key, self.src, self.src_idx = kind, key, src, src_idx
        self.dst, self.dst_idx, self.dst_shape, self.rows = dst, dst_idx, dst_shape, rows

    def copies(self, src_ref, dst_ref, send_sems, recv_sems, local_sem):
        x, y, c = _mesh_pos()
        me = 4 * x + 2 * y + c
        part = (lambda r: r) if self.rows is None else (lambda r: r.at[pl.ds(*self.rows)])

        def src_slot(j):
            if self.kind == "exchange":
                return part(src_ref.at[j])
            return part(src_ref if self.src_idx is None else src_ref.at[self.src_idx])

        def dst_slot(j):
            r = dst_ref.at[j]
            return part(r if self.dst_idx is None else r.at[self.dst_idx])

        local = pltpu.make_async_copy(src_slot(me), dst_slot(me), local_sem)
        sends, recvs = [], []
        for k in range(1, N_DEV):
            peer = (_flip(x, k & 4), _flip(y, k & 2), _flip(c, k & 1))
            peer_idx = 4 * peer[0] + 2 * peer[1] + peer[2]
            for dst_j, out in ((me, sends), (peer_idx, recvs)):
                out.append(pltpu.make_async_remote_copy(
                    src_ref=src_slot(peer_idx), dst_ref=dst_slot(dst_j), send_sem=send_sems.at[k - 1],
                    recv_sem=recv_sems.at[k - 1], device_id=peer, device_id_type=pl.DeviceIdType.MESH))
        return local, sends, recvs


class _Hub:
    def __init__(self, fetch, push, shards, landing):
        self.fetch, self.push, self.shards, self.landing = fetch, push, shards, landing
        self.weights = {}
        self.arriving = {}
        self.grads = {}

    def transfers(self, host):
        out = []
        for name, l, *part in self.fetch.get(host, ()):
            src = self.shards[name]
            shard = tuple(src.shape if l is None else src.shape[1:])
            p, n = part or (0, 1)
            rows = None if n == 1 else (p * (shard[0] // n), shard[0] // n)
            out.append(_Transfer("gather", (name, l, p == n - 1), src, src_idx=l, dst=self.arriving.get((name, l)),
                                 dst_shape=(N_DEV,) + shard, rows=rows))
        for name, l, *part in self.push.get(host, ()):
            src = self.grads[name, l]
            p, n = part or (0, 1)
            rows = None if n == 1 else (p * (src.shape[1] // n), src.shape[1] // n)
            out.append(_Transfer("exchange", (name, l, p == n - 1), src, dst=self.landing[name], dst_idx=l, rows=rows))
        return out

    def accept(self, transfers, results):
        for t, r in zip(transfers, results):
            name, l, complete = t.key
            if t.kind == "exchange":
                self.landing[name] = r
            elif complete:
                self.weights[name, l] = r
            else:
                self.arriving[name, l] = r


def _call(name, body, grid, ins, outs, scratch=(), hub=None):
    transfers = hub.transfers(name) if hub is not None else []
    n_in, n_out, n_scr, n_tr = len(ins), len(outs), len(scratch), len(transfers)
    c_in, c_out, aliases, places = [], [], {}, []
    for t in transfers:
        c_in.append(t.src)
        src_pos = len(c_in) - 1
        if t.dst is not None:
            c_in.append(t.dst)
            aliases[n_in + len(c_in) - 1] = n_out + len(c_out)
            c_out.append(SDS(t.dst.shape, t.dst.dtype))
        else:
            c_out.append(SDS(t.dst_shape, t.src.dtype))
        places.append((src_pos, len(c_out) - 1))
    sems = [pltpu.SemaphoreType.DMA((n_tr, N_DEV - 1)), pltpu.SemaphoreType.DMA((n_tr, N_DEV - 1)),
            pltpu.SemaphoreType.DMA((n_tr,))] if n_tr else []

    def wrapped(*refs):
        in_refs = refs[:n_in]
        cin_refs = refs[n_in:n_in + len(c_in)]
        o0 = n_in + len(c_in)
        out_refs = refs[o0:o0 + n_out]
        cout_refs = refs[o0 + n_out:o0 + n_out + len(c_out)]
        s0 = o0 + n_out + len(c_out)
        scr_refs = refs[s0:s0 + n_scr]
        if n_tr:
            send_sems, recv_sems, local_sems = refs[s0 + n_scr:]
            first = last = None
            for ax, n in enumerate(grid):
                i = pl.program_id(ax)
                first = (i == 0) if first is None else first & (i == 0)
                last = (i == n - 1) if last is None else last & (i == n - 1)

            def all_copies():
                return [t.copies(cin_refs[sp], cout_refs[dp], send_sems.at[n], recv_sems.at[n], local_sems.at[n])
                        for n, (t, (sp, dp)) in enumerate(zip(transfers, places))]

            @pl.when(first)
            def _():
                for local, sends, _ in all_copies():
                    local.start()
                    for cp in sends:
                        cp.start()

        body(*in_refs, *out_refs, *scr_refs)

        if n_tr:
            @pl.when(last)
            def _():
                for local, sends, recvs in all_copies():
                    for cp in recvs:
                        cp.wait_recv()
                    for cp in sends:
                        cp.wait_send()
                    local.wait()

    any_spec = pl.BlockSpec(memory_space=pl.ANY)
    res = pl.pallas_call(
        wrapped,
        name=name,
        grid=grid,
        in_specs=[s for _, s in ins] + [any_spec] * len(c_in),
        out_specs=[s for _, s in outs] + [any_spec] * len(c_out),
        out_shape=[o for o, _ in outs] + c_out,
        scratch_shapes=list(scratch) + sems,
        input_output_aliases=aliases,
        compiler_params=pltpu.CompilerParams(
            dimension_semantics=("arbitrary",) * len(grid), vmem_limit_bytes=V7X_VMEM_LIMIT),
    )(*[a for a, _ in ins], *c_in)
    if n_tr:
        hub.accept(transfers, res[n_out:])
    return res[:n_out]


def _rows(a, tm, cb=None, col=0):
    cb = cb or a.shape[1]
    return (a, pl.BlockSpec((tm, cb), lambda i: (i, col)))


def _full(a):
    nd = a.ndim
    return (a, pl.BlockSpec(a.shape, lambda i: (0,) * nd))


def _prev8(a, tm, cb, col):
    return (a, pl.BlockSpec((HALO, cb), lambda i: (jnp.maximum(i * (tm // HALO) - 1, 0), col)))


def _next8(a, tm, cb, col):
    last = a.shape[0] // HALO - 1
    return (a, pl.BlockSpec((HALO, cb), lambda i: (jnp.minimum((i + 1) * (tm // HALO), last), col)))


def _rows2(a, tt, cb=None, colfn=None):
    cb = cb or a.shape[1]
    colfn = colfn or (lambda s: 0)
    return (a, pl.BlockSpec((tt, cb), lambda s, t: (t, colfn(s))))


def _full2(a):
    nd = a.ndim
    return (a, pl.BlockSpec(a.shape, lambda s, t: (0,) * nd))


def _prev8_2(a, tt, cb, col):
    return (a, pl.BlockSpec((HALO, cb), lambda s, t: (jnp.maximum(t * (tt // HALO) - 1, 0), col)))


def _out_rows(T, n, dtype, tm):
    return (SDS((T, n), dtype), pl.BlockSpec((tm, n), lambda i: (i, 0)))


def _out_acc8(d):
    return (SDS((SUBLANES, d), F32), pl.BlockSpec((SUBLANES, d), lambda i: (0, 0)))


def _rstd(x):
    return lax.rsqrt(jnp.mean(x * x, axis=-1, keepdims=True) + EPS)


def _normed(h_ref, g_ref):
    x = h_ref[...]
    return x * _rstd(x) * g_ref[...]


def _acc8(ref, val, i, n):
    part = val.reshape(-1, SUBLANES, val.shape[-1]).sum(axis=0)

    @pl.when(i == 0)
    def _():
        ref[...] = part

    @pl.when(i > 0)
    def _():
        ref[...] += part

    @pl.when(i == n - 1)
    def _():
        ref[...] = jnp.broadcast_to(jnp.sum(ref[...], axis=0, keepdims=True), ref.shape)


def _gate(b_ref, c_ref, u_ref, ch_ref, uh_ref, cw_ref, first):
    b, c, u = (r[...].astype(F32) for r in (b_ref, c_ref, u_ref))
    cu = c * u
    halo = jnp.where(first, 0.0, ch_ref[...].astype(F32) * uh_ref[...].astype(F32))
    rows = lax.broadcasted_iota(jnp.int32, cu.shape, 0)
    h1 = halo[HALO - 1:HALO, :]
    h2 = halo[HALO - 2:HALO - 1, :]
    cu1 = jnp.where(rows == 0, h1, pltpu.roll(cu, 1, 0))
    cu2 = jnp.where(rows == 0, h2, jnp.where(rows == 1, h1, pltpu.roll(cu, 2, 0)))
    conv = cw_ref[0:1, :] * cu + cw_ref[1:2, :] * cu1 + cw_ref[2:3, :] * cu2
    return b * conv, (b, c, u), conv, (cu, cu1, cu2)


def _relu2(a_ref):
    r = jnp.maximum(a_ref[...].astype(F32), 0.0)
    return r * r


def _dot(a, b):
    return jnp.dot(a, b, preferred_element_type=F32)


def _dot_nt(a, b):
    return lax.dot_general(a, b, (((1,), (1,)), ((), ())), preferred_element_type=F32)


def _dot_tn(a, b):
    return lax.dot_general(a, b, (((0,), (0,)), ((), ())), preferred_element_type=F32)


def _chunks(n):
    c = min(MM_CHUNK, n)
    while n % c:
        c -= 128
    assert c > 0, n
    return [(k * c, (k + 1) * c) for k in range(n // c)]


def _col_weight(w):
    _, K, ns = w.shape
    N = N_DEV * ns
    direct = ns % 128 == 0 and ns >= 256
    scratch = [] if direct else [pltpu.VMEM((K, N), BF16)]

    def prepare(w_ref, s_ref, step):
        if direct:
            return

        @pl.when(step == 0)
        def _():
            for j in range(N_DEV):
                s_ref[:, j * ns:(j + 1) * ns] = w_ref[j]

    def chunks(w_ref, s_ref):
        if direct:
            return [(j * ns, (j + 1) * ns, (lambda j=j: w_ref[j])) for j in range(N_DEV)]
        return [(lo, hi, (lambda lo=lo, hi=hi: s_ref[:, lo:hi])) for lo, hi in _chunks(N)]

    return N, scratch, prepare, chunks


def _norm_mm(name, h, g, w, tm=MM_ROWS, out_dtype=F32, hub=None):
    T, _ = h.shape
    N, w_scratch, prepare, chunks = _col_weight(w)

    def body(h_ref, g_ref, w_ref, o_ref, *s):
        s_ref = s[0] if s else None
        prepare(w_ref, s_ref, pl.program_id(0))
        a = _normed(h_ref, g_ref).astype(BF16)
        for lo, hi, load in chunks(w_ref, s_ref):
            o_ref[:, lo:hi] = _dot(a, load()).astype(out_dtype)

    return _call(name, body, (T // tm,), [_rows(h, tm), _full(g), _full(w)], [_out_rows(T, N, out_dtype, tm)],
                 scratch=w_scratch, hub=hub)[0]


def _gate_mm_res(name, bcu, cw, w, h, seq, tm=MM_ROWS, hub=None):
    T, D = h.shape

    def body(b_ref, c_ref, u_ref, ch_ref, uh_ref, cw_ref, w_ref, h_ref, o_ref):
        first = (pl.program_id(0) * tm) % seq == 0
        gated = _gate(b_ref, c_ref, u_ref, ch_ref, uh_ref, cw_ref, first)[0].astype(BF16)
        for lo, hi in _chunks(D):
            o_ref[:, lo:hi] = h_ref[:, lo:hi] + _dot(gated, w_ref[:, lo:hi])

    ins = [_rows(bcu, tm, D, 0), _rows(bcu, tm, D, 1), _rows(bcu, tm, D, 2), _prev8(bcu, tm, D, 1),
           _prev8(bcu, tm, D, 2), _full(cw), _full(w), _rows(h, tm)]
    return _call(name, body, (T // tm,), ins, [_out_rows(T, D, F32, tm)], hub=hub)[0]


def _relu2_mm_res(name, a, w, h, tm=MM_ROWS, hub=None):
    T, D = h.shape
    K = a.shape[1]

    def body(a_ref, w_ref, h_ref, o_ref, acc_ref):
        for n, (lo, hi) in enumerate(_chunks(K)):
            d = _dot(_relu2(a_ref.at[:, lo:hi]).astype(BF16), w_ref[lo:hi, :])
            if n == 0:
                acc_ref[...] = d
            else:
                acc_ref[...] += d
        o_ref[...] = h_ref[...] + acc_ref[...]

    return _call(name, body, (T // tm,), [_rows(a, tm), _full(w), _rows(h, tm)], [_out_rows(T, D, F32, tm)],
                 scratch=[pltpu.VMEM((tm, D), F32)], hub=hub)[0]


def _mm_res(name, a, w, h, tm=MM_ROWS):
    T, D = h.shape
    _, w_scratch, prepare, chunks = _col_weight(w)

    def body(a_ref, w_ref, h_ref, o_ref, *s):
        s_ref = s[0] if s else None
        prepare(w_ref, s_ref, pl.program_id(0))
        av = a_ref[...].astype(BF16)
        for lo, hi, load in chunks(w_ref, s_ref):
            o_ref[:, lo:hi] = h_ref[:, lo:hi] + _dot(av, load())

    return _call(name, body, (T // tm,), [_rows(a, tm), _full(w), _rows(h, tm)], [_out_rows(T, D, F32, tm)],
                 scratch=w_scratch)[0]


def _nt_relu2_bwd(name, dh, w, a, tm=MM_ROWS, hub=None):
    T, _ = dh.shape
    K = w.shape[0]

    def body(dh_ref, w_ref, a_ref, o_ref):
        d = dh_ref[...].astype(BF16)
        for lo, hi in _chunks(K):
            dr = _dot_nt(d, w_ref[lo:hi, :])
            o_ref[:, lo:hi] = (dr * (2.0 * jnp.maximum(a_ref[:, lo:hi].astype(F32), 0.0))).astype(BF16)

    return _call(name, body, (T // tm,), [_rows(dh, tm), _full(w), _rows(a, tm)], [_out_rows(T, K, BF16, tm)], hub=hub)[0]


def _concat_bf16(*refs):
    vals = [r[...].astype(BF16) for r in refs]
    return vals[0] if len(vals) == 1 else jnp.concatenate(vals, axis=1)


def _nt_plain(name, dy, w, tm=MM_ROWS):
    T, N = dy.shape
    if w.ndim == 3:
        K = w.shape[1]
        _, w_scratch, prepare, chunks = _col_weight(w)
    else:
        K = w.shape[0]
        w_scratch, prepare = [], (lambda w_ref, s_ref, step: None)
        chunks = lambda w_ref, s_ref: [(lo, hi, (lambda lo=lo, hi=hi: w_ref[:, lo:hi])) for lo, hi in _chunks(N)]

    def body(dy_ref, w_ref, o_ref, acc_ref, *s):
        s_ref = s[0] if s else None
        prepare(w_ref, s_ref, pl.program_id(0))
        for n, (lo, hi, load) in enumerate(chunks(w_ref, s_ref)):
            d = _dot_nt(dy_ref[:, lo:hi].astype(BF16), load())
            if n == 0:
                acc_ref[...] = d
            else:
                acc_ref[...] += d
        o_ref[...] = acc_ref[...]

    return _call(name, body, (T // tm,), [_rows(dy, tm), _full(w)], [_out_rows(T, K, F32, tm)],
                 scratch=[pltpu.VMEM((tm, K), F32)] + w_scratch)[0]


def _att_out_bwd(name, dy, w, o, tm=MM_ROWS):
    T, _ = dy.shape
    K = w.shape[1]
    _, w_scratch, prepare, chunks = _col_weight(w)

    def body(dy_ref, w_ref, o_ref, do_ref, dl_ref, acc_ref, *s):
        s_ref = s[0] if s else None
        prepare(w_ref, s_ref, pl.program_id(0))
        for n, (lo, hi, load) in enumerate(chunks(w_ref, s_ref)):
            d = _dot_nt(dy_ref[:, lo:hi].astype(BF16), load())
            if n == 0:
                acc_ref[...] = d
            else:
                acc_ref[...] += d
        do = acc_ref[...]
        do_ref[...] = do
        prod = do * o_ref[...]
        high = prod.astype(BF16)
        low = (prod - high.astype(F32)).astype(BF16)
        head_of = lambda axis: jnp.right_shift(lax.broadcasted_iota(jnp.int32, (K, K), axis), HEAD_DIM.bit_length() - 1)
        same_head = jnp.where(head_of(0) == head_of(1), 1.0, 0.0).astype(BF16)
        dl_ref[...] = _dot(high, same_head) + _dot(low, same_head)

    outs = [_out_rows(T, K, F32, tm), _out_rows(T, K, F32, tm)]
    return _call(name, body, (T // tm,), [_rows(dy, tm), _full(w), _rows(o, tm)], outs,
                 scratch=[pltpu.VMEM((tm, K), F32)] + w_scratch)


def _nt_norm_bwd(name, dys, w, h, g, dh_in, tm=MM_ROWS, hub=None):
    T, D = h.shape
    _, w_scratch, prepare, chunks = _col_weight(w)
    n_steps = T // tm
    n_dy = len(dys)

    def body(*refs):
        dy_refs = refs[:n_dy]
        w_ref, h_ref, g_ref, dhin_ref, o_ref, dg_ref, acc_ref = refs[n_dy:n_dy + 7]
        s_ref = refs[n_dy + 7] if len(refs) > n_dy + 7 else None
        i = pl.program_id(0)
        prepare(w_ref, s_ref, i)
        dy = _concat_bf16(*dy_refs)
        for n, (lo, hi, load) in enumerate(chunks(w_ref, s_ref)):
            d = _dot_nt(dy[:, lo:hi], load())
            if n == 0:
                acc_ref[...] = d
            else:
                acc_ref[...] += d
        dn = acc_ref[...]
        x = h_ref[...]
        rstd = _rstd(x)
        xhat = x * rstd
        dxhat = dn * g_ref[...]
        dx = rstd * (dxhat - xhat * jnp.mean(dxhat * xhat, axis=-1, keepdims=True))
        o_ref[...] = dhin_ref[...] + dx
        _acc8(dg_ref, dn * xhat, i, n_steps)

    ins = [_rows(d, tm) for d in dys] + [_full(w), _rows(h, tm), _full(g), _rows(dh_in, tm)]
    outs = [_out_rows(T, D, F32, tm), _out_acc8(D)]
    dh, dg = _call(name, body, (n_steps,), ins, outs, scratch=[pltpu.VMEM((tm, D), F32)] + w_scratch, hub=hub)
    return dh, dg[0:1]


def _tn(name, a_ins, a_fn, y_ins, y_fn, K, N, T, tt, split=None, out_cols=None, hub=None):
    kind, parts = split or ("n", 1)
    kb, nb = (K // parts, N) if kind == "k" else (K, N // parts)
    n_steps = T // tt
    n_a = len(a_ins)
    n_y = len(y_ins)
    assert out_cols is None or (kind == "n" and nb % out_cols == 0)

    def body(*refs):
        a_refs = refs[:n_a]
        y_refs = refs[n_a:n_a + n_y]
        o_ref, acc_ref = refs[n_a + n_y:]
        t = pl.program_id(1)
        a_t = a_fn(*a_refs).T.astype(BF16)
        y = y_fn(*y_refs).astype(BF16)
        for lo, hi in _chunks(nb):
            d = _dot(a_t, y[:, lo:hi])

            @pl.when(t == 0)
            def _():
                acc_ref[:, lo:hi] = d

            @pl.when(t > 0)
            def _():
                acc_ref[:, lo:hi] += d

        @pl.when(t == n_steps - 1)
        def _():
            if out_cols is None:
                o_ref[...] = acc_ref[...].astype(BF16)
            else:
                for j in range(nb // out_cols):
                    o_ref[j] = acc_ref[:, j * out_cols:(j + 1) * out_cols].astype(BF16)

    if out_cols is None:
        out = (SDS((K, N), BF16), pl.BlockSpec((kb, nb), (lambda s, t: (s, 0)) if kind == "k" else (lambda s, t: (0, s))))
    else:
        out = (SDS((N // out_cols, K, out_cols), BF16), pl.BlockSpec((nb // out_cols, K, out_cols), lambda s, t: (s, 0, 0)))
    return _call(name, body, (parts, n_steps), list(a_ins) + list(y_ins), [out],
                 scratch=[pltpu.VMEM((kb, nb), F32)], hub=hub)[0]


def _val(ref):
    return ref[...]


def _concat_f32(*refs):
    vals = [r[...] for r in refs]
    return vals[0] if len(vals) == 1 else jnp.concatenate(vals, axis=1)


ATT_TILE_ROWS = 512
HEAD_PAIRS = H_G // 2
ATT_SCALE = HEAD_DIM ** -0.5
ATT_UNITS_TOGETHER = 4


def _slope(h):
    return 2.0 ** (-ALIBI_MAX_BIAS * (h + 1) / H_G)


def _att_geom(T, bl, g):
    dil = PATTERNS[g][1]
    sub = ATT_BLK * dil
    nsub = max(1, ATT_TILE_ROWS // sub)
    rows = sub * nsub
    return dil, sub, nsub, rows, T // bl // rows


def _att_specs(T, bl, g):
    _, sub, nsub, rows, nt = _att_geom(T, bl, g)
    last_sub = T // sub - 1
    tile = lambda col: pl.BlockSpec((rows, 128), lambda b, i, hp: (b * nt + i, col(hp)))
    prev = lambda col: pl.BlockSpec((sub, 128), lambda b, i, hp: (jnp.maximum((b * nt + i) * nsub - 1, 0), col(hp)))
    nxt = lambda col: pl.BlockSpec((sub, 128), lambda b, i, hp: (jnp.minimum((b * nt + i + 1) * nsub, last_sub), col(hp)))
    return tile, prev, nxt


def _sub_rows(j, r, dil):
    start = j * ATT_BLK * dil + r
    return pl.ds(start, ATT_BLK, stride=dil) if dil > 1 else pl.ds(start, ATT_BLK)


def _att_consts(hp, dil):
    h0 = lax.broadcasted_iota(jnp.int32, (ATT_BLK, 128), 1) < HEAD_DIM
    a = lax.broadcasted_iota(jnp.int32, (ATT_BLK, ATT_BLK), 0)
    c = lax.broadcasted_iota(jnp.int32, (ATT_BLK, ATT_BLK), 1)
    dist_p = ((ATT_BLK + a - c) * dil).astype(F32)
    dist_c = ((a - c) * dil).astype(F32)
    bias_p, bias_c = [], []
    for h in range(2):
        slope = jnp.float32(_slope(2 * (HEAD_PAIRS - 1) + h))
        for p in range(HEAD_PAIRS - 2, -1, -1):
            slope = jnp.where(hp == p, jnp.float32(_slope(2 * p + h)), slope)
        bias_p.append(jnp.where(c >= a, -slope * dist_p, NEG_INF))
        bias_c.append(jnp.where(c <= a, -slope * dist_c, NEG_INF))
    return h0, bias_p, bias_c


def _split_heads(x, h0):
    return [jnp.where(h0, x, 0.0).astype(BF16), jnp.where(h0, 0.0, x).astype(BF16)]


def _head_cols(x):
    return [x[:, 0:1], x[:, HEAD_DIM:HEAD_DIM + 1]]


def _in_groups(units, first_stage, *later_stages):
    for u0 in range(0, len(units), ATT_UNITS_TOGETHER):
        staged = [first_stage(*u) for u in units[u0:u0 + ATT_UNITS_TOGETHER]]
        for stage in later_stages:
            staged = [stage(*s) for s in staged]


def _attn_fwd(name, q, kv, g, bl, hub=None):
    T = q.shape[0]
    dil, _, nsub, _, _ = _att_geom(T, bl, g)
    tile, prev, _ = _att_specs(T, bl, g)

    def body(q_ref, kp_ref, kc_ref, vp_ref, vc_ref, o_ref, lse_ref):
        first = pl.program_id(1) == 0
        h0, bias_p, bias_c = _att_consts(pl.program_id(2), dil)
        bias_first = [jnp.where(first, NEG_INF, b) for b in bias_p]
        ones = jnp.ones((ATT_BLK, 128), BF16)

        def scores(j, r):
            cur = _sub_rows(j, r, dil)
            if j == 0:
                before = _sub_rows(0, r, dil)
                kp, vp, bp = kp_ref[before, :], vp_ref[before, :], bias_first
            else:
                before = _sub_rows(j - 1, r, dil)
                kp, vp, bp = kc_ref[before, :], vc_ref[before, :], bias_p
            kp, kc = kp.astype(BF16), kc_ref[cur, :].astype(BF16)
            qh = _split_heads(q_ref[cur, :] * ATT_SCALE, h0)
            sp = [_dot_nt(qh[h], kp) + bp[h] for h in range(2)]
            sc = [_dot_nt(qh[h], kc) + bias_c[h] for h in range(2)]
            return cur, sp, sc, vp.astype(BF16), vc_ref[cur, :].astype(BF16)

        def weights(cur, sp, sc, vp, vc):
            mx = [jnp.max(jnp.maximum(sp[h], sc[h]), axis=-1, keepdims=True) for h in range(2)]
            ep = [jnp.exp(sp[h] - mx[h]).astype(BF16) for h in range(2)]
            ec = [jnp.exp(sc[h] - mx[h]).astype(BF16) for h in range(2)]
            return cur, mx, ep, ec, vp, vc

        def outputs(cur, mx, ep, ec, vp, vc):
            den = [_dot(ep[h], ones) + _dot(ec[h], ones) for h in range(2)]
            acc = [_dot(ep[h], vp) + _dot(ec[h], vc) for h in range(2)]
            o_ref[cur, :] = jnp.where(h0, acc[0] / den[0], acc[1] / den[1])
            lse_ref[cur, :] = jnp.where(h0, mx[0] + jnp.log(den[0]), mx[1] + jnp.log(den[1]))
            return ()

        _in_groups([(j, r) for j in range(nsub) for r in range(dil)], scores, weights, outputs)

    ins = [(q, tile(lambda hp: 4 * g + hp)), (kv, prev(lambda hp: 8 * g + hp)), (kv, tile(lambda hp: 8 * g + hp)),
           (kv, prev(lambda hp: 8 * g + 4 + hp)), (kv, tile(lambda hp: 8 * g + 4 + hp))]
    out = (SDS((T, QW), F32), tile(lambda hp: hp))
    _, _, _, _, nt = _att_geom(T, bl, g)
    return _call(name, body, (bl, nt, HEAD_PAIRS), ins, [out, out], hub=hub)


def _combine(name, os_, lses, tm=512):
    T = os_[0].shape[0]

    def body(o0, o1, o2, l0, l1, l2, o_ref, lse_ref):
        ls = [l0[...], l1[...], l2[...]]
        mx = jnp.maximum(jnp.maximum(ls[0], ls[1]), ls[2])
        es = [jnp.exp(l - mx) for l in ls]
        den = es[0] + es[1] + es[2]
        o_ref[...] = (es[0] * o0[...] + es[1] * o1[...] + es[2] * o2[...]) / den
        lse_ref[...] = mx + jnp.log(den)

    ins = [_rows(t, tm) for t in list(os_) + list(lses)]
    return _call(name, body, (T // tm,), ins, [_out_rows(T, QW, F32, tm), _out_rows(T, QW, F32, tm)])


def _attn_bwd_dq(name, q, kv, do, delta, lse, g, bl, hub=None):
    T = q.shape[0]
    dil, _, nsub, _, nt = _att_geom(T, bl, g)
    tile, prev, _ = _att_specs(T, bl, g)

    def body(q_ref, kp_ref, kc_ref, vp_ref, vc_ref, do_ref, dl_ref, lse_ref, dq_ref):
        first = pl.program_id(1) == 0
        h0, bias_p, bias_c = _att_consts(pl.program_id(2), dil)
        bias_first = [jnp.where(first, NEG_INF, b) for b in bias_p]

        def probs(j, r):
            cur = _sub_rows(j, r, dil)
            if j == 0:
                before = _sub_rows(0, r, dil)
                kp, vp, bp = kp_ref[before, :], vp_ref[before, :], bias_first
            else:
                before = _sub_rows(j - 1, r, dil)
                kp, vp, bp = kc_ref[before, :], vc_ref[before, :], bias_p
            kp, vp = kp.astype(BF16), vp.astype(BF16)
            kc, vc = kc_ref[cur, :].astype(BF16), vc_ref[cur, :].astype(BF16)
            qh = _split_heads(q_ref[cur, :] * ATT_SCALE, h0)
            dob = _split_heads(do_ref[cur, :], h0)
            lse_h = _head_cols(lse_ref[cur, :])
            pp = [jnp.exp(_dot_nt(qh[h], kp) + bp[h] - lse_h[h]) for h in range(2)]
            pc = [jnp.exp(_dot_nt(qh[h], kc) + bias_c[h] - lse_h[h]) for h in range(2)]
            dpp = [_dot_nt(dob[h], vp) for h in range(2)]
            dpc = [_dot_nt(dob[h], vc) for h in range(2)]
            return cur, pp, pc, dpp, dpc, kp, kc

        def dscores(cur, pp, pc, dpp, dpc, kp, kc):
            dl = _head_cols(dl_ref[cur, :])
            dsp = [(pp[h] * (dpp[h] - dl[h])).astype(BF16) for h in range(2)]
            dsc = [(pc[h] * (dpc[h] - dl[h])).astype(BF16) for h in range(2)]
            return cur, dsp, dsc, kp, kc

        def outputs(cur, dsp, dsc, kp, kc):
            dqh = [_dot(dsp[h], kp) + _dot(dsc[h], kc) for h in range(2)]
            dq_ref[cur, :] = jnp.where(h0, dqh[0], dqh[1]) * ATT_SCALE
            return ()

        _in_groups([(j, r) for j in range(nsub) for r in range(dil)], probs, dscores, outputs)

    own = lambda hp: hp
    ins = [(q, tile(lambda hp: 4 * g + hp)), (kv, prev(lambda hp: 8 * g + hp)), (kv, tile(lambda hp: 8 * g + hp)),
           (kv, prev(lambda hp: 8 * g + 4 + hp)), (kv, tile(lambda hp: 8 * g + 4 + hp)),
           (do, tile(own)), (delta, tile(own)), (lse, tile(own))]
    return _call(name, body, (bl, nt, HEAD_PAIRS), ins, [(SDS((T, QW), F32), tile(own))], hub=hub)[0]


def _attn_bwd_dkv(name, q, kv, do, delta, lse, g, bl, prev=None, hub=None):
    T = q.shape[0]
    dil, _, nsub, _, nt = _att_geom(T, bl, g)
    tile, _, nxt = _att_specs(T, bl, g)
    has_prev = prev is not None

    def body(*refs):
        k_ref, v_ref, q_ref, qn_ref, do_ref, don_ref, dl_ref, dln_ref, l_ref, ln_ref = refs[:10]
        rest = refs[10:]
        if has_prev:
            dkp_ref, dvp_ref, dk_ref, dv_ref = rest
        else:
            dk_ref, dv_ref = rest
        last = pl.program_id(1) == nt - 1
        h0, bias_p, bias_c = _att_consts(pl.program_id(2), dil)
        bias_last = [jnp.where(last, NEG_INF, b) for b in bias_p]

        def probs(j, r):
            cur = _sub_rows(j, r, dil)
            kb, vb = k_ref[cur, :].astype(BF16), v_ref[cur, :].astype(BF16)
            sets = [(q_ref, do_ref, dl_ref, l_ref, cur, bias_c)]
            if j < nsub - 1:
                sets.append((q_ref, do_ref, dl_ref, l_ref, _sub_rows(j + 1, r, dil), bias_p))
            else:
                sets.append((qn_ref, don_ref, dln_ref, ln_ref, _sub_rows(0, r, dil), bias_last))
            out = []
            for qr, dor, dlr, lr, rows, bias in sets:
                qs = qr[rows, :] * ATT_SCALE
                do2 = dor[rows, :]
                qh = _split_heads(qs, h0)
                dob = _split_heads(do2, h0)
                lse_h = _head_cols(lr[rows, :])
                p = [jnp.exp(_dot_nt(qh[h], kb) + bias[h] - lse_h[h]) for h in range(2)]
                dp = [_dot_nt(dob[h], vb) for h in range(2)]
                out.append((p, dp, dlr, rows, qs.astype(BF16), do2.astype(BF16)))
            return cur, out

        def dscores(cur, sets):
            out = []
            for p, dp, dlr, rows, qsb, do2b in sets:
                dl = _head_cols(dlr[rows, :])
                ds = [(p[h] * (dp[h] - dl[h])).astype(BF16) for h in range(2)]
                out.append(([p[h].astype(BF16) for h in range(2)], ds, qsb, do2b))
            return cur, out

        def outputs(cur, sets):
            dk = [None, None]
            dv = [None, None]
            for pb, ds, qsb, do2b in sets:
                for h in range(2):
                    dvh = _dot_tn(pb[h], do2b)
                    dkh = _dot_tn(ds[h], qsb)
                    dv[h] = dvh if dv[h] is None else dv[h] + dvh
                    dk[h] = dkh if dk[h] is None else dk[h] + dkh
            dk2 = jnp.where(h0, dk[0], dk[1])
            dv2 = jnp.where(h0, dv[0], dv[1])
            if has_prev:
                dk2 = dk2 + dkp_ref[cur, :]
                dv2 = dv2 + dvp_ref[cur, :]
            dk_ref[cur, :] = dk2
            dv_ref[cur, :] = dv2
            return ()

        _in_groups([(j, r) for j in range(nsub) for r in range(dil)], probs, dscores, outputs)

    own = lambda hp: hp
    qcol = lambda hp: 4 * g + hp
    ins = [(kv, tile(lambda hp: 8 * g + hp)), (kv, tile(lambda hp: 8 * g + 4 + hp)), (q, tile(qcol)), (q, nxt(qcol)),
           (do, tile(own)), (do, nxt(own)), (delta, tile(own)), (delta, nxt(own)), (lse, tile(own)), (lse, nxt(own))]
    if has_prev:
        ins += [(prev[0], tile(own)), (prev[1], tile(own))]
    out = (SDS((T, QW), F32), tile(own))
    return _call(name, body, (bl, nt, HEAD_PAIRS), ins, [out, out], hub=hub)


def _final_loss(name, h, tgt, g, tm=256):
    T, D = h.shape
    n_steps = T // tm

    def body(h_ref, t_ref, g_ref, dh_ref, loss_ref, dg_ref, sq_ref):
        i = pl.program_id(0)
        x = h_ref[...]
        rstd = _rstd(x)
        xhat = x * rstd
        err = xhat * g_ref[...] - t_ref[...]
        _acc8(sq_ref, err * err, i, n_steps)
        dy = err * (1.0 / D)
        dxhat = dy * g_ref[...]
        dh_ref[...] = rstd * (dxhat - xhat * jnp.mean(dxhat * xhat, axis=-1, keepdims=True))
        _acc8(dg_ref, dy * xhat, i, n_steps)

        @pl.when(i == n_steps - 1)
        def _():
            loss_ref[...] = jnp.full(loss_ref.shape, jnp.sum(sq_ref[0:1, :]), F32)

    outs = [_out_rows(T, D, F32, tm), (SDS((SUBLANES, 128), F32), pl.BlockSpec((SUBLANES, 128), lambda i: (0, 0))),
            _out_acc8(D)]
    dh, loss, dg = _call(name, body, (n_steps,), [_rows(h, tm), _rows(tgt, tm), _full(g)], outs,
                         scratch=[pltpu.VMEM((SUBLANES, D), F32)])
    return dh, loss[0, 0], dg[0:1]


def _conv_bwd(name, bcu, dgated, cw, seq, tm=256, hub=None):
    T, D = dgated.shape
    n_steps = T // tm

    def body(b_ref, c_ref, u_ref, ch_ref, uh_ref, dg_ref, dgn_ref, bn_ref, cw_ref, o_ref, t0_ref, t1_ref, t2_ref):
        i = pl.program_id(0)
        first = (i * tm) % seq == 0
        last = ((i + 1) * tm) % seq == 0
        _, (b, c, u), conv, (cu, cu1, cu2) = _gate(b_ref, c_ref, u_ref, ch_ref, uh_ref, cw_ref, first)
        dgat = dg_ref[...]
        dconv = dgat * b
        nxt = jnp.where(last, 0.0, dgn_ref[...] * bn_ref[...].astype(F32))
        rows = lax.broadcasted_iota(jnp.int32, dconv.shape, 0)
        n1 = nxt[0:1, :]
        n2 = nxt[1:2, :]
        dc1 = jnp.where(rows == tm - 1, n1, pltpu.roll(dconv, tm - 1, 0))
        dc2 = jnp.where(rows == tm - 1, n2, jnp.where(rows == tm - 2, n1, pltpu.roll(dconv, tm - 2, 0)))
        dcu = cw_ref[0:1, :] * dconv + cw_ref[1:2, :] * dc1 + cw_ref[2:3, :] * dc2
        o_ref[:, 0:D] = (dgat * conv).astype(BF16)
        o_ref[:, D:2 * D] = (dcu * u).astype(BF16)
        o_ref[:, 2 * D:3 * D] = (dcu * c).astype(BF16)
        _acc8(t0_ref, dconv * cu, i, n_steps)
        _acc8(t1_ref, dconv * cu1, i, n_steps)
        _acc8(t2_ref, dconv * cu2, i, n_steps)

    ins = [_rows(bcu, tm, D, 0), _rows(bcu, tm, D, 1), _rows(bcu, tm, D, 2), _prev8(bcu, tm, D, 1), _prev8(bcu, tm, D, 2),
           _rows(dgated, tm), _next8(dgated, tm, D, 0), _next8(bcu, tm, D, 0), _full(cw)]
    outs = [_out_rows(T, 3 * D, BF16, tm), _out_acc8(D), _out_acc8(D), _out_acc8(D)]
    dbcu, t0, t1, t2 = _call(name, body, (n_steps,), ins, outs, hub=hub)
    return dbcu, jnp.concatenate([t0[0:1], t1[0:1], t2[0:1]], axis=0)


def _sum8_adamw(name, parts, w, m, v, tr):
    R, C = w.shape
    b1c = 1.0 - ADAM_B1 ** ADAM_STEP
    b2c = 1.0 - ADAM_B2 ** ADAM_STEP

    def body(p_ref, w_ref, m_ref, v_ref, g_ref, d_ref, nm_ref, nv_ref):
        g = p_ref[0].astype(F32)
        for j in range(1, N_DEV):
            g = g + p_ref[j].astype(F32)
        nm = ADAM_B1 * m_ref[...] + (1.0 - ADAM_B1) * g
        nv = ADAM_B2 * v_ref[...] + (1.0 - ADAM_B2) * (g * g)
        m_hat = nm / b1c
        v_hat = nv / b2c
        g_ref[...] = g
        d_ref[...] = -ADAM_LR * (m_hat / (jnp.sqrt(v_hat) + ADAM_EPS) + ADAM_WD * w_ref[...])
        nm_ref[...] = nm
        nv_ref[...] = nv

    ins = [(parts, pl.BlockSpec((N_DEV, tr, C), lambda i: (0, i, 0))), _rows(w, tr), _rows(m, tr), _rows(v, tr)]
    outs = [_out_rows(R, C, F32, tr)] * 4
    return _call(name, body, (R // tr,), ins, outs)


def _all_gather(name, items):
    n = len(items)
    shapes = [tuple(a.shape if idx is None else a.shape[1:]) for a, idx in items]

    def body(*refs):
        x_refs, out_refs = refs[:n], refs[n:2 * n]
        send_sems, recv_sems, local_sems = refs[2 * n:]
        x, y, c = _mesh_pos()
        me, sibling = (x, y, c), (x, y, 1 - c)
        chips = [(1 - x, y), (x, 1 - y), (1 - x, 1 - y)]

        def copy(t, k, block, to, own=False):
            dst = out_refs[t].at[4 * block[0] + 2 * block[1] + block[2]]
            src = dst
            if own:
                src = x_refs[t] if items[t][1] is None else x_refs[t].at[items[t][1]]
            return pltpu.make_async_remote_copy(
                src_ref=src, dst_ref=dst, send_sem=send_sems.at[t, k], recv_sem=recv_sems.at[t, k],
                device_id=to, device_id_type=pl.DeviceIdType.MESH)

        started = []
        for t in range(n):
            src = x_refs[t] if items[t][1] is None else x_refs[t].at[items[t][1]]
            mine = pltpu.make_async_copy(src, out_refs[t].at[4 * x + 2 * y + c], local_sems.at[t])
            mine.start()
            first = [copy(t, 0, me, sibling, own=True)]
            first += [copy(t, 1 + j, me, (*chip, c), own=True) for j, chip in enumerate(chips)]
            for cp in first:
                cp.start()
            started.append((mine, first))
        passed = []
        for t in range(n):
            for j, chip in enumerate(chips):
                copy(t, 1 + j, (*chip, c), me).wait_recv()
                fwd = copy(t, 4 + j, (*chip, c), sibling)
                fwd.start()
                passed.append(fwd)
        for t in range(n):
            copy(t, 0, sibling, me).wait_recv()
            for j, chip in enumerate(chips):
                copy(t, 4 + j, (*chip, 1 - c), me).wait_recv()
        for mine, first in started:
            for cp in first:
                cp.wait_send()
            mine.wait()
        for cp in passed:
            cp.wait_send()

    any_spec = pl.BlockSpec(memory_space=pl.ANY)
    return pl.pallas_call(
        body, name=name,
        out_shape=[SDS((N_DEV,) + s, a.dtype) for s, (a, _) in zip(shapes, items)],
        in_specs=[any_spec] * n,
        out_specs=[any_spec] * n,
        scratch_shapes=[pltpu.SemaphoreType.DMA((n, 7)), pltpu.SemaphoreType.DMA((n, 7)), pltpu.SemaphoreType.DMA((n,))],
    )(*[a for a, _ in items])


def _pad8(t):
    return jnp.pad(t, ((0, SUBLANES - t.shape[0]), (0, 0)))


def _rows_merged(w):
    return w.reshape(w.shape[0] * w.shape[1], w.shape[2])


def _local_grads(x, tgt, norm_mix, norm_mlp, norm_kv, norm_final, conv_w, hub):
    bl, seq, D = x.shape
    T = bl * seq
    h = x.reshape(T, D)
    tgt = tgt.reshape(T, D)
    row = lambda t, l: t[l:l + 1]
    W = hub.weights
    saved = []
    kv = h_kv = None
    for l in range(DEPTH):
        if l < N_A_LAYERS:
            bcu = _norm_mm(f"l{l}_in", h, row(norm_mix, l), W["w_a_in", l], out_dtype=BF16, hub=hub)
            h2 = _gate_mm_res(f"l{l}_conv_out", bcu, _pad8(conv_w[l]), _rows_merged(W["w_a_out", l]), h, seq, hub=hub)
            saved.append((h, bcu, h2))
        else:
            i = l - N_A_LAYERS
            if l == N_A_LAYERS:
                h_kv = h
                kv = _norm_mm("kv", h, norm_kv.reshape(1, D), W["w_kv", None], hub=hub)
            q = _norm_mm(f"l{l}_q", h, row(norm_mix, l), W["w_q", i])
            per_group = [_attn_fwd(f"l{l}_att{g}", q, kv, g, bl, hub=hub) for g in range(N_GROUPS)]
            o, lse = _combine(f"l{l}_combine", [p[0] for p in per_group], [p[1] for p in per_group])
            h2 = _mm_res(f"l{l}_att_out", o, W["w_o", i], h)
            saved.append((h, q, o, lse, h2))
        a = _norm_mm(f"l{l}_up", h2, row(norm_mlp, l), W["w_up", l], out_dtype=BF16, hub=hub)
        h = _relu2_mm_res(f"l{l}_down", a, _rows_merged(W["w_down", l]), h2, hub=hub)
        saved[-1] = saved[-1] + (a,)

    dh, sq_err, d_norm_final = _final_loss("loss", h, tgt, norm_final.reshape(1, D))

    d_norm_mix = [None] * DEPTH
    d_norm_mlp = [None] * DEPTH
    d_conv = [None] * N_A_LAYERS
    d_norm_kv = None
    dkv_acc = [None] * N_GROUPS
    G = hub.grads
    as_slots = lambda g: g.reshape(N_DEV, g.shape[0] // N_DEV, g.shape[1])
    tt = DW_TOKENS
    tc = DW_TOKENS // 2
    for l in reversed(range(DEPTH)):
        a, h2 = saved[l][-1], saved[l][-2]
        h_in = saved[l][0]
        g_mlp = row(norm_mlp, l)
        g_mix = row(norm_mix, l)
        w_up_l = W["w_up", l]
        FF = N_DEV * w_up_l.shape[2]
        da = _nt_relu2_bwd(f"l{l}_down_bwd", dh, _rows_merged(W["w_down", l]), a, hub=hub)
        G["w_down", l] = as_slots(_tn(f"l{l}_dw_down", [_rows2(a, tt, FF // 2, lambda s: s)], _relu2,
                                      [_rows2(dh, tt)], _val, FF, D, T, tt, split=("k", 2)))
        G["w_up", l] = _tn(f"l{l}_dw_up", [_rows2(h2, 2 * tt), _full2(g_mlp)], _normed,
                           [_rows2(da, 2 * tt, FF // 4, lambda s: s)], _val, D, FF, T, 2 * tt, split=("n", 4),
                           out_cols=w_up_l.shape[2], hub=hub)
        dh2, d_norm_mlp[l] = _nt_norm_bwd(f"l{l}_up_bwd", [da], w_up_l, h2, g_mlp, dh, hub=hub)
        if l >= N_A_LAYERS:
            i = l - N_A_LAYERS
            _, q, o, lse, _, _ = saved[l]
            w_o_i, w_q_i = W["w_o", i], W["w_q", i]
            do, delta = _att_out_bwd(f"l{l}_att_out_bwd", dh2, w_o_i, o)
            G["w_o", i] = _tn(f"l{l}_dw_o", [_rows2(o, tt)], _val, [_rows2(dh2, tt)], _val, QW, D, T, tt,
                              out_cols=w_o_i.shape[2])
            dqs = []
            for g in range(N_GROUPS):
                dqs.append(_attn_bwd_dq(f"l{l}_att{g}_dq", q, kv, do, delta, lse, g, bl, hub=hub))
                dkv_acc[g] = _attn_bwd_dkv(f"l{l}_att{g}_dkv", q, kv, do, delta, lse, g, bl, prev=dkv_acc[g], hub=hub)
            G["w_q", i] = _tn(f"l{l}_dw_q", [_rows2(h_in, tt), _full2(g_mix)], _normed,
                              [_rows2(t, tt) for t in dqs], _concat_f32, D, N_GROUPS * QW, T, tt, out_cols=w_q_i.shape[2])
            dh, d_norm_mix[l] = _nt_norm_bwd(f"l{l}_q_bwd", dqs, w_q_i, h_in, g_mix, dh2, hub=hub)
            if l == N_A_LAYERS:
                dkvs = [t for pair in dkv_acc for t in pair]
                g_kv = norm_kv.reshape(1, D)
                w_kv = W["w_kv", None]
                G["w_kv", None] = _tn("dw_kv", [_rows2(h_kv, 256), _full2(g_kv)], _normed,
                                      [_rows2(t, 256) for t in dkvs], _concat_f32, D, 2 * N_GROUPS * QW, T, 256,
                                      out_cols=w_kv.shape[2])
                dh, d_norm_kv = _nt_norm_bwd("kv_bwd", dkvs, w_kv, h_kv, g_kv, dh, hub=hub)
        else:
            _, bcu, _, _ = saved[l]
            cw = _pad8(conv_w[l])
            w_in_l = W["w_a_in", l]
            dgated = _nt_plain(f"l{l}_conv_out_bwd", dh2, _rows_merged(W["w_a_out", l]))

            def gated_tile(b_ref, c_ref, u_ref, ch_ref, uh_ref, cw_ref):
                first = (pl.program_id(1) * tc) % seq == 0
                return _gate(b_ref, c_ref, u_ref, ch_ref, uh_ref, cw_ref, first)[0]

            G["w_a_out", l] = as_slots(_tn(
                f"l{l}_dw_conv_out",
                [_rows2(bcu, tc, D, lambda s: 0), _rows2(bcu, tc, D, lambda s: 1), _rows2(bcu, tc, D, lambda s: 2),
                 _prev8_2(bcu, tc, D, 1), _prev8_2(bcu, tc, D, 2), _full2(cw)], gated_tile,
                [_rows2(dh2, tc)], _val, D, D, T, tc, hub=hub))
            dbcu, d_conv[l] = _conv_bwd(f"l{l}_conv_bwd", bcu, dgated, cw, seq, hub=hub)
            G["w_a_in", l] = _tn(f"l{l}_dw_in", [_rows2(h_in, tt), _full2(g_mix)], _normed,
                                 [_rows2(dbcu, tt, 3 * D // 2, lambda s: s)], _val, D, 3 * D, T, tt, split=("n", 2),
                                 out_cols=w_in_l.shape[2], hub=hub)
            dh, d_norm_mix[l] = _nt_norm_bwd(f"l{l}_in_bwd", [dbcu], w_in_l, h_in, g_mix, dh2, hub=hub)

    small = jnp.concatenate(d_norm_mix + d_norm_mlp + [d_norm_kv, d_norm_final] + d_conv, axis=0)
    return sq_err, dh.reshape(bl, seq, D), small


def kernel(x, norm_mix, norm_mlp, w_a_in, conv_w, w_a_out, norm_kv, w_kv, w_q, w_o, w_up, w_down, norm_final, loss_target, m_norm_mix, m_norm_mlp, m_w_a_in, m_conv_w, m_w_a_out, m_norm_kv, m_w_kv, m_w_q, m_w_o, m_w_up, m_w_down, m_norm_final, v_norm_mix, v_norm_mlp, v_w_a_in, v_conv_w, v_w_a_out, v_norm_kv, v_w_kv, v_w_q, v_w_o, v_w_up, v_w_down, v_norm_final):
    D = x.shape[-1]
    xi, yi, ci = _mesh_pos()
    me_idx = 4 * xi + 2 * yi + ci
    w_big = dict(w_a_in=w_a_in, w_a_out=w_a_out, w_kv=w_kv, w_q=w_q, w_o=w_o, w_up=w_up, w_down=w_down)
    m_big = dict(w_a_in=m_w_a_in, w_a_out=m_w_a_out, w_kv=m_w_kv, w_q=m_w_q, w_o=m_w_o, w_up=m_w_up, w_down=m_w_down)
    v_big = dict(w_a_in=v_w_a_in, w_a_out=v_w_a_out, w_kv=v_w_kv, w_q=v_w_q, w_o=v_w_o, w_up=v_w_up, w_down=v_w_down)
    names = list(w_big)

    shards = {n: w.astype(BF16) for n, w in w_big.items()}
    landing = {n: lax.empty((N_DEV,) + w.shape, BF16) for n, w in w_big.items()}
    hub = _Hub(FETCH_DURING, PUSH_DURING, shards, landing)
    dc = conv_w.shape[-1]
    taps = conv_w.shape[0] * conv_w.shape[1]
    now = ["w_a_in", "w_a_out", "w_up", "w_down"]
    got = _all_gather("gather_layer0", [(shards[n], 0) for n in now] + [(_pad8(conv_w.reshape(taps, dc)), None)])
    for n, w in zip(now, got):
        hub.weights[n, 0] = w
    conv_full = jnp.moveaxis(got[-1][:, :taps], 0, 1).reshape(conv_w.shape[0], conv_w.shape[1], N_DEV * dc)

    sq_err, grad_x, small = _local_grads(x, loss_target, norm_mix, norm_mlp, norm_kv, norm_final, conv_full, hub)
    loss = lax.psum(sq_err * (0.5 / D), ("x", "y", "c"))

    grads, deltas, new_m, new_v = {}, {}, {}, {}
    for n in names:
        shape = w_big[n].shape
        cols = shape[-1]
        flat = lambda t: t.reshape(-1, cols)
        parts = hub.landing[n].reshape(N_DEV, -1, cols)
        outs = _sum8_adamw(f"adamw_{n}", parts, flat(w_big[n]), flat(m_big[n]), flat(v_big[n]), tr=min(256, parts.shape[1]))
        grads[n], deltas[n], new_m[n], new_v[n] = (t.reshape(shape) for t in outs)

    n_gain = 2 * DEPTH + 2
    rows_small = small.shape[0]
    small_all = _all_gather("gather_small_grads", [(small, None)])[0]

    def small_pack(nm, nl, nk, nf, cw):
        gains = jnp.concatenate([nm, nl, nk.reshape(1, D), nf.reshape(1, D)], axis=0)
        taps_full = lax.dynamic_update_slice(jnp.zeros((taps, D), F32), cw.reshape(taps, dc), (0, me_idx * dc))
        return jnp.concatenate([gains, taps_full], axis=0)

    sp = [small_pack(*t) for t in ((norm_mix, norm_mlp, norm_kv, norm_final, conv_w),
                                   (m_norm_mix, m_norm_mlp, m_norm_kv, m_norm_final, m_conv_w),
                                   (v_norm_mix, v_norm_mlp, v_norm_kv, v_norm_final, v_conv_w))]
    small_out = _sum8_adamw("adamw_small", small_all, *sp, tr=rows_small)

    def small_unpack(t):
        res = dict(norm_mix=t[0:DEPTH], norm_mlp=t[DEPTH:2 * DEPTH], norm_kv=t[2 * DEPTH], norm_final=t[2 * DEPTH + 1])
        res["conv_w"] = lax.dynamic_slice(t[n_gain:], (0, me_idx * dc), (taps, dc)).reshape(conv_w.shape)
        return res

    for dst, t in zip((grads, deltas, new_m, new_v), small_out):
        dst.update(small_unpack(t))

    order = ["norm_mix", "norm_mlp", "w_a_in", "conv_w", "w_a_out", "norm_kv", "w_kv", "w_q", "w_o", "w_up", "w_down",
             "norm_final"]
    return (loss, grad_x, *[grads[n] for n in order], *[deltas[n] for n in order], *[new_m[n] for n in order],
            *[new_v[n] for n in order])
```

```python
import functools

import jax
import jax.numpy as jnp
from jax import lax
from jax.experimental import pallas as pl
from jax.experimental.pallas import tpu as pltpu

F32 = jnp.float32
BF16 = jnp.bfloat16
SDS = jax.ShapeDtypeStruct

EPS = 1e-5
N_A_LAYERS = 2
DEPTH = 4
PATTERNS = ((128, 1), (512, 4), (2048, 16))
N_GROUPS = 3
H_G = 8
HEAD_DIM = 64
QW = H_G * HEAD_DIM
ATT_BLK = 128
ALIBI_MAX_BIAS = 8.0
NEG_INF = -1e30

ADAM_LR = 0.001
ADAM_B1 = 0.9
ADAM_B2 = 0.999
ADAM_EPS = 1e-08
ADAM_WD = 0.01
ADAM_STEP = 10

N_DEV = 8
SUBLANES = 8
HALO = 16
V7X_VMEM_LIMIT = 48 * 1024 * 1024
MM_CHUNK = 512
MM_ROWS = 512
DW_TOKENS = 1024

FETCH_UP_FRONT = [("w_a_in", 0), ("w_a_out", 0)]
FETCH_DURING = {
    "l0_in": [("w_up", 0)], "l0_conv_out": [("w_down", 0)], "l0_up": [("w_a_in", 1), ("w_a_out", 1)], "l0_down": [("w_up", 1)],
    "l1_in": [("w_down", 1)], "l1_conv_out": [("w_kv", None)],
    "l1_up": [("w_q", 0), ("w_o", 0), ("w_q", 1), ("w_o", 1)], "l1_down": [("w_up", 2)],
    "kv": [("w_down", 2)], "l2_att0": [("w_up", 3)], "l2_att1": [("w_down", 3)],
}
PUSH_DURING = {
    "l3_dw_up": [("w_down", 3, 0, 2)], "l3_up_bwd": [("w_down", 3, 1, 2)],
    "l3_att0_dq": [("w_up", 3, 0, 2)], "l3_att0_dkv": [("w_up", 3, 1, 2)], "l3_att1_dq": [("w_o", 1)], "l3_q_bwd": [("w_q", 1)],
    "l2_dw_up": [("w_down", 2, 0, 2)], "l2_up_bwd": [("w_down", 2, 1, 2)],
    "l2_att0_dq": [("w_up", 2, 0, 2)], "l2_att0_dkv": [("w_up", 2, 1, 2)], "l2_att1_dq": [("w_o", 0)], "l2_q_bwd": [("w_q", 0)],
    "kv_bwd": [("w_kv", None, 0, 2)], "l1_down_bwd": [("w_kv", None, 1, 2)],
    "l1_dw_up": [("w_down", 1, 0, 2)], "l1_up_bwd": [("w_down", 1, 1, 2)], "l1_conv_bwd": [("w_up", 1, 0, 2)],
    "l1_dw_in": [("w_up", 1, 1, 2), ("w_a_out", 1)], "l1_in_bwd": [("w_a_in", 1, 0, 2)], "l0_down_bwd": [("w_a_in", 1, 1, 2)],
    "l0_dw_up": [("w_down", 0, 0, 2)], "l0_up_bwd": [("w_down", 0, 1, 2)], "l0_conv_bwd": [("w_up", 0, 0, 2)],
    "l0_dw_in": [("w_up", 0, 1, 2), ("w_a_out", 0)], "l0_in_bwd": [("w_a_in", 0)],
}


def _mesh_pos():
    return lax.axis_index("x"), lax.axis_index("y"), lax.axis_index("c")


def _flip(v, bit):
    return 1 - v if bit else v


class _Transfer:
    def __init__(self, kind, key, src, src_idx=None, dst=None, dst_idx=None, dst_shape=None, rows=None):
        self.kind, self.key, self.src, self.src_idx = kind, key, src, src_idx
        self.dst, self.dst_idx, self.dst_shape, self.rows = dst, dst_idx, dst_shape, rows

    def copies(self, src_ref, dst_ref, send_sems, recv_sems, local_sem):
        x, y, c = _mesh_pos()
        me = 4 * x + 2 * y + c
        part = (lambda r: r) if self.rows is None else (lambda r: r.at[pl.ds(*self.rows)])

        def dst_slot(j):
            r = dst_ref.at[j]
            return part(r if self.dst_idx is None else r.at[self.dst_idx])

        def copy(k, src, dst_j, to):
            return pltpu.make_async_remote_copy(
                src_ref=src, dst_ref=dst_slot(dst_j), send_sem=send_sems.at[k], recv_sem=recv_sems.at[k],
                device_id=to, device_id_type=pl.DeviceIdType.MESH)

        if self.kind == "exchange":
            local = pltpu.make_async_copy(part(src_ref.at[me]), dst_slot(me), local_sem)
            sends, arrivals = [], []
            for k in range(1, N_DEV):
                peer = (_flip(x, k & 4), _flip(y, k & 2), _flip(c, k & 1))
                peer_idx = 4 * peer[0] + 2 * peer[1] + peer[2]
                sends.append(copy(k - 1, part(src_ref.at[peer_idx]), me, peer))
                arrivals.append(copy(k - 1, part(src_ref.at[peer_idx]), peer_idx, peer))
            return local, sends, [], arrivals

        own = part(src_ref if self.src_idx is None else src_ref.at[self.src_idx])
        idx = lambda px, py, pc: 4 * px + 2 * py + pc
        sibling = (x, y, 1 - c)
        chips = [(1 - x, y), (x, 1 - y), (1 - x, 1 - y)]
        local = pltpu.make_async_copy(own, dst_slot(me), local_sem)
        sends = [copy(0, own, me, sibling)] + [copy(1 + j, own, me, (*chip, c)) for j, chip in enumerate(chips)]
        relays = [(copy(1 + j, own, idx(*chip, c), sibling), copy(4 + j, dst_slot(idx(*chip, c)), idx(*chip, c), sibling))
                  for j, chip in enumerate(chips)]
        arrivals = [copy(0, own, idx(*sibling), sibling)]
        arrivals += [copy(4 + j, own, idx(*chip, 1 - c), sibling) for j, chip in enumerate(chips)]
        return local, sends, relays, arrivals


class _Hub:
    def __init__(self, fetch, push, shards, landing):
        self.fetch, self.push, self.shards, self.landing = fetch, push, shards, landing
        self.weights = {}
        self.arriving = {}
        self.grads = {}

    def transfers(self, host):
        out = []
        for name, l, *part in self.fetch.get(host, ()):
            src = self.shards[name]
            shard = tuple(src.shape if l is None else src.shape[1:])
            p, n = part or (0, 1)
            rows = None if n == 1 else (p * (shard[0] // n), shard[0] // n)
            out.append(_Transfer("gather", (name, l, p == n - 1), src, src_idx=l, dst=self.arriving.get((name, l)),
                                 dst_shape=(N_DEV,) + shard, rows=rows))
        for name, l, *part in self.push.get(host, ()):
            src = self.grads[name, l]
            p, n = part or (0, 1)
            rows = None if n == 1 else (p * (src.shape[1] // n), src.shape[1] // n)
            out.append(_Transfer("exchange", (name, l, p == n - 1), src, dst=self.landing[name], dst_idx=l, rows=rows))
        return out

    def accept(self, transfers, results):
        for t, r in zip(transfers, results):
            name, l, complete = t.key
            if t.kind == "exchange":
                self.landing[name] = r
            elif complete:
                self.weights[name, l] = r
            else:
                self.arriving[name, l] = r


def _call(name, body, grid, ins, outs, scratch=(), hub=None):
    transfers = hub.transfers(name) if hub is not None else []
    n_in, n_out, n_scr, n_tr = len(ins), len(outs), len(scratch), len(transfers)
    c_in, c_out, aliases, places = [], [], {}, []
    for t in transfers:
        c_in.append(t.src)
        src_pos = len(c_in) - 1
        if t.dst is not None:
            c_in.append(t.dst)
            aliases[n_in + len(c_in) - 1] = n_out + len(c_out)
            c_out.append(SDS(t.dst.shape, t.dst.dtype))
        else:
            c_out.append(SDS(t.dst_shape, t.src.dtype))
        places.append((src_pos, len(c_out) - 1))
    sems = [pltpu.SemaphoreType.DMA((n_tr, N_DEV - 1)), pltpu.SemaphoreType.DMA((n_tr, N_DEV - 1)),
            pltpu.SemaphoreType.DMA((n_tr,))] if n_tr else []

    def wrapped(*refs):
        in_refs = refs[:n_in]
        cin_refs = refs[n_in:n_in + len(c_in)]
        o0 = n_in + len(c_in)
        out_refs = refs[o0:o0 + n_out]
        cout_refs = refs[o0 + n_out:o0 + n_out + len(c_out)]
        s0 = o0 + n_out + len(c_out)
        scr_refs = refs[s0:s0 + n_scr]
        if n_tr:
            send_sems, recv_sems, local_sems = refs[s0 + n_scr:]
            first = last = relay = None
            for ax, n in enumerate(grid):
                i = pl.program_id(ax)
                at_relay = (i == max(n - 2, 0)) if ax == len(grid) - 1 else (i == n - 1)
                first = (i == 0) if first is None else first & (i == 0)
                last = (i == n - 1) if last is None else last & (i == n - 1)
                relay = at_relay if relay is None else relay & at_relay

            def all_copies():
                return [t.copies(cin_refs[sp], cout_refs[dp], send_sems.at[n], recv_sems.at[n], local_sems.at[n])
                        for n, (t, (sp, dp)) in enumerate(zip(transfers, places))]

            @pl.when(first)
            def _():
                for local, sends, _, _ in all_copies():
                    local.start()
                    for cp in sends:
                        cp.start()

            def pass_on():
                @pl.when(relay)
                def _():
                    for _, _, relays, _ in all_copies():
                        for arrival, onward in relays:
                            arrival.wait_recv()
                            onward.start()

            if grid[-1] > 1:
                pass_on()

        body(*in_refs, *out_refs, *scr_refs)

        if n_tr:
            if grid[-1] == 1:
                pass_on()

            @pl.when(last)
            def _():
                for local, sends, relays, arrivals in all_copies():
                    for cp in arrivals:
                        cp.wait_recv()
                    for cp in sends + [onward for _, onward in relays]:
                        cp.wait_send()
                    local.wait()

    any_spec = pl.BlockSpec(memory_space=pl.ANY)
    res = pl.pallas_call(
        wrapped,
        name=name,
        grid=grid,
        in_specs=[s for _, s in ins] + [any_spec] * len(c_in),
        out_specs=[s for _, s in outs] + [any_spec] * len(c_out),
        out_shape=[o for o, _ in outs] + c_out,
        scratch_shapes=list(scratch) + sems,
        input_output_aliases=aliases,
        compiler_params=pltpu.CompilerParams(
            dimension_semantics=("arbitrary",) * len(grid), vmem_limit_bytes=V7X_VMEM_LIMIT),
    )(*[a for a, _ in ins], *c_in)
    if n_tr:
        hub.accept(transfers, res[n_out:])
    return res[:n_out]


def _rows(a, tm, cb=None, col=0):
    cb = cb or a.shape[1]
    return (a, pl.BlockSpec((tm, cb), lambda i: (i, col)))


def _full(a):
    nd = a.ndim
    return (a, pl.BlockSpec(a.shape, lambda i: (0,) * nd))


def _prev8(a, tm, cb, col):
    return (a, pl.BlockSpec((HALO, cb), lambda i: (jnp.maximum(i * (tm // HALO) - 1, 0), col)))


def _next8(a, tm, cb, col):
    last = a.shape[0] // HALO - 1
    return (a, pl.BlockSpec((HALO, cb), lambda i: (jnp.minimum((i + 1) * (tm // HALO), last), col)))


def _rows2(a, tt, cb=None, colfn=None):
    cb = cb or a.shape[1]
    colfn = colfn or (lambda s: 0)
    return (a, pl.BlockSpec((tt, cb), lambda s, t: (t, colfn(s))))


def _full2(a):
    nd = a.ndim
    return (a, pl.BlockSpec(a.shape, lambda s, t: (0,) * nd))


def _prev8_2(a, tt, cb, col):
    return (a, pl.BlockSpec((HALO, cb), lambda s, t: (jnp.maximum(t * (tt // HALO) - 1, 0), col)))


def _out_rows(T, n, dtype, tm):
    return (SDS((T, n), dtype), pl.BlockSpec((tm, n), lambda i: (i, 0)))


def _out_acc8(d):
    return (SDS((SUBLANES, d), F32), pl.BlockSpec((SUBLANES, d), lambda i: (0, 0)))


def _rstd(x):
    return lax.rsqrt(jnp.mean(x * x, axis=-1, keepdims=True) + EPS)


def _normed(h_ref, g_ref):
    x = h_ref[...]
    return x * _rstd(x) * g_ref[...]


def _acc8(ref, val, i, n):
    part = val.reshape(-1, SUBLANES, val.shape[-1]).sum(axis=0)

    @pl.when(i == 0)
    def _():
        ref[...] = part

    @pl.when(i > 0)
    def _():
        ref[...] += part

    @pl.when(i == n - 1)
    def _():
        ref[...] = jnp.broadcast_to(jnp.sum(ref[...], axis=0, keepdims=True), ref.shape)


def _gate(b_ref, c_ref, u_ref, ch_ref, uh_ref, cw_ref, first):
    b, c, u = (r[...].astype(F32) for r in (b_ref, c_ref, u_ref))
    cu = c * u
    halo = jnp.where(first, 0.0, ch_ref[...].astype(F32) * uh_ref[...].astype(F32))
    rows = lax.broadcasted_iota(jnp.int32, cu.shape, 0)
    h1 = halo[HALO - 1:HALO, :]
    h2 = halo[HALO - 2:HALO - 1, :]
    cu1 = jnp.where(rows == 0, h1, pltpu.roll(cu, 1, 0))
    cu2 = jnp.where(rows == 0, h2, jnp.where(rows == 1, h1, pltpu.roll(cu, 2, 0)))
    conv = cw_ref[0:1, :] * cu + cw_ref[1:2, :] * cu1 + cw_ref[2:3, :] * cu2
    return b * conv, (b, c, u), conv, (cu, cu1, cu2)


def _relu2(a_ref):
    r = jnp.maximum(a_ref[...].astype(F32), 0.0)
    return r * r


def _dot(a, b):
    return jnp.dot(a, b, preferred_element_type=F32)


def _dot_nt(a, b):
    return lax.dot_general(a, b, (((1,), (1,)), ((), ())), preferred_element_type=F32)


def _dot_tn(a, b):
    return lax.dot_general(a, b, (((0,), (0,)), ((), ())), preferred_element_type=F32)


def _chunks(n):
    c = min(MM_CHUNK, n)
    while n % c:
        c -= 128
    assert c > 0, n
    return [(k * c, (k + 1) * c) for k in range(n // c)]


def _col_weight(w):
    _, K, ns = w.shape
    N = N_DEV * ns
    direct = ns % 128 == 0 and ns >= 256
    scratch = [] if direct else [pltpu.VMEM((K, N), BF16)]

    def prepare(w_ref, s_ref, step):
        if direct:
            return

        @pl.when(step == 0)
        def _():
            for j in range(N_DEV):
                s_ref[:, j * ns:(j + 1) * ns] = w_ref[j]

    def chunks(w_ref, s_ref):
        if direct:
            return [(j * ns, (j + 1) * ns, (lambda j=j: w_ref[j])) for j in range(N_DEV)]
        return [(lo, hi, (lambda lo=lo, hi=hi: s_ref[:, lo:hi])) for lo, hi in _chunks(N)]

    return N, scratch, prepare, chunks


def _norm_mm(name, h, g, w, tm=MM_ROWS, out_dtype=F32, hub=None):
    T, _ = h.shape
    N, w_scratch, prepare, chunks = _col_weight(w)

    def body(h_ref, g_ref, w_ref, o_ref, *s):
        s_ref = s[0] if s else None
        prepare(w_ref, s_ref, pl.program_id(0))
        a = _normed(h_ref, g_ref).astype(BF16)
        for lo, hi, load in chunks(w_ref, s_ref):
            o_ref[:, lo:hi] = _dot(a, load()).astype(out_dtype)

    return _call(name, body, (T // tm,), [_rows(h, tm), _full(g), _full(w)], [_out_rows(T, N, out_dtype, tm)],
                 scratch=w_scratch, hub=hub)[0]


def _gate_mm_res(name, bcu, cw, w, h, seq, tm=MM_ROWS, hub=None):
    T, D = h.shape

    def body(b_ref, c_ref, u_ref, ch_ref, uh_ref, cw_ref, w_ref, h_ref, o_ref):
        first = (pl.program_id(0) * tm) % seq == 0
        gated = _gate(b_ref, c_ref, u_ref, ch_ref, uh_ref, cw_ref, first)[0].astype(BF16)
        for lo, hi in _chunks(D):
            o_ref[:, lo:hi] = h_ref[:, lo:hi] + _dot(gated, w_ref[:, lo:hi])

    ins = [_rows(bcu, tm, D, 0), _rows(bcu, tm, D, 1), _rows(bcu, tm, D, 2), _prev8(bcu, tm, D, 1),
           _prev8(bcu, tm, D, 2), _full(cw), _full(w), _rows(h, tm)]
    return _call(name, body, (T // tm,), ins, [_out_rows(T, D, F32, tm)], hub=hub)[0]


def _relu2_mm_res(name, a, w, h, tm=MM_ROWS, hub=None):
    T, D = h.shape
    K = a.shape[1]

    def body(a_ref, w_ref, h_ref, o_ref, acc_ref):
        for n, (lo, hi) in enumerate(_chunks(K)):
            d = _dot(_relu2(a_ref.at[:, lo:hi]).astype(BF16), w_ref[lo:hi, :])
            if n == 0:
                acc_ref[...] = d
            else:
                acc_ref[...] += d
        o_ref[...] = h_ref[...] + acc_ref[...]

    return _call(name, body, (T // tm,), [_rows(a, tm), _full(w), _rows(h, tm)], [_out_rows(T, D, F32, tm)],
                 scratch=[pltpu.VMEM((tm, D), F32)], hub=hub)[0]


def _mm_res(name, a, w, h, tm=MM_ROWS):
    T, D = h.shape
    _, w_scratch, prepare, chunks = _col_weight(w)

    def body(a_ref, w_ref, h_ref, o_ref, *s):
        s_ref = s[0] if s else None
        prepare(w_ref, s_ref, pl.program_id(0))
        av = a_ref[...].astype(BF16)
        for lo, hi, load in chunks(w_ref, s_ref):
            o_ref[:, lo:hi] = h_ref[:, lo:hi] + _dot(av, load())

    return _call(name, body, (T // tm,), [_rows(a, tm), _full(w), _rows(h, tm)], [_out_rows(T, D, F32, tm)],
                 scratch=w_scratch)[0]


def _nt_relu2_bwd(name, dh, w, a, tm=MM_ROWS, hub=None):
    T, _ = dh.shape
    K = w.shape[0]

    def body(dh_ref, w_ref, a_ref, o_ref):
        d = dh_ref[...].astype(BF16)
        for lo, hi in _chunks(K):
            dr = _dot_nt(d, w_ref[lo:hi, :])
            o_ref[:, lo:hi] = (dr * (2.0 * jnp.maximum(a_ref[:, lo:hi].astype(F32), 0.0))).astype(BF16)

    return _call(name, body, (T // tm,), [_rows(dh, tm), _full(w), _rows(a, tm)], [_out_rows(T, K, BF16, tm)], hub=hub)[0]


def _concat_bf16(*refs):
    vals = [r[...].astype(BF16) for r in refs]
    return vals[0] if len(vals) == 1 else jnp.concatenate(vals, axis=1)


def _nt_plain(name, dy, w, tm=MM_ROWS):
    T, N = dy.shape
    if w.ndim == 3:
        K = w.shape[1]
        _, w_scratch, prepare, chunks = _col_weight(w)
    else:
        K = w.shape[0]
        w_scratch, prepare = [], (lambda w_ref, s_ref, step: None)
        chunks = lambda w_ref, s_ref: [(lo, hi, (lambda lo=lo, hi=hi: w_ref[:, lo:hi])) for lo, hi in _chunks(N)]

    def body(dy_ref, w_ref, o_ref, acc_ref, *s):
        s_ref = s[0] if s else None
        prepare(w_ref, s_ref, pl.program_id(0))
        for n, (lo, hi, load) in enumerate(chunks(w_ref, s_ref)):
            d = _dot_nt(dy_ref[:, lo:hi].astype(BF16), load())
            if n == 0:
                acc_ref[...] = d
            else:
                acc_ref[...] += d
        o_ref[...] = acc_ref[...]

    return _call(name, body, (T // tm,), [_rows(dy, tm), _full(w)], [_out_rows(T, K, F32, tm)],
                 scratch=[pltpu.VMEM((tm, K), F32)] + w_scratch)[0]


def _att_out_bwd(name, dy, w, o, tm=MM_ROWS):
    T, _ = dy.shape
    K = w.shape[1]
    _, w_scratch, prepare, chunks = _col_weight(w)

    def body(dy_ref, w_ref, o_ref, do_ref, dl_ref, acc_ref, *s):
        s_ref = s[0] if s else None
        prepare(w_ref, s_ref, pl.program_id(0))
        for n, (lo, hi, load) in enumerate(chunks(w_ref, s_ref)):
            d = _dot_nt(dy_ref[:, lo:hi].astype(BF16), load())
            if n == 0:
                acc_ref[...] = d
            else:
                acc_ref[...] += d
        do = acc_ref[...]
        do_ref[...] = do
        prod = do * o_ref[...]
        high = prod.astype(BF16)
        low = (prod - high.astype(F32)).astype(BF16)
        head_of = lambda axis: jnp.right_shift(lax.broadcasted_iota(jnp.int32, (K, K), axis), HEAD_DIM.bit_length() - 1)
        same_head = jnp.where(head_of(0) == head_of(1), 1.0, 0.0).astype(BF16)
        dl_ref[...] = _dot(high, same_head) + _dot(low, same_head)

    outs = [_out_rows(T, K, F32, tm), _out_rows(T, K, F32, tm)]
    return _call(name, body, (T // tm,), [_rows(dy, tm), _full(w), _rows(o, tm)], outs,
                 scratch=[pltpu.VMEM((tm, K), F32)] + w_scratch)


def _nt_norm_bwd(name, dys, w, h, g, dh_in, tm=MM_ROWS, hub=None):
    T, D = h.shape
    _, w_scratch, prepare, chunks = _col_weight(w)
    n_steps = T // tm
    n_dy = len(dys)

    def body(*refs):
        dy_refs = refs[:n_dy]
        w_ref, h_ref, g_ref, dhin_ref, o_ref, dg_ref, acc_ref = refs[n_dy:n_dy + 7]
        s_ref = refs[n_dy + 7] if len(refs) > n_dy + 7 else None
        i = pl.program_id(0)
        prepare(w_ref, s_ref, i)
        dy = _concat_bf16(*dy_refs)
        for n, (lo, hi, load) in enumerate(chunks(w_ref, s_ref)):
            d = _dot_nt(dy[:, lo:hi], load())
            if n == 0:
                acc_ref[...] = d
            else:
                acc_ref[...] += d
        dn = acc_ref[...]
        x = h_ref[...]
        rstd = _rstd(x)
        xhat = x * rstd
        dxhat = dn * g_ref[...]
        dx = rstd * (dxhat - xhat * jnp.mean(dxhat * xhat, axis=-1, keepdims=True))
        o_ref[...] = dhin_ref[...] + dx
        _acc8(dg_ref, dn * xhat, i, n_steps)

    ins = [_rows(d, tm) for d in dys] + [_full(w), _rows(h, tm), _full(g), _rows(dh_in, tm)]
    outs = [_out_rows(T, D, F32, tm), _out_acc8(D)]
    dh, dg = _call(name, body, (n_steps,), ins, outs, scratch=[pltpu.VMEM((tm, D), F32)] + w_scratch, hub=hub)
    return dh, dg[0:1]


def _tn(name, a_ins, a_fn, y_ins, y_fn, K, N, T, tt, split=None, out_cols=None, hub=None):
    kind, parts = split or ("n", 1)
    kb, nb = (K // parts, N) if kind == "k" else (K, N // parts)
    n_steps = T // tt
    n_a = len(a_ins)
    n_y = len(y_ins)
    assert out_cols is None or (kind == "n" and nb % out_cols == 0)

    def body(*refs):
        a_refs = refs[:n_a]
        y_refs = refs[n_a:n_a + n_y]
        o_ref, acc_ref = refs[n_a + n_y:]
        t = pl.program_id(1)
        a_t = a_fn(*a_refs).T.astype(BF16)
        y = y_fn(*y_refs).astype(BF16)
        for lo, hi in _chunks(nb):
            d = _dot(a_t, y[:, lo:hi])

            @pl.when(t == 0)
            def _():
                acc_ref[:, lo:hi] = d

            @pl.when(t > 0)
            def _():
                acc_ref[:, lo:hi] += d

        @pl.when(t == n_steps - 1)
        def _():
            if out_cols is None:
                o_ref[...] = acc_ref[...].astype(BF16)
            else:
                for j in range(nb // out_cols):
                    o_ref[j] = acc_ref[:, j * out_cols:(j + 1) * out_cols].astype(BF16)

    if out_cols is None:
        out = (SDS((K, N), BF16), pl.BlockSpec((kb, nb), (lambda s, t: (s, 0)) if kind == "k" else (lambda s, t: (0, s))))
    else:
        out = (SDS((N // out_cols, K, out_cols), BF16), pl.BlockSpec((nb // out_cols, K, out_cols), lambda s, t: (s, 0, 0)))
    return _call(name, body, (parts, n_steps), list(a_ins) + list(y_ins), [out],
                 scratch=[pltpu.VMEM((kb, nb), F32)], hub=hub)[0]


def _val(ref):
    return ref[...]


def _concat_f32(*refs):
    vals = [r[...] for r in refs]
    return vals[0] if len(vals) == 1 else jnp.concatenate(vals, axis=1)


ATT_TILE_ROWS = 2048
HEAD_PAIRS = H_G // 2
ATT_SCALE = HEAD_DIM ** -0.5
ATT_UNITS_TOGETHER = 4


def _slope(h):
    return 2.0 ** (-ALIBI_MAX_BIAS * (h + 1) / H_G)


def _att_geom(T, bl, g):
    dil = PATTERNS[g][1]
    sub = ATT_BLK * dil
    nsub = max(1, ATT_TILE_ROWS // sub)
    rows = sub * nsub
    return dil, sub, nsub, rows, T // bl // rows


def _att_specs(T, bl, g):
    _, sub, nsub, rows, nt = _att_geom(T, bl, g)
    last_sub = T // sub - 1
    tile = lambda col: pl.BlockSpec((rows, 128), lambda b, i, hp: (b * nt + i, col(hp)))
    prev = lambda col: pl.BlockSpec((sub, 128), lambda b, i, hp: (jnp.maximum((b * nt + i) * nsub - 1, 0), col(hp)))
    nxt = lambda col: pl.BlockSpec((sub, 128), lambda b, i, hp: (jnp.minimum((b * nt + i + 1) * nsub, last_sub), col(hp)))
    return tile, prev, nxt


def _sub_rows(j, r, dil):
    start = j * ATT_BLK * dil + r
    return pl.ds(start, ATT_BLK, stride=dil) if dil > 1 else pl.ds(start, ATT_BLK)


def _att_consts(hp, dil):
    h0 = lax.broadcasted_iota(jnp.int32, (ATT_BLK, 128), 1) < HEAD_DIM
    a = lax.broadcasted_iota(jnp.int32, (ATT_BLK, ATT_BLK), 0)
    c = lax.broadcasted_iota(jnp.int32, (ATT_BLK, ATT_BLK), 1)
    dist_p = ((ATT_BLK + a - c) * dil).astype(F32)
    dist_c = ((a - c) * dil).astype(F32)
    bias_p, bias_c = [], []
    for h in range(2):
        slope = jnp.float32(_slope(2 * (HEAD_PAIRS - 1) + h))
        for p in range(HEAD_PAIRS - 2, -1, -1):
            slope = jnp.where(hp == p, jnp.float32(_slope(2 * p + h)), slope)
        bias_p.append(jnp.where(c >= a, -slope * dist_p, NEG_INF))
        bias_c.append(jnp.where(c <= a, -slope * dist_c, NEG_INF))
    return h0, bias_p, bias_c


def _split_heads(x, h0):
    return [jnp.where(h0, x, 0.0).astype(BF16), jnp.where(h0, 0.0, x).astype(BF16)]


def _head_cols(x):
    return [x[:, 0:1], x[:, HEAD_DIM:HEAD_DIM + 1]]


def _in_groups(units, first_stage, *later_stages):
    for u0 in range(0, len(units), ATT_UNITS_TOGETHER):
        staged = [first_stage(*u) for u in units[u0:u0 + ATT_UNITS_TOGETHER]]
        for stage in later_stages:
            staged = [stage(*s) for s in staged]


def _attn_fwd(name, q, kv, g, bl, hub=None):
    T = q.shape[0]
    dil, _, nsub, _, _ = _att_geom(T, bl, g)
    tile, prev, _ = _att_specs(T, bl, g)

    def body(q_ref, kp_ref, kc_ref, vp_ref, vc_ref, o_ref, lse_ref):
        first = pl.program_id(1) == 0
        h0, bias_p, bias_c = _att_consts(pl.program_id(2), dil)
        bias_first = [jnp.where(first, NEG_INF, b) for b in bias_p]
        ones = jnp.ones((ATT_BLK, 128), BF16)

        def scores(j, r):
            cur = _sub_rows(j, r, dil)
            if j == 0:
                before = _sub_rows(0, r, dil)
                kp, vp, bp = kp_ref[before, :], vp_ref[before, :], bias_first
            else:
                before = _sub_rows(j - 1, r, dil)
                kp, vp, bp = kc_ref[before, :], vc_ref[before, :], bias_p
            kp, kc = kp.astype(BF16), kc_ref[cur, :].astype(BF16)
            qh = _split_heads(q_ref[cur, :] * ATT_SCALE, h0)
            sp = [_dot_nt(qh[h], kp) + bp[h] for h in range(2)]
            sc = [_dot_nt(qh[h], kc) + bias_c[h] for h in range(2)]
            return cur, sp, sc, vp.astype(BF16), vc_ref[cur, :].astype(BF16)

        def weights(cur, sp, sc, vp, vc):
            mx = [jnp.max(jnp.maximum(sp[h], sc[h]), axis=-1, keepdims=True) for h in range(2)]
            ep = [jnp.exp(sp[h] - mx[h]).astype(BF16) for h in range(2)]
            ec = [jnp.exp(sc[h] - mx[h]).astype(BF16) for h in range(2)]
            return cur, mx, ep, ec, vp, vc

        def outputs(cur, mx, ep, ec, vp, vc):
            den = [_dot(ep[h], ones) + _dot(ec[h], ones) for h in range(2)]
            acc = [_dot(ep[h], vp) + _dot(ec[h], vc) for h in range(2)]
            o_ref[cur, :] = jnp.where(h0, acc[0] / den[0], acc[1] / den[1])
            lse_ref[cur, :] = jnp.where(h0, mx[0] + jnp.log(den[0]), mx[1] + jnp.log(den[1]))
            return ()

        _in_groups([(j, r) for j in range(nsub) for r in range(dil)], scores, weights, outputs)

    ins = [(q, tile(lambda hp: 4 * g + hp)), (kv, prev(lambda hp: 8 * g + hp)), (kv, tile(lambda hp: 8 * g + hp)),
           (kv, prev(lambda hp: 8 * g + 4 + hp)), (kv, tile(lambda hp: 8 * g + 4 + hp))]
    out = (SDS((T, QW), F32), tile(lambda hp: hp))
    _, _, _, _, nt = _att_geom(T, bl, g)
    return _call(name, body, (bl, nt, HEAD_PAIRS), ins, [out, out], hub=hub)


def _combine(name, os_, lses, tm=512):
    T = os_[0].shape[0]

    def body(o0, o1, o2, l0, l1, l2, o_ref, lse_ref):
        ls = [l0[...], l1[...], l2[...]]
        mx = jnp.maximum(jnp.maximum(ls[0], ls[1]), ls[2])
        es = [jnp.exp(l - mx) for l in ls]
        den = es[0] + es[1] + es[2]
        o_ref[...] = (es[0] * o0[...] + es[1] * o1[...] + es[2] * o2[...]) / den
        lse_ref[...] = mx + jnp.log(den)

    ins = [_rows(t, tm) for t in list(os_) + list(lses)]
    return _call(name, body, (T // tm,), ins, [_out_rows(T, QW, F32, tm), _out_rows(T, QW, F32, tm)])


def _attn_bwd_dq(name, q, kv, do, delta, lse, g, bl, hub=None):
    T = q.shape[0]
    dil, _, nsub, _, nt = _att_geom(T, bl, g)
    tile, prev, _ = _att_specs(T, bl, g)

    def body(q_ref, kp_ref, kc_ref, vp_ref, vc_ref, do_ref, dl_ref, lse_ref, dq_ref):
        first = pl.program_id(1) == 0
        h0, bias_p, bias_c = _att_consts(pl.program_id(2), dil)
        bias_first = [jnp.where(first, NEG_INF, b) for b in bias_p]

        def probs(j, r):
            cur = _sub_rows(j, r, dil)
            if j == 0:
                before = _sub_rows(0, r, dil)
                kp, vp, bp = kp_ref[before, :], vp_ref[before, :], bias_first
            else:
                before = _sub_rows(j - 1, r, dil)
                kp, vp, bp = kc_ref[before, :], vc_ref[before, :], bias_p
            kp, vp = kp.astype(BF16), vp.astype(BF16)
            kc, vc = kc_ref[cur, :].astype(BF16), vc_ref[cur, :].astype(BF16)
            qh = _split_heads(q_ref[cur, :] * ATT_SCALE, h0)
            dob = _split_heads(do_ref[cur, :], h0)
            lse_h = _head_cols(lse_ref[cur, :])
            pp = [jnp.exp(_dot_nt(qh[h], kp) + bp[h] - lse_h[h]) for h in range(2)]
            pc = [jnp.exp(_dot_nt(qh[h], kc) + bias_c[h] - lse_h[h]) for h in range(2)]
            dpp = [_dot_nt(dob[h], vp) for h in range(2)]
            dpc = [_dot_nt(dob[h], vc) for h in range(2)]
            return cur, pp, pc, dpp, dpc, kp, kc

        def dscores(cur, pp, pc, dpp, dpc, kp, kc):
            dl = _head_cols(dl_ref[cur, :])
            dsp = [(pp[h] * (dpp[h] - dl[h])).astype(BF16) for h in range(2)]
            dsc = [(pc[h] * (dpc[h] - dl[h])).astype(BF16) for h in range(2)]
            return cur, dsp, dsc, kp, kc

        def outputs(cur, dsp, dsc, kp, kc):
            dqh = [_dot(dsp[h], kp) + _dot(dsc[h], kc) for h in range(2)]
            dq_ref[cur, :] = jnp.where(h0, dqh[0], dqh[1]) * ATT_SCALE
            return ()

        _in_groups([(j, r) for j in range(nsub) for r in range(dil)], probs, dscores, outputs)

    own = lambda hp: hp
    ins = [(q, tile(lambda hp: 4 * g + hp)), (kv, prev(lambda hp: 8 * g + hp)), (kv, tile(lambda hp: 8 * g + hp)),
           (kv, prev(lambda hp: 8 * g + 4 + hp)), (kv, tile(lambda hp: 8 * g + 4 + hp)),
           (do, tile(own)), (delta, tile(own)), (lse, tile(own))]
    return _call(name, body, (bl, nt, HEAD_PAIRS), ins, [(SDS((T, QW), F32), tile(own))], hub=hub)[0]


def _attn_bwd_dkv(name, q, kv, do, delta, lse, g, bl, prev=None, hub=None):
    T = q.shape[0]
    dil, _, nsub, _, nt = _att_geom(T, bl, g)
    tile, _, nxt = _att_specs(T, bl, g)
    has_prev = prev is not None

    def body(*refs):
        k_ref, v_ref, q_ref, qn_ref, do_ref, don_ref, dl_ref, dln_ref, l_ref, ln_ref = refs[:10]
        rest = refs[10:]
        if has_prev:
            dkp_ref, dvp_ref, dk_ref, dv_ref = rest
        else:
            dk_ref, dv_ref = rest
        last = pl.program_id(1) == nt - 1
        h0, bias_p, bias_c = _att_consts(pl.program_id(2), dil)
        bias_last = [jnp.where(last, NEG_INF, b) for b in bias_p]

        def probs(j, r):
            cur = _sub_rows(j, r, dil)
            kb, vb = k_ref[cur, :].astype(BF16), v_ref[cur, :].astype(BF16)
            sets = [(q_ref, do_ref, dl_ref, l_ref, cur, bias_c)]
            if j < nsub - 1:
                sets.append((q_ref, do_ref, dl_ref, l_ref, _sub_rows(j + 1, r, dil), bias_p))
            else:
                sets.append((qn_ref, don_ref, dln_ref, ln_ref, _sub_rows(0, r, dil), bias_last))
            out = []
            for qr, dor, dlr, lr, rows, bias in sets:
                qs = qr[rows, :] * ATT_SCALE
                do2 = dor[rows, :]
                qh = _split_heads(qs, h0)
                dob = _split_heads(do2, h0)
                lse_h = _head_cols(lr[rows, :])
                p = [jnp.exp(_dot_nt(qh[h], kb) + bias[h] - lse_h[h]) for h in range(2)]
                dp = [_dot_nt(dob[h], vb) for h in range(2)]
                out.append((p, dp, dlr, rows, qs.astype(BF16), do2.astype(BF16)))
            return cur, out

        def dscores(cur, sets):
            out = []
            for p, dp, dlr, rows, qsb, do2b in sets:
                dl = _head_cols(dlr[rows, :])
                ds = [(p[h] * (dp[h] - dl[h])).astype(BF16) for h in range(2)]
                out.append(([p[h].astype(BF16) for h in range(2)], ds, qsb, do2b))
            return cur, out

        def outputs(cur, sets):
            dk = [None, None]
            dv = [None, None]
            for pb, ds, qsb, do2b in sets:
                for h in range(2):
                    dvh = _dot_tn(pb[h], do2b)
                    dkh = _dot_tn(ds[h], qsb)
                    dv[h] = dvh if dv[h] is None else dv[h] + dvh
                    dk[h] = dkh if dk[h] is None else dk[h] + dkh
            dk2 = jnp.where(h0, dk[0], dk[1])
            dv2 = jnp.where(h0, dv[0], dv[1])
            if has_prev:
                dk2 = dk2 + dkp_ref[cur, :]
                dv2 = dv2 + dvp_ref[cur, :]
            dk_ref[cur, :] = dk2
            dv_ref[cur, :] = dv2
            return ()

        _in_groups([(j, r) for j in range(nsub) for r in range(dil)], probs, dscores, outputs)

    own = lambda hp: hp
    qcol = lambda hp: 4 * g + hp
    ins = [(kv, tile(lambda hp: 8 * g + hp)), (kv, tile(lambda hp: 8 * g + 4 + hp)), (q, tile(qcol)), (q, nxt(qcol)),
           (do, tile(own)), (do, nxt(own)), (delta, tile(own)), (delta, nxt(own)), (lse, tile(own)), (lse, nxt(own))]
    if has_prev:
        ins += [(prev[0], tile(own)), (prev[1], tile(own))]
    out = (SDS((T, QW), F32), tile(own))
    return _call(name, body, (bl, nt, HEAD_PAIRS), ins, [out, out], hub=hub)


def _final_loss(name, h, tgt, g, tm=256):
    T, D = h.shape
    n_steps = T // tm

    def body(h_ref, t_ref, g_ref, dh_ref, loss_ref, dg_ref, sq_ref):
        i = pl.program_id(0)
        x = h_ref[...]
        rstd = _rstd(x)
        xhat = x * rstd
        err = xhat * g_ref[...] - t_ref[...]
        _acc8(sq_ref, err * err, i, n_steps)
        dy = err * (1.0 / D)
        dxhat = dy * g_ref[...]
        dh_ref[...] = rstd * (dxhat - xhat * jnp.mean(dxhat * xhat, axis=-1, keepdims=True))
        _acc8(dg_ref, dy * xhat, i, n_steps)

        @pl.when(i == n_steps - 1)
        def _():
            loss_ref[...] = jnp.full(loss_ref.shape, jnp.sum(sq_ref[0:1, :]), F32)

    outs = [_out_rows(T, D, F32, tm), (SDS((SUBLANES, 128), F32), pl.BlockSpec((SUBLANES, 128), lambda i: (0, 0))),
            _out_acc8(D)]
    dh, loss, dg = _call(name, body, (n_steps,), [_rows(h, tm), _rows(tgt, tm), _full(g)], outs,
                         scratch=[pltpu.VMEM((SUBLANES, D), F32)])
    return dh, loss[0, 0], dg[0:1]


def _conv_bwd(name, bcu, dgated, cw, seq, tm=256, hub=None):
    T, D = dgated.shape
    n_steps = T // tm

    def body(b_ref, c_ref, u_ref, ch_ref, uh_ref, dg_ref, dgn_ref, bn_ref, cw_ref, o_ref, t0_ref, t1_ref, t2_ref):
        i = pl.program_id(0)
        first = (i * tm) % seq == 0
        last = ((i + 1) * tm) % seq == 0
        _, (b, c, u), conv, (cu, cu1, cu2) = _gate(b_ref, c_ref, u_ref, ch_ref, uh_ref, cw_ref, first)
        dgat = dg_ref[...]
        dconv = dgat * b
        nxt = jnp.where(last, 0.0, dgn_ref[...] * bn_ref[...].astype(F32))
        rows = lax.broadcasted_iota(jnp.int32, dconv.shape, 0)
        n1 = nxt[0:1, :]
        n2 = nxt[1:2, :]
        dc1 = jnp.where(rows == tm - 1, n1, pltpu.roll(dconv, tm - 1, 0))
        dc2 = jnp.where(rows == tm - 1, n2, jnp.where(rows == tm - 2, n1, pltpu.roll(dconv, tm - 2, 0)))
        dcu = cw_ref[0:1, :] * dconv + cw_ref[1:2, :] * dc1 + cw_ref[2:3, :] * dc2
        o_ref[:, 0:D] = (dgat * conv).astype(BF16)
        o_ref[:, D:2 * D] = (dcu * u).astype(BF16)
        o_ref[:, 2 * D:3 * D] = (dcu * c).astype(BF16)
        _acc8(t0_ref, dconv * cu, i, n_steps)
        _acc8(t1_ref, dconv * cu1, i, n_steps)
        _acc8(t2_ref, dconv * cu2, i, n_steps)

    ins = [_rows(bcu, tm, D, 0), _rows(bcu, tm, D, 1), _rows(bcu, tm, D, 2), _prev8(bcu, tm, D, 1), _prev8(bcu, tm, D, 2),
           _rows(dgated, tm), _next8(dgated, tm, D, 0), _next8(bcu, tm, D, 0), _full(cw)]
    outs = [_out_rows(T, 3 * D, BF16, tm), _out_acc8(D), _out_acc8(D), _out_acc8(D)]
    dbcu, t0, t1, t2 = _call(name, body, (n_steps,), ins, outs, hub=hub)
    return dbcu, jnp.concatenate([t0[0:1], t1[0:1], t2[0:1]], axis=0)


def _sum8_adamw(name, parts, w, m, v, tr):
    R, C = w.shape
    b1c = 1.0 - ADAM_B1 ** ADAM_STEP
    b2c = 1.0 - ADAM_B2 ** ADAM_STEP

    def body(p_ref, w_ref, m_ref, v_ref, g_ref, d_ref, nm_ref, nv_ref):
        g = p_ref[0].astype(F32)
        for j in range(1, N_DEV):
            g = g + p_ref[j].astype(F32)
        nm = ADAM_B1 * m_ref[...] + (1.0 - ADAM_B1) * g
        nv = ADAM_B2 * v_ref[...] + (1.0 - ADAM_B2) * (g * g)
        m_hat = nm / b1c
        v_hat = nv / b2c
        g_ref[...] = g
        d_ref[...] = -ADAM_LR * (m_hat / (jnp.sqrt(v_hat) + ADAM_EPS) + ADAM_WD * w_ref[...])
        nm_ref[...] = nm
        nv_ref[...] = nv

    ins = [(parts, pl.BlockSpec((N_DEV, tr, C), lambda i: (0, i, 0))), _rows(w, tr), _rows(m, tr), _rows(v, tr)]
    outs = [_out_rows(R, C, F32, tr)] * 4
    return _call(name, body, (R // tr,), ins, outs)


def _all_gather(name, items):
    n = len(items)
    shapes = [tuple(a.shape if idx is None else a.shape[1:]) for a, idx in items]

    def body(*refs):
        x_refs, out_refs = refs[:n], refs[n:2 * n]
        send_sems, recv_sems, local_sems = refs[2 * n:]
        x, y, c = _mesh_pos()
        me, sibling = (x, y, c), (x, y, 1 - c)
        chips = [(1 - x, y), (x, 1 - y), (1 - x, 1 - y)]

        def copy(t, k, block, to, own=False):
            dst = out_refs[t].at[4 * block[0] + 2 * block[1] + block[2]]
            src = dst
            if own:
                src = x_refs[t] if items[t][1] is None else x_refs[t].at[items[t][1]]
            return pltpu.make_async_remote_copy(
                src_ref=src, dst_ref=dst, send_sem=send_sems.at[t, k], recv_sem=recv_sems.at[t, k],
                device_id=to, device_id_type=pl.DeviceIdType.MESH)

        started = []
        for t in range(n):
            src = x_refs[t] if items[t][1] is None else x_refs[t].at[items[t][1]]
            mine = pltpu.make_async_copy(src, out_refs[t].at[4 * x + 2 * y + c], local_sems.at[t])
            mine.start()
            first = [copy(t, 0, me, sibling, own=True)]
            first += [copy(t, 1 + j, me, (*chip, c), own=True) for j, chip in enumerate(chips)]
            for cp in first:
                cp.start()
            started.append((mine, first))
        passed = []
        for t in range(n):
            for j, chip in enumerate(chips):
                copy(t, 1 + j, (*chip, c), me).wait_recv()
                fwd = copy(t, 4 + j, (*chip, c), sibling)
                fwd.start()
                passed.append(fwd)
        for t in range(n):
            copy(t, 0, sibling, me).wait_recv()
            for j, chip in enumerate(chips):
                copy(t, 4 + j, (*chip, 1 - c), me).wait_recv()
        for mine, first in started:
            for cp in first:
                cp.wait_send()
            mine.wait()
        for cp in passed:
            cp.wait_send()

    any_spec = pl.BlockSpec(memory_space=pl.ANY)
    return pl.pallas_call(
        body, name=name,
        out_shape=[SDS((N_DEV,) + s, a.dtype) for s, (a, _) in zip(shapes, items)],
        in_specs=[any_spec] * n,
        out_specs=[any_spec] * n,
        scratch_shapes=[pltpu.SemaphoreType.DMA((n, 7)), pltpu.SemaphoreType.DMA((n, 7)), pltpu.SemaphoreType.DMA((n,))],
    )(*[a for a, _ in items])


def _pad8(t):
    return jnp.pad(t, ((0, SUBLANES - t.shape[0]), (0, 0)))


def _rows_merged(w):
    return w.reshape(w.shape[0] * w.shape[1], w.shape[2])


def _local_grads(x, tgt, norm_mix, norm_mlp, norm_kv, norm_final, conv_w, hub):
    bl, seq, D = x.shape
    T = bl * seq
    h = x.reshape(T, D)
    tgt = tgt.reshape(T, D)
    row = lambda t, l: t[l:l + 1]
    W = hub.weights
    saved = []
    kv = h_kv = None
    for l in range(DEPTH):
        if l < N_A_LAYERS:
            bcu = _norm_mm(f"l{l}_in", h, row(norm_mix, l), W["w_a_in", l], out_dtype=BF16, hub=hub)
            h2 = _gate_mm_res(f"l{l}_conv_out", bcu, _pad8(conv_w[l]), _rows_merged(W["w_a_out", l]), h, seq, hub=hub)
            saved.append((h, bcu, h2))
        else:
            i = l - N_A_LAYERS
            if l == N_A_LAYERS:
                h_kv = h
                kv = _norm_mm("kv", h, norm_kv.reshape(1, D), W["w_kv", None], hub=hub)
            q = _norm_mm(f"l{l}_q", h, row(norm_mix, l), W["w_q", i])
            per_group = [_attn_fwd(f"l{l}_att{g}", q, kv, g, bl, hub=hub) for g in range(N_GROUPS)]
            o, lse = _combine(f"l{l}_combine", [p[0] for p in per_group], [p[1] for p in per_group])
            h2 = _mm_res(f"l{l}_att_out", o, W["w_o", i], h)
            saved.append((h, q, o, lse, h2))
        a = _norm_mm(f"l{l}_up", h2, row(norm_mlp, l), W["w_up", l], out_dtype=BF16, hub=hub)
        h = _relu2_mm_res(f"l{l}_down", a, _rows_merged(W["w_down", l]), h2, hub=hub)
        saved[-1] = saved[-1] + (a,)

    dh, sq_err, d_norm_final = _final_loss("loss", h, tgt, norm_final.reshape(1, D))

    d_norm_mix = [None] * DEPTH
    d_norm_mlp = [None] * DEPTH
    d_conv = [None] * N_A_LAYERS
    d_norm_kv = None
    dkv_acc = [None] * N_GROUPS
    G = hub.grads
    as_slots = lambda g: g.reshape(N_DEV, g.shape[0] // N_DEV, g.shape[1])
    tt = DW_TOKENS
    tc = DW_TOKENS // 2
    for l in reversed(range(DEPTH)):
        a, h2 = saved[l][-1], saved[l][-2]
        h_in = saved[l][0]
        g_mlp = row(norm_mlp, l)
        g_mix = row(norm_mix, l)
        w_up_l = W["w_up", l]
        FF = N_DEV * w_up_l.shape[2]
        da = _nt_relu2_bwd(f"l{l}_down_bwd", dh, _rows_merged(W["w_down", l]), a, hub=hub)
        G["w_down", l] = as_slots(_tn(f"l{l}_dw_down", [_rows2(a, tt, FF // 2, lambda s: s)], _relu2,
                                      [_rows2(dh, tt)], _val, FF, D, T, tt, split=("k", 2)))
        G["w_up", l] = _tn(f"l{l}_dw_up", [_rows2(h2, 2 * tt), _full2(g_mlp)], _normed,
                           [_rows2(da, 2 * tt, FF // 4, lambda s: s)], _val, D, FF, T, 2 * tt, split=("n", 4),
                           out_cols=w_up_l.shape[2], hub=hub)
        dh2, d_norm_mlp[l] = _nt_norm_bwd(f"l{l}_up_bwd", [da], w_up_l, h2, g_mlp, dh, hub=hub)
        if l >= N_A_LAYERS:
            i = l - N_A_LAYERS
            _, q, o, lse, _, _ = saved[l]
            w_o_i, w_q_i = W["w_o", i], W["w_q", i]
            do, delta = _att_out_bwd(f"l{l}_att_out_bwd", dh2, w_o_i, o)
            G["w_o", i] = _tn(f"l{l}_dw_o", [_rows2(o, tt)], _val, [_rows2(dh2, tt)], _val, QW, D, T, tt,
                              out_cols=w_o_i.shape[2])
            dqs = []
            for g in range(N_GROUPS):
                dqs.append(_attn_bwd_dq(f"l{l}_att{g}_dq", q, kv, do, delta, lse, g, bl, hub=hub))
                dkv_acc[g] = _attn_bwd_dkv(f"l{l}_att{g}_dkv", q, kv, do, delta, lse, g, bl, prev=dkv_acc[g], hub=hub)
            G["w_q", i] = _tn(f"l{l}_dw_q", [_rows2(h_in, tt), _full2(g_mix)], _normed,
                              [_rows2(t, tt) for t in dqs], _concat_f32, D, N_GROUPS * QW, T, tt, out_cols=w_q_i.shape[2])
            dh, d_norm_mix[l] = _nt_norm_bwd(f"l{l}_q_bwd", dqs, w_q_i, h_in, g_mix, dh2, hub=hub)
            if l == N_A_LAYERS:
                dkvs = [t for pair in dkv_acc for t in pair]
                g_kv = norm_kv.reshape(1, D)
                w_kv = W["w_kv", None]
                G["w_kv", None] = _tn("dw_kv", [_rows2(h_kv, 256), _full2(g_kv)], _normed,
                                      [_rows2(t, 256) for t in dkvs], _concat_f32, D, 2 * N_GROUPS * QW, T, 256,
                                      out_cols=w_kv.shape[2])
                dh, d_norm_kv = _nt_norm_bwd("kv_bwd", dkvs, w_kv, h_kv, g_kv, dh, hub=hub)
        else:
            _, bcu, _, _ = saved[l]
            cw = _pad8(conv_w[l])
            w_in_l = W["w_a_in", l]
            dgated = _nt_plain(f"l{l}_conv_out_bwd", dh2, _rows_merged(W["w_a_out", l]))

            def gated_tile(b_ref, c_ref, u_ref, ch_ref, uh_ref, cw_ref):
                first = (pl.program_id(1) * tc) % seq == 0
                return _gate(b_ref, c_ref, u_ref, ch_ref, uh_ref, cw_ref, first)[0]

            G["w_a_out", l] = as_slots(_tn(
                f"l{l}_dw_conv_out",
                [_rows2(bcu, tc, D, lambda s: 0), _rows2(bcu, tc, D, lambda s: 1), _rows2(bcu, tc, D, lambda s: 2),
                 _prev8_2(bcu, tc, D, 1), _prev8_2(bcu, tc, D, 2), _full2(cw)], gated_tile,
                [_rows2(dh2, tc)], _val, D, D, T, tc, hub=hub))
            dbcu, d_conv[l] = _conv_bwd(f"l{l}_conv_bwd", bcu, dgated, cw, seq, hub=hub)
            G["w_a_in", l] = _tn(f"l{l}_dw_in", [_rows2(h_in, tt), _full2(g_mix)], _normed,
                                 [_rows2(dbcu, tt, 3 * D // 2, lambda s: s)], _val, D, 3 * D, T, tt, split=("n", 2),
                                 out_cols=w_in_l.shape[2], hub=hub)
            dh, d_norm_mix[l] = _nt_norm_bwd(f"l{l}_in_bwd", [dbcu], w_in_l, h_in, g_mix, dh2, hub=hub)

    small = jnp.concatenate(d_norm_mix + d_norm_mlp + [d_norm_kv, d_norm_final] + d_conv, axis=0)
    return sq_err, dh.reshape(bl, seq, D), small


def kernel(x, norm_mix, norm_mlp, w_a_in, conv_w, w_a_out, norm_kv, w_kv, w_q, w_o, w_up, w_down, norm_final, loss_target, m_norm_mix, m_norm_mlp, m_w_a_in, m_conv_w, m_w_a_out, m_norm_kv, m_w_kv, m_w_q, m_w_o, m_w_up, m_w_down, m_norm_final, v_norm_mix, v_norm_mlp, v_w_a_in, v_conv_w, v_w_a_out, v_norm_kv, v_w_kv, v_w_q, v_w_o, v_w_up, v_w_down, v_norm_final):
    D = x.shape[-1]
    xi, yi, ci = _mesh_pos()
    me_idx = 4 * xi + 2 * yi + ci
    w_big = dict(w_a_in=w_a_in, w_a_out=w_a_out, w_kv=w_kv, w_q=w_q, w_o=w_o, w_up=w_up, w_down=w_down)
    m_big = dict(w_a_in=m_w_a_in, w_a_out=m_w_a_out, w_kv=m_w_kv, w_q=m_w_q, w_o=m_w_o, w_up=m_w_up, w_down=m_w_down)
    v_big = dict(w_a_in=v_w_a_in, w_a_out=v_w_a_out, w_kv=v_w_kv, w_q=v_w_q, w_o=v_w_o, w_up=v_w_up, w_down=v_w_down)
    names = list(w_big)

    shards = {n: w.astype(BF16) for n, w in w_big.items()}
    landing = {n: lax.empty((N_DEV,) + w.shape, BF16) for n, w in w_big.items()}
    hub = _Hub(FETCH_DURING, PUSH_DURING, shards, landing)
    dc = conv_w.shape[-1]
    taps = conv_w.shape[0] * conv_w.shape[1]
    got = _all_gather("gather_first", [(shards[n], l) for n, l in FETCH_UP_FRONT] + [(_pad8(conv_w.reshape(taps, dc)), None)])
    for key, w in zip(FETCH_UP_FRONT, got):
        hub.weights[key] = w
    conv_full = jnp.moveaxis(got[-1][:, :taps], 0, 1).reshape(conv_w.shape[0], conv_w.shape[1], N_DEV * dc)

    sq_err, grad_x, small = _local_grads(x, loss_target, norm_mix, norm_mlp, norm_kv, norm_final, conv_full, hub)
    loss = lax.psum(sq_err * (0.5 / D), ("x", "y", "c"))

    grads, deltas, new_m, new_v = {}, {}, {}, {}
    for n in names:
        shape = w_big[n].shape
        cols = shape[-1]
        flat = lambda t: t.reshape(-1, cols)
        parts = hub.landing[n].reshape(N_DEV, -1, cols)
        outs = _sum8_adamw(f"adamw_{n}", parts, flat(w_big[n]), flat(m_big[n]), flat(v_big[n]), tr=min(256, parts.shape[1]))
        grads[n], deltas[n], new_m[n], new_v[n] = (t.reshape(shape) for t in outs)

    n_gain = 2 * DEPTH + 2
    rows_small = small.shape[0]
    small_all = _all_gather("gather_small_grads", [(small, None)])[0]

    def small_pack(nm, nl, nk, nf, cw):
        gains = jnp.concatenate([nm, nl, nk.reshape(1, D), nf.reshape(1, D)], axis=0)
        taps_full = lax.dynamic_update_slice(jnp.zeros((taps, D), F32), cw.reshape(taps, dc), (0, me_idx * dc))
        return jnp.concatenate([gains, taps_full], axis=0)

    sp = [small_pack(*t) for t in ((norm_mix, norm_mlp, norm_kv, norm_final, conv_w),
                                   (m_norm_mix, m_norm_mlp, m_norm_kv, m_norm_final, m_conv_w),
                                   (v_norm_mix, v_norm_mlp, v_norm_kv, v_norm_final, v_conv_w))]
    small_out = _sum8_adamw("adamw_small", small_all, *sp, tr=rows_small)

    def small_unpack(t):
        res = dict(norm_mix=t[0:DEPTH], norm_mlp=t[DEPTH:2 * DEPTH], norm_kv=t[2 * DEPTH], norm_final=t[2 * DEPTH + 1])
        res["conv_w"] = lax.dynamic_slice(t[n_gain:], (0, me_idx * dc), (taps, dc)).reshape(conv_w.shape)
        return res

    for dst, t in zip((grads, deltas, new_m, new_v), small_out):
        dst.update(small_unpack(t))

    order = ["norm_mix", "norm_mlp", "w_a_in", "conv_w", "w_a_out", "norm_kv", "w_kv", "w_q", "w_o", "w_up", "w_down",
             "norm_final"]
    return (loss, grad_x, *[grads[n] for n in order], *[deltas[n] for n in order], *[new_m[n] for n in order],
            *[new_v[n] for n in order])
```

```python
import functools

import jax
import jax.numpy as jnp
from jax import lax
from jax.experimental import pallas as pl
from jax.experimental.pallas import tpu as pltpu

F32 = jnp.float32
BF16 = jnp.bfloat16
SDS = jax.ShapeDtypeStruct

EPS = 1e-5
N_A_LAYERS = 2
DEPTH = 4
PATTERNS = ((128, 1), (512, 4), (2048, 16))
N_GROUPS = 3
H_G = 8
HEAD_DIM = 64
QW = H_G * HEAD_DIM
ATT_BLK = 128
ALIBI_MAX_BIAS = 8.0
NEG_INF = -1e30

ADAM_LR = 0.001
ADAM_B1 = 0.9
ADAM_B2 = 0.999
ADAM_EPS = 1e-08
ADAM_WD = 0.01
ADAM_STEP = 10

N_DEV = 8
SUBLANES = 8
HALO = 16
V7X_VMEM_LIMIT = 48 * 1024 * 1024
MXU_COLS = 256
MM_CHUNK = 512
MM_ROWS = 512
DW_TOKENS = 1024

FETCH_UP_FRONT = [("w_a_in", 0), ("w_a_out", 0)]
FETCH_DURING = {
    "l0_in": [("w_up", 0)], "l0_conv_out": [("w_down", 0)], "l0_up": [("w_a_in", 1), ("w_a_out", 1)], "l0_down": [("w_up", 1)],
    "l1_in": [("w_down", 1)], "l1_conv_out": [("w_kv", None)],
    "l1_up": [("w_q", 0), ("w_o", 0), ("w_q", 1), ("w_o", 1)], "l1_down": [("w_up", 2)],
    "kv": [("w_down", 2)], "l2_att0": [("w_up", 3)], "l2_att1": [("w_down", 3)],
}
PUSH_DURING = {
    "l3_dw_up": [("w_down", 3, 0, 2)], "l3_up_bwd": [("w_down", 3, 1, 2)],
    "l3_att0_dq": [("w_up", 3, 0, 2)], "l3_att0_dkv": [("w_up", 3, 1, 2)], "l3_att1_dq": [("w_o", 1)], "l3_q_bwd": [("w_q", 1)],
    "l2_dw_up": [("w_down", 2, 0, 2)], "l2_up_bwd": [("w_down", 2, 1, 2)],
    "l2_att0_dq": [("w_up", 2, 0, 2)], "l2_att0_dkv": [("w_up", 2, 1, 2)], "l2_att1_dq": [("w_o", 0)], "l2_q_bwd": [("w_q", 0)],
    "kv_bwd": [("w_kv", None, 0, 2)], "l1_down_bwd": [("w_kv", None, 1, 2)],
    "l1_dw_up": [("w_down", 1, 0, 2)], "l1_up_bwd": [("w_down", 1, 1, 2)], "l1_conv_bwd": [("w_up", 1, 0, 2)],
    "l1_dw_in": [("w_up", 1, 1, 2), ("w_a_out", 1)], "l1_in_bwd": [("w_a_in", 1, 0, 2)], "l0_down_bwd": [("w_a_in", 1, 1, 2)],
    "l0_dw_up": [("w_down", 0, 0, 2)], "l0_up_bwd": [("w_down", 0, 1, 2)], "l0_conv_bwd": [("w_up", 0, 0, 2)],
    "l0_dw_in": [("w_up", 0, 1, 2), ("w_a_out", 0)], "l0_in_bwd": [("w_a_in", 0)],
}


def _mesh_pos():
    return lax.axis_index("x"), lax.axis_index("y"), lax.axis_index("c")


def _flip(v, bit):
    return 1 - v if bit else v


class _Transfer:
    def __init__(self, kind, key, src, src_idx=None, dst=None, dst_idx=None, dst_shape=None, rows=None):
        self.kind, self.key, self.src, self.src_idx = kind, key, src, src_idx
        self.dst, self.dst_idx, self.dst_shape, self.rows = dst, dst_idx, dst_shape, rows

    def copies(self, src_ref, dst_ref, send_sems, recv_sems, local_sem):
        x, y, c = _mesh_pos()
        me = 4 * x + 2 * y + c
        part = (lambda r: r) if self.rows is None else (lambda r: r.at[pl.ds(*self.rows)])

        def dst_slot(j):
            r = dst_ref.at[j]
            return part(r if self.dst_idx is None else r.at[self.dst_idx])

        def copy(k, src, dst_j, to):
            return pltpu.make_async_remote_copy(
                src_ref=src, dst_ref=dst_slot(dst_j), send_sem=send_sems.at[k], recv_sem=recv_sems.at[k],
                device_id=to, device_id_type=pl.DeviceIdType.MESH)

        if self.kind == "exchange":
            local = pltpu.make_async_copy(part(src_ref.at[me]), dst_slot(me), local_sem)
            sends, arrivals = [], []
            for k in range(1, N_DEV):
                peer = (_flip(x, k & 4), _flip(y, k & 2), _flip(c, k & 1))
                peer_idx = 4 * peer[0] + 2 * peer[1] + peer[2]
                sends.append(copy(k - 1, part(src_ref.at[peer_idx]), me, peer))
                arrivals.append(copy(k - 1, part(src_ref.at[peer_idx]), peer_idx, peer))
            return local, sends, [], arrivals

        own = part(src_ref if self.src_idx is None else src_ref.at[self.src_idx])
        idx = lambda px, py, pc: 4 * px + 2 * py + pc
        sibling = (x, y, 1 - c)
        chips = [(1 - x, y), (x, 1 - y), (1 - x, 1 - y)]
        local = pltpu.make_async_copy(own, dst_slot(me), local_sem)
        sends = [copy(0, own, me, sibling)] + [copy(1 + j, own, me, (*chip, c)) for j, chip in enumerate(chips)]
        relays = [(copy(1 + j, own, idx(*chip, c), sibling), copy(4 + j, dst_slot(idx(*chip, c)), idx(*chip, c), sibling))
                  for j, chip in enumerate(chips)]
        arrivals = [copy(0, own, idx(*sibling), sibling)]
        arrivals += [copy(4 + j, own, idx(*chip, 1 - c), sibling) for j, chip in enumerate(chips)]
        return local, sends, relays, arrivals


class _Hub:
    def __init__(self, fetch, push, shards, landing):
        self.fetch, self.push, self.shards, self.landing = fetch, push, shards, landing
        self.weights = {}
        self.arriving = {}
        self.grads = {}

    def transfers(self, host):
        out = []
        for name, l, *part in self.fetch.get(host, ()):
            src = self.shards[name]
            shard = tuple(src.shape if l is None else src.shape[1:])
            p, n = part or (0, 1)
            rows = None if n == 1 else (p * (shard[0] // n), shard[0] // n)
            out.append(_Transfer("gather", (name, l, p == n - 1), src, src_idx=l, dst=self.arriving.get((name, l)),
                                 dst_shape=(N_DEV,) + shard, rows=rows))
        for name, l, *part in self.push.get(host, ()):
            src = self.grads[name, l]
            p, n = part or (0, 1)
            rows = None if n == 1 else (p * (src.shape[1] // n), src.shape[1] // n)
            out.append(_Transfer("exchange", (name, l, p == n - 1), src, dst=self.landing[name], dst_idx=l, rows=rows))
        return out

    def accept(self, transfers, results):
        for t, r in zip(transfers, results):
            name, l, complete = t.key
            if t.kind == "exchange":
                self.landing[name] = r
            elif complete:
                self.weights[name, l] = r
            else:
                self.arriving[name, l] = r


def _call(name, body, grid, ins, outs, scratch=(), hub=None):
    transfers = hub.transfers(name) if hub is not None else []
    n_in, n_out, n_scr, n_tr = len(ins), len(outs), len(scratch), len(transfers)
    c_in, c_out, aliases, places = [], [], {}, []
    for t in transfers:
        c_in.append(t.src)
        src_pos = len(c_in) - 1
        if t.dst is not None:
            c_in.append(t.dst)
            aliases[n_in + len(c_in) - 1] = n_out + len(c_out)
            c_out.append(SDS(t.dst.shape, t.dst.dtype))
        else:
            c_out.append(SDS(t.dst_shape, t.src.dtype))
        places.append((src_pos, len(c_out) - 1))
    sems = [pltpu.SemaphoreType.DMA((n_tr, N_DEV - 1)), pltpu.SemaphoreType.DMA((n_tr, N_DEV - 1)),
            pltpu.SemaphoreType.DMA((n_tr,))] if n_tr else []

    def wrapped(*refs):
        in_refs = refs[:n_in]
        cin_refs = refs[n_in:n_in + len(c_in)]
        o0 = n_in + len(c_in)
        out_refs = refs[o0:o0 + n_out]
        cout_refs = refs[o0 + n_out:o0 + n_out + len(c_out)]
        s0 = o0 + n_out + len(c_out)
        scr_refs = refs[s0:s0 + n_scr]
        if n_tr:
            send_sems, recv_sems, local_sems = refs[s0 + n_scr:]
            first = last = relay = None
            for ax, n in enumerate(grid):
                i = pl.program_id(ax)
                at_relay = (i == max(n - 2, 0)) if ax == len(grid) - 1 else (i == n - 1)
                first = (i == 0) if first is None else first & (i == 0)
                last = (i == n - 1) if last is None else last & (i == n - 1)
                relay = at_relay if relay is None else relay & at_relay

            def all_copies():
                return [t.copies(cin_refs[sp], cout_refs[dp], send_sems.at[n], recv_sems.at[n], local_sems.at[n])
                        for n, (t, (sp, dp)) in enumerate(zip(transfers, places))]

            @pl.when(first)
            def _():
                for local, sends, _, _ in all_copies():
                    local.start()
                    for cp in sends:
                        cp.start()

            def pass_on():
                @pl.when(relay)
                def _():
                    for _, _, relays, _ in all_copies():
                        for arrival, onward in relays:
                            arrival.wait_recv()
                            onward.start()

            if grid[-1] > 1:
                pass_on()

        body(*in_refs, *out_refs, *scr_refs)

        if n_tr:
            if grid[-1] == 1:
                pass_on()

            @pl.when(last)
            def _():
                for local, sends, relays, arrivals in all_copies():
                    for cp in arrivals:
                        cp.wait_recv()
                    for cp in sends + [onward for _, onward in relays]:
                        cp.wait_send()
                    local.wait()

    any_spec = pl.BlockSpec(memory_space=pl.ANY)
    res = pl.pallas_call(
        wrapped,
        name=name,
        grid=grid,
        in_specs=[s for _, s in ins] + [any_spec] * len(c_in),
        out_specs=[s for _, s in outs] + [any_spec] * len(c_out),
        out_shape=[o for o, _ in outs] + c_out,
        scratch_shapes=list(scratch) + sems,
        input_output_aliases=aliases,
        compiler_params=pltpu.CompilerParams(
            dimension_semantics=("arbitrary",) * len(grid), vmem_limit_bytes=V7X_VMEM_LIMIT),
    )(*[a for a, _ in ins], *c_in)
    if n_tr:
        hub.accept(transfers, res[n_out:])
    return res[:n_out]


def _rows(a, tm, cb=None, col=0):
    cb = cb or a.shape[1]
    return (a, pl.BlockSpec((tm, cb), lambda i: (i, col)))


def _full(a):
    nd = a.ndim
    return (a, pl.BlockSpec(a.shape, lambda i: (0,) * nd))


def _prev8(a, tm, cb, col):
    return (a, pl.BlockSpec((HALO, cb), lambda i: (jnp.maximum(i * (tm // HALO) - 1, 0), col)))


def _next8(a, tm, cb, col):
    last = a.shape[0] // HALO - 1
    return (a, pl.BlockSpec((HALO, cb), lambda i: (jnp.minimum((i + 1) * (tm // HALO), last), col)))


def _rows2(a, tt, cb=None, colfn=None):
    cb = cb or a.shape[1]
    colfn = colfn or (lambda s: 0)
    return (a, pl.BlockSpec((tt, cb), lambda s, t: (t, colfn(s))))


def _full2(a):
    nd = a.ndim
    return (a, pl.BlockSpec(a.shape, lambda s, t: (0,) * nd))


def _prev8_2(a, tt, cb, col):
    return (a, pl.BlockSpec((HALO, cb), lambda s, t: (jnp.maximum(t * (tt // HALO) - 1, 0), col)))


def _out_rows(T, n, dtype, tm):
    return (SDS((T, n), dtype), pl.BlockSpec((tm, n), lambda i: (i, 0)))


def _out_acc8(d):
    return (SDS((SUBLANES, d), F32), pl.BlockSpec((SUBLANES, d), lambda i: (0, 0)))


def _rstd(x):
    return lax.rsqrt(jnp.mean(x * x, axis=-1, keepdims=True) + EPS)


def _normed(h_ref, g_ref):
    x = h_ref[...]
    return x * _rstd(x) * g_ref[...]


def _acc8(ref, val, i, n):
    part = val.reshape(-1, SUBLANES, val.shape[-1]).sum(axis=0)

    @pl.when(i == 0)
    def _():
        ref[...] = part

    @pl.when(i > 0)
    def _():
        ref[...] += part

    @pl.when(i == n - 1)
    def _():
        ref[...] = jnp.broadcast_to(jnp.sum(ref[...], axis=0, keepdims=True), ref.shape)


def _gate(b_ref, c_ref, u_ref, ch_ref, uh_ref, cw_ref, first):
    b, c, u = (r[...].astype(F32) for r in (b_ref, c_ref, u_ref))
    cu = c * u
    halo = jnp.where(first, 0.0, ch_ref[...].astype(F32) * uh_ref[...].astype(F32))
    rows = lax.broadcasted_iota(jnp.int32, cu.shape, 0)
    h1 = halo[HALO - 1:HALO, :]
    h2 = halo[HALO - 2:HALO - 1, :]
    cu1 = jnp.where(rows == 0, h1, pltpu.roll(cu, 1, 0))
    cu2 = jnp.where(rows == 0, h2, jnp.where(rows == 1, h1, pltpu.roll(cu, 2, 0)))
    conv = cw_ref[0:1, :] * cu + cw_ref[1:2, :] * cu1 + cw_ref[2:3, :] * cu2
    return b * conv, (b, c, u), conv, (cu, cu1, cu2)


def _relu2(a_ref):
    r = jnp.maximum(a_ref[...].astype(F32), 0.0)
    return r * r


def _dot(a, b):
    return jnp.dot(a, b, preferred_element_type=F32)


def _dot_nt(a, b):
    return lax.dot_general(a, b, (((1,), (1,)), ((), ())), preferred_element_type=F32)


def _dot_tn(a, b):
    return lax.dot_general(a, b, (((0,), (0,)), ((), ())), preferred_element_type=F32)


def _chunks(n):
    c = min(MM_CHUNK, n)
    while n % c:
        c -= 128
    assert c > 0, n
    return [(k * c, (k + 1) * c) for k in range(n // c)]


def _col_weight(w):
    _, K, ns = w.shape
    N = N_DEV * ns
    direct = ns % MXU_COLS == 0
    scratch = [] if direct else [pltpu.VMEM((K, N), BF16)]

    def prepare(w_ref, s_ref, step):
        if direct:
            return

        @pl.when(step == 0)
        def _():
            for j in range(N_DEV):
                s_ref[:, j * ns:(j + 1) * ns] = w_ref[j]

    def chunks(w_ref, s_ref):
        if direct:
            return [(j * ns, (j + 1) * ns, (lambda j=j: w_ref[j])) for j in range(N_DEV)]
        return [(lo, hi, (lambda lo=lo, hi=hi: s_ref[:, lo:hi])) for lo, hi in _chunks(N)]

    return N, scratch, prepare, chunks


def _norm_mm(name, h, g, w, tm=MM_ROWS, out_dtype=F32, hub=None):
    T, _ = h.shape
    N, w_scratch, prepare, chunks = _col_weight(w)

    def body(h_ref, g_ref, w_ref, o_ref, *s):
        s_ref = s[0] if s else None
        prepare(w_ref, s_ref, pl.program_id(0))
        a = _normed(h_ref, g_ref).astype(BF16)
        for lo, hi, load in chunks(w_ref, s_ref):
            o_ref[:, lo:hi] = _dot(a, load()).astype(out_dtype)

    return _call(name, body, (T // tm,), [_rows(h, tm), _full(g), _full(w)], [_out_rows(T, N, out_dtype, tm)],
                 scratch=w_scratch, hub=hub)[0]


def _gate_mm_res(name, bcu, cw, w, h, seq, tm=MM_ROWS, hub=None):
    T, D = h.shape

    def body(b_ref, c_ref, u_ref, ch_ref, uh_ref, cw_ref, w_ref, h_ref, o_ref):
        first = (pl.program_id(0) * tm) % seq == 0
        gated = _gate(b_ref, c_ref, u_ref, ch_ref, uh_ref, cw_ref, first)[0].astype(BF16)
        for lo, hi in _chunks(D):
            o_ref[:, lo:hi] = h_ref[:, lo:hi] + _dot(gated, w_ref[:, lo:hi])

    ins = [_rows(bcu, tm, D, 0), _rows(bcu, tm, D, 1), _rows(bcu, tm, D, 2), _prev8(bcu, tm, D, 1),
           _prev8(bcu, tm, D, 2), _full(cw), _full(w), _rows(h, tm)]
    return _call(name, body, (T // tm,), ins, [_out_rows(T, D, F32, tm)], hub=hub)[0]


def _relu2_mm_res(name, a, w, h, tm=MM_ROWS, hub=None):
    T, D = h.shape
    K = a.shape[1]

    def body(a_ref, w_ref, h_ref, o_ref, acc_ref):
        for n, (lo, hi) in enumerate(_chunks(K)):
            d = _dot(_relu2(a_ref.at[:, lo:hi]).astype(BF16), w_ref[lo:hi, :])
            if n == 0:
                acc_ref[...] = d
            else:
                acc_ref[...] += d
        o_ref[...] = h_ref[...] + acc_ref[...]

    return _call(name, body, (T // tm,), [_rows(a, tm), _full(w), _rows(h, tm)], [_out_rows(T, D, F32, tm)],
                 scratch=[pltpu.VMEM((tm, D), F32)], hub=hub)[0]


def _mm_res(name, a, w, h, tm=MM_ROWS):
    T, D = h.shape
    _, w_scratch, prepare, chunks = _col_weight(w)

    def body(a_ref, w_ref, h_ref, o_ref, *s):
        s_ref = s[0] if s else None
        prepare(w_ref, s_ref, pl.program_id(0))
        av = a_ref[...].astype(BF16)
        for lo, hi, load in chunks(w_ref, s_ref):
            o_ref[:, lo:hi] = h_ref[:, lo:hi] + _dot(av, load())

    return _call(name, body, (T // tm,), [_rows(a, tm), _full(w), _rows(h, tm)], [_out_rows(T, D, F32, tm)],
                 scratch=w_scratch)[0]


def _nt_relu2_bwd(name, dh, w, a, tm=MM_ROWS, hub=None):
    T, _ = dh.shape
    K = w.shape[0]

    def body(dh_ref, w_ref, a_ref, o_ref):
        d = dh_ref[...].astype(BF16)
        for lo, hi in _chunks(K):
            dr = _dot_nt(d, w_ref[lo:hi, :])
            o_ref[:, lo:hi] = (dr * (2.0 * jnp.maximum(a_ref[:, lo:hi].astype(F32), 0.0))).astype(BF16)

    return _call(name, body, (T // tm,), [_rows(dh, tm), _full(w), _rows(a, tm)], [_out_rows(T, K, BF16, tm)], hub=hub)[0]


def _concat_bf16(*refs):
    vals = [r[...].astype(BF16) for r in refs]
    return vals[0] if len(vals) == 1 else jnp.concatenate(vals, axis=1)


def _nt_plain(name, dy, w, tm=MM_ROWS):
    T, N = dy.shape
    if w.ndim == 3:
        K = w.shape[1]
        _, w_scratch, prepare, chunks = _col_weight(w)
    else:
        K = w.shape[0]
        w_scratch, prepare = [], (lambda w_ref, s_ref, step: None)
        chunks = lambda w_ref, s_ref: [(lo, hi, (lambda lo=lo, hi=hi: w_ref[:, lo:hi])) for lo, hi in _chunks(N)]

    def body(dy_ref, w_ref, o_ref, acc_ref, *s):
        s_ref = s[0] if s else None
        prepare(w_ref, s_ref, pl.program_id(0))
        for n, (lo, hi, load) in enumerate(chunks(w_ref, s_ref)):
            d = _dot_nt(dy_ref[:, lo:hi].astype(BF16), load())
            if n == 0:
                acc_ref[...] = d
            else:
                acc_ref[...] += d
        o_ref[...] = acc_ref[...]

    return _call(name, body, (T // tm,), [_rows(dy, tm), _full(w)], [_out_rows(T, K, F32, tm)],
                 scratch=[pltpu.VMEM((tm, K), F32)] + w_scratch)[0]


def _att_out_bwd(name, dy, w, o, tm=MM_ROWS):
    T, _ = dy.shape
    K = w.shape[1]
    _, w_scratch, prepare, chunks = _col_weight(w)

    def body(dy_ref, w_ref, o_ref, do_ref, dl_ref, acc_ref, *s):
        s_ref = s[0] if s else None
        prepare(w_ref, s_ref, pl.program_id(0))
        for n, (lo, hi, load) in enumerate(chunks(w_ref, s_ref)):
            d = _dot_nt(dy_ref[:, lo:hi].astype(BF16), load())
            if n == 0:
                acc_ref[...] = d
            else:
                acc_ref[...] += d
        do = acc_ref[...]
        do_ref[...] = do
        prod = do * o_ref[...]
        high = prod.astype(BF16)
        low = (prod - high.astype(F32)).astype(BF16)
        head_of = lambda axis: jnp.right_shift(lax.broadcasted_iota(jnp.int32, (K, K), axis), HEAD_DIM.bit_length() - 1)
        same_head = jnp.where(head_of(0) == head_of(1), 1.0, 0.0).astype(BF16)
        dl_ref[...] = _dot(high, same_head) + _dot(low, same_head)

    outs = [_out_rows(T, K, F32, tm), _out_rows(T, K, F32, tm)]
    return _call(name, body, (T // tm,), [_rows(dy, tm), _full(w), _rows(o, tm)], outs,
                 scratch=[pltpu.VMEM((tm, K), F32)] + w_scratch)


def _nt_norm_bwd(name, dys, w, h, g, dh_in, tm=MM_ROWS, hub=None):
    T, D = h.shape
    _, w_scratch, prepare, chunks = _col_weight(w)
    n_steps = T // tm
    n_dy = len(dys)

    def body(*refs):
        dy_refs = refs[:n_dy]
        w_ref, h_ref, g_ref, dhin_ref, o_ref, dg_ref, acc_ref = refs[n_dy:n_dy + 7]
        s_ref = refs[n_dy + 7] if len(refs) > n_dy + 7 else None
        i = pl.program_id(0)
        prepare(w_ref, s_ref, i)
        dy = _concat_bf16(*dy_refs)
        for n, (lo, hi, load) in enumerate(chunks(w_ref, s_ref)):
            d = _dot_nt(dy[:, lo:hi], load())
            if n == 0:
                acc_ref[...] = d
            else:
                acc_ref[...] += d
        dn = acc_ref[...]
        x = h_ref[...]
        rstd = _rstd(x)
        xhat = x * rstd
        dxhat = dn * g_ref[...]
        dx = rstd * (dxhat - xhat * jnp.mean(dxhat * xhat, axis=-1, keepdims=True))
        o_ref[...] = dhin_ref[...] + dx
        _acc8(dg_ref, dn * xhat, i, n_steps)

    ins = [_rows(d, tm) for d in dys] + [_full(w), _rows(h, tm), _full(g), _rows(dh_in, tm)]
    outs = [_out_rows(T, D, F32, tm), _out_acc8(D)]
    dh, dg = _call(name, body, (n_steps,), ins, outs, scratch=[pltpu.VMEM((tm, D), F32)] + w_scratch, hub=hub)
    return dh, dg[0:1]


def _tn(name, a_ins, a_fn, y_ins, y_fn, K, N, T, tt, split=None, out_cols=None, hub=None):
    kind, parts = split or ("n", 1)
    kb, nb = (K // parts, N) if kind == "k" else (K, N // parts)
    n_steps = T // tt
    n_a = len(a_ins)
    n_y = len(y_ins)
    assert out_cols is None or (kind == "n" and nb % out_cols == 0)

    def body(*refs):
        a_refs = refs[:n_a]
        y_refs = refs[n_a:n_a + n_y]
        o_ref, acc_ref = refs[n_a + n_y:]
        t = pl.program_id(1)
        a_t = a_fn(*a_refs).T.astype(BF16)
        y = y_fn(*y_refs).astype(BF16)
        for lo, hi in _chunks(nb):
            d = _dot(a_t, y[:, lo:hi])

            @pl.when(t == 0)
            def _():
                acc_ref[:, lo:hi] = d

            @pl.when(t > 0)
            def _():
                acc_ref[:, lo:hi] += d

        @pl.when(t == n_steps - 1)
        def _():
            if out_cols is None:
                o_ref[...] = acc_ref[...].astype(BF16)
            else:
                for j in range(nb // out_cols):
                    o_ref[j] = acc_ref[:, j * out_cols:(j + 1) * out_cols].astype(BF16)

    if out_cols is None:
        out = (SDS((K, N), BF16), pl.BlockSpec((kb, nb), (lambda s, t: (s, 0)) if kind == "k" else (lambda s, t: (0, s))))
    else:
        out = (SDS((N // out_cols, K, out_cols), BF16), pl.BlockSpec((nb // out_cols, K, out_cols), lambda s, t: (s, 0, 0)))
    return _call(name, body, (parts, n_steps), list(a_ins) + list(y_ins), [out],
                 scratch=[pltpu.VMEM((kb, nb), F32)], hub=hub)[0]


def _val(ref):
    return ref[...]


def _concat_f32(*refs):
    vals = [r[...] for r in refs]
    return vals[0] if len(vals) == 1 else jnp.concatenate(vals, axis=1)


ATT_TILE_ROWS = 2048
HEAD_PAIRS = H_G // 2
ATT_SCALE = HEAD_DIM ** -0.5
ATT_UNITS_TOGETHER = 4


def _slope(h):
    return 2.0 ** (-ALIBI_MAX_BIAS * (h + 1) / H_G)


def _att_geom(T, bl, g):
    dil = PATTERNS[g][1]
    sub = ATT_BLK * dil
    nsub = max(1, ATT_TILE_ROWS // sub)
    rows = sub * nsub
    return dil, sub, nsub, rows, T // bl // rows


def _att_specs(T, bl, g):
    _, sub, nsub, rows, nt = _att_geom(T, bl, g)
    last_sub = T // sub - 1
    tile = lambda col: pl.BlockSpec((rows, 128), lambda b, i, hp: (b * nt + i, col(hp)))
    prev = lambda col: pl.BlockSpec((sub, 128), lambda b, i, hp: (jnp.maximum((b * nt + i) * nsub - 1, 0), col(hp)))
    nxt = lambda col: pl.BlockSpec((sub, 128), lambda b, i, hp: (jnp.minimum((b * nt + i + 1) * nsub, last_sub), col(hp)))
    return tile, prev, nxt


def _sub_rows(j, r, dil):
    start = j * ATT_BLK * dil + r
    return pl.ds(start, ATT_BLK, stride=dil) if dil > 1 else pl.ds(start, ATT_BLK)


class _Residues:
    def __init__(self, dil):
        self.dil = dil
        self.whole = dil % SUBLANES == 0
        self.read, self.written = {}, {}

    def _block(self, j):
        return pl.ds(j * ATT_BLK * self.dil, ATT_BLK * self.dil)

    def load(self, ref, j, r):
        if not self.whole:
            return ref[_sub_rows(j, r, self.dil), :]
        if (id(ref), j) not in self.read:
            rows = ref[self._block(j), :]
            self.read[id(ref), j] = jnp.swapaxes(rows.reshape(ATT_BLK, self.dil, rows.shape[-1]), 0, 1)
        return self.read[id(ref), j][r]

    def store(self, ref, j, r, val):
        if not self.whole:
            ref[_sub_rows(j, r, self.dil), :] = val
            return
        got = self.written.setdefault((id(ref), j), {})
        got[r] = val
        if len(got) == self.dil:
            merged = jnp.swapaxes(jnp.stack([got[k] for k in range(self.dil)], axis=0), 0, 1)
            ref[self._block(j), :] = merged.reshape(ATT_BLK * self.dil, val.shape[-1])
            del self.written[id(ref), j]


def _att_consts(hp, dil):
    h0 = lax.broadcasted_iota(jnp.int32, (ATT_BLK, 128), 1) < HEAD_DIM
    a = lax.broadcasted_iota(jnp.int32, (ATT_BLK, ATT_BLK), 0)
    c = lax.broadcasted_iota(jnp.int32, (ATT_BLK, ATT_BLK), 1)
    dist_p = ((ATT_BLK + a - c) * dil).astype(F32)
    dist_c = ((a - c) * dil).astype(F32)
    bias_p, bias_c = [], []
    for h in range(2):
        slope = jnp.float32(_slope(2 * (HEAD_PAIRS - 1) + h))
        for p in range(HEAD_PAIRS - 2, -1, -1):
            slope = jnp.where(hp == p, jnp.float32(_slope(2 * p + h)), slope)
        bias_p.append(jnp.where(c >= a, -slope * dist_p, NEG_INF))
        bias_c.append(jnp.where(c <= a, -slope * dist_c, NEG_INF))
    return h0, bias_p, bias_c


def _split_heads(x, h0):
    return [jnp.where(h0, x, 0.0).astype(BF16), jnp.where(h0, 0.0, x).astype(BF16)]


def _head_cols(x):
    return [x[:, 0:1], x[:, HEAD_DIM:HEAD_DIM + 1]]


def _in_groups(units, first_stage, *later_stages):
    for u0 in range(0, len(units), ATT_UNITS_TOGETHER):
        staged = [first_stage(*u) for u in units[u0:u0 + ATT_UNITS_TOGETHER]]
        for stage in later_stages:
            staged = [stage(*s) for s in staged]


def _attn_fwd(name, q, kv, g, bl, hub=None):
    T = q.shape[0]
    dil, _, nsub, _, _ = _att_geom(T, bl, g)
    tile, prev, _ = _att_specs(T, bl, g)

    def body(q_ref, kp_ref, kc_ref, vp_ref, vc_ref, o_ref, lse_ref):
        first = pl.program_id(1) == 0
        h0, bias_p, bias_c = _att_consts(pl.program_id(2), dil)
        bias_first = [jnp.where(first, NEG_INF, b) for b in bias_p]
        ones = jnp.ones((ATT_BLK, 128), BF16)
        rows = _Residues(dil)

        def scores(j, r):
            if j == 0:
                kp, vp, bp = rows.load(kp_ref, 0, r), rows.load(vp_ref, 0, r), bias_first
            else:
                kp, vp, bp = rows.load(kc_ref, j - 1, r), rows.load(vc_ref, j - 1, r), bias_p
            kp, kc = kp.astype(BF16), rows.load(kc_ref, j, r).astype(BF16)
            qh = _split_heads(rows.load(q_ref, j, r) * ATT_SCALE, h0)
            sp = [_dot_nt(qh[h], kp) + bp[h] for h in range(2)]
            sc = [_dot_nt(qh[h], kc) + bias_c[h] for h in range(2)]
            return (j, r), sp, sc, vp.astype(BF16), rows.load(vc_ref, j, r).astype(BF16)

        def weights(unit, sp, sc, vp, vc):
            mx = [jnp.max(jnp.maximum(sp[h], sc[h]), axis=-1, keepdims=True) for h in range(2)]
            ep = [jnp.exp(sp[h] - mx[h]).astype(BF16) for h in range(2)]
            ec = [jnp.exp(sc[h] - mx[h]).astype(BF16) for h in range(2)]
            return unit, mx, ep, ec, vp, vc

        def outputs(unit, mx, ep, ec, vp, vc):
            den = [_dot(ep[h], ones) + _dot(ec[h], ones) for h in range(2)]
            acc = [_dot(ep[h], vp) + _dot(ec[h], vc) for h in range(2)]
            rows.store(o_ref, *unit, jnp.where(h0, acc[0] / den[0], acc[1] / den[1]))
            rows.store(lse_ref, *unit, jnp.where(h0, mx[0] + jnp.log(den[0]), mx[1] + jnp.log(den[1])))
            return ()

        _in_groups([(j, r) for j in range(nsub) for r in range(dil)], scores, weights, outputs)

    ins = [(q, tile(lambda hp: 4 * g + hp)), (kv, prev(lambda hp: 8 * g + hp)), (kv, tile(lambda hp: 8 * g + hp)),
           (kv, prev(lambda hp: 8 * g + 4 + hp)), (kv, tile(lambda hp: 8 * g + 4 + hp))]
    out = (SDS((T, QW), F32), tile(lambda hp: hp))
    _, _, _, _, nt = _att_geom(T, bl, g)
    return _call(name, body, (bl, nt, HEAD_PAIRS), ins, [out, out], hub=hub)


def _combine(name, os_, lses, tm=512):
    T = os_[0].shape[0]

    def body(o0, o1, o2, l0, l1, l2, o_ref, lse_ref):
        ls = [l0[...], l1[...], l2[...]]
        mx = jnp.maximum(jnp.maximum(ls[0], ls[1]), ls[2])
        es = [jnp.exp(l - mx) for l in ls]
        den = es[0] + es[1] + es[2]
        o_ref[...] = (es[0] * o0[...] + es[1] * o1[...] + es[2] * o2[...]) / den
        lse_ref[...] = mx + jnp.log(den)

    ins = [_rows(t, tm) for t in list(os_) + list(lses)]
    return _call(name, body, (T // tm,), ins, [_out_rows(T, QW, F32, tm), _out_rows(T, QW, F32, tm)])


def _attn_bwd_dq(name, q, kv, do, delta, lse, g, bl, hub=None):
    T = q.shape[0]
    dil, _, nsub, _, nt = _att_geom(T, bl, g)
    tile, prev, _ = _att_specs(T, bl, g)

    def body(q_ref, kp_ref, kc_ref, vp_ref, vc_ref, do_ref, dl_ref, lse_ref, dq_ref):
        first = pl.program_id(1) == 0
        h0, bias_p, bias_c = _att_consts(pl.program_id(2), dil)
        bias_first = [jnp.where(first, NEG_INF, b) for b in bias_p]

        rows = _Residues(dil)

        def probs(j, r):
            if j == 0:
                kp, vp, bp = rows.load(kp_ref, 0, r), rows.load(vp_ref, 0, r), bias_first
            else:
                kp, vp, bp = rows.load(kc_ref, j - 1, r), rows.load(vc_ref, j - 1, r), bias_p
            kp, vp = kp.astype(BF16), vp.astype(BF16)
            kc, vc = rows.load(kc_ref, j, r).astype(BF16), rows.load(vc_ref, j, r).astype(BF16)
            qh = _split_heads(rows.load(q_ref, j, r) * ATT_SCALE, h0)
            dob = _split_heads(rows.load(do_ref, j, r), h0)
            lse_h = _head_cols(rows.load(lse_ref, j, r))
            pp = [jnp.exp(_dot_nt(qh[h], kp) + bp[h] - lse_h[h]) for h in range(2)]
            pc = [jnp.exp(_dot_nt(qh[h], kc) + bias_c[h] - lse_h[h]) for h in range(2)]
            dpp = [_dot_nt(dob[h], vp) for h in range(2)]
            dpc = [_dot_nt(dob[h], vc) for h in range(2)]
            return (j, r), pp, pc, dpp, dpc, kp, kc

        def dscores(unit, pp, pc, dpp, dpc, kp, kc):
            dl = _head_cols(rows.load(dl_ref, *unit))
            dsp = [(pp[h] * (dpp[h] - dl[h])).astype(BF16) for h in range(2)]
            dsc = [(pc[h] * (dpc[h] - dl[h])).astype(BF16) for h in range(2)]
            return unit, dsp, dsc, kp, kc

        def outputs(unit, dsp, dsc, kp, kc):
            dqh = [_dot(dsp[h], kp) + _dot(dsc[h], kc) for h in range(2)]
            rows.store(dq_ref, *unit, jnp.where(h0, dqh[0], dqh[1]) * ATT_SCALE)
            return ()

        _in_groups([(j, r) for j in range(nsub) for r in range(dil)], probs, dscores, outputs)

    own = lambda hp: hp
    ins = [(q, tile(lambda hp: 4 * g + hp)), (kv, prev(lambda hp: 8 * g + hp)), (kv, tile(lambda hp: 8 * g + hp)),
           (kv, prev(lambda hp: 8 * g + 4 + hp)), (kv, tile(lambda hp: 8 * g + 4 + hp)),
           (do, tile(own)), (delta, tile(own)), (lse, tile(own))]
    return _call(name, body, (bl, nt, HEAD_PAIRS), ins, [(SDS((T, QW), F32), tile(own))], hub=hub)[0]


def _attn_bwd_dkv(name, q, kv, do, delta, lse, g, bl, prev=None, hub=None):
    T = q.shape[0]
    dil, _, nsub, _, nt = _att_geom(T, bl, g)
    tile, _, nxt = _att_specs(T, bl, g)
    has_prev = prev is not None

    def body(*refs):
        k_ref, v_ref, q_ref, qn_ref, do_ref, don_ref, dl_ref, dln_ref, l_ref, ln_ref = refs[:10]
        rest = refs[10:]
        if has_prev:
            dkp_ref, dvp_ref, dk_ref, dv_ref = rest
        else:
            dk_ref, dv_ref = rest
        last = pl.program_id(1) == nt - 1
        h0, bias_p, bias_c = _att_consts(pl.program_id(2), dil)
        bias_last = [jnp.where(last, NEG_INF, b) for b in bias_p]

        rows = _Residues(dil)

        def probs(j, r):
            kb, vb = rows.load(k_ref, j, r).astype(BF16), rows.load(v_ref, j, r).astype(BF16)
            sets = [(q_ref, do_ref, dl_ref, l_ref, j, bias_c)]
            if j < nsub - 1:
                sets.append((q_ref, do_ref, dl_ref, l_ref, j + 1, bias_p))
            else:
                sets.append((qn_ref, don_ref, dln_ref, ln_ref, 0, bias_last))
            out = []
            for qr, dor, dlr, lr, jq, bias in sets:
                qs = rows.load(qr, jq, r) * ATT_SCALE
                do2 = rows.load(dor, jq, r)
                qh = _split_heads(qs, h0)
                dob = _split_heads(do2, h0)
                lse_h = _head_cols(rows.load(lr, jq, r))
                p = [jnp.exp(_dot_nt(qh[h], kb) + bias[h] - lse_h[h]) for h in range(2)]
                dp = [_dot_nt(dob[h], vb) for h in range(2)]
                out.append((p, dp, dlr, jq, qs.astype(BF16), do2.astype(BF16)))
            return (j, r), out

        def dscores(unit, sets):
            out = []
            for p, dp, dlr, jq, qsb, do2b in sets:
                dl = _head_cols(rows.load(dlr, jq, unit[1]))
                ds = [(p[h] * (dp[h] - dl[h])).astype(BF16) for h in range(2)]
                out.append(([p[h].astype(BF16) for h in range(2)], ds, qsb, do2b))
            return unit, out

        def outputs(unit, sets):
            dk = [None, None]
            dv = [None, None]
            for pb, ds, qsb, do2b in sets:
                for h in range(2):
                    dvh = _dot_tn(pb[h], do2b)
                    dkh = _dot_tn(ds[h], qsb)
                    dv[h] = dvh if dv[h] is None else dv[h] + dvh
                    dk[h] = dkh if dk[h] is None else dk[h] + dkh
            dk2 = jnp.where(h0, dk[0], dk[1])
            dv2 = jnp.where(h0, dv[0], dv[1])
            if has_prev:
                dk2 = dk2 + rows.load(dkp_ref, *unit)
                dv2 = dv2 + rows.load(dvp_ref, *unit)
            rows.store(dk_ref, *unit, dk2)
            rows.store(dv_ref, *unit, dv2)
            return ()

        _in_groups([(j, r) for j in range(nsub) for r in range(dil)], probs, dscores, outputs)

    own = lambda hp: hp
    qcol = lambda hp: 4 * g + hp
    ins = [(kv, tile(lambda hp: 8 * g + hp)), (kv, tile(lambda hp: 8 * g + 4 + hp)), (q, tile(qcol)), (q, nxt(qcol)),
           (do, tile(own)), (do, nxt(own)), (delta, tile(own)), (delta, nxt(own)), (lse, tile(own)), (lse, nxt(own))]
    if has_prev:
        ins += [(prev[0], tile(own)), (prev[1], tile(own))]
    out = (SDS((T, QW), F32), tile(own))
    return _call(name, body, (bl, nt, HEAD_PAIRS), ins, [out, out], hub=hub)


def _final_loss(name, h, tgt, g, tm=256):
    T, D = h.shape
    n_steps = T // tm

    def body(h_ref, t_ref, g_ref, dh_ref, loss_ref, dg_ref, sq_ref):
        i = pl.program_id(0)
        x = h_ref[...]
        rstd = _rstd(x)
        xhat = x * rstd
        err = xhat * g_ref[...] - t_ref[...]
        _acc8(sq_ref, err * err, i, n_steps)
        dy = err * (1.0 / D)
        dxhat = dy * g_ref[...]
        dh_ref[...] = rstd * (dxhat - xhat * jnp.mean(dxhat * xhat, axis=-1, keepdims=True))
        _acc8(dg_ref, dy * xhat, i, n_steps)

        @pl.when(i == n_steps - 1)
        def _():
            loss_ref[...] = jnp.full(loss_ref.shape, jnp.sum(sq_ref[0:1, :]), F32)

    outs = [_out_rows(T, D, F32, tm), (SDS((SUBLANES, 128), F32), pl.BlockSpec((SUBLANES, 128), lambda i: (0, 0))),
            _out_acc8(D)]
    dh, loss, dg = _call(name, body, (n_steps,), [_rows(h, tm), _rows(tgt, tm), _full(g)], outs,
                         scratch=[pltpu.VMEM((SUBLANES, D), F32)])
    return dh, loss[0, 0], dg[0:1]


def _conv_bwd(name, bcu, dgated, cw, seq, tm=256, hub=None):
    T, D = dgated.shape
    n_steps = T // tm

    def body(b_ref, c_ref, u_ref, ch_ref, uh_ref, dg_ref, dgn_ref, bn_ref, cw_ref, o_ref, t0_ref, t1_ref, t2_ref):
        i = pl.program_id(0)
        first = (i * tm) % seq == 0
        last = ((i + 1) * tm) % seq == 0
        _, (b, c, u), conv, (cu, cu1, cu2) = _gate(b_ref, c_ref, u_ref, ch_ref, uh_ref, cw_ref, first)
        dgat = dg_ref[...]
        dconv = dgat * b
        nxt = jnp.where(last, 0.0, dgn_ref[...] * bn_ref[...].astype(F32))
        rows = lax.broadcasted_iota(jnp.int32, dconv.shape, 0)
        n1 = nxt[0:1, :]
        n2 = nxt[1:2, :]
        dc1 = jnp.where(rows == tm - 1, n1, pltpu.roll(dconv, tm - 1, 0))
        dc2 = jnp.where(rows == tm - 1, n2, jnp.where(rows == tm - 2, n1, pltpu.roll(dconv, tm - 2, 0)))
        dcu = cw_ref[0:1, :] * dconv + cw_ref[1:2, :] * dc1 + cw_ref[2:3, :] * dc2
        o_ref[:, 0:D] = (dgat * conv).astype(BF16)
        o_ref[:, D:2 * D] = (dcu * u).astype(BF16)
        o_ref[:, 2 * D:3 * D] = (dcu * c).astype(BF16)
        _acc8(t0_ref, dconv * cu, i, n_steps)
        _acc8(t1_ref, dconv * cu1, i, n_steps)
        _acc8(t2_ref, dconv * cu2, i, n_steps)

    ins = [_rows(bcu, tm, D, 0), _rows(bcu, tm, D, 1), _rows(bcu, tm, D, 2), _prev8(bcu, tm, D, 1), _prev8(bcu, tm, D, 2),
           _rows(dgated, tm), _next8(dgated, tm, D, 0), _next8(bcu, tm, D, 0), _full(cw)]
    outs = [_out_rows(T, 3 * D, BF16, tm), _out_acc8(D), _out_acc8(D), _out_acc8(D)]
    dbcu, t0, t1, t2 = _call(name, body, (n_steps,), ins, outs, hub=hub)
    return dbcu, jnp.concatenate([t0[0:1], t1[0:1], t2[0:1]], axis=0)


def _sum8_adamw(name, parts, w, m, v, tr):
    R, C = w.shape
    b1c = 1.0 - ADAM_B1 ** ADAM_STEP
    b2c = 1.0 - ADAM_B2 ** ADAM_STEP

    def body(p_ref, w_ref, m_ref, v_ref, g_ref, d_ref, nm_ref, nv_ref):
        g = p_ref[0].astype(F32)
        for j in range(1, N_DEV):
            g = g + p_ref[j].astype(F32)
        nm = ADAM_B1 * m_ref[...] + (1.0 - ADAM_B1) * g
        nv = ADAM_B2 * v_ref[...] + (1.0 - ADAM_B2) * (g * g)
        m_hat = nm / b1c
        v_hat = nv / b2c
        g_ref[...] = g
        d_ref[...] = -ADAM_LR * (m_hat / (jnp.sqrt(v_hat) + ADAM_EPS) + ADAM_WD * w_ref[...])
        nm_ref[...] = nm
        nv_ref[...] = nv

    ins = [(parts, pl.BlockSpec((N_DEV, tr, C), lambda i: (0, i, 0))), _rows(w, tr), _rows(m, tr), _rows(v, tr)]
    outs = [_out_rows(R, C, F32, tr)] * 4
    return _call(name, body, (R // tr,), ins, outs)


def _all_gather(name, items):
    n = len(items)
    shapes = [tuple(a.shape if idx is None else a.shape[1:]) for a, idx in items]

    def body(*refs):
        x_refs, out_refs = refs[:n], refs[n:2 * n]
        send_sems, recv_sems, local_sems = refs[2 * n:]
        x, y, c = _mesh_pos()
        me, sibling = (x, y, c), (x, y, 1 - c)
        chips = [(1 - x, y), (x, 1 - y), (1 - x, 1 - y)]

        def copy(t, k, block, to, own=False):
            dst = out_refs[t].at[4 * block[0] + 2 * block[1] + block[2]]
            src = dst
            if own:
                src = x_refs[t] if items[t][1] is None else x_refs[t].at[items[t][1]]
            return pltpu.make_async_remote_copy(
                src_ref=src, dst_ref=dst, send_sem=send_sems.at[t, k], recv_sem=recv_sems.at[t, k],
                device_id=to, device_id_type=pl.DeviceIdType.MESH)

        started = []
        for t in range(n):
            src = x_refs[t] if items[t][1] is None else x_refs[t].at[items[t][1]]
            mine = pltpu.make_async_copy(src, out_refs[t].at[4 * x + 2 * y + c], local_sems.at[t])
            mine.start()
            first = [copy(t, 0, me, sibling, own=True)]
            first += [copy(t, 1 + j, me, (*chip, c), own=True) for j, chip in enumerate(chips)]
            for cp in first:
                cp.start()
            started.append((mine, first))
        passed = []
        for t in range(n):
            for j, chip in enumerate(chips):
                copy(t, 1 + j, (*chip, c), me).wait_recv()
                fwd = copy(t, 4 + j, (*chip, c), sibling)
                fwd.start()
                passed.append(fwd)
        for t in range(n):
            copy(t, 0, sibling, me).wait_recv()
            for j, chip in enumerate(chips):
                copy(t, 4 + j, (*chip, 1 - c), me).wait_recv()
        for mine, first in started:
            for cp in first:
                cp.wait_send()
            mine.wait()
        for cp in passed:
            cp.wait_send()

    any_spec = pl.BlockSpec(memory_space=pl.ANY)
    return pl.pallas_call(
        body, name=name,
        out_shape=[SDS((N_DEV,) + s, a.dtype) for s, (a, _) in zip(shapes, items)],
        in_specs=[any_spec] * n,
        out_specs=[any_spec] * n,
        scratch_shapes=[pltpu.SemaphoreType.DMA((n, 7)), pltpu.SemaphoreType.DMA((n, 7)), pltpu.SemaphoreType.DMA((n,))],
    )(*[a for a, _ in items])


def _pad8(t):
    return jnp.pad(t, ((0, SUBLANES - t.shape[0]), (0, 0)))


def _rows_merged(w):
    return w.reshape(w.shape[0] * w.shape[1], w.shape[2])


def _local_grads(x, tgt, norm_mix, norm_mlp, norm_kv, norm_final, conv_w, hub):
    bl, seq, D = x.shape
    T = bl * seq
    h = x.reshape(T, D)
    tgt = tgt.reshape(T, D)
    row = lambda t, l: t[l:l + 1]
    W = hub.weights
    saved = []
    kv = h_kv = None
    for l in range(DEPTH):
        if l < N_A_LAYERS:
            bcu = _norm_mm(f"l{l}_in", h, row(norm_mix, l), W["w_a_in", l], out_dtype=BF16, hub=hub)
            h2 = _gate_mm_res(f"l{l}_conv_out", bcu, _pad8(conv_w[l]), _rows_merged(W["w_a_out", l]), h, seq, hub=hub)
            saved.append((h, bcu, h2))
        else:
            i = l - N_A_LAYERS
            if l == N_A_LAYERS:
                h_kv = h
                kv = _norm_mm("kv", h, norm_kv.reshape(1, D), W["w_kv", None], hub=hub)
            q = _norm_mm(f"l{l}_q", h, row(norm_mix, l), W["w_q", i])
            per_group = [_attn_fwd(f"l{l}_att{g}", q, kv, g, bl, hub=hub) for g in range(N_GROUPS)]
            o, lse = _combine(f"l{l}_combine", [p[0] for p in per_group], [p[1] for p in per_group])
            h2 = _mm_res(f"l{l}_att_out", o, W["w_o", i], h)
            saved.append((h, q, o, lse, h2))
        a = _norm_mm(f"l{l}_up", h2, row(norm_mlp, l), W["w_up", l], out_dtype=BF16, hub=hub)
        h = _relu2_mm_res(f"l{l}_down", a, _rows_merged(W["w_down", l]), h2, hub=hub)
        saved[-1] = saved[-1] + (a,)

    dh, sq_err, d_norm_final = _final_loss("loss", h, tgt, norm_final.reshape(1, D))

    d_norm_mix = [None] * DEPTH
    d_norm_mlp = [None] * DEPTH
    d_conv = [None] * N_A_LAYERS
    d_norm_kv = None
    dkv_acc = [None] * N_GROUPS
    G = hub.grads
    as_slots = lambda g: g.reshape(N_DEV, g.shape[0] // N_DEV, g.shape[1])
    tt = DW_TOKENS
    tc = DW_TOKENS // 2
    for l in reversed(range(DEPTH)):
        a, h2 = saved[l][-1], saved[l][-2]
        h_in = saved[l][0]
        g_mlp = row(norm_mlp, l)
        g_mix = row(norm_mix, l)
        w_up_l = W["w_up", l]
        FF = N_DEV * w_up_l.shape[2]
        da = _nt_relu2_bwd(f"l{l}_down_bwd", dh, _rows_merged(W["w_down", l]), a, hub=hub)
        G["w_down", l] = as_slots(_tn(f"l{l}_dw_down", [_rows2(a, tt, FF // 2, lambda s: s)], _relu2,
                                      [_rows2(dh, tt)], _val, FF, D, T, tt, split=("k", 2)))
        G["w_up", l] = _tn(f"l{l}_dw_up", [_rows2(h2, 2 * tt), _full2(g_mlp)], _normed,
                           [_rows2(da, 2 * tt, FF // 4, lambda s: s)], _val, D, FF, T, 2 * tt, split=("n", 4),
                           out_cols=w_up_l.shape[2], hub=hub)
        dh2, d_norm_mlp[l] = _nt_norm_bwd(f"l{l}_up_bwd", [da], w_up_l, h2, g_mlp, dh, hub=hub)
        if l >= N_A_LAYERS:
            i = l - N_A_LAYERS
            _, q, o, lse, _, _ = saved[l]
            w_o_i, w_q_i = W["w_o", i], W["w_q", i]
            do, delta = _att_out_bwd(f"l{l}_att_out_bwd", dh2, w_o_i, o)
            G["w_o", i] = _tn(f"l{l}_dw_o", [_rows2(o, tt)], _val, [_rows2(dh2, tt)], _val, QW, D, T, tt,
                              out_cols=w_o_i.shape[2])
            dqs = []
            for g in range(N_GROUPS):
                dqs.append(_attn_bwd_dq(f"l{l}_att{g}_dq", q, kv, do, delta, lse, g, bl, hub=hub))
                dkv_acc[g] = _attn_bwd_dkv(f"l{l}_att{g}_dkv", q, kv, do, delta, lse, g, bl, prev=dkv_acc[g], hub=hub)
            G["w_q", i] = _tn(f"l{l}_dw_q", [_rows2(h_in, tt), _full2(g_mix)], _normed,
                              [_rows2(t, tt) for t in dqs], _concat_f32, D, N_GROUPS * QW, T, tt, out_cols=w_q_i.shape[2])
            dh, d_norm_mix[l] = _nt_norm_bwd(f"l{l}_q_bwd", dqs, w_q_i, h_in, g_mix, dh2, hub=hub)
            if l == N_A_LAYERS:
                dkvs = [t for pair in dkv_acc for t in pair]
                g_kv = norm_kv.reshape(1, D)
                w_kv = W["w_kv", None]
                per_call = len(dkvs) // 2
                halves = [_tn(f"dw_kv{p}", [_rows2(h_kv, tt), _full2(g_kv)], _normed,
                              [_rows2(t, tt) for t in dkvs[p * per_call:(p + 1) * per_call]], _concat_f32,
                              D, per_call * QW, T, tt, out_cols=w_kv.shape[2]) for p in range(2)]
                G["w_kv", None] = jnp.concatenate(halves, axis=0)
                dh, d_norm_kv = _nt_norm_bwd("kv_bwd", dkvs, w_kv, h_kv, g_kv, dh, tm=MM_ROWS // 2, hub=hub)
        else:
            _, bcu, _, _ = saved[l]
            cw = _pad8(conv_w[l])
            w_in_l = W["w_a_in", l]
            dgated = _nt_plain(f"l{l}_conv_out_bwd", dh2, _rows_merged(W["w_a_out", l]))

            def gated_tile(b_ref, c_ref, u_ref, ch_ref, uh_ref, cw_ref):
                first = (pl.program_id(1) * tc) % seq == 0
                return _gate(b_ref, c_ref, u_ref, ch_ref, uh_ref, cw_ref, first)[0]

            G["w_a_out", l] = as_slots(_tn(
                f"l{l}_dw_conv_out",
                [_rows2(bcu, tc, D, lambda s: 0), _rows2(bcu, tc, D, lambda s: 1), _rows2(bcu, tc, D, lambda s: 2),
                 _prev8_2(bcu, tc, D, 1), _prev8_2(bcu, tc, D, 2), _full2(cw)], gated_tile,
                [_rows2(dh2, tc)], _val, D, D, T, tc, hub=hub))
            dbcu, d_conv[l] = _conv_bwd(f"l{l}_conv_bwd", bcu, dgated, cw, seq, hub=hub)
            G["w_a_in", l] = _tn(f"l{l}_dw_in", [_rows2(h_in, tt), _full2(g_mix)], _normed,
                                 [_rows2(dbcu, tt, 3 * D // 2, lambda s: s)], _val, D, 3 * D, T, tt, split=("n", 2),
                                 out_cols=w_in_l.shape[2], hub=hub)
            dh, d_norm_mix[l] = _nt_norm_bwd(f"l{l}_in_bwd", [dbcu], w_in_l, h_in, g_mix, dh2, hub=hub)

    small = jnp.concatenate(d_norm_mix + d_norm_mlp + [d_norm_kv, d_norm_final] + d_conv, axis=0)
    return sq_err, dh.reshape(bl, seq, D), small


def kernel(x, norm_mix, norm_mlp, w_a_in, conv_w, w_a_out, norm_kv, w_kv, w_q, w_o, w_up, w_down, norm_final, loss_target, m_norm_mix, m_norm_mlp, m_w_a_in, m_conv_w, m_w_a_out, m_norm_kv, m_w_kv, m_w_q, m_w_o, m_w_up, m_w_down, m_norm_final, v_norm_mix, v_norm_mlp, v_w_a_in, v_conv_w, v_w_a_out, v_norm_kv, v_w_kv, v_w_q, v_w_o, v_w_up, v_w_down, v_norm_final):
    D = x.shape[-1]
    xi, yi, ci = _mesh_pos()
    me_idx = 4 * xi + 2 * yi + ci
    w_big = dict(w_a_in=w_a_in, w_a_out=w_a_out, w_kv=w_kv, w_q=w_q, w_o=w_o, w_up=w_up, w_down=w_down)
    m_big = dict(w_a_in=m_w_a_in, w_a_out=m_w_a_out, w_kv=m_w_kv, w_q=m_w_q, w_o=m_w_o, w_up=m_w_up, w_down=m_w_down)
    v_big = dict(w_a_in=v_w_a_in, w_a_out=v_w_a_out, w_kv=v_w_kv, w_q=v_w_q, w_o=v_w_o, w_up=v_w_up, w_down=v_w_down)
    names = list(w_big)

    shards = {n: w.astype(BF16) for n, w in w_big.items()}
    landing = {n: lax.empty((N_DEV,) + w.shape, BF16) for n, w in w_big.items()}
    hub = _Hub(FETCH_DURING, PUSH_DURING, shards, landing)
    dc = conv_w.shape[-1]
    taps = conv_w.shape[0] * conv_w.shape[1]
    got = _all_gather("gather_first", [(shards[n], l) for n, l in FETCH_UP_FRONT] + [(_pad8(conv_w.reshape(taps, dc)), None)])
    for key, w in zip(FETCH_UP_FRONT, got):
        hub.weights[key] = w
    conv_full = jnp.moveaxis(got[-1][:, :taps], 0, 1).reshape(conv_w.shape[0], conv_w.shape[1], N_DEV * dc)

    sq_err, grad_x, small = _local_grads(x, loss_target, norm_mix, norm_mlp, norm_kv, norm_final, conv_full, hub)
    loss = lax.psum(sq_err * (0.5 / D), ("x", "y", "c"))

    grads, deltas, new_m, new_v = {}, {}, {}, {}
    for n in names:
        shape = w_big[n].shape
        cols = shape[-1]
        flat = lambda t: t.reshape(-1, cols)
        parts = hub.landing[n].reshape(N_DEV, -1, cols)
        outs = _sum8_adamw(f"adamw_{n}", parts, flat(w_big[n]), flat(m_big[n]), flat(v_big[n]), tr=min(256, parts.shape[1]))
        grads[n], deltas[n], new_m[n], new_v[n] = (t.reshape(shape) for t in outs)

    n_gain = 2 * DEPTH + 2
    rows_small = small.shape[0]
    small_all = _all_gather("gather_small_grads", [(small, None)])[0]

    def small_pack(nm, nl, nk, nf, cw):
        gains = jnp.concatenate([nm, nl, nk.reshape(1, D), nf.reshape(1, D)], axis=0)
        taps_full = lax.dynamic_update_slice(jnp.zeros((taps, D), F32), cw.reshape(taps, dc), (0, me_idx * dc))
        return jnp.concatenate([gains, taps_full], axis=0)

    sp = [small_pack(*t) for t in ((norm_mix, norm_mlp, norm_kv, norm_final, conv_w),
                                   (m_norm_mix, m_norm_mlp, m_norm_kv, m_norm_final, m_conv_w),
                                   (v_norm_mix, v_norm_mlp, v_norm_kv, v_norm_final, v_conv_w))]
    small_out = _sum8_adamw("adamw_small", small_all, *sp, tr=rows_small)

    def small_unpack(t):
        res = dict(norm_mix=t[0:DEPTH], norm_mlp=t[DEPTH:2 * DEPTH], norm_kv=t[2 * DEPTH], norm_final=t[2 * DEPTH + 1])
        res["conv_w"] = lax.dynamic_slice(t[n_gain:], (0, me_idx * dc), (taps, dc)).reshape(conv_w.shape)
        return res

    for dst, t in zip((grads, deltas, new_m, new_v), small_out):
        dst.update(small_unpack(t))

    order = ["norm_mix", "norm_mlp", "w_a_in", "conv_w", "w_a_out", "norm_kv", "w_kv", "w_q", "w_o", "w_up", "w_down",
             "norm_final"]
    return (loss, grad_x, *[grads[n] for n in order], *[deltas[n] for n in order], *[new_m[n] for n in order],
            *[new_v[n] for n in order])
```

```python
import functools

import jax
import jax.numpy as jnp
from jax import lax
from jax.experimental import pallas as pl
from jax.experimental.pallas import tpu as pltpu

F32 = jnp.float32
BF16 = jnp.bfloat16
SDS = jax.ShapeDtypeStruct

EPS = 1e-5
N_A_LAYERS = 2
DEPTH = 4
PATTERNS = ((128, 1), (512, 4), (2048, 16))
N_GROUPS = 3
H_G = 8
HEAD_DIM = 64
QW = H_G * HEAD_DIM
ATT_BLK = 128
ALIBI_MAX_BIAS = 8.0
NEG_INF = -1e30

ADAM_LR = 0.001
ADAM_B1 = 0.9
ADAM_B2 = 0.999
ADAM_EPS = 1e-08
ADAM_WD = 0.01
ADAM_STEP = 10

N_DEV = 8
SUBLANES = 8
HALO = 16
V7X_VMEM_LIMIT = 48 * 1024 * 1024
MXU_COLS = 256
MM_CHUNK = 512
MM_ROWS = 512
DW_TOKENS = 1024

FETCH_UP_FRONT = [("w_a_in", 0), ("w_a_out", 0)]
FETCH_DURING = {
    "l0_in": [("w_up", 0)], "l0_conv_out": [("w_down", 0)], "l0_up": [("w_a_in", 1), ("w_a_out", 1)], "l0_down": [("w_up", 1)],
    "l1_in": [("w_down", 1)], "l1_conv_out": [("w_kv", None)],
    "l1_up": [("w_q", 0), ("w_o", 0), ("w_q", 1), ("w_o", 1)], "l1_down": [("w_up", 2)],
    "kv": [("w_down", 2)], "l2_att0": [("w_up", 3)], "l2_att1": [("w_down", 3)],
}
PUSH_DURING = {
    "l3_dw_up": [("w_down", 3, 0, 2)], "l3_up_bwd": [("w_down", 3, 1, 2)],
    "l3_att0_dq": [("w_up", 3, 0, 2)], "l3_att0_dkv": [("w_up", 3, 1, 2)], "l3_att1_dq": [("w_o", 1)], "l3_q_bwd": [("w_q", 1)],
    "l2_dw_up": [("w_down", 2, 0, 2)], "l2_up_bwd": [("w_down", 2, 1, 2)],
    "l2_att0_dq": [("w_up", 2, 0, 2)], "l2_att0_dkv": [("w_up", 2, 1, 2)], "l2_att1_dq": [("w_o", 0)], "l2_q_bwd": [("w_q", 0)],
    "kv_bwd": [("w_kv", None, 0, 2)], "l1_down_bwd": [("w_kv", None, 1, 2)],
    "l1_dw_up": [("w_down", 1, 0, 2)], "l1_up_bwd": [("w_down", 1, 1, 2)], "l1_conv_bwd": [("w_up", 1, 0, 2)],
    "l1_dw_in": [("w_up", 1, 1, 2), ("w_a_out", 1)], "l1_in_bwd": [("w_a_in", 1, 0, 2)], "l0_down_bwd": [("w_a_in", 1, 1, 2)],
    "l0_dw_up": [("w_down", 0, 0, 2)], "l0_up_bwd": [("w_down", 0, 1, 2)], "l0_conv_bwd": [("w_up", 0, 0, 2)],
    "l0_dw_in": [("w_up", 0, 1, 2), ("w_a_out", 0)], "l0_in_bwd": [("w_a_in", 0)],
}


def _mesh_pos():
    return lax.axis_index("x"), lax.axis_index("y"), lax.axis_index("c")


def _flip(v, bit):
    return 1 - v if bit else v


class _Transfer:
    def __init__(self, kind, key, src, src_idx=None, dst=None, dst_idx=None, dst_shape=None, rows=None):
        self.kind, self.key, self.src, self.src_idx = kind, key, src, src_idx
        self.dst, self.dst_idx, self.dst_shape, self.rows = dst, dst_idx, dst_shape, rows

    def copies(self, src_ref, dst_ref, send_sems, recv_sems, local_sem):
        x, y, c = _mesh_pos()
        me = 4 * x + 2 * y + c
        part = (lambda r: r) if self.rows is None else (lambda r: r.at[pl.ds(*self.rows)])

        def dst_slot(j):
            r = dst_ref.at[j]
            return part(r if self.dst_idx is None else r.at[self.dst_idx])

        def copy(k, src, dst_j, to):
            return pltpu.make_async_remote_copy(
                src_ref=src, dst_ref=dst_slot(dst_j), send_sem=send_sems.at[k], recv_sem=recv_sems.at[k],
                device_id=to, device_id_type=pl.DeviceIdType.MESH)

        if self.kind == "exchange":
            local = pltpu.make_async_copy(part(src_ref.at[me]), dst_slot(me), local_sem)
            sends, arrivals = [], []
            for k in range(1, N_DEV):
                peer = (_flip(x, k & 4), _flip(y, k & 2), _flip(c, k & 1))
                peer_idx = 4 * peer[0] + 2 * peer[1] + peer[2]
                sends.append(copy(k - 1, part(src_ref.at[peer_idx]), me, peer))
                arrivals.append(copy(k - 1, part(src_ref.at[peer_idx]), peer_idx, peer))
            return local, sends, [], arrivals

        own = part(src_ref if self.src_idx is None else src_ref.at[self.src_idx])
        idx = lambda px, py, pc: 4 * px + 2 * py + pc
        sibling = (x, y, 1 - c)
        chips = [(1 - x, y), (x, 1 - y), (1 - x, 1 - y)]
        local = pltpu.make_async_copy(own, dst_slot(me), local_sem)
        sends = [copy(0, own, me, sibling)] + [copy(1 + j, own, me, (*chip, c)) for j, chip in enumerate(chips)]
        relays = [(copy(1 + j, own, idx(*chip, c), sibling), copy(4 + j, dst_slot(idx(*chip, c)), idx(*chip, c), sibling))
                  for j, chip in enumerate(chips)]
        arrivals = [copy(0, own, idx(*sibling), sibling)]
        arrivals += [copy(4 + j, own, idx(*chip, 1 - c), sibling) for j, chip in enumerate(chips)]
        return local, sends, relays, arrivals


class _Hub:
    def __init__(self, fetch, push, shards, landing):
        self.fetch, self.push, self.shards, self.landing = fetch, push, shards, landing
        self.weights = {}
        self.arriving = {}
        self.grads = {}

    def transfers(self, host):
        out = []
        for name, l, *part in self.fetch.get(host, ()):
            src = self.shards[name]
            shard = tuple(src.shape if l is None else src.shape[1:])
            p, n = part or (0, 1)
            rows = None if n == 1 else (p * (shard[0] // n), shard[0] // n)
            out.append(_Transfer("gather", (name, l, p == n - 1), src, src_idx=l, dst=self.arriving.get((name, l)),
                                 dst_shape=(N_DEV,) + shard, rows=rows))
        for name, l, *part in self.push.get(host, ()):
            src = self.grads[name, l]
            p, n = part or (0, 1)
            rows = None if n == 1 else (p * (src.shape[1] // n), src.shape[1] // n)
            out.append(_Transfer("exchange", (name, l, p == n - 1), src, dst=self.landing[name], dst_idx=l, rows=rows))
        return out

    def accept(self, transfers, results):
        for t, r in zip(transfers, results):
            name, l, complete = t.key
            if t.kind == "exchange":
                self.landing[name] = r
            elif complete:
                self.weights[name, l] = r
            else:
                self.arriving[name, l] = r


def _call(name, body, grid, ins, outs, scratch=(), hub=None):
    transfers = hub.transfers(name) if hub is not None else []
    n_in, n_out, n_scr, n_tr = len(ins), len(outs), len(scratch), len(transfers)
    c_in, c_out, aliases, places = [], [], {}, []
    for t in transfers:
        c_in.append(t.src)
        src_pos = len(c_in) - 1
        if t.dst is not None:
            c_in.append(t.dst)
            aliases[n_in + len(c_in) - 1] = n_out + len(c_out)
            c_out.append(SDS(t.dst.shape, t.dst.dtype))
        else:
            c_out.append(SDS(t.dst_shape, t.src.dtype))
        places.append((src_pos, len(c_out) - 1))
    sems = [pltpu.SemaphoreType.DMA((n_tr, N_DEV - 1)), pltpu.SemaphoreType.DMA((n_tr, N_DEV - 1)),
            pltpu.SemaphoreType.DMA((n_tr,))] if n_tr else []

    def wrapped(*refs):
        in_refs = refs[:n_in]
        cin_refs = refs[n_in:n_in + len(c_in)]
        o0 = n_in + len(c_in)
        out_refs = refs[o0:o0 + n_out]
        cout_refs = refs[o0 + n_out:o0 + n_out + len(c_out)]
        s0 = o0 + n_out + len(c_out)
        scr_refs = refs[s0:s0 + n_scr]
        if n_tr:
            send_sems, recv_sems, local_sems = refs[s0 + n_scr:]
            first = last = relay = None
            for ax, n in enumerate(grid):
                i = pl.program_id(ax)
                at_relay = (i == max(n - 2, 0)) if ax == len(grid) - 1 else (i == n - 1)
                first = (i == 0) if first is None else first & (i == 0)
                last = (i == n - 1) if last is None else last & (i == n - 1)
                relay = at_relay if relay is None else relay & at_relay

            def all_copies():
                return [t.copies(cin_refs[sp], cout_refs[dp], send_sems.at[n], recv_sems.at[n], local_sems.at[n])
                        for n, (t, (sp, dp)) in enumerate(zip(transfers, places))]

            @pl.when(first)
            def _():
                for local, sends, _, _ in all_copies():
                    local.start()
                    for cp in sends:
                        cp.start()

            def pass_on():
                @pl.when(relay)
                def _():
                    for _, _, relays, _ in all_copies():
                        for arrival, onward in relays:
                            arrival.wait_recv()
                            onward.start()

            if grid[-1] > 1:
                pass_on()

        body(*in_refs, *out_refs, *scr_refs)

        if n_tr:
            if grid[-1] == 1:
                pass_on()

            @pl.when(last)
            def _():
                for local, sends, relays, arrivals in all_copies():
                    for cp in arrivals:
                        cp.wait_recv()
                    for cp in sends + [onward for _, onward in relays]:
                        cp.wait_send()
                    local.wait()

    any_spec = pl.BlockSpec(memory_space=pl.ANY)
    res = pl.pallas_call(
        wrapped,
        name=name,
        grid=grid,
        in_specs=[s for _, s in ins] + [any_spec] * len(c_in),
        out_specs=[s for _, s in outs] + [any_spec] * len(c_out),
        out_shape=[o for o, _ in outs] + c_out,
        scratch_shapes=list(scratch) + sems,
        input_output_aliases=aliases,
        compiler_params=pltpu.CompilerParams(
            dimension_semantics=("arbitrary",) * len(grid), vmem_limit_bytes=V7X_VMEM_LIMIT),
    )(*[a for a, _ in ins], *c_in)
    if n_tr:
        hub.accept(transfers, res[n_out:])
    return res[:n_out]


def _rows(a, tm, cb=None, col=0):
    cb = cb or a.shape[1]
    return (a, pl.BlockSpec((tm, cb), lambda i: (i, col)))


def _full(a):
    nd = a.ndim
    return (a, pl.BlockSpec(a.shape, lambda i: (0,) * nd))


def _prev8(a, tm, cb, col):
    return (a, pl.BlockSpec((HALO, cb), lambda i: (jnp.maximum(i * (tm // HALO) - 1, 0), col)))


def _next8(a, tm, cb, col):
    last = a.shape[0] // HALO - 1
    return (a, pl.BlockSpec((HALO, cb), lambda i: (jnp.minimum((i + 1) * (tm // HALO), last), col)))


def _rows2(a, tt, cb=None, colfn=None):
    cb = cb or a.shape[1]
    colfn = colfn or (lambda s: 0)
    return (a, pl.BlockSpec((tt, cb), lambda s, t: (t, colfn(s))))


def _full2(a):
    nd = a.ndim
    return (a, pl.BlockSpec(a.shape, lambda s, t: (0,) * nd))


def _prev8_2(a, tt, cb, col):
    return (a, pl.BlockSpec((HALO, cb), lambda s, t: (jnp.maximum(t * (tt // HALO) - 1, 0), col)))


def _out_rows(T, n, dtype, tm):
    return (SDS((T, n), dtype), pl.BlockSpec((tm, n), lambda i: (i, 0)))


def _out_acc8(d):
    return (SDS((SUBLANES, d), F32), pl.BlockSpec((SUBLANES, d), lambda i: (0, 0)))


def _rstd(x):
    return lax.rsqrt(jnp.mean(x * x, axis=-1, keepdims=True) + EPS)


def _normed(h_ref, g_ref):
    x = h_ref[...]
    return x * _rstd(x) * g_ref[...]


def _acc8(ref, val, i, n):
    part = val.reshape(-1, SUBLANES, val.shape[-1]).sum(axis=0)

    @pl.when(i == 0)
    def _():
        ref[...] = part

    @pl.when(i > 0)
    def _():
        ref[...] += part

    @pl.when(i == n - 1)
    def _():
        ref[...] = jnp.broadcast_to(jnp.sum(ref[...], axis=0, keepdims=True), ref.shape)


def _gate(b_ref, c_ref, u_ref, ch_ref, uh_ref, cw_ref, first):
    b, c, u = (r[...].astype(F32) for r in (b_ref, c_ref, u_ref))
    cu = c * u
    halo = jnp.where(first, 0.0, ch_ref[...].astype(F32) * uh_ref[...].astype(F32))
    rows = lax.broadcasted_iota(jnp.int32, cu.shape, 0)
    h1 = halo[HALO - 1:HALO, :]
    h2 = halo[HALO - 2:HALO - 1, :]
    cu1 = jnp.where(rows == 0, h1, pltpu.roll(cu, 1, 0))
    cu2 = jnp.where(rows == 0, h2, jnp.where(rows == 1, h1, pltpu.roll(cu, 2, 0)))
    conv = cw_ref[0:1, :] * cu + cw_ref[1:2, :] * cu1 + cw_ref[2:3, :] * cu2
    return b * conv, (b, c, u), conv, (cu, cu1, cu2)


def _relu2(a_ref):
    r = jnp.maximum(a_ref[...].astype(F32), 0.0)
    return r * r


def _dot(a, b):
    return jnp.dot(a, b, preferred_element_type=F32)


def _dot_nt(a, b):
    return lax.dot_general(a, b, (((1,), (1,)), ((), ())), preferred_element_type=F32)


def _dot_tn(a, b):
    return lax.dot_general(a, b, (((0,), (0,)), ((), ())), preferred_element_type=F32)


def _chunks(n):
    c = min(MM_CHUNK, n)
    while n % c:
        c -= 128
    assert c > 0, n
    return [(k * c, (k + 1) * c) for k in range(n // c)]


def _col_weight(w):
    _, K, ns = w.shape
    N = N_DEV * ns
    direct = ns % MXU_COLS == 0
    scratch = [] if direct else [pltpu.VMEM((K, N), BF16)]

    def prepare(w_ref, s_ref, step):
        if direct:
            return

        @pl.when(step == 0)
        def _():
            for j in range(N_DEV):
                s_ref[:, j * ns:(j + 1) * ns] = w_ref[j]

    def chunks(w_ref, s_ref):
        if direct:
            return [(j * ns, (j + 1) * ns, (lambda j=j: w_ref[j])) for j in range(N_DEV)]
        return [(lo, hi, (lambda lo=lo, hi=hi: s_ref[:, lo:hi])) for lo, hi in _chunks(N)]

    return N, scratch, prepare, chunks


def _out_cols(n, T, tm):
    return (SDS((n, T), BF16), pl.BlockSpec((n, tm), lambda i: (0, i)))


def _norm_mm(name, h, g, w, tm=MM_ROWS, out_dtype=F32, transposed=False, hub=None):
    T, D = h.shape
    N, w_scratch, prepare, chunks = _col_weight(w)

    def body(h_ref, g_ref, w_ref, o_ref, *rest):
        at_ref, s = (rest[0], rest[1:]) if transposed else (None, rest)
        s_ref = s[0] if s else None
        prepare(w_ref, s_ref, pl.program_id(0))
        a32 = _normed(h_ref, g_ref)
        a = a32.astype(BF16)
        for lo, hi, load in chunks(w_ref, s_ref):
            o_ref[:, lo:hi] = _dot(a, load()).astype(out_dtype)
        if transposed:
            at_ref[...] = a32.T.astype(BF16)

    outs = [_out_rows(T, N, out_dtype, tm)] + ([_out_cols(D, T, tm)] if transposed else [])
    res = _call(name, body, (T // tm,), [_rows(h, tm), _full(g), _full(w)], outs, scratch=w_scratch, hub=hub)
    return res if transposed else res[0]


def _gate_mm_res(name, bcu, cw, w, h, seq, tm=MM_ROWS, hub=None):
    T, D = h.shape

    def body(b_ref, c_ref, u_ref, ch_ref, uh_ref, cw_ref, w_ref, h_ref, o_ref, gt_ref):
        first = (pl.program_id(0) * tm) % seq == 0
        gated32 = _gate(b_ref, c_ref, u_ref, ch_ref, uh_ref, cw_ref, first)[0]
        gated = gated32.astype(BF16)
        for lo, hi in _chunks(D):
            o_ref[:, lo:hi] = h_ref[:, lo:hi] + _dot(gated, w_ref[:, lo:hi])
        gt_ref[...] = gated32.T.astype(BF16)

    ins = [_rows(bcu, tm, D, 0), _rows(bcu, tm, D, 1), _rows(bcu, tm, D, 2), _prev8(bcu, tm, D, 1),
           _prev8(bcu, tm, D, 2), _full(cw), _full(w), _rows(h, tm)]
    return _call(name, body, (T // tm,), ins, [_out_rows(T, D, F32, tm), _out_cols(D, T, tm)], hub=hub)


def _relu2_mm_res(name, a, w, h, tm=MM_ROWS, hub=None):
    T, D = h.shape
    K = a.shape[1]

    def body(a_ref, w_ref, h_ref, o_ref, acc_ref):
        for n, (lo, hi) in enumerate(_chunks(K)):
            d = _dot(_relu2(a_ref.at[:, lo:hi]).astype(BF16), w_ref[lo:hi, :])
            if n == 0:
                acc_ref[...] = d
            else:
                acc_ref[...] += d
        o_ref[...] = h_ref[...] + acc_ref[...]

    return _call(name, body, (T // tm,), [_rows(a, tm), _full(w), _rows(h, tm)], [_out_rows(T, D, F32, tm)],
                 scratch=[pltpu.VMEM((tm, D), F32)], hub=hub)[0]


def _mm_res(name, a, w, h, tm=MM_ROWS):
    T, D = h.shape
    _, w_scratch, prepare, chunks = _col_weight(w)

    def body(a_ref, w_ref, h_ref, o_ref, *s):
        s_ref = s[0] if s else None
        prepare(w_ref, s_ref, pl.program_id(0))
        av = a_ref[...].astype(BF16)
        for lo, hi, load in chunks(w_ref, s_ref):
            o_ref[:, lo:hi] = h_ref[:, lo:hi] + _dot(av, load())

    return _call(name, body, (T // tm,), [_rows(a, tm), _full(w), _rows(h, tm)], [_out_rows(T, D, F32, tm)],
                 scratch=w_scratch)[0]


def _nt_relu2_bwd(name, dh, w, a, tm=MM_ROWS, hub=None):
    T, _ = dh.shape
    K = w.shape[0]

    def body(dh_ref, w_ref, a_ref, o_ref):
        d = dh_ref[...].astype(BF16)
        for lo, hi in _chunks(K):
            dr = _dot_nt(d, w_ref[lo:hi, :])
            o_ref[:, lo:hi] = (dr * (2.0 * jnp.maximum(a_ref[:, lo:hi].astype(F32), 0.0))).astype(BF16)

    return _call(name, body, (T // tm,), [_rows(dh, tm), _full(w), _rows(a, tm)], [_out_rows(T, K, BF16, tm)], hub=hub)[0]


def _concat_bf16(*refs):
    vals = [r[...].astype(BF16) for r in refs]
    return vals[0] if len(vals) == 1 else jnp.concatenate(vals, axis=1)


def _nt_plain(name, dy, w, tm=MM_ROWS):
    T, N = dy.shape
    if w.ndim == 3:
        K = w.shape[1]
        _, w_scratch, prepare, chunks = _col_weight(w)
    else:
        K = w.shape[0]
        w_scratch, prepare = [], (lambda w_ref, s_ref, step: None)
        chunks = lambda w_ref, s_ref: [(lo, hi, (lambda lo=lo, hi=hi: w_ref[:, lo:hi])) for lo, hi in _chunks(N)]

    def body(dy_ref, w_ref, o_ref, acc_ref, *s):
        s_ref = s[0] if s else None
        prepare(w_ref, s_ref, pl.program_id(0))
        for n, (lo, hi, load) in enumerate(chunks(w_ref, s_ref)):
            d = _dot_nt(dy_ref[:, lo:hi].astype(BF16), load())
            if n == 0:
                acc_ref[...] = d
            else:
                acc_ref[...] += d
        o_ref[...] = acc_ref[...]

    return _call(name, body, (T // tm,), [_rows(dy, tm), _full(w)], [_out_rows(T, K, F32, tm)],
                 scratch=[pltpu.VMEM((tm, K), F32)] + w_scratch)[0]


def _att_out_bwd(name, dy, w, o, tm=MM_ROWS):
    T, _ = dy.shape
    K = w.shape[1]
    _, w_scratch, prepare, chunks = _col_weight(w)

    def body(dy_ref, w_ref, o_ref, do_ref, dl_ref, acc_ref, *s):
        s_ref = s[0] if s else None
        prepare(w_ref, s_ref, pl.program_id(0))
        for n, (lo, hi, load) in enumerate(chunks(w_ref, s_ref)):
            d = _dot_nt(dy_ref[:, lo:hi].astype(BF16), load())
            if n == 0:
                acc_ref[...] = d
            else:
                acc_ref[...] += d
        do = acc_ref[...]
        do_ref[...] = do
        prod = do * o_ref[...]
        high = prod.astype(BF16)
        low = (prod - high.astype(F32)).astype(BF16)
        head_of = lambda axis: jnp.right_shift(lax.broadcasted_iota(jnp.int32, (K, K), axis), HEAD_DIM.bit_length() - 1)
        same_head = jnp.where(head_of(0) == head_of(1), 1.0, 0.0).astype(BF16)
        dl_ref[...] = _dot(high, same_head) + _dot(low, same_head)

    outs = [_out_rows(T, K, F32, tm), _out_rows(T, K, F32, tm)]
    return _call(name, body, (T // tm,), [_rows(dy, tm), _full(w), _rows(o, tm)], outs,
                 scratch=[pltpu.VMEM((tm, K), F32)] + w_scratch)


def _nt_norm_bwd(name, dys, w, h, g, dh_in, tm=MM_ROWS, hub=None):
    T, D = h.shape
    _, w_scratch, prepare, chunks = _col_weight(w)
    n_steps = T // tm
    n_dy = len(dys)

    def body(*refs):
        dy_refs = refs[:n_dy]
        w_ref, h_ref, g_ref, dhin_ref, o_ref, dg_ref, acc_ref = refs[n_dy:n_dy + 7]
        s_ref = refs[n_dy + 7] if len(refs) > n_dy + 7 else None
        i = pl.program_id(0)
        prepare(w_ref, s_ref, i)
        dy = _concat_bf16(*dy_refs)
        for n, (lo, hi, load) in enumerate(chunks(w_ref, s_ref)):
            d = _dot_nt(dy[:, lo:hi], load())
            if n == 0:
                acc_ref[...] = d
            else:
                acc_ref[...] += d
        dn = acc_ref[...]
        x = h_ref[...]
        rstd = _rstd(x)
        xhat = x * rstd
        dxhat = dn * g_ref[...]
        dx = rstd * (dxhat - xhat * jnp.mean(dxhat * xhat, axis=-1, keepdims=True))
        o_ref[...] = dhin_ref[...] + dx
        _acc8(dg_ref, dn * xhat, i, n_steps)

    ins = [_rows(d, tm) for d in dys] + [_full(w), _rows(h, tm), _full(g), _rows(dh_in, tm)]
    outs = [_out_rows(T, D, F32, tm), _out_acc8(D)]
    dh, dg = _call(name, body, (n_steps,), ins, outs, scratch=[pltpu.VMEM((tm, D), F32)] + w_scratch, hub=hub)
    return dh, dg[0:1]


def _cols2(a_t, tt, kb=None):
    kb = kb or a_t.shape[0]
    return (a_t, pl.BlockSpec((kb, tt), (lambda s, t: (s, t)) if kb != a_t.shape[0] else (lambda s, t: (0, t))))


def _tn(name, a_ins, a_fn, y_ins, y_fn, K, N, T, tt, split=None, out_cols=None, hub=None):
    kind, parts = split or ("n", 1)
    kb, nb = (K // parts, N) if kind == "k" else (K, N // parts)
    n_steps = T // tt
    n_a = len(a_ins)
    n_y = len(y_ins)
    assert out_cols is None or (kind == "n" and nb % out_cols == 0)

    def body(*refs):
        a_refs = refs[:n_a]
        y_refs = refs[n_a:n_a + n_y]
        o_ref, acc_ref = refs[n_a + n_y:]
        t = pl.program_id(1)
        a_t = a_refs[0][...] if a_fn is None else a_fn(*a_refs).T.astype(BF16)
        y = y_fn(*y_refs).astype(BF16)
        for lo, hi in _chunks(nb):
            d = _dot(a_t, y[:, lo:hi])

            @pl.when(t == 0)
            def _():
                acc_ref[:, lo:hi] = d

            @pl.when(t > 0)
            def _():
                acc_ref[:, lo:hi] += d

        @pl.when(t == n_steps - 1)
        def _():
            if out_cols is None:
                o_ref[...] = acc_ref[...].astype(BF16)
            else:
                for j in range(nb // out_cols):
                    o_ref[j] = acc_ref[:, j * out_cols:(j + 1) * out_cols].astype(BF16)

    if out_cols is None:
        out = (SDS((K, N), BF16), pl.BlockSpec((kb, nb), (lambda s, t: (s, 0)) if kind == "k" else (lambda s, t: (0, s))))
    else:
        out = (SDS((N // out_cols, K, out_cols), BF16), pl.BlockSpec((nb // out_cols, K, out_cols), lambda s, t: (s, 0, 0)))
    return _call(name, body, (parts, n_steps), list(a_ins) + list(y_ins), [out],
                 scratch=[pltpu.VMEM((kb, nb), F32)], hub=hub)[0]


def _val(ref):
    return ref[...]


def _concat_f32(*refs):
    vals = [r[...] for r in refs]
    return vals[0] if len(vals) == 1 else jnp.concatenate(vals, axis=1)


ATT_TILE_ROWS = 2048
HEAD_PAIRS = H_G // 2
ATT_SCALE = HEAD_DIM ** -0.5
ATT_UNITS_TOGETHER = 4


def _slope(h):
    return 2.0 ** (-ALIBI_MAX_BIAS * (h + 1) / H_G)


def _att_geom(T, bl, g):
    dil = PATTERNS[g][1]
    sub = ATT_BLK * dil
    nsub = max(1, ATT_TILE_ROWS // sub)
    rows = sub * nsub
    return dil, sub, nsub, rows, T // bl // rows


def _att_specs(T, bl, g):
    _, sub, nsub, rows, nt = _att_geom(T, bl, g)
    last_sub = T // sub - 1
    tile = lambda col: pl.BlockSpec((rows, 128), lambda b, i, hp: (b * nt + i, col(hp)))
    prev = lambda col: pl.BlockSpec((sub, 128), lambda b, i, hp: (jnp.maximum((b * nt + i) * nsub - 1, 0), col(hp)))
    nxt = lambda col: pl.BlockSpec((sub, 128), lambda b, i, hp: (jnp.minimum((b * nt + i + 1) * nsub, last_sub), col(hp)))
    return tile, prev, nxt


def _sub_rows(j, r, dil):
    start = j * ATT_BLK * dil + r
    return pl.ds(start, ATT_BLK, stride=dil) if dil > 1 else pl.ds(start, ATT_BLK)


class _Residues:
    def __init__(self, dil):
        self.dil = dil
        self.whole = dil % SUBLANES == 0
        self.read, self.written = {}, {}

    def _block(self, j):
        return pl.ds(j * ATT_BLK * self.dil, ATT_BLK * self.dil)

    def load(self, ref, j, r):
        if not self.whole:
            return ref[_sub_rows(j, r, self.dil), :]
        if (id(ref), j) not in self.read:
            rows = ref[self._block(j), :]
            self.read[id(ref), j] = jnp.swapaxes(rows.reshape(ATT_BLK, self.dil, rows.shape[-1]), 0, 1)
        return self.read[id(ref), j][r]

    def store(self, ref, j, r, val):
        if not self.whole:
            ref[_sub_rows(j, r, self.dil), :] = val
            return
        got = self.written.setdefault((id(ref), j), {})
        got[r] = val
        if len(got) == self.dil:
            merged = jnp.swapaxes(jnp.stack([got[k] for k in range(self.dil)], axis=0), 0, 1)
            ref[self._block(j), :] = merged.reshape(ATT_BLK * self.dil, val.shape[-1])
            del self.written[id(ref), j]


def _att_consts(hp, dil):
    h0 = lax.broadcasted_iota(jnp.int32, (ATT_BLK, 128), 1) < HEAD_DIM
    a = lax.broadcasted_iota(jnp.int32, (ATT_BLK, ATT_BLK), 0)
    c = lax.broadcasted_iota(jnp.int32, (ATT_BLK, ATT_BLK), 1)
    dist_p = ((ATT_BLK + a - c) * dil).astype(F32)
    dist_c = ((a - c) * dil).astype(F32)
    bias_p, bias_c = [], []
    for h in range(2):
        slope = jnp.float32(_slope(2 * (HEAD_PAIRS - 1) + h))
        for p in range(HEAD_PAIRS - 2, -1, -1):
            slope = jnp.where(hp == p, jnp.float32(_slope(2 * p + h)), slope)
        bias_p.append(jnp.where(c >= a, -slope * dist_p, NEG_INF))
        bias_c.append(jnp.where(c <= a, -slope * dist_c, NEG_INF))
    return h0, bias_p, bias_c


def _split_heads(x, h0):
    return [jnp.where(h0, x, 0.0).astype(BF16), jnp.where(h0, 0.0, x).astype(BF16)]


def _head_cols(x):
    return [x[:, 0:1], x[:, HEAD_DIM:HEAD_DIM + 1]]


def _in_groups(units, first_stage, *later_stages):
    for u0 in range(0, len(units), ATT_UNITS_TOGETHER):
        staged = [first_stage(*u) for u in units[u0:u0 + ATT_UNITS_TOGETHER]]
        for stage in later_stages:
            staged = [stage(*s) for s in staged]


def _attn_fwd(name, q, kv, g, bl, hub=None):
    T = q.shape[0]
    dil, _, nsub, _, _ = _att_geom(T, bl, g)
    tile, prev, _ = _att_specs(T, bl, g)

    def body(q_ref, kp_ref, kc_ref, vp_ref, vc_ref, o_ref, lse_ref):
        first = pl.program_id(1) == 0
        h0, bias_p, bias_c = _att_consts(pl.program_id(2), dil)
        bias_first = [jnp.where(first, NEG_INF, b) for b in bias_p]
        ones = jnp.ones((ATT_BLK, 128), BF16)
        rows = _Residues(dil)

        def scores(j, r):
            if j == 0:
                kp, vp, bp = rows.load(kp_ref, 0, r), rows.load(vp_ref, 0, r), bias_first
            else:
                kp, vp, bp = rows.load(kc_ref, j - 1, r), rows.load(vc_ref, j - 1, r), bias_p
            kp, kc = kp.astype(BF16), rows.load(kc_ref, j, r).astype(BF16)
            qh = _split_heads(rows.load(q_ref, j, r) * ATT_SCALE, h0)
            sp = [_dot_nt(qh[h], kp) + bp[h] for h in range(2)]
            sc = [_dot_nt(qh[h], kc) + bias_c[h] for h in range(2)]
            return (j, r), sp, sc, vp.astype(BF16), rows.load(vc_ref, j, r).astype(BF16)

        def weights(unit, sp, sc, vp, vc):
            mx = [jnp.max(jnp.maximum(sp[h], sc[h]), axis=-1, keepdims=True) for h in range(2)]
            ep = [jnp.exp(sp[h] - mx[h]).astype(BF16) for h in range(2)]
            ec = [jnp.exp(sc[h] - mx[h]).astype(BF16) for h in range(2)]
            return unit, mx, ep, ec, vp, vc

        def outputs(unit, mx, ep, ec, vp, vc):
            den = [_dot(ep[h], ones) + _dot(ec[h], ones) for h in range(2)]
            acc = [_dot(ep[h], vp) + _dot(ec[h], vc) for h in range(2)]
            rows.store(o_ref, *unit, jnp.where(h0, acc[0] / den[0], acc[1] / den[1]))
            rows.store(lse_ref, *unit, jnp.where(h0, mx[0] + jnp.log(den[0]), mx[1] + jnp.log(den[1])))
            return ()

        _in_groups([(j, r) for j in range(nsub) for r in range(dil)], scores, weights, outputs)

    ins = [(q, tile(lambda hp: 4 * g + hp)), (kv, prev(lambda hp: 8 * g + hp)), (kv, tile(lambda hp: 8 * g + hp)),
           (kv, prev(lambda hp: 8 * g + 4 + hp)), (kv, tile(lambda hp: 8 * g + 4 + hp))]
    out = (SDS((T, QW), F32), tile(lambda hp: hp))
    _, _, _, _, nt = _att_geom(T, bl, g)
    return _call(name, body, (bl, nt, HEAD_PAIRS), ins, [out, out], hub=hub)


def _combine(name, os_, lses, tm=512):
    T = os_[0].shape[0]

    def body(o0, o1, o2, l0, l1, l2, o_ref, lse_ref):
        ls = [l0[...], l1[...], l2[...]]
        mx = jnp.maximum(jnp.maximum(ls[0], ls[1]), ls[2])
        es = [jnp.exp(l - mx) for l in ls]
        den = es[0] + es[1] + es[2]
        o_ref[...] = (es[0] * o0[...] + es[1] * o1[...] + es[2] * o2[...]) / den
        lse_ref[...] = mx + jnp.log(den)

    ins = [_rows(t, tm) for t in list(os_) + list(lses)]
    return _call(name, body, (T // tm,), ins, [_out_rows(T, QW, F32, tm), _out_rows(T, QW, F32, tm)])


def _attn_bwd_dq(name, q, kv, do, delta, lse, g, bl, hub=None):
    T = q.shape[0]
    dil, _, nsub, _, nt = _att_geom(T, bl, g)
    tile, prev, _ = _att_specs(T, bl, g)

    def body(q_ref, kp_ref, kc_ref, vp_ref, vc_ref, do_ref, dl_ref, lse_ref, dq_ref):
        first = pl.program_id(1) == 0
        h0, bias_p, bias_c = _att_consts(pl.program_id(2), dil)
        bias_first = [jnp.where(first, NEG_INF, b) for b in bias_p]

        rows = _Residues(dil)

        def probs(j, r):
            if j == 0:
                kp, vp, bp = rows.load(kp_ref, 0, r), rows.load(vp_ref, 0, r), bias_first
            else:
                kp, vp, bp = rows.load(kc_ref, j - 1, r), rows.load(vc_ref, j - 1, r), bias_p
            kp, vp = kp.astype(BF16), vp.astype(BF16)
            kc, vc = rows.load(kc_ref, j, r).astype(BF16), rows.load(vc_ref, j, r).astype(BF16)
            qh = _split_heads(rows.load(q_ref, j, r) * ATT_SCALE, h0)
            dob = _split_heads(rows.load(do_ref, j, r), h0)
            lse_h = _head_cols(rows.load(lse_ref, j, r))
            pp = [jnp.exp(_dot_nt(qh[h], kp) + bp[h] - lse_h[h]) for h in range(2)]
            pc = [jnp.exp(_dot_nt(qh[h], kc) + bias_c[h] - lse_h[h]) for h in range(2)]
            dpp = [_dot_nt(dob[h], vp) for h in range(2)]
            dpc = [_dot_nt(dob[h], vc) for h in range(2)]
            return (j, r), pp, pc, dpp, dpc, kp, kc

        def dscores(unit, pp, pc, dpp, dpc, kp, kc):
            dl = _head_cols(rows.load(dl_ref, *unit))
            dsp = [(pp[h] * (dpp[h] - dl[h])).astype(BF16) for h in range(2)]
            dsc = [(pc[h] * (dpc[h] - dl[h])).astype(BF16) for h in range(2)]
            return unit, dsp, dsc, kp, kc

        def outputs(unit, dsp, dsc, kp, kc):
            dqh = [_dot(dsp[h], kp) + _dot(dsc[h], kc) for h in range(2)]
            rows.store(dq_ref, *unit, jnp.where(h0, dqh[0], dqh[1]) * ATT_SCALE)
            return ()

        _in_groups([(j, r) for j in range(nsub) for r in range(dil)], probs, dscores, outputs)

    own = lambda hp: hp
    ins = [(q, tile(lambda hp: 4 * g + hp)), (kv, prev(lambda hp: 8 * g + hp)), (kv, tile(lambda hp: 8 * g + hp)),
           (kv, prev(lambda hp: 8 * g + 4 + hp)), (kv, tile(lambda hp: 8 * g + 4 + hp)),
           (do, tile(own)), (delta, tile(own)), (lse, tile(own))]
    return _call(name, body, (bl, nt, HEAD_PAIRS), ins, [(SDS((T, QW), F32), tile(own))], hub=hub)[0]


def _attn_bwd_dkv(name, q, kv, do, delta, lse, g, bl, prev=None, hub=None):
    T = q.shape[0]
    dil, _, nsub, _, nt = _att_geom(T, bl, g)
    tile, _, nxt = _att_specs(T, bl, g)
    has_prev = prev is not None

    def body(*refs):
        k_ref, v_ref, q_ref, qn_ref, do_ref, don_ref, dl_ref, dln_ref, l_ref, ln_ref = refs[:10]
        rest = refs[10:]
        if has_prev:
            dkp_ref, dvp_ref, dk_ref, dv_ref = rest
        else:
            dk_ref, dv_ref = rest
        last = pl.program_id(1) == nt - 1
        h0, bias_p, bias_c = _att_consts(pl.program_id(2), dil)
        bias_last = [jnp.where(last, NEG_INF, b) for b in bias_p]

        rows = _Residues(dil)

        def probs(j, r):
            kb, vb = rows.load(k_ref, j, r).astype(BF16), rows.load(v_ref, j, r).astype(BF16)
            sets = [(q_ref, do_ref, dl_ref, l_ref, j, bias_c)]
            if j < nsub - 1:
                sets.append((q_ref, do_ref, dl_ref, l_ref, j + 1, bias_p))
            else:
                sets.append((qn_ref, don_ref, dln_ref, ln_ref, 0, bias_last))
            out = []
            for qr, dor, dlr, lr, jq, bias in sets:
                qs = rows.load(qr, jq, r) * ATT_SCALE
                do2 = rows.load(dor, jq, r)
                qh = _split_heads(qs, h0)
                dob = _split_heads(do2, h0)
                lse_h = _head_cols(rows.load(lr, jq, r))
                p = [jnp.exp(_dot_nt(qh[h], kb) + bias[h] - lse_h[h]) for h in range(2)]
                dp = [_dot_nt(dob[h], vb) for h in range(2)]
                out.append((p, dp, dlr, jq, qs.astype(BF16), do2.astype(BF16)))
            return (j, r), out

        def dscores(unit, sets):
            out = []
            for p, dp, dlr, jq, qsb, do2b in sets:
                dl = _head_cols(rows.load(dlr, jq, unit[1]))
                ds = [(p[h] * (dp[h] - dl[h])).astype(BF16) for h in range(2)]
                out.append(([p[h].astype(BF16) for h in range(2)], ds, qsb, do2b))
            return unit, out

        def outputs(unit, sets):
            dk = [None, None]
            dv = [None, None]
            for pb, ds, qsb, do2b in sets:
                for h in range(2):
                    dvh = _dot_tn(pb[h], do2b)
                    dkh = _dot_tn(ds[h], qsb)
                    dv[h] = dvh if dv[h] is None else dv[h] + dvh
                    dk[h] = dkh if dk[h] is None else dk[h] + dkh
            dk2 = jnp.where(h0, dk[0], dk[1])
            dv2 = jnp.where(h0, dv[0], dv[1])
            if has_prev:
                dk2 = dk2 + rows.load(dkp_ref, *unit)
                dv2 = dv2 + rows.load(dvp_ref, *unit)
            rows.store(dk_ref, *unit, dk2)
            rows.store(dv_ref, *unit, dv2)
            return ()

        _in_groups([(j, r) for j in range(nsub) for r in range(dil)], probs, dscores, outputs)

    own = lambda hp: hp
    qcol = lambda hp: 4 * g + hp
    ins = [(kv, tile(lambda hp: 8 * g + hp)), (kv, tile(lambda hp: 8 * g + 4 + hp)), (q, tile(qcol)), (q, nxt(qcol)),
           (do, tile(own)), (do, nxt(own)), (delta, tile(own)), (delta, nxt(own)), (lse, tile(own)), (lse, nxt(own))]
    if has_prev:
        ins += [(prev[0], tile(own)), (prev[1], tile(own))]
    out = (SDS((T, QW), F32), tile(own))
    return _call(name, body, (bl, nt, HEAD_PAIRS), ins, [out, out], hub=hub)


def _final_loss(name, h, tgt, g, tm=256):
    T, D = h.shape
    n_steps = T // tm

    def body(h_ref, t_ref, g_ref, dh_ref, loss_ref, dg_ref, sq_ref):
        i = pl.program_id(0)
        x = h_ref[...]
        rstd = _rstd(x)
        xhat = x * rstd
        err = xhat * g_ref[...] - t_ref[...]
        _acc8(sq_ref, err * err, i, n_steps)
        dy = err * (1.0 / D)
        dxhat = dy * g_ref[...]
        dh_ref[...] = rstd * (dxhat - xhat * jnp.mean(dxhat * xhat, axis=-1, keepdims=True))
        _acc8(dg_ref, dy * xhat, i, n_steps)

        @pl.when(i == n_steps - 1)
        def _():
            loss_ref[...] = jnp.full(loss_ref.shape, jnp.sum(sq_ref[0:1, :]), F32)

    outs = [_out_rows(T, D, F32, tm), (SDS((SUBLANES, 128), F32), pl.BlockSpec((SUBLANES, 128), lambda i: (0, 0))),
            _out_acc8(D)]
    dh, loss, dg = _call(name, body, (n_steps,), [_rows(h, tm), _rows(tgt, tm), _full(g)], outs,
                         scratch=[pltpu.VMEM((SUBLANES, D), F32)])
    return dh, loss[0, 0], dg[0:1]


def _conv_bwd(name, bcu, dgated, cw, seq, tm=256, hub=None):
    T, D = dgated.shape
    n_steps = T // tm

    def body(b_ref, c_ref, u_ref, ch_ref, uh_ref, dg_ref, dgn_ref, bn_ref, cw_ref, o_ref, t0_ref, t1_ref, t2_ref):
        i = pl.program_id(0)
        first = (i * tm) % seq == 0
        last = ((i + 1) * tm) % seq == 0
        _, (b, c, u), conv, (cu, cu1, cu2) = _gate(b_ref, c_ref, u_ref, ch_ref, uh_ref, cw_ref, first)
        dgat = dg_ref[...]
        dconv = dgat * b
        nxt = jnp.where(last, 0.0, dgn_ref[...] * bn_ref[...].astype(F32))
        rows = lax.broadcasted_iota(jnp.int32, dconv.shape, 0)
        n1 = nxt[0:1, :]
        n2 = nxt[1:2, :]
        dc1 = jnp.where(rows == tm - 1, n1, pltpu.roll(dconv, tm - 1, 0))
        dc2 = jnp.where(rows == tm - 1, n2, jnp.where(rows == tm - 2, n1, pltpu.roll(dconv, tm - 2, 0)))
        dcu = cw_ref[0:1, :] * dconv + cw_ref[1:2, :] * dc1 + cw_ref[2:3, :] * dc2
        o_ref[:, 0:D] = (dgat * conv).astype(BF16)
        o_ref[:, D:2 * D] = (dcu * u).astype(BF16)
        o_ref[:, 2 * D:3 * D] = (dcu * c).astype(BF16)
        _acc8(t0_ref, dconv * cu, i, n_steps)
        _acc8(t1_ref, dconv * cu1, i, n_steps)
        _acc8(t2_ref, dconv * cu2, i, n_steps)

    ins = [_rows(bcu, tm, D, 0), _rows(bcu, tm, D, 1), _rows(bcu, tm, D, 2), _prev8(bcu, tm, D, 1), _prev8(bcu, tm, D, 2),
           _rows(dgated, tm), _next8(dgated, tm, D, 0), _next8(bcu, tm, D, 0), _full(cw)]
    outs = [_out_rows(T, 3 * D, BF16, tm), _out_acc8(D), _out_acc8(D), _out_acc8(D)]
    dbcu, t0, t1, t2 = _call(name, body, (n_steps,), ins, outs, hub=hub)
    return dbcu, jnp.concatenate([t0[0:1], t1[0:1], t2[0:1]], axis=0)


def _sum8_adamw(name, parts, w, m, v, tr):
    R, C = w.shape
    b1c = 1.0 - ADAM_B1 ** ADAM_STEP
    b2c = 1.0 - ADAM_B2 ** ADAM_STEP

    def body(p_ref, w_ref, m_ref, v_ref, g_ref, d_ref, nm_ref, nv_ref):
        g = p_ref[0].astype(F32)
        for j in range(1, N_DEV):
            g = g + p_ref[j].astype(F32)
        nm = ADAM_B1 * m_ref[...] + (1.0 - ADAM_B1) * g
        nv = ADAM_B2 * v_ref[...] + (1.0 - ADAM_B2) * (g * g)
        m_hat = nm / b1c
        v_hat = nv / b2c
        g_ref[...] = g
        d_ref[...] = -ADAM_LR * (m_hat / (jnp.sqrt(v_hat) + ADAM_EPS) + ADAM_WD * w_ref[...])
        nm_ref[...] = nm
        nv_ref[...] = nv

    ins = [(parts, pl.BlockSpec((N_DEV, tr, C), lambda i: (0, i, 0))), _rows(w, tr), _rows(m, tr), _rows(v, tr)]
    outs = [_out_rows(R, C, F32, tr)] * 4
    return _call(name, body, (R // tr,), ins, outs)


def _all_gather(name, items):
    n = len(items)
    shapes = [tuple(a.shape if idx is None else a.shape[1:]) for a, idx in items]

    def body(*refs):
        x_refs, out_refs = refs[:n], refs[n:2 * n]
        send_sems, recv_sems, local_sems = refs[2 * n:]
        x, y, c = _mesh_pos()
        me, sibling = (x, y, c), (x, y, 1 - c)
        chips = [(1 - x, y), (x, 1 - y), (1 - x, 1 - y)]

        def copy(t, k, block, to, own=False):
            dst = out_refs[t].at[4 * block[0] + 2 * block[1] + block[2]]
            src = dst
            if own:
                src = x_refs[t] if items[t][1] is None else x_refs[t].at[items[t][1]]
            return pltpu.make_async_remote_copy(
                src_ref=src, dst_ref=dst, send_sem=send_sems.at[t, k], recv_sem=recv_sems.at[t, k],
                device_id=to, device_id_type=pl.DeviceIdType.MESH)

        started = []
        for t in range(n):
            src = x_refs[t] if items[t][1] is None else x_refs[t].at[items[t][1]]
            mine = pltpu.make_async_copy(src, out_refs[t].at[4 * x + 2 * y + c], local_sems.at[t])
            mine.start()
            first = [copy(t, 0, me, sibling, own=True)]
            first += [copy(t, 1 + j, me, (*chip, c), own=True) for j, chip in enumerate(chips)]
            for cp in first:
                cp.start()
            started.append((mine, first))
        passed = []
        for t in range(n):
            for j, chip in enumerate(chips):
                copy(t, 1 + j, (*chip, c), me).wait_recv()
                fwd = copy(t, 4 + j, (*chip, c), sibling)
                fwd.start()
                passed.append(fwd)
        for t in range(n):
            copy(t, 0, sibling, me).wait_recv()
            for j, chip in enumerate(chips):
                copy(t, 4 + j, (*chip, 1 - c), me).wait_recv()
        for mine, first in started:
            for cp in first:
                cp.wait_send()
            mine.wait()
        for cp in passed:
            cp.wait_send()

    any_spec = pl.BlockSpec(memory_space=pl.ANY)
    return pl.pallas_call(
        body, name=name,
        out_shape=[SDS((N_DEV,) + s, a.dtype) for s, (a, _) in zip(shapes, items)],
        in_specs=[any_spec] * n,
        out_specs=[any_spec] * n,
        scratch_shapes=[pltpu.SemaphoreType.DMA((n, 7)), pltpu.SemaphoreType.DMA((n, 7)), pltpu.SemaphoreType.DMA((n,))],
    )(*[a for a, _ in items])


def _pad8(t):
    return jnp.pad(t, ((0, SUBLANES - t.shape[0]), (0, 0)))


def _rows_merged(w):
    return w.reshape(w.shape[0] * w.shape[1], w.shape[2])


def _local_grads(x, tgt, norm_mix, norm_mlp, norm_kv, norm_final, conv_w, hub):
    bl, seq, D = x.shape
    T = bl * seq
    h = x.reshape(T, D)
    tgt = tgt.reshape(T, D)
    row = lambda t, l: t[l:l + 1]
    W = hub.weights
    saved = []
    kv = h_kv = hn_kv_t = None
    for l in range(DEPTH):
        if l < N_A_LAYERS:
            bcu, hn_t = _norm_mm(f"l{l}_in", h, row(norm_mix, l), W["w_a_in", l], out_dtype=BF16, transposed=True, hub=hub)
            h2, gated_t = _gate_mm_res(f"l{l}_conv_out", bcu, _pad8(conv_w[l]), _rows_merged(W["w_a_out", l]), h, seq, hub=hub)
            saved.append((h, bcu, gated_t, hn_t))
        else:
            i = l - N_A_LAYERS
            if l == N_A_LAYERS:
                h_kv = h
                kv, hn_kv_t = _norm_mm("kv", h, norm_kv.reshape(1, D), W["w_kv", None], transposed=True, hub=hub)
            q, hn_t = _norm_mm(f"l{l}_q", h, row(norm_mix, l), W["w_q", i], transposed=True)
            per_group = [_attn_fwd(f"l{l}_att{g}", q, kv, g, bl, hub=hub) for g in range(N_GROUPS)]
            o, lse = _combine(f"l{l}_combine", [p[0] for p in per_group], [p[1] for p in per_group])
            h2 = _mm_res(f"l{l}_att_out", o, W["w_o", i], h)
            saved.append((h, q, o, lse, hn_t))
        a = _norm_mm(f"l{l}_up", h2, row(norm_mlp, l), W["w_up", l], out_dtype=BF16, hub=hub)
        h = _relu2_mm_res(f"l{l}_down", a, _rows_merged(W["w_down", l]), h2, hub=hub)
        saved[-1] = saved[-1] + (h2, a)

    dh, sq_err, d_norm_final = _final_loss("loss", h, tgt, norm_final.reshape(1, D))

    d_norm_mix = [None] * DEPTH
    d_norm_mlp = [None] * DEPTH
    d_conv = [None] * N_A_LAYERS
    d_norm_kv = None
    dkv_acc = [None] * N_GROUPS
    G = hub.grads
    as_slots = lambda g: g.reshape(N_DEV, g.shape[0] // N_DEV, g.shape[1])
    tt = DW_TOKENS
    for l in reversed(range(DEPTH)):
        h2, a = saved[l][-2:]
        h_in = saved[l][0]
        g_mlp = row(norm_mlp, l)
        g_mix = row(norm_mix, l)
        w_up_l = W["w_up", l]
        FF = N_DEV * w_up_l.shape[2]
        da = _nt_relu2_bwd(f"l{l}_down_bwd", dh, _rows_merged(W["w_down", l]), a, hub=hub)
        G["w_down", l] = as_slots(_tn(f"l{l}_dw_down", [_rows2(a, tt, FF // 2, lambda s: s)], _relu2,
                                      [_rows2(dh, tt)], _val, FF, D, T, tt, split=("k", 2)))
        G["w_up", l] = _tn(f"l{l}_dw_up", [_rows2(h2, 2 * tt), _full2(g_mlp)], _normed,
                           [_rows2(da, 2 * tt, FF // 4, lambda s: s)], _val, D, FF, T, 2 * tt, split=("n", 4),
                           out_cols=w_up_l.shape[2], hub=hub)
        dh2, d_norm_mlp[l] = _nt_norm_bwd(f"l{l}_up_bwd", [da], w_up_l, h2, g_mlp, dh, hub=hub)
        if l >= N_A_LAYERS:
            i = l - N_A_LAYERS
            _, q, o, lse, hn_t = saved[l][:5]
            w_o_i, w_q_i = W["w_o", i], W["w_q", i]
            do, delta = _att_out_bwd(f"l{l}_att_out_bwd", dh2, w_o_i, o)
            G["w_o", i] = _tn(f"l{l}_dw_o", [_rows2(o, tt)], _val, [_rows2(dh2, tt)], _val, QW, D, T, tt,
                              out_cols=w_o_i.shape[2])
            dqs = []
            for g in range(N_GROUPS):
                dqs.append(_attn_bwd_dq(f"l{l}_att{g}_dq", q, kv, do, delta, lse, g, bl, hub=hub))
                dkv_acc[g] = _attn_bwd_dkv(f"l{l}_att{g}_dkv", q, kv, do, delta, lse, g, bl, prev=dkv_acc[g], hub=hub)
            G["w_q", i] = _tn(f"l{l}_dw_q", [_cols2(hn_t, tt)], None,
                              [_rows2(t, tt) for t in dqs], _concat_f32, D, N_GROUPS * QW, T, tt, out_cols=w_q_i.shape[2])
            dh, d_norm_mix[l] = _nt_norm_bwd(f"l{l}_q_bwd", dqs, w_q_i, h_in, g_mix, dh2, hub=hub)
            if l == N_A_LAYERS:
                dkvs = [t for pair in dkv_acc for t in pair]
                g_kv = norm_kv.reshape(1, D)
                w_kv = W["w_kv", None]
                per_call = len(dkvs) // 2
                halves = [_tn(f"dw_kv{p}", [_cols2(hn_kv_t, tt)], None,
                              [_rows2(t, tt) for t in dkvs[p * per_call:(p + 1) * per_call]], _concat_f32,
                              D, per_call * QW, T, tt, out_cols=w_kv.shape[2]) for p in range(2)]
                G["w_kv", None] = jnp.concatenate(halves, axis=0)
                dh, d_norm_kv = _nt_norm_bwd("kv_bwd", dkvs, w_kv, h_kv, g_kv, dh, tm=MM_ROWS // 2, hub=hub)
        else:
            _, bcu, gated_t, hn_t = saved[l][:4]
            cw = _pad8(conv_w[l])
            w_in_l = W["w_a_in", l]
            dgated = _nt_plain(f"l{l}_conv_out_bwd", dh2, _rows_merged(W["w_a_out", l]))
            G["w_a_out", l] = as_slots(_tn(f"l{l}_dw_conv_out", [_cols2(gated_t, 2 * tt)], None,
                                           [_rows2(dh2, 2 * tt)], _val, D, D, T, 2 * tt, hub=hub))
            dbcu, d_conv[l] = _conv_bwd(f"l{l}_conv_bwd", bcu, dgated, cw, seq, hub=hub)
            G["w_a_in", l] = _tn(f"l{l}_dw_in", [_cols2(hn_t, 2 * tt)], None,
                                 [_rows2(dbcu, 2 * tt, 3 * D // 2, lambda s: s)], _val, D, 3 * D, T, 2 * tt, split=("n", 2),
                                 out_cols=w_in_l.shape[2], hub=hub)
            dh, d_norm_mix[l] = _nt_norm_bwd(f"l{l}_in_bwd", [dbcu], w_in_l, h_in, g_mix, dh2, hub=hub)

    small = jnp.concatenate(d_norm_mix + d_norm_mlp + [d_norm_kv, d_norm_final] + d_conv, axis=0)
    return sq_err, dh.reshape(bl, seq, D), small


def kernel(x, norm_mix, norm_mlp, w_a_in, conv_w, w_a_out, norm_kv, w_kv, w_q, w_o, w_up, w_down, norm_final, loss_target, m_norm_mix, m_norm_mlp, m_w_a_in, m_conv_w, m_w_a_out, m_norm_kv, m_w_kv, m_w_q, m_w_o, m_w_up, m_w_down, m_norm_final, v_norm_mix, v_norm_mlp, v_w_a_in, v_conv_w, v_w_a_out, v_norm_kv, v_w_kv, v_w_q, v_w_o, v_w_up, v_w_down, v_norm_final):
    D = x.shape[-1]
    xi, yi, ci = _mesh_pos()
    me_idx = 4 * xi + 2 * yi + ci
    w_big = dict(w_a_in=w_a_in, w_a_out=w_a_out, w_kv=w_kv, w_q=w_q, w_o=w_o, w_up=w_up, w_down=w_down)
    m_big = dict(w_a_in=m_w_a_in, w_a_out=m_w_a_out, w_kv=m_w_kv, w_q=m_w_q, w_o=m_w_o, w_up=m_w_up, w_down=m_w_down)
    v_big = dict(w_a_in=v_w_a_in, w_a_out=v_w_a_out, w_kv=v_w_kv, w_q=v_w_q, w_o=v_w_o, w_up=v_w_up, w_down=v_w_down)
    names = list(w_big)

    shards = {n: w.astype(BF16) for n, w in w_big.items()}
    landing = {n: lax.empty((N_DEV,) + w.shape, BF16) for n, w in w_big.items()}
    hub = _Hub(FETCH_DURING, PUSH_DURING, shards, landing)
    dc = conv_w.shape[-1]
    taps = conv_w.shape[0] * conv_w.shape[1]
    got = _all_gather("gather_first", [(shards[n], l) for n, l in FETCH_UP_FRONT] + [(_pad8(conv_w.reshape(taps, dc)), None)])
    for key, w in zip(FETCH_UP_FRONT, got):
        hub.weights[key] = w
    conv_full = jnp.moveaxis(got[-1][:, :taps], 0, 1).reshape(conv_w.shape[0], conv_w.shape[1], N_DEV * dc)

    sq_err, grad_x, small = _local_grads(x, loss_target, norm_mix, norm_mlp, norm_kv, norm_final, conv_full, hub)
    loss = lax.psum(sq_err * (0.5 / D), ("x", "y", "c"))

    grads, deltas, new_m, new_v = {}, {}, {}, {}
    for n in names:
        shape = w_big[n].shape
        cols = shape[-1]
        flat = lambda t: t.reshape(-1, cols)
        parts = hub.landing[n].reshape(N_DEV, -1, cols)
        outs = _sum8_adamw(f"adamw_{n}", parts, flat(w_big[n]), flat(m_big[n]), flat(v_big[n]), tr=min(256, parts.shape[1]))
        grads[n], deltas[n], new_m[n], new_v[n] = (t.reshape(shape) for t in outs)

    n_gain = 2 * DEPTH + 2
    rows_small = small.shape[0]
    small_all = _all_gather("gather_small_grads", [(small, None)])[0]

    def small_pack(nm, nl, nk, nf, cw):
        gains = jnp.concatenate([nm, nl, nk.reshape(1, D), nf.reshape(1, D)], axis=0)
        taps_full = lax.dynamic_update_slice(jnp.zeros((taps, D), F32), cw.reshape(taps, dc), (0, me_idx * dc))
        return jnp.concatenate([gains, taps_full], axis=0)

    sp = [small_pack(*t) for t in ((norm_mix, norm_mlp, norm_kv, norm_final, conv_w),
                                   (m_norm_mix, m_norm_mlp, m_norm_kv, m_norm_final, m_conv_w),
                                   (v_norm_mix, v_norm_mlp, v_norm_kv, v_norm_final, v_conv_w))]
    small_out = _sum8_adamw("adamw_small", small_all, *sp, tr=rows_small)

    def small_unpack(t):
        res = dict(norm_mix=t[0:DEPTH], norm_mlp=t[DEPTH:2 * DEPTH], norm_kv=t[2 * DEPTH], norm_final=t[2 * DEPTH + 1])
        res["conv_w"] = lax.dynamic_slice(t[n_gain:], (0, me_idx * dc), (taps, dc)).reshape(conv_w.shape)
        return res

    for dst, t in zip((grads, deltas, new_m, new_v), small_out):
        dst.update(small_unpack(t))

    order = ["norm_mix", "norm_mlp", "w_a_in", "conv_w", "w_a_out", "norm_kv", "w_kv", "w_q", "w_o", "w_up", "w_down",
             "norm_final"]
    return (loss, grad_x, *[grads[n] for n in order], *[deltas[n] for n in order], *[new_m[n] for n in order],
            *[new_v[n] for n in order])
```

```python
import functools

import jax
import jax.numpy as jnp
from jax import lax
from jax.experimental import pallas as pl
from jax.experimental.pallas import tpu as pltpu

F32 = jnp.float32
BF16 = jnp.bfloat16
SDS = jax.ShapeDtypeStruct

EPS = 1e-5
N_A_LAYERS = 2
DEPTH = 4
PATTERNS = ((128, 1), (512, 4), (2048, 16))
N_GROUPS = 3
H_G = 8
HEAD_DIM = 64
QW = H_G * HEAD_DIM
ATT_BLK = 128
ALIBI_MAX_BIAS = 8.0
NEG_INF = -1e30

ADAM_LR = 0.001
ADAM_B1 = 0.9
ADAM_B2 = 0.999
ADAM_EPS = 1e-08
ADAM_WD = 0.01
ADAM_STEP = 10

N_DEV = 8
SUBLANES = 8
HALO = 16
V7X_VMEM_LIMIT = 56 * 1024 * 1024
MXU_COLS = 256
MM_CHUNK = 512
MM_ROWS = 1024
DW_TOKENS = 1024

FETCH_UP_FRONT = [("w_a_in", 0), ("w_a_out", 0)]
FETCH_DURING = {
    "l0_in": [("w_up", 0)], "l0_conv_out": [("w_down", 0)], "l0_up": [("w_a_in", 1), ("w_a_out", 1)], "l0_down": [("w_up", 1)],
    "l1_in": [("w_down", 1)], "l1_conv_out": [("w_kv", None)],
    "l1_up": [("w_q", 0), ("w_o", 0), ("w_q", 1), ("w_o", 1)], "l1_down": [("w_up", 2)],
    "kv": [("w_down", 2)], "l2_att0": [("w_up", 3)], "l2_att1": [("w_down", 3)],
}
PUSH_DURING = {
    "l3_dw_up": [("w_down", 3, 0, 2)], "l3_up_bwd": [("w_down", 3, 1, 2)],
    "l3_att0_dq": [("w_up", 3, 0, 2)], "l3_att0_dkv": [("w_up", 3, 1, 2)], "l3_att1_dq": [("w_o", 1)], "l3_q_bwd": [("w_q", 1)],
    "l2_dw_up": [("w_down", 2, 0, 2)], "l2_up_bwd": [("w_down", 2, 1, 2)],
    "l2_att0_dq": [("w_up", 2, 0, 2)], "l2_att0_dkv": [("w_up", 2, 1, 2)], "l2_att1_dq": [("w_o", 0)], "l2_q_bwd": [("w_q", 0)],
    "kv_bwd": [("w_kv", None, 0, 2)], "l1_down_bwd": [("w_kv", None, 1, 2)],
    "l1_dw_up": [("w_down", 1, 0, 2)], "l1_up_bwd": [("w_down", 1, 1, 2)], "l1_conv_bwd": [("w_up", 1, 0, 2)],
    "l1_dw_in": [("w_up", 1, 1, 2), ("w_a_out", 1)], "l1_in_bwd": [("w_a_in", 1, 0, 2)], "l0_down_bwd": [("w_a_in", 1, 1, 2)],
    "l0_dw_up": [("w_down", 0, 0, 2)], "l0_up_bwd": [("w_down", 0, 1, 2)], "l0_conv_bwd": [("w_up", 0, 0, 2)],
    "l0_dw_in": [("w_up", 0, 1, 2), ("w_a_out", 0)], "l0_in_bwd": [("w_a_in", 0)],
}


def _mesh_pos():
    return lax.axis_index("x"), lax.axis_index("y"), lax.axis_index("c")


def _flip(v, bit):
    return 1 - v if bit else v


class _Transfer:
    def __init__(self, kind, key, src, src_idx=None, dst=None, dst_idx=None, dst_shape=None, rows=None):
        self.kind, self.key, self.src, self.src_idx = kind, key, src, src_idx
        self.dst, self.dst_idx, self.dst_shape, self.rows = dst, dst_idx, dst_shape, rows

    def copies(self, src_ref, dst_ref, send_sems, recv_sems, local_sem):
        x, y, c = _mesh_pos()
        me = 4 * x + 2 * y + c
        part = (lambda r: r) if self.rows is None else (lambda r: r.at[pl.ds(*self.rows)])

        def dst_slot(j):
            r = dst_ref.at[j]
            return part(r if self.dst_idx is None else r.at[self.dst_idx])

        def copy(k, src, dst_j, to):
            return pltpu.make_async_remote_copy(
                src_ref=src, dst_ref=dst_slot(dst_j), send_sem=send_sems.at[k], recv_sem=recv_sems.at[k],
                device_id=to, device_id_type=pl.DeviceIdType.MESH)

        if self.kind == "exchange":
            local = pltpu.make_async_copy(part(src_ref.at[me]), dst_slot(me), local_sem)
            sends, arrivals = [], []
            for k in range(1, N_DEV):
                peer = (_flip(x, k & 4), _flip(y, k & 2), _flip(c, k & 1))
                peer_idx = 4 * peer[0] + 2 * peer[1] + peer[2]
                sends.append(copy(k - 1, part(src_ref.at[peer_idx]), me, peer))
                arrivals.append(copy(k - 1, part(src_ref.at[peer_idx]), peer_idx, peer))
            return local, sends, [], arrivals

        own = part(src_ref if self.src_idx is None else src_ref.at[self.src_idx])
        idx = lambda px, py, pc: 4 * px + 2 * py + pc
        sibling = (x, y, 1 - c)
        chips = [(1 - x, y), (x, 1 - y), (1 - x, 1 - y)]
        local = pltpu.make_async_copy(own, dst_slot(me), local_sem)
        sends = [copy(0, own, me, sibling)] + [copy(1 + j, own, me, (*chip, c)) for j, chip in enumerate(chips)]
        relays = [(copy(1 + j, own, idx(*chip, c), sibling), copy(4 + j, dst_slot(idx(*chip, c)), idx(*chip, c), sibling))
                  for j, chip in enumerate(chips)]
        arrivals = [copy(0, own, idx(*sibling), sibling)]
        arrivals += [copy(4 + j, own, idx(*chip, 1 - c), sibling) for j, chip in enumerate(chips)]
        return local, sends, relays, arrivals


class _Hub:
    def __init__(self, fetch, push, shards, landing):
        self.fetch, self.push, self.shards, self.landing = fetch, push, shards, landing
        self.weights = {}
        self.arriving = {}
        self.grads = {}

    def transfers(self, host):
        out = []
        for name, l, *part in self.fetch.get(host, ()):
            src = self.shards[name]
            shard = tuple(src.shape if l is None else src.shape[1:])
            p, n = part or (0, 1)
            rows = None if n == 1 else (p * (shard[0] // n), shard[0] // n)
            out.append(_Transfer("gather", (name, l, p == n - 1), src, src_idx=l, dst=self.arriving.get((name, l)),
                                 dst_shape=(N_DEV,) + shard, rows=rows))
        for name, l, *part in self.push.get(host, ()):
            src = self.grads[name, l]
            p, n = part or (0, 1)
            rows = None if n == 1 else (p * (src.shape[1] // n), src.shape[1] // n)
            out.append(_Transfer("exchange", (name, l, p == n - 1), src, dst=self.landing[name], dst_idx=l, rows=rows))
        return out

    def accept(self, transfers, results):
        for t, r in zip(transfers, results):
            name, l, complete = t.key
            if t.kind == "exchange":
                self.landing[name] = r
            elif complete:
                self.weights[name, l] = r
            else:
                self.arriving[name, l] = r


def _call(name, body, grid, ins, outs, scratch=(), hub=None):
    transfers = hub.transfers(name) if hub is not None else []
    n_in, n_out, n_scr, n_tr = len(ins), len(outs), len(scratch), len(transfers)
    c_in, c_out, aliases, places = [], [], {}, []
    for t in transfers:
        c_in.append(t.src)
        src_pos = len(c_in) - 1
        if t.dst is not None:
            c_in.append(t.dst)
            aliases[n_in + len(c_in) - 1] = n_out + len(c_out)
            c_out.append(SDS(t.dst.shape, t.dst.dtype))
        else:
            c_out.append(SDS(t.dst_shape, t.src.dtype))
        places.append((src_pos, len(c_out) - 1))
    sems = [pltpu.SemaphoreType.DMA((n_tr, N_DEV - 1)), pltpu.SemaphoreType.DMA((n_tr, N_DEV - 1)),
            pltpu.SemaphoreType.DMA((n_tr,))] if n_tr else []

    def wrapped(*refs):
        in_refs = refs[:n_in]
        cin_refs = refs[n_in:n_in + len(c_in)]
        o0 = n_in + len(c_in)
        out_refs = refs[o0:o0 + n_out]
        cout_refs = refs[o0 + n_out:o0 + n_out + len(c_out)]
        s0 = o0 + n_out + len(c_out)
        scr_refs = refs[s0:s0 + n_scr]
        if n_tr:
            send_sems, recv_sems, local_sems = refs[s0 + n_scr:]
            first = last = relay = None
            for ax, n in enumerate(grid):
                i = pl.program_id(ax)
                at_relay = (i == max(n - 2, 0)) if ax == len(grid) - 1 else (i == n - 1)
                first = (i == 0) if first is None else first & (i == 0)
                last = (i == n - 1) if last is None else last & (i == n - 1)
                relay = at_relay if relay is None else relay & at_relay

            def all_copies():
                return [t.copies(cin_refs[sp], cout_refs[dp], send_sems.at[n], recv_sems.at[n], local_sems.at[n])
                        for n, (t, (sp, dp)) in enumerate(zip(transfers, places))]

            @pl.when(first)
            def _():
                for local, sends, _, _ in all_copies():
                    local.start()
                    for cp in sends:
                        cp.start()

            def pass_on():
                @pl.when(relay)
                def _():
                    for _, _, relays, _ in all_copies():
                        for arrival, onward in relays:
                            arrival.wait_recv()
                            onward.start()

            if grid[-1] > 1:
                pass_on()

        body(*in_refs, *out_refs, *scr_refs)

        if n_tr:
            if grid[-1] == 1:
                pass_on()

            @pl.when(last)
            def _():
                for local, sends, relays, arrivals in all_copies():
                    for cp in arrivals:
                        cp.wait_recv()
                    for cp in sends + [onward for _, onward in relays]:
                        cp.wait_send()
                    local.wait()

    any_spec = pl.BlockSpec(memory_space=pl.ANY)
    res = pl.pallas_call(
        wrapped,
        name=name,
        grid=grid,
        in_specs=[s for _, s in ins] + [any_spec] * len(c_in),
        out_specs=[s for _, s in outs] + [any_spec] * len(c_out),
        out_shape=[o for o, _ in outs] + c_out,
        scratch_shapes=list(scratch) + sems,
        input_output_aliases=aliases,
        compiler_params=pltpu.CompilerParams(
            dimension_semantics=("arbitrary",) * len(grid), vmem_limit_bytes=V7X_VMEM_LIMIT),
    )(*[a for a, _ in ins], *c_in)
    if n_tr:
        hub.accept(transfers, res[n_out:])
    return res[:n_out]


def _rows(a, tm, cb=None, col=0):
    cb = cb or a.shape[1]
    return (a, pl.BlockSpec((tm, cb), lambda i: (i, col)))


def _full(a):
    nd = a.ndim
    return (a, pl.BlockSpec(a.shape, lambda i: (0,) * nd, pipeline_mode=pl.Buffered(1)))


def _prev8(a, tm, cb, col):
    return (a, pl.BlockSpec((HALO, cb), lambda i: (jnp.maximum(i * (tm // HALO) - 1, 0), col)))


def _next8(a, tm, cb, col):
    last = a.shape[0] // HALO - 1
    return (a, pl.BlockSpec((HALO, cb), lambda i: (jnp.minimum((i + 1) * (tm // HALO), last), col)))


def _rows2(a, tt, cb=None, colfn=None):
    cb = cb or a.shape[1]
    colfn = colfn or (lambda s: 0)
    return (a, pl.BlockSpec((tt, cb), lambda s, t: (t, colfn(s))))


def _full2(a):
    nd = a.ndim
    return (a, pl.BlockSpec(a.shape, lambda s, t: (0,) * nd))


def _prev8_2(a, tt, cb, col):
    return (a, pl.BlockSpec((HALO, cb), lambda s, t: (jnp.maximum(t * (tt // HALO) - 1, 0), col)))


def _out_rows(T, n, dtype, tm):
    return (SDS((T, n), dtype), pl.BlockSpec((tm, n), lambda i: (i, 0)))


def _out_acc8(d):
    return (SDS((SUBLANES, d), F32), pl.BlockSpec((SUBLANES, d), lambda i: (0, 0)))


def _rstd(x):
    return lax.rsqrt(jnp.mean(x * x, axis=-1, keepdims=True) + EPS)


def _normed(h_ref, g_ref):
    x = h_ref[...]
    return x * _rstd(x) * g_ref[...]


def _acc8(ref, val, i, n):
    part = val.reshape(-1, SUBLANES, val.shape[-1]).sum(axis=0)

    @pl.when(i == 0)
    def _():
        ref[...] = part

    @pl.when(i > 0)
    def _():
        ref[...] += part

    @pl.when(i == n - 1)
    def _():
        ref[...] = jnp.broadcast_to(jnp.sum(ref[...], axis=0, keepdims=True), ref.shape)


def _gate(b_ref, c_ref, u_ref, ch_ref, uh_ref, cw_ref, first):
    b, c, u = (r[...].astype(F32) for r in (b_ref, c_ref, u_ref))
    cu = c * u
    halo = jnp.where(first, 0.0, ch_ref[...].astype(F32) * uh_ref[...].astype(F32))
    rows = lax.broadcasted_iota(jnp.int32, cu.shape, 0)
    h1 = halo[HALO - 1:HALO, :]
    h2 = halo[HALO - 2:HALO - 1, :]
    cu1 = jnp.where(rows == 0, h1, pltpu.roll(cu, 1, 0))
    cu2 = jnp.where(rows == 0, h2, jnp.where(rows == 1, h1, pltpu.roll(cu, 2, 0)))
    conv = cw_ref[0:1, :] * cu + cw_ref[1:2, :] * cu1 + cw_ref[2:3, :] * cu2
    return b * conv, (b, c, u), conv, (cu, cu1, cu2)


def _relu2(a_ref):
    r = jnp.maximum(a_ref[...].astype(F32), 0.0)
    return r * r


def _dot(a, b):
    return jnp.dot(a, b, preferred_element_type=F32)


def _dot_nt(a, b):
    return lax.dot_general(a, b, (((1,), (1,)), ((), ())), preferred_element_type=F32)


def _dot_tn(a, b):
    return lax.dot_general(a, b, (((0,), (0,)), ((), ())), preferred_element_type=F32)


def _chunks(n):
    c = min(MM_CHUNK, n)
    while n % c:
        c -= 128
    assert c > 0, n
    return [(k * c, (k + 1) * c) for k in range(n // c)]


def _col_weight(w):
    _, K, ns = w.shape
    N = N_DEV * ns
    direct = ns % MXU_COLS == 0
    scratch = [] if direct else [pltpu.VMEM((K, N), BF16)]

    def prepare(w_ref, s_ref, step):
        if direct:
            return

        @pl.when(step == 0)
        def _():
            for j in range(N_DEV):
                s_ref[:, j * ns:(j + 1) * ns] = w_ref[j]

    def chunks(w_ref, s_ref):
        if direct:
            return [(j * ns, (j + 1) * ns, (lambda j=j: w_ref[j])) for j in range(N_DEV)]
        return [(lo, hi, (lambda lo=lo, hi=hi: s_ref[:, lo:hi])) for lo, hi in _chunks(N)]

    return N, scratch, prepare, chunks


def _out_cols(n, T, tm):
    return (SDS((n, T), BF16), pl.BlockSpec((n, tm), lambda i: (0, i)))


def _norm_mm(name, h, g, w, tm=MM_ROWS, out_dtype=F32, transposed=False, hub=None):
    T, D = h.shape
    N, w_scratch, prepare, chunks = _col_weight(w)

    def body(h_ref, g_ref, w_ref, o_ref, *rest):
        at_ref, s = (rest[0], rest[1:]) if transposed else (None, rest)
        s_ref = s[0] if s else None
        prepare(w_ref, s_ref, pl.program_id(0))
        a32 = _normed(h_ref, g_ref)
        a = a32.astype(BF16)
        for lo, hi, load in chunks(w_ref, s_ref):
            o_ref[:, lo:hi] = _dot(a, load()).astype(out_dtype)
        if transposed:
            at_ref[...] = a32.T.astype(BF16)

    outs = [_out_rows(T, N, out_dtype, tm)] + ([_out_cols(D, T, tm)] if transposed else [])
    res = _call(name, body, (T // tm,), [_rows(h, tm), _full(g), _full(w)], outs, scratch=w_scratch, hub=hub)
    return res if transposed else res[0]


def _gate_mm_res(name, bcu, cw, w, h, seq, tm=MM_ROWS, hub=None):
    T, D = h.shape

    def body(b_ref, c_ref, u_ref, ch_ref, uh_ref, cw_ref, w_ref, h_ref, o_ref, gt_ref):
        first = (pl.program_id(0) * tm) % seq == 0
        gated32 = _gate(b_ref, c_ref, u_ref, ch_ref, uh_ref, cw_ref, first)[0]
        gated = gated32.astype(BF16)
        for lo, hi in _chunks(D):
            o_ref[:, lo:hi] = h_ref[:, lo:hi] + _dot(gated, w_ref[:, lo:hi])
        gt_ref[...] = gated32.T.astype(BF16)

    ins = [_rows(bcu, tm, D, 0), _rows(bcu, tm, D, 1), _rows(bcu, tm, D, 2), _prev8(bcu, tm, D, 1),
           _prev8(bcu, tm, D, 2), _full(cw), _full(w), _rows(h, tm)]
    return _call(name, body, (T // tm,), ins, [_out_rows(T, D, F32, tm), _out_cols(D, T, tm)], hub=hub)


def _relu2_mm_res(name, a, w, h, tm=MM_ROWS, hub=None):
    T, D = h.shape
    K = a.shape[1]

    def body(a_ref, w_ref, h_ref, o_ref, acc_ref):
        for n, (lo, hi) in enumerate(_chunks(K)):
            d = _dot(_relu2(a_ref.at[:, lo:hi]).astype(BF16), w_ref[lo:hi, :])
            if n == 0:
                acc_ref[...] = d
            else:
                acc_ref[...] += d
        o_ref[...] = h_ref[...] + acc_ref[...]

    return _call(name, body, (T // tm,), [_rows(a, tm), _full(w), _rows(h, tm)], [_out_rows(T, D, F32, tm)],
                 scratch=[pltpu.VMEM((tm, D), F32)], hub=hub)[0]


def _mm_res(name, a, w, h, tm=MM_ROWS):
    T, D = h.shape
    _, w_scratch, prepare, chunks = _col_weight(w)

    def body(a_ref, w_ref, h_ref, o_ref, *s):
        s_ref = s[0] if s else None
        prepare(w_ref, s_ref, pl.program_id(0))
        av = a_ref[...].astype(BF16)
        for lo, hi, load in chunks(w_ref, s_ref):
            o_ref[:, lo:hi] = h_ref[:, lo:hi] + _dot(av, load())

    return _call(name, body, (T // tm,), [_rows(a, tm), _full(w), _rows(h, tm)], [_out_rows(T, D, F32, tm)],
                 scratch=w_scratch)[0]


def _nt_relu2_bwd(name, dh, w, a, tm=MM_ROWS, hub=None):
    T, _ = dh.shape
    K = w.shape[0]

    def body(dh_ref, w_ref, a_ref, o_ref):
        d = dh_ref[...].astype(BF16)
        for lo, hi in _chunks(K):
            dr = _dot_nt(d, w_ref[lo:hi, :])
            o_ref[:, lo:hi] = (dr * (2.0 * jnp.maximum(a_ref[:, lo:hi].astype(F32), 0.0))).astype(BF16)

    return _call(name, body, (T // tm,), [_rows(dh, tm), _full(w), _rows(a, tm)], [_out_rows(T, K, BF16, tm)], hub=hub)[0]


def _concat_bf16(*refs):
    vals = [r[...].astype(BF16) for r in refs]
    return vals[0] if len(vals) == 1 else jnp.concatenate(vals, axis=1)


def _nt_plain(name, dy, w, tm=MM_ROWS):
    T, N = dy.shape
    if w.ndim == 3:
        K = w.shape[1]
        _, w_scratch, prepare, chunks = _col_weight(w)
    else:
        K = w.shape[0]
        w_scratch, prepare = [], (lambda w_ref, s_ref, step: None)
        chunks = lambda w_ref, s_ref: [(lo, hi, (lambda lo=lo, hi=hi: w_ref[:, lo:hi])) for lo, hi in _chunks(N)]

    def body(dy_ref, w_ref, o_ref, acc_ref, *s):
        s_ref = s[0] if s else None
        prepare(w_ref, s_ref, pl.program_id(0))
        for n, (lo, hi, load) in enumerate(chunks(w_ref, s_ref)):
            d = _dot_nt(dy_ref[:, lo:hi].astype(BF16), load())
            if n == 0:
                acc_ref[...] = d
            else:
                acc_ref[...] += d
        o_ref[...] = acc_ref[...]

    return _call(name, body, (T // tm,), [_rows(dy, tm), _full(w)], [_out_rows(T, K, F32, tm)],
                 scratch=[pltpu.VMEM((tm, K), F32)] + w_scratch)[0]


def _att_out_bwd(name, dy, w, o, tm=MM_ROWS):
    T, _ = dy.shape
    K = w.shape[1]
    _, w_scratch, prepare, chunks = _col_weight(w)

    def body(dy_ref, w_ref, o_ref, do_ref, dl_ref, acc_ref, *s):
        s_ref = s[0] if s else None
        prepare(w_ref, s_ref, pl.program_id(0))
        for n, (lo, hi, load) in enumerate(chunks(w_ref, s_ref)):
            d = _dot_nt(dy_ref[:, lo:hi].astype(BF16), load())
            if n == 0:
                acc_ref[...] = d
            else:
                acc_ref[...] += d
        do = acc_ref[...]
        do_ref[...] = do
        prod = do * o_ref[...]
        high = prod.astype(BF16)
        low = (prod - high.astype(F32)).astype(BF16)
        head_of = lambda axis: jnp.right_shift(lax.broadcasted_iota(jnp.int32, (K, K), axis), HEAD_DIM.bit_length() - 1)
        same_head = jnp.where(head_of(0) == head_of(1), 1.0, 0.0).astype(BF16)
        dl_ref[...] = _dot(high, same_head) + _dot(low, same_head)

    outs = [_out_rows(T, K, F32, tm), _out_rows(T, K, F32, tm)]
    return _call(name, body, (T // tm,), [_rows(dy, tm), _full(w), _rows(o, tm)], outs,
                 scratch=[pltpu.VMEM((tm, K), F32)] + w_scratch)


def _nt_norm_bwd(name, dys, w, h, g, dh_in, tm=MM_ROWS // 2, hub=None):
    T, D = h.shape
    _, w_scratch, prepare, chunks = _col_weight(w)
    n_steps = T // tm
    n_dy = len(dys)

    def body(*refs):
        dy_refs = refs[:n_dy]
        w_ref, h_ref, g_ref, dhin_ref, o_ref, dg_ref, acc_ref = refs[n_dy:n_dy + 7]
        s_ref = refs[n_dy + 7] if len(refs) > n_dy + 7 else None
        i = pl.program_id(0)
        prepare(w_ref, s_ref, i)
        dy = _concat_bf16(*dy_refs)
        for n, (lo, hi, load) in enumerate(chunks(w_ref, s_ref)):
            d = _dot_nt(dy[:, lo:hi], load())
            if n == 0:
                acc_ref[...] = d
            else:
                acc_ref[...] += d
        dn = acc_ref[...]
        x = h_ref[...]
        rstd = _rstd(x)
        xhat = x * rstd
        dxhat = dn * g_ref[...]
        dx = rstd * (dxhat - xhat * jnp.mean(dxhat * xhat, axis=-1, keepdims=True))
        o_ref[...] = dhin_ref[...] + dx
        _acc8(dg_ref, dn * xhat, i, n_steps)

    ins = [_rows(d, tm) for d in dys] + [_full(w), _rows(h, tm), _full(g), _rows(dh_in, tm)]
    outs = [_out_rows(T, D, F32, tm), _out_acc8(D)]
    dh, dg = _call(name, body, (n_steps,), ins, outs, scratch=[pltpu.VMEM((tm, D), F32)] + w_scratch, hub=hub)
    return dh, dg[0:1]


def _cols2(a_t, tt, kb=None):
    kb = kb or a_t.shape[0]
    return (a_t, pl.BlockSpec((kb, tt), (lambda s, t: (s, t)) if kb != a_t.shape[0] else (lambda s, t: (0, t))))


def _tn(name, a_ins, a_fn, y_ins, y_fn, K, N, T, tt, split=None, out_cols=None, hub=None):
    kind, parts = split or ("n", 1)
    kb, nb = (K // parts, N) if kind == "k" else (K, N // parts)
    n_steps = T // tt
    n_a = len(a_ins)
    n_y = len(y_ins)
    assert out_cols is None or (kind == "n" and nb % out_cols == 0)

    def body(*refs):
        a_refs = refs[:n_a]
        y_refs = refs[n_a:n_a + n_y]
        o_ref, acc_ref = refs[n_a + n_y:]
        t = pl.program_id(1)
        a_t = a_refs[0][...] if a_fn is None else a_fn(*a_refs).T.astype(BF16)
        y = y_fn(*y_refs).astype(BF16)
        for lo, hi in _chunks(nb):
            d = _dot(a_t, y[:, lo:hi])

            @pl.when(t == 0)
            def _():
                acc_ref[:, lo:hi] = d

            @pl.when(t > 0)
            def _():
                acc_ref[:, lo:hi] += d

        @pl.when(t == n_steps - 1)
        def _():
            if out_cols is None:
                o_ref[...] = acc_ref[...].astype(BF16)
            else:
                for j in range(nb // out_cols):
                    o_ref[j] = acc_ref[:, j * out_cols:(j + 1) * out_cols].astype(BF16)

    if out_cols is None:
        out = (SDS((K, N), BF16), pl.BlockSpec((kb, nb), (lambda s, t: (s, 0)) if kind == "k" else (lambda s, t: (0, s))))
    else:
        out = (SDS((N // out_cols, K, out_cols), BF16), pl.BlockSpec((nb // out_cols, K, out_cols), lambda s, t: (s, 0, 0)))
    return _call(name, body, (parts, n_steps), list(a_ins) + list(y_ins), [out],
                 scratch=[pltpu.VMEM((kb, nb), F32)], hub=hub)[0]


def _val(ref):
    return ref[...]


def _concat_f32(*refs):
    vals = [r[...] for r in refs]
    return vals[0] if len(vals) == 1 else jnp.concatenate(vals, axis=1)


ATT_TILE_ROWS = 2048
HEAD_PAIRS = H_G // 2
ATT_SCALE = HEAD_DIM ** -0.5
ATT_UNITS_TOGETHER = 4


def _slope(h):
    return 2.0 ** (-ALIBI_MAX_BIAS * (h + 1) / H_G)


def _att_geom(T, bl, g):
    dil = PATTERNS[g][1]
    sub = ATT_BLK * dil
    nsub = max(1, ATT_TILE_ROWS // sub)
    rows = sub * nsub
    return dil, sub, nsub, rows, T // bl // rows


def _att_specs(T, bl, g):
    _, sub, nsub, rows, nt = _att_geom(T, bl, g)
    last_sub = T // sub - 1
    tile = lambda col: pl.BlockSpec((rows, 128), lambda b, i, hp: (b * nt + i, col(hp)))
    prev = lambda col: pl.BlockSpec((sub, 128), lambda b, i, hp: (jnp.maximum((b * nt + i) * nsub - 1, 0), col(hp)))
    nxt = lambda col: pl.BlockSpec((sub, 128), lambda b, i, hp: (jnp.minimum((b * nt + i + 1) * nsub, last_sub), col(hp)))
    return tile, prev, nxt


def _sub_rows(j, r, dil):
    start = j * ATT_BLK * dil + r
    return pl.ds(start, ATT_BLK, stride=dil) if dil > 1 else pl.ds(start, ATT_BLK)


class _Residues:
    def __init__(self, dil):
        self.dil = dil
        self.whole = dil % SUBLANES == 0
        self.read, self.written = {}, {}

    def _block(self, j):
        return pl.ds(j * ATT_BLK * self.dil, ATT_BLK * self.dil)

    def load(self, ref, j, r):
        if not self.whole:
            return ref[_sub_rows(j, r, self.dil), :]
        if (id(ref), j) not in self.read:
            rows = ref[self._block(j), :]
            self.read[id(ref), j] = jnp.swapaxes(rows.reshape(ATT_BLK, self.dil, rows.shape[-1]), 0, 1)
        return self.read[id(ref), j][r]

    def store(self, ref, j, r, val):
        if not self.whole:
            ref[_sub_rows(j, r, self.dil), :] = val
            return
        got = self.written.setdefault((id(ref), j), {})
        got[r] = val
        if len(got) == self.dil:
            merged = jnp.swapaxes(jnp.stack([got[k] for k in range(self.dil)], axis=0), 0, 1)
            ref[self._block(j), :] = merged.reshape(ATT_BLK * self.dil, val.shape[-1])
            del self.written[id(ref), j]


def _att_consts(hp, dil):
    h0 = lax.broadcasted_iota(jnp.int32, (ATT_BLK, 128), 1) < HEAD_DIM
    a = lax.broadcasted_iota(jnp.int32, (ATT_BLK, ATT_BLK), 0)
    c = lax.broadcasted_iota(jnp.int32, (ATT_BLK, ATT_BLK), 1)
    dist_p = ((ATT_BLK + a - c) * dil).astype(F32)
    dist_c = ((a - c) * dil).astype(F32)
    bias_p, bias_c = [], []
    for h in range(2):
        slope = jnp.float32(_slope(2 * (HEAD_PAIRS - 1) + h))
        for p in range(HEAD_PAIRS - 2, -1, -1):
            slope = jnp.where(hp == p, jnp.float32(_slope(2 * p + h)), slope)
        bias_p.append(jnp.where(c >= a, -slope * dist_p, NEG_INF))
        bias_c.append(jnp.where(c <= a, -slope * dist_c, NEG_INF))
    return h0, bias_p, bias_c


def _split_heads(x, h0):
    return [jnp.where(h0, x, 0.0).astype(BF16), jnp.where(h0, 0.0, x).astype(BF16)]


def _head_cols(x):
    return [x[:, 0:1], x[:, HEAD_DIM:HEAD_DIM + 1]]


def _in_groups(units, first_stage, *later_stages):
    for u0 in range(0, len(units), ATT_UNITS_TOGETHER):
        staged = [first_stage(*u) for u in units[u0:u0 + ATT_UNITS_TOGETHER]]
        for stage in later_stages:
            staged = [stage(*s) for s in staged]


def _attn_fwd(name, q, kv, g, bl, hub=None):
    T = q.shape[0]
    dil, _, nsub, _, _ = _att_geom(T, bl, g)
    tile, prev, _ = _att_specs(T, bl, g)

    def body(q_ref, kp_ref, kc_ref, vp_ref, vc_ref, o_ref, lse_ref):
        first = pl.program_id(1) == 0
        h0, bias_p, bias_c = _att_consts(pl.program_id(2), dil)
        bias_first = [jnp.where(first, NEG_INF, b) for b in bias_p]
        ones = jnp.ones((ATT_BLK, 128), BF16)
        rows = _Residues(dil)

        def scores(j, r):
            if j == 0:
                kp, vp, bp = rows.load(kp_ref, 0, r), rows.load(vp_ref, 0, r), bias_first
            else:
                kp, vp, bp = rows.load(kc_ref, j - 1, r), rows.load(vc_ref, j - 1, r), bias_p
            kp, kc = kp.astype(BF16), rows.load(kc_ref, j, r).astype(BF16)
            qh = _split_heads(rows.load(q_ref, j, r) * ATT_SCALE, h0)
            sp = [_dot_nt(qh[h], kp) + bp[h] for h in range(2)]
            sc = [_dot_nt(qh[h], kc) + bias_c[h] for h in range(2)]
            return (j, r), sp, sc, vp.astype(BF16), rows.load(vc_ref, j, r).astype(BF16)

        def weights(unit, sp, sc, vp, vc):
            mx = [jnp.max(jnp.maximum(sp[h], sc[h]), axis=-1, keepdims=True) for h in range(2)]
            ep = [jnp.exp(sp[h] - mx[h]).astype(BF16) for h in range(2)]
            ec = [jnp.exp(sc[h] - mx[h]).astype(BF16) for h in range(2)]
            return unit, mx, ep, ec, vp, vc

        def outputs(unit, mx, ep, ec, vp, vc):
            den = [_dot(ep[h], ones) + _dot(ec[h], ones) for h in range(2)]
            acc = [_dot(ep[h], vp) + _dot(ec[h], vc) for h in range(2)]
            rows.store(o_ref, *unit, jnp.where(h0, acc[0] / den[0], acc[1] / den[1]))
            rows.store(lse_ref, *unit, jnp.where(h0, mx[0] + jnp.log(den[0]), mx[1] + jnp.log(den[1])))
            return ()

        _in_groups([(j, r) for j in range(nsub) for r in range(dil)], scores, weights, outputs)

    ins = [(q, tile(lambda hp: 4 * g + hp)), (kv, prev(lambda hp: 8 * g + hp)), (kv, tile(lambda hp: 8 * g + hp)),
           (kv, prev(lambda hp: 8 * g + 4 + hp)), (kv, tile(lambda hp: 8 * g + 4 + hp))]
    out = (SDS((T, QW), F32), tile(lambda hp: hp))
    _, _, _, _, nt = _att_geom(T, bl, g)
    return _call(name, body, (bl, nt, HEAD_PAIRS), ins, [out, out], hub=hub)


def _combine(name, os_, lses, tm=512):
    T = os_[0].shape[0]

    def body(o0, o1, o2, l0, l1, l2, o_ref, lse_ref):
        ls = [l0[...], l1[...], l2[...]]
        mx = jnp.maximum(jnp.maximum(ls[0], ls[1]), ls[2])
        es = [jnp.exp(l - mx) for l in ls]
        den = es[0] + es[1] + es[2]
        o_ref[...] = (es[0] * o0[...] + es[1] * o1[...] + es[2] * o2[...]) / den
        lse_ref[...] = mx + jnp.log(den)

    ins = [_rows(t, tm) for t in list(os_) + list(lses)]
    return _call(name, body, (T // tm,), ins, [_out_rows(T, QW, F32, tm), _out_rows(T, QW, F32, tm)])


def _attn_bwd_dq(name, q, kv, do, delta, lse, g, bl, hub=None):
    T = q.shape[0]
    dil, _, nsub, _, nt = _att_geom(T, bl, g)
    tile, prev, _ = _att_specs(T, bl, g)

    def body(q_ref, kp_ref, kc_ref, vp_ref, vc_ref, do_ref, dl_ref, lse_ref, dq_ref):
        first = pl.program_id(1) == 0
        h0, bias_p, bias_c = _att_consts(pl.program_id(2), dil)
        bias_first = [jnp.where(first, NEG_INF, b) for b in bias_p]

        rows = _Residues(dil)

        def probs(j, r):
            if j == 0:
                kp, vp, bp = rows.load(kp_ref, 0, r), rows.load(vp_ref, 0, r), bias_first
            else:
                kp, vp, bp = rows.load(kc_ref, j - 1, r), rows.load(vc_ref, j - 1, r), bias_p
            kp, vp = kp.astype(BF16), vp.astype(BF16)
            kc, vc = rows.load(kc_ref, j, r).astype(BF16), rows.load(vc_ref, j, r).astype(BF16)
            qh = _split_heads(rows.load(q_ref, j, r) * ATT_SCALE, h0)
            dob = _split_heads(rows.load(do_ref, j, r), h0)
            lse_h = _head_cols(rows.load(lse_ref, j, r))
            pp = [jnp.exp(_dot_nt(qh[h], kp) + bp[h] - lse_h[h]) for h in range(2)]
            pc = [jnp.exp(_dot_nt(qh[h], kc) + bias_c[h] - lse_h[h]) for h in range(2)]
            dpp = [_dot_nt(dob[h], vp) for h in range(2)]
            dpc = [_dot_nt(dob[h], vc) for h in range(2)]
            return (j, r), pp, pc, dpp, dpc, kp, kc

        def dscores(unit, pp, pc, dpp, dpc, kp, kc):
            dl = _head_cols(rows.load(dl_ref, *unit))
            dsp = [(pp[h] * (dpp[h] - dl[h])).astype(BF16) for h in range(2)]
            dsc = [(pc[h] * (dpc[h] - dl[h])).astype(BF16) for h in range(2)]
            return unit, dsp, dsc, kp, kc

        def outputs(unit, dsp, dsc, kp, kc):
            dqh = [_dot(dsp[h], kp) + _dot(dsc[h], kc) for h in range(2)]
            rows.store(dq_ref, *unit, jnp.where(h0, dqh[0], dqh[1]) * ATT_SCALE)
            return ()

        _in_groups([(j, r) for j in range(nsub) for r in range(dil)], probs, dscores, outputs)

    own = lambda hp: hp
    ins = [(q, tile(lambda hp: 4 * g + hp)), (kv, prev(lambda hp: 8 * g + hp)), (kv, tile(lambda hp: 8 * g + hp)),
           (kv, prev(lambda hp: 8 * g + 4 + hp)), (kv, tile(lambda hp: 8 * g + 4 + hp)),
           (do, tile(own)), (delta, tile(own)), (lse, tile(own))]
    return _call(name, body, (bl, nt, HEAD_PAIRS), ins, [(SDS((T, QW), F32), tile(own))], hub=hub)[0]


def _attn_bwd_dkv(name, q, kv, do, delta, lse, g, bl, prev=None, hub=None):
    T = q.shape[0]
    dil, _, nsub, _, nt = _att_geom(T, bl, g)
    tile, _, nxt = _att_specs(T, bl, g)
    has_prev = prev is not None

    def body(*refs):
        k_ref, v_ref, q_ref, qn_ref, do_ref, don_ref, dl_ref, dln_ref, l_ref, ln_ref = refs[:10]
        rest = refs[10:]
        if has_prev:
            dkp_ref, dvp_ref, dk_ref, dv_ref = rest
        else:
            dk_ref, dv_ref = rest
        last = pl.program_id(1) == nt - 1
        h0, bias_p, bias_c = _att_consts(pl.program_id(2), dil)
        bias_last = [jnp.where(last, NEG_INF, b) for b in bias_p]

        rows = _Residues(dil)

        def probs(j, r):
            kb, vb = rows.load(k_ref, j, r).astype(BF16), rows.load(v_ref, j, r).astype(BF16)
            sets = [(q_ref, do_ref, dl_ref, l_ref, j, bias_c)]
            if j < nsub - 1:
                sets.append((q_ref, do_ref, dl_ref, l_ref, j + 1, bias_p))
            else:
                sets.append((qn_ref, don_ref, dln_ref, ln_ref, 0, bias_last))
            out = []
            for qr, dor, dlr, lr, jq, bias in sets:
                qs = rows.load(qr, jq, r) * ATT_SCALE
                do2 = rows.load(dor, jq, r)
                qh = _split_heads(qs, h0)
                dob = _split_heads(do2, h0)
                lse_h = _head_cols(rows.load(lr, jq, r))
                p = [jnp.exp(_dot_nt(qh[h], kb) + bias[h] - lse_h[h]) for h in range(2)]
                dp = [_dot_nt(dob[h], vb) for h in range(2)]
                out.append((p, dp, dlr, jq, qs.astype(BF16), do2.astype(BF16)))
            return (j, r), out

        def dscores(unit, sets):
            out = []
            for p, dp, dlr, jq, qsb, do2b in sets:
                dl = _head_cols(rows.load(dlr, jq, unit[1]))
                ds = [(p[h] * (dp[h] - dl[h])).astype(BF16) for h in range(2)]
                out.append(([p[h].astype(BF16) for h in range(2)], ds, qsb, do2b))
            return unit, out

        def outputs(unit, sets):
            dk = [None, None]
            dv = [None, None]
            for pb, ds, qsb, do2b in sets:
                for h in range(2):
                    dvh = _dot_tn(pb[h], do2b)
                    dkh = _dot_tn(ds[h], qsb)
                    dv[h] = dvh if dv[h] is None else dv[h] + dvh
                    dk[h] = dkh if dk[h] is None else dk[h] + dkh
            dk2 = jnp.where(h0, dk[0], dk[1])
            dv2 = jnp.where(h0, dv[0], dv[1])
            if has_prev:
                dk2 = dk2 + rows.load(dkp_ref, *unit)
                dv2 = dv2 + rows.load(dvp_ref, *unit)
            rows.store(dk_ref, *unit, dk2)
            rows.store(dv_ref, *unit, dv2)
            return ()

        _in_groups([(j, r) for j in range(nsub) for r in range(dil)], probs, dscores, outputs)

    own = lambda hp: hp
    qcol = lambda hp: 4 * g + hp
    ins = [(kv, tile(lambda hp: 8 * g + hp)), (kv, tile(lambda hp: 8 * g + 4 + hp)), (q, tile(qcol)), (q, nxt(qcol)),
           (do, tile(own)), (do, nxt(own)), (delta, tile(own)), (delta, nxt(own)), (lse, tile(own)), (lse, nxt(own))]
    if has_prev:
        ins += [(prev[0], tile(own)), (prev[1], tile(own))]
    out = (SDS((T, QW), F32), tile(own))
    return _call(name, body, (bl, nt, HEAD_PAIRS), ins, [out, out], hub=hub)


def _final_loss(name, h, tgt, g, tm=256):
    T, D = h.shape
    n_steps = T // tm

    def body(h_ref, t_ref, g_ref, dh_ref, loss_ref, dg_ref, sq_ref):
        i = pl.program_id(0)
        x = h_ref[...]
        rstd = _rstd(x)
        xhat = x * rstd
        err = xhat * g_ref[...] - t_ref[...]
        _acc8(sq_ref, err * err, i, n_steps)
        dy = err * (1.0 / D)
        dxhat = dy * g_ref[...]
        dh_ref[...] = rstd * (dxhat - xhat * jnp.mean(dxhat * xhat, axis=-1, keepdims=True))
        _acc8(dg_ref, dy * xhat, i, n_steps)

        @pl.when(i == n_steps - 1)
        def _():
            loss_ref[...] = jnp.full(loss_ref.shape, jnp.sum(sq_ref[0:1, :]), F32)

    outs = [_out_rows(T, D, F32, tm), (SDS((SUBLANES, 128), F32), pl.BlockSpec((SUBLANES, 128), lambda i: (0, 0))),
            _out_acc8(D)]
    dh, loss, dg = _call(name, body, (n_steps,), [_rows(h, tm), _rows(tgt, tm), _full(g)], outs,
                         scratch=[pltpu.VMEM((SUBLANES, D), F32)])
    return dh, loss[0, 0], dg[0:1]


def _conv_bwd(name, bcu, dgated, cw, seq, tm=256, hub=None):
    T, D = dgated.shape
    n_steps = T // tm

    def body(b_ref, c_ref, u_ref, ch_ref, uh_ref, dg_ref, dgn_ref, bn_ref, cw_ref, o_ref, t0_ref, t1_ref, t2_ref):
        i = pl.program_id(0)
        first = (i * tm) % seq == 0
        last = ((i + 1) * tm) % seq == 0
        _, (b, c, u), conv, (cu, cu1, cu2) = _gate(b_ref, c_ref, u_ref, ch_ref, uh_ref, cw_ref, first)
        dgat = dg_ref[...]
        dconv = dgat * b
        nxt = jnp.where(last, 0.0, dgn_ref[...] * bn_ref[...].astype(F32))
        rows = lax.broadcasted_iota(jnp.int32, dconv.shape, 0)
        n1 = nxt[0:1, :]
        n2 = nxt[1:2, :]
        dc1 = jnp.where(rows == tm - 1, n1, pltpu.roll(dconv, tm - 1, 0))
        dc2 = jnp.where(rows == tm - 1, n2, jnp.where(rows == tm - 2, n1, pltpu.roll(dconv, tm - 2, 0)))
        dcu = cw_ref[0:1, :] * dconv + cw_ref[1:2, :] * dc1 + cw_ref[2:3, :] * dc2
        o_ref[:, 0:D] = (dgat * conv).astype(BF16)
        o_ref[:, D:2 * D] = (dcu * u).astype(BF16)
        o_ref[:, 2 * D:3 * D] = (dcu * c).astype(BF16)
        _acc8(t0_ref, dconv * cu, i, n_steps)
        _acc8(t1_ref, dconv * cu1, i, n_steps)
        _acc8(t2_ref, dconv * cu2, i, n_steps)

    ins = [_rows(bcu, tm, D, 0), _rows(bcu, tm, D, 1), _rows(bcu, tm, D, 2), _prev8(bcu, tm, D, 1), _prev8(bcu, tm, D, 2),
           _rows(dgated, tm), _next8(dgated, tm, D, 0), _next8(bcu, tm, D, 0), _full(cw)]
    outs = [_out_rows(T, 3 * D, BF16, tm), _out_acc8(D), _out_acc8(D), _out_acc8(D)]
    dbcu, t0, t1, t2 = _call(name, body, (n_steps,), ins, outs, hub=hub)
    return dbcu, jnp.concatenate([t0[0:1], t1[0:1], t2[0:1]], axis=0)


def _sum8_adamw(name, parts, w, m, v, tr):
    R, C = w.shape
    b1c = 1.0 - ADAM_B1 ** ADAM_STEP
    b2c = 1.0 - ADAM_B2 ** ADAM_STEP

    def body(p_ref, w_ref, m_ref, v_ref, g_ref, d_ref, nm_ref, nv_ref):
        g = p_ref[0].astype(F32)
        for j in range(1, N_DEV):
            g = g + p_ref[j].astype(F32)
        nm = ADAM_B1 * m_ref[...] + (1.0 - ADAM_B1) * g
        nv = ADAM_B2 * v_ref[...] + (1.0 - ADAM_B2) * (g * g)
        m_hat = nm / b1c
        v_hat = nv / b2c
        g_ref[...] = g
        d_ref[...] = -ADAM_LR * (m_hat / (jnp.sqrt(v_hat) + ADAM_EPS) + ADAM_WD * w_ref[...])
        nm_ref[...] = nm
        nv_ref[...] = nv

    ins = [(parts, pl.BlockSpec((N_DEV, tr, C), lambda i: (0, i, 0))), _rows(w, tr), _rows(m, tr), _rows(v, tr)]
    outs = [_out_rows(R, C, F32, tr)] * 4
    return _call(name, body, (R // tr,), ins, outs)


def _all_gather(name, items):
    n = len(items)
    shapes = [tuple(a.shape if idx is None else a.shape[1:]) for a, idx in items]

    def body(*refs):
        x_refs, out_refs = refs[:n], refs[n:2 * n]
        send_sems, recv_sems, local_sems = refs[2 * n:]
        x, y, c = _mesh_pos()
        me, sibling = (x, y, c), (x, y, 1 - c)
        chips = [(1 - x, y), (x, 1 - y), (1 - x, 1 - y)]

        def copy(t, k, block, to, own=False):
            dst = out_refs[t].at[4 * block[0] + 2 * block[1] + block[2]]
            src = dst
            if own:
                src = x_refs[t] if items[t][1] is None else x_refs[t].at[items[t][1]]
            return pltpu.make_async_remote_copy(
                src_ref=src, dst_ref=dst, send_sem=send_sems.at[t, k], recv_sem=recv_sems.at[t, k],
                device_id=to, device_id_type=pl.DeviceIdType.MESH)

        started = []
        for t in range(n):
            src = x_refs[t] if items[t][1] is None else x_refs[t].at[items[t][1]]
            mine = pltpu.make_async_copy(src, out_refs[t].at[4 * x + 2 * y + c], local_sems.at[t])
            mine.start()
            first = [copy(t, 0, me, sibling, own=True)]
            first += [copy(t, 1 + j, me, (*chip, c), own=True) for j, chip in enumerate(chips)]
            for cp in first:
                cp.start()
            started.append((mine, first))
        passed = []
        for t in range(n):
            for j, chip in enumerate(chips):
                copy(t, 1 + j, (*chip, c), me).wait_recv()
                fwd = copy(t, 4 + j, (*chip, c), sibling)
                fwd.start()
                passed.append(fwd)
        for t in range(n):
            copy(t, 0, sibling, me).wait_recv()
            for j, chip in enumerate(chips):
                copy(t, 4 + j, (*chip, 1 - c), me).wait_recv()
        for mine, first in started:
            for cp in first:
                cp.wait_send()
            mine.wait()
        for cp in passed:
            cp.wait_send()

    any_spec = pl.BlockSpec(memory_space=pl.ANY)
    return pl.pallas_call(
        body, name=name,
        out_shape=[SDS((N_DEV,) + s, a.dtype) for s, (a, _) in zip(shapes, items)],
        in_specs=[any_spec] * n,
        out_specs=[any_spec] * n,
        scratch_shapes=[pltpu.SemaphoreType.DMA((n, 7)), pltpu.SemaphoreType.DMA((n, 7)), pltpu.SemaphoreType.DMA((n,))],
    )(*[a for a, _ in items])


def _pad8(t):
    return jnp.pad(t, ((0, SUBLANES - t.shape[0]), (0, 0)))


def _rows_merged(w):
    return w.reshape(w.shape[0] * w.shape[1], w.shape[2])


def _local_grads(x, tgt, norm_mix, norm_mlp, norm_kv, norm_final, conv_w, hub):
    bl, seq, D = x.shape
    T = bl * seq
    h = x.reshape(T, D)
    tgt = tgt.reshape(T, D)
    row = lambda t, l: t[l:l + 1]
    W = hub.weights
    saved = []
    kv = h_kv = hn_kv_t = None
    for l in range(DEPTH):
        if l < N_A_LAYERS:
            bcu, hn_t = _norm_mm(f"l{l}_in", h, row(norm_mix, l), W["w_a_in", l], out_dtype=BF16, transposed=True, hub=hub)
            h2, gated_t = _gate_mm_res(f"l{l}_conv_out", bcu, _pad8(conv_w[l]), _rows_merged(W["w_a_out", l]), h, seq, hub=hub)
            saved.append((h, bcu, gated_t, hn_t))
        else:
            i = l - N_A_LAYERS
            if l == N_A_LAYERS:
                h_kv = h
                kv, hn_kv_t = _norm_mm("kv", h, norm_kv.reshape(1, D), W["w_kv", None], transposed=True, hub=hub)
            q, hn_t = _norm_mm(f"l{l}_q", h, row(norm_mix, l), W["w_q", i], transposed=True)
            per_group = [_attn_fwd(f"l{l}_att{g}", q, kv, g, bl, hub=hub) for g in range(N_GROUPS)]
            o, lse = _combine(f"l{l}_combine", [p[0] for p in per_group], [p[1] for p in per_group])
            h2 = _mm_res(f"l{l}_att_out", o, W["w_o", i], h)
            saved.append((h, q, o, lse, hn_t))
        a = _norm_mm(f"l{l}_up", h2, row(norm_mlp, l), W["w_up", l], out_dtype=BF16, hub=hub)
        h = _relu2_mm_res(f"l{l}_down", a, _rows_merged(W["w_down", l]), h2, hub=hub)
        saved[-1] = saved[-1] + (h2, a)

    dh, sq_err, d_norm_final = _final_loss("loss", h, tgt, norm_final.reshape(1, D))

    d_norm_mix = [None] * DEPTH
    d_norm_mlp = [None] * DEPTH
    d_conv = [None] * N_A_LAYERS
    d_norm_kv = None
    dkv_acc = [None] * N_GROUPS
    G = hub.grads
    as_slots = lambda g: g.reshape(N_DEV, g.shape[0] // N_DEV, g.shape[1])
    tt = DW_TOKENS
    for l in reversed(range(DEPTH)):
        h2, a = saved[l][-2:]
        h_in = saved[l][0]
        g_mlp = row(norm_mlp, l)
        g_mix = row(norm_mix, l)
        w_up_l = W["w_up", l]
        FF = N_DEV * w_up_l.shape[2]
        da = _nt_relu2_bwd(f"l{l}_down_bwd", dh, _rows_merged(W["w_down", l]), a, hub=hub)
        G["w_down", l] = as_slots(_tn(f"l{l}_dw_down", [_rows2(a, tt, FF // 2, lambda s: s)], _relu2,
                                      [_rows2(dh, tt)], _val, FF, D, T, tt, split=("k", 2)))
        G["w_up", l] = _tn(f"l{l}_dw_up", [_rows2(h2, 2 * tt), _full2(g_mlp)], _normed,
                           [_rows2(da, 2 * tt, FF // 4, lambda s: s)], _val, D, FF, T, 2 * tt, split=("n", 4),
                           out_cols=w_up_l.shape[2], hub=hub)
        dh2, d_norm_mlp[l] = _nt_norm_bwd(f"l{l}_up_bwd", [da], w_up_l, h2, g_mlp, dh, hub=hub)
        if l >= N_A_LAYERS:
            i = l - N_A_LAYERS
            _, q, o, lse, hn_t = saved[l][:5]
            w_o_i, w_q_i = W["w_o", i], W["w_q", i]
            do, delta = _att_out_bwd(f"l{l}_att_out_bwd", dh2, w_o_i, o)
            G["w_o", i] = _tn(f"l{l}_dw_o", [_rows2(o, tt)], _val, [_rows2(dh2, tt)], _val, QW, D, T, tt,
                              out_cols=w_o_i.shape[2])
            dqs = []
            for g in range(N_GROUPS):
                dqs.append(_attn_bwd_dq(f"l{l}_att{g}_dq", q, kv, do, delta, lse, g, bl, hub=hub))
                dkv_acc[g] = _attn_bwd_dkv(f"l{l}_att{g}_dkv", q, kv, do, delta, lse, g, bl, prev=dkv_acc[g], hub=hub)
            G["w_q", i] = _tn(f"l{l}_dw_q", [_cols2(hn_t, tt)], None,
                              [_rows2(t, tt) for t in dqs], _concat_f32, D, N_GROUPS * QW, T, tt, out_cols=w_q_i.shape[2])
            dh, d_norm_mix[l] = _nt_norm_bwd(f"l{l}_q_bwd", dqs, w_q_i, h_in, g_mix, dh2, hub=hub)
            if l == N_A_LAYERS:
                dkvs = [t for pair in dkv_acc for t in pair]
                g_kv = norm_kv.reshape(1, D)
                w_kv = W["w_kv", None]
                per_call = len(dkvs) // 2
                halves = [_tn(f"dw_kv{p}", [_cols2(hn_kv_t, tt)], None,
                              [_rows2(t, tt) for t in dkvs[p * per_call:(p + 1) * per_call]], _concat_f32,
                              D, per_call * QW, T, tt, out_cols=w_kv.shape[2]) for p in range(2)]
                G["w_kv", None] = jnp.concatenate(halves, axis=0)
                dh, d_norm_kv = _nt_norm_bwd("kv_bwd", dkvs, w_kv, h_kv, g_kv, dh, hub=hub)
        else:
            _, bcu, gated_t, hn_t = saved[l][:4]
            cw = _pad8(conv_w[l])
            w_in_l = W["w_a_in", l]
            dgated = _nt_plain(f"l{l}_conv_out_bwd", dh2, _rows_merged(W["w_a_out", l]))
            G["w_a_out", l] = as_slots(_tn(f"l{l}_dw_conv_out", [_cols2(gated_t, 2 * tt)], None,
                                           [_rows2(dh2, 2 * tt)], _val, D, D, T, 2 * tt, hub=hub))
            dbcu, d_conv[l] = _conv_bwd(f"l{l}_conv_bwd", bcu, dgated, cw, seq, hub=hub)
            G["w_a_in", l] = _tn(f"l{l}_dw_in", [_cols2(hn_t, 2 * tt)], None,
                                 [_rows2(dbcu, 2 * tt, 3 * D // 2, lambda s: s)], _val, D, 3 * D, T, 2 * tt, split=("n", 2),
                                 out_cols=w_in_l.shape[2], hub=hub)
            dh, d_norm_mix[l] = _nt_norm_bwd(f"l{l}_in_bwd", [dbcu], w_in_l, h_in, g_mix, dh2, hub=hub)

    small = jnp.concatenate(d_norm_mix + d_norm_mlp + [d_norm_kv, d_norm_final] + d_conv, axis=0)
    return sq_err, dh.reshape(bl, seq, D), small


def kernel(x, norm_mix, norm_mlp, w_a_in, conv_w, w_a_out, norm_kv, w_kv, w_q, w_o, w_up, w_down, norm_final, loss_target, m_norm_mix, m_norm_mlp, m_w_a_in, m_conv_w, m_w_a_out, m_norm_kv, m_w_kv, m_w_q, m_w_o, m_w_up, m_w_down, m_norm_final, v_norm_mix, v_norm_mlp, v_w_a_in, v_conv_w, v_w_a_out, v_norm_kv, v_w_kv, v_w_q, v_w_o, v_w_up, v_w_down, v_norm_final):
    D = x.shape[-1]
    xi, yi, ci = _mesh_pos()
    me_idx = 4 * xi + 2 * yi + ci
    w_big = dict(w_a_in=w_a_in, w_a_out=w_a_out, w_kv=w_kv, w_q=w_q, w_o=w_o, w_up=w_up, w_down=w_down)
    m_big = dict(w_a_in=m_w_a_in, w_a_out=m_w_a_out, w_kv=m_w_kv, w_q=m_w_q, w_o=m_w_o, w_up=m_w_up, w_down=m_w_down)
    v_big = dict(w_a_in=v_w_a_in, w_a_out=v_w_a_out, w_kv=v_w_kv, w_q=v_w_q, w_o=v_w_o, w_up=v_w_up, w_down=v_w_down)
    names = list(w_big)

    shards = {n: w.astype(BF16) for n, w in w_big.items()}
    landing = {n: lax.empty((N_DEV,) + w.shape, BF16) for n, w in w_big.items()}
    hub = _Hub(FETCH_DURING, PUSH_DURING, shards, landing)
    dc = conv_w.shape[-1]
    taps = conv_w.shape[0] * conv_w.shape[1]
    got = _all_gather("gather_first", [(shards[n], l) for n, l in FETCH_UP_FRONT] + [(_pad8(conv_w.reshape(taps, dc)), None)])
    for key, w in zip(FETCH_UP_FRONT, got):
        hub.weights[key] = w
    conv_full = jnp.moveaxis(got[-1][:, :taps], 0, 1).reshape(conv_w.shape[0], conv_w.shape[1], N_DEV * dc)

    sq_err, grad_x, small = _local_grads(x, loss_target, norm_mix, norm_mlp, norm_kv, norm_final, conv_full, hub)
    loss = lax.psum(sq_err * (0.5 / D), ("x", "y", "c"))

    grads, deltas, new_m, new_v = {}, {}, {}, {}
    for n in names:
        shape = w_big[n].shape
        cols = shape[-1]
        flat = lambda t: t.reshape(-1, cols)
        parts = hub.landing[n].reshape(N_DEV, -1, cols)
        outs = _sum8_adamw(f"adamw_{n}", parts, flat(w_big[n]), flat(m_big[n]), flat(v_big[n]), tr=min(256, parts.shape[1]))
        grads[n], deltas[n], new_m[n], new_v[n] = (t.reshape(shape) for t in outs)

    n_gain = 2 * DEPTH + 2
    rows_small = small.shape[0]
    small_all = _all_gather("gather_small_grads", [(small, None)])[0]

    def small_pack(nm, nl, nk, nf, cw):
        gains = jnp.concatenate([nm, nl, nk.reshape(1, D), nf.reshape(1, D)], axis=0)
        taps_full = lax.dynamic_update_slice(jnp.zeros((taps, D), F32), cw.reshape(taps, dc), (0, me_idx * dc))
        return jnp.concatenate([gains, taps_full], axis=0)

    sp = [small_pack(*t) for t in ((norm_mix, norm_mlp, norm_kv, norm_final, conv_w),
                                   (m_norm_mix, m_norm_mlp, m_norm_kv, m_norm_final, m_conv_w),
                                   (v_norm_mix, v_norm_mlp, v_norm_kv, v_norm_final, v_conv_w))]
    small_out = _sum8_adamw("adamw_small", small_all, *sp, tr=rows_small)

    def small_unpack(t):
        res = dict(norm_mix=t[0:DEPTH], norm_mlp=t[DEPTH:2 * DEPTH], norm_kv=t[2 * DEPTH], norm_final=t[2 * DEPTH + 1])
        res["conv_w"] = lax.dynamic_slice(t[n_gain:], (0, me_idx * dc), (taps, dc)).reshape(conv_w.shape)
        return res

    for dst, t in zip((grads, deltas, new_m, new_v), small_out):
        dst.update(small_unpack(t))

    order = ["norm_mix", "norm_mlp", "w_a_in", "conv_w", "w_a_out", "norm_kv", "w_kv", "w_q", "w_o", "w_up", "w_down",
             "norm_final"]
    return (loss, grad_x, *[grads[n] for n in order], *[deltas[n] for n in order], *[new_m[n] for n in order],
            *[new_v[n] for n in order])
```

```python
import functools

import jax
import jax.numpy as jnp
from jax import lax
from jax.experimental import pallas as pl
from jax.experimental.pallas import tpu as pltpu

F32 = jnp.float32
BF16 = jnp.bfloat16
SDS = jax.ShapeDtypeStruct

EPS = 1e-5
N_A_LAYERS = 2
DEPTH = 4
PATTERNS = ((128, 1), (512, 4), (2048, 16))
N_GROUPS = 3
H_G = 8
HEAD_DIM = 64
QW = H_G * HEAD_DIM
ATT_BLK = 128
ALIBI_MAX_BIAS = 8.0
NEG_INF = -1e30

ADAM_LR = 0.001
ADAM_B1 = 0.9
ADAM_B2 = 0.999
ADAM_EPS = 1e-08
ADAM_WD = 0.01
ADAM_STEP = 10

N_DEV = 8
SUBLANES = 8
HALO = 16
V7X_VMEM_LIMIT = 48 * 1024 * 1024
MXU_COLS = 256
MM_CHUNK = 512
MM_ROWS = 512
DW_TOKENS = 1024

FETCH_UP_FRONT = [("w_a_in", 0), ("w_a_out", 0)]
FETCH_DURING = {
    "l0_in": [("w_up", 0)], "l0_conv_out": [("w_down", 0)], "l0_up": [("w_a_in", 1), ("w_a_out", 1)], "l0_down": [("w_up", 1)],
    "l1_in": [("w_down", 1)], "l1_conv_out": [("w_kv", None)],
    "l1_up": [("w_q", 0), ("w_o", 0), ("w_q", 1), ("w_o", 1)], "l1_down": [("w_up", 2)],
    "kv": [("w_down", 2)], "l2_att0": [("w_up", 3)], "l2_att1": [("w_down", 3)],
}
PUSH_DURING = {
    "l3_dw_up": [("w_down", 3, 0, 2)], "l3_up_bwd": [("w_down", 3, 1, 2)],
    "l3_att0_dq": [("w_up", 3, 0, 2)], "l3_att0_dkv": [("w_up", 3, 1, 2)], "l3_att1_dq": [("w_o", 1)], "l3_q_bwd": [("w_q", 1)],
    "l2_dw_up": [("w_down", 2, 0, 2)], "l2_up_bwd": [("w_down", 2, 1, 2)],
    "l2_att0_dq": [("w_up", 2, 0, 2)], "l2_att0_dkv": [("w_up", 2, 1, 2)], "l2_att1_dq": [("w_o", 0)], "l2_q_bwd": [("w_q", 0)],
    "kv_bwd": [("w_kv", None, 0, 2)], "l1_down_bwd": [("w_kv", None, 1, 2)],
    "l1_dw_up": [("w_down", 1, 0, 2)], "l1_up_bwd": [("w_down", 1, 1, 2)], "l1_conv_bwd": [("w_up", 1, 0, 2)],
    "l1_dw_in": [("w_up", 1, 1, 2), ("w_a_out", 1)], "l1_in_bwd": [("w_a_in", 1, 0, 2)], "l0_down_bwd": [("w_a_in", 1, 1, 2)],
    "l0_dw_up": [("w_down", 0, 0, 2)], "l0_up_bwd": [("w_down", 0, 1, 2)], "l0_conv_bwd": [("w_up", 0, 0, 2)],
    "l0_dw_in": [("w_up", 0, 1, 2), ("w_a_out", 0)], "l0_in_bwd": [("w_a_in", 0)],
}


def _mesh_pos():
    return lax.axis_index("x"), lax.axis_index("y"), lax.axis_index("c")


def _flip(v, bit):
    return 1 - v if bit else v


class _Transfer:
    def __init__(self, kind, key, src, src_idx=None, dst=None, dst_idx=None, dst_shape=None, rows=None):
        self.kind, self.key, self.src, self.src_idx = kind, key, src, src_idx
        self.dst, self.dst_idx, self.dst_shape, self.rows = dst, dst_idx, dst_shape, rows

    def copies(self, src_ref, dst_ref, send_sems, recv_sems, local_sem):
        x, y, c = _mesh_pos()
        me = 4 * x + 2 * y + c
        part = (lambda r: r) if self.rows is None else (lambda r: r.at[pl.ds(*self.rows)])

        def dst_slot(j):
            r = dst_ref.at[j]
            return part(r if self.dst_idx is None else r.at[self.dst_idx])

        def copy(k, src, dst_j, to):
            return pltpu.make_async_remote_copy(
                src_ref=src, dst_ref=dst_slot(dst_j), send_sem=send_sems.at[k], recv_sem=recv_sems.at[k],
                device_id=to, device_id_type=pl.DeviceIdType.MESH)

        if self.kind == "exchange":
            local = pltpu.make_async_copy(part(src_ref.at[me]), dst_slot(me), local_sem)
            sends, arrivals = [], []
            for k in range(1, N_DEV):
                peer = (_flip(x, k & 4), _flip(y, k & 2), _flip(c, k & 1))
                peer_idx = 4 * peer[0] + 2 * peer[1] + peer[2]
                sends.append(copy(k - 1, part(src_ref.at[peer_idx]), me, peer))
                arrivals.append(copy(k - 1, part(src_ref.at[peer_idx]), peer_idx, peer))
            return local, sends, [], arrivals

        own = part(src_ref if self.src_idx is None else src_ref.at[self.src_idx])
        idx = lambda px, py, pc: 4 * px + 2 * py + pc
        sibling = (x, y, 1 - c)
        chips = [(1 - x, y), (x, 1 - y), (1 - x, 1 - y)]
        local = pltpu.make_async_copy(own, dst_slot(me), local_sem)
        sends = [copy(0, own, me, sibling)] + [copy(1 + j, own, me, (*chip, c)) for j, chip in enumerate(chips)]
        relays = [(copy(1 + j, own, idx(*chip, c), sibling), copy(4 + j, dst_slot(idx(*chip, c)), idx(*chip, c), sibling))
                  for j, chip in enumerate(chips)]
        arrivals = [copy(0, own, idx(*sibling), sibling)]
        arrivals += [copy(4 + j, own, idx(*chip, 1 - c), sibling) for j, chip in enumerate(chips)]
        return local, sends, relays, arrivals


class _Hub:
    def __init__(self, fetch, push, shards, landing):
        self.fetch, self.push, self.shards, self.landing = fetch, push, shards, landing
        self.weights = {}
        self.arriving = {}
        self.grads = {}

    def transfers(self, host):
        out = []
        for name, l, *part in self.fetch.get(host, ()):
            src = self.shards[name]
            shard = tuple(src.shape if l is None else src.shape[1:])
            p, n = part or (0, 1)
            rows = None if n == 1 else (p * (shard[0] // n), shard[0] // n)
            out.append(_Transfer("gather", (name, l, p == n - 1), src, src_idx=l, dst=self.arriving.get((name, l)),
                                 dst_shape=(N_DEV,) + shard, rows=rows))
        for name, l, *part in self.push.get(host, ()):
            src = self.grads[name, l]
            p, n = part or (0, 1)
            rows = None if n == 1 else (p * (src.shape[1] // n), src.shape[1] // n)
            out.append(_Transfer("exchange", (name, l, p == n - 1), src, dst=self.landing[name], dst_idx=l, rows=rows))
        return out

    def accept(self, transfers, results):
        for t, r in zip(transfers, results):
            name, l, complete = t.key
            if t.kind == "exchange":
                self.landing[name] = r
            elif complete:
                self.weights[name, l] = r
            else:
                self.arriving[name, l] = r


def _call(name, body, grid, ins, outs, scratch=(), hub=None):
    transfers = hub.transfers(name) if hub is not None else []
    n_in, n_out, n_scr, n_tr = len(ins), len(outs), len(scratch), len(transfers)
    c_in, c_out, aliases, places = [], [], {}, []
    for t in transfers:
        c_in.append(t.src)
        src_pos = len(c_in) - 1
        if t.dst is not None:
            c_in.append(t.dst)
            aliases[n_in + len(c_in) - 1] = n_out + len(c_out)
            c_out.append(SDS(t.dst.shape, t.dst.dtype))
        else:
            c_out.append(SDS(t.dst_shape, t.src.dtype))
        places.append((src_pos, len(c_out) - 1))
    sems = [pltpu.SemaphoreType.DMA((n_tr, N_DEV - 1)), pltpu.SemaphoreType.DMA((n_tr, N_DEV - 1)),
            pltpu.SemaphoreType.DMA((n_tr,))] if n_tr else []

    def wrapped(*refs):
        in_refs = refs[:n_in]
        cin_refs = refs[n_in:n_in + len(c_in)]
        o0 = n_in + len(c_in)
        out_refs = refs[o0:o0 + n_out]
        cout_refs = refs[o0 + n_out:o0 + n_out + len(c_out)]
        s0 = o0 + n_out + len(c_out)
        scr_refs = refs[s0:s0 + n_scr]
        if n_tr:
            send_sems, recv_sems, local_sems = refs[s0 + n_scr:]
            first = last = relay = None
            for ax, n in enumerate(grid):
                i = pl.program_id(ax)
                at_relay = (i == max(n - 2, 0)) if ax == len(grid) - 1 else (i == n - 1)
                first = (i == 0) if first is None else first & (i == 0)
                last = (i == n - 1) if last is None else last & (i == n - 1)
                relay = at_relay if relay is None else relay & at_relay

            def all_copies():
                return [t.copies(cin_refs[sp], cout_refs[dp], send_sems.at[n], recv_sems.at[n], local_sems.at[n])
                        for n, (t, (sp, dp)) in enumerate(zip(transfers, places))]

            @pl.when(first)
            def _():
                for local, sends, _, _ in all_copies():
                    local.start()
                    for cp in sends:
                        cp.start()

            def pass_on():
                @pl.when(relay)
                def _():
                    for _, _, relays, _ in all_copies():
                        for arrival, onward in relays:
                            arrival.wait_recv()
                            onward.start()

            if grid[-1] > 1:
                pass_on()

        body(*in_refs, *out_refs, *scr_refs)

        if n_tr:
            if grid[-1] == 1:
                pass_on()

            @pl.when(last)
            def _():
                for local, sends, relays, arrivals in all_copies():
                    for cp in arrivals:
                        cp.wait_recv()
                    for cp in sends + [onward for _, onward in relays]:
                        cp.wait_send()
                    local.wait()

    any_spec = pl.BlockSpec(memory_space=pl.ANY)
    res = pl.pallas_call(
        wrapped,
        name=name,
        grid=grid,
        in_specs=[s for _, s in ins] + [any_spec] * len(c_in),
        out_specs=[s for _, s in outs] + [any_spec] * len(c_out),
        out_shape=[o for o, _ in outs] + c_out,
        scratch_shapes=list(scratch) + sems,
        input_output_aliases=aliases,
        compiler_params=pltpu.CompilerParams(
            dimension_semantics=("arbitrary",) * len(grid), vmem_limit_bytes=V7X_VMEM_LIMIT),
    )(*[a for a, _ in ins], *c_in)
    if n_tr:
        hub.accept(transfers, res[n_out:])
    return res[:n_out]


def _rows(a, tm, cb=None, col=0):
    cb = cb or a.shape[1]
    return (a, pl.BlockSpec((tm, cb), lambda i: (i, col)))


def _full(a):
    nd = a.ndim
    return (a, pl.BlockSpec(a.shape, lambda i: (0,) * nd))


def _prev8(a, tm, cb, col):
    return (a, pl.BlockSpec((HALO, cb), lambda i: (jnp.maximum(i * (tm // HALO) - 1, 0), col)))


def _next8(a, tm, cb, col):
    last = a.shape[0] // HALO - 1
    return (a, pl.BlockSpec((HALO, cb), lambda i: (jnp.minimum((i + 1) * (tm // HALO), last), col)))


def _rows2(a, tt, cb=None, colfn=None):
    cb = cb or a.shape[1]
    colfn = colfn or (lambda s: 0)
    return (a, pl.BlockSpec((tt, cb), lambda s, t: (t, colfn(s))))


def _full2(a):
    nd = a.ndim
    return (a, pl.BlockSpec(a.shape, lambda s, t: (0,) * nd))


def _prev8_2(a, tt, cb, col):
    return (a, pl.BlockSpec((HALO, cb), lambda s, t: (jnp.maximum(t * (tt // HALO) - 1, 0), col)))


def _out_rows(T, n, dtype, tm):
    return (SDS((T, n), dtype), pl.BlockSpec((tm, n), lambda i: (i, 0)))


def _out_acc8(d):
    return (SDS((SUBLANES, d), F32), pl.BlockSpec((SUBLANES, d), lambda i: (0, 0)))


def _rstd(x):
    return lax.rsqrt(jnp.mean(x * x, axis=-1, keepdims=True) + EPS)


def _normed(h_ref, g_ref):
    x = h_ref[...]
    return x * _rstd(x) * g_ref[...]


def _acc8(ref, val, i, n):
    part = val.reshape(-1, SUBLANES, val.shape[-1]).sum(axis=0)

    @pl.when(i == 0)
    def _():
        ref[...] = part

    @pl.when(i > 0)
    def _():
        ref[...] += part

    @pl.when(i == n - 1)
    def _():
        ref[...] = jnp.broadcast_to(jnp.sum(ref[...], axis=0, keepdims=True), ref.shape)


def _gate(b_ref, c_ref, u_ref, ch_ref, uh_ref, cw_ref, first):
    b, c, u = (r[...].astype(F32) for r in (b_ref, c_ref, u_ref))
    cu = c * u
    halo = jnp.where(first, 0.0, ch_ref[...].astype(F32) * uh_ref[...].astype(F32))
    rows = lax.broadcasted_iota(jnp.int32, cu.shape, 0)
    h1 = halo[HALO - 1:HALO, :]
    h2 = halo[HALO - 2:HALO - 1, :]
    cu1 = jnp.where(rows == 0, h1, pltpu.roll(cu, 1, 0))
    cu2 = jnp.where(rows == 0, h2, jnp.where(rows == 1, h1, pltpu.roll(cu, 2, 0)))
    conv = cw_ref[0:1, :] * cu + cw_ref[1:2, :] * cu1 + cw_ref[2:3, :] * cu2
    return b * conv, (b, c, u), conv, (cu, cu1, cu2)


def _relu2(a_ref):
    r = jnp.maximum(a_ref[...].astype(F32), 0.0)
    return r * r


def _dot(a, b):
    return jnp.dot(a, b, preferred_element_type=F32)


def _dot_nt(a, b):
    return lax.dot_general(a, b, (((1,), (1,)), ((), ())), preferred_element_type=F32)


def _dot_tn(a, b):
    return lax.dot_general(a, b, (((0,), (0,)), ((), ())), preferred_element_type=F32)


def _chunks(n):
    c = min(MM_CHUNK, n)
    while n % c:
        c -= 128
    assert c > 0, n
    return [(k * c, (k + 1) * c) for k in range(n // c)]


def _col_weight(w):
    _, K, ns = w.shape
    N = N_DEV * ns
    direct = ns % MXU_COLS == 0
    scratch = [] if direct else [pltpu.VMEM((K, N), BF16)]

    def prepare(w_ref, s_ref, step):
        if direct:
            return

        @pl.when(step == 0)
        def _():
            for j in range(N_DEV):
                s_ref[:, j * ns:(j + 1) * ns] = w_ref[j]

    def chunks(w_ref, s_ref):
        if direct:
            return [(j * ns, (j + 1) * ns, (lambda j=j: w_ref[j])) for j in range(N_DEV)]
        return [(lo, hi, (lambda lo=lo, hi=hi: s_ref[:, lo:hi])) for lo, hi in _chunks(N)]

    return N, scratch, prepare, chunks


def _out_cols(n, T, tm):
    return (SDS((n, T), BF16), pl.BlockSpec((n, tm), lambda i: (0, i)))


def _norm_mm(name, h, g, w, tm=MM_ROWS, out_dtype=F32, transposed=False, hub=None):
    T, D = h.shape
    N, w_scratch, prepare, chunks = _col_weight(w)

    def body(h_ref, g_ref, w_ref, o_ref, *rest):
        at_ref, s = (rest[0], rest[1:]) if transposed else (None, rest)
        s_ref = s[0] if s else None
        prepare(w_ref, s_ref, pl.program_id(0))
        a32 = _normed(h_ref, g_ref)
        a = a32.astype(BF16)
        for lo, hi, load in chunks(w_ref, s_ref):
            o_ref[:, lo:hi] = _dot(a, load()).astype(out_dtype)
        if transposed:
            at_ref[...] = a32.T.astype(BF16)

    outs = [_out_rows(T, N, out_dtype, tm)] + ([_out_cols(D, T, tm)] if transposed else [])
    res = _call(name, body, (T // tm,), [_rows(h, tm), _full(g), _full(w)], outs, scratch=w_scratch, hub=hub)
    return res if transposed else res[0]


def _gate_mm_res(name, bcu, cw, w, h, seq, tm=MM_ROWS, hub=None):
    T, D = h.shape

    def body(b_ref, c_ref, u_ref, ch_ref, uh_ref, cw_ref, w_ref, h_ref, o_ref, gt_ref):
        first = (pl.program_id(0) * tm) % seq == 0
        gated32 = _gate(b_ref, c_ref, u_ref, ch_ref, uh_ref, cw_ref, first)[0]
        gated = gated32.astype(BF16)
        for lo, hi in _chunks(D):
            o_ref[:, lo:hi] = h_ref[:, lo:hi] + _dot(gated, w_ref[:, lo:hi])
        gt_ref[...] = gated32.T.astype(BF16)

    ins = [_rows(bcu, tm, D, 0), _rows(bcu, tm, D, 1), _rows(bcu, tm, D, 2), _prev8(bcu, tm, D, 1),
           _prev8(bcu, tm, D, 2), _full(cw), _full(w), _rows(h, tm)]
    return _call(name, body, (T // tm,), ins, [_out_rows(T, D, F32, tm), _out_cols(D, T, tm)], hub=hub)


def _relu2_mm_res(name, a, w, h, tm=MM_ROWS, hub=None):
    T, D = h.shape
    K = a.shape[1]

    def body(a_ref, w_ref, h_ref, o_ref, acc_ref):
        for n, (lo, hi) in enumerate(_chunks(K)):
            d = _dot(_relu2(a_ref.at[:, lo:hi]).astype(BF16), w_ref[lo:hi, :])
            if n == 0:
                acc_ref[...] = d
            else:
                acc_ref[...] += d
        o_ref[...] = h_ref[...] + acc_ref[...]

    return _call(name, body, (T // tm,), [_rows(a, tm), _full(w), _rows(h, tm)], [_out_rows(T, D, F32, tm)],
                 scratch=[pltpu.VMEM((tm, D), F32)], hub=hub)[0]


def _mm_res(name, a, w, h, tm=MM_ROWS):
    T, D = h.shape
    _, w_scratch, prepare, chunks = _col_weight(w)

    def body(a_ref, w_ref, h_ref, o_ref, *s):
        s_ref = s[0] if s else None
        prepare(w_ref, s_ref, pl.program_id(0))
        av = a_ref[...].astype(BF16)
        for lo, hi, load in chunks(w_ref, s_ref):
            o_ref[:, lo:hi] = h_ref[:, lo:hi] + _dot(av, load())

    return _call(name, body, (T // tm,), [_rows(a, tm), _full(w), _rows(h, tm)], [_out_rows(T, D, F32, tm)],
                 scratch=w_scratch)[0]


def _nt_relu2_bwd(name, dh, w, a, tm=MM_ROWS, hub=None):
    T, _ = dh.shape
    K = w.shape[0]

    def body(dh_ref, w_ref, a_ref, o_ref):
        d = dh_ref[...].astype(BF16)
        for lo, hi in _chunks(K):
            dr = _dot_nt(d, w_ref[lo:hi, :])
            o_ref[:, lo:hi] = (dr * (2.0 * jnp.maximum(a_ref[:, lo:hi].astype(F32), 0.0))).astype(BF16)

    return _call(name, body, (T // tm,), [_rows(dh, tm), _full(w), _rows(a, tm)], [_out_rows(T, K, BF16, tm)], hub=hub)[0]


def _concat_bf16(*refs):
    vals = [r[...].astype(BF16) for r in refs]
    return vals[0] if len(vals) == 1 else jnp.concatenate(vals, axis=1)


def _nt_plain(name, dy, w, tm=MM_ROWS):
    T, N = dy.shape
    if w.ndim == 3:
        K = w.shape[1]
        _, w_scratch, prepare, chunks = _col_weight(w)
    else:
        K = w.shape[0]
        w_scratch, prepare = [], (lambda w_ref, s_ref, step: None)
        chunks = lambda w_ref, s_ref: [(lo, hi, (lambda lo=lo, hi=hi: w_ref[:, lo:hi])) for lo, hi in _chunks(N)]

    def body(dy_ref, w_ref, o_ref, acc_ref, *s):
        s_ref = s[0] if s else None
        prepare(w_ref, s_ref, pl.program_id(0))
        for n, (lo, hi, load) in enumerate(chunks(w_ref, s_ref)):
            d = _dot_nt(dy_ref[:, lo:hi].astype(BF16), load())
            if n == 0:
                acc_ref[...] = d
            else:
                acc_ref[...] += d
        o_ref[...] = acc_ref[...]

    return _call(name, body, (T // tm,), [_rows(dy, tm), _full(w)], [_out_rows(T, K, F32, tm)],
                 scratch=[pltpu.VMEM((tm, K), F32)] + w_scratch)[0]


def _att_out_bwd(name, dy, w, o, tm=MM_ROWS):
    T, _ = dy.shape
    K = w.shape[1]
    _, w_scratch, prepare, chunks = _col_weight(w)

    def body(dy_ref, w_ref, o_ref, do_ref, dl_ref, acc_ref, *s):
        s_ref = s[0] if s else None
        prepare(w_ref, s_ref, pl.program_id(0))
        for n, (lo, hi, load) in enumerate(chunks(w_ref, s_ref)):
            d = _dot_nt(dy_ref[:, lo:hi].astype(BF16), load())
            if n == 0:
                acc_ref[...] = d
            else:
                acc_ref[...] += d
        do = acc_ref[...]
        do_ref[...] = do
        prod = do * o_ref[...]
        high = prod.astype(BF16)
        low = (prod - high.astype(F32)).astype(BF16)
        head_of = lambda axis: jnp.right_shift(lax.broadcasted_iota(jnp.int32, (K, K), axis), HEAD_DIM.bit_length() - 1)
        same_head = jnp.where(head_of(0) == head_of(1), 1.0, 0.0).astype(BF16)
        dl_ref[...] = _dot(high, same_head) + _dot(low, same_head)

    outs = [_out_rows(T, K, F32, tm), _out_rows(T, K, F32, tm)]
    return _call(name, body, (T // tm,), [_rows(dy, tm), _full(w), _rows(o, tm)], outs,
                 scratch=[pltpu.VMEM((tm, K), F32)] + w_scratch)


def _nt_norm_bwd(name, dys, w, h, g, dh_in, tm=MM_ROWS, hub=None):
    T, D = h.shape
    _, w_scratch, prepare, chunks = _col_weight(w)
    n_steps = T // tm
    n_dy = len(dys)

    def body(*refs):
        dy_refs = refs[:n_dy]
        w_ref, h_ref, g_ref, dhin_ref, o_ref, dg_ref, acc_ref = refs[n_dy:n_dy + 7]
        s_ref = refs[n_dy + 7] if len(refs) > n_dy + 7 else None
        i = pl.program_id(0)
        prepare(w_ref, s_ref, i)
        dy = _concat_bf16(*dy_refs)
        for n, (lo, hi, load) in enumerate(chunks(w_ref, s_ref)):
            d = _dot_nt(dy[:, lo:hi], load())
            if n == 0:
                acc_ref[...] = d
            else:
                acc_ref[...] += d
        dn = acc_ref[...]
        x = h_ref[...]
        rstd = _rstd(x)
        xhat = x * rstd
        dxhat = dn * g_ref[...]
        dx = rstd * (dxhat - xhat * jnp.mean(dxhat * xhat, axis=-1, keepdims=True))
        o_ref[...] = dhin_ref[...] + dx
        _acc8(dg_ref, dn * xhat, i, n_steps)

    ins = [_rows(d, tm) for d in dys] + [_full(w), _rows(h, tm), _full(g), _rows(dh_in, tm)]
    outs = [_out_rows(T, D, F32, tm), _out_acc8(D)]
    dh, dg = _call(name, body, (n_steps,), ins, outs, scratch=[pltpu.VMEM((tm, D), F32)] + w_scratch, hub=hub)
    return dh, dg[0:1]


def _cols2(a_t, tt, kb=None):
    kb = kb or a_t.shape[0]
    return (a_t, pl.BlockSpec((kb, tt), (lambda s, t: (s, t)) if kb != a_t.shape[0] else (lambda s, t: (0, t))))


def _tn(name, a_ins, a_fn, y_ins, y_fn, K, N, T, tt, split=None, out_cols=None, hub=None):
    kind, parts = split or ("n", 1)
    kb, nb = (K // parts, N) if kind == "k" else (K, N // parts)
    n_steps = T // tt
    n_a = len(a_ins)
    n_y = len(y_ins)
    assert out_cols is None or (kind == "n" and nb % out_cols == 0)

    def body(*refs):
        a_refs = refs[:n_a]
        y_refs = refs[n_a:n_a + n_y]
        o_ref, acc_ref = refs[n_a + n_y:]
        t = pl.program_id(1)
        a_t = a_refs[0][...] if a_fn is None else a_fn(*a_refs).T.astype(BF16)
        y = y_fn(*y_refs).astype(BF16)
        for lo, hi in _chunks(nb):
            d = _dot(a_t, y[:, lo:hi])

            @pl.when(t == 0)
            def _():
                acc_ref[:, lo:hi] = d

            @pl.when(t > 0)
            def _():
                acc_ref[:, lo:hi] += d

        @pl.when(t == n_steps - 1)
        def _():
            if out_cols is None:
                o_ref[...] = acc_ref[...].astype(BF16)
            else:
                for j in range(nb // out_cols):
                    o_ref[j] = acc_ref[:, j * out_cols:(j + 1) * out_cols].astype(BF16)

    if out_cols is None:
        out = (SDS((K, N), BF16), pl.BlockSpec((kb, nb), (lambda s, t: (s, 0)) if kind == "k" else (lambda s, t: (0, s))))
    else:
        out = (SDS((N // out_cols, K, out_cols), BF16), pl.BlockSpec((nb // out_cols, K, out_cols), lambda s, t: (s, 0, 0)))
    return _call(name, body, (parts, n_steps), list(a_ins) + list(y_ins), [out],
                 scratch=[pltpu.VMEM((kb, nb), F32)], hub=hub)[0]


def _val(ref):
    return ref[...]


def _concat_f32(*refs):
    vals = [r[...] for r in refs]
    return vals[0] if len(vals) == 1 else jnp.concatenate(vals, axis=1)


ATT_TILE_ROWS = 2048
HEAD_PAIRS = H_G // 2
ATT_SCALE = HEAD_DIM ** -0.5
ATT_UNITS_TOGETHER = 4


def _slope(h):
    return 2.0 ** (-ALIBI_MAX_BIAS * (h + 1) / H_G)


def _att_geom(T, bl, g):
    dil = PATTERNS[g][1]
    sub = ATT_BLK * dil
    nsub = max(1, ATT_TILE_ROWS // sub)
    rows = sub * nsub
    return dil, sub, nsub, rows, T // bl // rows


def _att_specs(T, bl, g):
    _, sub, nsub, rows, nt = _att_geom(T, bl, g)
    last_sub = T // sub - 1
    tile = lambda col: pl.BlockSpec((rows, 128), lambda b, i, hp: (b * nt + i, col(hp)))
    prev = lambda col: pl.BlockSpec((sub, 128), lambda b, i, hp: (jnp.maximum((b * nt + i) * nsub - 1, 0), col(hp)))
    nxt = lambda col: pl.BlockSpec((sub, 128), lambda b, i, hp: (jnp.minimum((b * nt + i + 1) * nsub, last_sub), col(hp)))
    return tile, prev, nxt


def _sub_rows(j, r, dil):
    start = j * ATT_BLK * dil + r
    return pl.ds(start, ATT_BLK, stride=dil) if dil > 1 else pl.ds(start, ATT_BLK)


class _Residues:
    def __init__(self, dil):
        self.dil = dil
        self.whole = dil % SUBLANES == 0
        self.read, self.written = {}, {}

    def _block(self, j):
        return pl.ds(j * ATT_BLK * self.dil, ATT_BLK * self.dil)

    def load(self, ref, j, r):
        if not self.whole:
            return ref[_sub_rows(j, r, self.dil), :]
        if (id(ref), j) not in self.read:
            rows = ref[self._block(j), :]
            self.read[id(ref), j] = jnp.swapaxes(rows.reshape(ATT_BLK, self.dil, rows.shape[-1]), 0, 1)
        return self.read[id(ref), j][r]

    def store(self, ref, j, r, val):
        if not self.whole:
            ref[_sub_rows(j, r, self.dil), :] = val
            return
        got = self.written.setdefault((id(ref), j), {})
        got[r] = val
        if len(got) == self.dil:
            merged = jnp.swapaxes(jnp.stack([got[k] for k in range(self.dil)], axis=0), 0, 1)
            ref[self._block(j), :] = merged.reshape(ATT_BLK * self.dil, val.shape[-1])
            del self.written[id(ref), j]


def _att_consts(hp, dil, keys_first=False):
    h0 = lax.broadcasted_iota(jnp.int32, (ATT_BLK, 128), 1) < HEAD_DIM
    a = lax.broadcasted_iota(jnp.int32, (ATT_BLK, ATT_BLK), 1 if keys_first else 0)
    c = lax.broadcasted_iota(jnp.int32, (ATT_BLK, ATT_BLK), 0 if keys_first else 1)
    dist_p = ((ATT_BLK + a - c) * dil).astype(F32)
    dist_c = ((a - c) * dil).astype(F32)
    bias_p, bias_c = [], []
    for h in range(2):
        slope = jnp.float32(_slope(2 * (HEAD_PAIRS - 1) + h))
        for p in range(HEAD_PAIRS - 2, -1, -1):
            slope = jnp.where(hp == p, jnp.float32(_slope(2 * p + h)), slope)
        bias_p.append(jnp.where(c >= a, -slope * dist_p, NEG_INF))
        bias_c.append(jnp.where(c <= a, -slope * dist_c, NEG_INF))
    return h0, jnp.concatenate(bias_p, axis=0), jnp.concatenate(bias_c, axis=0)


def _stack_heads(x, h0):
    return jnp.concatenate([jnp.where(h0, x, 0.0), jnp.where(h0, 0.0, x)], axis=0).astype(BF16)


def _unstack_heads(x, h0):
    return jnp.where(h0, x[:ATT_BLK], x[ATT_BLK:])


def _stack_cols(x):
    return jnp.concatenate(_head_cols(x), axis=0)


def _split_heads(x, h0):
    return [jnp.where(h0, x, 0.0).astype(BF16), jnp.where(h0, 0.0, x).astype(BF16)]


def _head_cols(x):
    return [x[:, 0:1], x[:, HEAD_DIM:HEAD_DIM + 1]]


def _in_groups(units, first_stage, *later_stages):
    for u0 in range(0, len(units), ATT_UNITS_TOGETHER):
        staged = [first_stage(*u) for u in units[u0:u0 + ATT_UNITS_TOGETHER]]
        for stage in later_stages:
            staged = [stage(*s) for s in staged]


def _attn_fwd(name, q, kv, g, bl, hub=None):
    T = q.shape[0]
    dil, _, nsub, _, _ = _att_geom(T, bl, g)
    tile, prev, _ = _att_specs(T, bl, g)

    def body(q_ref, kp_ref, kc_ref, vp_ref, vc_ref, o_ref, lse_ref):
        first = pl.program_id(1) == 0
        h0, bias_p, bias_c = _att_consts(pl.program_id(2), dil)
        bias_first = jnp.where(first, NEG_INF, bias_p)
        ones = jnp.ones((ATT_BLK, 128), BF16)
        rows = _Residues(dil)

        def scores(j, r):
            if j == 0:
                kp, vp, bp = rows.load(kp_ref, 0, r), rows.load(vp_ref, 0, r), bias_first
            else:
                kp, vp, bp = rows.load(kc_ref, j - 1, r), rows.load(vc_ref, j - 1, r), bias_p
            kp, kc = kp.astype(BF16), rows.load(kc_ref, j, r).astype(BF16)
            qs = _stack_heads(rows.load(q_ref, j, r) * ATT_SCALE, h0)
            sp = _dot_nt(qs, kp) + bp
            sc = _dot_nt(qs, kc) + bias_c
            return (j, r), sp, sc, vp.astype(BF16), rows.load(vc_ref, j, r).astype(BF16)

        def weights(unit, sp, sc, vp, vc):
            mx = jnp.max(jnp.maximum(sp, sc), axis=-1, keepdims=True)
            return unit, mx, jnp.exp(sp - mx).astype(BF16), jnp.exp(sc - mx).astype(BF16), vp, vc

        def outputs(unit, mx, ep, ec, vp, vc):
            den = _dot(ep, ones) + _dot(ec, ones)
            acc = _dot(ep, vp) + _dot(ec, vc)
            rows.store(o_ref, *unit, _unstack_heads(acc / den, h0))
            rows.store(lse_ref, *unit, _unstack_heads(mx + jnp.log(den), h0))
            return ()

        _in_groups([(j, r) for j in range(nsub) for r in range(dil)], scores, weights, outputs)

    ins = [(q, tile(lambda hp: 4 * g + hp)), (kv, prev(lambda hp: 8 * g + hp)), (kv, tile(lambda hp: 8 * g + hp)),
           (kv, prev(lambda hp: 8 * g + 4 + hp)), (kv, tile(lambda hp: 8 * g + 4 + hp))]
    out = (SDS((T, QW), F32), tile(lambda hp: hp))
    _, _, _, _, nt = _att_geom(T, bl, g)
    return _call(name, body, (bl, nt, HEAD_PAIRS), ins, [out, out], hub=hub)


def _combine(name, os_, lses, tm=512):
    T = os_[0].shape[0]

    def body(o0, o1, o2, l0, l1, l2, o_ref, lse_ref):
        ls = [l0[...], l1[...], l2[...]]
        mx = jnp.maximum(jnp.maximum(ls[0], ls[1]), ls[2])
        es = [jnp.exp(l - mx) for l in ls]
        den = es[0] + es[1] + es[2]
        o_ref[...] = (es[0] * o0[...] + es[1] * o1[...] + es[2] * o2[...]) / den
        lse_ref[...] = mx + jnp.log(den)

    ins = [_rows(t, tm) for t in list(os_) + list(lses)]
    return _call(name, body, (T // tm,), ins, [_out_rows(T, QW, F32, tm), _out_rows(T, QW, F32, tm)])


def _attn_bwd_dq(name, q, kv, do, delta, lse, g, bl, hub=None):
    T = q.shape[0]
    dil, _, nsub, _, nt = _att_geom(T, bl, g)
    tile, prev, _ = _att_specs(T, bl, g)

    def body(q_ref, kp_ref, kc_ref, vp_ref, vc_ref, do_ref, dl_ref, lse_ref, dq_ref):
        first = pl.program_id(1) == 0
        h0, bias_p, bias_c = _att_consts(pl.program_id(2), dil)
        bias_first = jnp.where(first, NEG_INF, bias_p)
        rows = _Residues(dil)

        def probs(j, r):
            if j == 0:
                kp, vp, bp = rows.load(kp_ref, 0, r), rows.load(vp_ref, 0, r), bias_first
            else:
                kp, vp, bp = rows.load(kc_ref, j - 1, r), rows.load(vc_ref, j - 1, r), bias_p
            kp, vp = kp.astype(BF16), vp.astype(BF16)
            kc, vc = rows.load(kc_ref, j, r).astype(BF16), rows.load(vc_ref, j, r).astype(BF16)
            qs = _stack_heads(rows.load(q_ref, j, r) * ATT_SCALE, h0)
            dos = _stack_heads(rows.load(do_ref, j, r), h0)
            lse = _stack_cols(rows.load(lse_ref, j, r))
            pp = jnp.exp(_dot_nt(qs, kp) + bp - lse)
            pc = jnp.exp(_dot_nt(qs, kc) + bias_c - lse)
            return (j, r), pp, pc, _dot_nt(dos, vp), _dot_nt(dos, vc), kp, kc

        def dscores(unit, pp, pc, dpp, dpc, kp, kc):
            dl = _stack_cols(rows.load(dl_ref, *unit))
            return unit, (pp * (dpp - dl)).astype(BF16), (pc * (dpc - dl)).astype(BF16), kp, kc

        def outputs(unit, dsp, dsc, kp, kc):
            rows.store(dq_ref, *unit, _unstack_heads(_dot(dsp, kp) + _dot(dsc, kc), h0) * ATT_SCALE)
            return ()

        _in_groups([(j, r) for j in range(nsub) for r in range(dil)], probs, dscores, outputs)

    own = lambda hp: hp
    ins = [(q, tile(lambda hp: 4 * g + hp)), (kv, prev(lambda hp: 8 * g + hp)), (kv, tile(lambda hp: 8 * g + hp)),
           (kv, prev(lambda hp: 8 * g + 4 + hp)), (kv, tile(lambda hp: 8 * g + 4 + hp)),
           (do, tile(own)), (delta, tile(own)), (lse, tile(own))]
    return _call(name, body, (bl, nt, HEAD_PAIRS), ins, [(SDS((T, QW), F32), tile(own))], hub=hub)[0]


def _attn_bwd_dkv(name, q, kv, do, delta, lse, g, bl, prev=None, hub=None):
    T = q.shape[0]
    dil, _, nsub, _, nt = _att_geom(T, bl, g)
    tile, _, nxt = _att_specs(T, bl, g)
    has_prev = prev is not None

    def body(*refs):
        k_ref, v_ref, q_ref, qn_ref, do_ref, don_ref, dl_ref, dln_ref, l_ref, ln_ref = refs[:10]
        rest = refs[10:]
        if has_prev:
            dkp_ref, dvp_ref, dk_ref, dv_ref = rest
        else:
            dk_ref, dv_ref = rest
        last = pl.program_id(1) == nt - 1
        h0, bias_p, bias_c = _att_consts(pl.program_id(2), dil, keys_first=True)
        bias_last = jnp.where(last, NEG_INF, bias_p)
        rows = _Residues(dil)

        def per_query_rows(x):
            xt = x.T
            return jnp.concatenate([jnp.broadcast_to(xt[0:1], (ATT_BLK, ATT_BLK)),
                                    jnp.broadcast_to(xt[HEAD_DIM:HEAD_DIM + 1], (ATT_BLK, ATT_BLK))], axis=0)

        def probs(j, r):
            ks = _stack_heads(rows.load(k_ref, j, r), h0)
            vs = _stack_heads(rows.load(v_ref, j, r), h0)
            sets = [(q_ref, do_ref, dl_ref, l_ref, j, bias_c)]
            if j < nsub - 1:
                sets.append((q_ref, do_ref, dl_ref, l_ref, j + 1, bias_p))
            else:
                sets.append((qn_ref, don_ref, dln_ref, ln_ref, 0, bias_last))
            out = []
            for qr, dor, dlr, lr, jq, bias in sets:
                qsb = (rows.load(qr, jq, r) * ATT_SCALE).astype(BF16)
                do2b = rows.load(dor, jq, r).astype(BF16)
                p = jnp.exp(_dot_nt(ks, qsb) + bias - per_query_rows(rows.load(lr, jq, r)))
                out.append((p, _dot_nt(vs, do2b), dlr, jq, qsb, do2b))
            return (j, r), out

        def dscores(unit, sets):
            out = []
            for p, dp, dlr, jq, qsb, do2b in sets:
                ds = (p * (dp - per_query_rows(rows.load(dlr, jq, unit[1])))).astype(BF16)
                out.append((p.astype(BF16), ds, qsb, do2b))
            return unit, out

        def outputs(unit, sets):
            dk_st = dv_st = None
            for pb, ds, qsb, do2b in sets:
                dvs, dks = _dot(pb, do2b), _dot(ds, qsb)
                dv_st = dvs if dv_st is None else dv_st + dvs
                dk_st = dks if dk_st is None else dk_st + dks
            dk2 = _unstack_heads(dk_st, h0)
            dv2 = _unstack_heads(dv_st, h0)
            if has_prev:
                dk2 = dk2 + rows.load(dkp_ref, *unit)
                dv2 = dv2 + rows.load(dvp_ref, *unit)
            rows.store(dk_ref, *unit, dk2)
            rows.store(dv_ref, *unit, dv2)
            return ()

        _in_groups([(j, r) for j in range(nsub) for r in range(dil)], probs, dscores, outputs)

    own = lambda hp: hp
    qcol = lambda hp: 4 * g + hp
    ins = [(kv, tile(lambda hp: 8 * g + hp)), (kv, tile(lambda hp: 8 * g + 4 + hp)), (q, tile(qcol)), (q, nxt(qcol)),
           (do, tile(own)), (do, nxt(own)), (delta, tile(own)), (delta, nxt(own)), (lse, tile(own)), (lse, nxt(own))]
    if has_prev:
        ins += [(prev[0], tile(own)), (prev[1], tile(own))]
    out = (SDS((T, QW), F32), tile(own))
    return _call(name, body, (bl, nt, HEAD_PAIRS), ins, [out, out], hub=hub)


def _final_loss(name, h, tgt, g, tm=256):
    T, D = h.shape
    n_steps = T // tm

    def body(h_ref, t_ref, g_ref, dh_ref, loss_ref, dg_ref, sq_ref):
        i = pl.program_id(0)
        x = h_ref[...]
        rstd = _rstd(x)
        xhat = x * rstd
        err = xhat * g_ref[...] - t_ref[...]
        _acc8(sq_ref, err * err, i, n_steps)
        dy = err * (1.0 / D)
        dxhat = dy * g_ref[...]
        dh_ref[...] = rstd * (dxhat - xhat * jnp.mean(dxhat * xhat, axis=-1, keepdims=True))
        _acc8(dg_ref, dy * xhat, i, n_steps)

        @pl.when(i == n_steps - 1)
        def _():
            loss_ref[...] = jnp.full(loss_ref.shape, jnp.sum(sq_ref[0:1, :]), F32)

    outs = [_out_rows(T, D, F32, tm), (SDS((SUBLANES, 128), F32), pl.BlockSpec((SUBLANES, 128), lambda i: (0, 0))),
            _out_acc8(D)]
    dh, loss, dg = _call(name, body, (n_steps,), [_rows(h, tm), _rows(tgt, tm), _full(g)], outs,
                         scratch=[pltpu.VMEM((SUBLANES, D), F32)])
    return dh, loss[0, 0], dg[0:1]


def _conv_bwd(name, bcu, dgated, cw, seq, tm=256, hub=None):
    T, D = dgated.shape
    n_steps = T // tm

    def body(b_ref, c_ref, u_ref, ch_ref, uh_ref, dg_ref, dgn_ref, bn_ref, cw_ref, o_ref, t0_ref, t1_ref, t2_ref):
        i = pl.program_id(0)
        first = (i * tm) % seq == 0
        last = ((i + 1) * tm) % seq == 0
        _, (b, c, u), conv, (cu, cu1, cu2) = _gate(b_ref, c_ref, u_ref, ch_ref, uh_ref, cw_ref, first)
        dgat = dg_ref[...]
        dconv = dgat * b
        nxt = jnp.where(last, 0.0, dgn_ref[...] * bn_ref[...].astype(F32))
        rows = lax.broadcasted_iota(jnp.int32, dconv.shape, 0)
        n1 = nxt[0:1, :]
        n2 = nxt[1:2, :]
        dc1 = jnp.where(rows == tm - 1, n1, pltpu.roll(dconv, tm - 1, 0))
        dc2 = jnp.where(rows == tm - 1, n2, jnp.where(rows == tm - 2, n1, pltpu.roll(dconv, tm - 2, 0)))
        dcu = cw_ref[0:1, :] * dconv + cw_ref[1:2, :] * dc1 + cw_ref[2:3, :] * dc2
        o_ref[:, 0:D] = (dgat * conv).astype(BF16)
        o_ref[:, D:2 * D] = (dcu * u).astype(BF16)
        o_ref[:, 2 * D:3 * D] = (dcu * c).astype(BF16)
        _acc8(t0_ref, dconv * cu, i, n_steps)
        _acc8(t1_ref, dconv * cu1, i, n_steps)
        _acc8(t2_ref, dconv * cu2, i, n_steps)

    ins = [_rows(bcu, tm, D, 0), _rows(bcu, tm, D, 1), _rows(bcu, tm, D, 2), _prev8(bcu, tm, D, 1), _prev8(bcu, tm, D, 2),
           _rows(dgated, tm), _next8(dgated, tm, D, 0), _next8(bcu, tm, D, 0), _full(cw)]
    outs = [_out_rows(T, 3 * D, BF16, tm), _out_acc8(D), _out_acc8(D), _out_acc8(D)]
    dbcu, t0, t1, t2 = _call(name, body, (n_steps,), ins, outs, hub=hub)
    return dbcu, jnp.concatenate([t0[0:1], t1[0:1], t2[0:1]], axis=0)


def _sum8_adamw(name, parts, w, m, v, tr):
    R, C = w.shape
    b1c = 1.0 - ADAM_B1 ** ADAM_STEP
    b2c = 1.0 - ADAM_B2 ** ADAM_STEP

    def body(p_ref, w_ref, m_ref, v_ref, g_ref, d_ref, nm_ref, nv_ref):
        g = p_ref[0].astype(F32)
        for j in range(1, N_DEV):
            g = g + p_ref[j].astype(F32)
        nm = ADAM_B1 * m_ref[...] + (1.0 - ADAM_B1) * g
        nv = ADAM_B2 * v_ref[...] + (1.0 - ADAM_B2) * (g * g)
        m_hat = nm / b1c
        v_hat = nv / b2c
        g_ref[...] = g
        d_ref[...] = -ADAM_LR * (m_hat / (jnp.sqrt(v_hat) + ADAM_EPS) + ADAM_WD * w_ref[...])
        nm_ref[...] = nm
        nv_ref[...] = nv

    ins = [(parts, pl.BlockSpec((N_DEV, tr, C), lambda i: (0, i, 0))), _rows(w, tr), _rows(m, tr), _rows(v, tr)]
    outs = [_out_rows(R, C, F32, tr)] * 4
    return _call(name, body, (R // tr,), ins, outs)


def _all_gather(name, items):
    n = len(items)
    shapes = [tuple(a.shape if idx is None else a.shape[1:]) for a, idx in items]

    def body(*refs):
        x_refs, out_refs = refs[:n], refs[n:2 * n]
        send_sems, recv_sems, local_sems = refs[2 * n:]
        x, y, c = _mesh_pos()
        me, sibling = (x, y, c), (x, y, 1 - c)
        chips = [(1 - x, y), (x, 1 - y), (1 - x, 1 - y)]

        def copy(t, k, block, to, own=False):
            dst = out_refs[t].at[4 * block[0] + 2 * block[1] + block[2]]
            src = dst
            if own:
                src = x_refs[t] if items[t][1] is None else x_refs[t].at[items[t][1]]
            return pltpu.make_async_remote_copy(
                src_ref=src, dst_ref=dst, send_sem=send_sems.at[t, k], recv_sem=recv_sems.at[t, k],
                device_id=to, device_id_type=pl.DeviceIdType.MESH)

        started = []
        for t in range(n):
            src = x_refs[t] if items[t][1] is None else x_refs[t].at[items[t][1]]
            mine = pltpu.make_async_copy(src, out_refs[t].at[4 * x + 2 * y + c], local_sems.at[t])
            mine.start()
            first = [copy(t, 0, me, sibling, own=True)]
            first += [copy(t, 1 + j, me, (*chip, c), own=True) for j, chip in enumerate(chips)]
            for cp in first:
                cp.start()
            started.append((mine, first))
        passed = []
        for t in range(n):
            for j, chip in enumerate(chips):
                copy(t, 1 + j, (*chip, c), me).wait_recv()
                fwd = copy(t, 4 + j, (*chip, c), sibling)
                fwd.start()
                passed.append(fwd)
        for t in range(n):
            copy(t, 0, sibling, me).wait_recv()
            for j, chip in enumerate(chips):
                copy(t, 4 + j, (*chip, 1 - c), me).wait_recv()
        for mine, first in started:
            for cp in first:
                cp.wait_send()
            mine.wait()
        for cp in passed:
            cp.wait_send()

    any_spec = pl.BlockSpec(memory_space=pl.ANY)
    return pl.pallas_call(
        body, name=name,
        out_shape=[SDS((N_DEV,) + s, a.dtype) for s, (a, _) in zip(shapes, items)],
        in_specs=[any_spec] * n,
        out_specs=[any_spec] * n,
        scratch_shapes=[pltpu.SemaphoreType.DMA((n, 7)), pltpu.SemaphoreType.DMA((n, 7)), pltpu.SemaphoreType.DMA((n,))],
    )(*[a for a, _ in items])


def _pad8(t):
    return jnp.pad(t, ((0, SUBLANES - t.shape[0]), (0, 0)))


def _rows_merged(w):
    return w.reshape(w.shape[0] * w.shape[1], w.shape[2])


def _local_grads(x, tgt, norm_mix, norm_mlp, norm_kv, norm_final, conv_w, hub):
    bl, seq, D = x.shape
    T = bl * seq
    h = x.reshape(T, D)
    tgt = tgt.reshape(T, D)
    row = lambda t, l: t[l:l + 1]
    W = hub.weights
    saved = []
    kv = h_kv = hn_kv_t = None
    for l in range(DEPTH):
        if l < N_A_LAYERS:
            bcu, hn_t = _norm_mm(f"l{l}_in", h, row(norm_mix, l), W["w_a_in", l], out_dtype=BF16, transposed=True, hub=hub)
            h2, gated_t = _gate_mm_res(f"l{l}_conv_out", bcu, _pad8(conv_w[l]), _rows_merged(W["w_a_out", l]), h, seq, hub=hub)
            saved.append((h, bcu, gated_t, hn_t))
        else:
            i = l - N_A_LAYERS
            if l == N_A_LAYERS:
                h_kv = h
                kv, hn_kv_t = _norm_mm("kv", h, norm_kv.reshape(1, D), W["w_kv", None], transposed=True, hub=hub)
            q, hn_t = _norm_mm(f"l{l}_q", h, row(norm_mix, l), W["w_q", i], transposed=True)
            per_group = [_attn_fwd(f"l{l}_att{g}", q, kv, g, bl, hub=hub) for g in range(N_GROUPS)]
            o, lse = _combine(f"l{l}_combine", [p[0] for p in per_group], [p[1] for p in per_group])
            h2 = _mm_res(f"l{l}_att_out", o, W["w_o", i], h)
            saved.append((h, q, o, lse, hn_t))
        a = _norm_mm(f"l{l}_up", h2, row(norm_mlp, l), W["w_up", l], out_dtype=BF16, hub=hub)
        h = _relu2_mm_res(f"l{l}_down", a, _rows_merged(W["w_down", l]), h2, hub=hub)
        saved[-1] = saved[-1] + (h2, a)

    dh, sq_err, d_norm_final = _final_loss("loss", h, tgt, norm_final.reshape(1, D))

    d_norm_mix = [None] * DEPTH
    d_norm_mlp = [None] * DEPTH
    d_conv = [None] * N_A_LAYERS
    d_norm_kv = None
    dkv_acc = [None] * N_GROUPS
    G = hub.grads
    as_slots = lambda g: g.reshape(N_DEV, g.shape[0] // N_DEV, g.shape[1])
    tt = DW_TOKENS
    for l in reversed(range(DEPTH)):
        h2, a = saved[l][-2:]
        h_in = saved[l][0]
        g_mlp = row(norm_mlp, l)
        g_mix = row(norm_mix, l)
        w_up_l = W["w_up", l]
        FF = N_DEV * w_up_l.shape[2]
        da = _nt_relu2_bwd(f"l{l}_down_bwd", dh, _rows_merged(W["w_down", l]), a, hub=hub)
        G["w_down", l] = as_slots(_tn(f"l{l}_dw_down", [_rows2(a, tt, FF // 2, lambda s: s)], _relu2,
                                      [_rows2(dh, tt)], _val, FF, D, T, tt, split=("k", 2)))
        G["w_up", l] = _tn(f"l{l}_dw_up", [_rows2(h2, 2 * tt), _full2(g_mlp)], _normed,
                           [_rows2(da, 2 * tt, FF // 4, lambda s: s)], _val, D, FF, T, 2 * tt, split=("n", 4),
                           out_cols=w_up_l.shape[2], hub=hub)
        dh2, d_norm_mlp[l] = _nt_norm_bwd(f"l{l}_up_bwd", [da], w_up_l, h2, g_mlp, dh, hub=hub)
        if l >= N_A_LAYERS:
            i = l - N_A_LAYERS
            _, q, o, lse, hn_t = saved[l][:5]
            w_o_i, w_q_i = W["w_o", i], W["w_q", i]
            do, delta = _att_out_bwd(f"l{l}_att_out_bwd", dh2, w_o_i, o)
            G["w_o", i] = _tn(f"l{l}_dw_o", [_rows2(o, tt)], _val, [_rows2(dh2, tt)], _val, QW, D, T, tt,
                              out_cols=w_o_i.shape[2])
            dqs = []
            for g in range(N_GROUPS):
                dqs.append(_attn_bwd_dq(f"l{l}_att{g}_dq", q, kv, do, delta, lse, g, bl, hub=hub))
                dkv_acc[g] = _attn_bwd_dkv(f"l{l}_att{g}_dkv", q, kv, do, delta, lse, g, bl, prev=dkv_acc[g], hub=hub)
            G["w_q", i] = _tn(f"l{l}_dw_q", [_cols2(hn_t, tt)], None,
                              [_rows2(t, tt) for t in dqs], _concat_f32, D, N_GROUPS * QW, T, tt, out_cols=w_q_i.shape[2])
            dh, d_norm_mix[l] = _nt_norm_bwd(f"l{l}_q_bwd", dqs, w_q_i, h_in, g_mix, dh2, hub=hub)
            if l == N_A_LAYERS:
                dkvs = [t for pair in dkv_acc for t in pair]
                g_kv = norm_kv.reshape(1, D)
                w_kv = W["w_kv", None]
                per_call = len(dkvs) // 2
                halves = [_tn(f"dw_kv{p}", [_cols2(hn_kv_t, tt)], None,
                              [_rows2(t, tt) for t in dkvs[p * per_call:(p + 1) * per_call]], _concat_f32,
                              D, per_call * QW, T, tt, out_cols=w_kv.shape[2]) for p in range(2)]
                G["w_kv", None] = jnp.concatenate(halves, axis=0)
                dh, d_norm_kv = _nt_norm_bwd("kv_bwd", dkvs, w_kv, h_kv, g_kv, dh, tm=MM_ROWS // 2, hub=hub)
        else:
            _, bcu, gated_t, hn_t = saved[l][:4]
            cw = _pad8(conv_w[l])
            w_in_l = W["w_a_in", l]
            dgated = _nt_plain(f"l{l}_conv_out_bwd", dh2, _rows_merged(W["w_a_out", l]))
            G["w_a_out", l] = as_slots(_tn(f"l{l}_dw_conv_out", [_cols2(gated_t, 2 * tt)], None,
                                           [_rows2(dh2, 2 * tt)], _val, D, D, T, 2 * tt, hub=hub))
            dbcu, d_conv[l] = _conv_bwd(f"l{l}_conv_bwd", bcu, dgated, cw, seq, hub=hub)
            G["w_a_in", l] = _tn(f"l{l}_dw_in", [_cols2(hn_t, 2 * tt)], None,
                                 [_rows2(dbcu, 2 * tt, 3 * D // 2, lambda s: s)], _val, D, 3 * D, T, 2 * tt, split=("n", 2),
                                 out_cols=w_in_l.shape[2], hub=hub)
            dh, d_norm_mix[l] = _nt_norm_bwd(f"l{l}_in_bwd", [dbcu], w_in_l, h_in, g_mix, dh2, hub=hub)

    small = jnp.concatenate(d_norm_mix + d_norm_mlp + [d_norm_kv, d_norm_final] + d_conv, axis=0)
    return sq_err, dh.reshape(bl, seq, D), small


def kernel(x, norm_mix, norm_mlp, w_a_in, conv_w, w_a_out, norm_kv, w_kv, w_q, w_o, w_up, w_down, norm_final, loss_target, m_norm_mix, m_norm_mlp, m_w_a_in, m_conv_w, m_w_a_out, m_norm_kv, m_w_kv, m_w_q, m_w_o, m_w_up, m_w_down, m_norm_final, v_norm_mix, v_norm_mlp, v_w_a_in, v_conv_w, v_w_a_out, v_norm_kv, v_w_kv, v_w_q, v_w_o, v_w_up, v_w_down, v_norm_final):
    D = x.shape[-1]
    xi, yi, ci = _mesh_pos()
    me_idx = 4 * xi + 2 * yi + ci
    w_big = dict(w_a_in=w_a_in, w_a_out=w_a_out, w_kv=w_kv, w_q=w_q, w_o=w_o, w_up=w_up, w_down=w_down)
    m_big = dict(w_a_in=m_w_a_in, w_a_out=m_w_a_out, w_kv=m_w_kv, w_q=m_w_q, w_o=m_w_o, w_up=m_w_up, w_down=m_w_down)
    v_big = dict(w_a_in=v_w_a_in, w_a_out=v_w_a_out, w_kv=v_w_kv, w_q=v_w_q, w_o=v_w_o, w_up=v_w_up, w_down=v_w_down)
    names = list(w_big)

    shards = {n: w.astype(BF16) for n, w in w_big.items()}
    landing = {n: lax.empty((N_DEV,) + w.shape, BF16) for n, w in w_big.items()}
    hub = _Hub(FETCH_DURING, PUSH_DURING, shards, landing)
    dc = conv_w.shape[-1]
    taps = conv_w.shape[0] * conv_w.shape[1]
    got = _all_gather("gather_first", [(shards[n], l) for n, l in FETCH_UP_FRONT] + [(_pad8(conv_w.reshape(taps, dc)), None)])
    for key, w in zip(FETCH_UP_FRONT, got):
        hub.weights[key] = w
    conv_full = jnp.moveaxis(got[-1][:, :taps], 0, 1).reshape(conv_w.shape[0], conv_w.shape[1], N_DEV * dc)

    sq_err, grad_x, small = _local_grads(x, loss_target, norm_mix, norm_mlp, norm_kv, norm_final, conv_full, hub)
    loss = lax.psum(sq_err * (0.5 / D), ("x", "y", "c"))

    grads, deltas, new_m, new_v = {}, {}, {}, {}
    for n in names:
        shape = w_big[n].shape
        cols = shape[-1]
        flat = lambda t: t.reshape(-1, cols)
        parts = hub.landing[n].reshape(N_DEV, -1, cols)
        outs = _sum8_adamw(f"adamw_{n}", parts, flat(w_big[n]), flat(m_big[n]), flat(v_big[n]), tr=min(256, parts.shape[1]))
        grads[n], deltas[n], new_m[n], new_v[n] = (t.reshape(shape) for t in outs)

    n_gain = 2 * DEPTH + 2
    rows_small = small.shape[0]
    small_all = _all_gather("gather_small_grads", [(small, None)])[0]

    def small_pack(nm, nl, nk, nf, cw):
        gains = jnp.concatenate([nm, nl, nk.reshape(1, D), nf.reshape(1, D)], axis=0)
        taps_full = lax.dynamic_update_slice(jnp.zeros((taps, D), F32), cw.reshape(taps, dc), (0, me_idx * dc))
        return jnp.concatenate([gains, taps_full], axis=0)

    sp = [small_pack(*t) for t in ((norm_mix, norm_mlp, norm_kv, norm_final, conv_w),
                                   (m_norm_mix, m_norm_mlp, m_norm_kv, m_norm_final, m_conv_w),
                                   (v_norm_mix, v_norm_mlp, v_norm_kv, v_norm_final, v_conv_w))]
    small_out = _sum8_adamw("adamw_small", small_all, *sp, tr=rows_small)

    def small_unpack(t):
        res = dict(norm_mix=t[0:DEPTH], norm_mlp=t[DEPTH:2 * DEPTH], norm_kv=t[2 * DEPTH], norm_final=t[2 * DEPTH + 1])
        res["conv_w"] = lax.dynamic_slice(t[n_gain:], (0, me_idx * dc), (taps, dc)).reshape(conv_w.shape)
        return res

    for dst, t in zip((grads, deltas, new_m, new_v), small_out):
        dst.update(small_unpack(t))

    order = ["norm_mix", "norm_mlp", "w_a_in", "conv_w", "w_a_out", "norm_kv", "w_kv", "w_q", "w_o", "w_up", "w_down",
             "norm_final"]
    return (loss, grad_x, *[grads[n] for n in order], *[deltas[n] for n in order], *[new_m[n] for n in order],
            *[new_v[n] for n in order])
```

```python
import functools

import jax
import jax.numpy as jnp
from jax import lax
from jax.experimental import pallas as pl
from jax.experimental.pallas import tpu as pltpu

F32 = jnp.float32
BF16 = jnp.bfloat16
SDS = jax.ShapeDtypeStruct

EPS = 1e-5
N_A_LAYERS = 2
DEPTH = 4
PATTERNS = ((128, 1), (512, 4), (2048, 16))
N_GROUPS = 3
H_G = 8
HEAD_DIM = 64
QW = H_G * HEAD_DIM
ATT_BLK = 128
ALIBI_MAX_BIAS = 8.0
NEG_INF = -1e30

ADAM_LR = 0.001
ADAM_B1 = 0.9
ADAM_B2 = 0.999
ADAM_EPS = 1e-08
ADAM_WD = 0.01
ADAM_STEP = 10

N_DEV = 8
SUBLANES = 8
HALO = 16
V7X_VMEM_LIMIT = 48 * 1024 * 1024
MXU_COLS = 256
MM_CHUNK = 512
MM_ROWS = 512
DW_TOKENS = 1024

FETCH_UP_FRONT = [("w_a_in", 0), ("w_a_out", 0)]
FETCH_DURING = {
    "l0_in": [("w_up", 0)], "l0_conv_out": [("w_down", 0)], "l0_up": [("w_a_in", 1), ("w_a_out", 1)], "l0_down": [("w_up", 1)],
    "l1_in": [("w_down", 1)], "l1_conv_out": [("w_kv", None)],
    "l1_up": [("w_q", 0), ("w_o", 0), ("w_q", 1), ("w_o", 1)], "l1_down": [("w_up", 2)],
    "kv": [("w_down", 2)], "l2_att0": [("w_up", 3)], "l2_att1": [("w_down", 3)],
}
PUSH_DURING = {
    "l3_dw_up": [("w_down", 3, 0, 2)], "l3_up_bwd": [("w_down", 3, 1, 2)],
    "l3_att0_dq": [("w_up", 3, 0, 2)], "l3_att0_dkv": [("w_up", 3, 1, 2)], "l3_att1_dq": [("w_o", 1)], "l3_q_bwd": [("w_q", 1)],
    "l2_dw_up": [("w_down", 2, 0, 2)], "l2_up_bwd": [("w_down", 2, 1, 2)],
    "l2_att0_dq": [("w_up", 2, 0, 2)], "l2_att0_dkv": [("w_up", 2, 1, 2)], "l2_att1_dq": [("w_o", 0)], "l2_q_bwd": [("w_q", 0)],
    "kv_bwd": [("w_kv", None, 0, 2)], "l1_down_bwd": [("w_kv", None, 1, 2)],
    "l1_dw_up": [("w_down", 1, 0, 2)], "l1_up_bwd": [("w_down", 1, 1, 2)], "l1_conv_bwd": [("w_up", 1, 0, 2)],
    "l1_dw_in": [("w_up", 1, 1, 2), ("w_a_out", 1)], "l1_in_bwd": [("w_a_in", 1, 0, 2)], "l0_down_bwd": [("w_a_in", 1, 1, 2)],
    "l0_dw_up": [("w_down", 0, 0, 2)], "l0_up_bwd": [("w_down", 0, 1, 2)], "l0_conv_bwd": [("w_up", 0, 0, 2)],
    "l0_dw_in": [("w_up", 0, 1, 2), ("w_a_out", 0)], "l0_in_bwd": [("w_a_in", 0)],
}


def _mesh_pos():
    return lax.axis_index("x"), lax.axis_index("y"), lax.axis_index("c")


def _flip(v, bit):
    return 1 - v if bit else v


class _Transfer:
    def __init__(self, kind, key, src, src_idx=None, dst=None, dst_idx=None, dst_shape=None, rows=None):
        self.kind, self.key, self.src, self.src_idx = kind, key, src, src_idx
        self.dst, self.dst_idx, self.dst_shape, self.rows = dst, dst_idx, dst_shape, rows

    def copies(self, src_ref, dst_ref, send_sems, recv_sems, local_sem):
        x, y, c = _mesh_pos()
        me = 4 * x + 2 * y + c
        part = (lambda r: r) if self.rows is None else (lambda r: r.at[pl.ds(*self.rows)])

        def dst_slot(j):
            r = dst_ref.at[j]
            return part(r if self.dst_idx is None else r.at[self.dst_idx])

        def copy(k, src, dst_j, to):
            return pltpu.make_async_remote_copy(
                src_ref=src, dst_ref=dst_slot(dst_j), send_sem=send_sems.at[k], recv_sem=recv_sems.at[k],
                device_id=to, device_id_type=pl.DeviceIdType.MESH)

        if self.kind == "exchange":
            local = pltpu.make_async_copy(part(src_ref.at[me]), dst_slot(me), local_sem)
            sends, arrivals = [], []
            for k in range(1, N_DEV):
                peer = (_flip(x, k & 4), _flip(y, k & 2), _flip(c, k & 1))
                peer_idx = 4 * peer[0] + 2 * peer[1] + peer[2]
                sends.append(copy(k - 1, part(src_ref.at[peer_idx]), me, peer))
                arrivals.append(copy(k - 1, part(src_ref.at[peer_idx]), peer_idx, peer))
            return local, sends, [], arrivals

        own = part(src_ref if self.src_idx is None else src_ref.at[self.src_idx])
        idx = lambda px, py, pc: 4 * px + 2 * py + pc
        sibling = (x, y, 1 - c)
        chips = [(1 - x, y), (x, 1 - y), (1 - x, 1 - y)]
        local = pltpu.make_async_copy(own, dst_slot(me), local_sem)
        sends = [copy(0, own, me, sibling)] + [copy(1 + j, own, me, (*chip, c)) for j, chip in enumerate(chips)]
        relays = [(copy(1 + j, own, idx(*chip, c), sibling), copy(4 + j, dst_slot(idx(*chip, c)), idx(*chip, c), sibling))
                  for j, chip in enumerate(chips)]
        arrivals = [copy(0, own, idx(*sibling), sibling)]
        arrivals += [copy(4 + j, own, idx(*chip, 1 - c), sibling) for j, chip in enumerate(chips)]
        return local, sends, relays, arrivals


class _Hub:
    def __init__(self, fetch, push, shards, landing):
        self.fetch, self.push, self.shards, self.landing = fetch, push, shards, landing
        self.weights = {}
        self.arriving = {}
        self.grads = {}

    def transfers(self, host):
        out = []
        for name, l, *part in self.fetch.get(host, ()):
            src = self.shards[name]
            shard = tuple(src.shape if l is None else src.shape[1:])
            p, n = part or (0, 1)
            rows = None if n == 1 else (p * (shard[0] // n), shard[0] // n)
            out.append(_Transfer("gather", (name, l, p == n - 1), src, src_idx=l, dst=self.arriving.get((name, l)),
                                 dst_shape=(N_DEV,) + shard, rows=rows))
        for name, l, *part in self.push.get(host, ()):
            src = self.grads[name, l]
            p, n = part or (0, 1)
            rows = None if n == 1 else (p * (src.shape[1] // n), src.shape[1] // n)
            out.append(_Transfer("exchange", (name, l, p == n - 1), src, dst=self.landing[name], dst_idx=l, rows=rows))
        return out

    def accept(self, transfers, results):
        for t, r in zip(transfers, results):
            name, l, complete = t.key
            if t.kind == "exchange":
                self.landing[name] = r
            elif complete:
                self.weights[name, l] = r
            else:
                self.arriving[name, l] = r


def _call(name, body, grid, ins, outs, scratch=(), hub=None):
    transfers = hub.transfers(name) if hub is not None else []
    n_in, n_out, n_scr, n_tr = len(ins), len(outs), len(scratch), len(transfers)
    c_in, c_out, aliases, places = [], [], {}, []
    for t in transfers:
        c_in.append(t.src)
        src_pos = len(c_in) - 1
        if t.dst is not None:
            c_in.append(t.dst)
            aliases[n_in + len(c_in) - 1] = n_out + len(c_out)
            c_out.append(SDS(t.dst.shape, t.dst.dtype))
        else:
            c_out.append(SDS(t.dst_shape, t.src.dtype))
        places.append((src_pos, len(c_out) - 1))
    sems = [pltpu.SemaphoreType.DMA((n_tr, N_DEV - 1)), pltpu.SemaphoreType.DMA((n_tr, N_DEV - 1)),
            pltpu.SemaphoreType.DMA((n_tr,))] if n_tr else []

    def wrapped(*refs):
        in_refs = refs[:n_in]
        cin_refs = refs[n_in:n_in + len(c_in)]
        o0 = n_in + len(c_in)
        out_refs = refs[o0:o0 + n_out]
        cout_refs = refs[o0 + n_out:o0 + n_out + len(c_out)]
        s0 = o0 + n_out + len(c_out)
        scr_refs = refs[s0:s0 + n_scr]
        if n_tr:
            send_sems, recv_sems, local_sems = refs[s0 + n_scr:]
            first = last = relay = None
            for ax, n in enumerate(grid):
                i = pl.program_id(ax)
                at_relay = (i == max(n - 2, 0)) if ax == len(grid) - 1 else (i == n - 1)
                first = (i == 0) if first is None else first & (i == 0)
                last = (i == n - 1) if last is None else last & (i == n - 1)
                relay = at_relay if relay is None else relay & at_relay

            def all_copies():
                return [t.copies(cin_refs[sp], cout_refs[dp], send_sems.at[n], recv_sems.at[n], local_sems.at[n])
                        for n, (t, (sp, dp)) in enumerate(zip(transfers, places))]

            @pl.when(first)
            def _():
                for local, sends, _, _ in all_copies():
                    local.start()
                    for cp in sends:
                        cp.start()

            def pass_on():
                @pl.when(relay)
                def _():
                    for _, _, relays, _ in all_copies():
                        for arrival, onward in relays:
                            arrival.wait_recv()
                            onward.start()

            if grid[-1] > 1:
                pass_on()

        body(*in_refs, *out_refs, *scr_refs)

        if n_tr:
            if grid[-1] == 1:
                pass_on()

            @pl.when(last)
            def _():
                for local, sends, relays, arrivals in all_copies():
                    for cp in arrivals:
                        cp.wait_recv()
                    for cp in sends + [onward for _, onward in relays]:
                        cp.wait_send()
                    local.wait()

    any_spec = pl.BlockSpec(memory_space=pl.ANY)
    res = pl.pallas_call(
        wrapped,
        name=name,
        grid=grid,
        in_specs=[s for _, s in ins] + [any_spec] * len(c_in),
        out_specs=[s for _, s in outs] + [any_spec] * len(c_out),
        out_shape=[o for o, _ in outs] + c_out,
        scratch_shapes=list(scratch) + sems,
        input_output_aliases=aliases,
        compiler_params=pltpu.CompilerParams(
            dimension_semantics=("arbitrary",) * len(grid), vmem_limit_bytes=V7X_VMEM_LIMIT),
    )(*[a for a, _ in ins], *c_in)
    if n_tr:
        hub.accept(transfers, res[n_out:])
    return res[:n_out]


def _rows(a, tm, cb=None, col=0):
    cb = cb or a.shape[1]
    return (a, pl.BlockSpec((tm, cb), lambda i: (i, col)))


def _full(a):
    nd = a.ndim
    return (a, pl.BlockSpec(a.shape, lambda i: (0,) * nd))


def _prev8(a, tm, cb, col):
    return (a, pl.BlockSpec((HALO, cb), lambda i: (jnp.maximum(i * (tm // HALO) - 1, 0), col)))


def _next8(a, tm, cb, col):
    last = a.shape[0] // HALO - 1
    return (a, pl.BlockSpec((HALO, cb), lambda i: (jnp.minimum((i + 1) * (tm // HALO), last), col)))


def _rows2(a, tt, cb=None, colfn=None):
    cb = cb or a.shape[1]
    colfn = colfn or (lambda s: 0)
    return (a, pl.BlockSpec((tt, cb), lambda s, t: (t, colfn(s))))


def _full2(a):
    nd = a.ndim
    return (a, pl.BlockSpec(a.shape, lambda s, t: (0,) * nd))


def _prev8_2(a, tt, cb, col):
    return (a, pl.BlockSpec((HALO, cb), lambda s, t: (jnp.maximum(t * (tt // HALO) - 1, 0), col)))


def _out_rows(T, n, dtype, tm):
    return (SDS((T, n), dtype), pl.BlockSpec((tm, n), lambda i: (i, 0)))


def _out_acc8(d):
    return (SDS((SUBLANES, d), F32), pl.BlockSpec((SUBLANES, d), lambda i: (0, 0)))


def _rstd(x):
    return lax.rsqrt(jnp.mean(x * x, axis=-1, keepdims=True) + EPS)


def _normed(h_ref, g_ref):
    x = h_ref[...]
    return x * _rstd(x) * g_ref[...]


def _acc8(ref, val, i, n):
    part = val.reshape(-1, SUBLANES, val.shape[-1]).sum(axis=0)

    @pl.when(i == 0)
    def _():
        ref[...] = part

    @pl.when(i > 0)
    def _():
        ref[...] += part

    @pl.when(i == n - 1)
    def _():
        ref[...] = jnp.broadcast_to(jnp.sum(ref[...], axis=0, keepdims=True), ref.shape)


def _gate(b_ref, c_ref, u_ref, ch_ref, uh_ref, cw_ref, first):
    b, c, u = (r[...].astype(F32) for r in (b_ref, c_ref, u_ref))
    cu = c * u
    halo = jnp.where(first, 0.0, ch_ref[...].astype(F32) * uh_ref[...].astype(F32))
    rows = lax.broadcasted_iota(jnp.int32, cu.shape, 0)
    h1 = halo[HALO - 1:HALO, :]
    h2 = halo[HALO - 2:HALO - 1, :]
    cu1 = jnp.where(rows == 0, h1, pltpu.roll(cu, 1, 0))
    cu2 = jnp.where(rows == 0, h2, jnp.where(rows == 1, h1, pltpu.roll(cu, 2, 0)))
    conv = cw_ref[0:1, :] * cu + cw_ref[1:2, :] * cu1 + cw_ref[2:3, :] * cu2
    return b * conv, (b, c, u), conv, (cu, cu1, cu2)


def _relu2(a_ref):
    r = jnp.maximum(a_ref[...].astype(F32), 0.0)
    return r * r


def _dot(a, b):
    return jnp.dot(a, b, preferred_element_type=F32)


def _dot_nt(a, b):
    return lax.dot_general(a, b, (((1,), (1,)), ((), ())), preferred_element_type=F32)


def _dot_tn(a, b):
    return lax.dot_general(a, b, (((0,), (0,)), ((), ())), preferred_element_type=F32)


def _chunks(n):
    c = min(MM_CHUNK, n)
    while n % c:
        c -= 128
    assert c > 0, n
    return [(k * c, (k + 1) * c) for k in range(n // c)]


def _col_weight(w):
    _, K, ns = w.shape
    N = N_DEV * ns
    direct = ns % MXU_COLS == 0
    scratch = [] if direct else [pltpu.VMEM((K, N), BF16)]

    def prepare(w_ref, s_ref, step):
        if direct:
            return

        @pl.when(step == 0)
        def _():
            for j in range(N_DEV):
                s_ref[:, j * ns:(j + 1) * ns] = w_ref[j]

    def chunks(w_ref, s_ref):
        if direct:
            return [(j * ns, (j + 1) * ns, (lambda j=j: w_ref[j])) for j in range(N_DEV)]
        return [(lo, hi, (lambda lo=lo, hi=hi: s_ref[:, lo:hi])) for lo, hi in _chunks(N)]

    return N, scratch, prepare, chunks


def _out_cols(n, T, tm):
    return (SDS((n, T), BF16), pl.BlockSpec((n, tm), lambda i: (0, i)))


def _norm_mm(name, h, g, w, tm=MM_ROWS, out_dtype=F32, transposed=False, hub=None):
    T, D = h.shape
    N, w_scratch, prepare, chunks = _col_weight(w)

    def body(h_ref, g_ref, w_ref, o_ref, *rest):
        at_ref, s = (rest[0], rest[1:]) if transposed else (None, rest)
        s_ref = s[0] if s else None
        prepare(w_ref, s_ref, pl.program_id(0))
        a32 = _normed(h_ref, g_ref)
        a = a32.astype(BF16)
        for lo, hi, load in chunks(w_ref, s_ref):
            o_ref[:, lo:hi] = _dot(a, load()).astype(out_dtype)
        if transposed:
            at_ref[...] = a32.T.astype(BF16)

    outs = [_out_rows(T, N, out_dtype, tm)] + ([_out_cols(D, T, tm)] if transposed else [])
    res = _call(name, body, (T // tm,), [_rows(h, tm), _full(g), _full(w)], outs, scratch=w_scratch, hub=hub)
    return res if transposed else res[0]


def _gate_mm_res(name, bcu, cw, w, h, seq, tm=MM_ROWS, hub=None):
    T, D = h.shape

    def body(b_ref, c_ref, u_ref, ch_ref, uh_ref, cw_ref, w_ref, h_ref, o_ref, gt_ref):
        first = (pl.program_id(0) * tm) % seq == 0
        gated32 = _gate(b_ref, c_ref, u_ref, ch_ref, uh_ref, cw_ref, first)[0]
        gated = gated32.astype(BF16)
        for lo, hi in _chunks(D):
            o_ref[:, lo:hi] = h_ref[:, lo:hi] + _dot(gated, w_ref[:, lo:hi])
        gt_ref[...] = gated32.T.astype(BF16)

    ins = [_rows(bcu, tm, D, 0), _rows(bcu, tm, D, 1), _rows(bcu, tm, D, 2), _prev8(bcu, tm, D, 1),
           _prev8(bcu, tm, D, 2), _full(cw), _full(w), _rows(h, tm)]
    return _call(name, body, (T // tm,), ins, [_out_rows(T, D, F32, tm), _out_cols(D, T, tm)], hub=hub)


def _relu2_mm_res(name, a, w, h, tm=MM_ROWS, hub=None):
    T, D = h.shape
    K = a.shape[1]

    def body(a_ref, w_ref, h_ref, o_ref, acc_ref):
        for n, (lo, hi) in enumerate(_chunks(K)):
            d = _dot(_relu2(a_ref.at[:, lo:hi]).astype(BF16), w_ref[lo:hi, :])
            if n == 0:
                acc_ref[...] = d
            else:
                acc_ref[...] += d
        o_ref[...] = h_ref[...] + acc_ref[...]

    return _call(name, body, (T // tm,), [_rows(a, tm), _full(w), _rows(h, tm)], [_out_rows(T, D, F32, tm)],
                 scratch=[pltpu.VMEM((tm, D), F32)], hub=hub)[0]


def _mm_res(name, a, w, h, tm=MM_ROWS):
    T, D = h.shape
    _, w_scratch, prepare, chunks = _col_weight(w)

    def body(a_ref, w_ref, h_ref, o_ref, *s):
        s_ref = s[0] if s else None
        prepare(w_ref, s_ref, pl.program_id(0))
        av = a_ref[...].astype(BF16)
        for lo, hi, load in chunks(w_ref, s_ref):
            o_ref[:, lo:hi] = h_ref[:, lo:hi] + _dot(av, load())

    return _call(name, body, (T // tm,), [_rows(a, tm), _full(w), _rows(h, tm)], [_out_rows(T, D, F32, tm)],
                 scratch=w_scratch)[0]


def _nt_relu2_bwd(name, dh, w, a, tm=MM_ROWS, hub=None):
    T, _ = dh.shape
    K = w.shape[0]

    def body(dh_ref, w_ref, a_ref, o_ref):
        d = dh_ref[...].astype(BF16)
        for lo, hi in _chunks(K):
            dr = _dot_nt(d, w_ref[lo:hi, :])
            o_ref[:, lo:hi] = (dr * (2.0 * jnp.maximum(a_ref[:, lo:hi].astype(F32), 0.0))).astype(BF16)

    return _call(name, body, (T // tm,), [_rows(dh, tm), _full(w), _rows(a, tm)], [_out_rows(T, K, BF16, tm)], hub=hub)[0]


def _concat_bf16(*refs):
    vals = [r[...].astype(BF16) for r in refs]
    return vals[0] if len(vals) == 1 else jnp.concatenate(vals, axis=1)


def _nt_plain(name, dy, w, tm=MM_ROWS):
    T, N = dy.shape
    if w.ndim == 3:
        K = w.shape[1]
        _, w_scratch, prepare, chunks = _col_weight(w)
    else:
        K = w.shape[0]
        w_scratch, prepare = [], (lambda w_ref, s_ref, step: None)
        chunks = lambda w_ref, s_ref: [(lo, hi, (lambda lo=lo, hi=hi: w_ref[:, lo:hi])) for lo, hi in _chunks(N)]

    def body(dy_ref, w_ref, o_ref, acc_ref, *s):
        s_ref = s[0] if s else None
        prepare(w_ref, s_ref, pl.program_id(0))
        for n, (lo, hi, load) in enumerate(chunks(w_ref, s_ref)):
            d = _dot_nt(dy_ref[:, lo:hi].astype(BF16), load())
            if n == 0:
                acc_ref[...] = d
            else:
                acc_ref[...] += d
        o_ref[...] = acc_ref[...]

    return _call(name, body, (T // tm,), [_rows(dy, tm), _full(w)], [_out_rows(T, K, F32, tm)],
                 scratch=[pltpu.VMEM((tm, K), F32)] + w_scratch)[0]


def _att_out_bwd(name, dy, w, o, tm=MM_ROWS):
    T, _ = dy.shape
    K = w.shape[1]
    _, w_scratch, prepare, chunks = _col_weight(w)

    def body(dy_ref, w_ref, o_ref, do_ref, dl_ref, acc_ref, *s):
        s_ref = s[0] if s else None
        prepare(w_ref, s_ref, pl.program_id(0))
        for n, (lo, hi, load) in enumerate(chunks(w_ref, s_ref)):
            d = _dot_nt(dy_ref[:, lo:hi].astype(BF16), load())
            if n == 0:
                acc_ref[...] = d
            else:
                acc_ref[...] += d
        do = acc_ref[...]
        do_ref[...] = do
        prod = do * o_ref[...]
        high = prod.astype(BF16)
        low = (prod - high.astype(F32)).astype(BF16)
        head_of = lambda axis: jnp.right_shift(lax.broadcasted_iota(jnp.int32, (K, K), axis), HEAD_DIM.bit_length() - 1)
        same_head = jnp.where(head_of(0) == head_of(1), 1.0, 0.0).astype(BF16)
        dl_ref[...] = _dot(high, same_head) + _dot(low, same_head)

    outs = [_out_rows(T, K, F32, tm), _out_rows(T, K, F32, tm)]
    return _call(name, body, (T // tm,), [_rows(dy, tm), _full(w), _rows(o, tm)], outs,
                 scratch=[pltpu.VMEM((tm, K), F32)] + w_scratch)


def _nt_norm_bwd(name, dys, w, h, g, dh_in, tm=MM_ROWS, hub=None):
    T, D = h.shape
    _, w_scratch, prepare, chunks = _col_weight(w)
    n_steps = T // tm
    n_dy = len(dys)

    def body(*refs):
        dy_refs = refs[:n_dy]
        w_ref, h_ref, g_ref, dhin_ref, o_ref, dg_ref, acc_ref = refs[n_dy:n_dy + 7]
        s_ref = refs[n_dy + 7] if len(refs) > n_dy + 7 else None
        i = pl.program_id(0)
        prepare(w_ref, s_ref, i)
        dy = _concat_bf16(*dy_refs)
        for n, (lo, hi, load) in enumerate(chunks(w_ref, s_ref)):
            d = _dot_nt(dy[:, lo:hi], load())
            if n == 0:
                acc_ref[...] = d
            else:
                acc_ref[...] += d
        dn = acc_ref[...]
        x = h_ref[...]
        rstd = _rstd(x)
        xhat = x * rstd
        dxhat = dn * g_ref[...]
        dx = rstd * (dxhat - xhat * jnp.mean(dxhat * xhat, axis=-1, keepdims=True))
        o_ref[...] = dhin_ref[...] + dx
        _acc8(dg_ref, dn * xhat, i, n_steps)

    ins = [_rows(d, tm) for d in dys] + [_full(w), _rows(h, tm), _full(g), _rows(dh_in, tm)]
    outs = [_out_rows(T, D, F32, tm), _out_acc8(D)]
    dh, dg = _call(name, body, (n_steps,), ins, outs, scratch=[pltpu.VMEM((tm, D), F32)] + w_scratch, hub=hub)
    return dh, dg[0:1]


def _cols2(a_t, tt, kb=None):
    kb = kb or a_t.shape[0]
    return (a_t, pl.BlockSpec((kb, tt), (lambda s, t: (s, t)) if kb != a_t.shape[0] else (lambda s, t: (0, t))))


def _tn(name, a_ins, a_fn, y_ins, y_fn, K, N, T, tt, split=None, out_cols=None, hub=None):
    kind, parts = split or ("n", 1)
    kb, nb = (K // parts, N) if kind == "k" else (K, N // parts)
    n_steps = T // tt
    n_a = len(a_ins)
    n_y = len(y_ins)
    assert out_cols is None or (kind == "n" and nb % out_cols == 0)

    def body(*refs):
        a_refs = refs[:n_a]
        y_refs = refs[n_a:n_a + n_y]
        o_ref, acc_ref = refs[n_a + n_y:]
        t = pl.program_id(1)
        a_t = a_refs[0][...] if a_fn is None else a_fn(*a_refs).T.astype(BF16)
        y = y_fn(*y_refs).astype(BF16)
        for lo, hi in _chunks(nb):
            d = _dot(a_t, y[:, lo:hi])

            @pl.when(t == 0)
            def _():
                acc_ref[:, lo:hi] = d

            @pl.when(t > 0)
            def _():
                acc_ref[:, lo:hi] += d

        @pl.when(t == n_steps - 1)
        def _():
            if out_cols is None:
                o_ref[...] = acc_ref[...].astype(BF16)
            else:
                for j in range(nb // out_cols):
                    o_ref[j] = acc_ref[:, j * out_cols:(j + 1) * out_cols].astype(BF16)

    if out_cols is None:
        out = (SDS((K, N), BF16), pl.BlockSpec((kb, nb), (lambda s, t: (s, 0)) if kind == "k" else (lambda s, t: (0, s))))
    else:
        out = (SDS((N // out_cols, K, out_cols), BF16), pl.BlockSpec((nb // out_cols, K, out_cols), lambda s, t: (s, 0, 0)))
    return _call(name, body, (parts, n_steps), list(a_ins) + list(y_ins), [out],
                 scratch=[pltpu.VMEM((kb, nb), F32)], hub=hub)[0]


def _val(ref):
    return ref[...]


def _concat_f32(*refs):
    vals = [r[...] for r in refs]
    return vals[0] if len(vals) == 1 else jnp.concatenate(vals, axis=1)


ATT_TILE_ROWS = 2048
HEAD_PAIRS = H_G // 2
ATT_SCALE = HEAD_DIM ** -0.5
ATT_UNITS_TOGETHER = 4


def _slope(h):
    return 2.0 ** (-ALIBI_MAX_BIAS * (h + 1) / H_G)


def _att_geom(T, bl, g):
    dil = PATTERNS[g][1]
    sub = ATT_BLK * dil
    nsub = max(1, ATT_TILE_ROWS // sub)
    rows = sub * nsub
    return dil, sub, nsub, rows, T // bl // rows


def _att_specs(T, bl, g):
    _, sub, nsub, rows, nt = _att_geom(T, bl, g)
    last_sub = T // sub - 1
    tile = lambda col: pl.BlockSpec((rows, 128), lambda b, i, hp: (b * nt + i, col(hp)))
    prev = lambda col: pl.BlockSpec((sub, 128), lambda b, i, hp: (jnp.maximum((b * nt + i) * nsub - 1, 0), col(hp)))
    nxt = lambda col: pl.BlockSpec((sub, 128), lambda b, i, hp: (jnp.minimum((b * nt + i + 1) * nsub, last_sub), col(hp)))
    return tile, prev, nxt


def _sub_rows(j, r, dil):
    start = j * ATT_BLK * dil + r
    return pl.ds(start, ATT_BLK, stride=dil) if dil > 1 else pl.ds(start, ATT_BLK)


class _Residues:
    def __init__(self, dil):
        self.dil = dil
        self.whole = dil % SUBLANES == 0
        self.read, self.written = {}, {}

    def _block(self, j):
        return pl.ds(j * ATT_BLK * self.dil, ATT_BLK * self.dil)

    def load(self, ref, j, r):
        if not self.whole:
            return ref[_sub_rows(j, r, self.dil), :]
        if (id(ref), j) not in self.read:
            rows = ref[self._block(j), :]
            self.read[id(ref), j] = jnp.swapaxes(rows.reshape(ATT_BLK, self.dil, rows.shape[-1]), 0, 1)
        return self.read[id(ref), j][r]

    def store(self, ref, j, r, val):
        if not self.whole:
            ref[_sub_rows(j, r, self.dil), :] = val
            return
        got = self.written.setdefault((id(ref), j), {})
        got[r] = val
        if len(got) == self.dil:
            merged = jnp.swapaxes(jnp.stack([got[k] for k in range(self.dil)], axis=0), 0, 1)
            ref[self._block(j), :] = merged.reshape(ATT_BLK * self.dil, val.shape[-1])
            del self.written[id(ref), j]


def _att_consts(hp, dil, keys_first=False):
    h0 = lax.broadcasted_iota(jnp.int32, (ATT_BLK, 128), 1) < HEAD_DIM
    a = lax.broadcasted_iota(jnp.int32, (ATT_BLK, ATT_BLK), 1 if keys_first else 0)
    c = lax.broadcasted_iota(jnp.int32, (ATT_BLK, ATT_BLK), 0 if keys_first else 1)
    dist_p = ((ATT_BLK + a - c) * dil).astype(F32)
    dist_c = ((a - c) * dil).astype(F32)
    bias_p, bias_c = [], []
    for h in range(2):
        slope = jnp.float32(_slope(2 * (HEAD_PAIRS - 1) + h))
        for p in range(HEAD_PAIRS - 2, -1, -1):
            slope = jnp.where(hp == p, jnp.float32(_slope(2 * p + h)), slope)
        bias_p.append(jnp.where(c >= a, -slope * dist_p, NEG_INF))
        bias_c.append(jnp.where(c <= a, -slope * dist_c, NEG_INF))
    return h0, jnp.concatenate(bias_p, axis=0), jnp.concatenate(bias_c, axis=0)


def _stack_heads(x, h0):
    return jnp.concatenate([jnp.where(h0, x, 0.0), jnp.where(h0, 0.0, x)], axis=0).astype(BF16)


def _unstack_heads(x, h0):
    return jnp.where(h0, x[:ATT_BLK], x[ATT_BLK:])


def _stack_cols(x):
    return jnp.concatenate(_head_cols(x), axis=0)


def _split_heads(x, h0):
    return [jnp.where(h0, x, 0.0).astype(BF16), jnp.where(h0, 0.0, x).astype(BF16)]


def _head_cols(x):
    return [x[:, 0:1], x[:, HEAD_DIM:HEAD_DIM + 1]]


def _in_groups(units, first_stage, *later_stages):
    for u0 in range(0, len(units), ATT_UNITS_TOGETHER):
        staged = [first_stage(*u) for u in units[u0:u0 + ATT_UNITS_TOGETHER]]
        for stage in later_stages:
            staged = [stage(*s) for s in staged]


def _attn_fwd(name, q, kv, g, bl, hub=None):
    T = q.shape[0]
    dil, _, nsub, _, _ = _att_geom(T, bl, g)
    tile, prev, _ = _att_specs(T, bl, g)

    def body(q_ref, kp_ref, kc_ref, vp_ref, vc_ref, o_ref, lse_ref):
        first = pl.program_id(1) == 0
        h0, bias_p, bias_c = _att_consts(pl.program_id(2), dil)
        bias_first = jnp.where(first, NEG_INF, bias_p)
        rows = _Residues(dil)

        def with_ones(v):
            return [jnp.where(h0, v, 1.0).astype(BF16), jnp.where(h0, 1.0, v).astype(BF16)]

        def scores(j, r):
            if j == 0:
                kp, vp, bp = rows.load(kp_ref, 0, r), rows.load(vp_ref, 0, r), bias_first
            else:
                kp, vp, bp = rows.load(kc_ref, j - 1, r), rows.load(vc_ref, j - 1, r), bias_p
            kp, kc = kp.astype(BF16), rows.load(kc_ref, j, r).astype(BF16)
            qs = _stack_heads(rows.load(q_ref, j, r) * ATT_SCALE, h0)
            sp = _dot_nt(qs, kp) + bp
            sc = _dot_nt(qs, kc) + bias_c
            return (j, r), sp, sc, with_ones(vp), with_ones(rows.load(vc_ref, j, r))

        def weights(unit, sp, sc, vp, vc):
            mx = jnp.max(jnp.maximum(sp, sc), axis=-1, keepdims=True)
            return unit, mx, jnp.exp(sp - mx).astype(BF16), jnp.exp(sc - mx).astype(BF16), vp, vc

        def outputs(unit, mx, ep, ec, vp, vc):
            heads = [slice(h * ATT_BLK, (h + 1) * ATT_BLK) for h in range(2)]
            acc = [_dot(ep[hs], vp[h]) + _dot(ec[hs], vc[h]) for h, hs in enumerate(heads)]
            den = [pltpu.roll(a, HEAD_DIM, 1) for a in acc]
            rows.store(o_ref, *unit, jnp.where(h0, acc[0] / den[0], acc[1] / den[1]))
            rows.store(lse_ref, *unit, jnp.where(h0, mx[heads[0]] + jnp.log(den[0]), mx[heads[1]] + jnp.log(den[1])))
            return ()

        _in_groups([(j, r) for j in range(nsub) for r in range(dil)], scores, weights, outputs)

    ins = [(q, tile(lambda hp: 4 * g + hp)), (kv, prev(lambda hp: 8 * g + hp)), (kv, tile(lambda hp: 8 * g + hp)),
           (kv, prev(lambda hp: 8 * g + 4 + hp)), (kv, tile(lambda hp: 8 * g + 4 + hp))]
    out = (SDS((T, QW), F32), tile(lambda hp: hp))
    _, _, _, _, nt = _att_geom(T, bl, g)
    return _call(name, body, (bl, nt, HEAD_PAIRS), ins, [out, out], hub=hub)


def _combine(name, os_, lses, tm=512):
    T = os_[0].shape[0]

    def body(o0, o1, o2, l0, l1, l2, o_ref, lse_ref):
        ls = [l0[...], l1[...], l2[...]]
        mx = jnp.maximum(jnp.maximum(ls[0], ls[1]), ls[2])
        es = [jnp.exp(l - mx) for l in ls]
        den = es[0] + es[1] + es[2]
        o_ref[...] = (es[0] * o0[...] + es[1] * o1[...] + es[2] * o2[...]) / den
        lse_ref[...] = mx + jnp.log(den)

    ins = [_rows(t, tm) for t in list(os_) + list(lses)]
    return _call(name, body, (T // tm,), ins, [_out_rows(T, QW, F32, tm), _out_rows(T, QW, F32, tm)])


def _attn_bwd_dq(name, q, kv, do, delta, lse, g, bl, hub=None):
    T = q.shape[0]
    dil, _, nsub, _, nt = _att_geom(T, bl, g)
    tile, prev, _ = _att_specs(T, bl, g)

    def body(q_ref, kp_ref, kc_ref, vp_ref, vc_ref, do_ref, dl_ref, lse_ref, dq_ref):
        first = pl.program_id(1) == 0
        h0, bias_p, bias_c = _att_consts(pl.program_id(2), dil)
        bias_first = jnp.where(first, NEG_INF, bias_p)
        rows = _Residues(dil)

        def probs(j, r):
            if j == 0:
                kp, vp, bp = rows.load(kp_ref, 0, r), rows.load(vp_ref, 0, r), bias_first
            else:
                kp, vp, bp = rows.load(kc_ref, j - 1, r), rows.load(vc_ref, j - 1, r), bias_p
            kp, vp = kp.astype(BF16), vp.astype(BF16)
            kc, vc = rows.load(kc_ref, j, r).astype(BF16), rows.load(vc_ref, j, r).astype(BF16)
            qs = _stack_heads(rows.load(q_ref, j, r) * ATT_SCALE, h0)
            dos = _stack_heads(rows.load(do_ref, j, r), h0)
            lse = _stack_cols(rows.load(lse_ref, j, r))
            pp = jnp.exp(_dot_nt(qs, kp) + bp - lse)
            pc = jnp.exp(_dot_nt(qs, kc) + bias_c - lse)
            return (j, r), pp, pc, _dot_nt(dos, vp), _dot_nt(dos, vc), kp, kc

        def dscores(unit, pp, pc, dpp, dpc, kp, kc):
            dl = _stack_cols(rows.load(dl_ref, *unit))
            return unit, (pp * (dpp - dl)).astype(BF16), (pc * (dpc - dl)).astype(BF16), kp, kc

        def outputs(unit, dsp, dsc, kp, kc):
            rows.store(dq_ref, *unit, _unstack_heads(_dot(dsp, kp) + _dot(dsc, kc), h0) * ATT_SCALE)
            return ()

        _in_groups([(j, r) for j in range(nsub) for r in range(dil)], probs, dscores, outputs)

    own = lambda hp: hp
    ins = [(q, tile(lambda hp: 4 * g + hp)), (kv, prev(lambda hp: 8 * g + hp)), (kv, tile(lambda hp: 8 * g + hp)),
           (kv, prev(lambda hp: 8 * g + 4 + hp)), (kv, tile(lambda hp: 8 * g + 4 + hp)),
           (do, tile(own)), (delta, tile(own)), (lse, tile(own))]
    return _call(name, body, (bl, nt, HEAD_PAIRS), ins, [(SDS((T, QW), F32), tile(own))], hub=hub)[0]


def _attn_bwd_dkv(name, q, kv, do, delta, lse, g, bl, prev=None, hub=None):
    T = q.shape[0]
    dil, _, nsub, _, nt = _att_geom(T, bl, g)
    tile, _, nxt = _att_specs(T, bl, g)
    has_prev = prev is not None

    def body(*refs):
        k_ref, v_ref, q_ref, qn_ref, do_ref, don_ref, dl_ref, dln_ref, l_ref, ln_ref = refs[:10]
        rest = refs[10:]
        if has_prev:
            dkp_ref, dvp_ref, dk_ref, dv_ref = rest
        else:
            dk_ref, dv_ref = rest
        last = pl.program_id(1) == nt - 1
        h0, bias_p, bias_c = _att_consts(pl.program_id(2), dil, keys_first=True)
        bias_last = jnp.where(last, NEG_INF, bias_p)
        rows = _Residues(dil)

        def per_query_rows(x):
            xt = x.T
            return jnp.concatenate([jnp.broadcast_to(xt[0:1], (ATT_BLK, ATT_BLK)),
                                    jnp.broadcast_to(xt[HEAD_DIM:HEAD_DIM + 1], (ATT_BLK, ATT_BLK))], axis=0)

        def probs(j, r):
            ks = _stack_heads(rows.load(k_ref, j, r), h0)
            vs = _stack_heads(rows.load(v_ref, j, r), h0)
            sets = [(q_ref, do_ref, dl_ref, l_ref, j, bias_c)]
            if j < nsub - 1:
                sets.append((q_ref, do_ref, dl_ref, l_ref, j + 1, bias_p))
            else:
                sets.append((qn_ref, don_ref, dln_ref, ln_ref, 0, bias_last))
            out = []
            for qr, dor, dlr, lr, jq, bias in sets:
                qsb = (rows.load(qr, jq, r) * ATT_SCALE).astype(BF16)
                do2b = rows.load(dor, jq, r).astype(BF16)
                p = jnp.exp(_dot_nt(ks, qsb) + bias - per_query_rows(rows.load(lr, jq, r)))
                out.append((p, _dot_nt(vs, do2b), dlr, jq, qsb, do2b))
            return (j, r), out

        def dscores(unit, sets):
            out = []
            for p, dp, dlr, jq, qsb, do2b in sets:
                ds = (p * (dp - per_query_rows(rows.load(dlr, jq, unit[1])))).astype(BF16)
                out.append((p.astype(BF16), ds, qsb, do2b))
            return unit, out

        def outputs(unit, sets):
            dk_st = dv_st = None
            for pb, ds, qsb, do2b in sets:
                dvs, dks = _dot(pb, do2b), _dot(ds, qsb)
                dv_st = dvs if dv_st is None else dv_st + dvs
                dk_st = dks if dk_st is None else dk_st + dks
            dk2 = _unstack_heads(dk_st, h0)
            dv2 = _unstack_heads(dv_st, h0)
            if has_prev:
                dk2 = dk2 + rows.load(dkp_ref, *unit)
                dv2 = dv2 + rows.load(dvp_ref, *unit)
            rows.store(dk_ref, *unit, dk2)
            rows.store(dv_ref, *unit, dv2)
            return ()

        _in_groups([(j, r) for j in range(nsub) for r in range(dil)], probs, dscores, outputs)

    own = lambda hp: hp
    qcol = lambda hp: 4 * g + hp
    ins = [(kv, tile(lambda hp: 8 * g + hp)), (kv, tile(lambda hp: 8 * g + 4 + hp)), (q, tile(qcol)), (q, nxt(qcol)),
           (do, tile(own)), (do, nxt(own)), (delta, tile(own)), (delta, nxt(own)), (lse, tile(own)), (lse, nxt(own))]
    if has_prev:
        ins += [(prev[0], tile(own)), (prev[1], tile(own))]
    out = (SDS((T, QW), F32), tile(own))
    return _call(name, body, (bl, nt, HEAD_PAIRS), ins, [out, out], hub=hub)


def _final_loss(name, h, tgt, g, tm=256):
    T, D = h.shape
    n_steps = T // tm

    def body(h_ref, t_ref, g_ref, dh_ref, loss_ref, dg_ref, sq_ref):
        i = pl.program_id(0)
        x = h_ref[...]
        rstd = _rstd(x)
        xhat = x * rstd
        err = xhat * g_ref[...] - t_ref[...]
        _acc8(sq_ref, err * err, i, n_steps)
        dy = err * (1.0 / D)
        dxhat = dy * g_ref[...]
        dh_ref[...] = rstd * (dxhat - xhat * jnp.mean(dxhat * xhat, axis=-1, keepdims=True))
        _acc8(dg_ref, dy * xhat, i, n_steps)

        @pl.when(i == n_steps - 1)
        def _():
            loss_ref[...] = jnp.full(loss_ref.shape, jnp.sum(sq_ref[0:1, :]), F32)

    outs = [_out_rows(T, D, F32, tm), (SDS((SUBLANES, 128), F32), pl.BlockSpec((SUBLANES, 128), lambda i: (0, 0))),
            _out_acc8(D)]
    dh, loss, dg = _call(name, body, (n_steps,), [_rows(h, tm), _rows(tgt, tm), _full(g)], outs,
                         scratch=[pltpu.VMEM((SUBLANES, D), F32)])
    return dh, loss[0, 0], dg[0:1]


def _conv_bwd(name, bcu, dgated, cw, seq, tm=256, hub=None):
    T, D = dgated.shape
    n_steps = T // tm

    def body(b_ref, c_ref, u_ref, ch_ref, uh_ref, dg_ref, dgn_ref, bn_ref, cw_ref, o_ref, t0_ref, t1_ref, t2_ref):
        i = pl.program_id(0)
        first = (i * tm) % seq == 0
        last = ((i + 1) * tm) % seq == 0
        _, (b, c, u), conv, (cu, cu1, cu2) = _gate(b_ref, c_ref, u_ref, ch_ref, uh_ref, cw_ref, first)
        dgat = dg_ref[...]
        dconv = dgat * b
        nxt = jnp.where(last, 0.0, dgn_ref[...] * bn_ref[...].astype(F32))
        rows = lax.broadcasted_iota(jnp.int32, dconv.shape, 0)
        n1 = nxt[0:1, :]
        n2 = nxt[1:2, :]
        dc1 = jnp.where(rows == tm - 1, n1, pltpu.roll(dconv, tm - 1, 0))
        dc2 = jnp.where(rows == tm - 1, n2, jnp.where(rows == tm - 2, n1, pltpu.roll(dconv, tm - 2, 0)))
        dcu = cw_ref[0:1, :] * dconv + cw_ref[1:2, :] * dc1 + cw_ref[2:3, :] * dc2
        o_ref[:, 0:D] = (dgat * conv).astype(BF16)
        o_ref[:, D:2 * D] = (dcu * u).astype(BF16)
        o_ref[:, 2 * D:3 * D] = (dcu * c).astype(BF16)
        _acc8(t0_ref, dconv * cu, i, n_steps)
        _acc8(t1_ref, dconv * cu1, i, n_steps)
        _acc8(t2_ref, dconv * cu2, i, n_steps)

    ins = [_rows(bcu, tm, D, 0), _rows(bcu, tm, D, 1), _rows(bcu, tm, D, 2), _prev8(bcu, tm, D, 1), _prev8(bcu, tm, D, 2),
           _rows(dgated, tm), _next8(dgated, tm, D, 0), _next8(bcu, tm, D, 0), _full(cw)]
    outs = [_out_rows(T, 3 * D, BF16, tm), _out_acc8(D), _out_acc8(D), _out_acc8(D)]
    dbcu, t0, t1, t2 = _call(name, body, (n_steps,), ins, outs, hub=hub)
    return dbcu, jnp.concatenate([t0[0:1], t1[0:1], t2[0:1]], axis=0)


def _sum8_adamw(name, parts, w, m, v, tr):
    R, C = w.shape
    b1c = 1.0 - ADAM_B1 ** ADAM_STEP
    b2c = 1.0 - ADAM_B2 ** ADAM_STEP

    def body(p_ref, w_ref, m_ref, v_ref, g_ref, d_ref, nm_ref, nv_ref):
        g = p_ref[0].astype(F32)
        for j in range(1, N_DEV):
            g = g + p_ref[j].astype(F32)
        nm = ADAM_B1 * m_ref[...] + (1.0 - ADAM_B1) * g
        nv = ADAM_B2 * v_ref[...] + (1.0 - ADAM_B2) * (g * g)
        m_hat = nm / b1c
        v_hat = nv / b2c
        g_ref[...] = g
        d_ref[...] = -ADAM_LR * (m_hat / (jnp.sqrt(v_hat) + ADAM_EPS) + ADAM_WD * w_ref[...])
        nm_ref[...] = nm
        nv_ref[...] = nv

    ins = [(parts, pl.BlockSpec((N_DEV, tr, C), lambda i: (0, i, 0))), _rows(w, tr), _rows(m, tr), _rows(v, tr)]
    outs = [_out_rows(R, C, F32, tr)] * 4
    return _call(name, body, (R // tr,), ins, outs)


def _all_gather(name, items):
    n = len(items)
    shapes = [tuple(a.shape if idx is None else a.shape[1:]) for a, idx in items]

    def body(*refs):
        x_refs, out_refs = refs[:n], refs[n:2 * n]
        send_sems, recv_sems, local_sems = refs[2 * n:]
        x, y, c = _mesh_pos()
        me, sibling = (x, y, c), (x, y, 1 - c)
        chips = [(1 - x, y), (x, 1 - y), (1 - x, 1 - y)]

        def copy(t, k, block, to, own=False):
            dst = out_refs[t].at[4 * block[0] + 2 * block[1] + block[2]]
            src = dst
            if own:
                src = x_refs[t] if items[t][1] is None else x_refs[t].at[items[t][1]]
            return pltpu.make_async_remote_copy(
                src_ref=src, dst_ref=dst, send_sem=send_sems.at[t, k], recv_sem=recv_sems.at[t, k],
                device_id=to, device_id_type=pl.DeviceIdType.MESH)

        started = []
        for t in range(n):
            src = x_refs[t] if items[t][1] is None else x_refs[t].at[items[t][1]]
            mine = pltpu.make_async_copy(src, out_refs[t].at[4 * x + 2 * y + c], local_sems.at[t])
            mine.start()
            first = [copy(t, 0, me, sibling, own=True)]
            first += [copy(t, 1 + j, me, (*chip, c), own=True) for j, chip in enumerate(chips)]
            for cp in first:
                cp.start()
            started.append((mine, first))
        passed = []
        for t in range(n):
            for j, chip in enumerate(chips):
                copy(t, 1 + j, (*chip, c), me).wait_recv()
                fwd = copy(t, 4 + j, (*chip, c), sibling)
                fwd.start()
                passed.append(fwd)
        for t in range(n):
            copy(t, 0, sibling, me).wait_recv()
            for j, chip in enumerate(chips):
                copy(t, 4 + j, (*chip, 1 - c), me).wait_recv()
        for mine, first in started:
            for cp in first:
                cp.wait_send()
            mine.wait()
        for cp in passed:
            cp.wait_send()

    any_spec = pl.BlockSpec(memory_space=pl.ANY)
    return pl.pallas_call(
        body, name=name,
        out_shape=[SDS((N_DEV,) + s, a.dtype) for s, (a, _) in zip(shapes, items)],
        in_specs=[any_spec] * n,
        out_specs=[any_spec] * n,
        scratch_shapes=[pltpu.SemaphoreType.DMA((n, 7)), pltpu.SemaphoreType.DMA((n, 7)), pltpu.SemaphoreType.DMA((n,))],
    )(*[a for a, _ in items])


def _pad8(t):
    return jnp.pad(t, ((0, SUBLANES - t.shape[0]), (0, 0)))


def _rows_merged(w):
    return w.reshape(w.shape[0] * w.shape[1], w.shape[2])


def _local_grads(x, tgt, norm_mix, norm_mlp, norm_kv, norm_final, conv_w, hub):
    bl, seq, D = x.shape
    T = bl * seq
    h = x.reshape(T, D)
    tgt = tgt.reshape(T, D)
    row = lambda t, l: t[l:l + 1]
    W = hub.weights
    saved = []
    kv = h_kv = hn_kv_t = None
    for l in range(DEPTH):
        if l < N_A_LAYERS:
            bcu, hn_t = _norm_mm(f"l{l}_in", h, row(norm_mix, l), W["w_a_in", l], out_dtype=BF16, transposed=True, hub=hub)
            h2, gated_t = _gate_mm_res(f"l{l}_conv_out", bcu, _pad8(conv_w[l]), _rows_merged(W["w_a_out", l]), h, seq, hub=hub)
            saved.append((h, bcu, gated_t, hn_t))
        else:
            i = l - N_A_LAYERS
            if l == N_A_LAYERS:
                h_kv = h
                kv, hn_kv_t = _norm_mm("kv", h, norm_kv.reshape(1, D), W["w_kv", None], transposed=True, hub=hub)
            q, hn_t = _norm_mm(f"l{l}_q", h, row(norm_mix, l), W["w_q", i], transposed=True)
            per_group = [_attn_fwd(f"l{l}_att{g}", q, kv, g, bl, hub=hub) for g in range(N_GROUPS)]
            o, lse = _combine(f"l{l}_combine", [p[0] for p in per_group], [p[1] for p in per_group])
            h2 = _mm_res(f"l{l}_att_out", o, W["w_o", i], h)
            saved.append((h, q, o, lse, hn_t))
        a = _norm_mm(f"l{l}_up", h2, row(norm_mlp, l), W["w_up", l], out_dtype=BF16, hub=hub)
        h = _relu2_mm_res(f"l{l}_down", a, _rows_merged(W["w_down", l]), h2, hub=hub)
        saved[-1] = saved[-1] + (h2, a)

    dh, sq_err, d_norm_final = _final_loss("loss", h, tgt, norm_final.reshape(1, D))

    d_norm_mix = [None] * DEPTH
    d_norm_mlp = [None] * DEPTH
    d_conv = [None] * N_A_LAYERS
    d_norm_kv = None
    dkv_acc = [None] * N_GROUPS
    G = hub.grads
    as_slots = lambda g: g.reshape(N_DEV, g.shape[0] // N_DEV, g.shape[1])
    tt = DW_TOKENS
    for l in reversed(range(DEPTH)):
        h2, a = saved[l][-2:]
        h_in = saved[l][0]
        g_mlp = row(norm_mlp, l)
        g_mix = row(norm_mix, l)
        w_up_l = W["w_up", l]
        FF = N_DEV * w_up_l.shape[2]
        da = _nt_relu2_bwd(f"l{l}_down_bwd", dh, _rows_merged(W["w_down", l]), a, hub=hub)
        G["w_down", l] = as_slots(_tn(f"l{l}_dw_down", [_rows2(a, tt, FF // 2, lambda s: s)], _relu2,
                                      [_rows2(dh, tt)], _val, FF, D, T, tt, split=("k", 2)))
        G["w_up", l] = _tn(f"l{l}_dw_up", [_rows2(h2, 2 * tt), _full2(g_mlp)], _normed,
                           [_rows2(da, 2 * tt, FF // 4, lambda s: s)], _val, D, FF, T, 2 * tt, split=("n", 4),
                           out_cols=w_up_l.shape[2], hub=hub)
        dh2, d_norm_mlp[l] = _nt_norm_bwd(f"l{l}_up_bwd", [da], w_up_l, h2, g_mlp, dh, hub=hub)
        if l >= N_A_LAYERS:
            i = l - N_A_LAYERS
            _, q, o, lse, hn_t = saved[l][:5]
            w_o_i, w_q_i = W["w_o", i], W["w_q", i]
            do, delta = _att_out_bwd(f"l{l}_att_out_bwd", dh2, w_o_i, o)
            G["w_o", i] = _tn(f"l{l}_dw_o", [_rows2(o, tt)], _val, [_rows2(dh2, tt)], _val, QW, D, T, tt,
                              out_cols=w_o_i.shape[2])
            dqs = []
            for g in range(N_GROUPS):
                dqs.append(_attn_bwd_dq(f"l{l}_att{g}_dq", q, kv, do, delta, lse, g, bl, hub=hub))
                dkv_acc[g] = _attn_bwd_dkv(f"l{l}_att{g}_dkv", q, kv, do, delta, lse, g, bl, prev=dkv_acc[g], hub=hub)
            G["w_q", i] = _tn(f"l{l}_dw_q", [_cols2(hn_t, tt)], None,
                              [_rows2(t, tt) for t in dqs], _concat_f32, D, N_GROUPS * QW, T, tt, out_cols=w_q_i.shape[2])
            dh, d_norm_mix[l] = _nt_norm_bwd(f"l{l}_q_bwd", dqs, w_q_i, h_in, g_mix, dh2, hub=hub)
            if l == N_A_LAYERS:
                dkvs = [t for pair in dkv_acc for t in pair]
                g_kv = norm_kv.reshape(1, D)
                w_kv = W["w_kv", None]
                per_call = len(dkvs) // 2
                halves = [_tn(f"dw_kv{p}", [_cols2(hn_kv_t, tt)], None,
                              [_rows2(t, tt) for t in dkvs[p * per_call:(p + 1) * per_call]], _concat_f32,
                              D, per_call * QW, T, tt, out_cols=w_kv.shape[2]) for p in range(2)]
                G["w_kv", None] = jnp.concatenate(halves, axis=0)
                dh, d_norm_kv = _nt_norm_bwd("kv_bwd", dkvs, w_kv, h_kv, g_kv, dh, tm=MM_ROWS // 2, hub=hub)
        else:
            _, bcu, gated_t, hn_t = saved[l][:4]
            cw = _pad8(conv_w[l])
            w_in_l = W["w_a_in", l]
            dgated = _nt_plain(f"l{l}_conv_out_bwd", dh2, _rows_merged(W["w_a_out", l]))
            G["w_a_out", l] = as_slots(_tn(f"l{l}_dw_conv_out", [_cols2(gated_t, 2 * tt)], None,
                                           [_rows2(dh2, 2 * tt)], _val, D, D, T, 2 * tt, hub=hub))
            dbcu, d_conv[l] = _conv_bwd(f"l{l}_conv_bwd", bcu, dgated, cw, seq, hub=hub)
            G["w_a_in", l] = _tn(f"l{l}_dw_in", [_cols2(hn_t, 2 * tt)], None,
                                 [_rows2(dbcu, 2 * tt, 3 * D // 2, lambda s: s)], _val, D, 3 * D, T, 2 * tt, split=("n", 2),
                                 out_cols=w_in_l.shape[2], hub=hub)
            dh, d_norm_mix[l] = _nt_norm_bwd(f"l{l}_in_bwd", [dbcu], w_in_l, h_in, g_mix, dh2, hub=hub)

    small = jnp.concatenate(d_norm_mix + d_norm_mlp + [d_norm_kv, d_norm_final] + d_conv, axis=0)
    return sq_err, dh.reshape(bl, seq, D), small


def kernel(x, norm_mix, norm_mlp, w_a_in, conv_w, w_a_out, norm_kv, w_kv, w_q, w_o, w_up, w_down, norm_final, loss_target, m_norm_mix, m_norm_mlp, m_w_a_in, m_conv_w, m_w_a_out, m_norm_kv, m_w_kv, m_w_q, m_w_o, m_w_up, m_w_down, m_norm_final, v_norm_mix, v_norm_mlp, v_w_a_in, v_conv_w, v_w_a_out, v_norm_kv, v_w_kv, v_w_q, v_w_o, v_w_up, v_w_down, v_norm_final):
    D = x.shape[-1]
    xi, yi, ci = _mesh_pos()
    me_idx = 4 * xi + 2 * yi + ci
    w_big = dict(w_a_in=w_a_in, w_a_out=w_a_out, w_kv=w_kv, w_q=w_q, w_o=w_o, w_up=w_up, w_down=w_down)
    m_big = dict(w_a_in=m_w_a_in, w_a_out=m_w_a_out, w_kv=m_w_kv, w_q=m_w_q, w_o=m_w_o, w_up=m_w_up, w_down=m_w_down)
    v_big = dict(w_a_in=v_w_a_in, w_a_out=v_w_a_out, w_kv=v_w_kv, w_q=v_w_q, w_o=v_w_o, w_up=v_w_up, w_down=v_w_down)
    names = list(w_big)

    shards = {n: w.astype(BF16) for n, w in w_big.items()}
    landing = {n: lax.empty((N_DEV,) + w.shape, BF16) for n, w in w_big.items()}
    hub = _Hub(FETCH_DURING, PUSH_DURING, shards, landing)
    dc = conv_w.shape[-1]
    taps = conv_w.shape[0] * conv_w.shape[1]
    got = _all_gather("gather_first", [(shards[n], l) for n, l in FETCH_UP_FRONT] + [(_pad8(conv_w.reshape(taps, dc)), None)])
    for key, w in zip(FETCH_UP_FRONT, got):
        hub.weights[key] = w
    conv_full = jnp.moveaxis(got[-1][:, :taps], 0, 1).reshape(conv_w.shape[0], conv_w.shape[1], N_DEV * dc)

    sq_err, grad_x, small = _local_grads(x, loss_target, norm_mix, norm_mlp, norm_kv, norm_final, conv_full, hub)

    grads, deltas, new_m, new_v = {}, {}, {}, {}
    for n in names:
        shape = w_big[n].shape
        cols = shape[-1]
        flat = lambda t: t.reshape(-1, cols)
        parts = hub.landing[n].reshape(N_DEV, -1, cols)
        outs = _sum8_adamw(f"adamw_{n}", parts, flat(w_big[n]), flat(m_big[n]), flat(v_big[n]), tr=min(256, parts.shape[1]))
        grads[n], deltas[n], new_m[n], new_v[n] = (t.reshape(shape) for t in outs)

    n_gain = 2 * DEPTH + 2
    n_small = small.shape[0]
    small = jnp.concatenate([small, jnp.full((SUBLANES, D), sq_err, F32)], axis=0)
    rows_small = small.shape[0]
    small_all = _all_gather("gather_small_grads", [(small, None)])[0]

    def small_pack(nm, nl, nk, nf, cw):
        gains = jnp.concatenate([nm, nl, nk.reshape(1, D), nf.reshape(1, D)], axis=0)
        taps_full = lax.dynamic_update_slice(jnp.zeros((taps, D), F32), cw.reshape(taps, dc), (0, me_idx * dc))
        return jnp.concatenate([gains, taps_full, jnp.zeros((SUBLANES, D), F32)], axis=0)

    sp = [small_pack(*t) for t in ((norm_mix, norm_mlp, norm_kv, norm_final, conv_w),
                                   (m_norm_mix, m_norm_mlp, m_norm_kv, m_norm_final, m_conv_w),
                                   (v_norm_mix, v_norm_mlp, v_norm_kv, v_norm_final, v_conv_w))]
    small_out = _sum8_adamw("adamw_small", small_all, *sp, tr=rows_small)
    loss = small_out[0][n_small, 0] * (0.5 / D)

    def small_unpack(t):
        res = dict(norm_mix=t[0:DEPTH], norm_mlp=t[DEPTH:2 * DEPTH], norm_kv=t[2 * DEPTH], norm_final=t[2 * DEPTH + 1])
        res["conv_w"] = lax.dynamic_slice(t[n_gain:], (0, me_idx * dc), (taps, dc)).reshape(conv_w.shape)
        return res

    for dst, t in zip((grads, deltas, new_m, new_v), small_out):
        dst.update(small_unpack(t))

    order = ["norm_mix", "norm_mlp", "w_a_in", "conv_w", "w_a_out", "norm_kv", "w_kv", "w_q", "w_o", "w_up", "w_down",
             "norm_final"]
    return (loss, grad_x, *[grads[n] for n in order], *[deltas[n] for n in order], *[new_m[n] for n in order],
            *[new_v[n] for n in order])
```

```python
import functools

import jax
import jax.numpy as jnp
from jax import lax
from jax.experimental import pallas as pl
from jax.experimental.pallas import tpu as pltpu

F32 = jnp.float32
BF16 = jnp.bfloat16
SDS = jax.ShapeDtypeStruct

EPS = 1e-5
N_A_LAYERS = 2
DEPTH = 4
PATTERNS = ((128, 1), (512, 4), (2048, 16))
N_GROUPS = 3
H_G = 8
HEAD_DIM = 64
QW = H_G * HEAD_DIM
ATT_BLK = 128
ALIBI_MAX_BIAS = 8.0
NEG_INF = -1e30

ADAM_LR = 0.001
ADAM_B1 = 0.9
ADAM_B2 = 0.999
ADAM_EPS = 1e-08
ADAM_WD = 0.01
ADAM_STEP = 10

N_DEV = 8
SUBLANES = 8
HALO = 16
V7X_VMEM_LIMIT = 48 * 1024 * 1024
MXU_COLS = 256
MM_CHUNK = 512
MM_ROWS = 512
ADAMW_TILE = 256 * 1024
DW_TOKENS = 1024

FETCH_UP_FRONT = [("w_a_in", 0)]
FETCH_DURING = {
    "l0_in": [("w_a_out", 0), ("w_up", 0)], "l0_conv_out": [("w_down", 0)], "l0_up": [("w_a_in", 1), ("w_a_out", 1)], "l0_down": [("w_up", 1)],
    "l1_in": [("w_down", 1)], "l1_conv_out": [("w_kv", None)],
    "l1_up": [("w_q", 0), ("w_o", 0), ("w_q", 1), ("w_o", 1)], "l1_down": [("w_up", 2)],
    "kv": [("w_down", 2)], "l2_att0": [("w_up", 3)], "l2_att1": [("w_down", 3)],
}
PUSH_DURING = {
    "l3_dw_up": [("w_down", 3, 0, 2)], "l3_up_bwd": [("w_down", 3, 1, 2)],
    "l3_att0_dq": [("w_up", 3, 0, 2)], "l3_att0_dkv": [("w_up", 3, 1, 2)], "l3_att1_dq": [("w_o", 1)], "l3_q_bwd": [("w_q", 1)],
    "l2_dw_up": [("w_down", 2, 0, 2)], "l2_up_bwd": [("w_down", 2, 1, 2)],
    "l2_att0_dq": [("w_up", 2, 0, 2)], "l2_att0_dkv": [("w_up", 2, 1, 2)], "l2_att1_dq": [("w_o", 0)], "l2_q_bwd": [("w_q", 0)],
    "kv_bwd": [("w_kv", None, 0, 2)], "l1_down_bwd": [("w_kv", None, 1, 2)],
    "l1_dw_up": [("w_down", 1, 0, 2)], "l1_up_bwd": [("w_down", 1, 1, 2)], "l1_conv_bwd": [("w_up", 1, 0, 2)],
    "l1_dw_in": [("w_up", 1, 1, 2), ("w_a_out", 1)], "l1_in_bwd": [("w_a_in", 1, 0, 2)], "l0_down_bwd": [("w_a_in", 1, 1, 2)],
    "l0_dw_up": [("w_down", 0, 0, 2)], "l0_up_bwd": [("w_down", 0, 1, 2)], "l0_conv_bwd": [("w_up", 0, 0, 2)],
    "l0_dw_in": [("w_up", 0, 1, 2), ("w_a_out", 0)], "l0_in_bwd": [("w_a_in", 0)],
}


def _mesh_pos():
    return lax.axis_index("x"), lax.axis_index("y"), lax.axis_index("c")


def _flip(v, bit):
    return 1 - v if bit else v


class _Transfer:
    def __init__(self, kind, key, src, src_idx=None, dst=None, dst_idx=None, dst_shape=None, rows=None):
        self.kind, self.key, self.src, self.src_idx = kind, key, src, src_idx
        self.dst, self.dst_idx, self.dst_shape, self.rows = dst, dst_idx, dst_shape, rows

    def copies(self, src_ref, dst_ref, send_sems, recv_sems, local_sem):
        x, y, c = _mesh_pos()
        me = 4 * x + 2 * y + c
        part = (lambda r: r) if self.rows is None else (lambda r: r.at[pl.ds(*self.rows)])

        def dst_slot(j):
            r = dst_ref.at[j]
            return part(r if self.dst_idx is None else r.at[self.dst_idx])

        def copy(k, src, dst_j, to):
            return pltpu.make_async_remote_copy(
                src_ref=src, dst_ref=dst_slot(dst_j), send_sem=send_sems.at[k], recv_sem=recv_sems.at[k],
                device_id=to, device_id_type=pl.DeviceIdType.MESH)

        if self.kind == "exchange":
            local = pltpu.make_async_copy(part(src_ref.at[me]), dst_slot(me), local_sem)
            sends, arrivals = [], []
            for k in range(1, N_DEV):
                peer = (_flip(x, k & 4), _flip(y, k & 2), _flip(c, k & 1))
                peer_idx = 4 * peer[0] + 2 * peer[1] + peer[2]
                sends.append(copy(k - 1, part(src_ref.at[peer_idx]), me, peer))
                arrivals.append(copy(k - 1, part(src_ref.at[peer_idx]), peer_idx, peer))
            return local, sends, [], arrivals

        own = part(src_ref if self.src_idx is None else src_ref.at[self.src_idx])
        idx = lambda px, py, pc: 4 * px + 2 * py + pc
        sibling = (x, y, 1 - c)
        chips = [(1 - x, y), (x, 1 - y), (1 - x, 1 - y)]
        local = pltpu.make_async_copy(own, dst_slot(me), local_sem)
        sends = [copy(0, own, me, sibling)] + [copy(1 + j, own, me, (*chip, c)) for j, chip in enumerate(chips)]
        relays = [(copy(1 + j, own, idx(*chip, c), sibling), copy(4 + j, dst_slot(idx(*chip, c)), idx(*chip, c), sibling))
                  for j, chip in enumerate(chips)]
        arrivals = [copy(0, own, idx(*sibling), sibling)]
        arrivals += [copy(4 + j, own, idx(*chip, 1 - c), sibling) for j, chip in enumerate(chips)]
        return local, sends, relays, arrivals


class _Hub:
    def __init__(self, fetch, push, shards, landing):
        self.fetch, self.push, self.shards, self.landing = fetch, push, shards, landing
        self.weights = {}
        self.arriving = {}
        self.grads = {}

    def transfers(self, host):
        out = []
        for name, l, *part in self.fetch.get(host, ()):
            src = self.shards[name]
            shard = tuple(src.shape if l is None else src.shape[1:])
            p, n = part or (0, 1)
            rows = None if n == 1 else (p * (shard[0] // n), shard[0] // n)
            out.append(_Transfer("gather", (name, l, p == n - 1), src, src_idx=l, dst=self.arriving.get((name, l)),
                                 dst_shape=(N_DEV,) + shard, rows=rows))
        for name, l, *part in self.push.get(host, ()):
            src = self.grads[name, l]
            p, n = part or (0, 1)
            rows = None if n == 1 else (p * (src.shape[1] // n), src.shape[1] // n)
            out.append(_Transfer("exchange", (name, l, p == n - 1), src, dst=self.landing[name], dst_idx=l, rows=rows))
        return out

    def accept(self, transfers, results):
        for t, r in zip(transfers, results):
            name, l, complete = t.key
            if t.kind == "exchange":
                self.landing[name] = r
            elif complete:
                self.weights[name, l] = r
            else:
                self.arriving[name, l] = r


def _call(name, body, grid, ins, outs, scratch=(), hub=None):
    transfers = hub.transfers(name) if hub is not None else []
    n_in, n_out, n_scr, n_tr = len(ins), len(outs), len(scratch), len(transfers)
    c_in, c_out, aliases, places = [], [], {}, []
    for t in transfers:
        c_in.append(t.src)
        src_pos = len(c_in) - 1
        if t.dst is not None:
            c_in.append(t.dst)
            aliases[n_in + len(c_in) - 1] = n_out + len(c_out)
            c_out.append(SDS(t.dst.shape, t.dst.dtype))
        else:
            c_out.append(SDS(t.dst_shape, t.src.dtype))
        places.append((src_pos, len(c_out) - 1))
    sems = [pltpu.SemaphoreType.DMA((n_tr, N_DEV - 1)), pltpu.SemaphoreType.DMA((n_tr, N_DEV - 1)),
            pltpu.SemaphoreType.DMA((n_tr,))] if n_tr else []

    def wrapped(*refs):
        in_refs = refs[:n_in]
        cin_refs = refs[n_in:n_in + len(c_in)]
        o0 = n_in + len(c_in)
        out_refs = refs[o0:o0 + n_out]
        cout_refs = refs[o0 + n_out:o0 + n_out + len(c_out)]
        s0 = o0 + n_out + len(c_out)
        scr_refs = refs[s0:s0 + n_scr]
        if n_tr:
            send_sems, recv_sems, local_sems = refs[s0 + n_scr:]
            first = last = relay = None
            for ax, n in enumerate(grid):
                i = pl.program_id(ax)
                at_relay = (i == max(n - 2, 0)) if ax == len(grid) - 1 else (i == n - 1)
                first = (i == 0) if first is None else first & (i == 0)
                last = (i == n - 1) if last is None else last & (i == n - 1)
                relay = at_relay if relay is None else relay & at_relay

            def all_copies():
                return [t.copies(cin_refs[sp], cout_refs[dp], send_sems.at[n], recv_sems.at[n], local_sems.at[n])
                        for n, (t, (sp, dp)) in enumerate(zip(transfers, places))]

            @pl.when(first)
            def _():
                for local, sends, _, _ in all_copies():
                    local.start()
                    for cp in sends:
                        cp.start()

            def pass_on():
                @pl.when(relay)
                def _():
                    for _, _, relays, _ in all_copies():
                        for arrival, onward in relays:
                            arrival.wait_recv()
                            onward.start()

            if grid[-1] > 1:
                pass_on()

        body(*in_refs, *out_refs, *scr_refs)

        if n_tr:
            if grid[-1] == 1:
                pass_on()

            @pl.when(last)
            def _():
                for local, sends, relays, arrivals in all_copies():
                    for cp in arrivals:
                        cp.wait_recv()
                    for cp in sends + [onward for _, onward in relays]:
                        cp.wait_send()
                    local.wait()

    any_spec = pl.BlockSpec(memory_space=pl.ANY)
    res = pl.pallas_call(
        wrapped,
        name=name,
        grid=grid,
        in_specs=[s for _, s in ins] + [any_spec] * len(c_in),
        out_specs=[s for _, s in outs] + [any_spec] * len(c_out),
        out_shape=[o for o, _ in outs] + c_out,
        scratch_shapes=list(scratch) + sems,
        input_output_aliases=aliases,
        compiler_params=pltpu.CompilerParams(
            dimension_semantics=("arbitrary",) * len(grid), vmem_limit_bytes=V7X_VMEM_LIMIT),
    )(*[a for a, _ in ins], *c_in)
    if n_tr:
        hub.accept(transfers, res[n_out:])
    return res[:n_out]


def _rows(a, tm, cb=None, col=0):
    cb = cb or a.shape[1]
    return (a, pl.BlockSpec((tm, cb), lambda i: (i, col)))


def _full(a):
    nd = a.ndim
    return (a, pl.BlockSpec(a.shape, lambda i: (0,) * nd))


def _prev8(a, tm, cb, col):
    return (a, pl.BlockSpec((HALO, cb), lambda i: (jnp.maximum(i * (tm // HALO) - 1, 0), col)))


def _next8(a, tm, cb, col):
    last = a.shape[0] // HALO - 1
    return (a, pl.BlockSpec((HALO, cb), lambda i: (jnp.minimum((i + 1) * (tm // HALO), last), col)))


def _rows2(a, tt, cb=None, colfn=None):
    cb = cb or a.shape[1]
    colfn = colfn or (lambda s: 0)
    return (a, pl.BlockSpec((tt, cb), lambda s, t: (t, colfn(s))))


def _full2(a):
    nd = a.ndim
    return (a, pl.BlockSpec(a.shape, lambda s, t: (0,) * nd))


def _prev8_2(a, tt, cb, col):
    return (a, pl.BlockSpec((HALO, cb), lambda s, t: (jnp.maximum(t * (tt // HALO) - 1, 0), col)))


def _out_rows(T, n, dtype, tm):
    return (SDS((T, n), dtype), pl.BlockSpec((tm, n), lambda i: (i, 0)))


def _out_acc8(d):
    return (SDS((SUBLANES, d), F32), pl.BlockSpec((SUBLANES, d), lambda i: (0, 0)))


def _rstd(x):
    return lax.rsqrt(jnp.mean(x * x, axis=-1, keepdims=True) + EPS)


def _normed(h_ref, g_ref):
    x = h_ref[...]
    return x * _rstd(x) * g_ref[...]


def _acc8(ref, val, i, n):
    part = val.reshape(-1, SUBLANES, val.shape[-1]).sum(axis=0)

    @pl.when(i == 0)
    def _():
        ref[...] = part

    @pl.when(i > 0)
    def _():
        ref[...] += part

    @pl.when(i == n - 1)
    def _():
        ref[...] = jnp.broadcast_to(jnp.sum(ref[...], axis=0, keepdims=True), ref.shape)


def _gate(b_ref, c_ref, u_ref, ch_ref, uh_ref, cw_ref, first):
    b, c, u = (r[...].astype(F32) for r in (b_ref, c_ref, u_ref))
    cu = c * u
    halo = jnp.where(first, 0.0, ch_ref[...].astype(F32) * uh_ref[...].astype(F32))
    rows = lax.broadcasted_iota(jnp.int32, cu.shape, 0)
    h1 = halo[HALO - 1:HALO, :]
    h2 = halo[HALO - 2:HALO - 1, :]
    cu1 = jnp.where(rows == 0, h1, pltpu.roll(cu, 1, 0))
    cu2 = jnp.where(rows == 0, h2, jnp.where(rows == 1, h1, pltpu.roll(cu, 2, 0)))
    conv = cw_ref[0:1, :] * cu + cw_ref[1:2, :] * cu1 + cw_ref[2:3, :] * cu2
    return b * conv, (b, c, u), conv, (cu, cu1, cu2)


def _relu2(a_ref):
    r = jnp.maximum(a_ref[...].astype(F32), 0.0)
    return r * r


def _dot(a, b):
    return jnp.dot(a, b, preferred_element_type=F32)


def _dot_nt(a, b):
    return lax.dot_general(a, b, (((1,), (1,)), ((), ())), preferred_element_type=F32)


def _dot_tn(a, b):
    return lax.dot_general(a, b, (((0,), (0,)), ((), ())), preferred_element_type=F32)


def _chunks(n):
    c = min(MM_CHUNK, n)
    while n % c:
        c -= 128
    assert c > 0, n
    return [(k * c, (k + 1) * c) for k in range(n // c)]


def _col_weight(w):
    _, K, ns = w.shape
    N = N_DEV * ns
    direct = ns % MXU_COLS == 0
    scratch = [] if direct else [pltpu.VMEM((K, N), BF16)]

    def prepare(w_ref, s_ref, step):
        if direct:
            return

        @pl.when(step == 0)
        def _():
            for j in range(N_DEV):
                s_ref[:, j * ns:(j + 1) * ns] = w_ref[j]

    def chunks(w_ref, s_ref):
        if direct:
            return [(j * ns, (j + 1) * ns, (lambda j=j: w_ref[j])) for j in range(N_DEV)]
        return [(lo, hi, (lambda lo=lo, hi=hi: s_ref[:, lo:hi])) for lo, hi in _chunks(N)]

    return N, scratch, prepare, chunks


def _out_cols(n, T, tm):
    return (SDS((n, T), BF16), pl.BlockSpec((n, tm), lambda i: (0, i)))


def _norm_mm(name, h, g, w, tm=MM_ROWS, out_dtype=F32, transposed=False, hub=None):
    T, D = h.shape
    N, w_scratch, prepare, chunks = _col_weight(w)

    def body(h_ref, g_ref, w_ref, o_ref, *rest):
        at_ref, s = (rest[0], rest[1:]) if transposed else (None, rest)
        s_ref = s[0] if s else None
        prepare(w_ref, s_ref, pl.program_id(0))
        a32 = _normed(h_ref, g_ref)
        a = a32.astype(BF16)
        for lo, hi, load in chunks(w_ref, s_ref):
            o_ref[:, lo:hi] = _dot(a, load()).astype(out_dtype)
        if transposed:
            at_ref[...] = a32.T.astype(BF16)

    outs = [_out_rows(T, N, out_dtype, tm)] + ([_out_cols(D, T, tm)] if transposed else [])
    res = _call(name, body, (T // tm,), [_rows(h, tm), _full(g), _full(w)], outs, scratch=w_scratch, hub=hub)
    return res if transposed else res[0]


def _gate_mm_res(name, bcu, cw, w, h, seq, tm=MM_ROWS, hub=None):
    T, D = h.shape

    def body(b_ref, c_ref, u_ref, ch_ref, uh_ref, cw_ref, w_ref, h_ref, o_ref, gt_ref):
        first = (pl.program_id(0) * tm) % seq == 0
        gated32 = _gate(b_ref, c_ref, u_ref, ch_ref, uh_ref, cw_ref, first)[0]
        gated = gated32.astype(BF16)
        for lo, hi in _chunks(D):
            o_ref[:, lo:hi] = h_ref[:, lo:hi] + _dot(gated, w_ref[:, lo:hi])
        gt_ref[...] = gated32.T.astype(BF16)

    ins = [_rows(bcu, tm, D, 0), _rows(bcu, tm, D, 1), _rows(bcu, tm, D, 2), _prev8(bcu, tm, D, 1),
           _prev8(bcu, tm, D, 2), _full(cw), _full(w), _rows(h, tm)]
    return _call(name, body, (T // tm,), ins, [_out_rows(T, D, F32, tm), _out_cols(D, T, tm)], hub=hub)


def _relu2_mm_res(name, a, w, h, tm=MM_ROWS, hub=None):
    T, D = h.shape
    K = a.shape[1]

    def body(a_ref, w_ref, h_ref, o_ref, acc_ref):
        for n, (lo, hi) in enumerate(_chunks(K)):
            d = _dot(_relu2(a_ref.at[:, lo:hi]).astype(BF16), w_ref[lo:hi, :])
            if n == 0:
                acc_ref[...] = d
            else:
                acc_ref[...] += d
        o_ref[...] = h_ref[...] + acc_ref[...]

    return _call(name, body, (T // tm,), [_rows(a, tm), _full(w), _rows(h, tm)], [_out_rows(T, D, F32, tm)],
                 scratch=[pltpu.VMEM((tm, D), F32)], hub=hub)[0]


def _combine_mm_res(name, os_, lses, w, h, tm=MM_ROWS):
    T, D = h.shape
    _, w_scratch, prepare, chunks = _col_weight(w)

    def body(o0, o1, o2, l0, l1, l2, w_ref, h_ref, o_ref, lse_ref, out_ref, *s):
        s_ref = s[0] if s else None
        prepare(w_ref, s_ref, pl.program_id(0))
        ls = [l0[...], l1[...], l2[...]]
        mx = jnp.maximum(jnp.maximum(ls[0], ls[1]), ls[2])
        es = [jnp.exp(l - mx) for l in ls]
        den = es[0] + es[1] + es[2]
        o = (es[0] * o0[...] + es[1] * o1[...] + es[2] * o2[...]) / den
        o_ref[...] = o
        lse_ref[...] = mx + jnp.log(den)
        ob = o.astype(BF16)
        for lo, hi, load in chunks(w_ref, s_ref):
            out_ref[:, lo:hi] = h_ref[:, lo:hi] + _dot(ob, load())

    ins = [_rows(t, tm) for t in list(os_) + list(lses)] + [_full(w), _rows(h, tm)]
    outs = [_out_rows(T, QW, F32, tm), _out_rows(T, QW, F32, tm), _out_rows(T, D, F32, tm)]
    return _call(name, body, (T // tm,), ins, outs, scratch=w_scratch)


def _nt_relu2_bwd(name, dh, w, a, tm=MM_ROWS, hub=None):
    T, _ = dh.shape
    K = w.shape[0]

    def body(dh_ref, w_ref, a_ref, o_ref):
        d = dh_ref[...].astype(BF16)
        for lo, hi in _chunks(K):
            dr = _dot_nt(d, w_ref[lo:hi, :])
            o_ref[:, lo:hi] = (dr * (2.0 * jnp.maximum(a_ref[:, lo:hi].astype(F32), 0.0))).astype(BF16)

    return _call(name, body, (T // tm,), [_rows(dh, tm), _full(w), _rows(a, tm)], [_out_rows(T, K, BF16, tm)], hub=hub)[0]


def _concat_bf16(*refs):
    vals = [r[...].astype(BF16) for r in refs]
    return vals[0] if len(vals) == 1 else jnp.concatenate(vals, axis=1)


def _nt_plain(name, dy, w, tm=MM_ROWS):
    T, N = dy.shape
    if w.ndim == 3:
        K = w.shape[1]
        _, w_scratch, prepare, chunks = _col_weight(w)
    else:
        K = w.shape[0]
        w_scratch, prepare = [], (lambda w_ref, s_ref, step: None)
        chunks = lambda w_ref, s_ref: [(lo, hi, (lambda lo=lo, hi=hi: w_ref[:, lo:hi])) for lo, hi in _chunks(N)]

    def body(dy_ref, w_ref, o_ref, acc_ref, *s):
        s_ref = s[0] if s else None
        prepare(w_ref, s_ref, pl.program_id(0))
        for n, (lo, hi, load) in enumerate(chunks(w_ref, s_ref)):
            d = _dot_nt(dy_ref[:, lo:hi].astype(BF16), load())
            if n == 0:
                acc_ref[...] = d
            else:
                acc_ref[...] += d
        o_ref[...] = acc_ref[...]

    return _call(name, body, (T // tm,), [_rows(dy, tm), _full(w)], [_out_rows(T, K, F32, tm)],
                 scratch=[pltpu.VMEM((tm, K), F32)] + w_scratch)[0]


def _att_out_bwd(name, dy, w, o, tm=MM_ROWS):
    T, _ = dy.shape
    K = w.shape[1]
    _, w_scratch, prepare, chunks = _col_weight(w)

    def body(dy_ref, w_ref, o_ref, do_ref, dl_ref, acc_ref, *s):
        s_ref = s[0] if s else None
        prepare(w_ref, s_ref, pl.program_id(0))
        for n, (lo, hi, load) in enumerate(chunks(w_ref, s_ref)):
            d = _dot_nt(dy_ref[:, lo:hi].astype(BF16), load())
            if n == 0:
                acc_ref[...] = d
            else:
                acc_ref[...] += d
        do = acc_ref[...]
        do_ref[...] = do
        prod = do * o_ref[...]
        high = prod.astype(BF16)
        low = (prod - high.astype(F32)).astype(BF16)
        head_of = lambda axis: jnp.right_shift(lax.broadcasted_iota(jnp.int32, (K, K), axis), HEAD_DIM.bit_length() - 1)
        same_head = jnp.where(head_of(0) == head_of(1), 1.0, 0.0).astype(BF16)
        dl_ref[...] = _dot(high, same_head) + _dot(low, same_head)

    outs = [_out_rows(T, K, F32, tm), _out_rows(T, K, F32, tm)]
    return _call(name, body, (T // tm,), [_rows(dy, tm), _full(w), _rows(o, tm)], outs,
                 scratch=[pltpu.VMEM((tm, K), F32)] + w_scratch)


def _nt_norm_bwd(name, dys, w, h, g, dh_in, tm=MM_ROWS, hub=None):
    T, D = h.shape
    _, w_scratch, prepare, chunks = _col_weight(w)
    n_steps = T // tm
    n_dy = len(dys)

    def body(*refs):
        dy_refs = refs[:n_dy]
        w_ref, h_ref, g_ref, dhin_ref, o_ref, dg_ref, acc_ref = refs[n_dy:n_dy + 7]
        s_ref = refs[n_dy + 7] if len(refs) > n_dy + 7 else None
        i = pl.program_id(0)
        prepare(w_ref, s_ref, i)
        dy = _concat_bf16(*dy_refs)
        for n, (lo, hi, load) in enumerate(chunks(w_ref, s_ref)):
            d = _dot_nt(dy[:, lo:hi], load())
            if n == 0:
                acc_ref[...] = d
            else:
                acc_ref[...] += d
        dn = acc_ref[...]
        x = h_ref[...]
        rstd = _rstd(x)
        xhat = x * rstd
        dxhat = dn * g_ref[...]
        dx = rstd * (dxhat - xhat * jnp.mean(dxhat * xhat, axis=-1, keepdims=True))
        o_ref[...] = dhin_ref[...] + dx
        _acc8(dg_ref, dn * xhat, i, n_steps)

    ins = [_rows(d, tm) for d in dys] + [_full(w), _rows(h, tm), _full(g), _rows(dh_in, tm)]
    outs = [_out_rows(T, D, F32, tm), _out_acc8(D)]
    dh, dg = _call(name, body, (n_steps,), ins, outs, scratch=[pltpu.VMEM((tm, D), F32)] + w_scratch, hub=hub)
    return dh, dg[0:1]


def _cols2(a_t, tt, kb=None):
    kb = kb or a_t.shape[0]
    return (a_t, pl.BlockSpec((kb, tt), (lambda s, t: (s, t)) if kb != a_t.shape[0] else (lambda s, t: (0, t))))


def _tn(name, a_ins, a_fn, y_ins, y_fn, K, N, T, tt, split=None, out_cols=None, hub=None):
    kind, parts = split or ("n", 1)
    kb, nb = (K // parts, N) if kind == "k" else (K, N // parts)
    n_steps = T // tt
    n_a = len(a_ins)
    n_y = len(y_ins)
    assert out_cols is None or (kind == "n" and nb % out_cols == 0)

    def body(*refs):
        a_refs = refs[:n_a]
        y_refs = refs[n_a:n_a + n_y]
        o_ref, acc_ref = refs[n_a + n_y:]
        t = pl.program_id(1)
        a_t = a_refs[0][...] if a_fn is None else a_fn(*a_refs).T.astype(BF16)
        y = y_fn(*y_refs).astype(BF16)
        for lo, hi in _chunks(nb):
            d = _dot(a_t, y[:, lo:hi])

            @pl.when(t == 0)
            def _():
                acc_ref[:, lo:hi] = d

            @pl.when(t > 0)
            def _():
                acc_ref[:, lo:hi] += d

        @pl.when(t == n_steps - 1)
        def _():
            if out_cols is None:
                o_ref[...] = acc_ref[...].astype(BF16)
            else:
                for j in range(nb // out_cols):
                    o_ref[j] = acc_ref[:, j * out_cols:(j + 1) * out_cols].astype(BF16)

    if out_cols is None:
        out = (SDS((K, N), BF16), pl.BlockSpec((kb, nb), (lambda s, t: (s, 0)) if kind == "k" else (lambda s, t: (0, s))))
    else:
        out = (SDS((N // out_cols, K, out_cols), BF16), pl.BlockSpec((nb // out_cols, K, out_cols), lambda s, t: (s, 0, 0)))
    return _call(name, body, (parts, n_steps), list(a_ins) + list(y_ins), [out],
                 scratch=[pltpu.VMEM((kb, nb), F32)], hub=hub)[0]


def _val(ref):
    return ref[...]


def _concat_f32(*refs):
    vals = [r[...] for r in refs]
    return vals[0] if len(vals) == 1 else jnp.concatenate(vals, axis=1)


ATT_TILE_ROWS = 2048
HEAD_PAIRS = H_G // 2
ATT_SCALE = HEAD_DIM ** -0.5
ATT_UNITS_TOGETHER = 4


def _slope(h):
    return 2.0 ** (-ALIBI_MAX_BIAS * (h + 1) / H_G)


def _att_geom(T, bl, g):
    dil = PATTERNS[g][1]
    sub = ATT_BLK * dil
    nsub = max(1, ATT_TILE_ROWS // sub)
    rows = sub * nsub
    return dil, sub, nsub, rows, T // bl // rows


def _att_specs(T, bl, g):
    _, sub, nsub, rows, nt = _att_geom(T, bl, g)
    last_sub = T // sub - 1
    tile = lambda col: pl.BlockSpec((rows, 128), lambda b, i, hp: (b * nt + i, col(hp)))
    prev = lambda col: pl.BlockSpec((sub, 128), lambda b, i, hp: (jnp.maximum((b * nt + i) * nsub - 1, 0), col(hp)))
    nxt = lambda col: pl.BlockSpec((sub, 128), lambda b, i, hp: (jnp.minimum((b * nt + i + 1) * nsub, last_sub), col(hp)))
    return tile, prev, nxt


def _sub_rows(j, r, dil):
    start = j * ATT_BLK * dil + r
    return pl.ds(start, ATT_BLK, stride=dil) if dil > 1 else pl.ds(start, ATT_BLK)


class _Residues:
    def __init__(self, dil):
        self.dil = dil
        self.whole = dil % SUBLANES == 0
        self.read, self.written = {}, {}

    def _block(self, j):
        return pl.ds(j * ATT_BLK * self.dil, ATT_BLK * self.dil)

    def load(self, ref, j, r):
        if not self.whole:
            return ref[_sub_rows(j, r, self.dil), :]
        if (id(ref), j) not in self.read:
            rows = ref[self._block(j), :]
            self.read[id(ref), j] = jnp.swapaxes(rows.reshape(ATT_BLK, self.dil, rows.shape[-1]), 0, 1)
        return self.read[id(ref), j][r]

    def store(self, ref, j, r, val):
        if not self.whole:
            ref[_sub_rows(j, r, self.dil), :] = val
            return
        got = self.written.setdefault((id(ref), j), {})
        got[r] = val
        if len(got) == self.dil:
            merged = jnp.swapaxes(jnp.stack([got[k] for k in range(self.dil)], axis=0), 0, 1)
            ref[self._block(j), :] = merged.reshape(ATT_BLK * self.dil, val.shape[-1])
            del self.written[id(ref), j]


def _att_consts(hp, dil, keys_first=False):
    h0 = lax.broadcasted_iota(jnp.int32, (ATT_BLK, 128), 1) < HEAD_DIM
    a = lax.broadcasted_iota(jnp.int32, (ATT_BLK, ATT_BLK), 1 if keys_first else 0)
    c = lax.broadcasted_iota(jnp.int32, (ATT_BLK, ATT_BLK), 0 if keys_first else 1)
    dist_p = ((ATT_BLK + a - c) * dil).astype(F32)
    dist_c = ((a - c) * dil).astype(F32)
    bias_p, bias_c = [], []
    for h in range(2):
        slope = jnp.float32(_slope(2 * (HEAD_PAIRS - 1) + h))
        for p in range(HEAD_PAIRS - 2, -1, -1):
            slope = jnp.where(hp == p, jnp.float32(_slope(2 * p + h)), slope)
        bias_p.append(jnp.where(c >= a, -slope * dist_p, NEG_INF))
        bias_c.append(jnp.where(c <= a, -slope * dist_c, NEG_INF))
    return h0, jnp.concatenate(bias_p, axis=0), jnp.concatenate(bias_c, axis=0)


def _stack_heads(x, h0):
    return jnp.concatenate([jnp.where(h0, x, 0.0), jnp.where(h0, 0.0, x)], axis=0).astype(BF16)


def _unstack_heads(x, h0):
    return jnp.where(h0, x[:ATT_BLK], x[ATT_BLK:])


def _stack_cols(x):
    return jnp.concatenate(_head_cols(x), axis=0)


def _split_heads(x, h0):
    return [jnp.where(h0, x, 0.0).astype(BF16), jnp.where(h0, 0.0, x).astype(BF16)]


def _head_cols(x):
    return [x[:, 0:1], x[:, HEAD_DIM:HEAD_DIM + 1]]


def _in_groups(units, first_stage, *later_stages):
    for u0 in range(0, len(units), ATT_UNITS_TOGETHER):
        staged = [first_stage(*u) for u in units[u0:u0 + ATT_UNITS_TOGETHER]]
        for stage in later_stages:
            staged = [stage(*s) for s in staged]


def _attn_fwd(name, q, kv, g, bl, hub=None):
    T = q.shape[0]
    dil, _, nsub, _, _ = _att_geom(T, bl, g)
    tile, prev, _ = _att_specs(T, bl, g)

    def body(q_ref, kp_ref, kc_ref, vp_ref, vc_ref, o_ref, lse_ref):
        first = pl.program_id(1) == 0
        h0, bias_p, bias_c = _att_consts(pl.program_id(2), dil)
        bias_first = jnp.where(first, NEG_INF, bias_p)
        rows = _Residues(dil)

        def with_ones(v):
            return [jnp.where(h0, v, 1.0).astype(BF16), jnp.where(h0, 1.0, v).astype(BF16)]

        def scores(j, r):
            if j == 0:
                kp, vp, bp = rows.load(kp_ref, 0, r), rows.load(vp_ref, 0, r), bias_first
            else:
                kp, vp, bp = rows.load(kc_ref, j - 1, r), rows.load(vc_ref, j - 1, r), bias_p
            kp, kc = kp.astype(BF16), rows.load(kc_ref, j, r).astype(BF16)
            qs = _stack_heads(rows.load(q_ref, j, r) * ATT_SCALE, h0)
            sp = _dot_nt(qs, kp) + bp
            sc = _dot_nt(qs, kc) + bias_c
            return (j, r), sp, sc, with_ones(vp), with_ones(rows.load(vc_ref, j, r))

        def weights(unit, sp, sc, vp, vc):
            mx = jnp.max(jnp.maximum(sp, sc), axis=-1, keepdims=True)
            return unit, mx, jnp.exp(sp - mx).astype(BF16), jnp.exp(sc - mx).astype(BF16), vp, vc

        def outputs(unit, mx, ep, ec, vp, vc):
            heads = [slice(h * ATT_BLK, (h + 1) * ATT_BLK) for h in range(2)]
            acc = [_dot(ep[hs], vp[h]) + _dot(ec[hs], vc[h]) for h, hs in enumerate(heads)]
            den = [pltpu.roll(a, HEAD_DIM, 1) for a in acc]
            rows.store(o_ref, *unit, jnp.where(h0, acc[0] / den[0], acc[1] / den[1]))
            rows.store(lse_ref, *unit, jnp.where(h0, mx[heads[0]] + jnp.log(den[0]), mx[heads[1]] + jnp.log(den[1])))
            return ()

        _in_groups([(j, r) for j in range(nsub) for r in range(dil)], scores, weights, outputs)

    ins = [(q, tile(lambda hp: 4 * g + hp)), (kv, prev(lambda hp: 8 * g + hp)), (kv, tile(lambda hp: 8 * g + hp)),
           (kv, prev(lambda hp: 8 * g + 4 + hp)), (kv, tile(lambda hp: 8 * g + 4 + hp))]
    out = (SDS((T, QW), F32), tile(lambda hp: hp))
    _, _, _, _, nt = _att_geom(T, bl, g)
    return _call(name, body, (bl, nt, HEAD_PAIRS), ins, [out, out], hub=hub)


def _attn_bwd_dq(name, q, kv, do, delta, lse, g, bl, hub=None):
    T = q.shape[0]
    dil, _, nsub, _, nt = _att_geom(T, bl, g)
    tile, prev, _ = _att_specs(T, bl, g)

    def body(q_ref, kp_ref, kc_ref, vp_ref, vc_ref, do_ref, dl_ref, lse_ref, dq_ref):
        first = pl.program_id(1) == 0
        h0, bias_p, bias_c = _att_consts(pl.program_id(2), dil)
        bias_first = jnp.where(first, NEG_INF, bias_p)
        rows = _Residues(dil)

        def probs(j, r):
            if j == 0:
                kp, vp, bp = rows.load(kp_ref, 0, r), rows.load(vp_ref, 0, r), bias_first
            else:
                kp, vp, bp = rows.load(kc_ref, j - 1, r), rows.load(vc_ref, j - 1, r), bias_p
            kp, vp = kp.astype(BF16), vp.astype(BF16)
            kc, vc = rows.load(kc_ref, j, r).astype(BF16), rows.load(vc_ref, j, r).astype(BF16)
            qs = _stack_heads(rows.load(q_ref, j, r) * ATT_SCALE, h0)
            dos = _stack_heads(rows.load(do_ref, j, r), h0)
            lse = _stack_cols(rows.load(lse_ref, j, r))
            pp = jnp.exp(_dot_nt(qs, kp) + bp - lse)
            pc = jnp.exp(_dot_nt(qs, kc) + bias_c - lse)
            return (j, r), pp, pc, _dot_nt(dos, vp), _dot_nt(dos, vc), kp, kc

        def dscores(unit, pp, pc, dpp, dpc, kp, kc):
            dl = _stack_cols(rows.load(dl_ref, *unit))
            return unit, (pp * (dpp - dl)).astype(BF16), (pc * (dpc - dl)).astype(BF16), kp, kc

        def outputs(unit, dsp, dsc, kp, kc):
            rows.store(dq_ref, *unit, _unstack_heads(_dot(dsp, kp) + _dot(dsc, kc), h0) * ATT_SCALE)
            return ()

        _in_groups([(j, r) for j in range(nsub) for r in range(dil)], probs, dscores, outputs)

    own = lambda hp: hp
    ins = [(q, tile(lambda hp: 4 * g + hp)), (kv, prev(lambda hp: 8 * g + hp)), (kv, tile(lambda hp: 8 * g + hp)),
           (kv, prev(lambda hp: 8 * g + 4 + hp)), (kv, tile(lambda hp: 8 * g + 4 + hp)),
           (do, tile(own)), (delta, tile(own)), (lse, tile(own))]
    return _call(name, body, (bl, nt, HEAD_PAIRS), ins, [(SDS((T, QW), F32), tile(own))], hub=hub)[0]


def _attn_bwd_dkv(name, q, kv, do, delta, lse, g, bl, prev=None, hub=None):
    T = q.shape[0]
    dil, _, nsub, _, nt = _att_geom(T, bl, g)
    tile, _, nxt = _att_specs(T, bl, g)
    has_prev = prev is not None

    def body(*refs):
        k_ref, v_ref, q_ref, qn_ref, do_ref, don_ref, dl_ref, dln_ref, l_ref, ln_ref = refs[:10]
        rest = refs[10:]
        if has_prev:
            dkp_ref, dvp_ref, dk_ref, dv_ref = rest
        else:
            dk_ref, dv_ref = rest
        last = pl.program_id(1) == nt - 1
        h0, bias_p, bias_c = _att_consts(pl.program_id(2), dil, keys_first=True)
        bias_last = jnp.where(last, NEG_INF, bias_p)
        rows = _Residues(dil)

        def per_query_rows(x):
            xt = x.T
            return jnp.concatenate([jnp.broadcast_to(xt[0:1], (ATT_BLK, ATT_BLK)),
                                    jnp.broadcast_to(xt[HEAD_DIM:HEAD_DIM + 1], (ATT_BLK, ATT_BLK))], axis=0)

        def probs(j, r):
            ks = _stack_heads(rows.load(k_ref, j, r), h0)
            vs = _stack_heads(rows.load(v_ref, j, r), h0)
            sets = [(q_ref, do_ref, dl_ref, l_ref, j, bias_c)]
            if j < nsub - 1:
                sets.append((q_ref, do_ref, dl_ref, l_ref, j + 1, bias_p))
            else:
                sets.append((qn_ref, don_ref, dln_ref, ln_ref, 0, bias_last))
            out = []
            for qr, dor, dlr, lr, jq, bias in sets:
                qsb = (rows.load(qr, jq, r) * ATT_SCALE).astype(BF16)
                do2b = rows.load(dor, jq, r).astype(BF16)
                p = jnp.exp(_dot_nt(ks, qsb) + bias - per_query_rows(rows.load(lr, jq, r)))
                out.append((p, _dot_nt(vs, do2b), dlr, jq, qsb, do2b))
            return (j, r), out

        def dscores(unit, sets):
            out = []
            for p, dp, dlr, jq, qsb, do2b in sets:
                ds = (p * (dp - per_query_rows(rows.load(dlr, jq, unit[1])))).astype(BF16)
                out.append((p.astype(BF16), ds, qsb, do2b))
            return unit, out

        def outputs(unit, sets):
            dk_st = dv_st = None
            for pb, ds, qsb, do2b in sets:
                dvs, dks = _dot(pb, do2b), _dot(ds, qsb)
                dv_st = dvs if dv_st is None else dv_st + dvs
                dk_st = dks if dk_st is None else dk_st + dks
            dk2 = _unstack_heads(dk_st, h0)
            dv2 = _unstack_heads(dv_st, h0)
            if has_prev:
                dk2 = dk2 + rows.load(dkp_ref, *unit)
                dv2 = dv2 + rows.load(dvp_ref, *unit)
            rows.store(dk_ref, *unit, dk2)
            rows.store(dv_ref, *unit, dv2)
            return ()

        _in_groups([(j, r) for j in range(nsub) for r in range(dil)], probs, dscores, outputs)

    own = lambda hp: hp
    qcol = lambda hp: 4 * g + hp
    ins = [(kv, tile(lambda hp: 8 * g + hp)), (kv, tile(lambda hp: 8 * g + 4 + hp)), (q, tile(qcol)), (q, nxt(qcol)),
           (do, tile(own)), (do, nxt(own)), (delta, tile(own)), (delta, nxt(own)), (lse, tile(own)), (lse, nxt(own))]
    if has_prev:
        ins += [(prev[0], tile(own)), (prev[1], tile(own))]
    out = (SDS((T, QW), F32), tile(own))
    return _call(name, body, (bl, nt, HEAD_PAIRS), ins, [out, out], hub=hub)


def _final_loss(name, h, tgt, g, tm=256):
    T, D = h.shape
    n_steps = T // tm

    def body(h_ref, t_ref, g_ref, dh_ref, loss_ref, dg_ref, sq_ref):
        i = pl.program_id(0)
        x = h_ref[...]
        rstd = _rstd(x)
        xhat = x * rstd
        err = xhat * g_ref[...] - t_ref[...]
        _acc8(sq_ref, err * err, i, n_steps)
        dy = err * (1.0 / D)
        dxhat = dy * g_ref[...]
        dh_ref[...] = rstd * (dxhat - xhat * jnp.mean(dxhat * xhat, axis=-1, keepdims=True))
        _acc8(dg_ref, dy * xhat, i, n_steps)

        @pl.when(i == n_steps - 1)
        def _():
            loss_ref[...] = jnp.full(loss_ref.shape, jnp.sum(sq_ref[0:1, :]), F32)

    outs = [_out_rows(T, D, F32, tm), (SDS((SUBLANES, 128), F32), pl.BlockSpec((SUBLANES, 128), lambda i: (0, 0))),
            _out_acc8(D)]
    dh, loss, dg = _call(name, body, (n_steps,), [_rows(h, tm), _rows(tgt, tm), _full(g)], outs,
                         scratch=[pltpu.VMEM((SUBLANES, D), F32)])
    return dh, loss[0, 0], dg[0:1]


def _conv_bwd(name, bcu, dgated, cw, seq, tm=256, hub=None):
    T, D = dgated.shape
    n_steps = T // tm

    def body(b_ref, c_ref, u_ref, ch_ref, uh_ref, dg_ref, dgn_ref, bn_ref, cw_ref, o_ref, t0_ref, t1_ref, t2_ref):
        i = pl.program_id(0)
        first = (i * tm) % seq == 0
        last = ((i + 1) * tm) % seq == 0
        _, (b, c, u), conv, (cu, cu1, cu2) = _gate(b_ref, c_ref, u_ref, ch_ref, uh_ref, cw_ref, first)
        dgat = dg_ref[...]
        dconv = dgat * b
        nxt = jnp.where(last, 0.0, dgn_ref[...] * bn_ref[...].astype(F32))
        rows = lax.broadcasted_iota(jnp.int32, dconv.shape, 0)
        n1 = nxt[0:1, :]
        n2 = nxt[1:2, :]
        dc1 = jnp.where(rows == tm - 1, n1, pltpu.roll(dconv, tm - 1, 0))
        dc2 = jnp.where(rows == tm - 1, n2, jnp.where(rows == tm - 2, n1, pltpu.roll(dconv, tm - 2, 0)))
        dcu = cw_ref[0:1, :] * dconv + cw_ref[1:2, :] * dc1 + cw_ref[2:3, :] * dc2
        o_ref[:, 0:D] = (dgat * conv).astype(BF16)
        o_ref[:, D:2 * D] = (dcu * u).astype(BF16)
        o_ref[:, 2 * D:3 * D] = (dcu * c).astype(BF16)
        _acc8(t0_ref, dconv * cu, i, n_steps)
        _acc8(t1_ref, dconv * cu1, i, n_steps)
        _acc8(t2_ref, dconv * cu2, i, n_steps)

    ins = [_rows(bcu, tm, D, 0), _rows(bcu, tm, D, 1), _rows(bcu, tm, D, 2), _prev8(bcu, tm, D, 1), _prev8(bcu, tm, D, 2),
           _rows(dgated, tm), _next8(dgated, tm, D, 0), _next8(bcu, tm, D, 0), _full(cw)]
    outs = [_out_rows(T, 3 * D, BF16, tm), _out_acc8(D), _out_acc8(D), _out_acc8(D)]
    dbcu, t0, t1, t2 = _call(name, body, (n_steps,), ins, outs, hub=hub)
    return dbcu, jnp.concatenate([t0[0:1], t1[0:1], t2[0:1]], axis=0)


def _sum8_adamw(name, parts, w, m, v, tr):
    R, C = w.shape
    b1c = 1.0 - ADAM_B1 ** ADAM_STEP
    b2c = 1.0 - ADAM_B2 ** ADAM_STEP

    def body(p_ref, w_ref, m_ref, v_ref, g_ref, d_ref, nm_ref, nv_ref):
        g = p_ref[0].astype(F32)
        for j in range(1, N_DEV):
            g = g + p_ref[j].astype(F32)
        nm = ADAM_B1 * m_ref[...] + (1.0 - ADAM_B1) * g
        nv = ADAM_B2 * v_ref[...] + (1.0 - ADAM_B2) * (g * g)
        m_hat = nm / b1c
        v_hat = nv / b2c
        g_ref[...] = g
        d_ref[...] = -ADAM_LR * (m_hat / (jnp.sqrt(v_hat) + ADAM_EPS) + ADAM_WD * w_ref[...])
        nm_ref[...] = nm
        nv_ref[...] = nv

    ins = [(parts, pl.BlockSpec((N_DEV, tr, C), lambda i: (0, i, 0))), _rows(w, tr), _rows(m, tr), _rows(v, tr)]
    outs = [_out_rows(R, C, F32, tr)] * 4
    return _call(name, body, (R // tr,), ins, outs)


def _all_gather(name, items):
    n = len(items)
    shapes = [tuple(a.shape if idx is None else a.shape[1:]) for a, idx in items]

    def body(*refs):
        x_refs, out_refs = refs[:n], refs[n:2 * n]
        send_sems, recv_sems, local_sems = refs[2 * n:]
        x, y, c = _mesh_pos()
        me, sibling = (x, y, c), (x, y, 1 - c)
        chips = [(1 - x, y), (x, 1 - y), (1 - x, 1 - y)]

        def copy(t, k, block, to, own=False):
            dst = out_refs[t].at[4 * block[0] + 2 * block[1] + block[2]]
            src = dst
            if own:
                src = x_refs[t] if items[t][1] is None else x_refs[t].at[items[t][1]]
            return pltpu.make_async_remote_copy(
                src_ref=src, dst_ref=dst, send_sem=send_sems.at[t, k], recv_sem=recv_sems.at[t, k],
                device_id=to, device_id_type=pl.DeviceIdType.MESH)

        started = []
        for t in range(n):
            src = x_refs[t] if items[t][1] is None else x_refs[t].at[items[t][1]]
            mine = pltpu.make_async_copy(src, out_refs[t].at[4 * x + 2 * y + c], local_sems.at[t])
            mine.start()
            first = [copy(t, 0, me, sibling, own=True)]
            first += [copy(t, 1 + j, me, (*chip, c), own=True) for j, chip in enumerate(chips)]
            for cp in first:
                cp.start()
            started.append((mine, first))
        passed = []
        for t in range(n):
            for j, chip in enumerate(chips):
                copy(t, 1 + j, (*chip, c), me).wait_recv()
                fwd = copy(t, 4 + j, (*chip, c), sibling)
                fwd.start()
                passed.append(fwd)
        for t in range(n):
            copy(t, 0, sibling, me).wait_recv()
            for j, chip in enumerate(chips):
                copy(t, 4 + j, (*chip, 1 - c), me).wait_recv()
        for mine, first in started:
            for cp in first:
                cp.wait_send()
            mine.wait()
        for cp in passed:
            cp.wait_send()

    any_spec = pl.BlockSpec(memory_space=pl.ANY)
    return pl.pallas_call(
        body, name=name,
        out_shape=[SDS((N_DEV,) + s, a.dtype) for s, (a, _) in zip(shapes, items)],
        in_specs=[any_spec] * n,
        out_specs=[any_spec] * n,
        scratch_shapes=[pltpu.SemaphoreType.DMA((n, 7)), pltpu.SemaphoreType.DMA((n, 7)), pltpu.SemaphoreType.DMA((n,))],
    )(*[a for a, _ in items])


def _pad8(t):
    return jnp.pad(t, ((0, SUBLANES - t.shape[0]), (0, 0)))


def _rows_merged(w):
    return w.reshape(w.shape[0] * w.shape[1], w.shape[2])


def _local_grads(x, tgt, norm_mix, norm_mlp, norm_kv, norm_final, conv_w, hub):
    bl, seq, D = x.shape
    T = bl * seq
    h = x.reshape(T, D)
    tgt = tgt.reshape(T, D)
    row = lambda t, l: t[l:l + 1]
    W = hub.weights
    saved = []
    kv = h_kv = hn_kv_t = None
    for l in range(DEPTH):
        if l < N_A_LAYERS:
            bcu, hn_t = _norm_mm(f"l{l}_in", h, row(norm_mix, l), W["w_a_in", l], out_dtype=BF16, transposed=True, hub=hub)
            h2, gated_t = _gate_mm_res(f"l{l}_conv_out", bcu, _pad8(conv_w[l]), _rows_merged(W["w_a_out", l]), h, seq, hub=hub)
            saved.append((h, bcu, gated_t, hn_t))
        else:
            i = l - N_A_LAYERS
            if l == N_A_LAYERS:
                h_kv = h
                kv, hn_kv_t = _norm_mm("kv", h, norm_kv.reshape(1, D), W["w_kv", None], transposed=True, hub=hub)
            q, hn_t = _norm_mm(f"l{l}_q", h, row(norm_mix, l), W["w_q", i], transposed=True)
            per_group = [_attn_fwd(f"l{l}_att{g}", q, kv, g, bl, hub=hub) for g in range(N_GROUPS)]
            o, lse, h2 = _combine_mm_res(f"l{l}_att_out", [p[0] for p in per_group], [p[1] for p in per_group],
                                         W["w_o", i], h)
            saved.append((h, q, o, lse, hn_t))
        a = _norm_mm(f"l{l}_up", h2, row(norm_mlp, l), W["w_up", l], out_dtype=BF16, hub=hub)
        h = _relu2_mm_res(f"l{l}_down", a, _rows_merged(W["w_down", l]), h2, hub=hub)
        saved[-1] = saved[-1] + (h2, a)

    dh, sq_err, d_norm_final = _final_loss("loss", h, tgt, norm_final.reshape(1, D))

    d_norm_mix = [None] * DEPTH
    d_norm_mlp = [None] * DEPTH
    d_conv = [None] * N_A_LAYERS
    d_norm_kv = None
    dkv_acc = [None] * N_GROUPS
    G = hub.grads
    as_slots = lambda g: g.reshape(N_DEV, g.shape[0] // N_DEV, g.shape[1])
    tt = DW_TOKENS
    for l in reversed(range(DEPTH)):
        h2, a = saved[l][-2:]
        h_in = saved[l][0]
        g_mlp = row(norm_mlp, l)
        g_mix = row(norm_mix, l)
        w_up_l = W["w_up", l]
        FF = N_DEV * w_up_l.shape[2]
        da = _nt_relu2_bwd(f"l{l}_down_bwd", dh, _rows_merged(W["w_down", l]), a, hub=hub)
        G["w_down", l] = as_slots(_tn(f"l{l}_dw_down", [_rows2(a, tt, FF // 2, lambda s: s)], _relu2,
                                      [_rows2(dh, tt)], _val, FF, D, T, tt, split=("k", 2)))
        G["w_up", l] = _tn(f"l{l}_dw_up", [_rows2(h2, 2 * tt), _full2(g_mlp)], _normed,
                           [_rows2(da, 2 * tt, FF // 4, lambda s: s)], _val, D, FF, T, 2 * tt, split=("n", 4),
                           out_cols=w_up_l.shape[2], hub=hub)
        dh2, d_norm_mlp[l] = _nt_norm_bwd(f"l{l}_up_bwd", [da], w_up_l, h2, g_mlp, dh, hub=hub)
        if l >= N_A_LAYERS:
            i = l - N_A_LAYERS
            _, q, o, lse, hn_t = saved[l][:5]
            w_o_i, w_q_i = W["w_o", i], W["w_q", i]
            do, delta = _att_out_bwd(f"l{l}_att_out_bwd", dh2, w_o_i, o)
            G["w_o", i] = _tn(f"l{l}_dw_o", [_rows2(o, tt)], _val, [_rows2(dh2, tt)], _val, QW, D, T, tt,
                              out_cols=w_o_i.shape[2])
            dqs = []
            for g in range(N_GROUPS):
                dqs.append(_attn_bwd_dq(f"l{l}_att{g}_dq", q, kv, do, delta, lse, g, bl, hub=hub))
                dkv_acc[g] = _attn_bwd_dkv(f"l{l}_att{g}_dkv", q, kv, do, delta, lse, g, bl, prev=dkv_acc[g], hub=hub)
            G["w_q", i] = _tn(f"l{l}_dw_q", [_cols2(hn_t, tt)], None,
                              [_rows2(t, tt) for t in dqs], _concat_f32, D, N_GROUPS * QW, T, tt, out_cols=w_q_i.shape[2])
            dh, d_norm_mix[l] = _nt_norm_bwd(f"l{l}_q_bwd", dqs, w_q_i, h_in, g_mix, dh2, hub=hub)
            if l == N_A_LAYERS:
                dkvs = [t for pair in dkv_acc for t in pair]
                g_kv = norm_kv.reshape(1, D)
                w_kv = W["w_kv", None]
                per_call = len(dkvs) // 2
                halves = [_tn(f"dw_kv{p}", [_cols2(hn_kv_t, tt)], None,
                              [_rows2(t, tt) for t in dkvs[p * per_call:(p + 1) * per_call]], _concat_f32,
                              D, per_call * QW, T, tt, out_cols=w_kv.shape[2]) for p in range(2)]
                G["w_kv", None] = jnp.concatenate(halves, axis=0)
                dh, d_norm_kv = _nt_norm_bwd("kv_bwd", dkvs, w_kv, h_kv, g_kv, dh, tm=MM_ROWS // 2, hub=hub)
        else:
            _, bcu, gated_t, hn_t = saved[l][:4]
            cw = _pad8(conv_w[l])
            w_in_l = W["w_a_in", l]
            dgated = _nt_plain(f"l{l}_conv_out_bwd", dh2, _rows_merged(W["w_a_out", l]))
            G["w_a_out", l] = as_slots(_tn(f"l{l}_dw_conv_out", [_cols2(gated_t, 2 * tt)], None,
                                           [_rows2(dh2, 2 * tt)], _val, D, D, T, 2 * tt, hub=hub))
            dbcu, d_conv[l] = _conv_bwd(f"l{l}_conv_bwd", bcu, dgated, cw, seq, hub=hub)
            G["w_a_in", l] = _tn(f"l{l}_dw_in", [_cols2(hn_t, 2 * tt)], None,
                                 [_rows2(dbcu, 2 * tt, 3 * D // 2, lambda s: s)], _val, D, 3 * D, T, 2 * tt, split=("n", 2),
                                 out_cols=w_in_l.shape[2], hub=hub)
            dh, d_norm_mix[l] = _nt_norm_bwd(f"l{l}_in_bwd", [dbcu], w_in_l, h_in, g_mix, dh2, hub=hub)

    small = jnp.concatenate(d_norm_mix + d_norm_mlp + [d_norm_kv, d_norm_final] + d_conv, axis=0)
    return sq_err, dh.reshape(bl, seq, D), small


def kernel(x, norm_mix, norm_mlp, w_a_in, conv_w, w_a_out, norm_kv, w_kv, w_q, w_o, w_up, w_down, norm_final, loss_target, m_norm_mix, m_norm_mlp, m_w_a_in, m_conv_w, m_w_a_out, m_norm_kv, m_w_kv, m_w_q, m_w_o, m_w_up, m_w_down, m_norm_final, v_norm_mix, v_norm_mlp, v_w_a_in, v_conv_w, v_w_a_out, v_norm_kv, v_w_kv, v_w_q, v_w_o, v_w_up, v_w_down, v_norm_final):
    D = x.shape[-1]
    xi, yi, ci = _mesh_pos()
    me_idx = 4 * xi + 2 * yi + ci
    w_big = dict(w_a_in=w_a_in, w_a_out=w_a_out, w_kv=w_kv, w_q=w_q, w_o=w_o, w_up=w_up, w_down=w_down)
    m_big = dict(w_a_in=m_w_a_in, w_a_out=m_w_a_out, w_kv=m_w_kv, w_q=m_w_q, w_o=m_w_o, w_up=m_w_up, w_down=m_w_down)
    v_big = dict(w_a_in=v_w_a_in, w_a_out=v_w_a_out, w_kv=v_w_kv, w_q=v_w_q, w_o=v_w_o, w_up=v_w_up, w_down=v_w_down)
    names = list(w_big)

    shards = {n: w.astype(BF16) for n, w in w_big.items()}
    landing = {n: lax.empty((N_DEV,) + w.shape, BF16) for n, w in w_big.items()}
    hub = _Hub(FETCH_DURING, PUSH_DURING, shards, landing)
    dc = conv_w.shape[-1]
    taps = conv_w.shape[0] * conv_w.shape[1]
    got = _all_gather("gather_first", [(shards[n], l) for n, l in FETCH_UP_FRONT] + [(_pad8(conv_w.reshape(taps, dc)), None)])
    for key, w in zip(FETCH_UP_FRONT, got):
        hub.weights[key] = w
    conv_full = jnp.moveaxis(got[-1][:, :taps], 0, 1).reshape(conv_w.shape[0], conv_w.shape[1], N_DEV * dc)

    sq_err, grad_x, small = _local_grads(x, loss_target, norm_mix, norm_mlp, norm_kv, norm_final, conv_full, hub)

    grads, deltas, new_m, new_v = {}, {}, {}, {}
    for n in names:
        shape = w_big[n].shape
        cols = shape[-1]
        flat = lambda t: t.reshape(-1, cols)
        parts = hub.landing[n].reshape(N_DEV, -1, cols)
        tr = parts.shape[1]
        while tr * cols > ADAMW_TILE and tr % 32 == 0:
            tr //= 2
        outs = _sum8_adamw(f"adamw_{n}", parts, flat(w_big[n]), flat(m_big[n]), flat(v_big[n]), tr=tr)
        grads[n], deltas[n], new_m[n], new_v[n] = (t.reshape(shape) for t in outs)

    n_gain = 2 * DEPTH + 2
    n_small = small.shape[0]
    small = jnp.concatenate([small, jnp.full((SUBLANES, D), sq_err, F32)], axis=0)
    rows_small = small.shape[0]
    small_all = _all_gather("gather_small_grads", [(small, None)])[0]

    def small_pack(nm, nl, nk, nf, cw):
        gains = jnp.concatenate([nm, nl, nk.reshape(1, D), nf.reshape(1, D)], axis=0)
        taps_full = lax.dynamic_update_slice(jnp.zeros((taps, D), F32), cw.reshape(taps, dc), (0, me_idx * dc))
        return jnp.concatenate([gains, taps_full, jnp.zeros((SUBLANES, D), F32)], axis=0)

    sp = [small_pack(*t) for t in ((norm_mix, norm_mlp, norm_kv, norm_final, conv_w),
                                   (m_norm_mix, m_norm_mlp, m_norm_kv, m_norm_final, m_conv_w),
                                   (v_norm_mix, v_norm_mlp, v_norm_kv, v_norm_final, v_conv_w))]
    small_out = _sum8_adamw("adamw_small", small_all, *sp, tr=rows_small)
    loss = small_out[0][n_small, 0] * (0.5 / D)

    def small_unpack(t):
        res = dict(norm_mix=t[0:DEPTH], norm_mlp=t[DEPTH:2 * DEPTH], norm_kv=t[2 * DEPTH], norm_final=t[2 * DEPTH + 1])
        res["conv_w"] = lax.dynamic_slice(t[n_gain:], (0, me_idx * dc), (taps, dc)).reshape(conv_w.shape)
        return res

    for dst, t in zip((grads, deltas, new_m, new_v), small_out):
        dst.update(small_unpack(t))

    order = ["norm_mix", "norm_mlp", "w_a_in", "conv_w", "w_a_out", "norm_kv", "w_kv", "w_q", "w_o", "w_up", "w_down",
             "norm_final"]
    return (loss, grad_x, *[grads[n] for n in order], *[deltas[n] for n in order], *[new_m[n] for n in order],
            *[new_v[n] for n in order])
```

```python
import jax
import jax.numpy as jnp
from jax import lax
from jax.experimental import pallas as pl
from jax.experimental.pallas import tpu as pltpu

F32 = jnp.float32
BF16 = jnp.bfloat16
SDS = jax.ShapeDtypeStruct

EPS = 1e-5
N_A_LAYERS = 2
DEPTH = 4
PATTERNS = ((128, 1), (512, 4), (2048, 16))
N_GROUPS = 3
H_G = 8
HEAD_DIM = 64
QW = H_G * HEAD_DIM
ATT_BLK = 128
ALIBI_MAX_BIAS = 8.0
NEG_INF = -1e30

ADAM_LR = 0.001
ADAM_B1 = 0.9
ADAM_B2 = 0.999
ADAM_EPS = 1e-08
ADAM_WD = 0.01
ADAM_STEP = 10

N_DEV = 8
SUBLANES = 8
HALO = 16
V7X_VMEM_LIMIT = 48 * 1024 * 1024
MXU_COLS = 256
MM_CHUNK = 512
MM_ROWS = 512
ADAMW_TILE = 256 * 1024
DW_TOKENS = 1024

FETCH_UP_FRONT = [("w_a_in", 0)]
FETCH_DURING = {
    "l0_in": [("w_a_out", 0), ("w_up", 0)], "l0_conv_out": [("w_down", 0)], "l0_up": [("w_a_in", 1), ("w_a_out", 1)], "l0_down": [("w_up", 1)],
    "l1_in": [("w_down", 1)], "l1_conv_out": [("w_kv", None)],
    "l1_up": [("w_q", 0), ("w_o", 0), ("w_q", 1), ("w_o", 1)], "l1_down": [("w_up", 2)],
    "kv": [("w_down", 2)], "l2_att0": [("w_up", 3)], "l2_att1": [("w_down", 3)],
}
PUSH_DURING = {
    "l3_dw_up": [("w_down", 3, 0, 2)], "l3_up_bwd": [("w_down", 3, 1, 2)],
    "l3_att0_dq": [("w_up", 3, 0, 2)], "l3_att0_dkv": [("w_up", 3, 1, 2)], "l3_att1_dq": [("w_o", 1)], "l3_q_bwd": [("w_q", 1)],
    "l2_dw_up": [("w_down", 2, 0, 2)], "l2_up_bwd": [("w_down", 2, 1, 2)],
    "l2_att0_dq": [("w_up", 2, 0, 2)], "l2_att0_dkv": [("w_up", 2, 1, 2)], "l2_att1_dq": [("w_o", 0)], "l2_q_bwd": [("w_q", 0)],
    "kv_bwd": [("w_kv", None, 0, 2)], "l1_down_bwd": [("w_kv", None, 1, 2)],
    "l1_dw_up": [("w_down", 1, 0, 2)], "l1_up_bwd": [("w_down", 1, 1, 2)], "l1_conv_bwd": [("w_up", 1, 0, 2)],
    "l1_dw_in": [("w_up", 1, 1, 2), ("w_a_out", 1)], "l1_in_bwd": [("w_a_in", 1, 0, 2)], "l0_down_bwd": [("w_a_in", 1, 1, 2)],
    "l0_dw_up": [("w_down", 0, 0, 2)], "l0_up_bwd": [("w_down", 0, 1, 2)], "l0_conv_bwd": [("w_up", 0, 0, 2)],
    "l0_dw_in": [("w_up", 0, 1, 2), ("w_a_out", 0)], "l0_in_bwd": [("w_a_in", 0)],
}


def _mesh_pos():
    return lax.axis_index("x"), lax.axis_index("y"), lax.axis_index("c")


def _flip(v, bit):
    return 1 - v if bit else v


class _Transfer:
    def __init__(self, kind, key, src, src_idx=None, dst=None, dst_idx=None, dst_shape=None, rows=None):
        self.kind, self.key, self.src, self.src_idx = kind, key, src, src_idx
        self.dst, self.dst_idx, self.dst_shape, self.rows = dst, dst_idx, dst_shape, rows

    def copies(self, src_ref, dst_ref, send_sems, recv_sems, local_sem):
        x, y, c = _mesh_pos()
        me = 4 * x + 2 * y + c
        part = (lambda r: r) if self.rows is None else (lambda r: r.at[pl.ds(*self.rows)])

        def dst_slot(j):
            r = dst_ref.at[j]
            return part(r if self.dst_idx is None else r.at[self.dst_idx])

        def copy(k, src, dst_j, to):
            return pltpu.make_async_remote_copy(
                src_ref=src, dst_ref=dst_slot(dst_j), send_sem=send_sems.at[k], recv_sem=recv_sems.at[k],
                device_id=to, device_id_type=pl.DeviceIdType.MESH)

        if self.kind == "exchange":
            local = pltpu.make_async_copy(part(src_ref.at[me]), dst_slot(me), local_sem)
            sends, arrivals = [], []
            for k in range(1, N_DEV):
                peer = (_flip(x, k & 4), _flip(y, k & 2), _flip(c, k & 1))
                peer_idx = 4 * peer[0] + 2 * peer[1] + peer[2]
                sends.append(copy(k - 1, part(src_ref.at[peer_idx]), me, peer))
                arrivals.append(copy(k - 1, part(src_ref.at[peer_idx]), peer_idx, peer))
            return local, sends, [], arrivals

        own = part(src_ref if self.src_idx is None else src_ref.at[self.src_idx])
        idx = lambda px, py, pc: 4 * px + 2 * py + pc
        sibling = (x, y, 1 - c)
        chips = [(1 - x, y), (x, 1 - y), (1 - x, 1 - y)]
        local = pltpu.make_async_copy(own, dst_slot(me), local_sem)
        sends = [copy(0, own, me, sibling)] + [copy(1 + j, own, me, (*chip, c)) for j, chip in enumerate(chips)]
        relays = [(copy(1 + j, own, idx(*chip, c), sibling), copy(4 + j, dst_slot(idx(*chip, c)), idx(*chip, c), sibling))
                  for j, chip in enumerate(chips)]
        arrivals = [copy(0, own, idx(*sibling), sibling)]
        arrivals += [copy(4 + j, own, idx(*chip, 1 - c), sibling) for j, chip in enumerate(chips)]
        return local, sends, relays, arrivals


class _Hub:
    def __init__(self, fetch, push, shards, landing):
        self.fetch, self.push, self.shards, self.landing = fetch, push, shards, landing
        self.weights = {}
        self.arriving = {}
        self.grads = {}

    def transfers(self, host):
        out = []
        for name, l, *part in self.fetch.get(host, ()):
            src = self.shards[name]
            shard = tuple(src.shape if l is None else src.shape[1:])
            p, n = part or (0, 1)
            rows = None if n == 1 else (p * (shard[0] // n), shard[0] // n)
            out.append(_Transfer("gather", (name, l, p == n - 1), src, src_idx=l, dst=self.arriving.get((name, l)),
                                 dst_shape=(N_DEV,) + shard, rows=rows))
        for name, l, *part in self.push.get(host, ()):
            src = self.grads[name, l]
            p, n = part or (0, 1)
            rows = None if n == 1 else (p * (src.shape[1] // n), src.shape[1] // n)
            out.append(_Transfer("exchange", (name, l, p == n - 1), src, dst=self.landing[name], dst_idx=l, rows=rows))
        return out

    def accept(self, transfers, results):
        for t, r in zip(transfers, results):
            name, l, complete = t.key
            if t.kind == "exchange":
                self.landing[name] = r
            elif complete:
                self.weights[name, l] = r
            else:
                self.arriving[name, l] = r


def _call(name, body, grid, ins, outs, scratch=(), hub=None):
    transfers = hub.transfers(name) if hub is not None else []
    n_in, n_out, n_scr, n_tr = len(ins), len(outs), len(scratch), len(transfers)
    c_in, c_out, aliases, places = [], [], {}, []
    for t in transfers:
        c_in.append(t.src)
        src_pos = len(c_in) - 1
        if t.dst is not None:
            c_in.append(t.dst)
            aliases[n_in + len(c_in) - 1] = n_out + len(c_out)
            c_out.append(SDS(t.dst.shape, t.dst.dtype))
        else:
            c_out.append(SDS(t.dst_shape, t.src.dtype))
        places.append((src_pos, len(c_out) - 1))
    sems = [pltpu.SemaphoreType.DMA((n_tr, N_DEV - 1)), pltpu.SemaphoreType.DMA((n_tr, N_DEV - 1)),
            pltpu.SemaphoreType.DMA((n_tr,))] if n_tr else []

    def wrapped(*refs):
        in_refs = refs[:n_in]
        cin_refs = refs[n_in:n_in + len(c_in)]
        o0 = n_in + len(c_in)
        out_refs = refs[o0:o0 + n_out]
        cout_refs = refs[o0 + n_out:o0 + n_out + len(c_out)]
        s0 = o0 + n_out + len(c_out)
        scr_refs = refs[s0:s0 + n_scr]
        if n_tr:
            send_sems, recv_sems, local_sems = refs[s0 + n_scr:]
            first = last = relay = None
            for ax, n in enumerate(grid):
                i = pl.program_id(ax)
                at_relay = (i == max(n - 2, 0)) if ax == len(grid) - 1 else (i == n - 1)
                first = (i == 0) if first is None else first & (i == 0)
                last = (i == n - 1) if last is None else last & (i == n - 1)
                relay = at_relay if relay is None else relay & at_relay

            def all_copies():
                return [t.copies(cin_refs[sp], cout_refs[dp], send_sems.at[n], recv_sems.at[n], local_sems.at[n])
                        for n, (t, (sp, dp)) in enumerate(zip(transfers, places))]

            @pl.when(first)
            def _():
                for local, sends, _, _ in all_copies():
                    local.start()
                    for cp in sends:
                        cp.start()

            def pass_on():
                @pl.when(relay)
                def _():
                    for _, _, relays, _ in all_copies():
                        for arrival, onward in relays:
                            arrival.wait_recv()
                            onward.start()

            if grid[-1] > 1:
                pass_on()

        body(*in_refs, *out_refs, *scr_refs)

        if n_tr:
            if grid[-1] == 1:
                pass_on()

            @pl.when(last)
            def _():
                for local, sends, relays, arrivals in all_copies():
                    for cp in arrivals:
                        cp.wait_recv()
                    for cp in sends + [onward for _, onward in relays]:
                        cp.wait_send()
                    local.wait()

    any_spec = pl.BlockSpec(memory_space=pl.ANY)
    res = pl.pallas_call(
        wrapped,
        name=name,
        grid=grid,
        in_specs=[s for _, s in ins] + [any_spec] * len(c_in),
        out_specs=[s for _, s in outs] + [any_spec] * len(c_out),
        out_shape=[o for o, _ in outs] + c_out,
        scratch_shapes=list(scratch) + sems,
        input_output_aliases=aliases,
        compiler_params=pltpu.CompilerParams(
            dimension_semantics=("arbitrary",) * len(grid), vmem_limit_bytes=V7X_VMEM_LIMIT),
    )(*[a for a, _ in ins], *c_in)
    if n_tr:
        hub.accept(transfers, res[n_out:])
    return res[:n_out]


def _rows(a, tm, cb=None, col=0):
    cb = cb or a.shape[1]
    return (a, pl.BlockSpec((tm, cb), lambda i: (i, col)))


def _full(a):
    nd = a.ndim
    return (a, pl.BlockSpec(a.shape, lambda i: (0,) * nd))


def _prev8(a, tm, cb, col):
    return (a, pl.BlockSpec((HALO, cb), lambda i: (jnp.maximum(i * (tm // HALO) - 1, 0), col)))


def _next8(a, tm, cb, col):
    last = a.shape[0] // HALO - 1
    return (a, pl.BlockSpec((HALO, cb), lambda i: (jnp.minimum((i + 1) * (tm // HALO), last), col)))


def _rows2(a, tt, cb=None, colfn=None):
    cb = cb or a.shape[1]
    colfn = colfn or (lambda s: 0)
    return (a, pl.BlockSpec((tt, cb), lambda s, t: (t, colfn(s))))


def _full2(a):
    nd = a.ndim
    return (a, pl.BlockSpec(a.shape, lambda s, t: (0,) * nd))


def _out_rows(T, n, dtype, tm):
    return (SDS((T, n), dtype), pl.BlockSpec((tm, n), lambda i: (i, 0)))


def _out_acc8(d):
    return (SDS((SUBLANES, d), F32), pl.BlockSpec((SUBLANES, d), lambda i: (0, 0)))


def _rstd(x):
    return lax.rsqrt(jnp.mean(x * x, axis=-1, keepdims=True) + EPS)


def _normed(h_ref, g_ref):
    x = h_ref[...]
    return x * _rstd(x) * g_ref[...]


def _acc8(ref, val, i, n):
    part = val.reshape(-1, SUBLANES, val.shape[-1]).sum(axis=0)

    @pl.when(i == 0)
    def _():
        ref[...] = part

    @pl.when(i > 0)
    def _():
        ref[...] += part

    @pl.when(i == n - 1)
    def _():
        ref[...] = jnp.broadcast_to(jnp.sum(ref[...], axis=0, keepdims=True), ref.shape)


def _gate(b_ref, c_ref, u_ref, ch_ref, uh_ref, cw_ref, first):
    b, c, u = (r[...].astype(F32) for r in (b_ref, c_ref, u_ref))
    cu = c * u
    halo = jnp.where(first, 0.0, ch_ref[...].astype(F32) * uh_ref[...].astype(F32))
    rows = lax.broadcasted_iota(jnp.int32, cu.shape, 0)
    h1 = halo[HALO - 1:HALO, :]
    h2 = halo[HALO - 2:HALO - 1, :]
    cu1 = jnp.where(rows == 0, h1, pltpu.roll(cu, 1, 0))
    cu2 = jnp.where(rows == 0, h2, jnp.where(rows == 1, h1, pltpu.roll(cu, 2, 0)))
    conv = cw_ref[0:1, :] * cu + cw_ref[1:2, :] * cu1 + cw_ref[2:3, :] * cu2
    return b * conv, (b, c, u), conv, (cu, cu1, cu2)


def _relu2(a_ref):
    r = jnp.maximum(a_ref[...].astype(F32), 0.0)
    return r * r


def _dot(a, b):
    return jnp.dot(a, b, preferred_element_type=F32)


def _dot_nt(a, b):
    return lax.dot_general(a, b, (((1,), (1,)), ((), ())), preferred_element_type=F32)


def _chunks(n):
    c = min(MM_CHUNK, n)
    while n % c:
        c -= 128
    assert c > 0, n
    return [(k * c, (k + 1) * c) for k in range(n // c)]


def _col_weight(w):
    _, K, ns = w.shape
    N = N_DEV * ns
    direct = ns % MXU_COLS == 0
    scratch = [] if direct else [pltpu.VMEM((K, N), BF16)]

    def prepare(w_ref, s_ref, step):
        if direct:
            return

        @pl.when(step == 0)
        def _():
            for j in range(N_DEV):
                s_ref[:, j * ns:(j + 1) * ns] = w_ref[j]

    def chunks(w_ref, s_ref):
        if direct:
            return [(j * ns, (j + 1) * ns, (lambda j=j: w_ref[j])) for j in range(N_DEV)]
        return [(lo, hi, (lambda lo=lo, hi=hi: s_ref[:, lo:hi])) for lo, hi in _chunks(N)]

    return N, scratch, prepare, chunks


def _out_cols(n, T, tm):
    return (SDS((n, T), BF16), pl.BlockSpec((n, tm), lambda i: (0, i)))


def _norm_mm(name, h, g, w, tm=MM_ROWS, out_dtype=F32, transposed=False, hub=None):
    T, D = h.shape
    N, w_scratch, prepare, chunks = _col_weight(w)

    def body(h_ref, g_ref, w_ref, o_ref, *rest):
        at_ref, s = (rest[0], rest[1:]) if transposed else (None, rest)
        s_ref = s[0] if s else None
        prepare(w_ref, s_ref, pl.program_id(0))
        a32 = _normed(h_ref, g_ref)
        a = a32.astype(BF16)
        for lo, hi, load in chunks(w_ref, s_ref):
            o_ref[:, lo:hi] = _dot(a, load()).astype(out_dtype)
        if transposed:
            at_ref[...] = a32.T.astype(BF16)

    outs = [_out_rows(T, N, out_dtype, tm)] + ([_out_cols(D, T, tm)] if transposed else [])
    res = _call(name, body, (T // tm,), [_rows(h, tm), _full(g), _full(w)], outs, scratch=w_scratch, hub=hub)
    return res if transposed else res[0]


def _gate_mm_res(name, bcu, cw, w, h, seq, tm=MM_ROWS, hub=None):
    T, D = h.shape

    def body(b_ref, c_ref, u_ref, ch_ref, uh_ref, cw_ref, w_ref, h_ref, o_ref, gt_ref):
        first = (pl.program_id(0) * tm) % seq == 0
        gated32 = _gate(b_ref, c_ref, u_ref, ch_ref, uh_ref, cw_ref, first)[0]
        gated = gated32.astype(BF16)
        for lo, hi in _chunks(D):
            o_ref[:, lo:hi] = h_ref[:, lo:hi] + _dot(gated, w_ref[:, lo:hi])
        gt_ref[...] = gated32.T.astype(BF16)

    ins = [_rows(bcu, tm, D, 0), _rows(bcu, tm, D, 1), _rows(bcu, tm, D, 2), _prev8(bcu, tm, D, 1),
           _prev8(bcu, tm, D, 2), _full(cw), _full(w), _rows(h, tm)]
    return _call(name, body, (T // tm,), ins, [_out_rows(T, D, F32, tm), _out_cols(D, T, tm)], hub=hub)


def _relu2_mm_res(name, a, w, h, tm=MM_ROWS, hub=None):
    T, D = h.shape
    K = a.shape[1]

    def body(a_ref, w_ref, h_ref, o_ref, acc_ref):
        for n, (lo, hi) in enumerate(_chunks(K)):
            d = _dot(_relu2(a_ref.at[:, lo:hi]).astype(BF16), w_ref[lo:hi, :])
            if n == 0:
                acc_ref[...] = d
            else:
                acc_ref[...] += d
        o_ref[...] = h_ref[...] + acc_ref[...]

    return _call(name, body, (T // tm,), [_rows(a, tm), _full(w), _rows(h, tm)], [_out_rows(T, D, F32, tm)],
                 scratch=[pltpu.VMEM((tm, D), F32)], hub=hub)[0]


def _combine_mm_res(name, os_, lses, w, h, tm=MM_ROWS):
    T, D = h.shape
    _, w_scratch, prepare, chunks = _col_weight(w)

    def body(o0, o1, o2, l0, l1, l2, w_ref, h_ref, o_ref, lse_ref, out_ref, *s):
        s_ref = s[0] if s else None
        prepare(w_ref, s_ref, pl.program_id(0))
        ls = [l0[...], l1[...], l2[...]]
        mx = jnp.maximum(jnp.maximum(ls[0], ls[1]), ls[2])
        es = [jnp.exp(l - mx) for l in ls]
        den = es[0] + es[1] + es[2]
        o = (es[0] * o0[...] + es[1] * o1[...] + es[2] * o2[...]) / den
        o_ref[...] = o
        lse_ref[...] = mx + jnp.log(den)
        ob = o.astype(BF16)
        for lo, hi, load in chunks(w_ref, s_ref):
            out_ref[:, lo:hi] = h_ref[:, lo:hi] + _dot(ob, load())

    ins = [_rows(t, tm) for t in list(os_) + list(lses)] + [_full(w), _rows(h, tm)]
    outs = [_out_rows(T, QW, F32, tm), _out_rows(T, QW, F32, tm), _out_rows(T, D, F32, tm)]
    return _call(name, body, (T // tm,), ins, outs, scratch=w_scratch)


def _nt_relu2_bwd(name, dh, w, a, tm=MM_ROWS, hub=None):
    T, _ = dh.shape
    K = w.shape[0]

    def body(dh_ref, w_ref, a_ref, o_ref):
        d = dh_ref[...].astype(BF16)
        for lo, hi in _chunks(K):
            dr = _dot_nt(d, w_ref[lo:hi, :])
            o_ref[:, lo:hi] = (dr * (2.0 * jnp.maximum(a_ref[:, lo:hi].astype(F32), 0.0))).astype(BF16)

    return _call(name, body, (T // tm,), [_rows(dh, tm), _full(w), _rows(a, tm)], [_out_rows(T, K, BF16, tm)], hub=hub)[0]


def _concat_bf16(*refs):
    vals = [r[...].astype(BF16) for r in refs]
    return vals[0] if len(vals) == 1 else jnp.concatenate(vals, axis=1)


def _nt_plain(name, dy, w, tm=MM_ROWS):
    T, N = dy.shape
    K = w.shape[0]

    def body(dy_ref, w_ref, o_ref, acc_ref):
        for n, (lo, hi) in enumerate(_chunks(N)):
            d = _dot_nt(dy_ref[:, lo:hi].astype(BF16), w_ref[:, lo:hi])
            if n == 0:
                acc_ref[...] = d
            else:
                acc_ref[...] += d
        o_ref[...] = acc_ref[...]

    return _call(name, body, (T // tm,), [_rows(dy, tm), _full(w)], [_out_rows(T, K, F32, tm)],
                 scratch=[pltpu.VMEM((tm, K), F32)])[0]


def _att_out_bwd(name, dy, w, o, tm=MM_ROWS):
    T, _ = dy.shape
    K = w.shape[1]
    _, w_scratch, prepare, chunks = _col_weight(w)

    def body(dy_ref, w_ref, o_ref, do_ref, dl_ref, acc_ref, *s):
        s_ref = s[0] if s else None
        prepare(w_ref, s_ref, pl.program_id(0))
        for n, (lo, hi, load) in enumerate(chunks(w_ref, s_ref)):
            d = _dot_nt(dy_ref[:, lo:hi].astype(BF16), load())
            if n == 0:
                acc_ref[...] = d
            else:
                acc_ref[...] += d
        do = acc_ref[...]
        do_ref[...] = do
        prod = do * o_ref[...]
        high = prod.astype(BF16)
        low = (prod - high.astype(F32)).astype(BF16)
        head_of = lambda axis: jnp.right_shift(lax.broadcasted_iota(jnp.int32, (K, K), axis), HEAD_DIM.bit_length() - 1)
        same_head = jnp.where(head_of(0) == head_of(1), 1.0, 0.0).astype(BF16)
        dl_ref[...] = _dot(high, same_head) + _dot(low, same_head)

    outs = [_out_rows(T, K, F32, tm), _out_rows(T, K, F32, tm)]
    return _call(name, body, (T // tm,), [_rows(dy, tm), _full(w), _rows(o, tm)], outs,
                 scratch=[pltpu.VMEM((tm, K), F32)] + w_scratch)


def _nt_norm_bwd(name, dys, w, h, g, dh_in, tm=MM_ROWS, hub=None):
    T, D = h.shape
    _, w_scratch, prepare, chunks = _col_weight(w)
    n_steps = T // tm
    n_dy = len(dys)

    def body(*refs):
        dy_refs = refs[:n_dy]
        w_ref, h_ref, g_ref, dhin_ref, o_ref, dg_ref, acc_ref = refs[n_dy:n_dy + 7]
        s_ref = refs[n_dy + 7] if len(refs) > n_dy + 7 else None
        i = pl.program_id(0)
        prepare(w_ref, s_ref, i)
        dy = _concat_bf16(*dy_refs)
        for n, (lo, hi, load) in enumerate(chunks(w_ref, s_ref)):
            d = _dot_nt(dy[:, lo:hi], load())
            if n == 0:
                acc_ref[...] = d
            else:
                acc_ref[...] += d
        dn = acc_ref[...]
        x = h_ref[...]
        rstd = _rstd(x)
        xhat = x * rstd
        dxhat = dn * g_ref[...]
        dx = rstd * (dxhat - xhat * jnp.mean(dxhat * xhat, axis=-1, keepdims=True))
        o_ref[...] = dhin_ref[...] + dx
        _acc8(dg_ref, dn * xhat, i, n_steps)

    ins = [_rows(d, tm) for d in dys] + [_full(w), _rows(h, tm), _full(g), _rows(dh_in, tm)]
    outs = [_out_rows(T, D, F32, tm), _out_acc8(D)]
    dh, dg = _call(name, body, (n_steps,), ins, outs, scratch=[pltpu.VMEM((tm, D), F32)] + w_scratch, hub=hub)
    return dh, dg[0:1]


def _cols2(a_t, tt, kb=None):
    kb = kb or a_t.shape[0]
    return (a_t, pl.BlockSpec((kb, tt), (lambda s, t: (s, t)) if kb != a_t.shape[0] else (lambda s, t: (0, t))))


def _tn(name, a_ins, a_fn, y_ins, y_fn, K, N, T, tt, split=None, out_cols=None, hub=None):
    kind, parts = split or ("n", 1)
    kb, nb = (K // parts, N) if kind == "k" else (K, N // parts)
    n_steps = T // tt
    n_a = len(a_ins)
    n_y = len(y_ins)
    assert out_cols is None or (kind == "n" and nb % out_cols == 0)

    def body(*refs):
        a_refs = refs[:n_a]
        y_refs = refs[n_a:n_a + n_y]
        o_ref, acc_ref = refs[n_a + n_y:]
        t = pl.program_id(1)
        a_t = a_refs[0][...] if a_fn is None else a_fn(*a_refs).T.astype(BF16)
        y = y_fn(*y_refs).astype(BF16)
        for lo, hi in _chunks(nb):
            d = _dot(a_t, y[:, lo:hi])

            @pl.when(t == 0)
            def _():
                acc_ref[:, lo:hi] = d

            @pl.when(t > 0)
            def _():
                acc_ref[:, lo:hi] += d

        @pl.when(t == n_steps - 1)
        def _():
            if out_cols is None:
                o_ref[...] = acc_ref[...].astype(BF16)
            else:
                for j in range(nb // out_cols):
                    o_ref[j] = acc_ref[:, j * out_cols:(j + 1) * out_cols].astype(BF16)

    if out_cols is None:
        out = (SDS((K, N), BF16), pl.BlockSpec((kb, nb), (lambda s, t: (s, 0)) if kind == "k" else (lambda s, t: (0, s))))
    else:
        out = (SDS((N // out_cols, K, out_cols), BF16), pl.BlockSpec((nb // out_cols, K, out_cols), lambda s, t: (s, 0, 0)))
    return _call(name, body, (parts, n_steps), list(a_ins) + list(y_ins), [out],
                 scratch=[pltpu.VMEM((kb, nb), F32)], hub=hub)[0]


def _val(ref):
    return ref[...]


def _concat_f32(*refs):
    vals = [r[...] for r in refs]
    return vals[0] if len(vals) == 1 else jnp.concatenate(vals, axis=1)


ATT_TILE_ROWS = 2048
HEAD_PAIRS = H_G // 2
ATT_SCALE = HEAD_DIM ** -0.5
ATT_UNITS_TOGETHER = 4


def _slope(h):
    return 2.0 ** (-ALIBI_MAX_BIAS * (h + 1) / H_G)


def _att_geom(T, bl, g):
    dil = PATTERNS[g][1]
    sub = ATT_BLK * dil
    nsub = max(1, ATT_TILE_ROWS // sub)
    rows = sub * nsub
    return dil, sub, nsub, rows, T // bl // rows


def _att_specs(T, bl, g):
    _, sub, nsub, rows, nt = _att_geom(T, bl, g)
    last_sub = T // sub - 1
    tile = lambda col: pl.BlockSpec((rows, 128), lambda b, i, hp: (b * nt + i, col(hp)))
    prev = lambda col: pl.BlockSpec((sub, 128), lambda b, i, hp: (jnp.maximum((b * nt + i) * nsub - 1, 0), col(hp)))
    nxt = lambda col: pl.BlockSpec((sub, 128), lambda b, i, hp: (jnp.minimum((b * nt + i + 1) * nsub, last_sub), col(hp)))
    return tile, prev, nxt


def _sub_rows(j, r, dil):
    start = j * ATT_BLK * dil + r
    return pl.ds(start, ATT_BLK, stride=dil) if dil > 1 else pl.ds(start, ATT_BLK)


class _Residues:
    def __init__(self, dil):
        self.dil = dil
        self.whole = dil % SUBLANES == 0
        self.read, self.written = {}, {}

    def _block(self, j):
        return pl.ds(j * ATT_BLK * self.dil, ATT_BLK * self.dil)

    def load(self, ref, j, r):
        if not self.whole:
            return ref[_sub_rows(j, r, self.dil), :]
        if (id(ref), j) not in self.read:
            rows = ref[self._block(j), :]
            self.read[id(ref), j] = jnp.swapaxes(rows.reshape(ATT_BLK, self.dil, rows.shape[-1]), 0, 1)
        return self.read[id(ref), j][r]

    def store(self, ref, j, r, val):
        if not self.whole:
            ref[_sub_rows(j, r, self.dil), :] = val
            return
        got = self.written.setdefault((id(ref), j), {})
        got[r] = val
        if len(got) == self.dil:
            merged = jnp.swapaxes(jnp.stack([got[k] for k in range(self.dil)], axis=0), 0, 1)
            ref[self._block(j), :] = merged.reshape(ATT_BLK * self.dil, val.shape[-1])
            del self.written[id(ref), j]


def _att_consts(hp, dil, keys_first=False):
    h0 = lax.broadcasted_iota(jnp.int32, (ATT_BLK, 128), 1) < HEAD_DIM
    a = lax.broadcasted_iota(jnp.int32, (ATT_BLK, ATT_BLK), 1 if keys_first else 0)
    c = lax.broadcasted_iota(jnp.int32, (ATT_BLK, ATT_BLK), 0 if keys_first else 1)
    dist_p = ((ATT_BLK + a - c) * dil).astype(F32)
    dist_c = ((a - c) * dil).astype(F32)
    bias_p, bias_c = [], []
    for h in range(2):
        slope = jnp.float32(_slope(2 * (HEAD_PAIRS - 1) + h))
        for p in range(HEAD_PAIRS - 2, -1, -1):
            slope = jnp.where(hp == p, jnp.float32(_slope(2 * p + h)), slope)
        bias_p.append(jnp.where(c >= a, -slope * dist_p, NEG_INF))
        bias_c.append(jnp.where(c <= a, -slope * dist_c, NEG_INF))
    return h0, jnp.concatenate(bias_p, axis=0), jnp.concatenate(bias_c, axis=0)


def _stack_heads(x, h0):
    return jnp.concatenate([jnp.where(h0, x, 0.0), jnp.where(h0, 0.0, x)], axis=0).astype(BF16)


def _unstack_heads(x, h0):
    return jnp.where(h0, x[:ATT_BLK], x[ATT_BLK:])


def _stack_cols(x):
    return jnp.concatenate(_head_cols(x), axis=0)


def _head_cols(x):
    return [x[:, 0:1], x[:, HEAD_DIM:HEAD_DIM + 1]]


def _in_groups(units, first_stage, *later_stages):
    for u0 in range(0, len(units), ATT_UNITS_TOGETHER):
        staged = [first_stage(*u) for u in units[u0:u0 + ATT_UNITS_TOGETHER]]
        for stage in later_stages:
            staged = [stage(*s) for s in staged]


def _attn_fwd(name, q, kv, g, bl, hub=None):
    T = q.shape[0]
    dil, _, nsub, _, _ = _att_geom(T, bl, g)
    tile, prev, _ = _att_specs(T, bl, g)

    def body(q_ref, kp_ref, kc_ref, vp_ref, vc_ref, o_ref, lse_ref):
        first = pl.program_id(1) == 0
        h0, bias_p, bias_c = _att_consts(pl.program_id(2), dil)
        bias_first = jnp.where(first, NEG_INF, bias_p)
        rows = _Residues(dil)

        def with_ones(v):
            return [jnp.where(h0, v, 1.0).astype(BF16), jnp.where(h0, 1.0, v).astype(BF16)]

        def scores(j, r):
            if j == 0:
                kp, vp, bp = rows.load(kp_ref, 0, r), rows.load(vp_ref, 0, r), bias_first
            else:
                kp, vp, bp = rows.load(kc_ref, j - 1, r), rows.load(vc_ref, j - 1, r), bias_p
            kp, kc = kp.astype(BF16), rows.load(kc_ref, j, r).astype(BF16)
            qs = _stack_heads(rows.load(q_ref, j, r) * ATT_SCALE, h0)
            sp = _dot_nt(qs, kp) + bp
            sc = _dot_nt(qs, kc) + bias_c
            return (j, r), sp, sc, with_ones(vp), with_ones(rows.load(vc_ref, j, r))

        def weights(unit, sp, sc, vp, vc):
            mx = jnp.max(jnp.maximum(sp, sc), axis=-1, keepdims=True)
            return unit, mx, jnp.exp(sp - mx).astype(BF16), jnp.exp(sc - mx).astype(BF16), vp, vc

        def outputs(unit, mx, ep, ec, vp, vc):
            heads = [slice(h * ATT_BLK, (h + 1) * ATT_BLK) for h in range(2)]
            acc = [_dot(ep[hs], vp[h]) + _dot(ec[hs], vc[h]) for h, hs in enumerate(heads)]
            den = [pltpu.roll(a, HEAD_DIM, 1) for a in acc]
            rows.store(o_ref, *unit, jnp.where(h0, acc[0] / den[0], acc[1] / den[1]))
            rows.store(lse_ref, *unit, jnp.where(h0, mx[heads[0]] + jnp.log(den[0]), mx[heads[1]] + jnp.log(den[1])))
            return ()

        _in_groups([(j, r) for j in range(nsub) for r in range(dil)], scores, weights, outputs)

    ins = [(q, tile(lambda hp: 4 * g + hp)), (kv, prev(lambda hp: 8 * g + hp)), (kv, tile(lambda hp: 8 * g + hp)),
           (kv, prev(lambda hp: 8 * g + 4 + hp)), (kv, tile(lambda hp: 8 * g + 4 + hp))]
    out = (SDS((T, QW), F32), tile(lambda hp: hp))
    _, _, _, _, nt = _att_geom(T, bl, g)
    return _call(name, body, (bl, nt, HEAD_PAIRS), ins, [out, out], hub=hub)


def _attn_bwd_dq(name, q, kv, do, delta, lse, g, bl, hub=None):
    T = q.shape[0]
    dil, _, nsub, _, nt = _att_geom(T, bl, g)
    tile, prev, _ = _att_specs(T, bl, g)

    def body(q_ref, kp_ref, kc_ref, vp_ref, vc_ref, do_ref, dl_ref, lse_ref, dq_ref):
        first = pl.program_id(1) == 0
        h0, bias_p, bias_c = _att_consts(pl.program_id(2), dil)
        bias_first = jnp.where(first, NEG_INF, bias_p)
        rows = _Residues(dil)

        def probs(j, r):
            if j == 0:
                kp, vp, bp = rows.load(kp_ref, 0, r), rows.load(vp_ref, 0, r), bias_first
            else:
                kp, vp, bp = rows.load(kc_ref, j - 1, r), rows.load(vc_ref, j - 1, r), bias_p
            kp, vp = kp.astype(BF16), vp.astype(BF16)
            kc, vc = rows.load(kc_ref, j, r).astype(BF16), rows.load(vc_ref, j, r).astype(BF16)
            qs = _stack_heads(rows.load(q_ref, j, r) * ATT_SCALE, h0)
            dos = _stack_heads(rows.load(do_ref, j, r), h0)
            lse = _stack_cols(rows.load(lse_ref, j, r))
            pp = jnp.exp(_dot_nt(qs, kp) + bp - lse)
            pc = jnp.exp(_dot_nt(qs, kc) + bias_c - lse)
            return (j, r), pp, pc, _dot_nt(dos, vp), _dot_nt(dos, vc), kp, kc

        def dscores(unit, pp, pc, dpp, dpc, kp, kc):
            dl = _stack_cols(rows.load(dl_ref, *unit))
            return unit, (pp * (dpp - dl)).astype(BF16), (pc * (dpc - dl)).astype(BF16), kp, kc

        def outputs(unit, dsp, dsc, kp, kc):
            rows.store(dq_ref, *unit, _unstack_heads(_dot(dsp, kp) + _dot(dsc, kc), h0) * ATT_SCALE)
            return ()

        _in_groups([(j, r) for j in range(nsub) for r in range(dil)], probs, dscores, outputs)

    own = lambda hp: hp
    ins = [(q, tile(lambda hp: 4 * g + hp)), (kv, prev(lambda hp: 8 * g + hp)), (kv, tile(lambda hp: 8 * g + hp)),
           (kv, prev(lambda hp: 8 * g + 4 + hp)), (kv, tile(lambda hp: 8 * g + 4 + hp)),
           (do, tile(own)), (delta, tile(own)), (lse, tile(own))]
    return _call(name, body, (bl, nt, HEAD_PAIRS), ins, [(SDS((T, QW), F32), tile(own))], hub=hub)[0]


def _attn_bwd_dkv(name, q, kv, do, delta, lse, g, bl, prev=None, hub=None):
    T = q.shape[0]
    dil, _, nsub, _, nt = _att_geom(T, bl, g)
    tile, _, nxt = _att_specs(T, bl, g)
    has_prev = prev is not None

    def body(*refs):
        k_ref, v_ref, q_ref, qn_ref, do_ref, don_ref, dl_ref, dln_ref, l_ref, ln_ref = refs[:10]
        rest = refs[10:]
        if has_prev:
            dkp_ref, dvp_ref, dk_ref, dv_ref = rest
        else:
            dk_ref, dv_ref = rest
        last = pl.program_id(1) == nt - 1
        h0, bias_p, bias_c = _att_consts(pl.program_id(2), dil, keys_first=True)
        bias_last = jnp.where(last, NEG_INF, bias_p)
        rows = _Residues(dil)

        def per_query_rows(x):
            xt = x.T
            return jnp.concatenate([jnp.broadcast_to(xt[0:1], (ATT_BLK, ATT_BLK)),
                                    jnp.broadcast_to(xt[HEAD_DIM:HEAD_DIM + 1], (ATT_BLK, ATT_BLK))], axis=0)

        def probs(j, r):
            ks = _stack_heads(rows.load(k_ref, j, r), h0)
            vs = _stack_heads(rows.load(v_ref, j, r), h0)
            sets = [(q_ref, do_ref, dl_ref, l_ref, j, bias_c)]
            if j < nsub - 1:
                sets.append((q_ref, do_ref, dl_ref, l_ref, j + 1, bias_p))
            else:
                sets.append((qn_ref, don_ref, dln_ref, ln_ref, 0, bias_last))
            out = []
            for qr, dor, dlr, lr, jq, bias in sets:
                qsb = (rows.load(qr, jq, r) * ATT_SCALE).astype(BF16)
                do2b = rows.load(dor, jq, r).astype(BF16)
                p = jnp.exp(_dot_nt(ks, qsb) + bias - per_query_rows(rows.load(lr, jq, r)))
                out.append((p, _dot_nt(vs, do2b), dlr, jq, qsb, do2b))
            return (j, r), out

        def dscores(unit, sets):
            out = []
            for p, dp, dlr, jq, qsb, do2b in sets:
                ds = (p * (dp - per_query_rows(rows.load(dlr, jq, unit[1])))).astype(BF16)
                out.append((p.astype(BF16), ds, qsb, do2b))
            return unit, out

        def outputs(unit, sets):
            dk_st = dv_st = None
            for pb, ds, qsb, do2b in sets:
                dvs, dks = _dot(pb, do2b), _dot(ds, qsb)
                dv_st = dvs if dv_st is None else dv_st + dvs
                dk_st = dks if dk_st is None else dk_st + dks
            dk2 = _unstack_heads(dk_st, h0)
            dv2 = _unstack_heads(dv_st, h0)
            if has_prev:
                dk2 = dk2 + rows.load(dkp_ref, *unit)
                dv2 = dv2 + rows.load(dvp_ref, *unit)
            rows.store(dk_ref, *unit, dk2)
            rows.store(dv_ref, *unit, dv2)
            return ()

        _in_groups([(j, r) for j in range(nsub) for r in range(dil)], probs, dscores, outputs)

    own = lambda hp: hp
    qcol = lambda hp: 4 * g + hp
    ins = [(kv, tile(lambda hp: 8 * g + hp)), (kv, tile(lambda hp: 8 * g + 4 + hp)), (q, tile(qcol)), (q, nxt(qcol)),
           (do, tile(own)), (do, nxt(own)), (delta, tile(own)), (delta, nxt(own)), (lse, tile(own)), (lse, nxt(own))]
    if has_prev:
        ins += [(prev[0], tile(own)), (prev[1], tile(own))]
    out = (SDS((T, QW), F32), tile(own))
    return _call(name, body, (bl, nt, HEAD_PAIRS), ins, [out, out], hub=hub)


def _final_loss(name, h, tgt, g, tm=MM_ROWS):
    T, D = h.shape
    n_steps = T // tm

    def body(h_ref, t_ref, g_ref, dh_ref, loss_ref, dg_ref, sq_ref):
        i = pl.program_id(0)
        x = h_ref[...]
        rstd = _rstd(x)
        xhat = x * rstd
        err = xhat * g_ref[...] - t_ref[...]
        _acc8(sq_ref, err * err, i, n_steps)
        dy = err * (1.0 / D)
        dxhat = dy * g_ref[...]
        dh_ref[...] = rstd * (dxhat - xhat * jnp.mean(dxhat * xhat, axis=-1, keepdims=True))
        _acc8(dg_ref, dy * xhat, i, n_steps)

        @pl.when(i == n_steps - 1)
        def _():
            loss_ref[...] = jnp.full(loss_ref.shape, jnp.sum(sq_ref[0:1, :]), F32)

    outs = [_out_rows(T, D, F32, tm), (SDS((SUBLANES, 128), F32), pl.BlockSpec((SUBLANES, 128), lambda i: (0, 0))),
            _out_acc8(D)]
    dh, loss, dg = _call(name, body, (n_steps,), [_rows(h, tm), _rows(tgt, tm), _full(g)], outs,
                         scratch=[pltpu.VMEM((SUBLANES, D), F32)])
    return dh, loss[0, 0], dg[0:1]


def _conv_bwd(name, bcu, dgated, cw, seq, tm=MM_ROWS, hub=None):
    T, D = dgated.shape
    n_steps = T // tm

    def body(b_ref, c_ref, u_ref, ch_ref, uh_ref, dg_ref, dgn_ref, bn_ref, cw_ref, o_ref, t0_ref, t1_ref, t2_ref):
        i = pl.program_id(0)
        first = (i * tm) % seq == 0
        last = ((i + 1) * tm) % seq == 0
        _, (b, c, u), conv, (cu, cu1, cu2) = _gate(b_ref, c_ref, u_ref, ch_ref, uh_ref, cw_ref, first)
        dgat = dg_ref[...]
        dconv = dgat * b
        nxt = jnp.where(last, 0.0, dgn_ref[...] * bn_ref[...].astype(F32))
        rows = lax.broadcasted_iota(jnp.int32, dconv.shape, 0)
        n1 = nxt[0:1, :]
        n2 = nxt[1:2, :]
        dc1 = jnp.where(rows == tm - 1, n1, pltpu.roll(dconv, tm - 1, 0))
        dc2 = jnp.where(rows == tm - 1, n2, jnp.where(rows == tm - 2, n1, pltpu.roll(dconv, tm - 2, 0)))
        dcu = cw_ref[0:1, :] * dconv + cw_ref[1:2, :] * dc1 + cw_ref[2:3, :] * dc2
        o_ref[:, 0:D] = (dgat * conv).astype(BF16)
        o_ref[:, D:2 * D] = (dcu * u).astype(BF16)
        o_ref[:, 2 * D:3 * D] = (dcu * c).astype(BF16)
        _acc8(t0_ref, dconv * cu, i, n_steps)
        _acc8(t1_ref, dconv * cu1, i, n_steps)
        _acc8(t2_ref, dconv * cu2, i, n_steps)

    ins = [_rows(bcu, tm, D, 0), _rows(bcu, tm, D, 1), _rows(bcu, tm, D, 2), _prev8(bcu, tm, D, 1), _prev8(bcu, tm, D, 2),
           _rows(dgated, tm), _next8(dgated, tm, D, 0), _next8(bcu, tm, D, 0), _full(cw)]
    outs = [_out_rows(T, 3 * D, BF16, tm), _out_acc8(D), _out_acc8(D), _out_acc8(D)]
    dbcu, t0, t1, t2 = _call(name, body, (n_steps,), ins, outs, hub=hub)
    return dbcu, jnp.concatenate([t0[0:1], t1[0:1], t2[0:1]], axis=0)


def _sum8_adamw(name, parts, w, m, v, tr):
    R, C = w.shape
    b1c = 1.0 - ADAM_B1 ** ADAM_STEP
    b2c = 1.0 - ADAM_B2 ** ADAM_STEP

    def body(p_ref, w_ref, m_ref, v_ref, g_ref, d_ref, nm_ref, nv_ref):
        g = p_ref[0].astype(F32)
        for j in range(1, N_DEV):
            g = g + p_ref[j].astype(F32)
        nm = ADAM_B1 * m_ref[...] + (1.0 - ADAM_B1) * g
        nv = ADAM_B2 * v_ref[...] + (1.0 - ADAM_B2) * (g * g)
        m_hat = nm / b1c
        v_hat = nv / b2c
        g_ref[...] = g
        d_ref[...] = -ADAM_LR * (m_hat / (jnp.sqrt(v_hat) + ADAM_EPS) + ADAM_WD * w_ref[...])
        nm_ref[...] = nm
        nv_ref[...] = nv

    ins = [(parts, pl.BlockSpec((N_DEV, tr, C), lambda i: (0, i, 0))), _rows(w, tr), _rows(m, tr), _rows(v, tr)]
    outs = [_out_rows(R, C, F32, tr)] * 4
    return _call(name, body, (R // tr,), ins, outs)


def _all_gather(name, items):
    n = len(items)
    shapes = [tuple(a.shape if idx is None else a.shape[1:]) for a, idx in items]

    def body(*refs):
        x_refs, out_refs = refs[:n], refs[n:2 * n]
        send_sems, recv_sems, local_sems = refs[2 * n:]
        x, y, c = _mesh_pos()
        me, sibling = (x, y, c), (x, y, 1 - c)
        chips = [(1 - x, y), (x, 1 - y), (1 - x, 1 - y)]

        def copy(t, k, block, to, own=False):
            dst = out_refs[t].at[4 * block[0] + 2 * block[1] + block[2]]
            src = dst
            if own:
                src = x_refs[t] if items[t][1] is None else x_refs[t].at[items[t][1]]
            return pltpu.make_async_remote_copy(
                src_ref=src, dst_ref=dst, send_sem=send_sems.at[t, k], recv_sem=recv_sems.at[t, k],
                device_id=to, device_id_type=pl.DeviceIdType.MESH)

        started = []
        for t in range(n):
            src = x_refs[t] if items[t][1] is None else x_refs[t].at[items[t][1]]
            mine = pltpu.make_async_copy(src, out_refs[t].at[4 * x + 2 * y + c], local_sems.at[t])
            mine.start()
            first = [copy(t, 0, me, sibling, own=True)]
            first += [copy(t, 1 + j, me, (*chip, c), own=True) for j, chip in enumerate(chips)]
            for cp in first:
                cp.start()
            started.append((mine, first))
        passed = []
        for t in range(n):
            for j, chip in enumerate(chips):
                copy(t, 1 + j, (*chip, c), me).wait_recv()
                fwd = copy(t, 4 + j, (*chip, c), sibling)
                fwd.start()
                passed.append(fwd)
        for t in range(n):
            copy(t, 0, sibling, me).wait_recv()
            for j, chip in enumerate(chips):
                copy(t, 4 + j, (*chip, 1 - c), me).wait_recv()
        for mine, first in started:
            for cp in first:
                cp.wait_send()
            mine.wait()
        for cp in passed:
            cp.wait_send()

    any_spec = pl.BlockSpec(memory_space=pl.ANY)
    return pl.pallas_call(
        body, name=name,
        out_shape=[SDS((N_DEV,) + s, a.dtype) for s, (a, _) in zip(shapes, items)],
        in_specs=[any_spec] * n,
        out_specs=[any_spec] * n,
        scratch_shapes=[pltpu.SemaphoreType.DMA((n, 7)), pltpu.SemaphoreType.DMA((n, 7)), pltpu.SemaphoreType.DMA((n,))],
    )(*[a for a, _ in items])


def _pad8(t):
    return jnp.pad(t, ((0, SUBLANES - t.shape[0]), (0, 0)))


def _rows_merged(w):
    return w.reshape(w.shape[0] * w.shape[1], w.shape[2])


def _local_grads(x, tgt, norm_mix, norm_mlp, norm_kv, norm_final, conv_w, hub):
    bl, seq, D = x.shape
    T = bl * seq
    h = x.reshape(T, D)
    tgt = tgt.reshape(T, D)
    row = lambda t, l: t[l:l + 1]
    W = hub.weights
    saved = []
    kv = h_kv = hn_kv_t = None
    for l in range(DEPTH):
        if l < N_A_LAYERS:
            bcu, hn_t = _norm_mm(f"l{l}_in", h, row(norm_mix, l), W["w_a_in", l], out_dtype=BF16, transposed=True, hub=hub)
            h2, gated_t = _gate_mm_res(f"l{l}_conv_out", bcu, _pad8(conv_w[l]), _rows_merged(W["w_a_out", l]), h, seq, hub=hub)
            saved.append((h, bcu, gated_t, hn_t))
        else:
            i = l - N_A_LAYERS
            if l == N_A_LAYERS:
                h_kv = h
                kv, hn_kv_t = _norm_mm("kv", h, norm_kv.reshape(1, D), W["w_kv", None], transposed=True, hub=hub)
            q, hn_t = _norm_mm(f"l{l}_q", h, row(norm_mix, l), W["w_q", i], transposed=True)
            per_group = [_attn_fwd(f"l{l}_att{g}", q, kv, g, bl, hub=hub) for g in range(N_GROUPS)]
            o, lse, h2 = _combine_mm_res(f"l{l}_att_out", [p[0] for p in per_group], [p[1] for p in per_group],
                                         W["w_o", i], h)
            saved.append((h, q, o, lse, hn_t))
        a = _norm_mm(f"l{l}_up", h2, row(norm_mlp, l), W["w_up", l], out_dtype=BF16, hub=hub)
        h = _relu2_mm_res(f"l{l}_down", a, _rows_merged(W["w_down", l]), h2, hub=hub)
        saved[-1] = saved[-1] + (h2, a)

    dh, sq_err, d_norm_final = _final_loss("loss", h, tgt, norm_final.reshape(1, D))

    d_norm_mix = [None] * DEPTH
    d_norm_mlp = [None] * DEPTH
    d_conv = [None] * N_A_LAYERS
    d_norm_kv = None
    dkv_acc = [None] * N_GROUPS
    G = hub.grads
    as_slots = lambda g: g.reshape(N_DEV, g.shape[0] // N_DEV, g.shape[1])
    tt = DW_TOKENS
    for l in reversed(range(DEPTH)):
        h2, a = saved[l][-2:]
        h_in = saved[l][0]
        g_mlp = row(norm_mlp, l)
        g_mix = row(norm_mix, l)
        w_up_l = W["w_up", l]
        FF = N_DEV * w_up_l.shape[2]
        da = _nt_relu2_bwd(f"l{l}_down_bwd", dh, _rows_merged(W["w_down", l]), a, hub=hub)
        G["w_down", l] = as_slots(_tn(f"l{l}_dw_down", [_rows2(a, tt, FF // 2, lambda s: s)], _relu2,
                                      [_rows2(dh, tt)], _val, FF, D, T, tt, split=("k", 2)))
        G["w_up", l] = _tn(f"l{l}_dw_up", [_rows2(h2, 2 * tt), _full2(g_mlp)], _normed,
                           [_rows2(da, 2 * tt, FF // 4, lambda s: s)], _val, D, FF, T, 2 * tt, split=("n", 4),
                           out_cols=w_up_l.shape[2], hub=hub)
        dh2, d_norm_mlp[l] = _nt_norm_bwd(f"l{l}_up_bwd", [da], w_up_l, h2, g_mlp, dh, hub=hub)
        if l >= N_A_LAYERS:
            i = l - N_A_LAYERS
            _, q, o, lse, hn_t = saved[l][:5]
            w_o_i, w_q_i = W["w_o", i], W["w_q", i]
            do, delta = _att_out_bwd(f"l{l}_att_out_bwd", dh2, w_o_i, o)
            G["w_o", i] = _tn(f"l{l}_dw_o", [_rows2(o, tt)], _val, [_rows2(dh2, tt)], _val, QW, D, T, tt,
                              out_cols=w_o_i.shape[2])
            dqs = []
            for g in range(N_GROUPS):
                dqs.append(_attn_bwd_dq(f"l{l}_att{g}_dq", q, kv, do, delta, lse, g, bl, hub=hub))
                dkv_acc[g] = _attn_bwd_dkv(f"l{l}_att{g}_dkv", q, kv, do, delta, lse, g, bl, prev=dkv_acc[g], hub=hub)
            G["w_q", i] = _tn(f"l{l}_dw_q", [_cols2(hn_t, tt)], None,
                              [_rows2(t, tt) for t in dqs], _concat_f32, D, N_GROUPS * QW, T, tt, out_cols=w_q_i.shape[2])
            dh, d_norm_mix[l] = _nt_norm_bwd(f"l{l}_q_bwd", dqs, w_q_i, h_in, g_mix, dh2, hub=hub)
            if l == N_A_LAYERS:
                dkvs = [t for pair in dkv_acc for t in pair]
                g_kv = norm_kv.reshape(1, D)
                w_kv = W["w_kv", None]
                per_call = len(dkvs) // 2
                halves = [_tn(f"dw_kv{p}", [_cols2(hn_kv_t, tt)], None,
                              [_rows2(t, tt) for t in dkvs[p * per_call:(p + 1) * per_call]], _concat_f32,
                              D, per_call * QW, T, tt, out_cols=w_kv.shape[2]) for p in range(2)]
                G["w_kv", None] = jnp.concatenate(halves, axis=0)
                dh, d_norm_kv = _nt_norm_bwd("kv_bwd", dkvs, w_kv, h_kv, g_kv, dh, tm=MM_ROWS // 2, hub=hub)
        else:
            _, bcu, gated_t, hn_t = saved[l][:4]
            cw = _pad8(conv_w[l])
            w_in_l = W["w_a_in", l]
            dgated = _nt_plain(f"l{l}_conv_out_bwd", dh2, _rows_merged(W["w_a_out", l]))
            G["w_a_out", l] = as_slots(_tn(f"l{l}_dw_conv_out", [_cols2(gated_t, 2 * tt)], None,
                                           [_rows2(dh2, 2 * tt)], _val, D, D, T, 2 * tt, hub=hub))
            dbcu, d_conv[l] = _conv_bwd(f"l{l}_conv_bwd", bcu, dgated, cw, seq, hub=hub)
            G["w_a_in", l] = _tn(f"l{l}_dw_in", [_cols2(hn_t, 2 * tt)], None,
                                 [_rows2(dbcu, 2 * tt, 3 * D // 2, lambda s: s)], _val, D, 3 * D, T, 2 * tt, split=("n", 2),
                                 out_cols=w_in_l.shape[2], hub=hub)
            dh, d_norm_mix[l] = _nt_norm_bwd(f"l{l}_in_bwd", [dbcu], w_in_l, h_in, g_mix, dh2, hub=hub)

    small = jnp.concatenate(d_norm_mix + d_norm_mlp + [d_norm_kv, d_norm_final] + d_conv, axis=0)
    return sq_err, dh.reshape(bl, seq, D), small


def kernel(x, norm_mix, norm_mlp, w_a_in, conv_w, w_a_out, norm_kv, w_kv, w_q, w_o, w_up, w_down, norm_final, loss_target, m_norm_mix, m_norm_mlp, m_w_a_in, m_conv_w, m_w_a_out, m_norm_kv, m_w_kv, m_w_q, m_w_o, m_w_up, m_w_down, m_norm_final, v_norm_mix, v_norm_mlp, v_w_a_in, v_conv_w, v_w_a_out, v_norm_kv, v_w_kv, v_w_q, v_w_o, v_w_up, v_w_down, v_norm_final):
    D = x.shape[-1]
    xi, yi, ci = _mesh_pos()
    me_idx = 4 * xi + 2 * yi + ci
    w_big = dict(w_a_in=w_a_in, w_a_out=w_a_out, w_kv=w_kv, w_q=w_q, w_o=w_o, w_up=w_up, w_down=w_down)
    m_big = dict(w_a_in=m_w_a_in, w_a_out=m_w_a_out, w_kv=m_w_kv, w_q=m_w_q, w_o=m_w_o, w_up=m_w_up, w_down=m_w_down)
    v_big = dict(w_a_in=v_w_a_in, w_a_out=v_w_a_out, w_kv=v_w_kv, w_q=v_w_q, w_o=v_w_o, w_up=v_w_up, w_down=v_w_down)
    names = list(w_big)

    shards = {n: w.astype(BF16) for n, w in w_big.items()}
    landing = {n: lax.empty((N_DEV,) + w.shape, BF16) for n, w in w_big.items()}
    hub = _Hub(FETCH_DURING, PUSH_DURING, shards, landing)
    dc = conv_w.shape[-1]
    taps = conv_w.shape[0] * conv_w.shape[1]
    got = _all_gather("gather_first", [(shards[n], l) for n, l in FETCH_UP_FRONT] + [(_pad8(conv_w.reshape(taps, dc)), None)])
    for key, w in zip(FETCH_UP_FRONT, got):
        hub.weights[key] = w
    conv_full = jnp.moveaxis(got[-1][:, :taps], 0, 1).reshape(conv_w.shape[0], conv_w.shape[1], N_DEV * dc)

    sq_err, grad_x, small = _local_grads(x, loss_target, norm_mix, norm_mlp, norm_kv, norm_final, conv_full, hub)

    grads, deltas, new_m, new_v = {}, {}, {}, {}
    for n in names:
        shape = w_big[n].shape
        cols = shape[-1]
        flat = lambda t: t.reshape(-1, cols)
        parts = hub.landing[n].reshape(N_DEV, -1, cols)
        tr = parts.shape[1]
        while tr * cols > ADAMW_TILE and tr % 32 == 0:
            tr //= 2
        outs = _sum8_adamw(f"adamw_{n}", parts, flat(w_big[n]), flat(m_big[n]), flat(v_big[n]), tr=tr)
        grads[n], deltas[n], new_m[n], new_v[n] = (t.reshape(shape) for t in outs)

    n_gain = 2 * DEPTH + 2
    n_small = small.shape[0]
    small = jnp.concatenate([small, jnp.full((SUBLANES, D), sq_err, F32)], axis=0)
    rows_small = small.shape[0]
    small_all = _all_gather("gather_small_grads", [(small, None)])[0]

    def small_pack(nm, nl, nk, nf, cw):
        gains = jnp.concatenate([nm, nl, nk.reshape(1, D), nf.reshape(1, D)], axis=0)
        taps_full = lax.dynamic_update_slice(jnp.zeros((taps, D), F32), cw.reshape(taps, dc), (0, me_idx * dc))
        return jnp.concatenate([gains, taps_full, jnp.zeros((SUBLANES, D), F32)], axis=0)

    sp = [small_pack(*t) for t in ((norm_mix, norm_mlp, norm_kv, norm_final, conv_w),
                                   (m_norm_mix, m_norm_mlp, m_norm_kv, m_norm_final, m_conv_w),
                                   (v_norm_mix, v_norm_mlp, v_norm_kv, v_norm_final, v_conv_w))]
    small_out = _sum8_adamw("adamw_small", small_all, *sp, tr=rows_small)
    loss = small_out[0][n_small, 0] * (0.5 / D)

    def small_unpack(t):
        res = dict(norm_mix=t[0:DEPTH], norm_mlp=t[DEPTH:2 * DEPTH], norm_kv=t[2 * DEPTH], norm_final=t[2 * DEPTH + 1])
        res["conv_w"] = lax.dynamic_slice(t[n_gain:], (0, me_idx * dc), (taps, dc)).reshape(conv_w.shape)
        return res

    for dst, t in zip((grads, deltas, new_m, new_v), small_out):
        dst.update(small_unpack(t))

    order = ["norm_mix", "norm_mlp", "w_a_in", "conv_w", "w_a_out", "norm_kv", "w_kv", "w_q", "w_o", "w_up", "w_down",
             "norm_final"]
    return (loss, grad_x, *[grads[n] for n in order], *[deltas[n] for n in order], *[new_m[n] for n in order],
            *[new_v[n] for n in order])
```

```python
import jax
import jax.numpy as jnp
from jax import lax
from jax.experimental import pallas as pl
from jax.experimental.pallas import tpu as pltpu

F32 = jnp.float32
BF16 = jnp.bfloat16
SDS = jax.ShapeDtypeStruct

EPS = 1e-5
N_A_LAYERS = 2
DEPTH = 4
PATTERNS = ((128, 1), (512, 4), (2048, 16))
N_GROUPS = 3
H_G = 8
HEAD_DIM = 64
QW = H_G * HEAD_DIM
ATT_BLK = 128
ALIBI_MAX_BIAS = 8.0
NEG_INF = -1e30

ADAM_LR = 0.001
ADAM_B1 = 0.9
ADAM_B2 = 0.999
ADAM_EPS = 1e-08
ADAM_WD = 0.01
ADAM_STEP = 10

N_DEV = 8
SUBLANES = 8
HALO = 16
V7X_VMEM_LIMIT = 48 * 1024 * 1024
MXU_COLS = 256
MM_CHUNK = 512
MM_ROWS = 512
ADAMW_TILE = 256 * 1024
DW_TOKENS = 1024

FETCH_UP_FRONT = [("w_a_in", 0)]
FETCH_DURING = {
    "l0_in": [("w_a_out", 0), ("w_up", 0)], "l0_conv_out": [("w_down", 0)], "l0_up": [("w_a_in", 1), ("w_a_out", 1)], "l0_down": [("w_up", 1)],
    "l1_in": [("w_down", 1)], "l1_conv_out": [("w_kv", None)],
    "l1_up": [("w_q", 0), ("w_o", 0), ("w_q", 1), ("w_o", 1)], "l1_down": [("w_up", 2)],
    "kv": [("w_down", 2)], "l2_att0": [("w_up", 3)], "l2_att1": [("w_down", 3)],
}
PUSH_DURING = {
    "l3_dw_up": [("w_down", 3, 0, 2)], "l3_up_bwd": [("w_down", 3, 1, 2)],
    "l3_att0_dq": [("w_up", 3, 0, 2)], "l3_att0_dkv": [("w_up", 3, 1, 2)], "l3_att1_dq": [("w_o", 1)], "l3_q_bwd": [("w_q", 1)],
    "l2_dw_up": [("w_down", 2, 0, 2)], "l2_up_bwd": [("w_down", 2, 1, 2)],
    "l2_att0_dq": [("w_up", 2, 0, 2)], "l2_att0_dkv": [("w_up", 2, 1, 2)], "l2_att1_dq": [("w_o", 0)], "l2_q_bwd": [("w_q", 0)],
    "kv_bwd": [("w_kv", None, 0, 2)], "l1_down_bwd": [("w_kv", None, 1, 2)],
    "l1_dw_up": [("w_down", 1, 0, 2)], "l1_up_bwd": [("w_down", 1, 1, 2)], "l1_conv_bwd": [("w_up", 1, 0, 2)],
    "l1_dw_in": [("w_up", 1, 1, 2), ("w_a_out", 1)], "l1_in_bwd": [("w_a_in", 1, 0, 2)], "l0_down_bwd": [("w_a_in", 1, 1, 2)],
    "l0_dw_up": [("w_down", 0, 0, 2)], "l0_up_bwd": [("w_down", 0, 1, 2)], "l0_conv_bwd": [("w_up", 0, 0, 2)],
    "l0_dw_in": [("w_up", 0, 1, 2), ("w_a_out", 0)], "l0_in_bwd": [("w_a_in", 0)],
}


def _mesh_pos():
    return lax.axis_index("x"), lax.axis_index("y"), lax.axis_index("c")


def _flip(v, bit):
    return 1 - v if bit else v


class _Transfer:
    def __init__(self, kind, key, src, src_idx=None, dst=None, dst_idx=None, dst_shape=None, rows=None):
        self.kind, self.key, self.src, self.src_idx = kind, key, src, src_idx
        self.dst, self.dst_idx, self.dst_shape, self.rows = dst, dst_idx, dst_shape, rows

    def copies(self, src_ref, dst_ref, send_sems, recv_sems, local_sem):
        x, y, c = _mesh_pos()
        me = 4 * x + 2 * y + c
        part = (lambda r: r) if self.rows is None else (lambda r: r.at[pl.ds(*self.rows)])

        def dst_slot(j):
            r = dst_ref.at[j]
            return part(r if self.dst_idx is None else r.at[self.dst_idx])

        def copy(k, src, dst_j, to):
            return pltpu.make_async_remote_copy(
                src_ref=src, dst_ref=dst_slot(dst_j), send_sem=send_sems.at[k], recv_sem=recv_sems.at[k],
                device_id=to, device_id_type=pl.DeviceIdType.MESH)

        if self.kind == "exchange":
            local = pltpu.make_async_copy(part(src_ref.at[me]), dst_slot(me), local_sem)
            sends, arrivals = [], []
            for k in range(1, N_DEV):
                peer = (_flip(x, k & 4), _flip(y, k & 2), _flip(c, k & 1))
                peer_idx = 4 * peer[0] + 2 * peer[1] + peer[2]
                sends.append(copy(k - 1, part(src_ref.at[peer_idx]), me, peer))
                arrivals.append(copy(k - 1, part(src_ref.at[peer_idx]), peer_idx, peer))
            return local, sends, [], arrivals

        own = part(src_ref if self.src_idx is None else src_ref.at[self.src_idx])
        idx = lambda px, py, pc: 4 * px + 2 * py + pc
        sibling = (x, y, 1 - c)
        chips = [(1 - x, y), (x, 1 - y), (1 - x, 1 - y)]
        local = pltpu.make_async_copy(own, dst_slot(me), local_sem)
        sends = [copy(0, own, me, sibling)] + [copy(1 + j, own, me, (*chip, c)) for j, chip in enumerate(chips)]
        relays = [(copy(1 + j, own, idx(*chip, c), sibling), copy(4 + j, dst_slot(idx(*chip, c)), idx(*chip, c), sibling))
                  for j, chip in enumerate(chips)]
        arrivals = [copy(0, own, idx(*sibling), sibling)]
        arrivals += [copy(4 + j, own, idx(*chip, 1 - c), sibling) for j, chip in enumerate(chips)]
        return local, sends, relays, arrivals


class _Hub:
    def __init__(self, fetch, push, shards, landing):
        self.fetch, self.push, self.shards, self.landing = fetch, push, shards, landing
        self.weights = {}
        self.arriving = {}
        self.grads = {}

    def transfers(self, host):
        out = []
        for name, l, *part in self.fetch.get(host, ()):
            src = self.shards[name]
            shard = tuple(src.shape if l is None else src.shape[1:])
            p, n = part or (0, 1)
            rows = None if n == 1 else (p * (shard[0] // n), shard[0] // n)
            out.append(_Transfer("gather", (name, l, p == n - 1), src, src_idx=l, dst=self.arriving.get((name, l)),
                                 dst_shape=(N_DEV,) + shard, rows=rows))
        for name, l, *part in self.push.get(host, ()):
            src = self.grads[name, l]
            p, n = part or (0, 1)
            rows = None if n == 1 else (p * (src.shape[1] // n), src.shape[1] // n)
            out.append(_Transfer("exchange", (name, l, p == n - 1), src, dst=self.landing[name], dst_idx=l, rows=rows))
        return out

    def accept(self, transfers, results):
        for t, r in zip(transfers, results):
            name, l, complete = t.key
            if t.kind == "exchange":
                self.landing[name] = r
            elif complete:
                self.weights[name, l] = r
            else:
                self.arriving[name, l] = r


def _call(name, body, grid, ins, outs, scratch=(), hub=None):
    transfers = hub.transfers(name) if hub is not None else []
    n_in, n_out, n_scr, n_tr = len(ins), len(outs), len(scratch), len(transfers)
    c_in, c_out, aliases, places = [], [], {}, []
    for t in transfers:
        c_in.append(t.src)
        src_pos = len(c_in) - 1
        if t.dst is not None:
            c_in.append(t.dst)
            aliases[n_in + len(c_in) - 1] = n_out + len(c_out)
            c_out.append(SDS(t.dst.shape, t.dst.dtype))
        else:
            c_out.append(SDS(t.dst_shape, t.src.dtype))
        places.append((src_pos, len(c_out) - 1))
    sems = [pltpu.SemaphoreType.DMA((n_tr, N_DEV - 1)), pltpu.SemaphoreType.DMA((n_tr, N_DEV - 1)),
            pltpu.SemaphoreType.DMA((n_tr,))] if n_tr else []

    def wrapped(*refs):
        in_refs = refs[:n_in]
        cin_refs = refs[n_in:n_in + len(c_in)]
        o0 = n_in + len(c_in)
        out_refs = refs[o0:o0 + n_out]
        cout_refs = refs[o0 + n_out:o0 + n_out + len(c_out)]
        s0 = o0 + n_out + len(c_out)
        scr_refs = refs[s0:s0 + n_scr]
        if n_tr:
            send_sems, recv_sems, local_sems = refs[s0 + n_scr:]
            first = last = relay = None
            for ax, n in enumerate(grid):
                i = pl.program_id(ax)
                at_relay = (i == max(n - 2, 0)) if ax == len(grid) - 1 else (i == n - 1)
                first = (i == 0) if first is None else first & (i == 0)
                last = (i == n - 1) if last is None else last & (i == n - 1)
                relay = at_relay if relay is None else relay & at_relay

            def all_copies():
                return [t.copies(cin_refs[sp], cout_refs[dp], send_sems.at[n], recv_sems.at[n], local_sems.at[n])
                        for n, (t, (sp, dp)) in enumerate(zip(transfers, places))]

            @pl.when(first)
            def _():
                for local, sends, _, _ in all_copies():
                    local.start()
                    for cp in sends:
                        cp.start()

            def pass_on():
                @pl.when(relay)
                def _():
                    for _, _, relays, _ in all_copies():
                        for arrival, onward in relays:
                            arrival.wait_recv()
                            onward.start()

            if grid[-1] > 1:
                pass_on()

        body(*in_refs, *out_refs, *scr_refs)

        if n_tr:
            if grid[-1] == 1:
                pass_on()

            @pl.when(last)
            def _():
                for local, sends, relays, arrivals in all_copies():
                    for cp in arrivals:
                        cp.wait_recv()
                    for cp in sends + [onward for _, onward in relays]:
                        cp.wait_send()
                    local.wait()

    any_spec = pl.BlockSpec(memory_space=pl.ANY)
    res = pl.pallas_call(
        wrapped,
        name=name,
        grid=grid,
        in_specs=[s for _, s in ins] + [any_spec] * len(c_in),
        out_specs=[s for _, s in outs] + [any_spec] * len(c_out),
        out_shape=[o for o, _ in outs] + c_out,
        scratch_shapes=list(scratch) + sems,
        input_output_aliases=aliases,
        compiler_params=pltpu.CompilerParams(
            dimension_semantics=("arbitrary",) * len(grid), vmem_limit_bytes=V7X_VMEM_LIMIT),
    )(*[a for a, _ in ins], *c_in)
    if n_tr:
        hub.accept(transfers, res[n_out:])
    return res[:n_out]


def _rows(a, tm, cb=None, col=0):
    cb = cb or a.shape[1]
    return (a, pl.BlockSpec((tm, cb), lambda i: (i, col)))


def _full(a):
    nd = a.ndim
    return (a, pl.BlockSpec(a.shape, lambda i: (0,) * nd))


def _prev8(a, tm, cb, col):
    return (a, pl.BlockSpec((HALO, cb), lambda i: (jnp.maximum(i * (tm // HALO) - 1, 0), col)))


def _next8(a, tm, cb, col):
    last = a.shape[0] // HALO - 1
    return (a, pl.BlockSpec((HALO, cb), lambda i: (jnp.minimum((i + 1) * (tm // HALO), last), col)))


def _rows2(a, tt, cb=None, colfn=None):
    cb = cb or a.shape[1]
    colfn = colfn or (lambda s: 0)
    return (a, pl.BlockSpec((tt, cb), lambda s, t: (t, colfn(s))))


def _full2(a):
    nd = a.ndim
    return (a, pl.BlockSpec(a.shape, lambda s, t: (0,) * nd))


def _out_rows(T, n, dtype, tm):
    return (SDS((T, n), dtype), pl.BlockSpec((tm, n), lambda i: (i, 0)))


def _out_acc8(d):
    return (SDS((SUBLANES, d), F32), pl.BlockSpec((SUBLANES, d), lambda i: (0, 0)))


def _rstd(x):
    return lax.rsqrt(jnp.mean(x * x, axis=-1, keepdims=True) + EPS)


def _normed(h_ref, g_ref):
    x = h_ref[...]
    return x * _rstd(x) * g_ref[...]


def _acc8(ref, val, i, n):
    part = val.reshape(-1, SUBLANES, val.shape[-1]).sum(axis=0)

    @pl.when(i == 0)
    def _():
        ref[...] = part

    @pl.when(i > 0)
    def _():
        ref[...] += part

    @pl.when(i == n - 1)
    def _():
        ref[...] = jnp.broadcast_to(jnp.sum(ref[...], axis=0, keepdims=True), ref.shape)


def _gate(b_ref, c_ref, u_ref, ch_ref, uh_ref, cw_ref, first):
    b, c, u = (r[...].astype(F32) for r in (b_ref, c_ref, u_ref))
    cu = c * u
    halo = jnp.where(first, 0.0, ch_ref[...].astype(F32) * uh_ref[...].astype(F32))
    rows = lax.broadcasted_iota(jnp.int32, cu.shape, 0)
    h1 = halo[HALO - 1:HALO, :]
    h2 = halo[HALO - 2:HALO - 1, :]
    cu1 = jnp.where(rows == 0, h1, pltpu.roll(cu, 1, 0))
    cu2 = jnp.where(rows == 0, h2, jnp.where(rows == 1, h1, pltpu.roll(cu, 2, 0)))
    conv = cw_ref[0:1, :] * cu + cw_ref[1:2, :] * cu1 + cw_ref[2:3, :] * cu2
    return b * conv, (b, c, u), conv, (cu, cu1, cu2)


def _relu2(a_ref):
    r = jnp.maximum(a_ref[...].astype(F32), 0.0)
    return r * r


def _dot(a, b):
    return jnp.dot(a, b, preferred_element_type=F32)


def _dot_nt(a, b):
    return lax.dot_general(a, b, (((1,), (1,)), ((), ())), preferred_element_type=F32)


def _chunks(n):
    c = min(MM_CHUNK, n)
    while n % c:
        c -= 128
    assert c > 0, n
    return [(k * c, (k + 1) * c) for k in range(n // c)]


def _col_weight(w):
    _, K, ns = w.shape
    N = N_DEV * ns
    direct = ns % MXU_COLS == 0
    scratch = [] if direct else [pltpu.VMEM((K, N), BF16)]

    def prepare(w_ref, s_ref, step):
        if direct:
            return

        @pl.when(step == 0)
        def _():
            for j in range(N_DEV):
                s_ref[:, j * ns:(j + 1) * ns] = w_ref[j]

    def chunks(w_ref, s_ref):
        if direct:
            return [(j * ns, (j + 1) * ns, (lambda j=j: w_ref[j])) for j in range(N_DEV)]
        return [(lo, hi, (lambda lo=lo, hi=hi: s_ref[:, lo:hi])) for lo, hi in _chunks(N)]

    return N, scratch, prepare, chunks


def _out_cols(n, T, tm):
    return (SDS((n, T), BF16), pl.BlockSpec((n, tm), lambda i: (0, i)))


def _norm_mm(name, h, g, w, tm=MM_ROWS, out_dtype=F32, transposed=False, hub=None):
    T, D = h.shape
    N, w_scratch, prepare, chunks = _col_weight(w)

    def body(h_ref, g_ref, w_ref, o_ref, *rest):
        at_ref, s = (rest[0], rest[1:]) if transposed else (None, rest)
        s_ref = s[0] if s else None
        prepare(w_ref, s_ref, pl.program_id(0))
        a32 = _normed(h_ref, g_ref)
        a = a32.astype(BF16)
        for lo, hi, load in chunks(w_ref, s_ref):
            o_ref[:, lo:hi] = _dot(a, load()).astype(out_dtype)
        if transposed:
            at_ref[...] = a32.T.astype(BF16)

    outs = [_out_rows(T, N, out_dtype, tm)] + ([_out_cols(D, T, tm)] if transposed else [])
    res = _call(name, body, (T // tm,), [_rows(h, tm), _full(g), _full(w)], outs, scratch=w_scratch, hub=hub)
    return res if transposed else res[0]


def _gate_mm_res(name, bcu, cw, w, h, seq, tm=MM_ROWS, hub=None):
    T, D = h.shape

    def body(b_ref, c_ref, u_ref, ch_ref, uh_ref, cw_ref, w_ref, h_ref, o_ref, gt_ref):
        first = (pl.program_id(0) * tm) % seq == 0
        gated32 = _gate(b_ref, c_ref, u_ref, ch_ref, uh_ref, cw_ref, first)[0]
        gated = gated32.astype(BF16)
        for lo, hi in _chunks(D):
            o_ref[:, lo:hi] = h_ref[:, lo:hi] + _dot(gated, w_ref[:, lo:hi])
        gt_ref[...] = gated32.T.astype(BF16)

    ins = [_rows(bcu, tm, D, 0), _rows(bcu, tm, D, 1), _rows(bcu, tm, D, 2), _prev8(bcu, tm, D, 1),
           _prev8(bcu, tm, D, 2), _full(cw), _full(w), _rows(h, tm)]
    return _call(name, body, (T // tm,), ins, [_out_rows(T, D, F32, tm), _out_cols(D, T, tm)], hub=hub)


def _relu2_mm_res(name, a, w, h, tm=MM_ROWS, hub=None):
    T, D = h.shape
    K = a.shape[1]

    def body(a_ref, w_ref, h_ref, o_ref, acc_ref):
        for n, (lo, hi) in enumerate(_chunks(K)):
            d = _dot(_relu2(a_ref.at[:, lo:hi]).astype(BF16), w_ref[lo:hi, :])
            if n == 0:
                acc_ref[...] = d
            else:
                acc_ref[...] += d
        o_ref[...] = h_ref[...] + acc_ref[...]

    return _call(name, body, (T // tm,), [_rows(a, tm), _full(w), _rows(h, tm)], [_out_rows(T, D, F32, tm)],
                 scratch=[pltpu.VMEM((tm, D), F32)], hub=hub)[0]


def _combine_mm_res(name, os_, lses, w, h, tm=MM_ROWS):
    T, D = h.shape
    _, w_scratch, prepare, chunks = _col_weight(w)

    def body(o0, o1, o2, l0, l1, l2, w_ref, h_ref, o_ref, lse_ref, out_ref, *s):
        s_ref = s[0] if s else None
        prepare(w_ref, s_ref, pl.program_id(0))
        ls = [l0[...], l1[...], l2[...]]
        mx = jnp.maximum(jnp.maximum(ls[0], ls[1]), ls[2])
        es = [jnp.exp(l - mx) for l in ls]
        den = es[0] + es[1] + es[2]
        o = (es[0] * o0[...] + es[1] * o1[...] + es[2] * o2[...]) / den
        o_ref[...] = o
        lse_ref[...] = mx + jnp.log(den)
        ob = o.astype(BF16)
        for lo, hi, load in chunks(w_ref, s_ref):
            out_ref[:, lo:hi] = h_ref[:, lo:hi] + _dot(ob, load())

    ins = [_rows(t, tm) for t in list(os_) + list(lses)] + [_full(w), _rows(h, tm)]
    outs = [_out_rows(T, QW, F32, tm), _out_rows(T, QW, F32, tm), _out_rows(T, D, F32, tm)]
    return _call(name, body, (T // tm,), ins, outs, scratch=w_scratch)


def _nt_relu2_bwd(name, dh, w, a, tm=MM_ROWS, hub=None):
    T, _ = dh.shape
    K = w.shape[0]

    def body(dh_ref, w_ref, a_ref, o_ref):
        d = dh_ref[...].astype(BF16)
        for lo, hi in _chunks(K):
            dr = _dot_nt(d, w_ref[lo:hi, :])
            o_ref[:, lo:hi] = (dr * (2.0 * jnp.maximum(a_ref[:, lo:hi].astype(F32), 0.0))).astype(BF16)

    return _call(name, body, (T // tm,), [_rows(dh, tm), _full(w), _rows(a, tm)], [_out_rows(T, K, BF16, tm)], hub=hub)[0]


def _concat_bf16(*refs):
    vals = [r[...].astype(BF16) for r in refs]
    return vals[0] if len(vals) == 1 else jnp.concatenate(vals, axis=1)


def _nt_plain(name, dy, w, tm=MM_ROWS):
    T, N = dy.shape
    K = w.shape[0]

    def body(dy_ref, w_ref, o_ref, acc_ref):
        for n, (lo, hi) in enumerate(_chunks(N)):
            d = _dot_nt(dy_ref[:, lo:hi].astype(BF16), w_ref[:, lo:hi])
            if n == 0:
                acc_ref[...] = d
            else:
                acc_ref[...] += d
        o_ref[...] = acc_ref[...]

    return _call(name, body, (T // tm,), [_rows(dy, tm), _full(w)], [_out_rows(T, K, F32, tm)],
                 scratch=[pltpu.VMEM((tm, K), F32)])[0]


def _att_out_bwd(name, dy, w, o, tm=MM_ROWS):
    T, _ = dy.shape
    K = w.shape[1]
    _, w_scratch, prepare, chunks = _col_weight(w)

    def body(dy_ref, w_ref, o_ref, do_ref, dl_ref, acc_ref, *s):
        s_ref = s[0] if s else None
        prepare(w_ref, s_ref, pl.program_id(0))
        for n, (lo, hi, load) in enumerate(chunks(w_ref, s_ref)):
            d = _dot_nt(dy_ref[:, lo:hi].astype(BF16), load())
            if n == 0:
                acc_ref[...] = d
            else:
                acc_ref[...] += d
        do = acc_ref[...]
        do_ref[...] = do
        prod = do * o_ref[...]
        high = prod.astype(BF16)
        low = (prod - high.astype(F32)).astype(BF16)
        head_of = lambda axis: jnp.right_shift(lax.broadcasted_iota(jnp.int32, (K, K), axis), HEAD_DIM.bit_length() - 1)
        same_head = jnp.where(head_of(0) == head_of(1), 1.0, 0.0).astype(BF16)
        dl_ref[...] = _dot(high, same_head) + _dot(low, same_head)

    outs = [_out_rows(T, K, F32, tm), _out_rows(T, K, F32, tm)]
    return _call(name, body, (T // tm,), [_rows(dy, tm), _full(w), _rows(o, tm)], outs,
                 scratch=[pltpu.VMEM((tm, K), F32)] + w_scratch)


def _nt_norm_bwd(name, dys, w, h, g, dh_in, tm=MM_ROWS, hub=None):
    T, D = h.shape
    _, w_scratch, prepare, chunks = _col_weight(w)
    n_steps = T // tm
    n_dy = len(dys)

    def body(*refs):
        dy_refs = refs[:n_dy]
        w_ref, h_ref, g_ref, dhin_ref, o_ref, dg_ref, acc_ref = refs[n_dy:n_dy + 7]
        s_ref = refs[n_dy + 7] if len(refs) > n_dy + 7 else None
        i = pl.program_id(0)
        prepare(w_ref, s_ref, i)
        dy = _concat_bf16(*dy_refs)
        for n, (lo, hi, load) in enumerate(chunks(w_ref, s_ref)):
            d = _dot_nt(dy[:, lo:hi], load())
            if n == 0:
                acc_ref[...] = d
            else:
                acc_ref[...] += d
        dn = acc_ref[...]
        x = h_ref[...]
        rstd = _rstd(x)
        xhat = x * rstd
        dxhat = dn * g_ref[...]
        dx = rstd * (dxhat - xhat * jnp.mean(dxhat * xhat, axis=-1, keepdims=True))
        o_ref[...] = dhin_ref[...] + dx
        _acc8(dg_ref, dn * xhat, i, n_steps)

    ins = [_rows(d, tm) for d in dys] + [_full(w), _rows(h, tm), _full(g), _rows(dh_in, tm)]
    outs = [_out_rows(T, D, F32, tm), _out_acc8(D)]
    dh, dg = _call(name, body, (n_steps,), ins, outs, scratch=[pltpu.VMEM((tm, D), F32)] + w_scratch, hub=hub)
    return dh, dg[0:1]


def _cols2(a_t, tt, kb=None):
    kb = kb or a_t.shape[0]
    return (a_t, pl.BlockSpec((kb, tt), (lambda s, t: (s, t)) if kb != a_t.shape[0] else (lambda s, t: (0, t))))


def _tn(name, a_ins, a_fn, y_ins, y_fn, K, N, T, tt, split=None, out_cols=None, hub=None):
    kind, parts = split or ("n", 1)
    kb, nb = (K // parts, N) if kind == "k" else (K, N // parts)
    n_steps = T // tt
    n_a = len(a_ins)
    n_y = len(y_ins)
    assert out_cols is None or (kind == "n" and nb % out_cols == 0)

    def body(*refs):
        a_refs = refs[:n_a]
        y_refs = refs[n_a:n_a + n_y]
        o_ref, acc_ref = refs[n_a + n_y:]
        t = pl.program_id(1)
        a_t = a_refs[0][...] if a_fn is None else a_fn(*a_refs).T.astype(BF16)
        y = y_fn(*y_refs).astype(BF16)
        for lo, hi in _chunks(nb):
            d = _dot(a_t, y[:, lo:hi])

            @pl.when(t == 0)
            def _():
                acc_ref[:, lo:hi] = d

            @pl.when(t > 0)
            def _():
                acc_ref[:, lo:hi] += d

        @pl.when(t == n_steps - 1)
        def _():
            if out_cols is None:
                o_ref[...] = acc_ref[...].astype(BF16)
            else:
                for j in range(nb // out_cols):
                    o_ref[j] = acc_ref[:, j * out_cols:(j + 1) * out_cols].astype(BF16)

    if out_cols is None:
        out = (SDS((K, N), BF16), pl.BlockSpec((kb, nb), (lambda s, t: (s, 0)) if kind == "k" else (lambda s, t: (0, s))))
    else:
        out = (SDS((N // out_cols, K, out_cols), BF16), pl.BlockSpec((nb // out_cols, K, out_cols), lambda s, t: (s, 0, 0)))
    return _call(name, body, (parts, n_steps), list(a_ins) + list(y_ins), [out],
                 scratch=[pltpu.VMEM((kb, nb), F32)], hub=hub)[0]


def _val(ref):
    return ref[...]


def _concat_f32(*refs):
    vals = [r[...] for r in refs]
    return vals[0] if len(vals) == 1 else jnp.concatenate(vals, axis=1)


ATT_TILE_ROWS = 2048
HEAD_PAIRS = H_G // 2
ATT_SCALE = HEAD_DIM ** -0.5
ATT_UNITS_TOGETHER = 4


def _slope(h):
    return 2.0 ** (-ALIBI_MAX_BIAS * (h + 1) / H_G)


def _att_geom(T, bl, g):
    dil = PATTERNS[g][1]
    sub = ATT_BLK * dil
    nsub = max(1, ATT_TILE_ROWS // sub)
    rows = sub * nsub
    return dil, sub, nsub, rows, T // bl // rows


def _att_specs(T, bl, g):
    _, sub, nsub, rows, nt = _att_geom(T, bl, g)
    last_sub = T // sub - 1
    tile = lambda col: pl.BlockSpec((rows, 128), lambda b, i, hp: (b * nt + i, col(hp)))
    prev = lambda col: pl.BlockSpec((sub, 128), lambda b, i, hp: (jnp.maximum((b * nt + i) * nsub - 1, 0), col(hp)))
    nxt = lambda col: pl.BlockSpec((sub, 128), lambda b, i, hp: (jnp.minimum((b * nt + i + 1) * nsub, last_sub), col(hp)))
    return tile, prev, nxt


def _sub_rows(j, r, dil):
    start = j * ATT_BLK * dil + r
    return pl.ds(start, ATT_BLK, stride=dil) if dil > 1 else pl.ds(start, ATT_BLK)


class _Residues:
    def __init__(self, dil):
        self.dil = dil
        self.whole = dil % SUBLANES == 0
        self.read, self.written = {}, {}

    def _block(self, j):
        return pl.ds(j * ATT_BLK * self.dil, ATT_BLK * self.dil)

    def load(self, ref, j, r):
        if not self.whole:
            return ref[_sub_rows(j, r, self.dil), :]
        if (id(ref), j) not in self.read:
            rows = ref[self._block(j), :]
            self.read[id(ref), j] = jnp.swapaxes(rows.reshape(ATT_BLK, self.dil, rows.shape[-1]), 0, 1)
        return self.read[id(ref), j][r]

    def store(self, ref, j, r, val):
        if not self.whole:
            ref[_sub_rows(j, r, self.dil), :] = val
            return
        got = self.written.setdefault((id(ref), j), {})
        got[r] = val
        if len(got) == self.dil:
            merged = jnp.swapaxes(jnp.stack([got[k] for k in range(self.dil)], axis=0), 0, 1)
            ref[self._block(j), :] = merged.reshape(ATT_BLK * self.dil, val.shape[-1])
            del self.written[id(ref), j]


def _att_consts(hp, dil, keys_first=False):
    h0 = lax.broadcasted_iota(jnp.int32, (ATT_BLK, 128), 1) < HEAD_DIM
    a = lax.broadcasted_iota(jnp.int32, (ATT_BLK, ATT_BLK), 1 if keys_first else 0)
    c = lax.broadcasted_iota(jnp.int32, (ATT_BLK, ATT_BLK), 0 if keys_first else 1)
    dist_p = ((ATT_BLK + a - c) * dil).astype(F32)
    dist_c = ((a - c) * dil).astype(F32)
    bias_p, bias_c = [], []
    for h in range(2):
        slope = jnp.float32(_slope(2 * (HEAD_PAIRS - 1) + h))
        for p in range(HEAD_PAIRS - 2, -1, -1):
            slope = jnp.where(hp == p, jnp.float32(_slope(2 * p + h)), slope)
        bias_p.append(jnp.where(c >= a, -slope * dist_p, NEG_INF))
        bias_c.append(jnp.where(c <= a, -slope * dist_c, NEG_INF))
    return h0, jnp.concatenate(bias_p, axis=0), jnp.concatenate(bias_c, axis=0)


def _stack_heads(x, h0):
    return jnp.concatenate([jnp.where(h0, x, 0.0), jnp.where(h0, 0.0, x)], axis=0).astype(BF16)


def _unstack_heads(x, h0):
    return jnp.where(h0, x[:ATT_BLK], x[ATT_BLK:])


def _stack_cols(x):
    return jnp.concatenate(_head_cols(x), axis=0)


def _head_cols(x):
    return [x[:, 0:1], x[:, HEAD_DIM:HEAD_DIM + 1]]


def _in_groups(units, first_stage, *later_stages):
    for u0 in range(0, len(units), ATT_UNITS_TOGETHER):
        staged = [first_stage(*u) for u in units[u0:u0 + ATT_UNITS_TOGETHER]]
        for stage in later_stages:
            staged = [stage(*s) for s in staged]


def _attn_fwd(name, q, kv, g, bl, hub=None):
    T = q.shape[0]
    dil, _, nsub, _, _ = _att_geom(T, bl, g)
    tile, prev, _ = _att_specs(T, bl, g)

    def body(q_ref, kp_ref, kc_ref, vp_ref, vc_ref, o_ref, lse_ref):
        first = pl.program_id(1) == 0
        h0, bias_p, bias_c = _att_consts(pl.program_id(2), dil)
        bias_first = jnp.where(first, NEG_INF, bias_p)
        rows = _Residues(dil)

        def with_ones(v):
            return [jnp.where(h0, v, 1.0).astype(BF16), jnp.where(h0, 1.0, v).astype(BF16)]

        def scores(j, r):
            if j == 0:
                kp, vp, bp = rows.load(kp_ref, 0, r), rows.load(vp_ref, 0, r), bias_first
            else:
                kp, vp, bp = rows.load(kc_ref, j - 1, r), rows.load(vc_ref, j - 1, r), bias_p
            kp, kc = kp.astype(BF16), rows.load(kc_ref, j, r).astype(BF16)
            qs = _stack_heads(rows.load(q_ref, j, r) * ATT_SCALE, h0)
            sp = _dot_nt(qs, kp) + bp
            sc = _dot_nt(qs, kc) + bias_c
            return (j, r), sp, sc, with_ones(vp), with_ones(rows.load(vc_ref, j, r))

        def weights(unit, sp, sc, vp, vc):
            mx = jnp.max(jnp.maximum(sp, sc), axis=-1, keepdims=True)
            return unit, mx, jnp.exp(sp - mx).astype(BF16), jnp.exp(sc - mx).astype(BF16), vp, vc

        def outputs(unit, mx, ep, ec, vp, vc):
            heads = [slice(h * ATT_BLK, (h + 1) * ATT_BLK) for h in range(2)]
            acc = [_dot(ep[hs], vp[h]) + _dot(ec[hs], vc[h]) for h, hs in enumerate(heads)]
            den = [pltpu.roll(a, HEAD_DIM, 1) for a in acc]
            rows.store(o_ref, *unit, jnp.where(h0, acc[0] / den[0], acc[1] / den[1]))
            rows.store(lse_ref, *unit, jnp.where(h0, mx[heads[0]] + jnp.log(den[0]), mx[heads[1]] + jnp.log(den[1])))
            return ()

        _in_groups([(j, r) for j in range(nsub) for r in range(dil)], scores, weights, outputs)

    ins = [(q, tile(lambda hp: 4 * g + hp)), (kv, prev(lambda hp: 8 * g + hp)), (kv, tile(lambda hp: 8 * g + hp)),
           (kv, prev(lambda hp: 8 * g + 4 + hp)), (kv, tile(lambda hp: 8 * g + 4 + hp))]
    out = (SDS((T, QW), F32), tile(lambda hp: hp))
    _, _, _, _, nt = _att_geom(T, bl, g)
    return _call(name, body, (bl, nt, HEAD_PAIRS), ins, [out, out], hub=hub)


def _attn_bwd_dq(name, q, kv, do, delta, lse, g, bl, hub=None):
    T = q.shape[0]
    dil, _, nsub, _, nt = _att_geom(T, bl, g)
    tile, prev, _ = _att_specs(T, bl, g)

    def body(q_ref, kp_ref, kc_ref, vp_ref, vc_ref, do_ref, dl_ref, lse_ref, dq_ref):
        first = pl.program_id(1) == 0
        h0, bias_p, bias_c = _att_consts(pl.program_id(2), dil)
        bias_first = jnp.where(first, NEG_INF, bias_p)
        rows = _Residues(dil)

        def probs(j, r):
            if j == 0:
                kp, vp, bp = rows.load(kp_ref, 0, r), rows.load(vp_ref, 0, r), bias_first
            else:
                kp, vp, bp = rows.load(kc_ref, j - 1, r), rows.load(vc_ref, j - 1, r), bias_p
            kp, vp = kp.astype(BF16), vp.astype(BF16)
            kc, vc = rows.load(kc_ref, j, r).astype(BF16), rows.load(vc_ref, j, r).astype(BF16)
            qs = _stack_heads(rows.load(q_ref, j, r) * ATT_SCALE, h0)
            dos = _stack_heads(rows.load(do_ref, j, r), h0)
            lse = _stack_cols(rows.load(lse_ref, j, r))
            pp = jnp.exp(_dot_nt(qs, kp) + bp - lse)
            pc = jnp.exp(_dot_nt(qs, kc) + bias_c - lse)
            return (j, r), pp, pc, _dot_nt(dos, vp), _dot_nt(dos, vc), kp, kc

        def dscores(unit, pp, pc, dpp, dpc, kp, kc):
            dl = _stack_cols(rows.load(dl_ref, *unit))
            return unit, (pp * (dpp - dl)).astype(BF16), (pc * (dpc - dl)).astype(BF16), kp, kc

        def outputs(unit, dsp, dsc, kp, kc):
            rows.store(dq_ref, *unit, _unstack_heads(_dot(dsp, kp) + _dot(dsc, kc), h0) * ATT_SCALE)
            return ()

        _in_groups([(j, r) for j in range(nsub) for r in range(dil)], probs, dscores, outputs)

    own = lambda hp: hp
    ins = [(q, tile(lambda hp: 4 * g + hp)), (kv, prev(lambda hp: 8 * g + hp)), (kv, tile(lambda hp: 8 * g + hp)),
           (kv, prev(lambda hp: 8 * g + 4 + hp)), (kv, tile(lambda hp: 8 * g + 4 + hp)),
           (do, tile(own)), (delta, tile(own)), (lse, tile(own))]
    return _call(name, body, (bl, nt, HEAD_PAIRS), ins, [(SDS((T, QW), F32), tile(own))], hub=hub)[0]


def _attn_bwd_dkv(name, q, kv, do, delta, lse, g, bl, prev=None, hub=None):
    T = q.shape[0]
    dil, _, nsub, _, nt = _att_geom(T, bl, g)
    tile, _, nxt = _att_specs(T, bl, g)
    has_prev = prev is not None

    def body(*refs):
        k_ref, v_ref, q_ref, qn_ref, do_ref, don_ref, dl_ref, dln_ref, l_ref, ln_ref = refs[:10]
        rest = refs[10:]
        if has_prev:
            dkp_ref, dvp_ref, dk_ref, dv_ref = rest
        else:
            dk_ref, dv_ref = rest
        last = pl.program_id(1) == nt - 1
        h0, bias_p, bias_c = _att_consts(pl.program_id(2), dil, keys_first=True)
        bias_last = jnp.where(last, NEG_INF, bias_p)
        rows = _Residues(dil)

        def per_query_rows(x):
            xt = x.T
            return jnp.concatenate([jnp.broadcast_to(xt[0:1], (ATT_BLK, ATT_BLK)),
                                    jnp.broadcast_to(xt[HEAD_DIM:HEAD_DIM + 1], (ATT_BLK, ATT_BLK))], axis=0)

        def probs(j, r):
            ks = _stack_heads(rows.load(k_ref, j, r), h0)
            vs = _stack_heads(rows.load(v_ref, j, r), h0)
            sets = [(q_ref, do_ref, dl_ref, l_ref, j, bias_c)]
            if j < nsub - 1:
                sets.append((q_ref, do_ref, dl_ref, l_ref, j + 1, bias_p))
            else:
                sets.append((qn_ref, don_ref, dln_ref, ln_ref, 0, bias_last))
            out = []
            for qr, dor, dlr, lr, jq, bias in sets:
                qsb = (rows.load(qr, jq, r) * ATT_SCALE).astype(BF16)
                do2b = rows.load(dor, jq, r).astype(BF16)
                p = jnp.exp(_dot_nt(ks, qsb) + bias - per_query_rows(rows.load(lr, jq, r)))
                out.append((p, _dot_nt(vs, do2b), dlr, jq, qsb, do2b))
            return (j, r), out

        def dscores(unit, sets):
            out = []
            for p, dp, dlr, jq, qsb, do2b in sets:
                ds = (p * (dp - per_query_rows(rows.load(dlr, jq, unit[1])))).astype(BF16)
                out.append((p.astype(BF16), ds, qsb, do2b))
            return unit, out

        def outputs(unit, sets):
            dk_st = dv_st = None
            for pb, ds, qsb, do2b in sets:
                dvs, dks = _dot(pb, do2b), _dot(ds, qsb)
                dv_st = dvs if dv_st is None else dv_st + dvs
                dk_st = dks if dk_st is None else dk_st + dks
            dk2 = _unstack_heads(dk_st, h0)
            dv2 = _unstack_heads(dv_st, h0)
            if has_prev:
                dk2 = dk2 + rows.load(dkp_ref, *unit)
                dv2 = dv2 + rows.load(dvp_ref, *unit)
            rows.store(dk_ref, *unit, dk2)
            rows.store(dv_ref, *unit, dv2)
            return ()

        _in_groups([(j, r) for j in range(nsub) for r in range(dil)], probs, dscores, outputs)

    own = lambda hp: hp
    qcol = lambda hp: 4 * g + hp
    ins = [(kv, tile(lambda hp: 8 * g + hp)), (kv, tile(lambda hp: 8 * g + 4 + hp)), (q, tile(qcol)), (q, nxt(qcol)),
           (do, tile(own)), (do, nxt(own)), (delta, tile(own)), (delta, nxt(own)), (lse, tile(own)), (lse, nxt(own))]
    if has_prev:
        ins += [(prev[0], tile(own)), (prev[1], tile(own))]
    out = (SDS((T, QW), F32), tile(own))
    return _call(name, body, (bl, nt, HEAD_PAIRS), ins, [out, out], hub=hub)


def _final_loss(name, h, tgt, g, tm=MM_ROWS):
    T, D = h.shape
    n_steps = T // tm

    def body(h_ref, t_ref, g_ref, dh_ref, loss_ref, dg_ref, sq_ref):
        i = pl.program_id(0)
        x = h_ref[...]
        rstd = _rstd(x)
        xhat = x * rstd
        err = xhat * g_ref[...] - t_ref[...]
        _acc8(sq_ref, err * err, i, n_steps)
        dy = err * (1.0 / D)
        dxhat = dy * g_ref[...]
        dh_ref[...] = rstd * (dxhat - xhat * jnp.mean(dxhat * xhat, axis=-1, keepdims=True))
        _acc8(dg_ref, dy * xhat, i, n_steps)

        @pl.when(i == n_steps - 1)
        def _():
            loss_ref[...] = jnp.full(loss_ref.shape, jnp.sum(sq_ref[0:1, :]), F32)

    outs = [_out_rows(T, D, F32, tm), (SDS((SUBLANES, 128), F32), pl.BlockSpec((SUBLANES, 128), lambda i: (0, 0))),
            _out_acc8(D)]
    dh, loss, dg = _call(name, body, (n_steps,), [_rows(h, tm), _rows(tgt, tm), _full(g)], outs,
                         scratch=[pltpu.VMEM((SUBLANES, D), F32)])
    return dh, loss[0, 0], dg[0:1]


def _conv_bwd(name, bcu, dgated, cw, seq, tm=MM_ROWS, hub=None):
    T, D = dgated.shape
    n_steps = T // tm

    def body(b_ref, c_ref, u_ref, ch_ref, uh_ref, dg_ref, dgn_ref, bn_ref, cw_ref, o_ref, t0_ref, t1_ref, t2_ref):
        i = pl.program_id(0)
        first = (i * tm) % seq == 0
        last = ((i + 1) * tm) % seq == 0
        _, (b, c, u), conv, (cu, cu1, cu2) = _gate(b_ref, c_ref, u_ref, ch_ref, uh_ref, cw_ref, first)
        dgat = dg_ref[...]
        dconv = dgat * b
        nxt = jnp.where(last, 0.0, dgn_ref[...] * bn_ref[...].astype(F32))
        rows = lax.broadcasted_iota(jnp.int32, dconv.shape, 0)
        n1 = nxt[0:1, :]
        n2 = nxt[1:2, :]
        dc1 = jnp.where(rows == tm - 1, n1, pltpu.roll(dconv, tm - 1, 0))
        dc2 = jnp.where(rows == tm - 1, n2, jnp.where(rows == tm - 2, n1, pltpu.roll(dconv, tm - 2, 0)))
        dcu = cw_ref[0:1, :] * dconv + cw_ref[1:2, :] * dc1 + cw_ref[2:3, :] * dc2
        o_ref[:, 0:D] = (dgat * conv).astype(BF16)
        o_ref[:, D:2 * D] = (dcu * u).astype(BF16)
        o_ref[:, 2 * D:3 * D] = (dcu * c).astype(BF16)
        _acc8(t0_ref, dconv * cu, i, n_steps)
        _acc8(t1_ref, dconv * cu1, i, n_steps)
        _acc8(t2_ref, dconv * cu2, i, n_steps)

    ins = [_rows(bcu, tm, D, 0), _rows(bcu, tm, D, 1), _rows(bcu, tm, D, 2), _prev8(bcu, tm, D, 1), _prev8(bcu, tm, D, 2),
           _rows(dgated, tm), _next8(dgated, tm, D, 0), _next8(bcu, tm, D, 0), _full(cw)]
    outs = [_out_rows(T, 3 * D, BF16, tm), _out_acc8(D), _out_acc8(D), _out_acc8(D)]
    dbcu, t0, t1, t2 = _call(name, body, (n_steps,), ins, outs, hub=hub)
    return dbcu, jnp.concatenate([t0[0:1], t1[0:1], t2[0:1]], axis=0)


def _sum8_adamw(name, parts, w, m, v, tr):
    R, C = w.shape
    b1c = 1.0 - ADAM_B1 ** ADAM_STEP
    b2c = 1.0 - ADAM_B2 ** ADAM_STEP

    def body(p_ref, w_ref, m_ref, v_ref, g_ref, d_ref, nm_ref, nv_ref):
        g = p_ref[0].astype(F32)
        for j in range(1, N_DEV):
            g = g + p_ref[j].astype(F32)
        nm = ADAM_B1 * m_ref[...] + (1.0 - ADAM_B1) * g
        nv = ADAM_B2 * v_ref[...] + (1.0 - ADAM_B2) * (g * g)
        m_hat = nm / b1c
        v_hat = nv / b2c
        g_ref[...] = g
        d_ref[...] = -ADAM_LR * (m_hat / (jnp.sqrt(v_hat) + ADAM_EPS) + ADAM_WD * w_ref[...])
        nm_ref[...] = nm
        nv_ref[...] = nv

    ins = [(parts, pl.BlockSpec((N_DEV, tr, C), lambda i: (0, i, 0))), _rows(w, tr), _rows(m, tr), _rows(v, tr)]
    outs = [_out_rows(R, C, F32, tr)] * 4
    return _call(name, body, (R // tr,), ins, outs)


def _all_gather(name, items):
    n = len(items)
    shapes = [tuple(a.shape if idx is None else a.shape[1:]) for a, idx in items]

    def body(*refs):
        x_refs, out_refs = refs[:n], refs[n:2 * n]
        send_sems, recv_sems, local_sems = refs[2 * n:]
        x, y, c = _mesh_pos()
        me, sibling = (x, y, c), (x, y, 1 - c)
        chips = [(1 - x, y), (x, 1 - y), (1 - x, 1 - y)]

        def copy(t, k, block, to, own=False):
            dst = out_refs[t].at[4 * block[0] + 2 * block[1] + block[2]]
            src = dst
            if own:
                src = x_refs[t] if items[t][1] is None else x_refs[t].at[items[t][1]]
            return pltpu.make_async_remote_copy(
                src_ref=src, dst_ref=dst, send_sem=send_sems.at[t, k], recv_sem=recv_sems.at[t, k],
                device_id=to, device_id_type=pl.DeviceIdType.MESH)

        started = []
        for t in range(n):
            src = x_refs[t] if items[t][1] is None else x_refs[t].at[items[t][1]]
            mine = pltpu.make_async_copy(src, out_refs[t].at[4 * x + 2 * y + c], local_sems.at[t])
            mine.start()
            first = [copy(t, 0, me, sibling, own=True)]
            first += [copy(t, 1 + j, me, (*chip, c), own=True) for j, chip in enumerate(chips)]
            for cp in first:
                cp.start()
            started.append((mine, first))
        passed = []
        for t in range(n):
            for j, chip in enumerate(chips):
                copy(t, 1 + j, (*chip, c), me).wait_recv()
                fwd = copy(t, 4 + j, (*chip, c), sibling)
                fwd.start()
                passed.append(fwd)
        for t in range(n):
            copy(t, 0, sibling, me).wait_recv()
            for j, chip in enumerate(chips):
                copy(t, 4 + j, (*chip, 1 - c), me).wait_recv()
        for mine, first in started:
            for cp in first:
                cp.wait_send()
            mine.wait()
        for cp in passed:
            cp.wait_send()

    any_spec = pl.BlockSpec(memory_space=pl.ANY)
    return pl.pallas_call(
        body, name=name,
        out_shape=[SDS((N_DEV,) + s, a.dtype) for s, (a, _) in zip(shapes, items)],
        in_specs=[any_spec] * n,
        out_specs=[any_spec] * n,
        scratch_shapes=[pltpu.SemaphoreType.DMA((n, 7)), pltpu.SemaphoreType.DMA((n, 7)), pltpu.SemaphoreType.DMA((n,))],
    )(*[a for a, _ in items])


def _pad8(t):
    return jnp.pad(t, ((0, SUBLANES - t.shape[0]), (0, 0)))


def _rows_merged(w):
    return w.reshape(w.shape[0] * w.shape[1], w.shape[2])


def _local_grads(x, tgt, norm_mix, norm_mlp, norm_kv, norm_final, conv_w, hub):
    bl, seq, D = x.shape
    T = bl * seq
    h = x.reshape(T, D)
    tgt = tgt.reshape(T, D)
    row = lambda t, l: t[l:l + 1]
    W = hub.weights
    saved = []
    kv = h_kv = hn_kv_t = None
    for l in range(DEPTH):
        if l < N_A_LAYERS:
            bcu, hn_t = _norm_mm(f"l{l}_in", h, row(norm_mix, l), W["w_a_in", l], out_dtype=BF16, transposed=True, hub=hub)
            h2, gated_t = _gate_mm_res(f"l{l}_conv_out", bcu, _pad8(conv_w[l]), _rows_merged(W["w_a_out", l]), h, seq, hub=hub)
            saved.append((h, bcu, gated_t, hn_t))
        else:
            i = l - N_A_LAYERS
            if l == N_A_LAYERS:
                h_kv = h
                kv, hn_kv_t = _norm_mm("kv", h, norm_kv.reshape(1, D), W["w_kv", None], transposed=True, hub=hub)
            q, hn_t = _norm_mm(f"l{l}_q", h, row(norm_mix, l), W["w_q", i], transposed=True)
            per_group = [_attn_fwd(f"l{l}_att{g}", q, kv, g, bl, hub=hub) for g in range(N_GROUPS)]
            o, lse, h2 = _combine_mm_res(f"l{l}_att_out", [p[0] for p in per_group], [p[1] for p in per_group],
                                         W["w_o", i], h)
            saved.append((h, q, o, lse, hn_t))
        a = _norm_mm(f"l{l}_up", h2, row(norm_mlp, l), W["w_up", l], out_dtype=BF16, hub=hub)
        h = _relu2_mm_res(f"l{l}_down", a, _rows_merged(W["w_down", l]), h2, hub=hub)
        saved[-1] = saved[-1] + (h2, a)

    dh, sq_err, d_norm_final = _final_loss("loss", h, tgt, norm_final.reshape(1, D))

    d_norm_mix = [None] * DEPTH
    d_norm_mlp = [None] * DEPTH
    d_conv = [None] * N_A_LAYERS
    d_norm_kv = None
    dkv_acc = [None] * N_GROUPS
    G = hub.grads
    as_slots = lambda g: g.reshape(N_DEV, g.shape[0] // N_DEV, g.shape[1])
    tt = DW_TOKENS
    for l in reversed(range(DEPTH)):
        h2, a = saved[l][-2:]
        h_in = saved[l][0]
        g_mlp = row(norm_mlp, l)
        g_mix = row(norm_mix, l)
        w_up_l = W["w_up", l]
        FF = N_DEV * w_up_l.shape[2]
        da = _nt_relu2_bwd(f"l{l}_down_bwd", dh, _rows_merged(W["w_down", l]), a, hub=hub)
        G["w_down", l] = as_slots(_tn(f"l{l}_dw_down", [_rows2(a, 2 * tt, FF // 4, lambda s: s)], _relu2,
                                      [_rows2(dh, 2 * tt)], _val, FF, D, T, 2 * tt, split=("k", 4)))
        G["w_up", l] = _tn(f"l{l}_dw_up", [_rows2(h2, 2 * tt), _full2(g_mlp)], _normed,
                           [_rows2(da, 2 * tt, FF // 4, lambda s: s)], _val, D, FF, T, 2 * tt, split=("n", 4),
                           out_cols=w_up_l.shape[2], hub=hub)
        dh2, d_norm_mlp[l] = _nt_norm_bwd(f"l{l}_up_bwd", [da], w_up_l, h2, g_mlp, dh, hub=hub)
        if l >= N_A_LAYERS:
            i = l - N_A_LAYERS
            _, q, o, lse, hn_t = saved[l][:5]
            w_o_i, w_q_i = W["w_o", i], W["w_q", i]
            do, delta = _att_out_bwd(f"l{l}_att_out_bwd", dh2, w_o_i, o)
            G["w_o", i] = _tn(f"l{l}_dw_o", [_rows2(o, tt)], _val, [_rows2(dh2, tt)], _val, QW, D, T, tt,
                              out_cols=w_o_i.shape[2])
            dqs = []
            for g in range(N_GROUPS):
                dqs.append(_attn_bwd_dq(f"l{l}_att{g}_dq", q, kv, do, delta, lse, g, bl, hub=hub))
                dkv_acc[g] = _attn_bwd_dkv(f"l{l}_att{g}_dkv", q, kv, do, delta, lse, g, bl, prev=dkv_acc[g], hub=hub)
            G["w_q", i] = _tn(f"l{l}_dw_q", [_cols2(hn_t, tt)], None,
                              [_rows2(t, tt) for t in dqs], _concat_f32, D, N_GROUPS * QW, T, tt, out_cols=w_q_i.shape[2])
            dh, d_norm_mix[l] = _nt_norm_bwd(f"l{l}_q_bwd", dqs, w_q_i, h_in, g_mix, dh2, hub=hub)
            if l == N_A_LAYERS:
                dkvs = [t for pair in dkv_acc for t in pair]
                g_kv = norm_kv.reshape(1, D)
                w_kv = W["w_kv", None]
                per_call = len(dkvs) // 2
                halves = [_tn(f"dw_kv{p}", [_cols2(hn_kv_t, tt)], None,
                              [_rows2(t, tt) for t in dkvs[p * per_call:(p + 1) * per_call]], _concat_f32,
                              D, per_call * QW, T, tt, out_cols=w_kv.shape[2]) for p in range(2)]
                G["w_kv", None] = jnp.concatenate(halves, axis=0)
                dh, d_norm_kv = _nt_norm_bwd("kv_bwd", dkvs, w_kv, h_kv, g_kv, dh, tm=MM_ROWS // 2, hub=hub)
        else:
            _, bcu, gated_t, hn_t = saved[l][:4]
            cw = _pad8(conv_w[l])
            w_in_l = W["w_a_in", l]
            dgated = _nt_plain(f"l{l}_conv_out_bwd", dh2, _rows_merged(W["w_a_out", l]))
            G["w_a_out", l] = as_slots(_tn(f"l{l}_dw_conv_out", [_cols2(gated_t, 2 * tt)], None,
                                           [_rows2(dh2, 2 * tt)], _val, D, D, T, 2 * tt, hub=hub))
            dbcu, d_conv[l] = _conv_bwd(f"l{l}_conv_bwd", bcu, dgated, cw, seq, hub=hub)
            G["w_a_in", l] = _tn(f"l{l}_dw_in", [_cols2(hn_t, 2 * tt)], None,
                                 [_rows2(dbcu, 2 * tt, 3 * D // 2, lambda s: s)], _val, D, 3 * D, T, 2 * tt, split=("n", 2),
                                 out_cols=w_in_l.shape[2], hub=hub)
            dh, d_norm_mix[l] = _nt_norm_bwd(f"l{l}_in_bwd", [dbcu], w_in_l, h_in, g_mix, dh2, hub=hub)

    small = jnp.concatenate(d_norm_mix + d_norm_mlp + [d_norm_kv, d_norm_final] + d_conv, axis=0)
    return sq_err, dh.reshape(bl, seq, D), small


def kernel(x, norm_mix, norm_mlp, w_a_in, conv_w, w_a_out, norm_kv, w_kv, w_q, w_o, w_up, w_down, norm_final, loss_target, m_norm_mix, m_norm_mlp, m_w_a_in, m_conv_w, m_w_a_out, m_norm_kv, m_w_kv, m_w_q, m_w_o, m_w_up, m_w_down, m_norm_final, v_norm_mix, v_norm_mlp, v_w_a_in, v_conv_w, v_w_a_out, v_norm_kv, v_w_kv, v_w_q, v_w_o, v_w_up, v_w_down, v_norm_final):
    D = x.shape[-1]
    xi, yi, ci = _mesh_pos()
    me_idx = 4 * xi + 2 * yi + ci
    w_big = dict(w_a_in=w_a_in, w_a_out=w_a_out, w_kv=w_kv, w_q=w_q, w_o=w_o, w_up=w_up, w_down=w_down)
    m_big = dict(w_a_in=m_w_a_in, w_a_out=m_w_a_out, w_kv=m_w_kv, w_q=m_w_q, w_o=m_w_o, w_up=m_w_up, w_down=m_w_down)
    v_big = dict(w_a_in=v_w_a_in, w_a_out=v_w_a_out, w_kv=v_w_kv, w_q=v_w_q, w_o=v_w_o, w_up=v_w_up, w_down=v_w_down)
    names = list(w_big)

    shards = {n: w.astype(BF16) for n, w in w_big.items()}
    landing = {n: lax.empty((N_DEV,) + w.shape, BF16) for n, w in w_big.items()}
    hub = _Hub(FETCH_DURING, PUSH_DURING, shards, landing)
    dc = conv_w.shape[-1]
    taps = conv_w.shape[0] * conv_w.shape[1]
    got = _all_gather("gather_first", [(shards[n], l) for n, l in FETCH_UP_FRONT] + [(_pad8(conv_w.reshape(taps, dc)), None)])
    for key, w in zip(FETCH_UP_FRONT, got):
        hub.weights[key] = w
    conv_full = jnp.moveaxis(got[-1][:, :taps], 0, 1).reshape(conv_w.shape[0], conv_w.shape[1], N_DEV * dc)

    sq_err, grad_x, small = _local_grads(x, loss_target, norm_mix, norm_mlp, norm_kv, norm_final, conv_full, hub)

    grads, deltas, new_m, new_v = {}, {}, {}, {}
    for n in names:
        shape = w_big[n].shape
        cols = shape[-1]
        flat = lambda t: t.reshape(-1, cols)
        parts = hub.landing[n].reshape(N_DEV, -1, cols)
        tr = parts.shape[1]
        while tr * cols > ADAMW_TILE and tr % 32 == 0:
            tr //= 2
        outs = _sum8_adamw(f"adamw_{n}", parts, flat(w_big[n]), flat(m_big[n]), flat(v_big[n]), tr=tr)
        grads[n], deltas[n], new_m[n], new_v[n] = (t.reshape(shape) for t in outs)

    n_gain = 2 * DEPTH + 2
    n_small = small.shape[0]
    small = jnp.concatenate([small, jnp.full((SUBLANES, D), sq_err, F32)], axis=0)
    rows_small = small.shape[0]
    small_all = _all_gather("gather_small_grads", [(small, None)])[0]

    def small_pack(nm, nl, nk, nf, cw):
        gains = jnp.concatenate([nm, nl, nk.reshape(1, D), nf.reshape(1, D)], axis=0)
        taps_full = lax.dynamic_update_slice(jnp.zeros((taps, D), F32), cw.reshape(taps, dc), (0, me_idx * dc))
        return jnp.concatenate([gains, taps_full, jnp.zeros((SUBLANES, D), F32)], axis=0)

    sp = [small_pack(*t) for t in ((norm_mix, norm_mlp, norm_kv, norm_final, conv_w),
                                   (m_norm_mix, m_norm_mlp, m_norm_kv, m_norm_final, m_conv_w),
                                   (v_norm_mix, v_norm_mlp, v_norm_kv, v_norm_final, v_conv_w))]
    small_out = _sum8_adamw("adamw_small", small_all, *sp, tr=rows_small)
    loss = small_out[0][n_small, 0] * (0.5 / D)

    def small_unpack(t):
        res = dict(norm_mix=t[0:DEPTH], norm_mlp=t[DEPTH:2 * DEPTH], norm_kv=t[2 * DEPTH], norm_final=t[2 * DEPTH + 1])
        res["conv_w"] = lax.dynamic_slice(t[n_gain:], (0, me_idx * dc), (taps, dc)).reshape(conv_w.shape)
        return res

    for dst, t in zip((grads, deltas, new_m, new_v), small_out):
        dst.update(small_unpack(t))

    order = ["norm_mix", "norm_mlp", "w_a_in", "conv_w", "w_a_out", "norm_kv", "w_kv", "w_q", "w_o", "w_up", "w_down",
             "norm_final"]
    return (loss, grad_x, *[grads[n] for n in order], *[deltas[n] for n in order], *[new_m[n] for n in order],
            *[new_v[n] for n in order])
```

```python
import jax
import jax.numpy as jnp
from jax import lax
from jax.experimental import pallas as pl
from jax.experimental.pallas import tpu as pltpu

F32 = jnp.float32
BF16 = jnp.bfloat16
SDS = jax.ShapeDtypeStruct

EPS = 1e-5
N_A_LAYERS = 2
DEPTH = 4
PATTERNS = ((128, 1), (512, 4), (2048, 16))
N_GROUPS = 3
H_G = 8
HEAD_DIM = 64
QW = H_G * HEAD_DIM
ATT_BLK = 128
ALIBI_MAX_BIAS = 8.0
NEG_INF = -1e30

ADAM_LR = 0.001
ADAM_B1 = 0.9
ADAM_B2 = 0.999
ADAM_EPS = 1e-08
ADAM_WD = 0.01
ADAM_STEP = 10

N_DEV = 8
SUBLANES = 8
HALO = 16
V7X_VMEM_LIMIT = 48 * 1024 * 1024
MXU_COLS = 256
MM_CHUNK = 512
MM_ROWS = 512
ADAMW_TILE = 256 * 1024
DW_TOKENS = 1024

FETCH_UP_FRONT = [("w_a_in", 0)]
FETCH_DURING = {
    "l0_in": [("w_a_out", 0), ("w_up", 0)], "l0_conv_out": [("w_down", 0)], "l0_up": [("w_a_in", 1), ("w_a_out", 1)], "l0_down": [("w_up", 1)],
    "l1_in": [("w_down", 1)], "l1_conv_out": [("w_kv", None)],
    "l1_up": [("w_q", 0), ("w_o", 0), ("w_q", 1), ("w_o", 1)], "l1_down": [("w_up", 2)],
    "kv": [("w_down", 2)], "l2_up": [("w_up", 3)], "l2_down": [("w_down", 3)],
}
PUSH_DURING = {
    "l3_dw_up": [("w_down", 3, 0, 2)], "l3_up_bwd": [("w_down", 3, 1, 2)],
    "l3_att0_dq": [("w_up", 3, 0, 2)], "l3_att0_dkv": [("w_up", 3, 1, 2)], "l3_att1_dq": [("w_o", 1)], "l3_q_bwd": [("w_q", 1)],
    "l2_dw_up": [("w_down", 2, 0, 2)], "l2_up_bwd": [("w_down", 2, 1, 2)],
    "l2_att0_dq": [("w_up", 2, 0, 2)], "l2_att0_dkv": [("w_up", 2, 1, 2)], "l2_att1_dq": [("w_o", 0)], "l2_q_bwd": [("w_q", 0)],
    "kv_bwd": [("w_kv", None, 0, 2)], "l1_down_bwd": [("w_kv", None, 1, 2)],
    "l1_dw_up": [("w_down", 1, 0, 2)], "l1_up_bwd": [("w_down", 1, 1, 2)], "l1_conv_bwd": [("w_up", 1, 0, 2)],
    "l1_dw_in": [("w_up", 1, 1, 2), ("w_a_out", 1)], "l1_in_bwd": [("w_a_in", 1, 0, 2)], "l0_down_bwd": [("w_a_in", 1, 1, 2)],
    "l0_dw_up": [("w_down", 0, 0, 2)], "l0_up_bwd": [("w_down", 0, 1, 2)], "l0_conv_bwd": [("w_up", 0, 0, 2)],
    "l0_dw_in": [("w_up", 0, 1, 2), ("w_a_out", 0)], "l0_in_bwd": [("w_a_in", 0)],
}


def _mesh_pos():
    return lax.axis_index("x"), lax.axis_index("y"), lax.axis_index("c")


def _flip(v, bit):
    return 1 - v if bit else v


class _Transfer:
    def __init__(self, kind, key, src, src_idx=None, dst=None, dst_idx=None, dst_shape=None, rows=None):
        self.kind, self.key, self.src, self.src_idx = kind, key, src, src_idx
        self.dst, self.dst_idx, self.dst_shape, self.rows = dst, dst_idx, dst_shape, rows

    def copies(self, src_ref, dst_ref, send_sems, recv_sems, local_sem):
        x, y, c = _mesh_pos()
        me = 4 * x + 2 * y + c
        part = (lambda r: r) if self.rows is None else (lambda r: r.at[pl.ds(*self.rows)])

        def dst_slot(j):
            r = dst_ref.at[j]
            return part(r if self.dst_idx is None else r.at[self.dst_idx])

        def copy(k, src, dst_j, to):
            return pltpu.make_async_remote_copy(
                src_ref=src, dst_ref=dst_slot(dst_j), send_sem=send_sems.at[k], recv_sem=recv_sems.at[k],
                device_id=to, device_id_type=pl.DeviceIdType.MESH)

        if self.kind == "exchange":
            local = pltpu.make_async_copy(part(src_ref.at[me]), dst_slot(me), local_sem)
            sends, arrivals = [], []
            for k in range(1, N_DEV):
                peer = (_flip(x, k & 4), _flip(y, k & 2), _flip(c, k & 1))
                peer_idx = 4 * peer[0] + 2 * peer[1] + peer[2]
                sends.append(copy(k - 1, part(src_ref.at[peer_idx]), me, peer))
                arrivals.append(copy(k - 1, part(src_ref.at[peer_idx]), peer_idx, peer))
            return local, sends, [], arrivals

        own = part(src_ref if self.src_idx is None else src_ref.at[self.src_idx])
        idx = lambda px, py, pc: 4 * px + 2 * py + pc
        sibling = (x, y, 1 - c)
        chips = [(1 - x, y), (x, 1 - y), (1 - x, 1 - y)]
        local = pltpu.make_async_copy(own, dst_slot(me), local_sem)
        sends = [copy(0, own, me, sibling)] + [copy(1 + j, own, me, (*chip, c)) for j, chip in enumerate(chips)]
        relays = [(copy(1 + j, own, idx(*chip, c), sibling), copy(4 + j, dst_slot(idx(*chip, c)), idx(*chip, c), sibling))
                  for j, chip in enumerate(chips)]
        arrivals = [copy(0, own, idx(*sibling), sibling)]
        arrivals += [copy(4 + j, own, idx(*chip, 1 - c), sibling) for j, chip in enumerate(chips)]
        return local, sends, relays, arrivals


class _Hub:
    def __init__(self, fetch, push, shards, landing):
        self.fetch, self.push, self.shards, self.landing = fetch, push, shards, landing
        self.weights = {}
        self.arriving = {}
        self.grads = {}

    def transfers(self, host):
        out = []
        for name, l, *part in self.fetch.get(host, ()):
            src = self.shards[name]
            shard = tuple(src.shape if l is None else src.shape[1:])
            p, n = part or (0, 1)
            rows = None if n == 1 else (p * (shard[0] // n), shard[0] // n)
            out.append(_Transfer("gather", (name, l, p == n - 1), src, src_idx=l, dst=self.arriving.get((name, l)),
                                 dst_shape=(N_DEV,) + shard, rows=rows))
        for name, l, *part in self.push.get(host, ()):
            src = self.grads[name, l]
            p, n = part or (0, 1)
            rows = None if n == 1 else (p * (src.shape[1] // n), src.shape[1] // n)
            out.append(_Transfer("exchange", (name, l, p == n - 1), src, dst=self.landing[name], dst_idx=l, rows=rows))
        return out

    def accept(self, transfers, results):
        for t, r in zip(transfers, results):
            name, l, complete = t.key
            if t.kind == "exchange":
                self.landing[name] = r
            elif complete:
                self.weights[name, l] = r
            else:
                self.arriving[name, l] = r


def _call(name, body, grid, ins, outs, scratch=(), hub=None):
    transfers = hub.transfers(name) if hub is not None else []
    n_in, n_out, n_scr, n_tr = len(ins), len(outs), len(scratch), len(transfers)
    c_in, c_out, aliases, places = [], [], {}, []
    for t in transfers:
        c_in.append(t.src)
        src_pos = len(c_in) - 1
        if t.dst is not None:
            c_in.append(t.dst)
            aliases[n_in + len(c_in) - 1] = n_out + len(c_out)
            c_out.append(SDS(t.dst.shape, t.dst.dtype))
        else:
            c_out.append(SDS(t.dst_shape, t.src.dtype))
        places.append((src_pos, len(c_out) - 1))
    sems = [pltpu.SemaphoreType.DMA((n_tr, N_DEV - 1)), pltpu.SemaphoreType.DMA((n_tr, N_DEV - 1)),
            pltpu.SemaphoreType.DMA((n_tr,))] if n_tr else []

    def wrapped(*refs):
        in_refs = refs[:n_in]
        cin_refs = refs[n_in:n_in + len(c_in)]
        o0 = n_in + len(c_in)
        out_refs = refs[o0:o0 + n_out]
        cout_refs = refs[o0 + n_out:o0 + n_out + len(c_out)]
        s0 = o0 + n_out + len(c_out)
        scr_refs = refs[s0:s0 + n_scr]
        if n_tr:
            send_sems, recv_sems, local_sems = refs[s0 + n_scr:]
            first = last = relay = None
            for ax, n in enumerate(grid):
                i = pl.program_id(ax)
                at_relay = (i == max(n - 2, 0)) if ax == len(grid) - 1 else (i == n - 1)
                first = (i == 0) if first is None else first & (i == 0)
                last = (i == n - 1) if last is None else last & (i == n - 1)
                relay = at_relay if relay is None else relay & at_relay

            def all_copies():
                return [t.copies(cin_refs[sp], cout_refs[dp], send_sems.at[n], recv_sems.at[n], local_sems.at[n])
                        for n, (t, (sp, dp)) in enumerate(zip(transfers, places))]

            @pl.when(first)
            def _():
                for local, sends, _, _ in all_copies():
                    local.start()
                    for cp in sends:
                        cp.start()

            def pass_on():
                @pl.when(relay)
                def _():
                    for _, _, relays, _ in all_copies():
                        for arrival, onward in relays:
                            arrival.wait_recv()
                            onward.start()

            if grid[-1] > 1:
                pass_on()

        body(*in_refs, *out_refs, *scr_refs)

        if n_tr:
            if grid[-1] == 1:
                pass_on()

            @pl.when(last)
            def _():
                for local, sends, relays, arrivals in all_copies():
                    for cp in arrivals:
                        cp.wait_recv()
                    for cp in sends + [onward for _, onward in relays]:
                        cp.wait_send()
                    local.wait()

    any_spec = pl.BlockSpec(memory_space=pl.ANY)
    res = pl.pallas_call(
        wrapped,
        name=name,
        grid=grid,
        in_specs=[s for _, s in ins] + [any_spec] * len(c_in),
        out_specs=[s for _, s in outs] + [any_spec] * len(c_out),
        out_shape=[o for o, _ in outs] + c_out,
        scratch_shapes=list(scratch) + sems,
        input_output_aliases=aliases,
        compiler_params=pltpu.CompilerParams(
            dimension_semantics=("arbitrary",) * len(grid), vmem_limit_bytes=V7X_VMEM_LIMIT),
    )(*[a for a, _ in ins], *c_in)
    if n_tr:
        hub.accept(transfers, res[n_out:])
    return res[:n_out]


def _rows(a, tm, cb=None, col=0):
    cb = cb or a.shape[1]
    return (a, pl.BlockSpec((tm, cb), lambda i: (i, col)))


def _full(a):
    nd = a.ndim
    return (a, pl.BlockSpec(a.shape, lambda i: (0,) * nd))


def _prev8(a, tm, cb, col):
    return (a, pl.BlockSpec((HALO, cb), lambda i: (jnp.maximum(i * (tm // HALO) - 1, 0), col)))


def _next8(a, tm, cb, col):
    last = a.shape[0] // HALO - 1
    return (a, pl.BlockSpec((HALO, cb), lambda i: (jnp.minimum((i + 1) * (tm // HALO), last), col)))


def _rows2(a, tt, cb=None, colfn=None):
    cb = cb or a.shape[1]
    colfn = colfn or (lambda s: 0)
    return (a, pl.BlockSpec((tt, cb), lambda s, t: (t, colfn(s))))


def _full2(a):
    nd = a.ndim
    return (a, pl.BlockSpec(a.shape, lambda s, t: (0,) * nd))


def _out_rows(T, n, dtype, tm):
    return (SDS((T, n), dtype), pl.BlockSpec((tm, n), lambda i: (i, 0)))


def _out_acc8(d):
    return (SDS((SUBLANES, d), F32), pl.BlockSpec((SUBLANES, d), lambda i: (0, 0)))


def _rstd(x):
    return lax.rsqrt(jnp.mean(x * x, axis=-1, keepdims=True) + EPS)


def _normed(h_ref, g_ref):
    x = h_ref[...]
    return x * _rstd(x) * g_ref[...]


def _acc8(ref, val, i, n):
    part = val.reshape(-1, SUBLANES, val.shape[-1]).sum(axis=0)

    @pl.when(i == 0)
    def _():
        ref[...] = part

    @pl.when(i > 0)
    def _():
        ref[...] += part

    @pl.when(i == n - 1)
    def _():
        ref[...] = jnp.broadcast_to(jnp.sum(ref[...], axis=0, keepdims=True), ref.shape)


def _gate(b_ref, c_ref, u_ref, ch_ref, uh_ref, cw_ref, first):
    b, c, u = (r[...].astype(F32) for r in (b_ref, c_ref, u_ref))
    cu = c * u
    halo = jnp.where(first, 0.0, ch_ref[...].astype(F32) * uh_ref[...].astype(F32))
    rows = lax.broadcasted_iota(jnp.int32, cu.shape, 0)
    h1 = halo[HALO - 1:HALO, :]
    h2 = halo[HALO - 2:HALO - 1, :]
    cu1 = jnp.where(rows == 0, h1, pltpu.roll(cu, 1, 0))
    cu2 = jnp.where(rows == 0, h2, jnp.where(rows == 1, h1, pltpu.roll(cu, 2, 0)))
    conv = cw_ref[0:1, :] * cu + cw_ref[1:2, :] * cu1 + cw_ref[2:3, :] * cu2
    return b * conv, (b, c, u), conv, (cu, cu1, cu2)


def _relu2(a_ref):
    r = jnp.maximum(a_ref[...].astype(F32), 0.0)
    return r * r


def _dot(a, b):
    return jnp.dot(a, b, preferred_element_type=F32)


def _dot_nt(a, b):
    return lax.dot_general(a, b, (((1,), (1,)), ((), ())), preferred_element_type=F32)


def _chunks(n):
    c = min(MM_CHUNK, n)
    while n % c:
        c -= 128
    assert c > 0, n
    return [(k * c, (k + 1) * c) for k in range(n // c)]


def _col_weight(w):
    _, K, ns = w.shape
    N = N_DEV * ns
    direct = ns % MXU_COLS == 0
    scratch = [] if direct else [pltpu.VMEM((K, N), BF16)]

    def prepare(w_ref, s_ref, step):
        if direct:
            return

        @pl.when(step == 0)
        def _():
            for j in range(N_DEV):
                s_ref[:, j * ns:(j + 1) * ns] = w_ref[j]

    def chunks(w_ref, s_ref):
        if direct:
            return [(j * ns, (j + 1) * ns, (lambda j=j: w_ref[j])) for j in range(N_DEV)]
        return [(lo, hi, (lambda lo=lo, hi=hi: s_ref[:, lo:hi])) for lo, hi in _chunks(N)]

    return N, scratch, prepare, chunks


def _out_cols(n, T, tm):
    return (SDS((n, T), BF16), pl.BlockSpec((n, tm), lambda i: (0, i)))


def _norm_mm(name, h, g, w, tm=MM_ROWS, out_dtype=F32, transposed=False, hub=None):
    T, D = h.shape
    N, w_scratch, prepare, chunks = _col_weight(w)

    def body(h_ref, g_ref, w_ref, o_ref, *rest):
        at_ref, s = (rest[0], rest[1:]) if transposed else (None, rest)
        s_ref = s[0] if s else None
        prepare(w_ref, s_ref, pl.program_id(0))
        a32 = _normed(h_ref, g_ref)
        a = a32.astype(BF16)
        for lo, hi, load in chunks(w_ref, s_ref):
            o_ref[:, lo:hi] = _dot(a, load()).astype(out_dtype)
        if transposed:
            at_ref[...] = a32.T.astype(BF16)

    outs = [_out_rows(T, N, out_dtype, tm)] + ([_out_cols(D, T, tm)] if transposed else [])
    res = _call(name, body, (T // tm,), [_rows(h, tm), _full(g), _full(w)], outs, scratch=w_scratch, hub=hub)
    return res if transposed else res[0]


def _gate_mm_res(name, bcu, cw, w, h, seq, tm=MM_ROWS, hub=None):
    T, D = h.shape

    def body(b_ref, c_ref, u_ref, ch_ref, uh_ref, cw_ref, w_ref, h_ref, o_ref, gt_ref):
        first = (pl.program_id(0) * tm) % seq == 0
        gated32 = _gate(b_ref, c_ref, u_ref, ch_ref, uh_ref, cw_ref, first)[0]
        gated = gated32.astype(BF16)
        for lo, hi in _chunks(D):
            o_ref[:, lo:hi] = h_ref[:, lo:hi] + _dot(gated, w_ref[:, lo:hi])
        gt_ref[...] = gated32.T.astype(BF16)

    ins = [_rows(bcu, tm, D, 0), _rows(bcu, tm, D, 1), _rows(bcu, tm, D, 2), _prev8(bcu, tm, D, 1),
           _prev8(bcu, tm, D, 2), _full(cw), _full(w), _rows(h, tm)]
    return _call(name, body, (T // tm,), ins, [_out_rows(T, D, F32, tm), _out_cols(D, T, tm)], hub=hub)


def _relu2_mm_res(name, a, w, h, tm=MM_ROWS, hub=None):
    T, D = h.shape
    K = a.shape[1]

    def body(a_ref, w_ref, h_ref, o_ref, acc_ref):
        for n, (lo, hi) in enumerate(_chunks(K)):
            d = _dot(_relu2(a_ref.at[:, lo:hi]).astype(BF16), w_ref[lo:hi, :])
            if n == 0:
                acc_ref[...] = d
            else:
                acc_ref[...] += d
        o_ref[...] = h_ref[...] + acc_ref[...]

    return _call(name, body, (T // tm,), [_rows(a, tm), _full(w), _rows(h, tm)], [_out_rows(T, D, F32, tm)],
                 scratch=[pltpu.VMEM((tm, D), F32)], hub=hub)[0]


def _combine_mm_res(name, os_, lses, w, h, tm=MM_ROWS):
    T, D = h.shape
    _, w_scratch, prepare, chunks = _col_weight(w)

    def body(o0, o1, o2, l0, l1, l2, w_ref, h_ref, o_ref, lse_ref, out_ref, *s):
        s_ref = s[0] if s else None
        prepare(w_ref, s_ref, pl.program_id(0))
        ls = [l0[...], l1[...], l2[...]]
        mx = jnp.maximum(jnp.maximum(ls[0], ls[1]), ls[2])
        es = [jnp.exp(l - mx) for l in ls]
        den = es[0] + es[1] + es[2]
        o = (es[0] * o0[...] + es[1] * o1[...] + es[2] * o2[...]) / den
        o_ref[...] = o
        lse_ref[...] = mx + jnp.log(den)
        ob = o.astype(BF16)
        for lo, hi, load in chunks(w_ref, s_ref):
            out_ref[:, lo:hi] = h_ref[:, lo:hi] + _dot(ob, load())

    ins = [_rows(t, tm) for t in list(os_) + list(lses)] + [_full(w), _rows(h, tm)]
    outs = [_out_rows(T, QW, F32, tm), _out_rows(T, QW, F32, tm), _out_rows(T, D, F32, tm)]
    return _call(name, body, (T // tm,), ins, outs, scratch=w_scratch)


def _nt_relu2_bwd(name, dh, w, a, tm=MM_ROWS, hub=None):
    T, _ = dh.shape
    K = w.shape[0]

    def body(dh_ref, w_ref, a_ref, o_ref):
        d = dh_ref[...].astype(BF16)
        for lo, hi in _chunks(K):
            dr = _dot_nt(d, w_ref[lo:hi, :])
            o_ref[:, lo:hi] = (dr * (2.0 * jnp.maximum(a_ref[:, lo:hi].astype(F32), 0.0))).astype(BF16)

    return _call(name, body, (T // tm,), [_rows(dh, tm), _full(w), _rows(a, tm)], [_out_rows(T, K, BF16, tm)], hub=hub)[0]


def _concat_bf16(*refs):
    vals = [r[...].astype(BF16) for r in refs]
    return vals[0] if len(vals) == 1 else jnp.concatenate(vals, axis=1)


def _nt_plain(name, dy, w, tm=MM_ROWS):
    T, N = dy.shape
    K = w.shape[0]

    def body(dy_ref, w_ref, o_ref, acc_ref):
        for n, (lo, hi) in enumerate(_chunks(N)):
            d = _dot_nt(dy_ref[:, lo:hi].astype(BF16), w_ref[:, lo:hi])
            if n == 0:
                acc_ref[...] = d
            else:
                acc_ref[...] += d
        o_ref[...] = acc_ref[...]

    return _call(name, body, (T // tm,), [_rows(dy, tm), _full(w)], [_out_rows(T, K, F32, tm)],
                 scratch=[pltpu.VMEM((tm, K), F32)])[0]


def _att_out_bwd(name, dy, w, o, tm=MM_ROWS):
    T, _ = dy.shape
    K = w.shape[1]
    _, w_scratch, prepare, chunks = _col_weight(w)

    def body(dy_ref, w_ref, o_ref, do_ref, dl_ref, acc_ref, *s):
        s_ref = s[0] if s else None
        prepare(w_ref, s_ref, pl.program_id(0))
        for n, (lo, hi, load) in enumerate(chunks(w_ref, s_ref)):
            d = _dot_nt(dy_ref[:, lo:hi].astype(BF16), load())
            if n == 0:
                acc_ref[...] = d
            else:
                acc_ref[...] += d
        do = acc_ref[...]
        do_ref[...] = do
        prod = do * o_ref[...]
        high = prod.astype(BF16)
        low = (prod - high.astype(F32)).astype(BF16)
        head_of = lambda axis: jnp.right_shift(lax.broadcasted_iota(jnp.int32, (K, K), axis), HEAD_DIM.bit_length() - 1)
        same_head = jnp.where(head_of(0) == head_of(1), 1.0, 0.0).astype(BF16)
        dl_ref[...] = _dot(high, same_head) + _dot(low, same_head)

    outs = [_out_rows(T, K, F32, tm), _out_rows(T, K, F32, tm)]
    return _call(name, body, (T // tm,), [_rows(dy, tm), _full(w), _rows(o, tm)], outs,
                 scratch=[pltpu.VMEM((tm, K), F32)] + w_scratch)


def _nt_norm_bwd(name, dys, w, h, g, dh_in, tm=MM_ROWS, hub=None):
    T, D = h.shape
    _, w_scratch, prepare, chunks = _col_weight(w)
    n_steps = T // tm
    n_dy = len(dys)

    def body(*refs):
        dy_refs = refs[:n_dy]
        w_ref, h_ref, g_ref, dhin_ref, o_ref, dg_ref, acc_ref = refs[n_dy:n_dy + 7]
        s_ref = refs[n_dy + 7] if len(refs) > n_dy + 7 else None
        i = pl.program_id(0)
        prepare(w_ref, s_ref, i)
        dy = _concat_bf16(*dy_refs)
        for n, (lo, hi, load) in enumerate(chunks(w_ref, s_ref)):
            d = _dot_nt(dy[:, lo:hi], load())
            if n == 0:
                acc_ref[...] = d
            else:
                acc_ref[...] += d
        dn = acc_ref[...]
        x = h_ref[...]
        rstd = _rstd(x)
        xhat = x * rstd
        dxhat = dn * g_ref[...]
        dx = rstd * (dxhat - xhat * jnp.mean(dxhat * xhat, axis=-1, keepdims=True))
        o_ref[...] = dhin_ref[...] + dx
        _acc8(dg_ref, dn * xhat, i, n_steps)

    ins = [_rows(d, tm) for d in dys] + [_full(w), _rows(h, tm), _full(g), _rows(dh_in, tm)]
    outs = [_out_rows(T, D, F32, tm), _out_acc8(D)]
    dh, dg = _call(name, body, (n_steps,), ins, outs, scratch=[pltpu.VMEM((tm, D), F32)] + w_scratch, hub=hub)
    return dh, dg[0:1]


def _cols2(a_t, tt, kb=None):
    kb = kb or a_t.shape[0]
    return (a_t, pl.BlockSpec((kb, tt), (lambda s, t: (s, t)) if kb != a_t.shape[0] else (lambda s, t: (0, t))))


def _tn(name, a_ins, a_fn, y_ins, y_fn, K, N, T, tt, split=None, out_cols=None, hub=None):
    kind, parts = split or ("n", 1)
    kb, nb = (K // parts, N) if kind == "k" else (K, N // parts)
    n_steps = T // tt
    n_a = len(a_ins)
    n_y = len(y_ins)
    assert out_cols is None or (kind == "n" and nb % out_cols == 0)

    def body(*refs):
        a_refs = refs[:n_a]
        y_refs = refs[n_a:n_a + n_y]
        o_ref, acc_ref = refs[n_a + n_y:]
        t = pl.program_id(1)
        a_t = a_refs[0][...] if a_fn is None else a_fn(*a_refs).T.astype(BF16)
        y = y_fn(*y_refs).astype(BF16)
        for lo, hi in _chunks(nb):
            d = _dot(a_t, y[:, lo:hi])

            @pl.when(t == 0)
            def _():
                acc_ref[:, lo:hi] = d

            @pl.when(t > 0)
            def _():
                acc_ref[:, lo:hi] += d

        @pl.when(t == n_steps - 1)
        def _():
            if out_cols is None:
                o_ref[...] = acc_ref[...].astype(BF16)
            else:
                for j in range(nb // out_cols):
                    o_ref[j] = acc_ref[:, j * out_cols:(j + 1) * out_cols].astype(BF16)

    if out_cols is None:
        out = (SDS((K, N), BF16), pl.BlockSpec((kb, nb), (lambda s, t: (s, 0)) if kind == "k" else (lambda s, t: (0, s))))
    else:
        out = (SDS((N // out_cols, K, out_cols), BF16), pl.BlockSpec((nb // out_cols, K, out_cols), lambda s, t: (s, 0, 0)))
    return _call(name, body, (parts, n_steps), list(a_ins) + list(y_ins), [out],
                 scratch=[pltpu.VMEM((kb, nb), F32)], hub=hub)[0]


def _val(ref):
    return ref[...]


def _concat_f32(*refs):
    vals = [r[...] for r in refs]
    return vals[0] if len(vals) == 1 else jnp.concatenate(vals, axis=1)


ATT_TILE_ROWS = 2048
HEAD_PAIRS = H_G // 2
ATT_SCALE = HEAD_DIM ** -0.5
ATT_UNITS_TOGETHER = 4


def _slope(h):
    return 2.0 ** (-ALIBI_MAX_BIAS * (h + 1) / H_G)


def _att_geom(T, bl, g):
    dil = PATTERNS[g][1]
    sub = ATT_BLK * dil
    nsub = max(1, ATT_TILE_ROWS // sub)
    rows = sub * nsub
    return dil, sub, nsub, rows, T // bl // rows


def _att_specs(T, bl, g):
    _, sub, nsub, rows, nt = _att_geom(T, bl, g)
    last_sub = T // sub - 1
    tile = lambda col: pl.BlockSpec((rows, 128), lambda b, i, hp: (b * nt + i, col(hp)))
    prev = lambda col: pl.BlockSpec((sub, 128), lambda b, i, hp: (jnp.maximum((b * nt + i) * nsub - 1, 0), col(hp)))
    nxt = lambda col: pl.BlockSpec((sub, 128), lambda b, i, hp: (jnp.minimum((b * nt + i + 1) * nsub, last_sub), col(hp)))
    return tile, prev, nxt


def _sub_rows(j, r, dil):
    start = j * ATT_BLK * dil + r
    return pl.ds(start, ATT_BLK, stride=dil) if dil > 1 else pl.ds(start, ATT_BLK)


class _Residues:
    def __init__(self, dil):
        self.dil = dil
        self.whole = dil % SUBLANES == 0
        self.read, self.written = {}, {}

    def _block(self, j):
        return pl.ds(j * ATT_BLK * self.dil, ATT_BLK * self.dil)

    def load(self, ref, j, r):
        if not self.whole:
            return ref[_sub_rows(j, r, self.dil), :]
        if (id(ref), j) not in self.read:
            rows = ref[self._block(j), :]
            self.read[id(ref), j] = jnp.swapaxes(rows.reshape(ATT_BLK, self.dil, rows.shape[-1]), 0, 1)
        return self.read[id(ref), j][r]

    def store(self, ref, j, r, val):
        if not self.whole:
            ref[_sub_rows(j, r, self.dil), :] = val
            return
        got = self.written.setdefault((id(ref), j), {})
        got[r] = val
        if len(got) == self.dil:
            merged = jnp.swapaxes(jnp.stack([got[k] for k in range(self.dil)], axis=0), 0, 1)
            ref[self._block(j), :] = merged.reshape(ATT_BLK * self.dil, val.shape[-1])
            del self.written[id(ref), j]


def _att_consts(hp, dil, keys_first=False):
    h0 = lax.broadcasted_iota(jnp.int32, (ATT_BLK, 128), 1) < HEAD_DIM
    a = lax.broadcasted_iota(jnp.int32, (ATT_BLK, ATT_BLK), 1 if keys_first else 0)
    c = lax.broadcasted_iota(jnp.int32, (ATT_BLK, ATT_BLK), 0 if keys_first else 1)
    dist_p = ((ATT_BLK + a - c) * dil).astype(F32)
    dist_c = ((a - c) * dil).astype(F32)
    bias_p, bias_c = [], []
    for h in range(2):
        slope = jnp.float32(_slope(2 * (HEAD_PAIRS - 1) + h))
        for p in range(HEAD_PAIRS - 2, -1, -1):
            slope = jnp.where(hp == p, jnp.float32(_slope(2 * p + h)), slope)
        bias_p.append(jnp.where(c >= a, -slope * dist_p, NEG_INF))
        bias_c.append(jnp.where(c <= a, -slope * dist_c, NEG_INF))
    return h0, jnp.concatenate(bias_p, axis=0), jnp.concatenate(bias_c, axis=0)


def _stack_heads(x, h0):
    return jnp.concatenate([jnp.where(h0, x, 0.0), jnp.where(h0, 0.0, x)], axis=0).astype(BF16)


def _unstack_heads(x, h0):
    return jnp.where(h0, x[:ATT_BLK], x[ATT_BLK:])


def _stack_cols(x):
    return jnp.concatenate(_head_cols(x), axis=0)


def _head_cols(x):
    return [x[:, 0:1], x[:, HEAD_DIM:HEAD_DIM + 1]]


def _in_groups(units, first_stage, *later_stages):
    for u0 in range(0, len(units), ATT_UNITS_TOGETHER):
        staged = [first_stage(*u) for u in units[u0:u0 + ATT_UNITS_TOGETHER]]
        for stage in later_stages:
            staged = [stage(*s) for s in staged]


def _attn_fwd(name, q, kv, g, bl, hub=None):
    T = q.shape[0]
    dil, _, nsub, _, _ = _att_geom(T, bl, g)
    tile, prev, _ = _att_specs(T, bl, g)

    def body(q_ref, kp_ref, kc_ref, vp_ref, vc_ref, o_ref, lse_ref):
        first = pl.program_id(1) == 0
        h0, bias_p, bias_c = _att_consts(pl.program_id(2), dil)
        bias_first = jnp.where(first, NEG_INF, bias_p)
        rows = _Residues(dil)

        def with_ones(v):
            return [jnp.where(h0, v, 1.0).astype(BF16), jnp.where(h0, 1.0, v).astype(BF16)]

        def scores(j, r):
            if j == 0:
                kp, vp, bp = rows.load(kp_ref, 0, r), rows.load(vp_ref, 0, r), bias_first
            else:
                kp, vp, bp = rows.load(kc_ref, j - 1, r), rows.load(vc_ref, j - 1, r), bias_p
            kp, kc = kp.astype(BF16), rows.load(kc_ref, j, r).astype(BF16)
            qs = _stack_heads(rows.load(q_ref, j, r) * ATT_SCALE, h0)
            sp = _dot_nt(qs, kp) + bp
            sc = _dot_nt(qs, kc) + bias_c
            return (j, r), sp, sc, with_ones(vp), with_ones(rows.load(vc_ref, j, r))

        def weights(unit, sp, sc, vp, vc):
            mx = jnp.max(jnp.maximum(sp, sc), axis=-1, keepdims=True)
            return unit, mx, jnp.exp(sp - mx).astype(BF16), jnp.exp(sc - mx).astype(BF16), vp, vc

        def outputs(unit, mx, ep, ec, vp, vc):
            heads = [slice(h * ATT_BLK, (h + 1) * ATT_BLK) for h in range(2)]
            acc = [_dot(ep[hs], vp[h]) + _dot(ec[hs], vc[h]) for h, hs in enumerate(heads)]
            den = [pltpu.roll(a, HEAD_DIM, 1) for a in acc]
            rows.store(o_ref, *unit, jnp.where(h0, acc[0] / den[0], acc[1] / den[1]))
            rows.store(lse_ref, *unit, jnp.where(h0, mx[heads[0]] + jnp.log(den[0]), mx[heads[1]] + jnp.log(den[1])))
            return ()

        _in_groups([(j, r) for j in range(nsub) for r in range(dil)], scores, weights, outputs)

    ins = [(q, tile(lambda hp: 4 * g + hp)), (kv, prev(lambda hp: 8 * g + hp)), (kv, tile(lambda hp: 8 * g + hp)),
           (kv, prev(lambda hp: 8 * g + 4 + hp)), (kv, tile(lambda hp: 8 * g + 4 + hp))]
    out = (SDS((T, QW), F32), tile(lambda hp: hp))
    _, _, _, _, nt = _att_geom(T, bl, g)
    return _call(name, body, (bl, nt, HEAD_PAIRS), ins, [out, out], hub=hub)


def _attn_bwd_dq(name, q, kv, do, delta, lse, g, bl, hub=None):
    T = q.shape[0]
    dil, _, nsub, _, nt = _att_geom(T, bl, g)
    tile, prev, _ = _att_specs(T, bl, g)

    def body(q_ref, kp_ref, kc_ref, vp_ref, vc_ref, do_ref, dl_ref, lse_ref, dq_ref):
        first = pl.program_id(1) == 0
        h0, bias_p, bias_c = _att_consts(pl.program_id(2), dil)
        bias_first = jnp.where(first, NEG_INF, bias_p)
        rows = _Residues(dil)

        def probs(j, r):
            if j == 0:
                kp, vp, bp = rows.load(kp_ref, 0, r), rows.load(vp_ref, 0, r), bias_first
            else:
                kp, vp, bp = rows.load(kc_ref, j - 1, r), rows.load(vc_ref, j - 1, r), bias_p
            kp, vp = kp.astype(BF16), vp.astype(BF16)
            kc, vc = rows.load(kc_ref, j, r).astype(BF16), rows.load(vc_ref, j, r).astype(BF16)
            qs = _stack_heads(rows.load(q_ref, j, r) * ATT_SCALE, h0)
            dos = _stack_heads(rows.load(do_ref, j, r), h0)
            lse = _stack_cols(rows.load(lse_ref, j, r))
            pp = jnp.exp(_dot_nt(qs, kp) + bp - lse)
            pc = jnp.exp(_dot_nt(qs, kc) + bias_c - lse)
            return (j, r), pp, pc, _dot_nt(dos, vp), _dot_nt(dos, vc), kp, kc

        def dscores(unit, pp, pc, dpp, dpc, kp, kc):
            dl = _stack_cols(rows.load(dl_ref, *unit))
            return unit, (pp * (dpp - dl)).astype(BF16), (pc * (dpc - dl)).astype(BF16), kp, kc

        def outputs(unit, dsp, dsc, kp, kc):
            rows.store(dq_ref, *unit, _unstack_heads(_dot(dsp, kp) + _dot(dsc, kc), h0) * ATT_SCALE)
            return ()

        _in_groups([(j, r) for j in range(nsub) for r in range(dil)], probs, dscores, outputs)

    own = lambda hp: hp
    ins = [(q, tile(lambda hp: 4 * g + hp)), (kv, prev(lambda hp: 8 * g + hp)), (kv, tile(lambda hp: 8 * g + hp)),
           (kv, prev(lambda hp: 8 * g + 4 + hp)), (kv, tile(lambda hp: 8 * g + 4 + hp)),
           (do, tile(own)), (delta, tile(own)), (lse, tile(own))]
    return _call(name, body, (bl, nt, HEAD_PAIRS), ins, [(SDS((T, QW), F32), tile(own))], hub=hub)[0]


def _attn_bwd_dkv(name, q, kv, do, delta, lse, g, bl, prev=None, hub=None):
    T = q.shape[0]
    dil, _, nsub, _, nt = _att_geom(T, bl, g)
    tile, _, nxt = _att_specs(T, bl, g)
    has_prev = prev is not None

    def body(*refs):
        k_ref, v_ref, q_ref, qn_ref, do_ref, don_ref, dl_ref, dln_ref, l_ref, ln_ref = refs[:10]
        rest = refs[10:]
        if has_prev:
            dkp_ref, dvp_ref, dk_ref, dv_ref = rest
        else:
            dk_ref, dv_ref = rest
        last = pl.program_id(1) == nt - 1
        h0, bias_p, bias_c = _att_consts(pl.program_id(2), dil, keys_first=True)
        bias_last = jnp.where(last, NEG_INF, bias_p)
        rows = _Residues(dil)

        def per_query_rows(x):
            xt = x.T
            return jnp.concatenate([jnp.broadcast_to(xt[0:1], (ATT_BLK, ATT_BLK)),
                                    jnp.broadcast_to(xt[HEAD_DIM:HEAD_DIM + 1], (ATT_BLK, ATT_BLK))], axis=0)

        def probs(j, r):
            ks = _stack_heads(rows.load(k_ref, j, r), h0)
            vs = _stack_heads(rows.load(v_ref, j, r), h0)
            sets = [(q_ref, do_ref, dl_ref, l_ref, j, bias_c)]
            if j < nsub - 1:
                sets.append((q_ref, do_ref, dl_ref, l_ref, j + 1, bias_p))
            else:
                sets.append((qn_ref, don_ref, dln_ref, ln_ref, 0, bias_last))
            out = []
            for qr, dor, dlr, lr, jq, bias in sets:
                qsb = (rows.load(qr, jq, r) * ATT_SCALE).astype(BF16)
                do2b = rows.load(dor, jq, r).astype(BF16)
                p = jnp.exp(_dot_nt(ks, qsb) + bias - per_query_rows(rows.load(lr, jq, r)))
                out.append((p, _dot_nt(vs, do2b), dlr, jq, qsb, do2b))
            return (j, r), out

        def dscores(unit, sets):
            out = []
            for p, dp, dlr, jq, qsb, do2b in sets:
                ds = (p * (dp - per_query_rows(rows.load(dlr, jq, unit[1])))).astype(BF16)
                out.append((p.astype(BF16), ds, qsb, do2b))
            return unit, out

        def outputs(unit, sets):
            dk_st = dv_st = None
            for pb, ds, qsb, do2b in sets:
                dvs, dks = _dot(pb, do2b), _dot(ds, qsb)
                dv_st = dvs if dv_st is None else dv_st + dvs
                dk_st = dks if dk_st is None else dk_st + dks
            dk2 = _unstack_heads(dk_st, h0)
            dv2 = _unstack_heads(dv_st, h0)
            if has_prev:
                dk2 = dk2 + rows.load(dkp_ref, *unit)
                dv2 = dv2 + rows.load(dvp_ref, *unit)
            rows.store(dk_ref, *unit, dk2)
            rows.store(dv_ref, *unit, dv2)
            return ()

        _in_groups([(j, r) for j in range(nsub) for r in range(dil)], probs, dscores, outputs)

    own = lambda hp: hp
    qcol = lambda hp: 4 * g + hp
    ins = [(kv, tile(lambda hp: 8 * g + hp)), (kv, tile(lambda hp: 8 * g + 4 + hp)), (q, tile(qcol)), (q, nxt(qcol)),
           (do, tile(own)), (do, nxt(own)), (delta, tile(own)), (delta, nxt(own)), (lse, tile(own)), (lse, nxt(own))]
    if has_prev:
        ins += [(prev[0], tile(own)), (prev[1], tile(own))]
    out = (SDS((T, QW), F32), tile(own))
    return _call(name, body, (bl, nt, HEAD_PAIRS), ins, [out, out], hub=hub)


def _final_loss(name, h, tgt, g, tm=MM_ROWS):
    T, D = h.shape
    n_steps = T // tm

    def body(h_ref, t_ref, g_ref, dh_ref, loss_ref, dg_ref, sq_ref):
        i = pl.program_id(0)
        x = h_ref[...]
        rstd = _rstd(x)
        xhat = x * rstd
        err = xhat * g_ref[...] - t_ref[...]
        _acc8(sq_ref, err * err, i, n_steps)
        dy = err * (1.0 / D)
        dxhat = dy * g_ref[...]
        dh_ref[...] = rstd * (dxhat - xhat * jnp.mean(dxhat * xhat, axis=-1, keepdims=True))
        _acc8(dg_ref, dy * xhat, i, n_steps)

        @pl.when(i == n_steps - 1)
        def _():
            loss_ref[...] = jnp.full(loss_ref.shape, jnp.sum(sq_ref[0:1, :]), F32)

    outs = [_out_rows(T, D, F32, tm), (SDS((SUBLANES, 128), F32), pl.BlockSpec((SUBLANES, 128), lambda i: (0, 0))),
            _out_acc8(D)]
    dh, loss, dg = _call(name, body, (n_steps,), [_rows(h, tm), _rows(tgt, tm), _full(g)], outs,
                         scratch=[pltpu.VMEM((SUBLANES, D), F32)])
    return dh, loss[0, 0], dg[0:1]


def _conv_bwd(name, bcu, dgated, cw, seq, tm=MM_ROWS, hub=None):
    T, D = dgated.shape
    n_steps = T // tm

    def body(b_ref, c_ref, u_ref, ch_ref, uh_ref, dg_ref, dgn_ref, bn_ref, cw_ref, o_ref, t0_ref, t1_ref, t2_ref):
        i = pl.program_id(0)
        first = (i * tm) % seq == 0
        last = ((i + 1) * tm) % seq == 0
        _, (b, c, u), conv, (cu, cu1, cu2) = _gate(b_ref, c_ref, u_ref, ch_ref, uh_ref, cw_ref, first)
        dgat = dg_ref[...]
        dconv = dgat * b
        nxt = jnp.where(last, 0.0, dgn_ref[...] * bn_ref[...].astype(F32))
        rows = lax.broadcasted_iota(jnp.int32, dconv.shape, 0)
        n1 = nxt[0:1, :]
        n2 = nxt[1:2, :]
        dc1 = jnp.where(rows == tm - 1, n1, pltpu.roll(dconv, tm - 1, 0))
        dc2 = jnp.where(rows == tm - 1, n2, jnp.where(rows == tm - 2, n1, pltpu.roll(dconv, tm - 2, 0)))
        dcu = cw_ref[0:1, :] * dconv + cw_ref[1:2, :] * dc1 + cw_ref[2:3, :] * dc2
        o_ref[:, 0:D] = (dgat * conv).astype(BF16)
        o_ref[:, D:2 * D] = (dcu * u).astype(BF16)
        o_ref[:, 2 * D:3 * D] = (dcu * c).astype(BF16)
        _acc8(t0_ref, dconv * cu, i, n_steps)
        _acc8(t1_ref, dconv * cu1, i, n_steps)
        _acc8(t2_ref, dconv * cu2, i, n_steps)

    ins = [_rows(bcu, tm, D, 0), _rows(bcu, tm, D, 1), _rows(bcu, tm, D, 2), _prev8(bcu, tm, D, 1), _prev8(bcu, tm, D, 2),
           _rows(dgated, tm), _next8(dgated, tm, D, 0), _next8(bcu, tm, D, 0), _full(cw)]
    outs = [_out_rows(T, 3 * D, BF16, tm), _out_acc8(D), _out_acc8(D), _out_acc8(D)]
    dbcu, t0, t1, t2 = _call(name, body, (n_steps,), ins, outs, hub=hub)
    return dbcu, jnp.concatenate([t0[0:1], t1[0:1], t2[0:1]], axis=0)


def _sum8_adamw(name, parts, w, m, v, tr):
    R, C = w.shape
    b1c = 1.0 - ADAM_B1 ** ADAM_STEP
    b2c = 1.0 - ADAM_B2 ** ADAM_STEP

    def body(p_ref, w_ref, m_ref, v_ref, g_ref, d_ref, nm_ref, nv_ref):
        g = p_ref[0].astype(F32)
        for j in range(1, N_DEV):
            g = g + p_ref[j].astype(F32)
        nm = ADAM_B1 * m_ref[...] + (1.0 - ADAM_B1) * g
        nv = ADAM_B2 * v_ref[...] + (1.0 - ADAM_B2) * (g * g)
        m_hat = nm / b1c
        v_hat = nv / b2c
        g_ref[...] = g
        d_ref[...] = -ADAM_LR * (m_hat / (jnp.sqrt(v_hat) + ADAM_EPS) + ADAM_WD * w_ref[...])
        nm_ref[...] = nm
        nv_ref[...] = nv

    ins = [(parts, pl.BlockSpec((N_DEV, tr, C), lambda i: (0, i, 0))), _rows(w, tr), _rows(m, tr), _rows(v, tr)]
    outs = [_out_rows(R, C, F32, tr)] * 4
    return _call(name, body, (R // tr,), ins, outs)


def _all_gather(name, items):
    n = len(items)
    shapes = [tuple(a.shape if idx is None else a.shape[1:]) for a, idx in items]

    def body(*refs):
        x_refs, out_refs = refs[:n], refs[n:2 * n]
        send_sems, recv_sems, local_sems = refs[2 * n:]
        x, y, c = _mesh_pos()
        me, sibling = (x, y, c), (x, y, 1 - c)
        chips = [(1 - x, y), (x, 1 - y), (1 - x, 1 - y)]

        def copy(t, k, block, to, own=False):
            dst = out_refs[t].at[4 * block[0] + 2 * block[1] + block[2]]
            src = dst
            if own:
                src = x_refs[t] if items[t][1] is None else x_refs[t].at[items[t][1]]
            return pltpu.make_async_remote_copy(
                src_ref=src, dst_ref=dst, send_sem=send_sems.at[t, k], recv_sem=recv_sems.at[t, k],
                device_id=to, device_id_type=pl.DeviceIdType.MESH)

        started = []
        for t in range(n):
            src = x_refs[t] if items[t][1] is None else x_refs[t].at[items[t][1]]
            mine = pltpu.make_async_copy(src, out_refs[t].at[4 * x + 2 * y + c], local_sems.at[t])
            mine.start()
            first = [copy(t, 0, me, sibling, own=True)]
            first += [copy(t, 1 + j, me, (*chip, c), own=True) for j, chip in enumerate(chips)]
            for cp in first:
                cp.start()
            started.append((mine, first))
        passed = []
        for t in range(n):
            for j, chip in enumerate(chips):
                copy(t, 1 + j, (*chip, c), me).wait_recv()
                fwd = copy(t, 4 + j, (*chip, c), sibling)
                fwd.start()
                passed.append(fwd)
        for t in range(n):
            copy(t, 0, sibling, me).wait_recv()
            for j, chip in enumerate(chips):
                copy(t, 4 + j, (*chip, 1 - c), me).wait_recv()
        for mine, first in started:
            for cp in first:
                cp.wait_send()
            mine.wait()
        for cp in passed:
            cp.wait_send()

    any_spec = pl.BlockSpec(memory_space=pl.ANY)
    return pl.pallas_call(
        body, name=name,
        out_shape=[SDS((N_DEV,) + s, a.dtype) for s, (a, _) in zip(shapes, items)],
        in_specs=[any_spec] * n,
        out_specs=[any_spec] * n,
        scratch_shapes=[pltpu.SemaphoreType.DMA((n, 7)), pltpu.SemaphoreType.DMA((n, 7)), pltpu.SemaphoreType.DMA((n,))],
    )(*[a for a, _ in items])


def _pad8(t):
    return jnp.pad(t, ((0, SUBLANES - t.shape[0]), (0, 0)))


def _rows_merged(w):
    return w.reshape(w.shape[0] * w.shape[1], w.shape[2])


def _local_grads(x, tgt, norm_mix, norm_mlp, norm_kv, norm_final, conv_w, hub):
    bl, seq, D = x.shape
    T = bl * seq
    h = x.reshape(T, D)
    tgt = tgt.reshape(T, D)
    row = lambda t, l: t[l:l + 1]
    W = hub.weights
    saved = []
    kv = h_kv = hn_kv_t = None
    for l in range(DEPTH):
        if l < N_A_LAYERS:
            bcu, hn_t = _norm_mm(f"l{l}_in", h, row(norm_mix, l), W["w_a_in", l], out_dtype=BF16, transposed=True, hub=hub)
            h2, gated_t = _gate_mm_res(f"l{l}_conv_out", bcu, _pad8(conv_w[l]), _rows_merged(W["w_a_out", l]), h, seq, hub=hub)
            saved.append((h, bcu, gated_t, hn_t))
        else:
            i = l - N_A_LAYERS
            if l == N_A_LAYERS:
                h_kv = h
                kv, hn_kv_t = _norm_mm("kv", h, norm_kv.reshape(1, D), W["w_kv", None], transposed=True, hub=hub)
            q, hn_t = _norm_mm(f"l{l}_q", h, row(norm_mix, l), W["w_q", i], transposed=True)
            per_group = [_attn_fwd(f"l{l}_att{g}", q, kv, g, bl, hub=hub) for g in range(N_GROUPS)]
            o, lse, h2 = _combine_mm_res(f"l{l}_att_out", [p[0] for p in per_group], [p[1] for p in per_group],
                                         W["w_o", i], h)
            saved.append((h, q, o, lse, hn_t))
        a = _norm_mm(f"l{l}_up", h2, row(norm_mlp, l), W["w_up", l], out_dtype=BF16, hub=hub)
        h = _relu2_mm_res(f"l{l}_down", a, _rows_merged(W["w_down", l]), h2, hub=hub)
        saved[-1] = saved[-1] + (h2, a)

    dh, sq_err, d_norm_final = _final_loss("loss", h, tgt, norm_final.reshape(1, D))

    d_norm_mix = [None] * DEPTH
    d_norm_mlp = [None] * DEPTH
    d_conv = [None] * N_A_LAYERS
    d_norm_kv = None
    dkv_acc = [None] * N_GROUPS
    G = hub.grads
    as_slots = lambda g: g.reshape(N_DEV, g.shape[0] // N_DEV, g.shape[1])
    tt = DW_TOKENS
    for l in reversed(range(DEPTH)):
        h2, a = saved[l][-2:]
        h_in = saved[l][0]
        g_mlp = row(norm_mlp, l)
        g_mix = row(norm_mix, l)
        w_up_l = W["w_up", l]
        FF = N_DEV * w_up_l.shape[2]
        da = _nt_relu2_bwd(f"l{l}_down_bwd", dh, _rows_merged(W["w_down", l]), a, hub=hub)
        G["w_down", l] = as_slots(_tn(f"l{l}_dw_down", [_rows2(a, 2 * tt, FF // 4, lambda s: s)], _relu2,
                                      [_rows2(dh, 2 * tt)], _val, FF, D, T, 2 * tt, split=("k", 4)))
        G["w_up", l] = _tn(f"l{l}_dw_up", [_rows2(h2, 2 * tt), _full2(g_mlp)], _normed,
                           [_rows2(da, 2 * tt, FF // 4, lambda s: s)], _val, D, FF, T, 2 * tt, split=("n", 4),
                           out_cols=w_up_l.shape[2], hub=hub)
        dh2, d_norm_mlp[l] = _nt_norm_bwd(f"l{l}_up_bwd", [da], w_up_l, h2, g_mlp, dh, hub=hub)
        if l >= N_A_LAYERS:
            i = l - N_A_LAYERS
            _, q, o, lse, hn_t = saved[l][:5]
            w_o_i, w_q_i = W["w_o", i], W["w_q", i]
            do, delta = _att_out_bwd(f"l{l}_att_out_bwd", dh2, w_o_i, o)
            G["w_o", i] = _tn(f"l{l}_dw_o", [_rows2(o, tt)], _val, [_rows2(dh2, tt)], _val, QW, D, T, tt,
                              out_cols=w_o_i.shape[2])
            dqs = []
            for g in range(N_GROUPS):
                dqs.append(_attn_bwd_dq(f"l{l}_att{g}_dq", q, kv, do, delta, lse, g, bl, hub=hub))
                dkv_acc[g] = _attn_bwd_dkv(f"l{l}_att{g}_dkv", q, kv, do, delta, lse, g, bl, prev=dkv_acc[g], hub=hub)
            G["w_q", i] = _tn(f"l{l}_dw_q", [_cols2(hn_t, tt)], None,
                              [_rows2(t, tt) for t in dqs], _concat_f32, D, N_GROUPS * QW, T, tt, out_cols=w_q_i.shape[2])
            dh, d_norm_mix[l] = _nt_norm_bwd(f"l{l}_q_bwd", dqs, w_q_i, h_in, g_mix, dh2, hub=hub)
            if l == N_A_LAYERS:
                dkvs = [t for pair in dkv_acc for t in pair]
                g_kv = norm_kv.reshape(1, D)
                w_kv = W["w_kv", None]
                per_call = len(dkvs) // 2
                halves = [_tn(f"dw_kv{p}", [_cols2(hn_kv_t, tt)], None,
                              [_rows2(t, tt) for t in dkvs[p * per_call:(p + 1) * per_call]], _concat_f32,
                              D, per_call * QW, T, tt, out_cols=w_kv.shape[2]) for p in range(2)]
                G["w_kv", None] = jnp.concatenate(halves, axis=0)
                dh, d_norm_kv = _nt_norm_bwd("kv_bwd", dkvs, w_kv, h_kv, g_kv, dh, tm=MM_ROWS // 2, hub=hub)
        else:
            _, bcu, gated_t, hn_t = saved[l][:4]
            cw = _pad8(conv_w[l])
            w_in_l = W["w_a_in", l]
            dgated = _nt_plain(f"l{l}_conv_out_bwd", dh2, _rows_merged(W["w_a_out", l]))
            G["w_a_out", l] = as_slots(_tn(f"l{l}_dw_conv_out", [_cols2(gated_t, 2 * tt)], None,
                                           [_rows2(dh2, 2 * tt)], _val, D, D, T, 2 * tt, hub=hub))
            dbcu, d_conv[l] = _conv_bwd(f"l{l}_conv_bwd", bcu, dgated, cw, seq, hub=hub)
            G["w_a_in", l] = _tn(f"l{l}_dw_in", [_cols2(hn_t, 2 * tt)], None,
                                 [_rows2(dbcu, 2 * tt, 3 * D // 2, lambda s: s)], _val, D, 3 * D, T, 2 * tt, split=("n", 2),
                                 out_cols=w_in_l.shape[2], hub=hub)
            dh, d_norm_mix[l] = _nt_norm_bwd(f"l{l}_in_bwd", [dbcu], w_in_l, h_in, g_mix, dh2, hub=hub)

    small = jnp.concatenate(d_norm_mix + d_norm_mlp + [d_norm_kv, d_norm_final] + d_conv, axis=0)
    return sq_err, dh.reshape(bl, seq, D), small


def kernel(x, norm_mix, norm_mlp, w_a_in, conv_w, w_a_out, norm_kv, w_kv, w_q, w_o, w_up, w_down, norm_final, loss_target, m_norm_mix, m_norm_mlp, m_w_a_in, m_conv_w, m_w_a_out, m_norm_kv, m_w_kv, m_w_q, m_w_o, m_w_up, m_w_down, m_norm_final, v_norm_mix, v_norm_mlp, v_w_a_in, v_conv_w, v_w_a_out, v_norm_kv, v_w_kv, v_w_q, v_w_o, v_w_up, v_w_down, v_norm_final):
    D = x.shape[-1]
    xi, yi, ci = _mesh_pos()
    me_idx = 4 * xi + 2 * yi + ci
    w_big = dict(w_a_in=w_a_in, w_a_out=w_a_out, w_kv=w_kv, w_q=w_q, w_o=w_o, w_up=w_up, w_down=w_down)
    m_big = dict(w_a_in=m_w_a_in, w_a_out=m_w_a_out, w_kv=m_w_kv, w_q=m_w_q, w_o=m_w_o, w_up=m_w_up, w_down=m_w_down)
    v_big = dict(w_a_in=v_w_a_in, w_a_out=v_w_a_out, w_kv=v_w_kv, w_q=v_w_q, w_o=v_w_o, w_up=v_w_up, w_down=v_w_down)
    names = list(w_big)

    shards = {n: w.astype(BF16) for n, w in w_big.items()}
    landing = {n: lax.empty((N_DEV,) + w.shape, BF16) for n, w in w_big.items()}
    hub = _Hub(FETCH_DURING, PUSH_DURING, shards, landing)
    dc = conv_w.shape[-1]
    taps = conv_w.shape[0] * conv_w.shape[1]
    got = _all_gather("gather_first", [(shards[n], l) for n, l in FETCH_UP_FRONT] + [(_pad8(conv_w.reshape(taps, dc)), None)])
    for key, w in zip(FETCH_UP_FRONT, got):
        hub.weights[key] = w
    conv_full = jnp.moveaxis(got[-1][:, :taps], 0, 1).reshape(conv_w.shape[0], conv_w.shape[1], N_DEV * dc)

    sq_err, grad_x, small = _local_grads(x, loss_target, norm_mix, norm_mlp, norm_kv, norm_final, conv_full, hub)

    grads, deltas, new_m, new_v = {}, {}, {}, {}
    for n in names:
        shape = w_big[n].shape
        cols = shape[-1]
        flat = lambda t: t.reshape(-1, cols)
        parts = hub.landing[n].reshape(N_DEV, -1, cols)
        tr = parts.shape[1]
        while tr * cols > ADAMW_TILE and tr % 32 == 0:
            tr //= 2
        outs = _sum8_adamw(f"adamw_{n}", parts, flat(w_big[n]), flat(m_big[n]), flat(v_big[n]), tr=tr)
        grads[n], deltas[n], new_m[n], new_v[n] = (t.reshape(shape) for t in outs)

    n_gain = 2 * DEPTH + 2
    n_small = small.shape[0]
    small = jnp.concatenate([small, jnp.full((SUBLANES, D), sq_err, F32)], axis=0)
    rows_small = small.shape[0]
    small_all = _all_gather("gather_small_grads", [(small, None)])[0]

    def small_pack(nm, nl, nk, nf, cw):
        gains = jnp.concatenate([nm, nl, nk.reshape(1, D), nf.reshape(1, D)], axis=0)
        taps_full = lax.dynamic_update_slice(jnp.zeros((taps, D), F32), cw.reshape(taps, dc), (0, me_idx * dc))
        return jnp.concatenate([gains, taps_full, jnp.zeros((SUBLANES, D), F32)], axis=0)

    sp = [small_pack(*t) for t in ((norm_mix, norm_mlp, norm_kv, norm_final, conv_w),
                                   (m_norm_mix, m_norm_mlp, m_norm_kv, m_norm_final, m_conv_w),
                                   (v_norm_mix, v_norm_mlp, v_norm_kv, v_norm_final, v_conv_w))]
    small_out = _sum8_adamw("adamw_small", small_all, *sp, tr=rows_small)
    loss = small_out[0][n_small, 0] * (0.5 / D)

    def small_unpack(t):
        res = dict(norm_mix=t[0:DEPTH], norm_mlp=t[DEPTH:2 * DEPTH], norm_kv=t[2 * DEPTH], norm_final=t[2 * DEPTH + 1])
        res["conv_w"] = lax.dynamic_slice(t[n_gain:], (0, me_idx * dc), (taps, dc)).reshape(conv_w.shape)
        return res

    for dst, t in zip((grads, deltas, new_m, new_v), small_out):
        dst.update(small_unpack(t))

    order = ["norm_mix", "norm_mlp", "w_a_in", "conv_w", "w_a_out", "norm_kv", "w_kv", "w_q", "w_o", "w_up", "w_down",
             "norm_final"]
    return (loss, grad_x, *[grads[n] for n in order], *[deltas[n] for n in order], *[new_m[n] for n in order],
            *[new_v[n] for n in order])
```

```python
import jax
import jax.numpy as jnp
from jax import lax
from jax.experimental import pallas as pl
from jax.experimental.pallas import tpu as pltpu

F32 = jnp.float32
BF16 = jnp.bfloat16
SDS = jax.ShapeDtypeStruct

EPS = 1e-5
N_A_LAYERS = 2
DEPTH = 4
PATTERNS = ((128, 1), (512, 4), (2048, 16))
N_GROUPS = 3
H_G = 8
HEAD_DIM = 64
QW = H_G * HEAD_DIM
ATT_BLK = 128
ALIBI_MAX_BIAS = 8.0
NEG_INF = -1e30

ADAM_LR = 0.001
ADAM_B1 = 0.9
ADAM_B2 = 0.999
ADAM_EPS = 1e-08
ADAM_WD = 0.01
ADAM_STEP = 10

N_DEV = 8
SUBLANES = 8
HALO = 16
V7X_VMEM_LIMIT = 48 * 1024 * 1024
MXU_COLS = 256
MM_CHUNK = 512
MM_ROWS = 512
ADAMW_TILE = 256 * 1024
DW_TOKENS = 1024

FETCH_UP_FRONT = [("w_a_in", 0)]
FETCH_DURING = {
    "l0_in": [("w_a_out", 0), ("w_up", 0)], "l0_conv_out": [("w_down", 0)], "l0_up": [("w_a_in", 1), ("w_a_out", 1)], "l0_down": [("w_up", 1)],
    "l1_in": [("w_down", 1)], "l1_conv_out": [("w_kv", None)],
    "l1_up": [("w_q", 0), ("w_o", 0), ("w_q", 1), ("w_o", 1)], "l1_down": [("w_up", 2)],
    "kv": [("w_down", 2)], "l2_up": [("w_up", 3)], "l2_down": [("w_down", 3)],
}
PUSH_DURING = {
    "l3_dw_up": [("w_down", 3, 0, 2)], "l3_up_bwd": [("w_down", 3, 1, 2)],
    "l3_att0_dq": [("w_up", 3, 0, 2)], "l3_att0_dkv": [("w_up", 3, 1, 2)], "l3_att1_dq": [("w_o", 1)], "l3_q_bwd": [("w_q", 1)],
    "l2_dw_up": [("w_down", 2, 0, 2)], "l2_up_bwd": [("w_down", 2, 1, 2)],
    "l2_att0_dq": [("w_up", 2, 0, 2)], "l2_att0_dkv": [("w_up", 2, 1, 2)], "l2_att1_dq": [("w_o", 0)], "l2_q_bwd": [("w_q", 0)],
    "kv_bwd": [("w_kv", None, 0, 2)], "l1_down_bwd": [("w_kv", None, 1, 2)],
    "l1_dw_up": [("w_down", 1, 0, 2)], "l1_up_bwd": [("w_down", 1, 1, 2)], "l1_conv_bwd": [("w_up", 1, 0, 2)],
    "l1_dw_in": [("w_up", 1, 1, 2), ("w_a_out", 1)], "l1_in_bwd": [("w_a_in", 1, 0, 2)], "l0_down_bwd": [("w_a_in", 1, 1, 2)],
    "l0_dw_up": [("w_down", 0, 0, 2)], "l0_up_bwd": [("w_down", 0, 1, 2)], "l0_conv_bwd": [("w_up", 0, 0, 2)],
    "l0_dw_in": [("w_up", 0, 1, 2), ("w_a_out", 0)], "l0_in_bwd": [("w_a_in", 0)],
}


def _mesh_pos():
    return lax.axis_index("x"), lax.axis_index("y"), lax.axis_index("c")


def _flip(v, bit):
    return 1 - v if bit else v


class _Transfer:
    def __init__(self, kind, key, src, src_idx=None, dst=None, dst_idx=None, dst_shape=None, rows=None):
        self.kind, self.key, self.src, self.src_idx = kind, key, src, src_idx
        self.dst, self.dst_idx, self.dst_shape, self.rows = dst, dst_idx, dst_shape, rows

    def copies(self, src_ref, dst_ref, send_sems, recv_sems, local_sem):
        x, y, c = _mesh_pos()
        me = 4 * x + 2 * y + c
        part = (lambda r: r) if self.rows is None else (lambda r: r.at[pl.ds(*self.rows)])

        def dst_slot(j):
            r = dst_ref.at[j]
            return part(r if self.dst_idx is None else r.at[self.dst_idx])

        def copy(k, src, dst_j, to):
            return pltpu.make_async_remote_copy(
                src_ref=src, dst_ref=dst_slot(dst_j), send_sem=send_sems.at[k], recv_sem=recv_sems.at[k],
                device_id=to, device_id_type=pl.DeviceIdType.MESH)

        if self.kind == "exchange":
            local = pltpu.make_async_copy(part(src_ref.at[me]), dst_slot(me), local_sem)
            sends, arrivals = [], []
            for k in range(1, N_DEV):
                peer = (_flip(x, k & 4), _flip(y, k & 2), _flip(c, k & 1))
                peer_idx = 4 * peer[0] + 2 * peer[1] + peer[2]
                sends.append(copy(k - 1, part(src_ref.at[peer_idx]), me, peer))
                arrivals.append(copy(k - 1, part(src_ref.at[peer_idx]), peer_idx, peer))
            return local, sends, [], arrivals

        own = part(src_ref if self.src_idx is None else src_ref.at[self.src_idx])
        idx = lambda px, py, pc: 4 * px + 2 * py + pc
        sibling = (x, y, 1 - c)
        chips = [(1 - x, y), (x, 1 - y), (1 - x, 1 - y)]
        local = pltpu.make_async_copy(own, dst_slot(me), local_sem)
        sends = [copy(0, own, me, sibling)] + [copy(1 + j, own, me, (*chip, c)) for j, chip in enumerate(chips)]
        relays = [(copy(1 + j, own, idx(*chip, c), sibling), copy(4 + j, dst_slot(idx(*chip, c)), idx(*chip, c), sibling))
                  for j, chip in enumerate(chips)]
        arrivals = [copy(0, own, idx(*sibling), sibling)]
        arrivals += [copy(4 + j, own, idx(*chip, 1 - c), sibling) for j, chip in enumerate(chips)]
        return local, sends, relays, arrivals


class _Hub:
    def __init__(self, fetch, push, shards, landing):
        self.fetch, self.push, self.shards, self.landing = fetch, push, shards, landing
        self.weights = {}
        self.arriving = {}
        self.grads = {}

    def transfers(self, host):
        out = []
        for name, l, *part in self.fetch.get(host, ()):
            src = self.shards[name]
            shard = tuple(src.shape if l is None else src.shape[1:])
            p, n = part or (0, 1)
            rows = None if n == 1 else (p * (shard[0] // n), shard[0] // n)
            out.append(_Transfer("gather", (name, l, p == n - 1), src, src_idx=l, dst=self.arriving.get((name, l)),
                                 dst_shape=(N_DEV,) + shard, rows=rows))
        for name, l, *part in self.push.get(host, ()):
            src = self.grads[name, l]
            p, n = part or (0, 1)
            rows = None if n == 1 else (p * (src.shape[1] // n), src.shape[1] // n)
            out.append(_Transfer("exchange", (name, l, p == n - 1), src, dst=self.landing[name], dst_idx=l, rows=rows))
        return out

    def accept(self, transfers, results):
        for t, r in zip(transfers, results):
            name, l, complete = t.key
            if t.kind == "exchange":
                self.landing[name] = r
            elif complete:
                self.weights[name, l] = r
            else:
                self.arriving[name, l] = r


def _call(name, body, grid, ins, outs, scratch=(), hub=None):
    transfers = hub.transfers(name) if hub is not None else []
    n_in, n_out, n_scr, n_tr = len(ins), len(outs), len(scratch), len(transfers)
    c_in, c_out, aliases, places = [], [], {}, []
    for t in transfers:
        c_in.append(t.src)
        src_pos = len(c_in) - 1
        if t.dst is not None:
            c_in.append(t.dst)
            aliases[n_in + len(c_in) - 1] = n_out + len(c_out)
            c_out.append(SDS(t.dst.shape, t.dst.dtype))
        else:
            c_out.append(SDS(t.dst_shape, t.src.dtype))
        places.append((src_pos, len(c_out) - 1))
    sems = [pltpu.SemaphoreType.DMA((n_tr, N_DEV - 1)), pltpu.SemaphoreType.DMA((n_tr, N_DEV - 1)),
            pltpu.SemaphoreType.DMA((n_tr,))] if n_tr else []

    def wrapped(*refs):
        in_refs = refs[:n_in]
        cin_refs = refs[n_in:n_in + len(c_in)]
        o0 = n_in + len(c_in)
        out_refs = refs[o0:o0 + n_out]
        cout_refs = refs[o0 + n_out:o0 + n_out + len(c_out)]
        s0 = o0 + n_out + len(c_out)
        scr_refs = refs[s0:s0 + n_scr]
        if n_tr:
            send_sems, recv_sems, local_sems = refs[s0 + n_scr:]
            first = last = relay = None
            for ax, n in enumerate(grid):
                i = pl.program_id(ax)
                at_relay = (i == max(n - 2, 0)) if ax == len(grid) - 1 else (i == n - 1)
                first = (i == 0) if first is None else first & (i == 0)
                last = (i == n - 1) if last is None else last & (i == n - 1)
                relay = at_relay if relay is None else relay & at_relay

            def all_copies():
                return [t.copies(cin_refs[sp], cout_refs[dp], send_sems.at[n], recv_sems.at[n], local_sems.at[n])
                        for n, (t, (sp, dp)) in enumerate(zip(transfers, places))]

            @pl.when(first)
            def _():
                for local, sends, _, _ in all_copies():
                    local.start()
                    for cp in sends:
                        cp.start()

            def pass_on():
                @pl.when(relay)
                def _():
                    for _, _, relays, _ in all_copies():
                        for arrival, onward in relays:
                            arrival.wait_recv()
                            onward.start()

            if grid[-1] > 1:
                pass_on()

        body(*in_refs, *out_refs, *scr_refs)

        if n_tr:
            if grid[-1] == 1:
                pass_on()

            @pl.when(last)
            def _():
                for local, sends, relays, arrivals in all_copies():
                    for cp in arrivals:
                        cp.wait_recv()
                    for cp in sends + [onward for _, onward in relays]:
                        cp.wait_send()
                    local.wait()

    any_spec = pl.BlockSpec(memory_space=pl.ANY)
    res = pl.pallas_call(
        wrapped,
        name=name,
        grid=grid,
        in_specs=[s for _, s in ins] + [any_spec] * len(c_in),
        out_specs=[s for _, s in outs] + [any_spec] * len(c_out),
        out_shape=[o for o, _ in outs] + c_out,
        scratch_shapes=list(scratch) + sems,
        input_output_aliases=aliases,
        compiler_params=pltpu.CompilerParams(
            dimension_semantics=("arbitrary",) * len(grid), vmem_limit_bytes=V7X_VMEM_LIMIT),
    )(*[a for a, _ in ins], *c_in)
    if n_tr:
        hub.accept(transfers, res[n_out:])
    return res[:n_out]


def _rows(a, tm, cb=None, col=0):
    cb = cb or a.shape[1]
    return (a, pl.BlockSpec((tm, cb), lambda i: (i, col)))


def _full(a):
    nd = a.ndim
    return (a, pl.BlockSpec(a.shape, lambda i: (0,) * nd))


def _prev8(a, tm, cb, col):
    return (a, pl.BlockSpec((HALO, cb), lambda i: (jnp.maximum(i * (tm // HALO) - 1, 0), col)))


def _next8(a, tm, cb, col):
    last = a.shape[0] // HALO - 1
    return (a, pl.BlockSpec((HALO, cb), lambda i: (jnp.minimum((i + 1) * (tm // HALO), last), col)))


def _rows2(a, tt, cb=None, colfn=None):
    cb = cb or a.shape[1]
    colfn = colfn or (lambda s: 0)
    return (a, pl.BlockSpec((tt, cb), lambda s, t: (t, colfn(s))))


def _full2(a):
    nd = a.ndim
    return (a, pl.BlockSpec(a.shape, lambda s, t: (0,) * nd))


def _out_rows(T, n, dtype, tm):
    return (SDS((T, n), dtype), pl.BlockSpec((tm, n), lambda i: (i, 0)))


def _out_acc8(d):
    return (SDS((SUBLANES, d), F32), pl.BlockSpec((SUBLANES, d), lambda i: (0, 0)))


def _rstd(x):
    return lax.rsqrt(jnp.mean(x * x, axis=-1, keepdims=True) + EPS)


def _normed(h_ref, g_ref):
    x = h_ref[...]
    return x * _rstd(x) * g_ref[...]


def _acc8(ref, val, i, n):
    part = val.reshape(-1, SUBLANES, val.shape[-1]).sum(axis=0)

    @pl.when(i == 0)
    def _():
        ref[...] = part

    @pl.when(i > 0)
    def _():
        ref[...] += part

    @pl.when(i == n - 1)
    def _():
        ref[...] = jnp.broadcast_to(jnp.sum(ref[...], axis=0, keepdims=True), ref.shape)


def _gate(b_ref, c_ref, u_ref, ch_ref, uh_ref, cw_ref, first):
    b, c, u = (r[...].astype(F32) for r in (b_ref, c_ref, u_ref))
    cu = c * u
    halo = jnp.where(first, 0.0, ch_ref[...].astype(F32) * uh_ref[...].astype(F32))
    rows = lax.broadcasted_iota(jnp.int32, cu.shape, 0)
    h1 = halo[HALO - 1:HALO, :]
    h2 = halo[HALO - 2:HALO - 1, :]
    cu1 = jnp.where(rows == 0, h1, pltpu.roll(cu, 1, 0))
    cu2 = jnp.where(rows == 0, h2, jnp.where(rows == 1, h1, pltpu.roll(cu, 2, 0)))
    conv = cw_ref[0:1, :] * cu + cw_ref[1:2, :] * cu1 + cw_ref[2:3, :] * cu2
    return b * conv, (b, c, u), conv, (cu, cu1, cu2)


def _relu2(a_ref):
    r = jnp.maximum(a_ref[...].astype(F32), 0.0)
    return r * r


def _dot(a, b):
    return jnp.dot(a, b, preferred_element_type=F32)


def _dot_nt(a, b):
    return lax.dot_general(a, b, (((1,), (1,)), ((), ())), preferred_element_type=F32)


def _chunks(n):
    c = min(MM_CHUNK, n)
    while n % c:
        c -= 128
    assert c > 0, n
    return [(k * c, (k + 1) * c) for k in range(n // c)]


def _col_weight(w):
    _, K, ns = w.shape
    N = N_DEV * ns
    direct = ns % MXU_COLS == 0
    scratch = [] if direct else [pltpu.VMEM((K, N), BF16)]

    def prepare(w_ref, s_ref, step):
        if direct:
            return

        @pl.when(step == 0)
        def _():
            for j in range(N_DEV):
                s_ref[:, j * ns:(j + 1) * ns] = w_ref[j]

    def chunks(w_ref, s_ref):
        if direct:
            return [(j * ns, (j + 1) * ns, (lambda j=j: w_ref[j])) for j in range(N_DEV)]
        return [(lo, hi, (lambda lo=lo, hi=hi: s_ref[:, lo:hi])) for lo, hi in _chunks(N)]

    return N, scratch, prepare, chunks


def _out_cols(n, T, tm):
    return (SDS((n, T), BF16), pl.BlockSpec((n, tm), lambda i: (0, i)))


def _norm_mm(name, h, g, w, tm=MM_ROWS, out_dtype=F32, transposed=False, hub=None):
    T, D = h.shape
    N, w_scratch, prepare, chunks = _col_weight(w)

    def body(h_ref, g_ref, w_ref, o_ref, *rest):
        at_ref, s = (rest[0], rest[1:]) if transposed else (None, rest)
        s_ref = s[0] if s else None
        prepare(w_ref, s_ref, pl.program_id(0))
        a32 = _normed(h_ref, g_ref)
        a = a32.astype(BF16)
        for lo, hi, load in chunks(w_ref, s_ref):
            o_ref[:, lo:hi] = _dot(a, load()).astype(out_dtype)
        if transposed:
            at_ref[...] = a32.T.astype(BF16)

    outs = [_out_rows(T, N, out_dtype, tm)] + ([_out_cols(D, T, tm)] if transposed else [])
    res = _call(name, body, (T // tm,), [_rows(h, tm), _full(g), _full(w)], outs, scratch=w_scratch, hub=hub)
    return res if transposed else res[0]


def _gate_mm_res(name, bcu, cw, w, h, seq, tm=MM_ROWS, hub=None):
    T, D = h.shape

    def body(b_ref, c_ref, u_ref, ch_ref, uh_ref, cw_ref, w_ref, h_ref, o_ref, gt_ref):
        first = (pl.program_id(0) * tm) % seq == 0
        gated32 = _gate(b_ref, c_ref, u_ref, ch_ref, uh_ref, cw_ref, first)[0]
        gated = gated32.astype(BF16)
        for lo, hi in _chunks(D):
            o_ref[:, lo:hi] = h_ref[:, lo:hi] + _dot(gated, w_ref[:, lo:hi])
        gt_ref[...] = gated32.T.astype(BF16)

    ins = [_rows(bcu, tm, D, 0), _rows(bcu, tm, D, 1), _rows(bcu, tm, D, 2), _prev8(bcu, tm, D, 1),
           _prev8(bcu, tm, D, 2), _full(cw), _full(w), _rows(h, tm)]
    return _call(name, body, (T // tm,), ins, [_out_rows(T, D, F32, tm), _out_cols(D, T, tm)], hub=hub)


def _relu2_mm_res(name, a, w, h, tm=MM_ROWS, hub=None):
    T, D = h.shape
    K = a.shape[1]

    def body(a_ref, w_ref, h_ref, o_ref, acc_ref):
        for n, (lo, hi) in enumerate(_chunks(K)):
            d = _dot(_relu2(a_ref.at[:, lo:hi]).astype(BF16), w_ref[lo:hi, :])
            if n == 0:
                acc_ref[...] = d
            else:
                acc_ref[...] += d
        o_ref[...] = h_ref[...] + acc_ref[...]

    return _call(name, body, (T // tm,), [_rows(a, tm), _full(w), _rows(h, tm)], [_out_rows(T, D, F32, tm)],
                 scratch=[pltpu.VMEM((tm, D), F32)], hub=hub)[0]


def _combine_mm_res(name, os_, lses, w, h, tm=MM_ROWS):
    T, D = h.shape
    _, w_scratch, prepare, chunks = _col_weight(w)

    def body(o0, o1, o2, l0, l1, l2, w_ref, h_ref, o_ref, lse_ref, out_ref, *s):
        s_ref = s[0] if s else None
        prepare(w_ref, s_ref, pl.program_id(0))
        ls = [l0[...], l1[...], l2[...]]
        mx = jnp.maximum(jnp.maximum(ls[0], ls[1]), ls[2])
        es = [jnp.exp(l - mx) for l in ls]
        den = es[0] + es[1] + es[2]
        o = (es[0] * o0[...] + es[1] * o1[...] + es[2] * o2[...]) / den
        o_ref[...] = o
        lse_ref[...] = mx + jnp.log(den)
        ob = o.astype(BF16)
        for lo, hi, load in chunks(w_ref, s_ref):
            out_ref[:, lo:hi] = h_ref[:, lo:hi] + _dot(ob, load())

    ins = [_rows(t, tm) for t in list(os_) + list(lses)] + [_full(w), _rows(h, tm)]
    outs = [_out_rows(T, QW, F32, tm), _out_rows(T, QW, F32, tm), _out_rows(T, D, F32, tm)]
    return _call(name, body, (T // tm,), ins, outs, scratch=w_scratch)


def _nt_relu2_bwd(name, dh, w, a, tm=MM_ROWS, hub=None):
    T, _ = dh.shape
    K = w.shape[0]

    def body(dh_ref, w_ref, a_ref, o_ref):
        d = dh_ref[...].astype(BF16)
        for lo, hi in _chunks(K):
            dr = _dot_nt(d, w_ref[lo:hi, :])
            o_ref[:, lo:hi] = (dr * (2.0 * jnp.maximum(a_ref[:, lo:hi].astype(F32), 0.0))).astype(BF16)

    return _call(name, body, (T // tm,), [_rows(dh, tm), _full(w), _rows(a, tm)], [_out_rows(T, K, BF16, tm)], hub=hub)[0]


def _concat_bf16(*refs):
    vals = [r[...].astype(BF16) for r in refs]
    return vals[0] if len(vals) == 1 else jnp.concatenate(vals, axis=1)


def _nt_plain(name, dy, w, tm=MM_ROWS):
    T, N = dy.shape
    K = w.shape[0]

    def body(dy_ref, w_ref, o_ref, acc_ref):
        for n, (lo, hi) in enumerate(_chunks(N)):
            d = _dot_nt(dy_ref[:, lo:hi].astype(BF16), w_ref[:, lo:hi])
            if n == 0:
                acc_ref[...] = d
            else:
                acc_ref[...] += d
        o_ref[...] = acc_ref[...]

    return _call(name, body, (T // tm,), [_rows(dy, tm), _full(w)], [_out_rows(T, K, F32, tm)],
                 scratch=[pltpu.VMEM((tm, K), F32)])[0]


def _att_out_bwd(name, dy, w, o, tm=MM_ROWS):
    T, _ = dy.shape
    K = w.shape[1]
    _, w_scratch, prepare, chunks = _col_weight(w)

    def body(dy_ref, w_ref, o_ref, do_ref, dl_ref, acc_ref, *s):
        s_ref = s[0] if s else None
        prepare(w_ref, s_ref, pl.program_id(0))
        for n, (lo, hi, load) in enumerate(chunks(w_ref, s_ref)):
            d = _dot_nt(dy_ref[:, lo:hi].astype(BF16), load())
            if n == 0:
                acc_ref[...] = d
            else:
                acc_ref[...] += d
        do = acc_ref[...]
        do_ref[...] = do
        prod = do * o_ref[...]
        high = prod.astype(BF16)
        low = (prod - high.astype(F32)).astype(BF16)
        head_of = lambda axis: jnp.right_shift(lax.broadcasted_iota(jnp.int32, (K, K), axis), HEAD_DIM.bit_length() - 1)
        same_head = jnp.where(head_of(0) == head_of(1), 1.0, 0.0).astype(BF16)
        dl_ref[...] = _dot(high, same_head) + _dot(low, same_head)

    outs = [_out_rows(T, K, F32, tm), _out_rows(T, K, F32, tm)]
    return _call(name, body, (T // tm,), [_rows(dy, tm), _full(w), _rows(o, tm)], outs,
                 scratch=[pltpu.VMEM((tm, K), F32)] + w_scratch)


def _nt_norm_bwd(name, dys, w, h, g, dh_in, tm=MM_ROWS, hub=None):
    T, D = h.shape
    _, w_scratch, prepare, chunks = _col_weight(w)
    n_steps = T // tm
    n_dy = len(dys)

    def body(*refs):
        dy_refs = refs[:n_dy]
        w_ref, h_ref, g_ref, dhin_ref, o_ref, dg_ref, acc_ref = refs[n_dy:n_dy + 7]
        s_ref = refs[n_dy + 7] if len(refs) > n_dy + 7 else None
        i = pl.program_id(0)
        prepare(w_ref, s_ref, i)
        dy = _concat_bf16(*dy_refs)
        for n, (lo, hi, load) in enumerate(chunks(w_ref, s_ref)):
            d = _dot_nt(dy[:, lo:hi], load())
            if n == 0:
                acc_ref[...] = d
            else:
                acc_ref[...] += d
        dn = acc_ref[...]
        x = h_ref[...]
        rstd = _rstd(x)
        xhat = x * rstd
        dxhat = dn * g_ref[...]
        dx = rstd * (dxhat - xhat * jnp.mean(dxhat * xhat, axis=-1, keepdims=True))
        o_ref[...] = dhin_ref[...] + dx
        _acc8(dg_ref, dn * xhat, i, n_steps)

    ins = [_rows(d, tm) for d in dys] + [_full(w), _rows(h, tm), _full(g), _rows(dh_in, tm)]
    outs = [_out_rows(T, D, F32, tm), _out_acc8(D)]
    dh, dg = _call(name, body, (n_steps,), ins, outs, scratch=[pltpu.VMEM((tm, D), F32)] + w_scratch, hub=hub)
    return dh, dg[0:1]


def _cols2(a_t, tt, kb=None):
    kb = kb or a_t.shape[0]
    return (a_t, pl.BlockSpec((kb, tt), (lambda s, t: (s, t)) if kb != a_t.shape[0] else (lambda s, t: (0, t))))


def _tn(name, a_ins, a_fn, y_ins, y_fn, K, N, T, tt, split=None, out_cols=None, hub=None):
    kind, parts = split or ("n", 1)
    kb, nb = (K // parts, N) if kind == "k" else (K, N // parts)
    n_steps = T // tt
    n_a = len(a_ins)
    n_y = len(y_ins)
    assert out_cols is None or (kind == "n" and nb % out_cols == 0)

    def body(*refs):
        a_refs = refs[:n_a]
        y_refs = refs[n_a:n_a + n_y]
        o_ref, acc_ref = refs[n_a + n_y:]
        t = pl.program_id(1)
        a_t = a_refs[0][...] if a_fn is None else a_fn(*a_refs).T.astype(BF16)
        y = y_fn(*y_refs).astype(BF16)
        for lo, hi in _chunks(nb):
            d = _dot(a_t, y[:, lo:hi])

            @pl.when(t == 0)
            def _():
                acc_ref[:, lo:hi] = d

            @pl.when(t > 0)
            def _():
                acc_ref[:, lo:hi] += d

        @pl.when(t == n_steps - 1)
        def _():
            if out_cols is None:
                o_ref[...] = acc_ref[...].astype(BF16)
            else:
                for j in range(nb // out_cols):
                    o_ref[j] = acc_ref[:, j * out_cols:(j + 1) * out_cols].astype(BF16)

    if out_cols is None:
        out = (SDS((K, N), BF16), pl.BlockSpec((kb, nb), (lambda s, t: (s, 0)) if kind == "k" else (lambda s, t: (0, s))))
    else:
        out = (SDS((N // out_cols, K, out_cols), BF16), pl.BlockSpec((nb // out_cols, K, out_cols), lambda s, t: (s, 0, 0)))
    return _call(name, body, (parts, n_steps), list(a_ins) + list(y_ins), [out],
                 scratch=[pltpu.VMEM((kb, nb), F32)], hub=hub)[0]


def _val(ref):
    return ref[...]


def _concat_f32(*refs):
    vals = [r[...] for r in refs]
    return vals[0] if len(vals) == 1 else jnp.concatenate(vals, axis=1)


ATT_TILE_ROWS = 2048
ATT_TILE_ROWS_Q = 4096
HEAD_PAIRS = H_G // 2
ATT_SCALE = HEAD_DIM ** -0.5
ATT_UNITS_TOGETHER = 4


def _slope(h):
    return 2.0 ** (-ALIBI_MAX_BIAS * (h + 1) / H_G)


def _att_geom(T, bl, g, tile_rows=ATT_TILE_ROWS):
    dil = PATTERNS[g][1]
    sub = ATT_BLK * dil
    nsub = max(1, min(tile_rows, T // bl) // sub)
    rows = sub * nsub
    return dil, sub, nsub, rows, T // bl // rows


def _att_specs(T, bl, g, tile_rows=ATT_TILE_ROWS):
    _, sub, nsub, rows, nt = _att_geom(T, bl, g, tile_rows)
    last_sub = T // sub - 1
    tile = lambda col: pl.BlockSpec((rows, 128), lambda b, i, hp: (b * nt + i, col(hp)))
    prev = lambda col: pl.BlockSpec((sub, 128), lambda b, i, hp: (jnp.maximum((b * nt + i) * nsub - 1, 0), col(hp)))
    nxt = lambda col: pl.BlockSpec((sub, 128), lambda b, i, hp: (jnp.minimum((b * nt + i + 1) * nsub, last_sub), col(hp)))
    return tile, prev, nxt


def _sub_rows(j, r, dil):
    start = j * ATT_BLK * dil + r
    return pl.ds(start, ATT_BLK, stride=dil) if dil > 1 else pl.ds(start, ATT_BLK)


class _Residues:
    def __init__(self, dil):
        self.dil = dil
        self.whole = dil % SUBLANES == 0
        self.read, self.written = {}, {}

    def _block(self, j):
        return pl.ds(j * ATT_BLK * self.dil, ATT_BLK * self.dil)

    def load(self, ref, j, r):
        if not self.whole:
            return ref[_sub_rows(j, r, self.dil), :]
        if (id(ref), j) not in self.read:
            rows = ref[self._block(j), :]
            self.read[id(ref), j] = jnp.swapaxes(rows.reshape(ATT_BLK, self.dil, rows.shape[-1]), 0, 1)
        return self.read[id(ref), j][r]

    def store(self, ref, j, r, val):
        if not self.whole:
            ref[_sub_rows(j, r, self.dil), :] = val
            return
        got = self.written.setdefault((id(ref), j), {})
        got[r] = val
        if len(got) == self.dil:
            merged = jnp.swapaxes(jnp.stack([got[k] for k in range(self.dil)], axis=0), 0, 1)
            ref[self._block(j), :] = merged.reshape(ATT_BLK * self.dil, val.shape[-1])
            del self.written[id(ref), j]


def _att_consts(hp, dil, keys_first=False):
    h0 = lax.broadcasted_iota(jnp.int32, (ATT_BLK, 128), 1) < HEAD_DIM
    a = lax.broadcasted_iota(jnp.int32, (ATT_BLK, ATT_BLK), 1 if keys_first else 0)
    c = lax.broadcasted_iota(jnp.int32, (ATT_BLK, ATT_BLK), 0 if keys_first else 1)
    dist_p = ((ATT_BLK + a - c) * dil).astype(F32)
    dist_c = ((a - c) * dil).astype(F32)
    bias_p, bias_c = [], []
    for h in range(2):
        slope = jnp.float32(_slope(2 * (HEAD_PAIRS - 1) + h))
        for p in range(HEAD_PAIRS - 2, -1, -1):
            slope = jnp.where(hp == p, jnp.float32(_slope(2 * p + h)), slope)
        bias_p.append(jnp.where(c >= a, -slope * dist_p, NEG_INF))
        bias_c.append(jnp.where(c <= a, -slope * dist_c, NEG_INF))
    return h0, jnp.concatenate(bias_p, axis=0), jnp.concatenate(bias_c, axis=0)


def _stack_heads(x, h0):
    return jnp.concatenate([jnp.where(h0, x, 0.0), jnp.where(h0, 0.0, x)], axis=0).astype(BF16)


def _unstack_heads(x, h0):
    return jnp.where(h0, x[:ATT_BLK], x[ATT_BLK:])


def _stack_cols(x):
    return jnp.concatenate(_head_cols(x), axis=0)


def _head_cols(x):
    return [x[:, 0:1], x[:, HEAD_DIM:HEAD_DIM + 1]]


def _in_groups(units, first_stage, *later_stages):
    for u0 in range(0, len(units), ATT_UNITS_TOGETHER):
        staged = [first_stage(*u) for u in units[u0:u0 + ATT_UNITS_TOGETHER]]
        for stage in later_stages:
            staged = [stage(*s) for s in staged]


def _attn_fwd(name, q, kv, g, bl, hub=None):
    T = q.shape[0]
    dil, _, nsub, _, nt = _att_geom(T, bl, g, ATT_TILE_ROWS_Q)
    tile, prev, _ = _att_specs(T, bl, g, ATT_TILE_ROWS_Q)
    halo = nt > 1

    def body(*refs):
        if halo:
            q_ref, kp_ref, kc_ref, vp_ref, vc_ref, o_ref, lse_ref = refs
        else:
            q_ref, kc_ref, vc_ref, o_ref, lse_ref = refs
        first = pl.program_id(1) == 0
        h0, bias_p, bias_c = _att_consts(pl.program_id(2), dil)
        bias_first = jnp.where(first, NEG_INF, bias_p)
        rows = _Residues(dil)

        def with_ones(v):
            return [jnp.where(h0, v, 1.0).astype(BF16), jnp.where(h0, 1.0, v).astype(BF16)]

        def scores(j, r):
            parts = [(rows.load(kc_ref, j, r), rows.load(vc_ref, j, r), bias_c)]
            if j > 0:
                parts.append((rows.load(kc_ref, j - 1, r), rows.load(vc_ref, j - 1, r), bias_p))
            elif halo:
                parts.append((rows.load(kp_ref, 0, r), rows.load(vp_ref, 0, r), bias_first))
            qs = _stack_heads(rows.load(q_ref, j, r) * ATT_SCALE, h0)
            return (j, r), [_dot_nt(qs, k.astype(BF16)) + b for k, _, b in parts], [with_ones(v) for _, v, _ in parts]

        def weights(unit, s, vals):
            mx = jnp.max(s[0] if len(s) == 1 else jnp.maximum(s[0], s[1]), axis=-1, keepdims=True)
            return unit, mx, [jnp.exp(x - mx).astype(BF16) for x in s], vals

        def outputs(unit, mx, e, vals):
            heads = [slice(h * ATT_BLK, (h + 1) * ATT_BLK) for h in range(2)]
            acc = []
            for h, hs in enumerate(heads):
                terms = [_dot(x[hs], v[h]) for x, v in zip(e, vals)]
                acc.append(terms[0] if len(terms) == 1 else terms[0] + terms[1])
            den = [pltpu.roll(a, HEAD_DIM, 1) for a in acc]
            rows.store(o_ref, *unit, jnp.where(h0, acc[0] / den[0], acc[1] / den[1]))
            rows.store(lse_ref, *unit, jnp.where(h0, mx[heads[0]] + jnp.log(den[0]), mx[heads[1]] + jnp.log(den[1])))
            return ()

        _in_groups([(j, r) for j in range(nsub) for r in range(dil)], scores, weights, outputs)

    kcol, vcol = (lambda hp: 8 * g + hp), (lambda hp: 8 * g + 4 + hp)
    ins = [(q, tile(lambda hp: 4 * g + hp))] + ([(kv, prev(kcol))] if halo else []) + [(kv, tile(kcol))]
    ins += ([(kv, prev(vcol))] if halo else []) + [(kv, tile(vcol))]
    out = (SDS((T, QW), F32), tile(lambda hp: hp))
    return _call(name, body, (bl, nt, HEAD_PAIRS), ins, [out, out], hub=hub)


def _attn_bwd_dq(name, q, kv, do, delta, lse, g, bl, hub=None):
    T = q.shape[0]
    dil, _, nsub, _, nt = _att_geom(T, bl, g, ATT_TILE_ROWS_Q)
    tile, prev, _ = _att_specs(T, bl, g, ATT_TILE_ROWS_Q)
    halo = nt > 1

    def body(*refs):
        if halo:
            q_ref, kp_ref, kc_ref, vp_ref, vc_ref, do_ref, dl_ref, lse_ref, dq_ref = refs
        else:
            q_ref, kc_ref, vc_ref, do_ref, dl_ref, lse_ref, dq_ref = refs
        first = pl.program_id(1) == 0
        h0, bias_p, bias_c = _att_consts(pl.program_id(2), dil)
        bias_first = jnp.where(first, NEG_INF, bias_p)
        rows = _Residues(dil)

        def probs(j, r):
            parts = [(rows.load(kc_ref, j, r), rows.load(vc_ref, j, r), bias_c)]
            if j > 0:
                parts.append((rows.load(kc_ref, j - 1, r), rows.load(vc_ref, j - 1, r), bias_p))
            elif halo:
                parts.append((rows.load(kp_ref, 0, r), rows.load(vp_ref, 0, r), bias_first))
            qs = _stack_heads(rows.load(q_ref, j, r) * ATT_SCALE, h0)
            dos = _stack_heads(rows.load(do_ref, j, r), h0)
            lse = _stack_cols(rows.load(lse_ref, j, r))
            keys = [k.astype(BF16) for k, _, _ in parts]
            p = [jnp.exp(_dot_nt(qs, k) + b - lse) for k, (_, _, b) in zip(keys, parts)]
            dp = [_dot_nt(dos, v.astype(BF16)) for _, v, _ in parts]
            return (j, r), p, dp, keys

        def dscores(unit, p, dp, keys):
            dl = _stack_cols(rows.load(dl_ref, *unit))
            return unit, [(x * (y - dl)).astype(BF16) for x, y in zip(p, dp)], keys

        def outputs(unit, ds, keys):
            terms = [_dot(x, k) for x, k in zip(ds, keys)]
            dq = terms[0] if len(terms) == 1 else terms[0] + terms[1]
            rows.store(dq_ref, *unit, _unstack_heads(dq, h0) * ATT_SCALE)
            return ()

        _in_groups([(j, r) for j in range(nsub) for r in range(dil)], probs, dscores, outputs)

    own = lambda hp: hp
    kcol, vcol = (lambda hp: 8 * g + hp), (lambda hp: 8 * g + 4 + hp)
    ins = [(q, tile(lambda hp: 4 * g + hp))] + ([(kv, prev(kcol))] if halo else []) + [(kv, tile(kcol))]
    ins += ([(kv, prev(vcol))] if halo else []) + [(kv, tile(vcol))]
    ins += [(do, tile(own)), (delta, tile(own)), (lse, tile(own))]
    return _call(name, body, (bl, nt, HEAD_PAIRS), ins, [(SDS((T, QW), F32), tile(own))], hub=hub)[0]


def _attn_bwd_dkv(name, q, kv, do, delta, lse, g, bl, prev=None, hub=None):
    T = q.shape[0]
    dil, _, nsub, _, nt = _att_geom(T, bl, g)
    tile, _, nxt = _att_specs(T, bl, g)
    has_prev = prev is not None

    def body(*refs):
        k_ref, v_ref, q_ref, qn_ref, do_ref, don_ref, dl_ref, dln_ref, l_ref, ln_ref = refs[:10]
        rest = refs[10:]
        if has_prev:
            dkp_ref, dvp_ref, dk_ref, dv_ref = rest
        else:
            dk_ref, dv_ref = rest
        last = pl.program_id(1) == nt - 1
        h0, bias_p, bias_c = _att_consts(pl.program_id(2), dil, keys_first=True)
        bias_last = jnp.where(last, NEG_INF, bias_p)
        rows = _Residues(dil)

        def per_query_rows(x):
            xt = x.T
            return jnp.concatenate([jnp.broadcast_to(xt[0:1], (ATT_BLK, ATT_BLK)),
                                    jnp.broadcast_to(xt[HEAD_DIM:HEAD_DIM + 1], (ATT_BLK, ATT_BLK))], axis=0)

        def probs(j, r):
            ks = _stack_heads(rows.load(k_ref, j, r), h0)
            vs = _stack_heads(rows.load(v_ref, j, r), h0)
            sets = [(q_ref, do_ref, dl_ref, l_ref, j, bias_c)]
            if j < nsub - 1:
                sets.append((q_ref, do_ref, dl_ref, l_ref, j + 1, bias_p))
            else:
                sets.append((qn_ref, don_ref, dln_ref, ln_ref, 0, bias_last))
            out = []
            for qr, dor, dlr, lr, jq, bias in sets:
                qsb = (rows.load(qr, jq, r) * ATT_SCALE).astype(BF16)
                do2b = rows.load(dor, jq, r).astype(BF16)
                p = jnp.exp(_dot_nt(ks, qsb) + bias - per_query_rows(rows.load(lr, jq, r)))
                out.append((p, _dot_nt(vs, do2b), dlr, jq, qsb, do2b))
            return (j, r), out

        def dscores(unit, sets):
            out = []
            for p, dp, dlr, jq, qsb, do2b in sets:
                ds = (p * (dp - per_query_rows(rows.load(dlr, jq, unit[1])))).astype(BF16)
                out.append((p.astype(BF16), ds, qsb, do2b))
            return unit, out

        def outputs(unit, sets):
            dk_st = dv_st = None
            for pb, ds, qsb, do2b in sets:
                dvs, dks = _dot(pb, do2b), _dot(ds, qsb)
                dv_st = dvs if dv_st is None else dv_st + dvs
                dk_st = dks if dk_st is None else dk_st + dks
            dk2 = _unstack_heads(dk_st, h0)
            dv2 = _unstack_heads(dv_st, h0)
            if has_prev:
                dk2 = dk2 + rows.load(dkp_ref, *unit)
                dv2 = dv2 + rows.load(dvp_ref, *unit)
            rows.store(dk_ref, *unit, dk2)
            rows.store(dv_ref, *unit, dv2)
            return ()

        _in_groups([(j, r) for j in range(nsub) for r in range(dil)], probs, dscores, outputs)

    own = lambda hp: hp
    qcol = lambda hp: 4 * g + hp
    ins = [(kv, tile(lambda hp: 8 * g + hp)), (kv, tile(lambda hp: 8 * g + 4 + hp)), (q, tile(qcol)), (q, nxt(qcol)),
           (do, tile(own)), (do, nxt(own)), (delta, tile(own)), (delta, nxt(own)), (lse, tile(own)), (lse, nxt(own))]
    if has_prev:
        ins += [(prev[0], tile(own)), (prev[1], tile(own))]
    out = (SDS((T, QW), F32), tile(own))
    return _call(name, body, (bl, nt, HEAD_PAIRS), ins, [out, out], hub=hub)


def _final_loss(name, h, tgt, g, tm=MM_ROWS):
    T, D = h.shape
    n_steps = T // tm

    def body(h_ref, t_ref, g_ref, dh_ref, loss_ref, dg_ref, sq_ref):
        i = pl.program_id(0)
        x = h_ref[...]
        rstd = _rstd(x)
        xhat = x * rstd
        err = xhat * g_ref[...] - t_ref[...]
        _acc8(sq_ref, err * err, i, n_steps)
        dy = err * (1.0 / D)
        dxhat = dy * g_ref[...]
        dh_ref[...] = rstd * (dxhat - xhat * jnp.mean(dxhat * xhat, axis=-1, keepdims=True))
        _acc8(dg_ref, dy * xhat, i, n_steps)

        @pl.when(i == n_steps - 1)
        def _():
            loss_ref[...] = jnp.full(loss_ref.shape, jnp.sum(sq_ref[0:1, :]), F32)

    outs = [_out_rows(T, D, F32, tm), (SDS((SUBLANES, 128), F32), pl.BlockSpec((SUBLANES, 128), lambda i: (0, 0))),
            _out_acc8(D)]
    dh, loss, dg = _call(name, body, (n_steps,), [_rows(h, tm), _rows(tgt, tm), _full(g)], outs,
                         scratch=[pltpu.VMEM((SUBLANES, D), F32)])
    return dh, loss[0, 0], dg[0:1]


def _conv_bwd(name, bcu, dgated, cw, seq, tm=MM_ROWS, hub=None):
    T, D = dgated.shape
    n_steps = T // tm

    def body(b_ref, c_ref, u_ref, ch_ref, uh_ref, dg_ref, dgn_ref, bn_ref, cw_ref, o_ref, t0_ref, t1_ref, t2_ref):
        i = pl.program_id(0)
        first = (i * tm) % seq == 0
        last = ((i + 1) * tm) % seq == 0
        _, (b, c, u), conv, (cu, cu1, cu2) = _gate(b_ref, c_ref, u_ref, ch_ref, uh_ref, cw_ref, first)
        dgat = dg_ref[...]
        dconv = dgat * b
        nxt = jnp.where(last, 0.0, dgn_ref[...] * bn_ref[...].astype(F32))
        rows = lax.broadcasted_iota(jnp.int32, dconv.shape, 0)
        n1 = nxt[0:1, :]
        n2 = nxt[1:2, :]
        dc1 = jnp.where(rows == tm - 1, n1, pltpu.roll(dconv, tm - 1, 0))
        dc2 = jnp.where(rows == tm - 1, n2, jnp.where(rows == tm - 2, n1, pltpu.roll(dconv, tm - 2, 0)))
        dcu = cw_ref[0:1, :] * dconv + cw_ref[1:2, :] * dc1 + cw_ref[2:3, :] * dc2
        o_ref[:, 0:D] = (dgat * conv).astype(BF16)
        o_ref[:, D:2 * D] = (dcu * u).astype(BF16)
        o_ref[:, 2 * D:3 * D] = (dcu * c).astype(BF16)
        _acc8(t0_ref, dconv * cu, i, n_steps)
        _acc8(t1_ref, dconv * cu1, i, n_steps)
        _acc8(t2_ref, dconv * cu2, i, n_steps)

    ins = [_rows(bcu, tm, D, 0), _rows(bcu, tm, D, 1), _rows(bcu, tm, D, 2), _prev8(bcu, tm, D, 1), _prev8(bcu, tm, D, 2),
           _rows(dgated, tm), _next8(dgated, tm, D, 0), _next8(bcu, tm, D, 0), _full(cw)]
    outs = [_out_rows(T, 3 * D, BF16, tm), _out_acc8(D), _out_acc8(D), _out_acc8(D)]
    dbcu, t0, t1, t2 = _call(name, body, (n_steps,), ins, outs, hub=hub)
    return dbcu, jnp.concatenate([t0[0:1], t1[0:1], t2[0:1]], axis=0)


def _sum8_adamw(name, parts, w, m, v, tr):
    R, C = w.shape
    b1c = 1.0 - ADAM_B1 ** ADAM_STEP
    b2c = 1.0 - ADAM_B2 ** ADAM_STEP

    def body(p_ref, w_ref, m_ref, v_ref, g_ref, d_ref, nm_ref, nv_ref):
        g = p_ref[0].astype(F32)
        for j in range(1, N_DEV):
            g = g + p_ref[j].astype(F32)
        nm = ADAM_B1 * m_ref[...] + (1.0 - ADAM_B1) * g
        nv = ADAM_B2 * v_ref[...] + (1.0 - ADAM_B2) * (g * g)
        m_hat = nm / b1c
        v_hat = nv / b2c
        g_ref[...] = g
        d_ref[...] = -ADAM_LR * (m_hat / (jnp.sqrt(v_hat) + ADAM_EPS) + ADAM_WD * w_ref[...])
        nm_ref[...] = nm
        nv_ref[...] = nv

    ins = [(parts, pl.BlockSpec((N_DEV, tr, C), lambda i: (0, i, 0))), _rows(w, tr), _rows(m, tr), _rows(v, tr)]
    outs = [_out_rows(R, C, F32, tr)] * 4
    return _call(name, body, (R // tr,), ins, outs)


def _all_gather(name, items):
    n = len(items)
    shapes = [tuple(a.shape if idx is None else a.shape[1:]) for a, idx in items]

    def body(*refs):
        x_refs, out_refs = refs[:n], refs[n:2 * n]
        send_sems, recv_sems, local_sems = refs[2 * n:]
        x, y, c = _mesh_pos()
        me, sibling = (x, y, c), (x, y, 1 - c)
        chips = [(1 - x, y), (x, 1 - y), (1 - x, 1 - y)]

        def copy(t, k, block, to, own=False):
            dst = out_refs[t].at[4 * block[0] + 2 * block[1] + block[2]]
            src = dst
            if own:
                src = x_refs[t] if items[t][1] is None else x_refs[t].at[items[t][1]]
            return pltpu.make_async_remote_copy(
                src_ref=src, dst_ref=dst, send_sem=send_sems.at[t, k], recv_sem=recv_sems.at[t, k],
                device_id=to, device_id_type=pl.DeviceIdType.MESH)

        started = []
        for t in range(n):
            src = x_refs[t] if items[t][1] is None else x_refs[t].at[items[t][1]]
            mine = pltpu.make_async_copy(src, out_refs[t].at[4 * x + 2 * y + c], local_sems.at[t])
            mine.start()
            first = [copy(t, 0, me, sibling, own=True)]
            first += [copy(t, 1 + j, me, (*chip, c), own=True) for j, chip in enumerate(chips)]
            for cp in first:
                cp.start()
            started.append((mine, first))
        passed = []
        for t in range(n):
            for j, chip in enumerate(chips):
                copy(t, 1 + j, (*chip, c), me).wait_recv()
                fwd = copy(t, 4 + j, (*chip, c), sibling)
                fwd.start()
                passed.append(fwd)
        for t in range(n):
            copy(t, 0, sibling, me).wait_recv()
            for j, chip in enumerate(chips):
                copy(t, 4 + j, (*chip, 1 - c), me).wait_recv()
        for mine, first in started:
            for cp in first:
                cp.wait_send()
            mine.wait()
        for cp in passed:
            cp.wait_send()

    any_spec = pl.BlockSpec(memory_space=pl.ANY)
    return pl.pallas_call(
        body, name=name,
        out_shape=[SDS((N_DEV,) + s, a.dtype) for s, (a, _) in zip(shapes, items)],
        in_specs=[any_spec] * n,
        out_specs=[any_spec] * n,
        scratch_shapes=[pltpu.SemaphoreType.DMA((n, 7)), pltpu.SemaphoreType.DMA((n, 7)), pltpu.SemaphoreType.DMA((n,))],
    )(*[a for a, _ in items])


def _pad8(t):
    return jnp.pad(t, ((0, SUBLANES - t.shape[0]), (0, 0)))


def _rows_merged(w):
    return w.reshape(w.shape[0] * w.shape[1], w.shape[2])


def _local_grads(x, tgt, norm_mix, norm_mlp, norm_kv, norm_final, conv_w, hub):
    bl, seq, D = x.shape
    T = bl * seq
    h = x.reshape(T, D)
    tgt = tgt.reshape(T, D)
    row = lambda t, l: t[l:l + 1]
    W = hub.weights
    saved = []
    kv = h_kv = hn_kv_t = None
    for l in range(DEPTH):
        if l < N_A_LAYERS:
            bcu, hn_t = _norm_mm(f"l{l}_in", h, row(norm_mix, l), W["w_a_in", l], out_dtype=BF16, transposed=True, hub=hub)
            h2, gated_t = _gate_mm_res(f"l{l}_conv_out", bcu, _pad8(conv_w[l]), _rows_merged(W["w_a_out", l]), h, seq, hub=hub)
            saved.append((h, bcu, gated_t, hn_t))
        else:
            i = l - N_A_LAYERS
            if l == N_A_LAYERS:
                h_kv = h
                kv, hn_kv_t = _norm_mm("kv", h, norm_kv.reshape(1, D), W["w_kv", None], transposed=True, hub=hub)
            q, hn_t = _norm_mm(f"l{l}_q", h, row(norm_mix, l), W["w_q", i], transposed=True)
            per_group = [_attn_fwd(f"l{l}_att{g}", q, kv, g, bl, hub=hub) for g in range(N_GROUPS)]
            o, lse, h2 = _combine_mm_res(f"l{l}_att_out", [p[0] for p in per_group], [p[1] for p in per_group],
                                         W["w_o", i], h)
            saved.append((h, q, o, lse, hn_t))
        a = _norm_mm(f"l{l}_up", h2, row(norm_mlp, l), W["w_up", l], out_dtype=BF16, hub=hub)
        h = _relu2_mm_res(f"l{l}_down", a, _rows_merged(W["w_down", l]), h2, hub=hub)
        saved[-1] = saved[-1] + (h2, a)

    dh, sq_err, d_norm_final = _final_loss("loss", h, tgt, norm_final.reshape(1, D))

    d_norm_mix = [None] * DEPTH
    d_norm_mlp = [None] * DEPTH
    d_conv = [None] * N_A_LAYERS
    d_norm_kv = None
    dkv_acc = [None] * N_GROUPS
    G = hub.grads
    as_slots = lambda g: g.reshape(N_DEV, g.shape[0] // N_DEV, g.shape[1])
    tt = DW_TOKENS
    for l in reversed(range(DEPTH)):
        h2, a = saved[l][-2:]
        h_in = saved[l][0]
        g_mlp = row(norm_mlp, l)
        g_mix = row(norm_mix, l)
        w_up_l = W["w_up", l]
        FF = N_DEV * w_up_l.shape[2]
        da = _nt_relu2_bwd(f"l{l}_down_bwd", dh, _rows_merged(W["w_down", l]), a, hub=hub)
        G["w_down", l] = as_slots(_tn(f"l{l}_dw_down", [_rows2(a, 2 * tt, FF // 4, lambda s: s)], _relu2,
                                      [_rows2(dh, 2 * tt)], _val, FF, D, T, 2 * tt, split=("k", 4)))
        G["w_up", l] = _tn(f"l{l}_dw_up", [_rows2(h2, 2 * tt), _full2(g_mlp)], _normed,
                           [_rows2(da, 2 * tt, FF // 4, lambda s: s)], _val, D, FF, T, 2 * tt, split=("n", 4),
                           out_cols=w_up_l.shape[2], hub=hub)
        dh2, d_norm_mlp[l] = _nt_norm_bwd(f"l{l}_up_bwd", [da], w_up_l, h2, g_mlp, dh, hub=hub)
        if l >= N_A_LAYERS:
            i = l - N_A_LAYERS
            _, q, o, lse, hn_t = saved[l][:5]
            w_o_i, w_q_i = W["w_o", i], W["w_q", i]
            do, delta = _att_out_bwd(f"l{l}_att_out_bwd", dh2, w_o_i, o)
            G["w_o", i] = _tn(f"l{l}_dw_o", [_rows2(o, tt)], _val, [_rows2(dh2, tt)], _val, QW, D, T, tt,
                              out_cols=w_o_i.shape[2])
            dqs = []
            for g in range(N_GROUPS):
                dqs.append(_attn_bwd_dq(f"l{l}_att{g}_dq", q, kv, do, delta, lse, g, bl, hub=hub))
                dkv_acc[g] = _attn_bwd_dkv(f"l{l}_att{g}_dkv", q, kv, do, delta, lse, g, bl, prev=dkv_acc[g], hub=hub)
            G["w_q", i] = _tn(f"l{l}_dw_q", [_cols2(hn_t, tt)], None,
                              [_rows2(t, tt) for t in dqs], _concat_f32, D, N_GROUPS * QW, T, tt, out_cols=w_q_i.shape[2])
            dh, d_norm_mix[l] = _nt_norm_bwd(f"l{l}_q_bwd", dqs, w_q_i, h_in, g_mix, dh2, hub=hub)
            if l == N_A_LAYERS:
                dkvs = [t for pair in dkv_acc for t in pair]
                g_kv = norm_kv.reshape(1, D)
                w_kv = W["w_kv", None]
                per_call = len(dkvs) // 2
                halves = [_tn(f"dw_kv{p}", [_cols2(hn_kv_t, tt)], None,
                              [_rows2(t, tt) for t in dkvs[p * per_call:(p + 1) * per_call]], _concat_f32,
                              D, per_call * QW, T, tt, out_cols=w_kv.shape[2]) for p in range(2)]
                G["w_kv", None] = jnp.concatenate(halves, axis=0)
                dh, d_norm_kv = _nt_norm_bwd("kv_bwd", dkvs, w_kv, h_kv, g_kv, dh, tm=MM_ROWS // 2, hub=hub)
        else:
            _, bcu, gated_t, hn_t = saved[l][:4]
            cw = _pad8(conv_w[l])
            w_in_l = W["w_a_in", l]
            dgated = _nt_plain(f"l{l}_conv_out_bwd", dh2, _rows_merged(W["w_a_out", l]))
            G["w_a_out", l] = as_slots(_tn(f"l{l}_dw_conv_out", [_cols2(gated_t, 2 * tt)], None,
                                           [_rows2(dh2, 2 * tt)], _val, D, D, T, 2 * tt, hub=hub))
            dbcu, d_conv[l] = _conv_bwd(f"l{l}_conv_bwd", bcu, dgated, cw, seq, hub=hub)
            G["w_a_in", l] = _tn(f"l{l}_dw_in", [_cols2(hn_t, 2 * tt)], None,
                                 [_rows2(dbcu, 2 * tt, 3 * D // 2, lambda s: s)], _val, D, 3 * D, T, 2 * tt, split=("n", 2),
                                 out_cols=w_in_l.shape[2], hub=hub)
            dh, d_norm_mix[l] = _nt_norm_bwd(f"l{l}_in_bwd", [dbcu], w_in_l, h_in, g_mix, dh2, hub=hub)

    small = jnp.concatenate(d_norm_mix + d_norm_mlp + [d_norm_kv, d_norm_final] + d_conv, axis=0)
    return sq_err, dh.reshape(bl, seq, D), small


def kernel(x, norm_mix, norm_mlp, w_a_in, conv_w, w_a_out, norm_kv, w_kv, w_q, w_o, w_up, w_down, norm_final, loss_target, m_norm_mix, m_norm_mlp, m_w_a_in, m_conv_w, m_w_a_out, m_norm_kv, m_w_kv, m_w_q, m_w_o, m_w_up, m_w_down, m_norm_final, v_norm_mix, v_norm_mlp, v_w_a_in, v_conv_w, v_w_a_out, v_norm_kv, v_w_kv, v_w_q, v_w_o, v_w_up, v_w_down, v_norm_final):
    D = x.shape[-1]
    xi, yi, ci = _mesh_pos()
    me_idx = 4 * xi + 2 * yi + ci
    w_big = dict(w_a_in=w_a_in, w_a_out=w_a_out, w_kv=w_kv, w_q=w_q, w_o=w_o, w_up=w_up, w_down=w_down)
    m_big = dict(w_a_in=m_w_a_in, w_a_out=m_w_a_out, w_kv=m_w_kv, w_q=m_w_q, w_o=m_w_o, w_up=m_w_up, w_down=m_w_down)
    v_big = dict(w_a_in=v_w_a_in, w_a_out=v_w_a_out, w_kv=v_w_kv, w_q=v_w_q, w_o=v_w_o, w_up=v_w_up, w_down=v_w_down)
    names = list(w_big)

    shards = {n: w.astype(BF16) for n, w in w_big.items()}
    landing = {n: lax.empty((N_DEV,) + w.shape, BF16) for n, w in w_big.items()}
    hub = _Hub(FETCH_DURING, PUSH_DURING, shards, landing)
    dc = conv_w.shape[-1]
    taps = conv_w.shape[0] * conv_w.shape[1]
    got = _all_gather("gather_first", [(shards[n], l) for n, l in FETCH_UP_FRONT] + [(_pad8(conv_w.reshape(taps, dc)), None)])
    for key, w in zip(FETCH_UP_FRONT, got):
        hub.weights[key] = w
    conv_full = jnp.moveaxis(got[-1][:, :taps], 0, 1).reshape(conv_w.shape[0], conv_w.shape[1], N_DEV * dc)

    sq_err, grad_x, small = _local_grads(x, loss_target, norm_mix, norm_mlp, norm_kv, norm_final, conv_full, hub)

    grads, deltas, new_m, new_v = {}, {}, {}, {}
    for n in names:
        shape = w_big[n].shape
        cols = shape[-1]
        flat = lambda t: t.reshape(-1, cols)
        parts = hub.landing[n].reshape(N_DEV, -1, cols)
        tr = parts.shape[1]
        while tr * cols > ADAMW_TILE and tr % 32 == 0:
            tr //= 2
        outs = _sum8_adamw(f"adamw_{n}", parts, flat(w_big[n]), flat(m_big[n]), flat(v_big[n]), tr=tr)
        grads[n], deltas[n], new_m[n], new_v[n] = (t.reshape(shape) for t in outs)

    n_gain = 2 * DEPTH + 2
    n_small = small.shape[0]
    small = jnp.concatenate([small, jnp.full((SUBLANES, D), sq_err, F32)], axis=0)
    rows_small = small.shape[0]
    small_all = _all_gather("gather_small_grads", [(small, None)])[0]

    def small_pack(nm, nl, nk, nf, cw):
        gains = jnp.concatenate([nm, nl, nk.reshape(1, D), nf.reshape(1, D)], axis=0)
        taps_full = lax.dynamic_update_slice(jnp.zeros((taps, D), F32), cw.reshape(taps, dc), (0, me_idx * dc))
        return jnp.concatenate([gains, taps_full, jnp.zeros((SUBLANES, D), F32)], axis=0)

    sp = [small_pack(*t) for t in ((norm_mix, norm_mlp, norm_kv, norm_final, conv_w),
                                   (m_norm_mix, m_norm_mlp, m_norm_kv, m_norm_final, m_conv_w),
                                   (v_norm_mix, v_norm_mlp, v_norm_kv, v_norm_final, v_conv_w))]
    small_out = _sum8_adamw("adamw_small", small_all, *sp, tr=rows_small)
    loss = small_out[0][n_small, 0] * (0.5 / D)

    def small_unpack(t):
        res = dict(norm_mix=t[0:DEPTH], norm_mlp=t[DEPTH:2 * DEPTH], norm_kv=t[2 * DEPTH], norm_final=t[2 * DEPTH + 1])
        res["conv_w"] = lax.dynamic_slice(t[n_gain:], (0, me_idx * dc), (taps, dc)).reshape(conv_w.shape)
        return res

    for dst, t in zip((grads, deltas, new_m, new_v), small_out):
        dst.update(small_unpack(t))

    order = ["norm_mix", "norm_mlp", "w_a_in", "conv_w", "w_a_out", "norm_kv", "w_kv", "w_q", "w_o", "w_up", "w_down",
             "norm_final"]
    return (loss, grad_x, *[grads[n] for n in order], *[deltas[n] for n in order], *[new_m[n] for n in order],
            *[new_v[n] for n in order])
```

```python
import jax
import jax.numpy as jnp
from jax import lax
from jax.experimental import pallas as pl
from jax.experimental.pallas import tpu as pltpu

F32 = jnp.float32
BF16 = jnp.bfloat16
SDS = jax.ShapeDtypeStruct

EPS = 1e-5
N_A_LAYERS = 2
DEPTH = 4
PATTERNS = ((128, 1), (512, 4), (2048, 16))
N_GROUPS = 3
H_G = 8
HEAD_DIM = 64
QW = H_G * HEAD_DIM
ATT_BLK = 128
ALIBI_MAX_BIAS = 8.0
NEG_INF = -1e30

ADAM_LR = 0.001
ADAM_B1 = 0.9
ADAM_B2 = 0.999
ADAM_EPS = 1e-08
ADAM_WD = 0.01
ADAM_STEP = 10

N_DEV = 8
SUBLANES = 8
HALO = 16
V7X_VMEM_LIMIT = 56 * 1024 * 1024
MXU_COLS = 256
MM_CHUNK = 512
MM_ROWS = 512
ADAMW_TILE = 256 * 1024
DW_TOKENS = 1024

FETCH_UP_FRONT = [("w_a_in", 0)]
FETCH_DURING = {
    "l0_in": [("w_a_out", 0), ("w_up", 0)], "l0_conv_out": [("w_down", 0)], "l0_up": [("w_a_in", 1), ("w_a_out", 1)], "l0_down": [("w_up", 1)],
    "l1_in": [("w_down", 1)], "l1_conv_out": [("w_kv", None)],
    "l1_up": [("w_q", 0), ("w_o", 0), ("w_q", 1), ("w_o", 1)], "l1_down": [("w_up", 2)],
    "kv": [("w_down", 2)], "l2_up": [("w_up", 3)], "l2_down": [("w_down", 3)],
}
PUSH_DURING = {
    "l3_dw_up": [("w_down", 3, 0, 2)], "l3_up_bwd": [("w_down", 3, 1, 2)],
    "l3_att0_dq": [("w_up", 3, 0, 2)], "l3_att0_dkv": [("w_up", 3, 1, 2)], "l3_att1_dq": [("w_o", 1)], "l3_q_bwd": [("w_q", 1)],
    "l2_dw_up": [("w_down", 2, 0, 2)], "l2_up_bwd": [("w_down", 2, 1, 2)],
    "l2_att0_dq": [("w_up", 2, 0, 2)], "l2_att0_dkv": [("w_up", 2, 1, 2)], "l2_att1_dq": [("w_o", 0)], "l2_q_bwd": [("w_q", 0)],
    "kv_bwd": [("w_kv", None, 0, 2)], "l1_down_bwd": [("w_kv", None, 1, 2)],
    "l1_dw_up": [("w_down", 1, 0, 2)], "l1_up_bwd": [("w_down", 1, 1, 2)], "l1_conv_bwd": [("w_up", 1, 0, 2)],
    "l1_dw_in": [("w_up", 1, 1, 2), ("w_a_out", 1)], "l1_in_bwd": [("w_a_in", 1, 0, 2)], "l0_down_bwd": [("w_a_in", 1, 1, 2)],
    "l0_dw_up": [("w_down", 0, 0, 2)], "l0_up_bwd": [("w_down", 0, 1, 2)], "l0_conv_bwd": [("w_up", 0, 0, 2)],
    "l0_dw_in": [("w_up", 0, 1, 2), ("w_a_out", 0)], "l0_in_bwd": [("w_a_in", 0)],
}


def _mesh_pos():
    return lax.axis_index("x"), lax.axis_index("y"), lax.axis_index("c")


def _flip(v, bit):
    return 1 - v if bit else v


class _Transfer:
    def __init__(self, kind, key, src, src_idx=None, dst=None, dst_idx=None, dst_shape=None, rows=None):
        self.kind, self.key, self.src, self.src_idx = kind, key, src, src_idx
        self.dst, self.dst_idx, self.dst_shape, self.rows = dst, dst_idx, dst_shape, rows

    def copies(self, src_ref, dst_ref, send_sems, recv_sems, local_sem):
        x, y, c = _mesh_pos()
        me = 4 * x + 2 * y + c
        part = (lambda r: r) if self.rows is None else (lambda r: r.at[pl.ds(*self.rows)])

        def dst_slot(j):
            r = dst_ref.at[j]
            return part(r if self.dst_idx is None else r.at[self.dst_idx])

        def copy(k, src, dst_j, to):
            return pltpu.make_async_remote_copy(
                src_ref=src, dst_ref=dst_slot(dst_j), send_sem=send_sems.at[k], recv_sem=recv_sems.at[k],
                device_id=to, device_id_type=pl.DeviceIdType.MESH)

        if self.kind == "exchange":
            local = pltpu.make_async_copy(part(src_ref.at[me]), dst_slot(me), local_sem)
            sends, arrivals = [], []
            for k in range(1, N_DEV):
                peer = (_flip(x, k & 4), _flip(y, k & 2), _flip(c, k & 1))
                peer_idx = 4 * peer[0] + 2 * peer[1] + peer[2]
                sends.append(copy(k - 1, part(src_ref.at[peer_idx]), me, peer))
                arrivals.append(copy(k - 1, part(src_ref.at[peer_idx]), peer_idx, peer))
            return local, sends, [], arrivals

        own = part(src_ref if self.src_idx is None else src_ref.at[self.src_idx])
        idx = lambda px, py, pc: 4 * px + 2 * py + pc
        sibling = (x, y, 1 - c)
        chips = [(1 - x, y), (x, 1 - y), (1 - x, 1 - y)]
        local = pltpu.make_async_copy(own, dst_slot(me), local_sem)
        sends = [copy(0, own, me, sibling)] + [copy(1 + j, own, me, (*chip, c)) for j, chip in enumerate(chips)]
        relays = [(copy(1 + j, own, idx(*chip, c), sibling), copy(4 + j, dst_slot(idx(*chip, c)), idx(*chip, c), sibling))
                  for j, chip in enumerate(chips)]
        arrivals = [copy(0, own, idx(*sibling), sibling)]
        arrivals += [copy(4 + j, own, idx(*chip, 1 - c), sibling) for j, chip in enumerate(chips)]
        return local, sends, relays, arrivals


class _Hub:
    def __init__(self, fetch, push, shards, landing):
        self.fetch, self.push, self.shards, self.landing = fetch, push, shards, landing
        self.weights = {}
        self.arriving = {}
        self.grads = {}

    def transfers(self, host):
        out = []
        for name, l, *part in self.fetch.get(host, ()):
            src = self.shards[name]
            shard = tuple(src.shape if l is None else src.shape[1:])
            p, n = part or (0, 1)
            rows = None if n == 1 else (p * (shard[0] // n), shard[0] // n)
            out.append(_Transfer("gather", (name, l, p == n - 1), src, src_idx=l, dst=self.arriving.get((name, l)),
                                 dst_shape=(N_DEV,) + shard, rows=rows))
        for name, l, *part in self.push.get(host, ()):
            src = self.grads[name, l]
            p, n = part or (0, 1)
            rows = None if n == 1 else (p * (src.shape[1] // n), src.shape[1] // n)
            out.append(_Transfer("exchange", (name, l, p == n - 1), src, dst=self.landing[name], dst_idx=l, rows=rows))
        return out

    def accept(self, transfers, results):
        for t, r in zip(transfers, results):
            name, l, complete = t.key
            if t.kind == "exchange":
                self.landing[name] = r
            elif complete:
                self.weights[name, l] = r
            else:
                self.arriving[name, l] = r


def _call(name, body, grid, ins, outs, scratch=(), hub=None):
    transfers = hub.transfers(name) if hub is not None else []
    n_in, n_out, n_scr, n_tr = len(ins), len(outs), len(scratch), len(transfers)
    c_in, c_out, aliases, places = [], [], {}, []
    for t in transfers:
        c_in.append(t.src)
        src_pos = len(c_in) - 1
        if t.dst is not None:
            c_in.append(t.dst)
            aliases[n_in + len(c_in) - 1] = n_out + len(c_out)
            c_out.append(SDS(t.dst.shape, t.dst.dtype))
        else:
            c_out.append(SDS(t.dst_shape, t.src.dtype))
        places.append((src_pos, len(c_out) - 1))
    sems = [pltpu.SemaphoreType.DMA((n_tr, N_DEV - 1)), pltpu.SemaphoreType.DMA((n_tr, N_DEV - 1)),
            pltpu.SemaphoreType.DMA((n_tr,))] if n_tr else []

    def wrapped(*refs):
        in_refs = refs[:n_in]
        cin_refs = refs[n_in:n_in + len(c_in)]
        o0 = n_in + len(c_in)
        out_refs = refs[o0:o0 + n_out]
        cout_refs = refs[o0 + n_out:o0 + n_out + len(c_out)]
        s0 = o0 + n_out + len(c_out)
        scr_refs = refs[s0:s0 + n_scr]
        if n_tr:
            send_sems, recv_sems, local_sems = refs[s0 + n_scr:]
            first = last = relay = None
            for ax, n in enumerate(grid):
                i = pl.program_id(ax)
                at_relay = (i == max(n - 2, 0)) if ax == len(grid) - 1 else (i == n - 1)
                first = (i == 0) if first is None else first & (i == 0)
                last = (i == n - 1) if last is None else last & (i == n - 1)
                relay = at_relay if relay is None else relay & at_relay

            def all_copies():
                return [t.copies(cin_refs[sp], cout_refs[dp], send_sems.at[n], recv_sems.at[n], local_sems.at[n])
                        for n, (t, (sp, dp)) in enumerate(zip(transfers, places))]

            @pl.when(first)
            def _():
                for local, sends, _, _ in all_copies():
                    local.start()
                    for cp in sends:
                        cp.start()

            def pass_on():
                @pl.when(relay)
                def _():
                    for _, _, relays, _ in all_copies():
                        for arrival, onward in relays:
                            arrival.wait_recv()
                            onward.start()

            if grid[-1] > 1:
                pass_on()

        body(*in_refs, *out_refs, *scr_refs)

        if n_tr:
            if grid[-1] == 1:
                pass_on()

            @pl.when(last)
            def _():
                for local, sends, relays, arrivals in all_copies():
                    for cp in arrivals:
                        cp.wait_recv()
                    for cp in sends + [onward for _, onward in relays]:
                        cp.wait_send()
                    local.wait()

    any_spec = pl.BlockSpec(memory_space=pl.ANY)
    res = pl.pallas_call(
        wrapped,
        name=name,
        grid=grid,
        in_specs=[s for _, s in ins] + [any_spec] * len(c_in),
        out_specs=[s for _, s in outs] + [any_spec] * len(c_out),
        out_shape=[o for o, _ in outs] + c_out,
        scratch_shapes=list(scratch) + sems,
        input_output_aliases=aliases,
        compiler_params=pltpu.CompilerParams(
            dimension_semantics=("arbitrary",) * len(grid), vmem_limit_bytes=V7X_VMEM_LIMIT),
    )(*[a for a, _ in ins], *c_in)
    if n_tr:
        hub.accept(transfers, res[n_out:])
    return res[:n_out]


def _rows(a, tm, cb=None, col=0):
    cb = cb or a.shape[1]
    return (a, pl.BlockSpec((tm, cb), lambda i: (i, col)))


def _full(a):
    nd = a.ndim
    return (a, pl.BlockSpec(a.shape, lambda i: (0,) * nd))


def _prev8(a, tm, cb, col):
    return (a, pl.BlockSpec((HALO, cb), lambda i: (jnp.maximum(i * (tm // HALO) - 1, 0), col)))


def _next8(a, tm, cb, col):
    last = a.shape[0] // HALO - 1
    return (a, pl.BlockSpec((HALO, cb), lambda i: (jnp.minimum((i + 1) * (tm // HALO), last), col)))


def _rows2(a, tt, cb=None, colfn=None):
    cb = cb or a.shape[1]
    colfn = colfn or (lambda s: 0)
    return (a, pl.BlockSpec((tt, cb), lambda s, t: (t, colfn(s))))


def _full2(a):
    nd = a.ndim
    return (a, pl.BlockSpec(a.shape, lambda s, t: (0,) * nd))


def _out_rows(T, n, dtype, tm):
    return (SDS((T, n), dtype), pl.BlockSpec((tm, n), lambda i: (i, 0)))


def _out_acc8(d):
    return (SDS((SUBLANES, d), F32), pl.BlockSpec((SUBLANES, d), lambda i: (0, 0)))


def _rstd(x):
    return lax.rsqrt(jnp.mean(x * x, axis=-1, keepdims=True) + EPS)


def _normed(h_ref, g_ref):
    x = h_ref[...]
    return x * _rstd(x) * g_ref[...]


def _acc8(ref, val, i, n):
    part = val.reshape(-1, SUBLANES, val.shape[-1]).sum(axis=0)

    @pl.when(i == 0)
    def _():
        ref[...] = part

    @pl.when(i > 0)
    def _():
        ref[...] += part

    @pl.when(i == n - 1)
    def _():
        ref[...] = jnp.broadcast_to(jnp.sum(ref[...], axis=0, keepdims=True), ref.shape)


def _gate(b_ref, c_ref, u_ref, ch_ref, uh_ref, cw_ref, first):
    b, c, u = (r[...].astype(F32) for r in (b_ref, c_ref, u_ref))
    cu = c * u
    halo = jnp.where(first, 0.0, ch_ref[...].astype(F32) * uh_ref[...].astype(F32))
    rows = lax.broadcasted_iota(jnp.int32, cu.shape, 0)
    h1 = halo[HALO - 1:HALO, :]
    h2 = halo[HALO - 2:HALO - 1, :]
    cu1 = jnp.where(rows == 0, h1, pltpu.roll(cu, 1, 0))
    cu2 = jnp.where(rows == 0, h2, jnp.where(rows == 1, h1, pltpu.roll(cu, 2, 0)))
    conv = cw_ref[0:1, :] * cu + cw_ref[1:2, :] * cu1 + cw_ref[2:3, :] * cu2
    return b * conv, (b, c, u), conv, (cu, cu1, cu2)


def _relu2(a_ref):
    r = jnp.maximum(a_ref[...].astype(F32), 0.0)
    return r * r


def _dot(a, b):
    return jnp.dot(a, b, preferred_element_type=F32)


def _dot_nt(a, b):
    return lax.dot_general(a, b, (((1,), (1,)), ((), ())), preferred_element_type=F32)


def _chunks(n):
    c = min(MM_CHUNK, n)
    while n % c:
        c -= 128
    assert c > 0, n
    return [(k * c, (k + 1) * c) for k in range(n // c)]


def _col_weight(w):
    _, K, ns = w.shape
    N = N_DEV * ns
    direct = ns % MXU_COLS == 0
    scratch = [] if direct else [pltpu.VMEM((K, N), BF16)]

    def prepare(w_ref, s_ref, step):
        if direct:
            return

        @pl.when(step == 0)
        def _():
            for j in range(N_DEV):
                s_ref[:, j * ns:(j + 1) * ns] = w_ref[j]

    def chunks(w_ref, s_ref):
        if direct:
            return [(j * ns, (j + 1) * ns, (lambda j=j: w_ref[j])) for j in range(N_DEV)]
        return [(lo, hi, (lambda lo=lo, hi=hi: s_ref[:, lo:hi])) for lo, hi in _chunks(N)]

    return N, scratch, prepare, chunks


def _out_cols(n, T, tm):
    return (SDS((n, T), BF16), pl.BlockSpec((n, tm), lambda i: (0, i)))


def _norm_mm(name, h, g, w, tm=MM_ROWS, out_dtype=F32, transposed=False, hub=None):
    T, D = h.shape
    N, w_scratch, prepare, chunks = _col_weight(w)

    def body(h_ref, g_ref, w_ref, o_ref, *rest):
        at_ref, s = (rest[0], rest[1:]) if transposed else (None, rest)
        s_ref = s[0] if s else None
        prepare(w_ref, s_ref, pl.program_id(0))
        a32 = _normed(h_ref, g_ref)
        a = a32.astype(BF16)
        for lo, hi, load in chunks(w_ref, s_ref):
            o_ref[:, lo:hi] = _dot(a, load()).astype(out_dtype)
        if transposed:
            at_ref[...] = a32.T.astype(BF16)

    outs = [_out_rows(T, N, out_dtype, tm)] + ([_out_cols(D, T, tm)] if transposed else [])
    res = _call(name, body, (T // tm,), [_rows(h, tm), _full(g), _full(w)], outs, scratch=w_scratch, hub=hub)
    return res if transposed else res[0]


def _gate_mm_res(name, bcu, cw, w, h, seq, tm=MM_ROWS, hub=None):
    T, D = h.shape

    def body(b_ref, c_ref, u_ref, ch_ref, uh_ref, cw_ref, w_ref, h_ref, o_ref, gt_ref):
        first = (pl.program_id(0) * tm) % seq == 0
        gated32 = _gate(b_ref, c_ref, u_ref, ch_ref, uh_ref, cw_ref, first)[0]
        gated = gated32.astype(BF16)
        for lo, hi in _chunks(D):
            o_ref[:, lo:hi] = h_ref[:, lo:hi] + _dot(gated, w_ref[:, lo:hi])
        gt_ref[...] = gated32.T.astype(BF16)

    ins = [_rows(bcu, tm, D, 0), _rows(bcu, tm, D, 1), _rows(bcu, tm, D, 2), _prev8(bcu, tm, D, 1),
           _prev8(bcu, tm, D, 2), _full(cw), _full(w), _rows(h, tm)]
    return _call(name, body, (T // tm,), ins, [_out_rows(T, D, F32, tm), _out_cols(D, T, tm)], hub=hub)


def _relu2_mm_res(name, a, w, h, tm=MM_ROWS, hub=None):
    T, D = h.shape
    K = a.shape[1]

    def body(a_ref, w_ref, h_ref, o_ref, acc_ref):
        for n, (lo, hi) in enumerate(_chunks(K)):
            d = _dot(_relu2(a_ref.at[:, lo:hi]).astype(BF16), w_ref[lo:hi, :])
            if n == 0:
                acc_ref[...] = d
            else:
                acc_ref[...] += d
        o_ref[...] = h_ref[...] + acc_ref[...]

    return _call(name, body, (T // tm,), [_rows(a, tm), _full(w), _rows(h, tm)], [_out_rows(T, D, F32, tm)],
                 scratch=[pltpu.VMEM((tm, D), F32)], hub=hub)[0]


def _combine_mm_res(name, os_, lses, w, h, tm=MM_ROWS):
    T, D = h.shape
    _, w_scratch, prepare, chunks = _col_weight(w)

    def body(o0, o1, o2, l0, l1, l2, w_ref, h_ref, o_ref, lse_ref, out_ref, *s):
        s_ref = s[0] if s else None
        prepare(w_ref, s_ref, pl.program_id(0))
        ls = [l0[...], l1[...], l2[...]]
        mx = jnp.maximum(jnp.maximum(ls[0], ls[1]), ls[2])
        es = [jnp.exp(l - mx) for l in ls]
        den = es[0] + es[1] + es[2]
        o = (es[0] * o0[...] + es[1] * o1[...] + es[2] * o2[...]) / den
        o_ref[...] = o
        lse_ref[...] = mx + jnp.log(den)
        ob = o.astype(BF16)
        for lo, hi, load in chunks(w_ref, s_ref):
            out_ref[:, lo:hi] = h_ref[:, lo:hi] + _dot(ob, load())

    ins = [_rows(t, tm) for t in list(os_) + list(lses)] + [_full(w), _rows(h, tm)]
    outs = [_out_rows(T, QW, F32, tm), _out_rows(T, QW, F32, tm), _out_rows(T, D, F32, tm)]
    return _call(name, body, (T // tm,), ins, outs, scratch=w_scratch)


def _nt_relu2_bwd(name, dh, w, a, tm=MM_ROWS, hub=None):
    T, _ = dh.shape
    K = w.shape[0]

    def body(dh_ref, w_ref, a_ref, o_ref):
        d = dh_ref[...].astype(BF16)
        for lo, hi in _chunks(K):
            dr = _dot_nt(d, w_ref[lo:hi, :])
            o_ref[:, lo:hi] = (dr * (2.0 * jnp.maximum(a_ref[:, lo:hi].astype(F32), 0.0))).astype(BF16)

    return _call(name, body, (T // tm,), [_rows(dh, tm), _full(w), _rows(a, tm)], [_out_rows(T, K, BF16, tm)], hub=hub)[0]


def _concat_bf16(*refs):
    vals = [r[...].astype(BF16) for r in refs]
    return vals[0] if len(vals) == 1 else jnp.concatenate(vals, axis=1)


def _nt_plain(name, dy, w, tm=MM_ROWS):
    T, N = dy.shape
    K = w.shape[0]

    def body(dy_ref, w_ref, o_ref, acc_ref):
        for n, (lo, hi) in enumerate(_chunks(N)):
            d = _dot_nt(dy_ref[:, lo:hi].astype(BF16), w_ref[:, lo:hi])
            if n == 0:
                acc_ref[...] = d
            else:
                acc_ref[...] += d
        o_ref[...] = acc_ref[...]

    return _call(name, body, (T // tm,), [_rows(dy, tm), _full(w)], [_out_rows(T, K, F32, tm)],
                 scratch=[pltpu.VMEM((tm, K), F32)])[0]


def _att_out_bwd(name, dy, w, o, tm=MM_ROWS):
    T, _ = dy.shape
    K = w.shape[1]
    _, w_scratch, prepare, chunks = _col_weight(w)

    def body(dy_ref, w_ref, o_ref, do_ref, dl_ref, acc_ref, *s):
        s_ref = s[0] if s else None
        prepare(w_ref, s_ref, pl.program_id(0))
        for n, (lo, hi, load) in enumerate(chunks(w_ref, s_ref)):
            d = _dot_nt(dy_ref[:, lo:hi].astype(BF16), load())
            if n == 0:
                acc_ref[...] = d
            else:
                acc_ref[...] += d
        do = acc_ref[...]
        do_ref[...] = do
        prod = do * o_ref[...]
        high = prod.astype(BF16)
        low = (prod - high.astype(F32)).astype(BF16)
        head_of = lambda axis: jnp.right_shift(lax.broadcasted_iota(jnp.int32, (K, K), axis), HEAD_DIM.bit_length() - 1)
        same_head = jnp.where(head_of(0) == head_of(1), 1.0, 0.0).astype(BF16)
        dl_ref[...] = _dot(high, same_head) + _dot(low, same_head)

    outs = [_out_rows(T, K, F32, tm), _out_rows(T, K, F32, tm)]
    return _call(name, body, (T // tm,), [_rows(dy, tm), _full(w), _rows(o, tm)], outs,
                 scratch=[pltpu.VMEM((tm, K), F32)] + w_scratch)


def _nt_norm_bwd(name, dys, w, h, g, dh_in, tm=MM_ROWS, hub=None):
    T, D = h.shape
    _, w_scratch, prepare, chunks = _col_weight(w)
    n_steps = T // tm
    n_dy = len(dys)

    def body(*refs):
        dy_refs = refs[:n_dy]
        w_ref, h_ref, g_ref, dhin_ref, o_ref, dg_ref, acc_ref = refs[n_dy:n_dy + 7]
        s_ref = refs[n_dy + 7] if len(refs) > n_dy + 7 else None
        i = pl.program_id(0)
        prepare(w_ref, s_ref, i)
        dy = _concat_bf16(*dy_refs)
        for n, (lo, hi, load) in enumerate(chunks(w_ref, s_ref)):
            d = _dot_nt(dy[:, lo:hi], load())
            if n == 0:
                acc_ref[...] = d
            else:
                acc_ref[...] += d
        dn = acc_ref[...]
        x = h_ref[...]
        rstd = _rstd(x)
        xhat = x * rstd
        dxhat = dn * g_ref[...]
        dx = rstd * (dxhat - xhat * jnp.mean(dxhat * xhat, axis=-1, keepdims=True))
        o_ref[...] = dhin_ref[...] + dx
        _acc8(dg_ref, dn * xhat, i, n_steps)

    ins = [_rows(d, tm) for d in dys] + [_full(w), _rows(h, tm), _full(g), _rows(dh_in, tm)]
    outs = [_out_rows(T, D, F32, tm), _out_acc8(D)]
    dh, dg = _call(name, body, (n_steps,), ins, outs, scratch=[pltpu.VMEM((tm, D), F32)] + w_scratch, hub=hub)
    return dh, dg[0:1]


def _cols2(a_t, tt, kb=None):
    kb = kb or a_t.shape[0]
    return (a_t, pl.BlockSpec((kb, tt), (lambda s, t: (s, t)) if kb != a_t.shape[0] else (lambda s, t: (0, t))))


def _tn(name, a_ins, a_fn, y_ins, y_fn, K, N, T, tt, split=None, out_cols=None, hub=None):
    kind, parts = split or ("n", 1)
    kb, nb = (K // parts, N) if kind == "k" else (K, N // parts)
    n_steps = T // tt
    n_a = len(a_ins)
    n_y = len(y_ins)
    assert out_cols is None or (kind == "n" and nb % out_cols == 0)

    def body(*refs):
        a_refs = refs[:n_a]
        y_refs = refs[n_a:n_a + n_y]
        o_ref, acc_ref = refs[n_a + n_y:]
        t = pl.program_id(1)
        a_t = a_refs[0][...] if a_fn is None else a_fn(*a_refs).T.astype(BF16)
        y = y_fn(*y_refs).astype(BF16)
        for lo, hi in _chunks(nb):
            d = _dot(a_t, y[:, lo:hi])

            @pl.when(t == 0)
            def _():
                acc_ref[:, lo:hi] = d

            @pl.when(t > 0)
            def _():
                acc_ref[:, lo:hi] += d

        @pl.when(t == n_steps - 1)
        def _():
            if out_cols is None:
                o_ref[...] = acc_ref[...].astype(BF16)
            else:
                for j in range(nb // out_cols):
                    o_ref[j] = acc_ref[:, j * out_cols:(j + 1) * out_cols].astype(BF16)

    if out_cols is None:
        out = (SDS((K, N), BF16), pl.BlockSpec((kb, nb), (lambda s, t: (s, 0)) if kind == "k" else (lambda s, t: (0, s))))
    else:
        out = (SDS((N // out_cols, K, out_cols), BF16), pl.BlockSpec((nb // out_cols, K, out_cols), lambda s, t: (s, 0, 0)))
    return _call(name, body, (parts, n_steps), list(a_ins) + list(y_ins), [out],
                 scratch=[pltpu.VMEM((kb, nb), F32)], hub=hub)[0]


def _val(ref):
    return ref[...]


def _concat_f32(*refs):
    vals = [r[...] for r in refs]
    return vals[0] if len(vals) == 1 else jnp.concatenate(vals, axis=1)


ATT_TILE_ROWS = 2048
ATT_TILE_ROWS_Q = 4096
HEAD_PAIRS = H_G // 2
ATT_SCALE = HEAD_DIM ** -0.5
ATT_UNITS_TOGETHER = 4


def _slope(h):
    return 2.0 ** (-ALIBI_MAX_BIAS * (h + 1) / H_G)


def _att_geom(T, bl, g, tile_rows=ATT_TILE_ROWS):
    dil = PATTERNS[g][1]
    sub = ATT_BLK * dil
    nsub = max(1, min(tile_rows, T // bl) // sub)
    rows = sub * nsub
    return dil, sub, nsub, rows, T // bl // rows


def _att_specs(T, bl, g, tile_rows=ATT_TILE_ROWS):
    _, sub, nsub, rows, nt = _att_geom(T, bl, g, tile_rows)
    last_sub = T // sub - 1
    tile = lambda col: pl.BlockSpec((rows, 128), lambda b, i, hp: (b * nt + i, col(hp)))
    prev = lambda col: pl.BlockSpec((sub, 128), lambda b, i, hp: (jnp.maximum((b * nt + i) * nsub - 1, 0), col(hp)))
    nxt = lambda col: pl.BlockSpec((sub, 128), lambda b, i, hp: (jnp.minimum((b * nt + i + 1) * nsub, last_sub), col(hp)))
    return tile, prev, nxt


def _sub_rows(j, r, dil):
    start = j * ATT_BLK * dil + r
    return pl.ds(start, ATT_BLK, stride=dil) if dil > 1 else pl.ds(start, ATT_BLK)


class _Residues:
    def __init__(self, dil):
        self.dil = dil
        self.whole = dil % SUBLANES == 0
        self.read, self.written = {}, {}

    def _block(self, j):
        return pl.ds(j * ATT_BLK * self.dil, ATT_BLK * self.dil)

    def load(self, ref, j, r):
        if not self.whole:
            return ref[_sub_rows(j, r, self.dil), :]
        if (id(ref), j) not in self.read:
            rows = ref[self._block(j), :]
            self.read[id(ref), j] = jnp.swapaxes(rows.reshape(ATT_BLK, self.dil, rows.shape[-1]), 0, 1)
        return self.read[id(ref), j][r]

    def store(self, ref, j, r, val):
        if not self.whole:
            ref[_sub_rows(j, r, self.dil), :] = val
            return
        got = self.written.setdefault((id(ref), j), {})
        got[r] = val
        if len(got) == self.dil:
            merged = jnp.swapaxes(jnp.stack([got[k] for k in range(self.dil)], axis=0), 0, 1)
            ref[self._block(j), :] = merged.reshape(ATT_BLK * self.dil, val.shape[-1])
            del self.written[id(ref), j]


def _att_consts(hp, dil, keys_first=False):
    h0 = lax.broadcasted_iota(jnp.int32, (ATT_BLK, 128), 1) < HEAD_DIM
    a = lax.broadcasted_iota(jnp.int32, (ATT_BLK, ATT_BLK), 1 if keys_first else 0)
    c = lax.broadcasted_iota(jnp.int32, (ATT_BLK, ATT_BLK), 0 if keys_first else 1)
    dist_p = ((ATT_BLK + a - c) * dil).astype(F32)
    dist_c = ((a - c) * dil).astype(F32)
    bias_p, bias_c = [], []
    for h in range(2):
        slope = jnp.float32(_slope(2 * (HEAD_PAIRS - 1) + h))
        for p in range(HEAD_PAIRS - 2, -1, -1):
            slope = jnp.where(hp == p, jnp.float32(_slope(2 * p + h)), slope)
        bias_p.append(jnp.where(c >= a, -slope * dist_p, NEG_INF))
        bias_c.append(jnp.where(c <= a, -slope * dist_c, NEG_INF))
    return h0, jnp.concatenate(bias_p, axis=0), jnp.concatenate(bias_c, axis=0)


def _stack_heads(x, h0):
    return jnp.concatenate([jnp.where(h0, x, 0.0), jnp.where(h0, 0.0, x)], axis=0).astype(BF16)


def _unstack_heads(x, h0):
    return jnp.where(h0, x[:ATT_BLK], x[ATT_BLK:])


def _stack_cols(x):
    return jnp.concatenate(_head_cols(x), axis=0)


def _head_cols(x):
    return [x[:, 0:1], x[:, HEAD_DIM:HEAD_DIM + 1]]


def _in_groups(units, first_stage, *later_stages):
    for u0 in range(0, len(units), ATT_UNITS_TOGETHER):
        staged = [first_stage(*u) for u in units[u0:u0 + ATT_UNITS_TOGETHER]]
        for stage in later_stages:
            staged = [stage(*s) for s in staged]


def _attn_fwd(name, q, kv, g, bl, hub=None):
    T = q.shape[0]
    dil, _, nsub, _, nt = _att_geom(T, bl, g, ATT_TILE_ROWS_Q)
    tile, prev, _ = _att_specs(T, bl, g, ATT_TILE_ROWS_Q)
    halo = nt > 1

    def body(*refs):
        if halo:
            q_ref, kp_ref, kc_ref, vp_ref, vc_ref, o_ref, lse_ref = refs
        else:
            q_ref, kc_ref, vc_ref, o_ref, lse_ref = refs
        first = pl.program_id(1) == 0
        h0, bias_p, bias_c = _att_consts(pl.program_id(2), dil)
        bias_first = jnp.where(first, NEG_INF, bias_p)
        rows = _Residues(dil)

        def with_ones(v):
            return [jnp.where(h0, v, 1.0).astype(BF16), jnp.where(h0, 1.0, v).astype(BF16)]

        def scores(j, r):
            parts = [(rows.load(kc_ref, j, r), rows.load(vc_ref, j, r), bias_c)]
            if j > 0:
                parts.append((rows.load(kc_ref, j - 1, r), rows.load(vc_ref, j - 1, r), bias_p))
            elif halo:
                parts.append((rows.load(kp_ref, 0, r), rows.load(vp_ref, 0, r), bias_first))
            qs = _stack_heads(rows.load(q_ref, j, r) * ATT_SCALE, h0)
            return (j, r), [_dot_nt(qs, k.astype(BF16)) + b for k, _, b in parts], [with_ones(v) for _, v, _ in parts]

        def weights(unit, s, vals):
            mx = jnp.max(s[0] if len(s) == 1 else jnp.maximum(s[0], s[1]), axis=-1, keepdims=True)
            return unit, mx, [jnp.exp(x - mx).astype(BF16) for x in s], vals

        def outputs(unit, mx, e, vals):
            heads = [slice(h * ATT_BLK, (h + 1) * ATT_BLK) for h in range(2)]
            acc = []
            for h, hs in enumerate(heads):
                terms = [_dot(x[hs], v[h]) for x, v in zip(e, vals)]
                acc.append(terms[0] if len(terms) == 1 else terms[0] + terms[1])
            den = [pltpu.roll(a, HEAD_DIM, 1) for a in acc]
            rows.store(o_ref, *unit, jnp.where(h0, acc[0] / den[0], acc[1] / den[1]))
            rows.store(lse_ref, *unit, jnp.where(h0, mx[heads[0]] + jnp.log(den[0]), mx[heads[1]] + jnp.log(den[1])))
            return ()

        _in_groups([(j, r) for j in range(nsub) for r in range(dil)], scores, weights, outputs)

    kcol, vcol = (lambda hp: 8 * g + hp), (lambda hp: 8 * g + 4 + hp)
    ins = [(q, tile(lambda hp: 4 * g + hp))] + ([(kv, prev(kcol))] if halo else []) + [(kv, tile(kcol))]
    ins += ([(kv, prev(vcol))] if halo else []) + [(kv, tile(vcol))]
    out = (SDS((T, QW), F32), tile(lambda hp: hp))
    return _call(name, body, (bl, nt, HEAD_PAIRS), ins, [out, out], hub=hub)


def _attn_bwd_dq(name, q, kv, do, delta, lse, g, bl, hub=None):
    T = q.shape[0]
    dil, _, nsub, _, nt = _att_geom(T, bl, g, ATT_TILE_ROWS_Q)
    tile, prev, _ = _att_specs(T, bl, g, ATT_TILE_ROWS_Q)
    halo = nt > 1

    def body(*refs):
        if halo:
            q_ref, kp_ref, kc_ref, vp_ref, vc_ref, do_ref, dl_ref, lse_ref, dq_ref = refs
        else:
            q_ref, kc_ref, vc_ref, do_ref, dl_ref, lse_ref, dq_ref = refs
        first = pl.program_id(1) == 0
        h0, bias_p, bias_c = _att_consts(pl.program_id(2), dil)
        bias_first = jnp.where(first, NEG_INF, bias_p)
        rows = _Residues(dil)

        def probs(j, r):
            parts = [(rows.load(kc_ref, j, r), rows.load(vc_ref, j, r), bias_c)]
            if j > 0:
                parts.append((rows.load(kc_ref, j - 1, r), rows.load(vc_ref, j - 1, r), bias_p))
            elif halo:
                parts.append((rows.load(kp_ref, 0, r), rows.load(vp_ref, 0, r), bias_first))
            qs = _stack_heads(rows.load(q_ref, j, r) * ATT_SCALE, h0)
            dos = _stack_heads(rows.load(do_ref, j, r), h0)
            lse = _stack_cols(rows.load(lse_ref, j, r))
            keys = [k.astype(BF16) for k, _, _ in parts]
            p = [jnp.exp(_dot_nt(qs, k) + b - lse) for k, (_, _, b) in zip(keys, parts)]
            dp = [_dot_nt(dos, v.astype(BF16)) for _, v, _ in parts]
            return (j, r), p, dp, keys

        def dscores(unit, p, dp, keys):
            dl = _stack_cols(rows.load(dl_ref, *unit))
            return unit, [(x * (y - dl)).astype(BF16) for x, y in zip(p, dp)], keys

        def outputs(unit, ds, keys):
            terms = [_dot(x, k) for x, k in zip(ds, keys)]
            dq = terms[0] if len(terms) == 1 else terms[0] + terms[1]
            rows.store(dq_ref, *unit, _unstack_heads(dq, h0) * ATT_SCALE)
            return ()

        _in_groups([(j, r) for j in range(nsub) for r in range(dil)], probs, dscores, outputs)

    own = lambda hp: hp
    kcol, vcol = (lambda hp: 8 * g + hp), (lambda hp: 8 * g + 4 + hp)
    ins = [(q, tile(lambda hp: 4 * g + hp))] + ([(kv, prev(kcol))] if halo else []) + [(kv, tile(kcol))]
    ins += ([(kv, prev(vcol))] if halo else []) + [(kv, tile(vcol))]
    ins += [(do, tile(own)), (delta, tile(own)), (lse, tile(own))]
    return _call(name, body, (bl, nt, HEAD_PAIRS), ins, [(SDS((T, QW), F32), tile(own))], hub=hub)[0]


def _attn_bwd_dkv(name, q, kv, do, delta, lse, g, bl, prev=None, hub=None):
    T = q.shape[0]
    dil, _, nsub, _, nt = _att_geom(T, bl, g, ATT_TILE_ROWS_Q)
    tile, _, nxt = _att_specs(T, bl, g, ATT_TILE_ROWS_Q)
    has_prev = prev is not None
    halo = nt > 1

    def body(*refs):
        if halo:
            k_ref, v_ref, q_ref, qn_ref, do_ref, don_ref, dl_ref, dln_ref, l_ref, ln_ref = refs[:10]
            rest = refs[10:]
        else:
            k_ref, v_ref, q_ref, do_ref, dl_ref, l_ref = refs[:6]
            rest = refs[6:]
        if has_prev:
            dkp_ref, dvp_ref, dk_ref, dv_ref = rest
        else:
            dk_ref, dv_ref = rest
        last = pl.program_id(1) == nt - 1
        h0, bias_p, bias_c = _att_consts(pl.program_id(2), dil, keys_first=True)
        bias_last = jnp.where(last, NEG_INF, bias_p)
        rows = _Residues(dil)

        def per_query_rows(x):
            xt = x.T
            return jnp.concatenate([jnp.broadcast_to(xt[0:1], (ATT_BLK, ATT_BLK)),
                                    jnp.broadcast_to(xt[HEAD_DIM:HEAD_DIM + 1], (ATT_BLK, ATT_BLK))], axis=0)

        def probs(j, r):
            ks = _stack_heads(rows.load(k_ref, j, r), h0)
            vs = _stack_heads(rows.load(v_ref, j, r), h0)
            sets = [(q_ref, do_ref, dl_ref, l_ref, j, bias_c)]
            if j < nsub - 1:
                sets.append((q_ref, do_ref, dl_ref, l_ref, j + 1, bias_p))
            elif halo:
                sets.append((qn_ref, don_ref, dln_ref, ln_ref, 0, bias_last))
            out = []
            for qr, dor, dlr, lr, jq, bias in sets:
                qsb = (rows.load(qr, jq, r) * ATT_SCALE).astype(BF16)
                do2b = rows.load(dor, jq, r).astype(BF16)
                p = jnp.exp(_dot_nt(ks, qsb) + bias - per_query_rows(rows.load(lr, jq, r)))
                out.append((p, _dot_nt(vs, do2b), dlr, jq, qsb, do2b))
            return (j, r), out

        def dscores(unit, sets):
            out = []
            for p, dp, dlr, jq, qsb, do2b in sets:
                ds = (p * (dp - per_query_rows(rows.load(dlr, jq, unit[1])))).astype(BF16)
                out.append((p.astype(BF16), ds, qsb, do2b))
            return unit, out

        def outputs(unit, sets):
            dk_st = dv_st = None
            for pb, ds, qsb, do2b in sets:
                dvs, dks = _dot(pb, do2b), _dot(ds, qsb)
                dv_st = dvs if dv_st is None else dv_st + dvs
                dk_st = dks if dk_st is None else dk_st + dks
            dk2 = _unstack_heads(dk_st, h0)
            dv2 = _unstack_heads(dv_st, h0)
            if has_prev:
                dk2 = dk2 + rows.load(dkp_ref, *unit)
                dv2 = dv2 + rows.load(dvp_ref, *unit)
            rows.store(dk_ref, *unit, dk2)
            rows.store(dv_ref, *unit, dv2)
            return ()

        _in_groups([(j, r) for j in range(nsub) for r in range(dil)], probs, dscores, outputs)

    own = lambda hp: hp
    qcol = lambda hp: 4 * g + hp
    ins = [(kv, tile(lambda hp: 8 * g + hp)), (kv, tile(lambda hp: 8 * g + 4 + hp))]
    for t, col in ((q, qcol), (do, own), (delta, own), (lse, own)):
        ins += [(t, tile(col))] + ([(t, nxt(col))] if halo else [])
    if has_prev:
        ins += [(prev[0], tile(own)), (prev[1], tile(own))]
    out = (SDS((T, QW), F32), tile(own))
    return _call(name, body, (bl, nt, HEAD_PAIRS), ins, [out, out], hub=hub)


def _final_loss(name, h, tgt, g, tm=MM_ROWS):
    T, D = h.shape
    n_steps = T // tm

    def body(h_ref, t_ref, g_ref, dh_ref, loss_ref, dg_ref, sq_ref):
        i = pl.program_id(0)
        x = h_ref[...]
        rstd = _rstd(x)
        xhat = x * rstd
        err = xhat * g_ref[...] - t_ref[...]
        _acc8(sq_ref, err * err, i, n_steps)
        dy = err * (1.0 / D)
        dxhat = dy * g_ref[...]
        dh_ref[...] = rstd * (dxhat - xhat * jnp.mean(dxhat * xhat, axis=-1, keepdims=True))
        _acc8(dg_ref, dy * xhat, i, n_steps)

        @pl.when(i == n_steps - 1)
        def _():
            loss_ref[...] = jnp.full(loss_ref.shape, jnp.sum(sq_ref[0:1, :]), F32)

    outs = [_out_rows(T, D, F32, tm), (SDS((SUBLANES, 128), F32), pl.BlockSpec((SUBLANES, 128), lambda i: (0, 0))),
            _out_acc8(D)]
    dh, loss, dg = _call(name, body, (n_steps,), [_rows(h, tm), _rows(tgt, tm), _full(g)], outs,
                         scratch=[pltpu.VMEM((SUBLANES, D), F32)])
    return dh, loss[0, 0], dg[0:1]


def _conv_bwd(name, bcu, dgated, cw, seq, tm=MM_ROWS, hub=None):
    T, D = dgated.shape
    n_steps = T // tm

    def body(b_ref, c_ref, u_ref, ch_ref, uh_ref, dg_ref, dgn_ref, bn_ref, cw_ref, o_ref, t0_ref, t1_ref, t2_ref):
        i = pl.program_id(0)
        first = (i * tm) % seq == 0
        last = ((i + 1) * tm) % seq == 0
        _, (b, c, u), conv, (cu, cu1, cu2) = _gate(b_ref, c_ref, u_ref, ch_ref, uh_ref, cw_ref, first)
        dgat = dg_ref[...]
        dconv = dgat * b
        nxt = jnp.where(last, 0.0, dgn_ref[...] * bn_ref[...].astype(F32))
        rows = lax.broadcasted_iota(jnp.int32, dconv.shape, 0)
        n1 = nxt[0:1, :]
        n2 = nxt[1:2, :]
        dc1 = jnp.where(rows == tm - 1, n1, pltpu.roll(dconv, tm - 1, 0))
        dc2 = jnp.where(rows == tm - 1, n2, jnp.where(rows == tm - 2, n1, pltpu.roll(dconv, tm - 2, 0)))
        dcu = cw_ref[0:1, :] * dconv + cw_ref[1:2, :] * dc1 + cw_ref[2:3, :] * dc2
        o_ref[:, 0:D] = (dgat * conv).astype(BF16)
        o_ref[:, D:2 * D] = (dcu * u).astype(BF16)
        o_ref[:, 2 * D:3 * D] = (dcu * c).astype(BF16)
        _acc8(t0_ref, dconv * cu, i, n_steps)
        _acc8(t1_ref, dconv * cu1, i, n_steps)
        _acc8(t2_ref, dconv * cu2, i, n_steps)

    ins = [_rows(bcu, tm, D, 0), _rows(bcu, tm, D, 1), _rows(bcu, tm, D, 2), _prev8(bcu, tm, D, 1), _prev8(bcu, tm, D, 2),
           _rows(dgated, tm), _next8(dgated, tm, D, 0), _next8(bcu, tm, D, 0), _full(cw)]
    outs = [_out_rows(T, 3 * D, BF16, tm), _out_acc8(D), _out_acc8(D), _out_acc8(D)]
    dbcu, t0, t1, t2 = _call(name, body, (n_steps,), ins, outs, hub=hub)
    return dbcu, jnp.concatenate([t0[0:1], t1[0:1], t2[0:1]], axis=0)


def _sum8_adamw(name, parts, w, m, v, tr):
    R, C = w.shape
    b1c = 1.0 - ADAM_B1 ** ADAM_STEP
    b2c = 1.0 - ADAM_B2 ** ADAM_STEP

    def body(p_ref, w_ref, m_ref, v_ref, g_ref, d_ref, nm_ref, nv_ref):
        g = p_ref[0].astype(F32)
        for j in range(1, N_DEV):
            g = g + p_ref[j].astype(F32)
        nm = ADAM_B1 * m_ref[...] + (1.0 - ADAM_B1) * g
        nv = ADAM_B2 * v_ref[...] + (1.0 - ADAM_B2) * (g * g)
        m_hat = nm / b1c
        v_hat = nv / b2c
        g_ref[...] = g
        d_ref[...] = -ADAM_LR * (m_hat / (jnp.sqrt(v_hat) + ADAM_EPS) + ADAM_WD * w_ref[...])
        nm_ref[...] = nm
        nv_ref[...] = nv

    ins = [(parts, pl.BlockSpec((N_DEV, tr, C), lambda i: (0, i, 0))), _rows(w, tr), _rows(m, tr), _rows(v, tr)]
    outs = [_out_rows(R, C, F32, tr)] * 4
    return _call(name, body, (R // tr,), ins, outs)


def _all_gather(name, items):
    n = len(items)
    shapes = [tuple(a.shape if idx is None else a.shape[1:]) for a, idx in items]

    def body(*refs):
        x_refs, out_refs = refs[:n], refs[n:2 * n]
        send_sems, recv_sems, local_sems = refs[2 * n:]
        x, y, c = _mesh_pos()
        me, sibling = (x, y, c), (x, y, 1 - c)
        chips = [(1 - x, y), (x, 1 - y), (1 - x, 1 - y)]

        def copy(t, k, block, to, own=False):
            dst = out_refs[t].at[4 * block[0] + 2 * block[1] + block[2]]
            src = dst
            if own:
                src = x_refs[t] if items[t][1] is None else x_refs[t].at[items[t][1]]
            return pltpu.make_async_remote_copy(
                src_ref=src, dst_ref=dst, send_sem=send_sems.at[t, k], recv_sem=recv_sems.at[t, k],
                device_id=to, device_id_type=pl.DeviceIdType.MESH)

        started = []
        for t in range(n):
            src = x_refs[t] if items[t][1] is None else x_refs[t].at[items[t][1]]
            mine = pltpu.make_async_copy(src, out_refs[t].at[4 * x + 2 * y + c], local_sems.at[t])
            mine.start()
            first = [copy(t, 0, me, sibling, own=True)]
            first += [copy(t, 1 + j, me, (*chip, c), own=True) for j, chip in enumerate(chips)]
            for cp in first:
                cp.start()
            started.append((mine, first))
        passed = []
        for t in range(n):
            for j, chip in enumerate(chips):
                copy(t, 1 + j, (*chip, c), me).wait_recv()
                fwd = copy(t, 4 + j, (*chip, c), sibling)
                fwd.start()
                passed.append(fwd)
        for t in range(n):
            copy(t, 0, sibling, me).wait_recv()
            for j, chip in enumerate(chips):
                copy(t, 4 + j, (*chip, 1 - c), me).wait_recv()
        for mine, first in started:
            for cp in first:
                cp.wait_send()
            mine.wait()
        for cp in passed:
            cp.wait_send()

    any_spec = pl.BlockSpec(memory_space=pl.ANY)
    return pl.pallas_call(
        body, name=name,
        out_shape=[SDS((N_DEV,) + s, a.dtype) for s, (a, _) in zip(shapes, items)],
        in_specs=[any_spec] * n,
        out_specs=[any_spec] * n,
        scratch_shapes=[pltpu.SemaphoreType.DMA((n, 7)), pltpu.SemaphoreType.DMA((n, 7)), pltpu.SemaphoreType.DMA((n,))],
    )(*[a for a, _ in items])


def _pad8(t):
    return jnp.pad(t, ((0, SUBLANES - t.shape[0]), (0, 0)))


def _rows_merged(w):
    return w.reshape(w.shape[0] * w.shape[1], w.shape[2])


def _local_grads(x, tgt, norm_mix, norm_mlp, norm_kv, norm_final, conv_w, hub):
    bl, seq, D = x.shape
    T = bl * seq
    h = x.reshape(T, D)
    tgt = tgt.reshape(T, D)
    row = lambda t, l: t[l:l + 1]
    W = hub.weights
    saved = []
    kv = h_kv = hn_kv_t = None
    for l in range(DEPTH):
        if l < N_A_LAYERS:
            bcu, hn_t = _norm_mm(f"l{l}_in", h, row(norm_mix, l), W["w_a_in", l], out_dtype=BF16, transposed=True, hub=hub)
            h2, gated_t = _gate_mm_res(f"l{l}_conv_out", bcu, _pad8(conv_w[l]), _rows_merged(W["w_a_out", l]), h, seq, hub=hub)
            saved.append((h, bcu, gated_t, hn_t))
        else:
            i = l - N_A_LAYERS
            if l == N_A_LAYERS:
                h_kv = h
                kv, hn_kv_t = _norm_mm("kv", h, norm_kv.reshape(1, D), W["w_kv", None], transposed=True, hub=hub)
            q, hn_t = _norm_mm(f"l{l}_q", h, row(norm_mix, l), W["w_q", i], transposed=True)
            per_group = [_attn_fwd(f"l{l}_att{g}", q, kv, g, bl, hub=hub) for g in range(N_GROUPS)]
            o, lse, h2 = _combine_mm_res(f"l{l}_att_out", [p[0] for p in per_group], [p[1] for p in per_group],
                                         W["w_o", i], h)
            saved.append((h, q, o, lse, hn_t))
        a = _norm_mm(f"l{l}_up", h2, row(norm_mlp, l), W["w_up", l], out_dtype=BF16, hub=hub)
        h = _relu2_mm_res(f"l{l}_down", a, _rows_merged(W["w_down", l]), h2, hub=hub)
        saved[-1] = saved[-1] + (h2, a)

    dh, sq_err, d_norm_final = _final_loss("loss", h, tgt, norm_final.reshape(1, D))

    d_norm_mix = [None] * DEPTH
    d_norm_mlp = [None] * DEPTH
    d_conv = [None] * N_A_LAYERS
    d_norm_kv = None
    dkv_acc = [None] * N_GROUPS
    G = hub.grads
    as_slots = lambda g: g.reshape(N_DEV, g.shape[0] // N_DEV, g.shape[1])
    tt = DW_TOKENS
    for l in reversed(range(DEPTH)):
        h2, a = saved[l][-2:]
        h_in = saved[l][0]
        g_mlp = row(norm_mlp, l)
        g_mix = row(norm_mix, l)
        w_up_l = W["w_up", l]
        FF = N_DEV * w_up_l.shape[2]
        da = _nt_relu2_bwd(f"l{l}_down_bwd", dh, _rows_merged(W["w_down", l]), a, hub=hub)
        G["w_down", l] = as_slots(_tn(f"l{l}_dw_down", [_rows2(a, 2 * tt, FF // 4, lambda s: s)], _relu2,
                                      [_rows2(dh, 2 * tt)], _val, FF, D, T, 2 * tt, split=("k", 4)))
        G["w_up", l] = _tn(f"l{l}_dw_up", [_rows2(h2, 2 * tt), _full2(g_mlp)], _normed,
                           [_rows2(da, 2 * tt, FF // 4, lambda s: s)], _val, D, FF, T, 2 * tt, split=("n", 4),
                           out_cols=w_up_l.shape[2], hub=hub)
        dh2, d_norm_mlp[l] = _nt_norm_bwd(f"l{l}_up_bwd", [da], w_up_l, h2, g_mlp, dh, hub=hub)
        if l >= N_A_LAYERS:
            i = l - N_A_LAYERS
            _, q, o, lse, hn_t = saved[l][:5]
            w_o_i, w_q_i = W["w_o", i], W["w_q", i]
            do, delta = _att_out_bwd(f"l{l}_att_out_bwd", dh2, w_o_i, o)
            G["w_o", i] = _tn(f"l{l}_dw_o", [_rows2(o, tt)], _val, [_rows2(dh2, tt)], _val, QW, D, T, tt,
                              out_cols=w_o_i.shape[2])
            dqs = []
            for g in range(N_GROUPS):
                dqs.append(_attn_bwd_dq(f"l{l}_att{g}_dq", q, kv, do, delta, lse, g, bl, hub=hub))
                dkv_acc[g] = _attn_bwd_dkv(f"l{l}_att{g}_dkv", q, kv, do, delta, lse, g, bl, prev=dkv_acc[g], hub=hub)
            G["w_q", i] = _tn(f"l{l}_dw_q", [_cols2(hn_t, tt)], None,
                              [_rows2(t, tt) for t in dqs], _concat_f32, D, N_GROUPS * QW, T, tt, out_cols=w_q_i.shape[2])
            dh, d_norm_mix[l] = _nt_norm_bwd(f"l{l}_q_bwd", dqs, w_q_i, h_in, g_mix, dh2, hub=hub)
            if l == N_A_LAYERS:
                dkvs = [t for pair in dkv_acc for t in pair]
                g_kv = norm_kv.reshape(1, D)
                w_kv = W["w_kv", None]
                per_call = len(dkvs) // 2
                halves = [_tn(f"dw_kv{p}", [_cols2(hn_kv_t, tt)], None,
                              [_rows2(t, tt) for t in dkvs[p * per_call:(p + 1) * per_call]], _concat_f32,
                              D, per_call * QW, T, tt, out_cols=w_kv.shape[2]) for p in range(2)]
                G["w_kv", None] = jnp.concatenate(halves, axis=0)
                dh, d_norm_kv = _nt_norm_bwd("kv_bwd", dkvs, w_kv, h_kv, g_kv, dh, tm=MM_ROWS // 2, hub=hub)
        else:
            _, bcu, gated_t, hn_t = saved[l][:4]
            cw = _pad8(conv_w[l])
            w_in_l = W["w_a_in", l]
            dgated = _nt_plain(f"l{l}_conv_out_bwd", dh2, _rows_merged(W["w_a_out", l]))
            G["w_a_out", l] = as_slots(_tn(f"l{l}_dw_conv_out", [_cols2(gated_t, 2 * tt)], None,
                                           [_rows2(dh2, 2 * tt)], _val, D, D, T, 2 * tt, hub=hub))
            dbcu, d_conv[l] = _conv_bwd(f"l{l}_conv_bwd", bcu, dgated, cw, seq, hub=hub)
            G["w_a_in", l] = _tn(f"l{l}_dw_in", [_cols2(hn_t, 2 * tt)], None,
                                 [_rows2(dbcu, 2 * tt, 3 * D // 2, lambda s: s)], _val, D, 3 * D, T, 2 * tt, split=("n", 2),
                                 out_cols=w_in_l.shape[2], hub=hub)
            dh, d_norm_mix[l] = _nt_norm_bwd(f"l{l}_in_bwd", [dbcu], w_in_l, h_in, g_mix, dh2, hub=hub)

    small = jnp.concatenate(d_norm_mix + d_norm_mlp + [d_norm_kv, d_norm_final] + d_conv, axis=0)
    return sq_err, dh.reshape(bl, seq, D), small


def kernel(x, norm_mix, norm_mlp, w_a_in, conv_w, w_a_out, norm_kv, w_kv, w_q, w_o, w_up, w_down, norm_final, loss_target, m_norm_mix, m_norm_mlp, m_w_a_in, m_conv_w, m_w_a_out, m_norm_kv, m_w_kv, m_w_q, m_w_o, m_w_up, m_w_down, m_norm_final, v_norm_mix, v_norm_mlp, v_w_a_in, v_conv_w, v_w_a_out, v_norm_kv, v_w_kv, v_w_q, v_w_o, v_w_up, v_w_down, v_norm_final):
    D = x.shape[-1]
    xi, yi, ci = _mesh_pos()
    me_idx = 4 * xi + 2 * yi + ci
    w_big = dict(w_a_in=w_a_in, w_a_out=w_a_out, w_kv=w_kv, w_q=w_q, w_o=w_o, w_up=w_up, w_down=w_down)
    m_big = dict(w_a_in=m_w_a_in, w_a_out=m_w_a_out, w_kv=m_w_kv, w_q=m_w_q, w_o=m_w_o, w_up=m_w_up, w_down=m_w_down)
    v_big = dict(w_a_in=v_w_a_in, w_a_out=v_w_a_out, w_kv=v_w_kv, w_q=v_w_q, w_o=v_w_o, w_up=v_w_up, w_down=v_w_down)
    names = list(w_big)

    shards = {n: w.astype(BF16) for n, w in w_big.items()}
    landing = {n: lax.empty((N_DEV,) + w.shape, BF16) for n, w in w_big.items()}
    hub = _Hub(FETCH_DURING, PUSH_DURING, shards, landing)
    dc = conv_w.shape[-1]
    taps = conv_w.shape[0] * conv_w.shape[1]
    got = _all_gather("gather_first", [(shards[n], l) for n, l in FETCH_UP_FRONT] + [(_pad8(conv_w.reshape(taps, dc)), None)])
    for key, w in zip(FETCH_UP_FRONT, got):
        hub.weights[key] = w
    conv_full = jnp.moveaxis(got[-1][:, :taps], 0, 1).reshape(conv_w.shape[0], conv_w.shape[1], N_DEV * dc)

    sq_err, grad_x, small = _local_grads(x, loss_target, norm_mix, norm_mlp, norm_kv, norm_final, conv_full, hub)

    grads, deltas, new_m, new_v = {}, {}, {}, {}
    for n in names:
        shape = w_big[n].shape
        cols = shape[-1]
        flat = lambda t: t.reshape(-1, cols)
        parts = hub.landing[n].reshape(N_DEV, -1, cols)
        tr = parts.shape[1]
        while tr * cols > ADAMW_TILE and tr % 32 == 0:
            tr //= 2
        outs = _sum8_adamw(f"adamw_{n}", parts, flat(w_big[n]), flat(m_big[n]), flat(v_big[n]), tr=tr)
        grads[n], deltas[n], new_m[n], new_v[n] = (t.reshape(shape) for t in outs)

    n_gain = 2 * DEPTH + 2
    n_small = small.shape[0]
    small = jnp.concatenate([small, jnp.full((SUBLANES, D), sq_err, F32)], axis=0)
    rows_small = small.shape[0]
    small_all = _all_gather("gather_small_grads", [(small, None)])[0]

    def small_pack(nm, nl, nk, nf, cw):
        gains = jnp.concatenate([nm, nl, nk.reshape(1, D), nf.reshape(1, D)], axis=0)
        taps_full = lax.dynamic_update_slice(jnp.zeros((taps, D), F32), cw.reshape(taps, dc), (0, me_idx * dc))
        return jnp.concatenate([gains, taps_full, jnp.zeros((SUBLANES, D), F32)], axis=0)

    sp = [small_pack(*t) for t in ((norm_mix, norm_mlp, norm_kv, norm_final, conv_w),
                                   (m_norm_mix, m_norm_mlp, m_norm_kv, m_norm_final, m_conv_w),
                                   (v_norm_mix, v_norm_mlp, v_norm_kv, v_norm_final, v_conv_w))]
    small_out = _sum8_adamw("adamw_small", small_all, *sp, tr=rows_small)
    loss = small_out[0][n_small, 0] * (0.5 / D)

    def small_unpack(t):
        res = dict(norm_mix=t[0:DEPTH], norm_mlp=t[DEPTH:2 * DEPTH], norm_kv=t[2 * DEPTH], norm_final=t[2 * DEPTH + 1])
        res["conv_w"] = lax.dynamic_slice(t[n_gain:], (0, me_idx * dc), (taps, dc)).reshape(conv_w.shape)
        return res

    for dst, t in zip((grads, deltas, new_m, new_v), small_out):
        dst.update(small_unpack(t))

    order = ["norm_mix", "norm_mlp", "w_a_in", "conv_w", "w_a_out", "norm_kv", "w_kv", "w_q", "w_o", "w_up", "w_down",
             "norm_final"]
    return (loss, grad_x, *[grads[n] for n in order], *[deltas[n] for n in order], *[new_m[n] for n in order],
            *[new_v[n] for n in order])
```

```python
import jax
import jax.numpy as jnp
from jax import lax
from jax.experimental import pallas as pl
from jax.experimental.pallas import tpu as pltpu

F32 = jnp.float32
BF16 = jnp.bfloat16
SDS = jax.ShapeDtypeStruct

EPS = 1e-5
N_A_LAYERS = 2
DEPTH = 4
PATTERNS = ((128, 1), (512, 4), (2048, 16))
N_GROUPS = 3
H_G = 8
HEAD_DIM = 64
QW = H_G * HEAD_DIM
ATT_BLK = 128
ALIBI_MAX_BIAS = 8.0
NEG_INF = -1e30

ADAM_LR = 0.001
ADAM_B1 = 0.9
ADAM_B2 = 0.999
ADAM_EPS = 1e-08
ADAM_WD = 0.01
ADAM_STEP = 10

N_DEV = 8
SUBLANES = 8
HALO = 16
V7X_VMEM_LIMIT = 56 * 1024 * 1024
MXU_COLS = 256
MM_CHUNK = 512
MM_ROWS = 512
ADAMW_TILE = 256 * 1024
DW_TOKENS = 1024

FETCH_UP_FRONT = [("w_a_in", 0)]
FETCH_DURING = {
    "l0_in": [("w_a_out", 0), ("w_up", 0)], "l0_conv_out": [("w_down", 0)], "l0_up": [("w_a_in", 1), ("w_a_out", 1)], "l0_down": [("w_up", 1)],
    "l1_in": [("w_down", 1)], "l1_conv_out": [("w_kv", None)],
    "l1_up": [("w_q", 0), ("w_o", 0), ("w_q", 1), ("w_o", 1)], "l1_down": [("w_up", 2)],
    "kv": [("w_down", 2)], "l2_up": [("w_up", 3)], "l2_down": [("w_down", 3)],
}
PUSH_DURING = {
    "l3_dw_up": [("w_down", 3, 0, 2)], "l3_up_bwd": [("w_down", 3, 1, 2)],
    "l3_att0_dq": [("w_up", 3, 0, 2)], "l3_att0_dkv": [("w_up", 3, 1, 2)], "l3_q_bwd": [("w_o", 1), ("w_q", 1)],
    "l2_dw_up": [("w_down", 2, 0, 2)], "l2_up_bwd": [("w_down", 2, 1, 2)],
    "l2_att0_dq": [("w_up", 2, 0, 2)], "l2_att0_dkv": [("w_up", 2, 1, 2)], "l2_q_bwd": [("w_o", 0), ("w_q", 0)],
    "kv_bwd": [("w_kv", None, 0, 2)], "l1_down_bwd": [("w_kv", None, 1, 2)],
    "l1_dw_up": [("w_down", 1, 0, 2)], "l1_up_bwd": [("w_down", 1, 1, 2)], "l1_conv_bwd": [("w_up", 1, 0, 2)],
    "l1_dw_in": [("w_up", 1, 1, 2), ("w_a_out", 1)], "l1_in_bwd": [("w_a_in", 1, 0, 2)], "l0_down_bwd": [("w_a_in", 1, 1, 2)],
    "l0_dw_up": [("w_down", 0, 0, 2)], "l0_up_bwd": [("w_down", 0, 1, 2)], "l0_conv_bwd": [("w_up", 0, 0, 2)],
    "l0_dw_in": [("w_up", 0, 1, 2), ("w_a_out", 0)], "l0_in_bwd": [("w_a_in", 0)],
}


def _mesh_pos():
    return lax.axis_index("x"), lax.axis_index("y"), lax.axis_index("c")


def _flip(v, bit):
    return 1 - v if bit else v


class _Transfer:
    def __init__(self, kind, key, src, src_idx=None, dst=None, dst_idx=None, dst_shape=None, rows=None):
        self.kind, self.key, self.src, self.src_idx = kind, key, src, src_idx
        self.dst, self.dst_idx, self.dst_shape, self.rows = dst, dst_idx, dst_shape, rows

    def copies(self, src_ref, dst_ref, send_sems, recv_sems, local_sem):
        x, y, c = _mesh_pos()
        me = 4 * x + 2 * y + c
        part = (lambda r: r) if self.rows is None else (lambda r: r.at[pl.ds(*self.rows)])

        def dst_slot(j):
            r = dst_ref.at[j]
            return part(r if self.dst_idx is None else r.at[self.dst_idx])

        def copy(k, src, dst_j, to):
            return pltpu.make_async_remote_copy(
                src_ref=src, dst_ref=dst_slot(dst_j), send_sem=send_sems.at[k], recv_sem=recv_sems.at[k],
                device_id=to, device_id_type=pl.DeviceIdType.MESH)

        if self.kind == "exchange":
            local = pltpu.make_async_copy(part(src_ref.at[me]), dst_slot(me), local_sem)
            sends, arrivals = [], []
            for k in range(1, N_DEV):
                peer = (_flip(x, k & 4), _flip(y, k & 2), _flip(c, k & 1))
                peer_idx = 4 * peer[0] + 2 * peer[1] + peer[2]
                sends.append(copy(k - 1, part(src_ref.at[peer_idx]), me, peer))
                arrivals.append(copy(k - 1, part(src_ref.at[peer_idx]), peer_idx, peer))
            return local, sends, [], arrivals

        own = part(src_ref if self.src_idx is None else src_ref.at[self.src_idx])
        idx = lambda px, py, pc: 4 * px + 2 * py + pc
        sibling = (x, y, 1 - c)
        chips = [(1 - x, y), (x, 1 - y), (1 - x, 1 - y)]
        local = pltpu.make_async_copy(own, dst_slot(me), local_sem)
        sends = [copy(0, own, me, sibling)] + [copy(1 + j, own, me, (*chip, c)) for j, chip in enumerate(chips)]
        relays = [(copy(1 + j, own, idx(*chip, c), sibling), copy(4 + j, dst_slot(idx(*chip, c)), idx(*chip, c), sibling))
                  for j, chip in enumerate(chips)]
        arrivals = [copy(0, own, idx(*sibling), sibling)]
        arrivals += [copy(4 + j, own, idx(*chip, 1 - c), sibling) for j, chip in enumerate(chips)]
        return local, sends, relays, arrivals


class _Hub:
    def __init__(self, fetch, push, shards, landing):
        self.fetch, self.push, self.shards, self.landing = fetch, push, shards, landing
        self.weights = {}
        self.arriving = {}
        self.grads = {}

    def transfers(self, host):
        out = []
        for name, l, *part in self.fetch.get(host, ()):
            src = self.shards[name]
            shard = tuple(src.shape if l is None else src.shape[1:])
            p, n = part or (0, 1)
            rows = None if n == 1 else (p * (shard[0] // n), shard[0] // n)
            out.append(_Transfer("gather", (name, l, p == n - 1), src, src_idx=l, dst=self.arriving.get((name, l)),
                                 dst_shape=(N_DEV,) + shard, rows=rows))
        for name, l, *part in self.push.get(host, ()):
            src = self.grads[name, l]
            p, n = part or (0, 1)
            rows = None if n == 1 else (p * (src.shape[1] // n), src.shape[1] // n)
            out.append(_Transfer("exchange", (name, l, p == n - 1), src, dst=self.landing[name], dst_idx=l, rows=rows))
        return out

    def accept(self, transfers, results):
        for t, r in zip(transfers, results):
            name, l, complete = t.key
            if t.kind == "exchange":
                self.landing[name] = r
            elif complete:
                self.weights[name, l] = r
            else:
                self.arriving[name, l] = r


def _call(name, body, grid, ins, outs, scratch=(), hub=None):
    transfers = hub.transfers(name) if hub is not None else []
    n_in, n_out, n_scr, n_tr = len(ins), len(outs), len(scratch), len(transfers)
    c_in, c_out, aliases, places = [], [], {}, []
    for t in transfers:
        c_in.append(t.src)
        src_pos = len(c_in) - 1
        if t.dst is not None:
            c_in.append(t.dst)
            aliases[n_in + len(c_in) - 1] = n_out + len(c_out)
            c_out.append(SDS(t.dst.shape, t.dst.dtype))
        else:
            c_out.append(SDS(t.dst_shape, t.src.dtype))
        places.append((src_pos, len(c_out) - 1))
    sems = [pltpu.SemaphoreType.DMA((n_tr, N_DEV - 1)), pltpu.SemaphoreType.DMA((n_tr, N_DEV - 1)),
            pltpu.SemaphoreType.DMA((n_tr,))] if n_tr else []

    def wrapped(*refs):
        in_refs = refs[:n_in]
        cin_refs = refs[n_in:n_in + len(c_in)]
        o0 = n_in + len(c_in)
        out_refs = refs[o0:o0 + n_out]
        cout_refs = refs[o0 + n_out:o0 + n_out + len(c_out)]
        s0 = o0 + n_out + len(c_out)
        scr_refs = refs[s0:s0 + n_scr]
        if n_tr:
            send_sems, recv_sems, local_sems = refs[s0 + n_scr:]
            first = last = relay = None
            for ax, n in enumerate(grid):
                i = pl.program_id(ax)
                at_relay = (i == max(n - 2, 0)) if ax == len(grid) - 1 else (i == n - 1)
                first = (i == 0) if first is None else first & (i == 0)
                last = (i == n - 1) if last is None else last & (i == n - 1)
                relay = at_relay if relay is None else relay & at_relay

            def all_copies():
                return [t.copies(cin_refs[sp], cout_refs[dp], send_sems.at[n], recv_sems.at[n], local_sems.at[n])
                        for n, (t, (sp, dp)) in enumerate(zip(transfers, places))]

            @pl.when(first)
            def _():
                for local, sends, _, _ in all_copies():
                    local.start()
                    for cp in sends:
                        cp.start()

            def pass_on():
                @pl.when(relay)
                def _():
                    for _, _, relays, _ in all_copies():
                        for arrival, onward in relays:
                            arrival.wait_recv()
                            onward.start()

            if grid[-1] > 1:
                pass_on()

        body(*in_refs, *out_refs, *scr_refs)

        if n_tr:
            if grid[-1] == 1:
                pass_on()

            @pl.when(last)
            def _():
                for local, sends, relays, arrivals in all_copies():
                    for cp in arrivals:
                        cp.wait_recv()
                    for cp in sends + [onward for _, onward in relays]:
                        cp.wait_send()
                    local.wait()

    any_spec = pl.BlockSpec(memory_space=pl.ANY)
    res = pl.pallas_call(
        wrapped,
        name=name,
        grid=grid,
        in_specs=[s for _, s in ins] + [any_spec] * len(c_in),
        out_specs=[s for _, s in outs] + [any_spec] * len(c_out),
        out_shape=[o for o, _ in outs] + c_out,
        scratch_shapes=list(scratch) + sems,
        input_output_aliases=aliases,
        compiler_params=pltpu.CompilerParams(
            dimension_semantics=("arbitrary",) * len(grid), vmem_limit_bytes=V7X_VMEM_LIMIT),
    )(*[a for a, _ in ins], *c_in)
    if n_tr:
        hub.accept(transfers, res[n_out:])
    return res[:n_out]


def _rows(a, tm, cb=None, col=0):
    cb = cb or a.shape[1]
    return (a, pl.BlockSpec((tm, cb), lambda i: (i, col)))


def _full(a):
    nd = a.ndim
    return (a, pl.BlockSpec(a.shape, lambda i: (0,) * nd))


def _prev8(a, tm, cb, col):
    return (a, pl.BlockSpec((HALO, cb), lambda i: (jnp.maximum(i * (tm // HALO) - 1, 0), col)))


def _next8(a, tm, cb, col):
    last = a.shape[0] // HALO - 1
    return (a, pl.BlockSpec((HALO, cb), lambda i: (jnp.minimum((i + 1) * (tm // HALO), last), col)))


def _rows2(a, tt, cb=None, colfn=None):
    cb = cb or a.shape[1]
    colfn = colfn or (lambda s: 0)
    return (a, pl.BlockSpec((tt, cb), lambda s, t: (t, colfn(s))))


def _full2(a):
    nd = a.ndim
    return (a, pl.BlockSpec(a.shape, lambda s, t: (0,) * nd))


def _out_rows(T, n, dtype, tm):
    return (SDS((T, n), dtype), pl.BlockSpec((tm, n), lambda i: (i, 0)))


def _out_acc8(d):
    return (SDS((SUBLANES, d), F32), pl.BlockSpec((SUBLANES, d), lambda i: (0, 0)))


def _rstd(x):
    return lax.rsqrt(jnp.mean(x * x, axis=-1, keepdims=True) + EPS)


def _normed(h_ref, g_ref):
    x = h_ref[...]
    return x * _rstd(x) * g_ref[...]


def _acc8(ref, val, i, n):
    part = val.reshape(-1, SUBLANES, val.shape[-1]).sum(axis=0)

    @pl.when(i == 0)
    def _():
        ref[...] = part

    @pl.when(i > 0)
    def _():
        ref[...] += part

    @pl.when(i == n - 1)
    def _():
        ref[...] = jnp.broadcast_to(jnp.sum(ref[...], axis=0, keepdims=True), ref.shape)


def _gate(b_ref, c_ref, u_ref, ch_ref, uh_ref, cw_ref, first):
    b, c, u = (r[...].astype(F32) for r in (b_ref, c_ref, u_ref))
    cu = c * u
    halo = jnp.where(first, 0.0, ch_ref[...].astype(F32) * uh_ref[...].astype(F32))
    rows = lax.broadcasted_iota(jnp.int32, cu.shape, 0)
    h1 = halo[HALO - 1:HALO, :]
    h2 = halo[HALO - 2:HALO - 1, :]
    cu1 = jnp.where(rows == 0, h1, pltpu.roll(cu, 1, 0))
    cu2 = jnp.where(rows == 0, h2, jnp.where(rows == 1, h1, pltpu.roll(cu, 2, 0)))
    conv = cw_ref[0:1, :] * cu + cw_ref[1:2, :] * cu1 + cw_ref[2:3, :] * cu2
    return b * conv, (b, c, u), conv, (cu, cu1, cu2)


def _relu2(a_ref):
    r = jnp.maximum(a_ref[...].astype(F32), 0.0)
    return r * r


def _dot(a, b):
    return jnp.dot(a, b, preferred_element_type=F32)


def _dot_nt(a, b):
    return lax.dot_general(a, b, (((1,), (1,)), ((), ())), preferred_element_type=F32)


def _chunks(n):
    c = min(MM_CHUNK, n)
    while n % c:
        c -= 128
    assert c > 0, n
    return [(k * c, (k + 1) * c) for k in range(n // c)]


def _col_weight(w):
    _, K, ns = w.shape
    N = N_DEV * ns
    direct = ns % MXU_COLS == 0
    scratch = [] if direct else [pltpu.VMEM((K, N), BF16)]

    def prepare(w_ref, s_ref, step):
        if direct:
            return

        @pl.when(step == 0)
        def _():
            for j in range(N_DEV):
                s_ref[:, j * ns:(j + 1) * ns] = w_ref[j]

    def chunks(w_ref, s_ref):
        if direct:
            return [(j * ns, (j + 1) * ns, (lambda j=j: w_ref[j])) for j in range(N_DEV)]
        return [(lo, hi, (lambda lo=lo, hi=hi: s_ref[:, lo:hi])) for lo, hi in _chunks(N)]

    return N, scratch, prepare, chunks


def _out_cols(n, T, tm):
    return (SDS((n, T), BF16), pl.BlockSpec((n, tm), lambda i: (0, i)))


def _norm_mm(name, h, g, w, tm=MM_ROWS, out_dtype=F32, transposed=False, hub=None):
    T, D = h.shape
    N, w_scratch, prepare, chunks = _col_weight(w)

    def body(h_ref, g_ref, w_ref, o_ref, *rest):
        at_ref, s = (rest[0], rest[1:]) if transposed else (None, rest)
        s_ref = s[0] if s else None
        prepare(w_ref, s_ref, pl.program_id(0))
        a32 = _normed(h_ref, g_ref)
        a = a32.astype(BF16)
        for lo, hi, load in chunks(w_ref, s_ref):
            o_ref[:, lo:hi] = _dot(a, load()).astype(out_dtype)
        if transposed:
            at_ref[...] = a32.T.astype(BF16)

    outs = [_out_rows(T, N, out_dtype, tm)] + ([_out_cols(D, T, tm)] if transposed else [])
    res = _call(name, body, (T // tm,), [_rows(h, tm), _full(g), _full(w)], outs, scratch=w_scratch, hub=hub)
    return res if transposed else res[0]


def _gate_mm_res(name, bcu, cw, w, h, seq, tm=MM_ROWS, hub=None):
    T, D = h.shape

    def body(b_ref, c_ref, u_ref, ch_ref, uh_ref, cw_ref, w_ref, h_ref, o_ref, gt_ref):
        first = (pl.program_id(0) * tm) % seq == 0
        gated32 = _gate(b_ref, c_ref, u_ref, ch_ref, uh_ref, cw_ref, first)[0]
        gated = gated32.astype(BF16)
        for lo, hi in _chunks(D):
            o_ref[:, lo:hi] = h_ref[:, lo:hi] + _dot(gated, w_ref[:, lo:hi])
        gt_ref[...] = gated32.T.astype(BF16)

    ins = [_rows(bcu, tm, D, 0), _rows(bcu, tm, D, 1), _rows(bcu, tm, D, 2), _prev8(bcu, tm, D, 1),
           _prev8(bcu, tm, D, 2), _full(cw), _full(w), _rows(h, tm)]
    return _call(name, body, (T // tm,), ins, [_out_rows(T, D, F32, tm), _out_cols(D, T, tm)], hub=hub)


def _relu2_mm_res(name, a, w, h, tm=MM_ROWS, hub=None):
    T, D = h.shape
    K = a.shape[1]

    def body(a_ref, w_ref, h_ref, o_ref, acc_ref):
        for n, (lo, hi) in enumerate(_chunks(K)):
            d = _dot(_relu2(a_ref.at[:, lo:hi]).astype(BF16), w_ref[lo:hi, :])
            if n == 0:
                acc_ref[...] = d
            else:
                acc_ref[...] += d
        o_ref[...] = h_ref[...] + acc_ref[...]

    return _call(name, body, (T // tm,), [_rows(a, tm), _full(w), _rows(h, tm)], [_out_rows(T, D, F32, tm)],
                 scratch=[pltpu.VMEM((tm, D), F32)], hub=hub)[0]


def _combine_mm_res(name, os_, lses, w, h, tm=MM_ROWS):
    T, D = h.shape
    _, w_scratch, prepare, chunks = _col_weight(w)

    def body(o0, o1, o2, l0, l1, l2, w_ref, h_ref, o_ref, lse_ref, out_ref, *s):
        s_ref = s[0] if s else None
        prepare(w_ref, s_ref, pl.program_id(0))
        ls = [l0[...], l1[...], l2[...]]
        mx = jnp.maximum(jnp.maximum(ls[0], ls[1]), ls[2])
        es = [jnp.exp(l - mx) for l in ls]
        den = es[0] + es[1] + es[2]
        o = (es[0] * o0[...] + es[1] * o1[...] + es[2] * o2[...]) / den
        o_ref[...] = o
        lse_ref[...] = mx + jnp.log(den)
        ob = o.astype(BF16)
        for lo, hi, load in chunks(w_ref, s_ref):
            out_ref[:, lo:hi] = h_ref[:, lo:hi] + _dot(ob, load())

    ins = [_rows(t, tm) for t in list(os_) + list(lses)] + [_full(w), _rows(h, tm)]
    outs = [_out_rows(T, QW, F32, tm), _out_rows(T, QW, F32, tm), _out_rows(T, D, F32, tm)]
    return _call(name, body, (T // tm,), ins, outs, scratch=w_scratch)


def _nt_relu2_bwd(name, dh, w, a, tm=MM_ROWS, hub=None):
    T, _ = dh.shape
    K = w.shape[0]

    def body(dh_ref, w_ref, a_ref, o_ref):
        d = dh_ref[...].astype(BF16)
        for lo, hi in _chunks(K):
            dr = _dot_nt(d, w_ref[lo:hi, :])
            o_ref[:, lo:hi] = (dr * (2.0 * jnp.maximum(a_ref[:, lo:hi].astype(F32), 0.0))).astype(BF16)

    return _call(name, body, (T // tm,), [_rows(dh, tm), _full(w), _rows(a, tm)], [_out_rows(T, K, BF16, tm)], hub=hub)[0]


def _concat_bf16(*refs):
    vals = [r[...].astype(BF16) for r in refs]
    return vals[0] if len(vals) == 1 else jnp.concatenate(vals, axis=1)


def _nt_plain(name, dy, w, tm=MM_ROWS):
    T, N = dy.shape
    K = w.shape[0]

    def body(dy_ref, w_ref, o_ref, acc_ref):
        for n, (lo, hi) in enumerate(_chunks(N)):
            d = _dot_nt(dy_ref[:, lo:hi].astype(BF16), w_ref[:, lo:hi])
            if n == 0:
                acc_ref[...] = d
            else:
                acc_ref[...] += d
        o_ref[...] = acc_ref[...]

    return _call(name, body, (T // tm,), [_rows(dy, tm), _full(w)], [_out_rows(T, K, F32, tm)],
                 scratch=[pltpu.VMEM((tm, K), F32)])[0]


def _att_out_bwd(name, dy, w, o, tm=MM_ROWS):
    T, _ = dy.shape
    K = w.shape[1]
    _, w_scratch, prepare, chunks = _col_weight(w)

    def body(dy_ref, w_ref, o_ref, do_ref, dl_ref, acc_ref, *s):
        s_ref = s[0] if s else None
        prepare(w_ref, s_ref, pl.program_id(0))
        for n, (lo, hi, load) in enumerate(chunks(w_ref, s_ref)):
            d = _dot_nt(dy_ref[:, lo:hi].astype(BF16), load())
            if n == 0:
                acc_ref[...] = d
            else:
                acc_ref[...] += d
        do = acc_ref[...]
        do_ref[...] = do
        prod = do * o_ref[...]
        high = prod.astype(BF16)
        low = (prod - high.astype(F32)).astype(BF16)
        head_of = lambda axis: jnp.right_shift(lax.broadcasted_iota(jnp.int32, (K, K), axis), HEAD_DIM.bit_length() - 1)
        same_head = jnp.where(head_of(0) == head_of(1), 1.0, 0.0).astype(BF16)
        dl_ref[...] = _dot(high, same_head) + _dot(low, same_head)

    outs = [_out_rows(T, K, F32, tm), _out_rows(T, K, F32, tm)]
    return _call(name, body, (T // tm,), [_rows(dy, tm), _full(w), _rows(o, tm)], outs,
                 scratch=[pltpu.VMEM((tm, K), F32)] + w_scratch)


def _nt_norm_bwd(name, dys, w, h, g, dh_in, tm=MM_ROWS, hub=None):
    T, D = h.shape
    _, w_scratch, prepare, chunks = _col_weight(w)
    n_steps = T // tm
    n_dy = len(dys)

    def body(*refs):
        dy_refs = refs[:n_dy]
        w_ref, h_ref, g_ref, dhin_ref, o_ref, dg_ref, acc_ref = refs[n_dy:n_dy + 7]
        s_ref = refs[n_dy + 7] if len(refs) > n_dy + 7 else None
        i = pl.program_id(0)
        prepare(w_ref, s_ref, i)
        dy = _concat_bf16(*dy_refs)
        for n, (lo, hi, load) in enumerate(chunks(w_ref, s_ref)):
            d = _dot_nt(dy[:, lo:hi], load())
            if n == 0:
                acc_ref[...] = d
            else:
                acc_ref[...] += d
        dn = acc_ref[...]
        x = h_ref[...]
        rstd = _rstd(x)
        xhat = x * rstd
        dxhat = dn * g_ref[...]
        dx = rstd * (dxhat - xhat * jnp.mean(dxhat * xhat, axis=-1, keepdims=True))
        o_ref[...] = dhin_ref[...] + dx
        _acc8(dg_ref, dn * xhat, i, n_steps)

    ins = [_rows(d, tm) for d in dys] + [_full(w), _rows(h, tm), _full(g), _rows(dh_in, tm)]
    outs = [_out_rows(T, D, F32, tm), _out_acc8(D)]
    dh, dg = _call(name, body, (n_steps,), ins, outs, scratch=[pltpu.VMEM((tm, D), F32)] + w_scratch, hub=hub)
    return dh, dg[0:1]


def _cols2(a_t, tt, kb=None):
    kb = kb or a_t.shape[0]
    return (a_t, pl.BlockSpec((kb, tt), (lambda s, t: (s, t)) if kb != a_t.shape[0] else (lambda s, t: (0, t))))


def _tn(name, a_ins, a_fn, y_ins, y_fn, K, N, T, tt, split=None, out_cols=None, hub=None):
    kind, parts = split or ("n", 1)
    kb, nb = (K // parts, N) if kind == "k" else (K, N // parts)
    n_steps = T // tt
    n_a = len(a_ins)
    n_y = len(y_ins)
    assert out_cols is None or (kind == "n" and nb % out_cols == 0)

    def body(*refs):
        a_refs = refs[:n_a]
        y_refs = refs[n_a:n_a + n_y]
        o_ref, acc_ref = refs[n_a + n_y:]
        t = pl.program_id(1)
        a_t = a_refs[0][...] if a_fn is None else a_fn(*a_refs).T.astype(BF16)
        y = y_fn(*y_refs).astype(BF16)
        for lo, hi in _chunks(nb):
            d = _dot(a_t, y[:, lo:hi])

            @pl.when(t == 0)
            def _():
                acc_ref[:, lo:hi] = d

            @pl.when(t > 0)
            def _():
                acc_ref[:, lo:hi] += d

        @pl.when(t == n_steps - 1)
        def _():
            if out_cols is None:
                o_ref[...] = acc_ref[...].astype(BF16)
            else:
                for j in range(nb // out_cols):
                    o_ref[j] = acc_ref[:, j * out_cols:(j + 1) * out_cols].astype(BF16)

    if out_cols is None:
        out = (SDS((K, N), BF16), pl.BlockSpec((kb, nb), (lambda s, t: (s, 0)) if kind == "k" else (lambda s, t: (0, s))))
    else:
        out = (SDS((N // out_cols, K, out_cols), BF16), pl.BlockSpec((nb // out_cols, K, out_cols), lambda s, t: (s, 0, 0)))
    return _call(name, body, (parts, n_steps), list(a_ins) + list(y_ins), [out],
                 scratch=[pltpu.VMEM((kb, nb), F32)], hub=hub)[0]


def _val(ref):
    return ref[...]


def _concat_f32(*refs):
    vals = [r[...] for r in refs]
    return vals[0] if len(vals) == 1 else jnp.concatenate(vals, axis=1)


ATT_TILE_ROWS = 2048
ATT_TILE_ROWS_Q = 4096
HEAD_PAIRS = H_G // 2
ATT_SCALE = HEAD_DIM ** -0.5
ATT_UNITS_TOGETHER = 4


def _slope(h):
    return 2.0 ** (-ALIBI_MAX_BIAS * (h + 1) / H_G)


def _att_geom(T, bl, g, tile_rows=ATT_TILE_ROWS):
    dil = PATTERNS[g][1]
    sub = ATT_BLK * dil
    nsub = max(1, min(tile_rows, T // bl) // sub)
    rows = sub * nsub
    return dil, sub, nsub, rows, T // bl // rows


def _att_specs(T, bl, g, tile_rows=ATT_TILE_ROWS):
    _, sub, nsub, rows, nt = _att_geom(T, bl, g, tile_rows)
    last_sub = T // sub - 1
    tile = lambda col: pl.BlockSpec((rows, 128), lambda b, i, hp: (b * nt + i, col(hp)))
    prev = lambda col: pl.BlockSpec((sub, 128), lambda b, i, hp: (jnp.maximum((b * nt + i) * nsub - 1, 0), col(hp)))
    nxt = lambda col: pl.BlockSpec((sub, 128), lambda b, i, hp: (jnp.minimum((b * nt + i + 1) * nsub, last_sub), col(hp)))
    return tile, prev, nxt


def _sub_rows(j, r, dil):
    start = j * ATT_BLK * dil + r
    return pl.ds(start, ATT_BLK, stride=dil) if dil > 1 else pl.ds(start, ATT_BLK)


class _Residues:
    def __init__(self, dil):
        self.dil = dil
        self.whole = dil % SUBLANES == 0
        self.read, self.written = {}, {}

    def _block(self, j):
        return pl.ds(j * ATT_BLK * self.dil, ATT_BLK * self.dil)

    def load(self, ref, j, r):
        if not self.whole:
            return ref[_sub_rows(j, r, self.dil), :]
        if (id(ref), j) not in self.read:
            rows = ref[self._block(j), :]
            self.read[id(ref), j] = jnp.swapaxes(rows.reshape(ATT_BLK, self.dil, rows.shape[-1]), 0, 1)
        return self.read[id(ref), j][r]

    def store(self, ref, j, r, val):
        if not self.whole:
            ref[_sub_rows(j, r, self.dil), :] = val
            return
        got = self.written.setdefault((id(ref), j), {})
        got[r] = val
        if len(got) == self.dil:
            merged = jnp.swapaxes(jnp.stack([got[k] for k in range(self.dil)], axis=0), 0, 1)
            ref[self._block(j), :] = merged.reshape(ATT_BLK * self.dil, val.shape[-1])
            del self.written[id(ref), j]


def _att_consts(hp, dil, keys_first=False):
    h0 = lax.broadcasted_iota(jnp.int32, (ATT_BLK, 128), 1) < HEAD_DIM
    a = lax.broadcasted_iota(jnp.int32, (ATT_BLK, ATT_BLK), 1 if keys_first else 0)
    c = lax.broadcasted_iota(jnp.int32, (ATT_BLK, ATT_BLK), 0 if keys_first else 1)
    dist_p = ((ATT_BLK + a - c) * dil).astype(F32)
    dist_c = ((a - c) * dil).astype(F32)
    bias_p, bias_c = [], []
    for h in range(2):
        slope = jnp.float32(_slope(2 * (HEAD_PAIRS - 1) + h))
        for p in range(HEAD_PAIRS - 2, -1, -1):
            slope = jnp.where(hp == p, jnp.float32(_slope(2 * p + h)), slope)
        bias_p.append(jnp.where(c >= a, -slope * dist_p, NEG_INF))
        bias_c.append(jnp.where(c <= a, -slope * dist_c, NEG_INF))
    return h0, jnp.concatenate(bias_p, axis=0), jnp.concatenate(bias_c, axis=0)


def _stack_heads(x, h0):
    return jnp.concatenate([jnp.where(h0, x, 0.0), jnp.where(h0, 0.0, x)], axis=0).astype(BF16)


def _unstack_heads(x, h0):
    return jnp.where(h0, x[:ATT_BLK], x[ATT_BLK:])


def _stack_cols(x):
    return jnp.concatenate(_head_cols(x), axis=0)


def _head_cols(x):
    return [x[:, 0:1], x[:, HEAD_DIM:HEAD_DIM + 1]]


def _in_groups(units, first_stage, *later_stages):
    for u0 in range(0, len(units), ATT_UNITS_TOGETHER):
        staged = [first_stage(*u) for u in units[u0:u0 + ATT_UNITS_TOGETHER]]
        for stage in later_stages:
            staged = [stage(*s) for s in staged]


def _attn_fwd(name, q, kv, g, bl, hub=None):
    T = q.shape[0]
    dil, _, nsub, _, nt = _att_geom(T, bl, g, ATT_TILE_ROWS_Q)
    tile, prev, _ = _att_specs(T, bl, g, ATT_TILE_ROWS_Q)
    halo = nt > 1

    def body(*refs):
        if halo:
            q_ref, kp_ref, kc_ref, vp_ref, vc_ref, o_ref, lse_ref = refs
        else:
            q_ref, kc_ref, vc_ref, o_ref, lse_ref = refs
        first = pl.program_id(1) == 0
        h0, bias_p, bias_c = _att_consts(pl.program_id(2), dil)
        bias_first = jnp.where(first, NEG_INF, bias_p)
        rows = _Residues(dil)

        def with_ones(v):
            return [jnp.where(h0, v, 1.0).astype(BF16), jnp.where(h0, 1.0, v).astype(BF16)]

        def scores(j, r):
            parts = [(rows.load(kc_ref, j, r), rows.load(vc_ref, j, r), bias_c)]
            if j > 0:
                parts.append((rows.load(kc_ref, j - 1, r), rows.load(vc_ref, j - 1, r), bias_p))
            elif halo:
                parts.append((rows.load(kp_ref, 0, r), rows.load(vp_ref, 0, r), bias_first))
            qs = _stack_heads(rows.load(q_ref, j, r) * ATT_SCALE, h0)
            return (j, r), [_dot_nt(qs, k.astype(BF16)) + b for k, _, b in parts], [with_ones(v) for _, v, _ in parts]

        def weights(unit, s, vals):
            mx = jnp.max(s[0] if len(s) == 1 else jnp.maximum(s[0], s[1]), axis=-1, keepdims=True)
            return unit, mx, [jnp.exp(x - mx).astype(BF16) for x in s], vals

        def outputs(unit, mx, e, vals):
            heads = [slice(h * ATT_BLK, (h + 1) * ATT_BLK) for h in range(2)]
            acc = []
            for h, hs in enumerate(heads):
                terms = [_dot(x[hs], v[h]) for x, v in zip(e, vals)]
                acc.append(terms[0] if len(terms) == 1 else terms[0] + terms[1])
            den = [pltpu.roll(a, HEAD_DIM, 1) for a in acc]
            rows.store(o_ref, *unit, jnp.where(h0, acc[0] / den[0], acc[1] / den[1]))
            rows.store(lse_ref, *unit, jnp.where(h0, mx[heads[0]] + jnp.log(den[0]), mx[heads[1]] + jnp.log(den[1])))
            return ()

        _in_groups([(j, r) for j in range(nsub) for r in range(dil)], scores, weights, outputs)

    kcol, vcol = (lambda hp: 8 * g + hp), (lambda hp: 8 * g + 4 + hp)
    ins = [(q, tile(lambda hp: 4 * g + hp))] + ([(kv, prev(kcol))] if halo else []) + [(kv, tile(kcol))]
    ins += ([(kv, prev(vcol))] if halo else []) + [(kv, tile(vcol))]
    out = (SDS((T, QW), F32), tile(lambda hp: hp))
    return _call(name, body, (bl, nt, HEAD_PAIRS), ins, [out, out], hub=hub)


def _attn_bwd_dq(name, q, kv, do, delta, lse, g, bl, hub=None):
    T = q.shape[0]
    dil, _, nsub, _, nt = _att_geom(T, bl, g, ATT_TILE_ROWS_Q)
    tile, prev, _ = _att_specs(T, bl, g, ATT_TILE_ROWS_Q)
    halo = nt > 1

    def body(*refs):
        if halo:
            q_ref, kp_ref, kc_ref, vp_ref, vc_ref, do_ref, dl_ref, lse_ref, dq_ref = refs
        else:
            q_ref, kc_ref, vc_ref, do_ref, dl_ref, lse_ref, dq_ref = refs
        first = pl.program_id(1) == 0
        h0, bias_p, bias_c = _att_consts(pl.program_id(2), dil)
        bias_first = jnp.where(first, NEG_INF, bias_p)
        rows = _Residues(dil)

        def probs(j, r):
            parts = [(rows.load(kc_ref, j, r), rows.load(vc_ref, j, r), bias_c)]
            if j > 0:
                parts.append((rows.load(kc_ref, j - 1, r), rows.load(vc_ref, j - 1, r), bias_p))
            elif halo:
                parts.append((rows.load(kp_ref, 0, r), rows.load(vp_ref, 0, r), bias_first))
            qs = _stack_heads(rows.load(q_ref, j, r) * ATT_SCALE, h0)
            dos = _stack_heads(rows.load(do_ref, j, r), h0)
            lse = _stack_cols(rows.load(lse_ref, j, r))
            keys = [k.astype(BF16) for k, _, _ in parts]
            p = [jnp.exp(_dot_nt(qs, k) + b - lse) for k, (_, _, b) in zip(keys, parts)]
            dp = [_dot_nt(dos, v.astype(BF16)) for _, v, _ in parts]
            return (j, r), p, dp, keys

        def dscores(unit, p, dp, keys):
            dl = _stack_cols(rows.load(dl_ref, *unit))
            return unit, [(x * (y - dl)).astype(BF16) for x, y in zip(p, dp)], keys

        def outputs(unit, ds, keys):
            terms = [_dot(x, k) for x, k in zip(ds, keys)]
            dq = terms[0] if len(terms) == 1 else terms[0] + terms[1]
            rows.store(dq_ref, *unit, _unstack_heads(dq, h0) * ATT_SCALE)
            return ()

        _in_groups([(j, r) for j in range(nsub) for r in range(dil)], probs, dscores, outputs)

    own = lambda hp: hp
    kcol, vcol = (lambda hp: 8 * g + hp), (lambda hp: 8 * g + 4 + hp)
    ins = [(q, tile(lambda hp: 4 * g + hp))] + ([(kv, prev(kcol))] if halo else []) + [(kv, tile(kcol))]
    ins += ([(kv, prev(vcol))] if halo else []) + [(kv, tile(vcol))]
    ins += [(do, tile(own)), (delta, tile(own)), (lse, tile(own))]
    return _call(name, body, (bl, nt, HEAD_PAIRS), ins, [(SDS((T, QW), F32), tile(own))], hub=hub)[0]


def _attn_bwd_dkv(name, q, kv, do, delta, lse, g, bl, prev=None, hub=None):
    T = q.shape[0]
    dil, _, nsub, _, nt = _att_geom(T, bl, g, ATT_TILE_ROWS_Q)
    tile, _, nxt = _att_specs(T, bl, g, ATT_TILE_ROWS_Q)
    has_prev = prev is not None
    halo = nt > 1

    def body(*refs):
        if halo:
            k_ref, v_ref, q_ref, qn_ref, do_ref, don_ref, dl_ref, dln_ref, l_ref, ln_ref = refs[:10]
            rest = refs[10:]
        else:
            k_ref, v_ref, q_ref, do_ref, dl_ref, l_ref = refs[:6]
            rest = refs[6:]
        if has_prev:
            dkp_ref, dvp_ref, dk_ref, dv_ref = rest
        else:
            dk_ref, dv_ref = rest
        last = pl.program_id(1) == nt - 1
        h0, bias_p, bias_c = _att_consts(pl.program_id(2), dil, keys_first=True)
        bias_last = jnp.where(last, NEG_INF, bias_p)
        rows = _Residues(dil)

        def per_query_rows(x):
            xt = x.T
            return jnp.concatenate([jnp.broadcast_to(xt[0:1], (ATT_BLK, ATT_BLK)),
                                    jnp.broadcast_to(xt[HEAD_DIM:HEAD_DIM + 1], (ATT_BLK, ATT_BLK))], axis=0)

        def probs(j, r):
            ks = _stack_heads(rows.load(k_ref, j, r), h0)
            vs = _stack_heads(rows.load(v_ref, j, r), h0)
            sets = [(q_ref, do_ref, dl_ref, l_ref, j, bias_c)]
            if j < nsub - 1:
                sets.append((q_ref, do_ref, dl_ref, l_ref, j + 1, bias_p))
            elif halo:
                sets.append((qn_ref, don_ref, dln_ref, ln_ref, 0, bias_last))
            out = []
            for qr, dor, dlr, lr, jq, bias in sets:
                qsb = (rows.load(qr, jq, r) * ATT_SCALE).astype(BF16)
                do2b = rows.load(dor, jq, r).astype(BF16)
                p = jnp.exp(_dot_nt(ks, qsb) + bias - per_query_rows(rows.load(lr, jq, r)))
                out.append((p, _dot_nt(vs, do2b), dlr, jq, qsb, do2b))
            return (j, r), out

        def dscores(unit, sets):
            out = []
            for p, dp, dlr, jq, qsb, do2b in sets:
                ds = (p * (dp - per_query_rows(rows.load(dlr, jq, unit[1])))).astype(BF16)
                out.append((p.astype(BF16), ds, qsb, do2b))
            return unit, out

        def outputs(unit, sets):
            dk_st = dv_st = None
            for pb, ds, qsb, do2b in sets:
                dvs, dks = _dot(pb, do2b), _dot(ds, qsb)
                dv_st = dvs if dv_st is None else dv_st + dvs
                dk_st = dks if dk_st is None else dk_st + dks
            dk2 = _unstack_heads(dk_st, h0)
            dv2 = _unstack_heads(dv_st, h0)
            if has_prev:
                dk2 = dk2 + rows.load(dkp_ref, *unit)
                dv2 = dv2 + rows.load(dvp_ref, *unit)
            rows.store(dk_ref, *unit, dk2)
            rows.store(dv_ref, *unit, dv2)
            return ()

        _in_groups([(j, r) for j in range(nsub) for r in range(dil)], probs, dscores, outputs)

    own = lambda hp: hp
    qcol = lambda hp: 4 * g + hp
    ins = [(kv, tile(lambda hp: 8 * g + hp)), (kv, tile(lambda hp: 8 * g + 4 + hp))]
    for t, col in ((q, qcol), (do, own), (delta, own), (lse, own)):
        ins += [(t, tile(col))] + ([(t, nxt(col))] if halo else [])
    if has_prev:
        ins += [(prev[0], tile(own)), (prev[1], tile(own))]
    out = (SDS((T, QW), F32), tile(own))
    return _call(name, body, (bl, nt, HEAD_PAIRS), ins, [out, out], hub=hub)


def _final_loss(name, h, tgt, g, tm=MM_ROWS):
    T, D = h.shape
    n_steps = T // tm

    def body(h_ref, t_ref, g_ref, dh_ref, loss_ref, dg_ref, sq_ref):
        i = pl.program_id(0)
        x = h_ref[...]
        rstd = _rstd(x)
        xhat = x * rstd
        err = xhat * g_ref[...] - t_ref[...]
        _acc8(sq_ref, err * err, i, n_steps)
        dy = err * (1.0 / D)
        dxhat = dy * g_ref[...]
        dh_ref[...] = rstd * (dxhat - xhat * jnp.mean(dxhat * xhat, axis=-1, keepdims=True))
        _acc8(dg_ref, dy * xhat, i, n_steps)

        @pl.when(i == n_steps - 1)
        def _():
            loss_ref[...] = jnp.full(loss_ref.shape, jnp.sum(sq_ref[0:1, :]), F32)

    outs = [_out_rows(T, D, F32, tm), (SDS((SUBLANES, 128), F32), pl.BlockSpec((SUBLANES, 128), lambda i: (0, 0))),
            _out_acc8(D)]
    dh, loss, dg = _call(name, body, (n_steps,), [_rows(h, tm), _rows(tgt, tm), _full(g)], outs,
                         scratch=[pltpu.VMEM((SUBLANES, D), F32)])
    return dh, loss[0, 0], dg[0:1]


def _conv_bwd(name, bcu, dgated, cw, seq, tm=MM_ROWS, hub=None):
    T, D = dgated.shape
    n_steps = T // tm

    def body(b_ref, c_ref, u_ref, ch_ref, uh_ref, dg_ref, dgn_ref, bn_ref, cw_ref, o_ref, t0_ref, t1_ref, t2_ref):
        i = pl.program_id(0)
        first = (i * tm) % seq == 0
        last = ((i + 1) * tm) % seq == 0
        _, (b, c, u), conv, (cu, cu1, cu2) = _gate(b_ref, c_ref, u_ref, ch_ref, uh_ref, cw_ref, first)
        dgat = dg_ref[...]
        dconv = dgat * b
        nxt = jnp.where(last, 0.0, dgn_ref[...] * bn_ref[...].astype(F32))
        rows = lax.broadcasted_iota(jnp.int32, dconv.shape, 0)
        n1 = nxt[0:1, :]
        n2 = nxt[1:2, :]
        dc1 = jnp.where(rows == tm - 1, n1, pltpu.roll(dconv, tm - 1, 0))
        dc2 = jnp.where(rows == tm - 1, n2, jnp.where(rows == tm - 2, n1, pltpu.roll(dconv, tm - 2, 0)))
        dcu = cw_ref[0:1, :] * dconv + cw_ref[1:2, :] * dc1 + cw_ref[2:3, :] * dc2
        o_ref[:, 0:D] = (dgat * conv).astype(BF16)
        o_ref[:, D:2 * D] = (dcu * u).astype(BF16)
        o_ref[:, 2 * D:3 * D] = (dcu * c).astype(BF16)
        _acc8(t0_ref, dconv * cu, i, n_steps)
        _acc8(t1_ref, dconv * cu1, i, n_steps)
        _acc8(t2_ref, dconv * cu2, i, n_steps)

    ins = [_rows(bcu, tm, D, 0), _rows(bcu, tm, D, 1), _rows(bcu, tm, D, 2), _prev8(bcu, tm, D, 1), _prev8(bcu, tm, D, 2),
           _rows(dgated, tm), _next8(dgated, tm, D, 0), _next8(bcu, tm, D, 0), _full(cw)]
    outs = [_out_rows(T, 3 * D, BF16, tm), _out_acc8(D), _out_acc8(D), _out_acc8(D)]
    dbcu, t0, t1, t2 = _call(name, body, (n_steps,), ins, outs, hub=hub)
    return dbcu, jnp.concatenate([t0[0:1], t1[0:1], t2[0:1]], axis=0)


def _sum8_adamw(name, parts, w, m, v, tr):
    R, C = w.shape
    b1c = 1.0 - ADAM_B1 ** ADAM_STEP
    b2c = 1.0 - ADAM_B2 ** ADAM_STEP

    def body(p_ref, w_ref, m_ref, v_ref, g_ref, d_ref, nm_ref, nv_ref):
        g = p_ref[0].astype(F32)
        for j in range(1, N_DEV):
            g = g + p_ref[j].astype(F32)
        nm = ADAM_B1 * m_ref[...] + (1.0 - ADAM_B1) * g
        nv = ADAM_B2 * v_ref[...] + (1.0 - ADAM_B2) * (g * g)
        m_hat = nm / b1c
        v_hat = nv / b2c
        g_ref[...] = g
        d_ref[...] = -ADAM_LR * (m_hat / (jnp.sqrt(v_hat) + ADAM_EPS) + ADAM_WD * w_ref[...])
        nm_ref[...] = nm
        nv_ref[...] = nv

    ins = [(parts, pl.BlockSpec((N_DEV, tr, C), lambda i: (0, i, 0))), _rows(w, tr), _rows(m, tr), _rows(v, tr)]
    outs = [_out_rows(R, C, F32, tr)] * 4
    return _call(name, body, (R // tr,), ins, outs)


def _all_gather(name, items):
    n = len(items)
    shapes = [tuple(a.shape if idx is None else a.shape[1:]) for a, idx in items]

    def body(*refs):
        x_refs, out_refs = refs[:n], refs[n:2 * n]
        send_sems, recv_sems, local_sems = refs[2 * n:]
        x, y, c = _mesh_pos()
        me, sibling = (x, y, c), (x, y, 1 - c)
        chips = [(1 - x, y), (x, 1 - y), (1 - x, 1 - y)]

        def copy(t, k, block, to, own=False):
            dst = out_refs[t].at[4 * block[0] + 2 * block[1] + block[2]]
            src = dst
            if own:
                src = x_refs[t] if items[t][1] is None else x_refs[t].at[items[t][1]]
            return pltpu.make_async_remote_copy(
                src_ref=src, dst_ref=dst, send_sem=send_sems.at[t, k], recv_sem=recv_sems.at[t, k],
                device_id=to, device_id_type=pl.DeviceIdType.MESH)

        started = []
        for t in range(n):
            src = x_refs[t] if items[t][1] is None else x_refs[t].at[items[t][1]]
            mine = pltpu.make_async_copy(src, out_refs[t].at[4 * x + 2 * y + c], local_sems.at[t])
            mine.start()
            first = [copy(t, 0, me, sibling, own=True)]
            first += [copy(t, 1 + j, me, (*chip, c), own=True) for j, chip in enumerate(chips)]
            for cp in first:
                cp.start()
            started.append((mine, first))
        passed = []
        for t in range(n):
            for j, chip in enumerate(chips):
                copy(t, 1 + j, (*chip, c), me).wait_recv()
                fwd = copy(t, 4 + j, (*chip, c), sibling)
                fwd.start()
                passed.append(fwd)
        for t in range(n):
            copy(t, 0, sibling, me).wait_recv()
            for j, chip in enumerate(chips):
                copy(t, 4 + j, (*chip, 1 - c), me).wait_recv()
        for mine, first in started:
            for cp in first:
                cp.wait_send()
            mine.wait()
        for cp in passed:
            cp.wait_send()

    any_spec = pl.BlockSpec(memory_space=pl.ANY)
    return pl.pallas_call(
        body, name=name,
        out_shape=[SDS((N_DEV,) + s, a.dtype) for s, (a, _) in zip(shapes, items)],
        in_specs=[any_spec] * n,
        out_specs=[any_spec] * n,
        scratch_shapes=[pltpu.SemaphoreType.DMA((n, 7)), pltpu.SemaphoreType.DMA((n, 7)), pltpu.SemaphoreType.DMA((n,))],
    )(*[a for a, _ in items])


def _pad8(t):
    return jnp.pad(t, ((0, SUBLANES - t.shape[0]), (0, 0)))


def _rows_merged(w):
    return w.reshape(w.shape[0] * w.shape[1], w.shape[2])


def _local_grads(x, tgt, norm_mix, norm_mlp, norm_kv, norm_final, conv_w, hub):
    bl, seq, D = x.shape
    T = bl * seq
    h = x.reshape(T, D)
    tgt = tgt.reshape(T, D)
    row = lambda t, l: t[l:l + 1]
    W = hub.weights
    saved = []
    kv = h_kv = hn_kv_t = None
    for l in range(DEPTH):
        if l < N_A_LAYERS:
            bcu, hn_t = _norm_mm(f"l{l}_in", h, row(norm_mix, l), W["w_a_in", l], out_dtype=BF16, transposed=True, hub=hub)
            h2, gated_t = _gate_mm_res(f"l{l}_conv_out", bcu, _pad8(conv_w[l]), _rows_merged(W["w_a_out", l]), h, seq, hub=hub)
            saved.append((h, bcu, gated_t, hn_t))
        else:
            i = l - N_A_LAYERS
            if l == N_A_LAYERS:
                h_kv = h
                kv, hn_kv_t = _norm_mm("kv", h, norm_kv.reshape(1, D), W["w_kv", None], transposed=True, hub=hub)
            q, hn_t = _norm_mm(f"l{l}_q", h, row(norm_mix, l), W["w_q", i], transposed=True)
            per_group = [_attn_fwd(f"l{l}_att{g}", q, kv, g, bl, hub=hub) for g in range(N_GROUPS)]
            o, lse, h2 = _combine_mm_res(f"l{l}_att_out", [p[0] for p in per_group], [p[1] for p in per_group],
                                         W["w_o", i], h)
            saved.append((h, q, o, lse, hn_t))
        a = _norm_mm(f"l{l}_up", h2, row(norm_mlp, l), W["w_up", l], out_dtype=BF16, hub=hub)
        h = _relu2_mm_res(f"l{l}_down", a, _rows_merged(W["w_down", l]), h2, hub=hub)
        saved[-1] = saved[-1] + (h2, a)

    dh, sq_err, d_norm_final = _final_loss("loss", h, tgt, norm_final.reshape(1, D))

    d_norm_mix = [None] * DEPTH
    d_norm_mlp = [None] * DEPTH
    d_conv = [None] * N_A_LAYERS
    d_norm_kv = None
    dkv_acc = [None] * N_GROUPS
    G = hub.grads
    as_slots = lambda g: g.reshape(N_DEV, g.shape[0] // N_DEV, g.shape[1])
    tt = DW_TOKENS
    for l in reversed(range(DEPTH)):
        h2, a = saved[l][-2:]
        h_in = saved[l][0]
        g_mlp = row(norm_mlp, l)
        g_mix = row(norm_mix, l)
        w_up_l = W["w_up", l]
        FF = N_DEV * w_up_l.shape[2]
        da = _nt_relu2_bwd(f"l{l}_down_bwd", dh, _rows_merged(W["w_down", l]), a, hub=hub)
        G["w_down", l] = as_slots(_tn(f"l{l}_dw_down", [_rows2(a, 2 * tt, FF // 4, lambda s: s)], _relu2,
                                      [_rows2(dh, 2 * tt)], _val, FF, D, T, 2 * tt, split=("k", 4)))
        G["w_up", l] = _tn(f"l{l}_dw_up", [_rows2(h2, 2 * tt), _full2(g_mlp)], _normed,
                           [_rows2(da, 2 * tt, FF // 4, lambda s: s)], _val, D, FF, T, 2 * tt, split=("n", 4),
                           out_cols=w_up_l.shape[2], hub=hub)
        dh2, d_norm_mlp[l] = _nt_norm_bwd(f"l{l}_up_bwd", [da], w_up_l, h2, g_mlp, dh, hub=hub)
        if l >= N_A_LAYERS:
            i = l - N_A_LAYERS
            _, q, o, lse, hn_t = saved[l][:5]
            w_o_i, w_q_i = W["w_o", i], W["w_q", i]
            do, delta = _att_out_bwd(f"l{l}_att_out_bwd", dh2, w_o_i, o)
            G["w_o", i] = _tn(f"l{l}_dw_o", [_rows2(o, tt)], _val, [_rows2(dh2, tt)], _val, QW, D, T, tt,
                              out_cols=w_o_i.shape[2])
            dqs = []
            for g in range(N_GROUPS):
                dqs.append(_attn_bwd_dq(f"l{l}_att{g}_dq", q, kv, do, delta, lse, g, bl, hub=hub))
                dkv_acc[g] = _attn_bwd_dkv(f"l{l}_att{g}_dkv", q, kv, do, delta, lse, g, bl, prev=dkv_acc[g], hub=hub)
            G["w_q", i] = _tn(f"l{l}_dw_q", [_cols2(hn_t, tt)], None,
                              [_rows2(t, tt) for t in dqs], _concat_f32, D, N_GROUPS * QW, T, tt, out_cols=w_q_i.shape[2])
            dh, d_norm_mix[l] = _nt_norm_bwd(f"l{l}_q_bwd", dqs, w_q_i, h_in, g_mix, dh2, hub=hub)
            if l == N_A_LAYERS:
                dkvs = [t for pair in dkv_acc for t in pair]
                g_kv = norm_kv.reshape(1, D)
                w_kv = W["w_kv", None]
                per_call = len(dkvs) // 2
                halves = [_tn(f"dw_kv{p}", [_cols2(hn_kv_t, tt)], None,
                              [_rows2(t, tt) for t in dkvs[p * per_call:(p + 1) * per_call]], _concat_f32,
                              D, per_call * QW, T, tt, out_cols=w_kv.shape[2]) for p in range(2)]
                G["w_kv", None] = jnp.concatenate(halves, axis=0)
                dh, d_norm_kv = _nt_norm_bwd("kv_bwd", dkvs, w_kv, h_kv, g_kv, dh, tm=MM_ROWS // 2, hub=hub)
        else:
            _, bcu, gated_t, hn_t = saved[l][:4]
            cw = _pad8(conv_w[l])
            w_in_l = W["w_a_in", l]
            dgated = _nt_plain(f"l{l}_conv_out_bwd", dh2, _rows_merged(W["w_a_out", l]))
            G["w_a_out", l] = as_slots(_tn(f"l{l}_dw_conv_out", [_cols2(gated_t, 2 * tt)], None,
                                           [_rows2(dh2, 2 * tt)], _val, D, D, T, 2 * tt, hub=hub))
            dbcu, d_conv[l] = _conv_bwd(f"l{l}_conv_bwd", bcu, dgated, cw, seq, hub=hub)
            G["w_a_in", l] = _tn(f"l{l}_dw_in", [_cols2(hn_t, 2 * tt)], None,
                                 [_rows2(dbcu, 2 * tt, 3 * D // 2, lambda s: s)], _val, D, 3 * D, T, 2 * tt, split=("n", 2),
                                 out_cols=w_in_l.shape[2], hub=hub)
            dh, d_norm_mix[l] = _nt_norm_bwd(f"l{l}_in_bwd", [dbcu], w_in_l, h_in, g_mix, dh2, hub=hub)

    small = jnp.concatenate(d_norm_mix + d_norm_mlp + [d_norm_kv, d_norm_final] + d_conv, axis=0)
    return sq_err, dh.reshape(bl, seq, D), small


def kernel(x, norm_mix, norm_mlp, w_a_in, conv_w, w_a_out, norm_kv, w_kv, w_q, w_o, w_up, w_down, norm_final, loss_target, m_norm_mix, m_norm_mlp, m_w_a_in, m_conv_w, m_w_a_out, m_norm_kv, m_w_kv, m_w_q, m_w_o, m_w_up, m_w_down, m_norm_final, v_norm_mix, v_norm_mlp, v_w_a_in, v_conv_w, v_w_a_out, v_norm_kv, v_w_kv, v_w_q, v_w_o, v_w_up, v_w_down, v_norm_final):
    D = x.shape[-1]
    xi, yi, ci = _mesh_pos()
    me_idx = 4 * xi + 2 * yi + ci
    w_big = dict(w_a_in=w_a_in, w_a_out=w_a_out, w_kv=w_kv, w_q=w_q, w_o=w_o, w_up=w_up, w_down=w_down)
    m_big = dict(w_a_in=m_w_a_in, w_a_out=m_w_a_out, w_kv=m_w_kv, w_q=m_w_q, w_o=m_w_o, w_up=m_w_up, w_down=m_w_down)
    v_big = dict(w_a_in=v_w_a_in, w_a_out=v_w_a_out, w_kv=v_w_kv, w_q=v_w_q, w_o=v_w_o, w_up=v_w_up, w_down=v_w_down)
    names = list(w_big)

    shards = {n: w.astype(BF16) for n, w in w_big.items()}
    landing = {n: lax.empty((N_DEV,) + w.shape, BF16) for n, w in w_big.items()}
    hub = _Hub(FETCH_DURING, PUSH_DURING, shards, landing)
    dc = conv_w.shape[-1]
    taps = conv_w.shape[0] * conv_w.shape[1]
    got = _all_gather("gather_first", [(shards[n], l) for n, l in FETCH_UP_FRONT] + [(_pad8(conv_w.reshape(taps, dc)), None)])
    for key, w in zip(FETCH_UP_FRONT, got):
        hub.weights[key] = w
    conv_full = jnp.moveaxis(got[-1][:, :taps], 0, 1).reshape(conv_w.shape[0], conv_w.shape[1], N_DEV * dc)

    sq_err, grad_x, small = _local_grads(x, loss_target, norm_mix, norm_mlp, norm_kv, norm_final, conv_full, hub)

    grads, deltas, new_m, new_v = {}, {}, {}, {}
    for n in names:
        shape = w_big[n].shape
        cols = shape[-1]
        flat = lambda t: t.reshape(-1, cols)
        parts = hub.landing[n].reshape(N_DEV, -1, cols)
        tr = parts.shape[1]
        while tr * cols > ADAMW_TILE and tr % 32 == 0:
            tr //= 2
        outs = _sum8_adamw(f"adamw_{n}", parts, flat(w_big[n]), flat(m_big[n]), flat(v_big[n]), tr=tr)
        grads[n], deltas[n], new_m[n], new_v[n] = (t.reshape(shape) for t in outs)

    n_gain = 2 * DEPTH + 2
    n_small = small.shape[0]
    small = jnp.concatenate([small, jnp.full((SUBLANES, D), sq_err, F32)], axis=0)
    rows_small = small.shape[0]
    small_all = _all_gather("gather_small_grads", [(small, None)])[0]

    def small_pack(nm, nl, nk, nf, cw):
        gains = jnp.concatenate([nm, nl, nk.reshape(1, D), nf.reshape(1, D)], axis=0)
        taps_full = lax.dynamic_update_slice(jnp.zeros((taps, D), F32), cw.reshape(taps, dc), (0, me_idx * dc))
        return jnp.concatenate([gains, taps_full, jnp.zeros((SUBLANES, D), F32)], axis=0)

    sp = [small_pack(*t) for t in ((norm_mix, norm_mlp, norm_kv, norm_final, conv_w),
                                   (m_norm_mix, m_norm_mlp, m_norm_kv, m_norm_final, m_conv_w),
                                   (v_norm_mix, v_norm_mlp, v_norm_kv, v_norm_final, v_conv_w))]
    small_out = _sum8_adamw("adamw_small", small_all, *sp, tr=rows_small)
    loss = small_out[0][n_small, 0] * (0.5 / D)

    def small_unpack(t):
        res = dict(norm_mix=t[0:DEPTH], norm_mlp=t[DEPTH:2 * DEPTH], norm_kv=t[2 * DEPTH], norm_final=t[2 * DEPTH + 1])
        res["conv_w"] = lax.dynamic_slice(t[n_gain:], (0, me_idx * dc), (taps, dc)).reshape(conv_w.shape)
        return res

    for dst, t in zip((grads, deltas, new_m, new_v), small_out):
        dst.update(small_unpack(t))

    order = ["norm_mix", "norm_mlp", "w_a_in", "conv_w", "w_a_out", "norm_kv", "w_kv", "w_q", "w_o", "w_up", "w_down",
             "norm_final"]
    return (loss, grad_x, *[grads[n] for n in order], *[deltas[n] for n in order], *[new_m[n] for n in order],
            *[new_v[n] for n in order])
```

```python
import jax
import jax.numpy as jnp
from jax import lax
from jax.experimental import pallas as pl
from jax.experimental.pallas import tpu as pltpu

F32 = jnp.float32
BF16 = jnp.bfloat16
SDS = jax.ShapeDtypeStruct

EPS = 1e-5
N_A_LAYERS = 2
DEPTH = 4
PATTERNS = ((128, 1), (512, 4), (2048, 16))
N_GROUPS = 3
H_G = 8
HEAD_DIM = 64
QW = H_G * HEAD_DIM
ATT_BLK = 128
ALIBI_MAX_BIAS = 8.0
NEG_INF = -1e30

ADAM_LR = 0.001
ADAM_B1 = 0.9
ADAM_B2 = 0.999
ADAM_EPS = 1e-08
ADAM_WD = 0.01
ADAM_STEP = 10

N_DEV = 8
SUBLANES = 8
HALO = 16
V7X_VMEM_LIMIT = 56 * 1024 * 1024
MXU_COLS = 256
MM_CHUNK = 512
MM_ROWS = 512
ADAMW_TILE = 256 * 1024
DW_TOKENS = 1024

FETCH_UP_FRONT = [("w_a_in", 0)]
FETCH_DURING = {
    "l0_in": [("w_a_out", 0), ("w_up", 0)], "l0_conv_out": [("w_down", 0)], "l0_up": [("w_a_in", 1), ("w_a_out", 1)], "l0_down": [("w_up", 1)],
    "l1_in": [("w_down", 1)], "l1_conv_out": [("w_kv", None)],
    "l1_up": [("w_q", 0), ("w_o", 0), ("w_q", 1), ("w_o", 1)], "l1_down": [("w_up", 2)],
    "kv": [("w_down", 2)], "l2_up": [("w_up", 3)], "l2_down": [("w_down", 3)],
}
PUSH_DURING = {
    "l3_dw_up": [("w_down", 3, 0, 2)], "l3_up_bwd": [("w_down", 3, 1, 2)],
    "l2_down_bwd": [("w_up", 3, 0, 2)], "l2_dw_down": [("w_up", 3, 1, 2)], "l3_q_bwd": [("w_o", 1), ("w_q", 1)],
    "l2_dw_up": [("w_down", 2, 0, 2)], "l2_up_bwd": [("w_down", 2, 1, 2)],
    "l2_q_bwd": [("w_o", 0), ("w_q", 0)],
    "kv_bwd": [("w_kv", None, 0, 2)], "l1_down_bwd": [("w_kv", None, 1, 2), ("w_up", 2, 0, 2)],
    "l1_dw_down": [("w_up", 2, 1, 2)],
    "l1_dw_up": [("w_down", 1, 0, 2)], "l1_up_bwd": [("w_down", 1, 1, 2)], "l1_conv_bwd": [("w_up", 1, 0, 2)],
    "l1_dw_in": [("w_up", 1, 1, 2), ("w_a_out", 1)], "l1_in_bwd": [("w_a_in", 1, 0, 2)], "l0_down_bwd": [("w_a_in", 1, 1, 2)],
    "l0_dw_up": [("w_down", 0, 0, 2)], "l0_up_bwd": [("w_down", 0, 1, 2)], "l0_conv_bwd": [("w_up", 0, 0, 2)],
    "l0_dw_in": [("w_up", 0, 1, 2), ("w_a_out", 0)], "l0_in_bwd": [("w_a_in", 0)],
}


def _mesh_pos():
    return lax.axis_index("x"), lax.axis_index("y"), lax.axis_index("c")


def _flip(v, bit):
    return 1 - v if bit else v


class _Transfer:
    def __init__(self, kind, key, src, src_idx=None, dst=None, dst_idx=None, dst_shape=None, rows=None):
        self.kind, self.key, self.src, self.src_idx = kind, key, src, src_idx
        self.dst, self.dst_idx, self.dst_shape, self.rows = dst, dst_idx, dst_shape, rows

    def copies(self, src_ref, dst_ref, send_sems, recv_sems, local_sem):
        x, y, c = _mesh_pos()
        me = 4 * x + 2 * y + c
        part = (lambda r: r) if self.rows is None else (lambda r: r.at[pl.ds(*self.rows)])

        def dst_slot(j):
            r = dst_ref.at[j]
            return part(r if self.dst_idx is None else r.at[self.dst_idx])

        def copy(k, src, dst_j, to):
            return pltpu.make_async_remote_copy(
                src_ref=src, dst_ref=dst_slot(dst_j), send_sem=send_sems.at[k], recv_sem=recv_sems.at[k],
                device_id=to, device_id_type=pl.DeviceIdType.MESH)

        if self.kind == "exchange":
            local = pltpu.make_async_copy(part(src_ref.at[me]), dst_slot(me), local_sem)
            sends, arrivals = [], []
            for k in range(1, N_DEV):
                peer = (_flip(x, k & 4), _flip(y, k & 2), _flip(c, k & 1))
                peer_idx = 4 * peer[0] + 2 * peer[1] + peer[2]
                sends.append(copy(k - 1, part(src_ref.at[peer_idx]), me, peer))
                arrivals.append(copy(k - 1, part(src_ref.at[peer_idx]), peer_idx, peer))
            return local, sends, [], arrivals

        own = part(src_ref if self.src_idx is None else src_ref.at[self.src_idx])
        idx = lambda px, py, pc: 4 * px + 2 * py + pc
        sibling = (x, y, 1 - c)
        chips = [(1 - x, y), (x, 1 - y), (1 - x, 1 - y)]
        local = pltpu.make_async_copy(own, dst_slot(me), local_sem)
        sends = [copy(0, own, me, sibling)] + [copy(1 + j, own, me, (*chip, c)) for j, chip in enumerate(chips)]
        relays = [(copy(1 + j, own, idx(*chip, c), sibling), copy(4 + j, dst_slot(idx(*chip, c)), idx(*chip, c), sibling))
                  for j, chip in enumerate(chips)]
        arrivals = [copy(0, own, idx(*sibling), sibling)]
        arrivals += [copy(4 + j, own, idx(*chip, 1 - c), sibling) for j, chip in enumerate(chips)]
        return local, sends, relays, arrivals


class _Hub:
    def __init__(self, fetch, push, shards, landing):
        self.fetch, self.push, self.shards, self.landing = fetch, push, shards, landing
        self.weights = {}
        self.arriving = {}
        self.grads = {}

    def transfers(self, host):
        out = []
        for name, l, *part in self.fetch.get(host, ()):
            src = self.shards[name]
            shard = tuple(src.shape if l is None else src.shape[1:])
            p, n = part or (0, 1)
            rows = None if n == 1 else (p * (shard[0] // n), shard[0] // n)
            out.append(_Transfer("gather", (name, l, p == n - 1), src, src_idx=l, dst=self.arriving.get((name, l)),
                                 dst_shape=(N_DEV,) + shard, rows=rows))
        for name, l, *part in self.push.get(host, ()):
            src = self.grads[name, l]
            p, n = part or (0, 1)
            rows = None if n == 1 else (p * (src.shape[1] // n), src.shape[1] // n)
            out.append(_Transfer("exchange", (name, l, p == n - 1), src, dst=self.landing[name], dst_idx=l, rows=rows))
        return out

    def accept(self, transfers, results):
        for t, r in zip(transfers, results):
            name, l, complete = t.key
            if t.kind == "exchange":
                self.landing[name] = r
            elif complete:
                self.weights[name, l] = r
            else:
                self.arriving[name, l] = r


def _call(name, body, grid, ins, outs, scratch=(), hub=None):
    transfers = hub.transfers(name) if hub is not None else []
    n_in, n_out, n_scr, n_tr = len(ins), len(outs), len(scratch), len(transfers)
    c_in, c_out, aliases, places = [], [], {}, []
    for t in transfers:
        c_in.append(t.src)
        src_pos = len(c_in) - 1
        if t.dst is not None:
            c_in.append(t.dst)
            aliases[n_in + len(c_in) - 1] = n_out + len(c_out)
            c_out.append(SDS(t.dst.shape, t.dst.dtype))
        else:
            c_out.append(SDS(t.dst_shape, t.src.dtype))
        places.append((src_pos, len(c_out) - 1))
    sems = [pltpu.SemaphoreType.DMA((n_tr, N_DEV - 1)), pltpu.SemaphoreType.DMA((n_tr, N_DEV - 1)),
            pltpu.SemaphoreType.DMA((n_tr,))] if n_tr else []

    def wrapped(*refs):
        in_refs = refs[:n_in]
        cin_refs = refs[n_in:n_in + len(c_in)]
        o0 = n_in + len(c_in)
        out_refs = refs[o0:o0 + n_out]
        cout_refs = refs[o0 + n_out:o0 + n_out + len(c_out)]
        s0 = o0 + n_out + len(c_out)
        scr_refs = refs[s0:s0 + n_scr]
        if n_tr:
            send_sems, recv_sems, local_sems = refs[s0 + n_scr:]
            first = last = relay = None
            for ax, n in enumerate(grid):
                i = pl.program_id(ax)
                at_relay = (i == max(n - 2, 0)) if ax == len(grid) - 1 else (i == n - 1)
                first = (i == 0) if first is None else first & (i == 0)
                last = (i == n - 1) if last is None else last & (i == n - 1)
                relay = at_relay if relay is None else relay & at_relay

            def all_copies():
                return [t.copies(cin_refs[sp], cout_refs[dp], send_sems.at[n], recv_sems.at[n], local_sems.at[n])
                        for n, (t, (sp, dp)) in enumerate(zip(transfers, places))]

            @pl.when(first)
            def _():
                for local, sends, _, _ in all_copies():
                    local.start()
                    for cp in sends:
                        cp.start()

            def pass_on():
                @pl.when(relay)
                def _():
                    for _, _, relays, _ in all_copies():
                        for arrival, onward in relays:
                            arrival.wait_recv()
                            onward.start()

            if grid[-1] > 1:
                pass_on()

        body(*in_refs, *out_refs, *scr_refs)

        if n_tr:
            if grid[-1] == 1:
                pass_on()

            @pl.when(last)
            def _():
                for local, sends, relays, arrivals in all_copies():
                    for cp in arrivals:
                        cp.wait_recv()
                    for cp in sends + [onward for _, onward in relays]:
                        cp.wait_send()
                    local.wait()

    any_spec = pl.BlockSpec(memory_space=pl.ANY)
    res = pl.pallas_call(
        wrapped,
        name=name,
        grid=grid,
        in_specs=[s for _, s in ins] + [any_spec] * len(c_in),
        out_specs=[s for _, s in outs] + [any_spec] * len(c_out),
        out_shape=[o for o, _ in outs] + c_out,
        scratch_shapes=list(scratch) + sems,
        input_output_aliases=aliases,
        compiler_params=pltpu.CompilerParams(
            dimension_semantics=("arbitrary",) * len(grid), vmem_limit_bytes=V7X_VMEM_LIMIT),
    )(*[a for a, _ in ins], *c_in)
    if n_tr:
        hub.accept(transfers, res[n_out:])
    return res[:n_out]


def _rows(a, tm, cb=None, col=0):
    cb = cb or a.shape[1]
    return (a, pl.BlockSpec((tm, cb), lambda i: (i, col)))


def _full(a):
    nd = a.ndim
    return (a, pl.BlockSpec(a.shape, lambda i: (0,) * nd))


def _prev8(a, tm, cb, col):
    return (a, pl.BlockSpec((HALO, cb), lambda i: (jnp.maximum(i * (tm // HALO) - 1, 0), col)))


def _next8(a, tm, cb, col):
    last = a.shape[0] // HALO - 1
    return (a, pl.BlockSpec((HALO, cb), lambda i: (jnp.minimum((i + 1) * (tm // HALO), last), col)))


def _rows2(a, tt, cb=None, colfn=None):
    cb = cb or a.shape[1]
    colfn = colfn or (lambda s: 0)
    return (a, pl.BlockSpec((tt, cb), lambda s, t: (t, colfn(s))))


def _full2(a):
    nd = a.ndim
    return (a, pl.BlockSpec(a.shape, lambda s, t: (0,) * nd))


def _out_rows(T, n, dtype, tm):
    return (SDS((T, n), dtype), pl.BlockSpec((tm, n), lambda i: (i, 0)))


def _out_acc8(d):
    return (SDS((SUBLANES, d), F32), pl.BlockSpec((SUBLANES, d), lambda i: (0, 0)))


def _rstd(x):
    return lax.rsqrt(jnp.mean(x * x, axis=-1, keepdims=True) + EPS)


def _normed(h_ref, g_ref):
    x = h_ref[...]
    return x * _rstd(x) * g_ref[...]


def _acc8(ref, val, i, n):
    part = val.reshape(-1, SUBLANES, val.shape[-1]).sum(axis=0)

    @pl.when(i == 0)
    def _():
        ref[...] = part

    @pl.when(i > 0)
    def _():
        ref[...] += part

    @pl.when(i == n - 1)
    def _():
        ref[...] = jnp.broadcast_to(jnp.sum(ref[...], axis=0, keepdims=True), ref.shape)


def _gate(b_ref, c_ref, u_ref, ch_ref, uh_ref, cw_ref, first):
    b, c, u = (r[...].astype(F32) for r in (b_ref, c_ref, u_ref))
    cu = c * u
    halo = jnp.where(first, 0.0, ch_ref[...].astype(F32) * uh_ref[...].astype(F32))
    rows = lax.broadcasted_iota(jnp.int32, cu.shape, 0)
    h1 = halo[HALO - 1:HALO, :]
    h2 = halo[HALO - 2:HALO - 1, :]
    cu1 = jnp.where(rows == 0, h1, pltpu.roll(cu, 1, 0))
    cu2 = jnp.where(rows == 0, h2, jnp.where(rows == 1, h1, pltpu.roll(cu, 2, 0)))
    conv = cw_ref[0:1, :] * cu + cw_ref[1:2, :] * cu1 + cw_ref[2:3, :] * cu2
    return b * conv, (b, c, u), conv, (cu, cu1, cu2)


def _relu2(a_ref):
    r = jnp.maximum(a_ref[...].astype(F32), 0.0)
    return r * r


def _dot(a, b):
    return jnp.dot(a, b, preferred_element_type=F32)


def _dot_nt(a, b):
    return lax.dot_general(a, b, (((1,), (1,)), ((), ())), preferred_element_type=F32)


def _chunks(n):
    c = min(MM_CHUNK, n)
    while n % c:
        c -= 128
    assert c > 0, n
    return [(k * c, (k + 1) * c) for k in range(n // c)]


def _col_weight(w):
    _, K, ns = w.shape
    N = N_DEV * ns
    direct = ns % MXU_COLS == 0
    scratch = [] if direct else [pltpu.VMEM((K, N), BF16)]

    def prepare(w_ref, s_ref, step):
        if direct:
            return

        @pl.when(step == 0)
        def _():
            for j in range(N_DEV):
                s_ref[:, j * ns:(j + 1) * ns] = w_ref[j]

    def chunks(w_ref, s_ref):
        if direct:
            return [(j * ns, (j + 1) * ns, (lambda j=j: w_ref[j])) for j in range(N_DEV)]
        return [(lo, hi, (lambda lo=lo, hi=hi: s_ref[:, lo:hi])) for lo, hi in _chunks(N)]

    return N, scratch, prepare, chunks


def _out_cols(n, T, tm):
    return (SDS((n, T), BF16), pl.BlockSpec((n, tm), lambda i: (0, i)))


def _norm_mm(name, h, g, w, tm=MM_ROWS, out_dtype=F32, transposed=False, hub=None):
    T, D = h.shape
    N, w_scratch, prepare, chunks = _col_weight(w)

    def body(h_ref, g_ref, w_ref, o_ref, *rest):
        at_ref, s = (rest[0], rest[1:]) if transposed else (None, rest)
        s_ref = s[0] if s else None
        prepare(w_ref, s_ref, pl.program_id(0))
        a32 = _normed(h_ref, g_ref)
        a = a32.astype(BF16)
        for lo, hi, load in chunks(w_ref, s_ref):
            o_ref[:, lo:hi] = _dot(a, load()).astype(out_dtype)
        if transposed:
            at_ref[...] = a32.T.astype(BF16)

    outs = [_out_rows(T, N, out_dtype, tm)] + ([_out_cols(D, T, tm)] if transposed else [])
    res = _call(name, body, (T // tm,), [_rows(h, tm), _full(g), _full(w)], outs, scratch=w_scratch, hub=hub)
    return res if transposed else res[0]


def _gate_mm_res(name, bcu, cw, w, h, seq, tm=MM_ROWS, hub=None):
    T, D = h.shape

    def body(b_ref, c_ref, u_ref, ch_ref, uh_ref, cw_ref, w_ref, h_ref, o_ref, gt_ref):
        first = (pl.program_id(0) * tm) % seq == 0
        gated32 = _gate(b_ref, c_ref, u_ref, ch_ref, uh_ref, cw_ref, first)[0]
        gated = gated32.astype(BF16)
        for lo, hi in _chunks(D):
            o_ref[:, lo:hi] = h_ref[:, lo:hi] + _dot(gated, w_ref[:, lo:hi])
        gt_ref[...] = gated32.T.astype(BF16)

    ins = [_rows(bcu, tm, D, 0), _rows(bcu, tm, D, 1), _rows(bcu, tm, D, 2), _prev8(bcu, tm, D, 1),
           _prev8(bcu, tm, D, 2), _full(cw), _full(w), _rows(h, tm)]
    return _call(name, body, (T // tm,), ins, [_out_rows(T, D, F32, tm), _out_cols(D, T, tm)], hub=hub)


def _relu2_mm_res(name, a, w, h, tm=MM_ROWS, hub=None):
    T, D = h.shape
    K = a.shape[1]

    def body(a_ref, w_ref, h_ref, o_ref, acc_ref):
        for n, (lo, hi) in enumerate(_chunks(K)):
            d = _dot(_relu2(a_ref.at[:, lo:hi]).astype(BF16), w_ref[lo:hi, :])
            if n == 0:
                acc_ref[...] = d
            else:
                acc_ref[...] += d
        o_ref[...] = h_ref[...] + acc_ref[...]

    return _call(name, body, (T // tm,), [_rows(a, tm), _full(w), _rows(h, tm)], [_out_rows(T, D, F32, tm)],
                 scratch=[pltpu.VMEM((tm, D), F32)], hub=hub)[0]


def _combine_mm_res(name, os_, lses, w, h, tm=MM_ROWS):
    T, D = h.shape
    _, w_scratch, prepare, chunks = _col_weight(w)

    def body(o0, o1, o2, l0, l1, l2, w_ref, h_ref, o_ref, lse_ref, out_ref, *s):
        s_ref = s[0] if s else None
        prepare(w_ref, s_ref, pl.program_id(0))
        ls = [l0[...], l1[...], l2[...]]
        mx = jnp.maximum(jnp.maximum(ls[0], ls[1]), ls[2])
        es = [jnp.exp(l - mx) for l in ls]
        den = es[0] + es[1] + es[2]
        o = (es[0] * o0[...] + es[1] * o1[...] + es[2] * o2[...]) / den
        o_ref[...] = o
        lse_ref[...] = mx + jnp.log(den)
        ob = o.astype(BF16)
        for lo, hi, load in chunks(w_ref, s_ref):
            out_ref[:, lo:hi] = h_ref[:, lo:hi] + _dot(ob, load())

    ins = [_rows(t, tm) for t in list(os_) + list(lses)] + [_full(w), _rows(h, tm)]
    outs = [_out_rows(T, QW, F32, tm), _out_rows(T, QW, F32, tm), _out_rows(T, D, F32, tm)]
    return _call(name, body, (T // tm,), ins, outs, scratch=w_scratch)


def _nt_relu2_bwd(name, dh, w, a, tm=MM_ROWS, hub=None):
    T, _ = dh.shape
    K = w.shape[0]

    def body(dh_ref, w_ref, a_ref, o_ref):
        d = dh_ref[...].astype(BF16)
        for lo, hi in _chunks(K):
            dr = _dot_nt(d, w_ref[lo:hi, :])
            o_ref[:, lo:hi] = (dr * (2.0 * jnp.maximum(a_ref[:, lo:hi].astype(F32), 0.0))).astype(BF16)

    return _call(name, body, (T // tm,), [_rows(dh, tm), _full(w), _rows(a, tm)], [_out_rows(T, K, BF16, tm)], hub=hub)[0]


def _concat_bf16(*refs):
    vals = [r[...].astype(BF16) for r in refs]
    return vals[0] if len(vals) == 1 else jnp.concatenate(vals, axis=1)


def _nt_plain(name, dy, w, tm=MM_ROWS):
    T, N = dy.shape
    K = w.shape[0]

    def body(dy_ref, w_ref, o_ref, acc_ref):
        for n, (lo, hi) in enumerate(_chunks(N)):
            d = _dot_nt(dy_ref[:, lo:hi].astype(BF16), w_ref[:, lo:hi])
            if n == 0:
                acc_ref[...] = d
            else:
                acc_ref[...] += d
        o_ref[...] = acc_ref[...]

    return _call(name, body, (T // tm,), [_rows(dy, tm), _full(w)], [_out_rows(T, K, F32, tm)],
                 scratch=[pltpu.VMEM((tm, K), F32)])[0]


def _att_out_bwd(name, dy, w, o, tm=MM_ROWS):
    T, _ = dy.shape
    K = w.shape[1]
    _, w_scratch, prepare, chunks = _col_weight(w)

    def body(dy_ref, w_ref, o_ref, do_ref, dl_ref, acc_ref, *s):
        s_ref = s[0] if s else None
        prepare(w_ref, s_ref, pl.program_id(0))
        for n, (lo, hi, load) in enumerate(chunks(w_ref, s_ref)):
            d = _dot_nt(dy_ref[:, lo:hi].astype(BF16), load())
            if n == 0:
                acc_ref[...] = d
            else:
                acc_ref[...] += d
        do = acc_ref[...]
        do_ref[...] = do
        prod = do * o_ref[...]
        high = prod.astype(BF16)
        low = (prod - high.astype(F32)).astype(BF16)
        head_of = lambda axis: jnp.right_shift(lax.broadcasted_iota(jnp.int32, (K, K), axis), HEAD_DIM.bit_length() - 1)
        same_head = jnp.where(head_of(0) == head_of(1), 1.0, 0.0).astype(BF16)
        dl_ref[...] = _dot(high, same_head) + _dot(low, same_head)

    outs = [_out_rows(T, K, F32, tm), _out_rows(T, K, F32, tm)]
    return _call(name, body, (T // tm,), [_rows(dy, tm), _full(w), _rows(o, tm)], outs,
                 scratch=[pltpu.VMEM((tm, K), F32)] + w_scratch)


def _nt_norm_bwd(name, dys, w, h, g, dh_in, tm=MM_ROWS, hub=None):
    T, D = h.shape
    _, w_scratch, prepare, chunks = _col_weight(w)
    n_steps = T // tm
    n_dy = len(dys)

    def body(*refs):
        dy_refs = refs[:n_dy]
        w_ref, h_ref, g_ref, dhin_ref, o_ref, dg_ref, acc_ref = refs[n_dy:n_dy + 7]
        s_ref = refs[n_dy + 7] if len(refs) > n_dy + 7 else None
        i = pl.program_id(0)
        prepare(w_ref, s_ref, i)
        dy = _concat_bf16(*dy_refs)
        for n, (lo, hi, load) in enumerate(chunks(w_ref, s_ref)):
            d = _dot_nt(dy[:, lo:hi], load())
            if n == 0:
                acc_ref[...] = d
            else:
                acc_ref[...] += d
        dn = acc_ref[...]
        x = h_ref[...]
        rstd = _rstd(x)
        xhat = x * rstd
        dxhat = dn * g_ref[...]
        dx = rstd * (dxhat - xhat * jnp.mean(dxhat * xhat, axis=-1, keepdims=True))
        o_ref[...] = dhin_ref[...] + dx
        _acc8(dg_ref, dn * xhat, i, n_steps)

    ins = [_rows(d, tm) for d in dys] + [_full(w), _rows(h, tm), _full(g), _rows(dh_in, tm)]
    outs = [_out_rows(T, D, F32, tm), _out_acc8(D)]
    dh, dg = _call(name, body, (n_steps,), ins, outs, scratch=[pltpu.VMEM((tm, D), F32)] + w_scratch, hub=hub)
    return dh, dg[0:1]


def _cols2(a_t, tt, kb=None):
    kb = kb or a_t.shape[0]
    return (a_t, pl.BlockSpec((kb, tt), (lambda s, t: (s, t)) if kb != a_t.shape[0] else (lambda s, t: (0, t))))


def _tn(name, a_ins, a_fn, y_ins, y_fn, K, N, T, tt, split=None, out_cols=None, hub=None):
    kind, parts = split or ("n", 1)
    kb, nb = (K // parts, N) if kind == "k" else (K, N // parts)
    n_steps = T // tt
    n_a = len(a_ins)
    n_y = len(y_ins)
    assert out_cols is None or (kind == "n" and nb % out_cols == 0)

    def body(*refs):
        a_refs = refs[:n_a]
        y_refs = refs[n_a:n_a + n_y]
        o_ref, acc_ref = refs[n_a + n_y:]
        t = pl.program_id(1)
        a_t = a_refs[0][...] if a_fn is None else a_fn(*a_refs).T.astype(BF16)
        y = y_fn(*y_refs).astype(BF16)
        for lo, hi in _chunks(nb):
            d = _dot(a_t, y[:, lo:hi])

            @pl.when(t == 0)
            def _():
                acc_ref[:, lo:hi] = d

            @pl.when(t > 0)
            def _():
                acc_ref[:, lo:hi] += d

        @pl.when(t == n_steps - 1)
        def _():
            if out_cols is None:
                o_ref[...] = acc_ref[...].astype(BF16)
            else:
                for j in range(nb // out_cols):
                    o_ref[j] = acc_ref[:, j * out_cols:(j + 1) * out_cols].astype(BF16)

    if out_cols is None:
        out = (SDS((K, N), BF16), pl.BlockSpec((kb, nb), (lambda s, t: (s, 0)) if kind == "k" else (lambda s, t: (0, s))))
    else:
        out = (SDS((N // out_cols, K, out_cols), BF16), pl.BlockSpec((nb // out_cols, K, out_cols), lambda s, t: (s, 0, 0)))
    return _call(name, body, (parts, n_steps), list(a_ins) + list(y_ins), [out],
                 scratch=[pltpu.VMEM((kb, nb), F32)], hub=hub)[0]


def _val(ref):
    return ref[...]


def _concat_f32(*refs):
    vals = [r[...] for r in refs]
    return vals[0] if len(vals) == 1 else jnp.concatenate(vals, axis=1)


ATT_TILE_ROWS = 2048
ATT_TILE_ROWS_Q = 4096
HEAD_PAIRS = H_G // 2
ATT_SCALE = HEAD_DIM ** -0.5
ATT_UNITS_TOGETHER = 4


def _slope(h):
    return 2.0 ** (-ALIBI_MAX_BIAS * (h + 1) / H_G)


def _att_geom(T, bl, g, tile_rows=ATT_TILE_ROWS):
    dil = PATTERNS[g][1]
    sub = ATT_BLK * dil
    nsub = max(1, min(tile_rows, T // bl) // sub)
    rows = sub * nsub
    return dil, sub, nsub, rows, T // bl // rows


def _att_specs(T, bl, g, tile_rows=ATT_TILE_ROWS):
    _, sub, nsub, rows, nt = _att_geom(T, bl, g, tile_rows)
    last_sub = T // sub - 1
    tile = lambda col: pl.BlockSpec((rows, 128), lambda b, i, hp: (b * nt + i, col(hp)))
    prev = lambda col: pl.BlockSpec((sub, 128), lambda b, i, hp: (jnp.maximum((b * nt + i) * nsub - 1, 0), col(hp)))
    nxt = lambda col: pl.BlockSpec((sub, 128), lambda b, i, hp: (jnp.minimum((b * nt + i + 1) * nsub, last_sub), col(hp)))
    return tile, prev, nxt


def _sub_rows(j, r, dil):
    start = j * ATT_BLK * dil + r
    return pl.ds(start, ATT_BLK, stride=dil) if dil > 1 else pl.ds(start, ATT_BLK)


class _Residues:
    def __init__(self, dil):
        self.dil = dil
        self.whole = dil % SUBLANES == 0
        self.read, self.written = {}, {}

    def _block(self, j):
        return pl.ds(j * ATT_BLK * self.dil, ATT_BLK * self.dil)

    def load(self, ref, j, r):
        if not self.whole:
            return ref[_sub_rows(j, r, self.dil), :]
        if (id(ref), j) not in self.read:
            rows = ref[self._block(j), :]
            self.read[id(ref), j] = jnp.swapaxes(rows.reshape(ATT_BLK, self.dil, rows.shape[-1]), 0, 1)
        return self.read[id(ref), j][r]

    def store(self, ref, j, r, val):
        if not self.whole:
            ref[_sub_rows(j, r, self.dil), :] = val
            return
        got = self.written.setdefault((id(ref), j), {})
        got[r] = val
        if len(got) == self.dil:
            merged = jnp.swapaxes(jnp.stack([got[k] for k in range(self.dil)], axis=0), 0, 1)
            ref[self._block(j), :] = merged.reshape(ATT_BLK * self.dil, val.shape[-1])
            del self.written[id(ref), j]


def _att_consts(hp, dil, keys_first=False):
    h0 = lax.broadcasted_iota(jnp.int32, (ATT_BLK, 128), 1) < HEAD_DIM
    a = lax.broadcasted_iota(jnp.int32, (ATT_BLK, ATT_BLK), 1 if keys_first else 0)
    c = lax.broadcasted_iota(jnp.int32, (ATT_BLK, ATT_BLK), 0 if keys_first else 1)
    dist_p = ((ATT_BLK + a - c) * dil).astype(F32)
    dist_c = ((a - c) * dil).astype(F32)
    bias_p, bias_c = [], []
    for h in range(2):
        slope = jnp.float32(_slope(2 * (HEAD_PAIRS - 1) + h))
        for p in range(HEAD_PAIRS - 2, -1, -1):
            slope = jnp.where(hp == p, jnp.float32(_slope(2 * p + h)), slope)
        bias_p.append(jnp.where(c >= a, -slope * dist_p, NEG_INF))
        bias_c.append(jnp.where(c <= a, -slope * dist_c, NEG_INF))
    return h0, jnp.concatenate(bias_p, axis=0), jnp.concatenate(bias_c, axis=0)


def _stack_heads(x, h0):
    return jnp.concatenate([jnp.where(h0, x, 0.0), jnp.where(h0, 0.0, x)], axis=0).astype(BF16)


def _unstack_heads(x, h0):
    return jnp.where(h0, x[:ATT_BLK], x[ATT_BLK:])


def _stack_cols(x):
    return jnp.concatenate(_head_cols(x), axis=0)


def _head_cols(x):
    return [x[:, 0:1], x[:, HEAD_DIM:HEAD_DIM + 1]]


def _in_groups(units, first_stage, *later_stages):
    for u0 in range(0, len(units), ATT_UNITS_TOGETHER):
        staged = [first_stage(*u) for u in units[u0:u0 + ATT_UNITS_TOGETHER]]
        for stage in later_stages:
            staged = [stage(*s) for s in staged]


def _attn_fwd(name, q, kv, g, bl, hub=None):
    T = q.shape[0]
    dil, _, nsub, _, nt = _att_geom(T, bl, g, ATT_TILE_ROWS_Q)
    tile, prev, _ = _att_specs(T, bl, g, ATT_TILE_ROWS_Q)
    halo = nt > 1

    def body(*refs):
        if halo:
            q_ref, kp_ref, kc_ref, vp_ref, vc_ref, o_ref, lse_ref = refs
        else:
            q_ref, kc_ref, vc_ref, o_ref, lse_ref = refs
        first = pl.program_id(1) == 0
        h0, bias_p, bias_c = _att_consts(pl.program_id(2), dil)
        bias_first = jnp.where(first, NEG_INF, bias_p)
        rows = _Residues(dil)

        def with_ones(v):
            return [jnp.where(h0, v, 1.0).astype(BF16), jnp.where(h0, 1.0, v).astype(BF16)]

        def scores(j, r):
            parts = [(rows.load(kc_ref, j, r), rows.load(vc_ref, j, r), bias_c)]
            if j > 0:
                parts.append((rows.load(kc_ref, j - 1, r), rows.load(vc_ref, j - 1, r), bias_p))
            elif halo:
                parts.append((rows.load(kp_ref, 0, r), rows.load(vp_ref, 0, r), bias_first))
            qs = _stack_heads(rows.load(q_ref, j, r) * ATT_SCALE, h0)
            return (j, r), [_dot_nt(qs, k.astype(BF16)) + b for k, _, b in parts], [with_ones(v) for _, v, _ in parts]

        def weights(unit, s, vals):
            mx = jnp.max(s[0] if len(s) == 1 else jnp.maximum(s[0], s[1]), axis=-1, keepdims=True)
            return unit, mx, [jnp.exp(x - mx).astype(BF16) for x in s], vals

        def outputs(unit, mx, e, vals):
            heads = [slice(h * ATT_BLK, (h + 1) * ATT_BLK) for h in range(2)]
            acc = []
            for h, hs in enumerate(heads):
                terms = [_dot(x[hs], v[h]) for x, v in zip(e, vals)]
                acc.append(terms[0] if len(terms) == 1 else terms[0] + terms[1])
            den = [pltpu.roll(a, HEAD_DIM, 1) for a in acc]
            rows.store(o_ref, *unit, jnp.where(h0, acc[0] / den[0], acc[1] / den[1]))
            rows.store(lse_ref, *unit, jnp.where(h0, mx[heads[0]] + jnp.log(den[0]), mx[heads[1]] + jnp.log(den[1])))
            return ()

        _in_groups([(j, r) for j in range(nsub) for r in range(dil)], scores, weights, outputs)

    kcol, vcol = (lambda hp: 8 * g + hp), (lambda hp: 8 * g + 4 + hp)
    ins = [(q, tile(lambda hp: 4 * g + hp))] + ([(kv, prev(kcol))] if halo else []) + [(kv, tile(kcol))]
    ins += ([(kv, prev(vcol))] if halo else []) + [(kv, tile(vcol))]
    out = (SDS((T, QW), F32), tile(lambda hp: hp))
    return _call(name, body, (bl, nt, HEAD_PAIRS), ins, [out, out], hub=hub)


def _attn_bwd_dq(name, q, kv, do, delta, lse, g, bl, hub=None):
    T = q.shape[0]
    dil, _, nsub, _, nt = _att_geom(T, bl, g, ATT_TILE_ROWS_Q)
    tile, prev, _ = _att_specs(T, bl, g, ATT_TILE_ROWS_Q)
    halo = nt > 1

    def body(*refs):
        if halo:
            q_ref, kp_ref, kc_ref, vp_ref, vc_ref, do_ref, dl_ref, lse_ref, dq_ref = refs
        else:
            q_ref, kc_ref, vc_ref, do_ref, dl_ref, lse_ref, dq_ref = refs
        first = pl.program_id(1) == 0
        h0, bias_p, bias_c = _att_consts(pl.program_id(2), dil)
        bias_first = jnp.where(first, NEG_INF, bias_p)
        rows = _Residues(dil)

        def probs(j, r):
            parts = [(rows.load(kc_ref, j, r), rows.load(vc_ref, j, r), bias_c)]
            if j > 0:
                parts.append((rows.load(kc_ref, j - 1, r), rows.load(vc_ref, j - 1, r), bias_p))
            elif halo:
                parts.append((rows.load(kp_ref, 0, r), rows.load(vp_ref, 0, r), bias_first))
            qs = _stack_heads(rows.load(q_ref, j, r) * ATT_SCALE, h0)
            dos = _stack_heads(rows.load(do_ref, j, r), h0)
            lse = _stack_cols(rows.load(lse_ref, j, r))
            keys = [k.astype(BF16) for k, _, _ in parts]
            p = [jnp.exp(_dot_nt(qs, k) + b - lse) for k, (_, _, b) in zip(keys, parts)]
            dp = [_dot_nt(dos, v.astype(BF16)) for _, v, _ in parts]
            return (j, r), p, dp, keys

        def dscores(unit, p, dp, keys):
            dl = _stack_cols(rows.load(dl_ref, *unit))
            return unit, [(x * (y - dl)).astype(BF16) for x, y in zip(p, dp)], keys

        def outputs(unit, ds, keys):
            terms = [_dot(x, k) for x, k in zip(ds, keys)]
            dq = terms[0] if len(terms) == 1 else terms[0] + terms[1]
            rows.store(dq_ref, *unit, _unstack_heads(dq, h0) * ATT_SCALE)
            return ()

        _in_groups([(j, r) for j in range(nsub) for r in range(dil)], probs, dscores, outputs)

    own = lambda hp: hp
    kcol, vcol = (lambda hp: 8 * g + hp), (lambda hp: 8 * g + 4 + hp)
    ins = [(q, tile(lambda hp: 4 * g + hp))] + ([(kv, prev(kcol))] if halo else []) + [(kv, tile(kcol))]
    ins += ([(kv, prev(vcol))] if halo else []) + [(kv, tile(vcol))]
    ins += [(do, tile(own)), (delta, tile(own)), (lse, tile(own))]
    return _call(name, body, (bl, nt, HEAD_PAIRS), ins, [(SDS((T, QW), F32), tile(own))], hub=hub)[0]


def _attn_bwd_dkv(name, q, kv, do, delta, lse, g, bl, prev=None, hub=None):
    T = q.shape[0]
    dil, _, nsub, _, nt = _att_geom(T, bl, g, ATT_TILE_ROWS_Q)
    tile, _, nxt = _att_specs(T, bl, g, ATT_TILE_ROWS_Q)
    has_prev = prev is not None
    halo = nt > 1

    def body(*refs):
        if halo:
            k_ref, v_ref, q_ref, qn_ref, do_ref, don_ref, dl_ref, dln_ref, l_ref, ln_ref = refs[:10]
            rest = refs[10:]
        else:
            k_ref, v_ref, q_ref, do_ref, dl_ref, l_ref = refs[:6]
            rest = refs[6:]
        if has_prev:
            dkp_ref, dvp_ref, dk_ref, dv_ref = rest
        else:
            dk_ref, dv_ref = rest
        last = pl.program_id(1) == nt - 1
        h0, bias_p, bias_c = _att_consts(pl.program_id(2), dil, keys_first=True)
        bias_last = jnp.where(last, NEG_INF, bias_p)
        rows = _Residues(dil)

        def per_query_rows(x):
            xt = x.T
            return jnp.concatenate([jnp.broadcast_to(xt[0:1], (ATT_BLK, ATT_BLK)),
                                    jnp.broadcast_to(xt[HEAD_DIM:HEAD_DIM + 1], (ATT_BLK, ATT_BLK))], axis=0)

        def probs(j, r):
            ks = _stack_heads(rows.load(k_ref, j, r), h0)
            vs = _stack_heads(rows.load(v_ref, j, r), h0)
            sets = [(q_ref, do_ref, dl_ref, l_ref, j, bias_c)]
            if j < nsub - 1:
                sets.append((q_ref, do_ref, dl_ref, l_ref, j + 1, bias_p))
            elif halo:
                sets.append((qn_ref, don_ref, dln_ref, ln_ref, 0, bias_last))
            out = []
            for qr, dor, dlr, lr, jq, bias in sets:
                qsb = (rows.load(qr, jq, r) * ATT_SCALE).astype(BF16)
                do2b = rows.load(dor, jq, r).astype(BF16)
                p = jnp.exp(_dot_nt(ks, qsb) + bias - per_query_rows(rows.load(lr, jq, r)))
                out.append((p, _dot_nt(vs, do2b), dlr, jq, qsb, do2b))
            return (j, r), out

        def dscores(unit, sets):
            out = []
            for p, dp, dlr, jq, qsb, do2b in sets:
                ds = (p * (dp - per_query_rows(rows.load(dlr, jq, unit[1])))).astype(BF16)
                out.append((p.astype(BF16), ds, qsb, do2b))
            return unit, out

        def outputs(unit, sets):
            dk_st = dv_st = None
            for pb, ds, qsb, do2b in sets:
                dvs, dks = _dot(pb, do2b), _dot(ds, qsb)
                dv_st = dvs if dv_st is None else dv_st + dvs
                dk_st = dks if dk_st is None else dk_st + dks
            dk2 = _unstack_heads(dk_st, h0)
            dv2 = _unstack_heads(dv_st, h0)
            if has_prev:
                dk2 = dk2 + rows.load(dkp_ref, *unit)
                dv2 = dv2 + rows.load(dvp_ref, *unit)
            rows.store(dk_ref, *unit, dk2)
            rows.store(dv_ref, *unit, dv2)
            return ()

        _in_groups([(j, r) for j in range(nsub) for r in range(dil)], probs, dscores, outputs)

    own = lambda hp: hp
    qcol = lambda hp: 4 * g + hp
    ins = [(kv, tile(lambda hp: 8 * g + hp)), (kv, tile(lambda hp: 8 * g + 4 + hp))]
    for t, col in ((q, qcol), (do, own), (delta, own), (lse, own)):
        ins += [(t, tile(col))] + ([(t, nxt(col))] if halo else [])
    if has_prev:
        ins += [(prev[0], tile(own)), (prev[1], tile(own))]
    out = (SDS((T, QW), F32), tile(own))
    return _call(name, body, (bl, nt, HEAD_PAIRS), ins, [out, out], hub=hub)


def _final_loss(name, h, tgt, g, tm=MM_ROWS):
    T, D = h.shape
    n_steps = T // tm

    def body(h_ref, t_ref, g_ref, dh_ref, loss_ref, dg_ref, sq_ref):
        i = pl.program_id(0)
        x = h_ref[...]
        rstd = _rstd(x)
        xhat = x * rstd
        err = xhat * g_ref[...] - t_ref[...]
        _acc8(sq_ref, err * err, i, n_steps)
        dy = err * (1.0 / D)
        dxhat = dy * g_ref[...]
        dh_ref[...] = rstd * (dxhat - xhat * jnp.mean(dxhat * xhat, axis=-1, keepdims=True))
        _acc8(dg_ref, dy * xhat, i, n_steps)

        @pl.when(i == n_steps - 1)
        def _():
            loss_ref[...] = jnp.full(loss_ref.shape, jnp.sum(sq_ref[0:1, :]), F32)

    outs = [_out_rows(T, D, F32, tm), (SDS((SUBLANES, 128), F32), pl.BlockSpec((SUBLANES, 128), lambda i: (0, 0))),
            _out_acc8(D)]
    dh, loss, dg = _call(name, body, (n_steps,), [_rows(h, tm), _rows(tgt, tm), _full(g)], outs,
                         scratch=[pltpu.VMEM((SUBLANES, D), F32)])
    return dh, loss[0, 0], dg[0:1]


def _conv_bwd(name, bcu, dgated, cw, seq, tm=MM_ROWS, hub=None):
    T, D = dgated.shape
    n_steps = T // tm

    def body(b_ref, c_ref, u_ref, ch_ref, uh_ref, dg_ref, dgn_ref, bn_ref, cw_ref, o_ref, t0_ref, t1_ref, t2_ref):
        i = pl.program_id(0)
        first = (i * tm) % seq == 0
        last = ((i + 1) * tm) % seq == 0
        _, (b, c, u), conv, (cu, cu1, cu2) = _gate(b_ref, c_ref, u_ref, ch_ref, uh_ref, cw_ref, first)
        dgat = dg_ref[...]
        dconv = dgat * b
        nxt = jnp.where(last, 0.0, dgn_ref[...] * bn_ref[...].astype(F32))
        rows = lax.broadcasted_iota(jnp.int32, dconv.shape, 0)
        n1 = nxt[0:1, :]
        n2 = nxt[1:2, :]
        dc1 = jnp.where(rows == tm - 1, n1, pltpu.roll(dconv, tm - 1, 0))
        dc2 = jnp.where(rows == tm - 1, n2, jnp.where(rows == tm - 2, n1, pltpu.roll(dconv, tm - 2, 0)))
        dcu = cw_ref[0:1, :] * dconv + cw_ref[1:2, :] * dc1 + cw_ref[2:3, :] * dc2
        o_ref[:, 0:D] = (dgat * conv).astype(BF16)
        o_ref[:, D:2 * D] = (dcu * u).astype(BF16)
        o_ref[:, 2 * D:3 * D] = (dcu * c).astype(BF16)
        _acc8(t0_ref, dconv * cu, i, n_steps)
        _acc8(t1_ref, dconv * cu1, i, n_steps)
        _acc8(t2_ref, dconv * cu2, i, n_steps)

    ins = [_rows(bcu, tm, D, 0), _rows(bcu, tm, D, 1), _rows(bcu, tm, D, 2), _prev8(bcu, tm, D, 1), _prev8(bcu, tm, D, 2),
           _rows(dgated, tm), _next8(dgated, tm, D, 0), _next8(bcu, tm, D, 0), _full(cw)]
    outs = [_out_rows(T, 3 * D, BF16, tm), _out_acc8(D), _out_acc8(D), _out_acc8(D)]
    dbcu, t0, t1, t2 = _call(name, body, (n_steps,), ins, outs, hub=hub)
    return dbcu, jnp.concatenate([t0[0:1], t1[0:1], t2[0:1]], axis=0)


def _sum8_adamw(name, parts, w, m, v, tr):
    R, C = w.shape
    b1c = 1.0 - ADAM_B1 ** ADAM_STEP
    b2c = 1.0 - ADAM_B2 ** ADAM_STEP

    def body(p_ref, w_ref, m_ref, v_ref, g_ref, d_ref, nm_ref, nv_ref):
        g = p_ref[0].astype(F32)
        for j in range(1, N_DEV):
            g = g + p_ref[j].astype(F32)
        nm = ADAM_B1 * m_ref[...] + (1.0 - ADAM_B1) * g
        nv = ADAM_B2 * v_ref[...] + (1.0 - ADAM_B2) * (g * g)
        m_hat = nm / b1c
        v_hat = nv / b2c
        g_ref[...] = g
        d_ref[...] = -ADAM_LR * (m_hat / (jnp.sqrt(v_hat) + ADAM_EPS) + ADAM_WD * w_ref[...])
        nm_ref[...] = nm
        nv_ref[...] = nv

    ins = [(parts, pl.BlockSpec((N_DEV, tr, C), lambda i: (0, i, 0))), _rows(w, tr), _rows(m, tr), _rows(v, tr)]
    outs = [_out_rows(R, C, F32, tr)] * 4
    return _call(name, body, (R // tr,), ins, outs)


def _all_gather(name, items):
    n = len(items)
    shapes = [tuple(a.shape if idx is None else a.shape[1:]) for a, idx in items]

    def body(*refs):
        x_refs, out_refs = refs[:n], refs[n:2 * n]
        send_sems, recv_sems, local_sems = refs[2 * n:]
        x, y, c = _mesh_pos()
        me, sibling = (x, y, c), (x, y, 1 - c)
        chips = [(1 - x, y), (x, 1 - y), (1 - x, 1 - y)]

        def copy(t, k, block, to, own=False):
            dst = out_refs[t].at[4 * block[0] + 2 * block[1] + block[2]]
            src = dst
            if own:
                src = x_refs[t] if items[t][1] is None else x_refs[t].at[items[t][1]]
            return pltpu.make_async_remote_copy(
                src_ref=src, dst_ref=dst, send_sem=send_sems.at[t, k], recv_sem=recv_sems.at[t, k],
                device_id=to, device_id_type=pl.DeviceIdType.MESH)

        started = []
        for t in range(n):
            src = x_refs[t] if items[t][1] is None else x_refs[t].at[items[t][1]]
            mine = pltpu.make_async_copy(src, out_refs[t].at[4 * x + 2 * y + c], local_sems.at[t])
            mine.start()
            first = [copy(t, 0, me, sibling, own=True)]
            first += [copy(t, 1 + j, me, (*chip, c), own=True) for j, chip in enumerate(chips)]
            for cp in first:
                cp.start()
            started.append((mine, first))
        passed = []
        for t in range(n):
            for j, chip in enumerate(chips):
                copy(t, 1 + j, (*chip, c), me).wait_recv()
                fwd = copy(t, 4 + j, (*chip, c), sibling)
                fwd.start()
                passed.append(fwd)
        for t in range(n):
            copy(t, 0, sibling, me).wait_recv()
            for j, chip in enumerate(chips):
                copy(t, 4 + j, (*chip, 1 - c), me).wait_recv()
        for mine, first in started:
            for cp in first:
                cp.wait_send()
            mine.wait()
        for cp in passed:
            cp.wait_send()

    any_spec = pl.BlockSpec(memory_space=pl.ANY)
    return pl.pallas_call(
        body, name=name,
        out_shape=[SDS((N_DEV,) + s, a.dtype) for s, (a, _) in zip(shapes, items)],
        in_specs=[any_spec] * n,
        out_specs=[any_spec] * n,
        scratch_shapes=[pltpu.SemaphoreType.DMA((n, 7)), pltpu.SemaphoreType.DMA((n, 7)), pltpu.SemaphoreType.DMA((n,))],
    )(*[a for a, _ in items])


def _pad8(t):
    return jnp.pad(t, ((0, SUBLANES - t.shape[0]), (0, 0)))


def _rows_merged(w):
    return w.reshape(w.shape[0] * w.shape[1], w.shape[2])


def _local_grads(x, tgt, norm_mix, norm_mlp, norm_kv, norm_final, conv_w, hub):
    bl, seq, D = x.shape
    T = bl * seq
    h = x.reshape(T, D)
    tgt = tgt.reshape(T, D)
    row = lambda t, l: t[l:l + 1]
    W = hub.weights
    saved = []
    kv = h_kv = hn_kv_t = None
    for l in range(DEPTH):
        if l < N_A_LAYERS:
            bcu, hn_t = _norm_mm(f"l{l}_in", h, row(norm_mix, l), W["w_a_in", l], out_dtype=BF16, transposed=True, hub=hub)
            h2, gated_t = _gate_mm_res(f"l{l}_conv_out", bcu, _pad8(conv_w[l]), _rows_merged(W["w_a_out", l]), h, seq, hub=hub)
            saved.append((h, bcu, gated_t, hn_t))
        else:
            i = l - N_A_LAYERS
            if l == N_A_LAYERS:
                h_kv = h
                kv, hn_kv_t = _norm_mm("kv", h, norm_kv.reshape(1, D), W["w_kv", None], transposed=True, hub=hub)
            q, hn_t = _norm_mm(f"l{l}_q", h, row(norm_mix, l), W["w_q", i], transposed=True)
            per_group = [_attn_fwd(f"l{l}_att{g}", q, kv, g, bl, hub=hub) for g in range(N_GROUPS)]
            o, lse, h2 = _combine_mm_res(f"l{l}_att_out", [p[0] for p in per_group], [p[1] for p in per_group],
                                         W["w_o", i], h)
            saved.append((h, q, o, lse, hn_t))
        a = _norm_mm(f"l{l}_up", h2, row(norm_mlp, l), W["w_up", l], out_dtype=BF16, hub=hub)
        h = _relu2_mm_res(f"l{l}_down", a, _rows_merged(W["w_down", l]), h2, hub=hub)
        saved[-1] = saved[-1] + (h2, a)

    dh, sq_err, d_norm_final = _final_loss("loss", h, tgt, norm_final.reshape(1, D))

    d_norm_mix = [None] * DEPTH
    d_norm_mlp = [None] * DEPTH
    d_conv = [None] * N_A_LAYERS
    d_norm_kv = None
    dkv_acc = [None] * N_GROUPS
    G = hub.grads
    as_slots = lambda g: g.reshape(N_DEV, g.shape[0] // N_DEV, g.shape[1])
    tt = DW_TOKENS
    for l in reversed(range(DEPTH)):
        h2, a = saved[l][-2:]
        h_in = saved[l][0]
        g_mlp = row(norm_mlp, l)
        g_mix = row(norm_mix, l)
        w_up_l = W["w_up", l]
        FF = N_DEV * w_up_l.shape[2]
        da = _nt_relu2_bwd(f"l{l}_down_bwd", dh, _rows_merged(W["w_down", l]), a, hub=hub)
        G["w_down", l] = as_slots(_tn(f"l{l}_dw_down", [_rows2(a, 2 * tt, FF // 4, lambda s: s)], _relu2,
                                      [_rows2(dh, 2 * tt)], _val, FF, D, T, 2 * tt, split=("k", 4), hub=hub))
        G["w_up", l] = _tn(f"l{l}_dw_up", [_rows2(h2, 2 * tt), _full2(g_mlp)], _normed,
                           [_rows2(da, 2 * tt, FF // 4, lambda s: s)], _val, D, FF, T, 2 * tt, split=("n", 4),
                           out_cols=w_up_l.shape[2], hub=hub)
        dh2, d_norm_mlp[l] = _nt_norm_bwd(f"l{l}_up_bwd", [da], w_up_l, h2, g_mlp, dh, hub=hub)
        if l >= N_A_LAYERS:
            i = l - N_A_LAYERS
            _, q, o, lse, hn_t = saved[l][:5]
            w_o_i, w_q_i = W["w_o", i], W["w_q", i]
            do, delta = _att_out_bwd(f"l{l}_att_out_bwd", dh2, w_o_i, o)
            G["w_o", i] = _tn(f"l{l}_dw_o", [_rows2(o, tt)], _val, [_rows2(dh2, tt)], _val, QW, D, T, tt,
                              out_cols=w_o_i.shape[2])
            dqs = []
            for g in range(N_GROUPS):
                dqs.append(_attn_bwd_dq(f"l{l}_att{g}_dq", q, kv, do, delta, lse, g, bl, hub=hub))
                dkv_acc[g] = _attn_bwd_dkv(f"l{l}_att{g}_dkv", q, kv, do, delta, lse, g, bl, prev=dkv_acc[g], hub=hub)
            G["w_q", i] = _tn(f"l{l}_dw_q", [_cols2(hn_t, tt)], None,
                              [_rows2(t, tt) for t in dqs], _concat_f32, D, N_GROUPS * QW, T, tt, out_cols=w_q_i.shape[2])
            dh, d_norm_mix[l] = _nt_norm_bwd(f"l{l}_q_bwd", dqs, w_q_i, h_in, g_mix, dh2, hub=hub)
            if l == N_A_LAYERS:
                dkvs = [t for pair in dkv_acc for t in pair]
                g_kv = norm_kv.reshape(1, D)
                w_kv = W["w_kv", None]
                per_call = len(dkvs) // 2
                halves = [_tn(f"dw_kv{p}", [_cols2(hn_kv_t, tt)], None,
                              [_rows2(t, tt) for t in dkvs[p * per_call:(p + 1) * per_call]], _concat_f32,
                              D, per_call * QW, T, tt, out_cols=w_kv.shape[2]) for p in range(2)]
                G["w_kv", None] = jnp.concatenate(halves, axis=0)
                dh, d_norm_kv = _nt_norm_bwd("kv_bwd", dkvs, w_kv, h_kv, g_kv, dh, tm=MM_ROWS // 2, hub=hub)
        else:
            _, bcu, gated_t, hn_t = saved[l][:4]
            cw = _pad8(conv_w[l])
            w_in_l = W["w_a_in", l]
            dgated = _nt_plain(f"l{l}_conv_out_bwd", dh2, _rows_merged(W["w_a_out", l]))
            G["w_a_out", l] = as_slots(_tn(f"l{l}_dw_conv_out", [_cols2(gated_t, 2 * tt)], None,
                                           [_rows2(dh2, 2 * tt)], _val, D, D, T, 2 * tt, hub=hub))
            dbcu, d_conv[l] = _conv_bwd(f"l{l}_conv_bwd", bcu, dgated, cw, seq, hub=hub)
            G["w_a_in", l] = _tn(f"l{l}_dw_in", [_cols2(hn_t, 2 * tt)], None,
                                 [_rows2(dbcu, 2 * tt, 3 * D // 2, lambda s: s)], _val, D, 3 * D, T, 2 * tt, split=("n", 2),
                                 out_cols=w_in_l.shape[2], hub=hub)
            dh, d_norm_mix[l] = _nt_norm_bwd(f"l{l}_in_bwd", [dbcu], w_in_l, h_in, g_mix, dh2, hub=hub)

    small = jnp.concatenate(d_norm_mix + d_norm_mlp + [d_norm_kv, d_norm_final] + d_conv, axis=0)
    return sq_err, dh.reshape(bl, seq, D), small


def kernel(x, norm_mix, norm_mlp, w_a_in, conv_w, w_a_out, norm_kv, w_kv, w_q, w_o, w_up, w_down, norm_final, loss_target, m_norm_mix, m_norm_mlp, m_w_a_in, m_conv_w, m_w_a_out, m_norm_kv, m_w_kv, m_w_q, m_w_o, m_w_up, m_w_down, m_norm_final, v_norm_mix, v_norm_mlp, v_w_a_in, v_conv_w, v_w_a_out, v_norm_kv, v_w_kv, v_w_q, v_w_o, v_w_up, v_w_down, v_norm_final):
    D = x.shape[-1]
    xi, yi, ci = _mesh_pos()
    me_idx = 4 * xi + 2 * yi + ci
    w_big = dict(w_a_in=w_a_in, w_a_out=w_a_out, w_kv=w_kv, w_q=w_q, w_o=w_o, w_up=w_up, w_down=w_down)
    m_big = dict(w_a_in=m_w_a_in, w_a_out=m_w_a_out, w_kv=m_w_kv, w_q=m_w_q, w_o=m_w_o, w_up=m_w_up, w_down=m_w_down)
    v_big = dict(w_a_in=v_w_a_in, w_a_out=v_w_a_out, w_kv=v_w_kv, w_q=v_w_q, w_o=v_w_o, w_up=v_w_up, w_down=v_w_down)
    names = list(w_big)

    shards = {n: w.astype(BF16) for n, w in w_big.items()}
    landing = {n: lax.empty((N_DEV,) + w.shape, BF16) for n, w in w_big.items()}
    hub = _Hub(FETCH_DURING, PUSH_DURING, shards, landing)
    dc = conv_w.shape[-1]
    taps = conv_w.shape[0] * conv_w.shape[1]
    got = _all_gather("gather_first", [(shards[n], l) for n, l in FETCH_UP_FRONT] + [(_pad8(conv_w.reshape(taps, dc)), None)])
    for key, w in zip(FETCH_UP_FRONT, got):
        hub.weights[key] = w
    conv_full = jnp.moveaxis(got[-1][:, :taps], 0, 1).reshape(conv_w.shape[0], conv_w.shape[1], N_DEV * dc)

    sq_err, grad_x, small = _local_grads(x, loss_target, norm_mix, norm_mlp, norm_kv, norm_final, conv_full, hub)

    grads, deltas, new_m, new_v = {}, {}, {}, {}
    for n in names:
        shape = w_big[n].shape
        cols = shape[-1]
        flat = lambda t: t.reshape(-1, cols)
        parts = hub.landing[n].reshape(N_DEV, -1, cols)
        tr = parts.shape[1]
        while tr * cols > ADAMW_TILE and tr % 32 == 0:
            tr //= 2
        outs = _sum8_adamw(f"adamw_{n}", parts, flat(w_big[n]), flat(m_big[n]), flat(v_big[n]), tr=tr)
        grads[n], deltas[n], new_m[n], new_v[n] = (t.reshape(shape) for t in outs)

    n_gain = 2 * DEPTH + 2
    n_small = small.shape[0]
    small = jnp.concatenate([small, jnp.full((SUBLANES, D), sq_err, F32)], axis=0)
    rows_small = small.shape[0]
    small_all = _all_gather("gather_small_grads", [(small, None)])[0]

    def small_pack(nm, nl, nk, nf, cw):
        gains = jnp.concatenate([nm, nl, nk.reshape(1, D), nf.reshape(1, D)], axis=0)
        taps_full = lax.dynamic_update_slice(jnp.zeros((taps, D), F32), cw.reshape(taps, dc), (0, me_idx * dc))
        return jnp.concatenate([gains, taps_full, jnp.zeros((SUBLANES, D), F32)], axis=0)

    sp = [small_pack(*t) for t in ((norm_mix, norm_mlp, norm_kv, norm_final, conv_w),
                                   (m_norm_mix, m_norm_mlp, m_norm_kv, m_norm_final, m_conv_w),
                                   (v_norm_mix, v_norm_mlp, v_norm_kv, v_norm_final, v_conv_w))]
    small_out = _sum8_adamw("adamw_small", small_all, *sp, tr=rows_small)
    loss = small_out[0][n_small, 0] * (0.5 / D)

    def small_unpack(t):
        res = dict(norm_mix=t[0:DEPTH], norm_mlp=t[DEPTH:2 * DEPTH], norm_kv=t[2 * DEPTH], norm_final=t[2 * DEPTH + 1])
        res["conv_w"] = lax.dynamic_slice(t[n_gain:], (0, me_idx * dc), (taps, dc)).reshape(conv_w.shape)
        return res

    for dst, t in zip((grads, deltas, new_m, new_v), small_out):
        dst.update(small_unpack(t))

    order = ["norm_mix", "norm_mlp", "w_a_in", "conv_w", "w_a_out", "norm_kv", "w_kv", "w_q", "w_o", "w_up", "w_down",
             "norm_final"]
    return (loss, grad_x, *[grads[n] for n in order], *[deltas[n] for n in order], *[new_m[n] for n in order],
            *[new_v[n] for n in order])
```
